```python
import jax, jax.numpy as jnp
from jax import lax
import numpy as np

D_MODEL = 1024
BATCH = 8
SEQ = 8192
DEPTH = 2

D_FF = 2816
SSD_EXPAND = 2
SSD_D_INNER = SSD_EXPAND * D_MODEL
SSD_HEAD_DIM = 64
SSD_HEADS = SSD_D_INNER // SSD_HEAD_DIM
SSD_GROUPS = 4
SSD_HPG = SSD_HEADS // SSD_GROUPS
SSD_STATE = 128
SSD_CONV = 4
SSD_CHUNK = 128
SSD_CONV_DIM = SSD_D_INNER + 2 * SSD_GROUPS * SSD_STATE
MLA_HEADS = 8
MLA_Q_LORA = 512
MLA_KV_LORA = 256
MLA_NOPE = 128
MLA_ROPE = 64
MLA_V = 128
MLA_QK = MLA_NOPE + MLA_ROPE
ATTN_BLOCK = 128
ROPE_THETA = 10000.0
EPS = 1e-6
IN_SPLIT_SIZES = (SSD_D_INNER, SSD_CONV_DIM, SSD_HEADS, MLA_Q_LORA, MLA_KV_LORA, MLA_ROPE, 2 * D_MODEL)
D_IN_PROJ = SSD_D_INNER + SSD_CONV_DIM + SSD_HEADS + MLA_Q_LORA + MLA_KV_LORA + MLA_ROPE + 2 * D_MODEL

kernel_name = "macaron_gated_ssd_mla_hybrid"


def rms_norm(x, g):
    xf = x.astype(jnp.float32)
    y = xf * lax.rsqrt(jnp.mean(xf * xf, axis=-1, keepdims=True) + EPS)
    return (y * g.astype(jnp.float32)).astype(x.dtype)


def swiglu(x, w13, w2):
    gu = x @ w13
    gate, up = gu[..., :D_FF], gu[..., D_FF:]
    return (jax.nn.silu(gate) * up) @ w2


def split_cols(t, sizes):
    out, start = [], 0
    for n in sizes:
        out.append(t[..., start:start + n])
        start += n
    return out


def rope_tables(positions):
    inv = 1.0 / (ROPE_THETA ** (jnp.arange(0, MLA_ROPE, 2, dtype=jnp.float32) / MLA_ROPE))
    ang = positions.astype(jnp.float32)[..., None] * inv
    return jnp.cos(ang), jnp.sin(ang)


def apply_rope(t, cos, sin):
    half = MLA_ROPE // 2
    tf = t.astype(jnp.float32)
    t1, t2 = tf[..., :half], tf[..., half:]
    c, s = cos[:, :, None, :], sin[:, :, None, :]
    return jnp.concatenate([t1 * c - t2 * s, t2 * c + t1 * s], axis=-1).astype(t.dtype)


def causal_depthwise_conv(t, w, b):
    y = lax.conv_general_dilated(
        t, w[:, None, :].astype(t.dtype), window_strides=(1,), padding=[(SSD_CONV - 1, 0)],
        dimension_numbers=('NWC', 'WIO', 'NWC'), feature_group_count=t.shape[-1])
    return y + b.astype(t.dtype)


def ssd_chunked_scan(xh, dt, a, bmat, cmat):
    bsz, s = xh.shape[0], xh.shape[1]
    nc = s // SSD_CHUNK

    def chunks(t):
        t = t.reshape((bsz, nc, SSD_CHUNK) + t.shape[2:])
        return jnp.moveaxis(t, 1, 0)

    x_c = chunks(xh.astype(jnp.float32).reshape(bsz, s, SSD_GROUPS, SSD_HPG, SSD_HEAD_DIM))
    dt_c = chunks(dt.reshape(bsz, s, SSD_GROUPS, SSD_HPG))
    b_c = chunks(bmat.astype(jnp.float32))
    c_c = chunks(cmat.astype(jnp.float32))
    a_g = a.reshape(SSD_GROUPS, SSD_HPG)
    causal = jnp.tril(jnp.ones((SSD_CHUNK, SSD_CHUNK), dtype=bool))[None, :, :, None, None]

    def step(state, inp):
        xc, dtc, bc, cc = inp
        acum = jnp.cumsum(dtc * a_g, axis=1)
        seg = acum[:, :, None] - acum[:, None, :]
        decay = jnp.exp(jnp.where(causal, seg, -jnp.inf))
        cb = jnp.einsum('btgn,bsgn->btsg', cc, bc)
        xdt = xc * dtc[..., None]
        y_diag = jnp.einsum('btsg,btsgh,bsghp->btghp', cb, decay, xdt)
        y_off = jnp.einsum('btgn,bghpn->btghp', cc, state) * jnp.exp(acum)[..., None]
        last = acum[:, -1]
        w_state = jnp.exp(last[:, None] - acum)
        new_state = (state * jnp.exp(last)[..., None, None]
                     + jnp.einsum('bsgn,bsgh,bsghp->bghpn', bc, w_state, xdt))
        return new_state, y_diag + y_off

    state0 = jnp.zeros((bsz, SSD_GROUPS, SSD_HPG, SSD_HEAD_DIM, SSD_STATE), jnp.float32)
    _, y = lax.scan(step, state0, (x_c, dt_c, b_c, c_c))
    return jnp.moveaxis(y, 0, 1).reshape(bsz, s, SSD_HEADS, SSD_HEAD_DIM)


def ssd_branch(z, xbc, dt_raw, conv_w, conv_b, dt_bias, a_log, d_skip, norm_g, w_out):
    bsz, s, _ = z.shape
    xbc = jax.nn.silu(causal_depthwise_conv(xbc, conv_w, conv_b))
    xs, bm, cm = split_cols(xbc, (SSD_D_INNER, SSD_GROUPS * SSD_STATE, SSD_GROUPS * SSD_STATE))
    xh = xs.reshape(bsz, s, SSD_HEADS, SSD_HEAD_DIM)
    bm = bm.reshape(bsz, s, SSD_GROUPS, SSD_STATE)
    cm = cm.reshape(bsz, s, SSD_GROUPS, SSD_STATE)
    dt = jax.nn.softplus(dt_raw.astype(jnp.float32) + dt_bias.astype(jnp.float32))
    a = -jnp.exp(a_log.astype(jnp.float32))
    y = ssd_chunked_scan(xh, dt, a, bm, cm) + xh.astype(jnp.float32) * d_skip.astype(jnp.float32)[:, None]
    y = y.reshape(bsz, s, SSD_D_INNER).astype(z.dtype)
    y = rms_norm(y * jax.nn.silu(z), norm_g)
    return y @ w_out


def causal_block_attention(q, k, v):
    bsz, s, h, dq = q.shape
    nb = s // ATTN_BLOCK
    scale = dq ** -0.5
    qb = jnp.moveaxis(q.reshape(bsz, nb, ATTN_BLOCK, h, dq), 1, 0)
    kpos = jnp.arange(s)

    def one_block(args):
        i, qi = args
        sc = jnp.einsum('bqhd,bkhd->bhqk', qi, k, preferred_element_type=jnp.float32) * scale
        qpos = i * ATTN_BLOCK + jnp.arange(ATTN_BLOCK)
        sc = jnp.where(kpos[None, :] <= qpos[:, None], sc, -jnp.inf)
        p = jax.nn.softmax(sc, axis=-1)
        return jnp.einsum('bhqk,bkhd->bqhd', p.astype(v.dtype), v)

    o = lax.map(one_block, (jnp.arange(nb), qb))
    return jnp.moveaxis(o, 0, 1).reshape(bsz, s, h, v.shape[-1])


def mla_branch(cq, ckv, kr, cos, sin, q_lora_g, w_uq, kv_lora_g, w_ukv, q_norm_g, k_norm_g, w_out):
    bsz, s, _ = cq.shape
    q = (rms_norm(cq, q_lora_g) @ w_uq).reshape(bsz, s, MLA_HEADS, MLA_QK)
    kv = (rms_norm(ckv, kv_lora_g) @ w_ukv).reshape(bsz, s, MLA_HEADS, MLA_NOPE + MLA_V)
    k_nope, v = kv[..., :MLA_NOPE], kv[..., MLA_NOPE:]
    k_pe = jnp.broadcast_to(kr[:, :, None, :], (bsz, s, MLA_HEADS, MLA_ROPE))
    k = jnp.concatenate([k_nope, k_pe], axis=-1)
    q = rms_norm(q, q_norm_g)
    k = rms_norm(k, k_norm_g)
    q = jnp.concatenate([q[..., :MLA_NOPE], apply_rope(q[..., MLA_NOPE:], cos, sin)], axis=-1)
    k = jnp.concatenate([k[..., :MLA_NOPE], apply_rope(k[..., MLA_NOPE:], cos, sin)], axis=-1)
    o = causal_block_attention(q, k, v)
    return o.reshape(bsz, s, MLA_HEADS * MLA_V) @ w_out


def _fwd_setup_inputs(seed: int = 0) -> dict:
    key = jax.random.key(seed)
    ks = jax.random.split(key, 32)
    f32 = jnp.float32

    def nrm(k, shape, fan_in):
        return jax.random.normal(k, shape, f32) * (fan_in ** -0.5)

    def gain(k, n):
        return 1.0 + 0.02 * jax.random.normal(k, (DEPTH, n), f32)

    dt0 = jnp.exp(jax.random.uniform(ks[10], (DEPTH, SSD_HEADS), f32) * (np.log(0.1) - np.log(0.001)) + np.log(0.001))
    dt_bias = dt0 + jnp.log(-jnp.expm1(-dt0))
    a_log = jnp.log(jax.random.uniform(ks[11], (DEPTH, SSD_HEADS), f32, 1.0, 16.0))
    return {
        "x": jax.random.normal(ks[0], (BATCH, SEQ, D_MODEL), f32),
        "positions": jnp.broadcast_to(jnp.arange(SEQ, dtype=jnp.int32)[None, :], (BATCH, SEQ)),
        "ln_ffn1": gain(ks[1], D_MODEL),
        "ffn1_w13": nrm(ks[2], (DEPTH, D_MODEL, 2 * D_FF), D_MODEL),
        "ffn1_w2": nrm(ks[3], (DEPTH, D_FF, D_MODEL), D_FF),
        "ln_mix": gain(ks[4], D_MODEL),
        "w_in": nrm(ks[5], (DEPTH, D_MODEL, D_IN_PROJ), D_MODEL),
        "conv_w": nrm(ks[6], (DEPTH, SSD_CONV, SSD_CONV_DIM), SSD_CONV),
        "conv_b": 0.02 * jax.random.normal(ks[7], (DEPTH, SSD_CONV_DIM), f32),
        "dt_bias": dt_bias,
        "a_log": a_log,
        "d_skip": 1.0 + 0.1 * jax.random.normal(ks[12], (DEPTH, SSD_HEADS), f32),
        "ssd_norm": gain(ks[13], SSD_D_INNER),
        "w_ssd_out": nrm(ks[14], (DEPTH, SSD_D_INNER, D_MODEL), SSD_D_INNER),
        "q_lora_norm": gain(ks[15], MLA_Q_LORA),
        "w_uq": nrm(ks[16], (DEPTH, MLA_Q_LORA, MLA_HEADS * MLA_QK), MLA_Q_LORA),
        "kv_lora_norm": gain(ks[17], MLA_KV_LORA),
        "w_ukv": nrm(ks[18], (DEPTH, MLA_KV_LORA, MLA_HEADS * (MLA_NOPE + MLA_V)), MLA_KV_LORA),
        "q_norm": gain(ks[19], MLA_QK),
        "k_norm": gain(ks[20], MLA_QK),
        "w_mla_out": nrm(ks[21], (DEPTH, MLA_HEADS * MLA_V, D_MODEL), MLA_HEADS * MLA_V),
        "w_o": nrm(ks[22], (DEPTH, D_MODEL, D_MODEL), D_MODEL),
        "ln_ffn2": gain(ks[23], D_MODEL),
        "ffn2_w13": nrm(ks[24], (DEPTH, D_MODEL, 2 * D_FF), D_MODEL),
        "ffn2_w2": nrm(ks[25], (DEPTH, D_FF, D_MODEL), D_FF),
    }


def _fwd_reference(x, positions, ln_ffn1, ffn1_w13, ffn1_w2, ln_mix, w_in, conv_w, conv_b, dt_bias,
              a_log, d_skip, ssd_norm, w_ssd_out, q_lora_norm, w_uq, kv_lora_norm, w_ukv,
              q_norm, k_norm, w_mla_out, w_o, ln_ffn2, ffn2_w13, ffn2_w2):
    cos, sin = rope_tables(positions)
    h = x
    for l in range(DEPTH):
        h = h + 0.5 * swiglu(rms_norm(h, ln_ffn1[l]), ffn1_w13[l], ffn1_w2[l])
        u = rms_norm(h, ln_mix[l])
        z, xbc, dt_raw, cq, ckv, kr, gates = split_cols(u @ w_in[l], IN_SPLIT_SIZES)
        y_ssd = ssd_branch(z, xbc, dt_raw, conv_w[l], conv_b[l], dt_bias[l], a_log[l], d_skip[l],
                           ssd_norm[l], w_ssd_out[l])
        y_mla = mla_branch(cq, ckv, kr, cos, sin, q_lora_norm[l], w_uq[l], kv_lora_norm[l], w_ukv[l],
                           q_norm[l], k_norm[l], w_mla_out[l])
        g = jax.nn.sigmoid(gates.astype(jnp.float32)).astype(h.dtype)
        merged = g[..., :D_MODEL] * y_ssd + g[..., D_MODEL:] * y_mla
        h = h + merged @ w_o[l]
        h = h + 0.5 * swiglu(rms_norm(h, ln_ffn2[l]), ffn2_w13[l], ffn2_w2[l])
    return h


import jax as _jax
import jax.numpy as _jnp

TWIN_FORMAT = 'train_step'
FWD_PARAMS = ['x', 'positions', 'ln_ffn1', 'ffn1_w13', 'ffn1_w2', 'ln_mix', 'w_in', 'conv_w', 'conv_b', 'dt_bias', 'a_log', 'd_skip', 'ssd_norm', 'w_ssd_out', 'q_lora_norm', 'w_uq', 'kv_lora_norm', 'w_ukv', 'q_norm', 'k_norm', 'w_mla_out', 'w_o', 'ln_ffn2', 'ffn2_w13', 'ffn2_w2']
TWIN_WEIGHTS = ['ln_ffn1', 'ffn1_w13', 'ffn1_w2', 'ln_mix', 'w_in', 'conv_w', 'conv_b', 'dt_bias', 'a_log', 'd_skip', 'ssd_norm', 'w_ssd_out', 'q_lora_norm', 'w_uq', 'kv_lora_norm', 'w_ukv', 'q_norm', 'k_norm', 'w_mla_out', 'w_o', 'ln_ffn2', 'ffn2_w13', 'ffn2_w2']
TWIN_DIFF_INPUT = 'x'
TWIN_INPUTS = ['x', 'positions', 'ln_ffn1', 'ffn1_w13', 'ffn1_w2', 'ln_mix', 'w_in', 'conv_w', 'conv_b', 'dt_bias', 'a_log', 'd_skip', 'ssd_norm', 'w_ssd_out', 'q_lora_norm', 'w_uq', 'kv_lora_norm', 'w_ukv', 'q_norm', 'k_norm', 'w_mla_out', 'w_o', 'ln_ffn2', 'ffn2_w13', 'ffn2_w2', 'loss_target', 'm_ln_ffn1', 'm_ffn1_w13', 'm_ffn1_w2', 'm_ln_mix', 'm_w_in', 'm_conv_w', 'm_conv_b', 'm_dt_bias', 'm_a_log', 'm_d_skip', 'm_ssd_norm', 'm_w_ssd_out', 'm_q_lora_norm', 'm_w_uq', 'm_kv_lora_norm', 'm_w_ukv', 'm_q_norm', 'm_k_norm', 'm_w_mla_out', 'm_w_o', 'm_ln_ffn2', 'm_ffn2_w13', 'm_ffn2_w2', 'v_ln_ffn1', 'v_ffn1_w13', 'v_ffn1_w2', 'v_ln_mix', 'v_w_in', 'v_conv_w', 'v_conv_b', 'v_dt_bias', 'v_a_log', 'v_d_skip', 'v_ssd_norm', 'v_w_ssd_out', 'v_q_lora_norm', 'v_w_uq', 'v_kv_lora_norm', 'v_w_ukv', 'v_q_norm', 'v_k_norm', 'v_w_mla_out', 'v_w_o', 'v_ln_ffn2', 'v_ffn2_w13', 'v_ffn2_w2']
TWIN_OUTPUTS = ['loss', 'grad_x', 'grad_ln_ffn1', 'grad_ffn1_w13', 'grad_ffn1_w2', 'grad_ln_mix', 'grad_w_in', 'grad_conv_w', 'grad_conv_b', 'grad_dt_bias', 'grad_a_log', 'grad_d_skip', 'grad_ssd_norm', 'grad_w_ssd_out', 'grad_q_lora_norm', 'grad_w_uq', 'grad_kv_lora_norm', 'grad_w_ukv', 'grad_q_norm', 'grad_k_norm', 'grad_w_mla_out', 'grad_w_o', 'grad_ln_ffn2', 'grad_ffn2_w13', 'grad_ffn2_w2', 'delta_ln_ffn1', 'delta_ffn1_w13', 'delta_ffn1_w2', 'delta_ln_mix', 'delta_w_in', 'delta_conv_w', 'delta_conv_b', 'delta_dt_bias', 'delta_a_log', 'delta_d_skip', 'delta_ssd_norm', 'delta_w_ssd_out', 'delta_q_lora_norm', 'delta_w_uq', 'delta_kv_lora_norm', 'delta_w_ukv', 'delta_q_norm', 'delta_k_norm', 'delta_w_mla_out', 'delta_w_o', 'delta_ln_ffn2', 'delta_ffn2_w13', 'delta_ffn2_w2', 'new_m_ln_ffn1', 'new_m_ffn1_w13', 'new_m_ffn1_w2', 'new_m_ln_mix', 'new_m_w_in', 'new_m_conv_w', 'new_m_conv_b', 'new_m_dt_bias', 'new_m_a_log', 'new_m_d_skip', 'new_m_ssd_norm', 'new_m_w_ssd_out', 'new_m_q_lora_norm', 'new_m_w_uq', 'new_m_kv_lora_norm', 'new_m_w_ukv', 'new_m_q_norm', 'new_m_k_norm', 'new_m_w_mla_out', 'new_m_w_o', 'new_m_ln_ffn2', 'new_m_ffn2_w13', 'new_m_ffn2_w2', 'new_v_ln_ffn1', 'new_v_ffn1_w13', 'new_v_ffn1_w2', 'new_v_ln_mix', 'new_v_w_in', 'new_v_conv_w', 'new_v_conv_b', 'new_v_dt_bias', 'new_v_a_log', 'new_v_d_skip', 'new_v_ssd_norm', 'new_v_w_ssd_out', 'new_v_q_lora_norm', 'new_v_w_uq', 'new_v_kv_lora_norm', 'new_v_w_ukv', 'new_v_q_norm', 'new_v_k_norm', 'new_v_w_mla_out', 'new_v_w_o', 'new_v_ln_ffn2', 'new_v_ffn2_w13', 'new_v_ffn2_w2']
TWIN_LEAF_KINDS = {'loss': 'loss', 'grad_x': 'grad_x', 'grad_ln_ffn1': 'grad_w', 'grad_ffn1_w13': 'grad_w', 'grad_ffn1_w2': 'grad_w', 'grad_ln_mix': 'grad_w', 'grad_w_in': 'grad_w', 'grad_conv_w': 'grad_w', 'grad_conv_b': 'grad_w', 'grad_dt_bias': 'grad_w', 'grad_a_log': 'grad_w', 'grad_d_skip': 'grad_w', 'grad_ssd_norm': 'grad_w', 'grad_w_ssd_out': 'grad_w', 'grad_q_lora_norm': 'grad_w', 'grad_w_uq': 'grad_w', 'grad_kv_lora_norm': 'grad_w', 'grad_w_ukv': 'grad_w', 'grad_q_norm': 'grad_w', 'grad_k_norm': 'grad_w', 'grad_w_mla_out': 'grad_w', 'grad_w_o': 'grad_w', 'grad_ln_ffn2': 'grad_w', 'grad_ffn2_w13': 'grad_w', 'grad_ffn2_w2': 'grad_w', 'delta_ln_ffn1': 'delta_w', 'delta_ffn1_w13': 'delta_w', 'delta_ffn1_w2': 'delta_w', 'delta_ln_mix': 'delta_w', 'delta_w_in': 'delta_w', 'delta_conv_w': 'delta_w', 'delta_conv_b': 'delta_w', 'delta_dt_bias': 'delta_w', 'delta_a_log': 'delta_w', 'delta_d_skip': 'delta_w', 'delta_ssd_norm': 'delta_w', 'delta_w_ssd_out': 'delta_w', 'delta_q_lora_norm': 'delta_w', 'delta_w_uq': 'delta_w', 'delta_kv_lora_norm': 'delta_w', 'delta_w_ukv': 'delta_w', 'delta_q_norm': 'delta_w', 'delta_k_norm': 'delta_w', 'delta_w_mla_out': 'delta_w', 'delta_w_o': 'delta_w', 'delta_ln_ffn2': 'delta_w', 'delta_ffn2_w13': 'delta_w', 'delta_ffn2_w2': 'delta_w', 'new_m_ln_ffn1': 'new_m', 'new_m_ffn1_w13': 'new_m', 'new_m_ffn1_w2': 'new_m', 'new_m_ln_mix': 'new_m', 'new_m_w_in': 'new_m', 'new_m_conv_w': 'new_m', 'new_m_conv_b': 'new_m', 'new_m_dt_bias': 'new_m', 'new_m_a_log': 'new_m', 'new_m_d_skip': 'new_m', 'new_m_ssd_norm': 'new_m', 'new_m_w_ssd_out': 'new_m', 'new_m_q_lora_norm': 'new_m', 'new_m_w_uq': 'new_m', 'new_m_kv_lora_norm': 'new_m', 'new_m_w_ukv': 'new_m', 'new_m_q_norm': 'new_m', 'new_m_k_norm': 'new_m', 'new_m_w_mla_out': 'new_m', 'new_m_w_o': 'new_m', 'new_m_ln_ffn2': 'new_m', 'new_m_ffn2_w13': 'new_m', 'new_m_ffn2_w2': 'new_m', 'new_v_ln_ffn1': 'new_v', 'new_v_ffn1_w13': 'new_v', 'new_v_ffn1_w2': 'new_v', 'new_v_ln_mix': 'new_v', 'new_v_w_in': 'new_v', 'new_v_conv_w': 'new_v', 'new_v_conv_b': 'new_v', 'new_v_dt_bias': 'new_v', 'new_v_a_log': 'new_v', 'new_v_d_skip': 'new_v', 'new_v_ssd_norm': 'new_v', 'new_v_w_ssd_out': 'new_v', 'new_v_q_lora_norm': 'new_v', 'new_v_w_uq': 'new_v', 'new_v_kv_lora_norm': 'new_v', 'new_v_w_ukv': 'new_v', 'new_v_q_norm': 'new_v', 'new_v_k_norm': 'new_v', 'new_v_w_mla_out': 'new_v', 'new_v_w_o': 'new_v', 'new_v_ln_ffn2': 'new_v', 'new_v_ffn2_w13': 'new_v', 'new_v_ffn2_w2': 'new_v'}


def _forward(args):
    return _fwd_reference(*[args[k] for k in FWD_PARAMS])


def _output_shape():
    def fwd():
        inp = _fwd_setup_inputs(0)
        return _fwd_reference(*[inp[k] for k in FWD_PARAMS])
    out = _jax.eval_shape(fwd)
    return out.shape, out.dtype

N_MICROBATCH = 1
ADAM_LR = 0.001
ADAM_B1 = 0.9
ADAM_B2 = 0.999
ADAM_EPS = 1e-08
ADAM_WD = 0.01
ADAM_STEP = 10
PER_EXAMPLE_BATCH_AXIS = {'x': 0, 'positions': 0, 'loss_target': 0}
SHARED_INPUTS = []
_WEIGHT_DTYPES = {'ln_ffn1': _jnp.float32, 'ffn1_w13': _jnp.float32, 'ffn1_w2': _jnp.float32, 'ln_mix': _jnp.float32, 'w_in': _jnp.float32, 'conv_w': _jnp.float32, 'conv_b': _jnp.float32, 'dt_bias': _jnp.float32, 'a_log': _jnp.float32, 'd_skip': _jnp.float32, 'ssd_norm': _jnp.float32, 'w_ssd_out': _jnp.float32, 'q_lora_norm': _jnp.float32, 'w_uq': _jnp.float32, 'kv_lora_norm': _jnp.float32, 'w_ukv': _jnp.float32, 'q_norm': _jnp.float32, 'k_norm': _jnp.float32, 'w_mla_out': _jnp.float32, 'w_o': _jnp.float32, 'ln_ffn2': _jnp.float32, 'ffn2_w13': _jnp.float32, 'ffn2_w2': _jnp.float32}
MOMENT_SCALE = {'ln_ffn1': 1.226239e+01, 'ffn1_w13': 1.596016e-01, 'ffn1_w2': 2.744627e-01, 'ln_mix': 1.704804e+00, 'w_in': 2.445744e-01, 'conv_w': 5.719712e-01, 'conv_b': 1.992388e+00, 'dt_bias': 7.338357e-01, 'a_log': 4.378333e+00, 'd_skip': 4.388036e+00, 'ssd_norm': 1.451750e+01, 'w_ssd_out': 1.426154e+00, 'q_lora_norm': 8.690267e-02, 'w_uq': 4.994461e-02, 'kv_lora_norm': 7.724072e-01, 'w_ukv': 1.244431e-01, 'q_norm': 4.956031e-01, 'k_norm': 4.957677e-01, 'w_mla_out': 1.888455e-01, 'w_o': 1.212484e+00, 'ln_ffn2': 1.232481e+01, 'ffn2_w13': 1.556605e-01, 'ffn2_w2': 2.642645e-01}


def _to_microbatches(a, axis):
    t = _jnp.moveaxis(a, axis, 0)
    t = t.reshape((N_MICROBATCH, t.shape[0] // N_MICROBATCH) + t.shape[1:])
    return _jnp.moveaxis(t, 1, axis + 1)


def setup_inputs(seed: int = 0) -> dict:
    inp = _fwd_setup_inputs(seed)
    key = _jax.random.fold_in(_jax.random.key(seed), 7919)
    shape, _ = _output_shape()
    out = dict(inp)
    out["loss_target"] = _jax.random.normal(_jax.random.fold_in(key, 0), shape, _jnp.float32)
    for i, name in enumerate(TWIN_WEIGHTS):
        w = inp[name].astype(_jnp.float32)
        if MOMENT_SCALE is None:
            s = _jnp.sqrt(_jnp.mean(_jnp.square(w)) + 1e-30)
        else:
            s = MOMENT_SCALE[name]
        km, kv = _jax.random.split(_jax.random.fold_in(key, i + 1))
        out[name] = w
        out["m_" + name] = s * _jax.random.normal(km, w.shape, _jnp.float32)
        out["v_" + name] = (s * s) * _jax.random.uniform(kv, w.shape, _jnp.float32, 0.5, 1.5)
    if N_MICROBATCH > 1:
        for name, axis in PER_EXAMPLE_BATCH_AXIS.items():
            out[name] = _to_microbatches(out[name], axis)
    return {'x': out['x'], 'positions': out['positions'], 'ln_ffn1': out['ln_ffn1'], 'ffn1_w13': out['ffn1_w13'], 'ffn1_w2': out['ffn1_w2'], 'ln_mix': out['ln_mix'], 'w_in': out['w_in'], 'conv_w': out['conv_w'], 'conv_b': out['conv_b'], 'dt_bias': out['dt_bias'], 'a_log': out['a_log'], 'd_skip': out['d_skip'], 'ssd_norm': out['ssd_norm'], 'w_ssd_out': out['w_ssd_out'], 'q_lora_norm': out['q_lora_norm'], 'w_uq': out['w_uq'], 'kv_lora_norm': out['kv_lora_norm'], 'w_ukv': out['w_ukv'], 'q_norm': out['q_norm'], 'k_norm': out['k_norm'], 'w_mla_out': out['w_mla_out'], 'w_o': out['w_o'], 'ln_ffn2': out['ln_ffn2'], 'ffn2_w13': out['ffn2_w13'], 'ffn2_w2': out['ffn2_w2'], 'loss_target': out['loss_target'], 'm_ln_ffn1': out['m_ln_ffn1'], 'm_ffn1_w13': out['m_ffn1_w13'], 'm_ffn1_w2': out['m_ffn1_w2'], 'm_ln_mix': out['m_ln_mix'], 'm_w_in': out['m_w_in'], 'm_conv_w': out['m_conv_w'], 'm_conv_b': out['m_conv_b'], 'm_dt_bias': out['m_dt_bias'], 'm_a_log': out['m_a_log'], 'm_d_skip': out['m_d_skip'], 'm_ssd_norm': out['m_ssd_norm'], 'm_w_ssd_out': out['m_w_ssd_out'], 'm_q_lora_norm': out['m_q_lora_norm'], 'm_w_uq': out['m_w_uq'], 'm_kv_lora_norm': out['m_kv_lora_norm'], 'm_w_ukv': out['m_w_ukv'], 'm_q_norm': out['m_q_norm'], 'm_k_norm': out['m_k_norm'], 'm_w_mla_out': out['m_w_mla_out'], 'm_w_o': out['m_w_o'], 'm_ln_ffn2': out['m_ln_ffn2'], 'm_ffn2_w13': out['m_ffn2_w13'], 'm_ffn2_w2': out['m_ffn2_w2'], 'v_ln_ffn1': out['v_ln_ffn1'], 'v_ffn1_w13': out['v_ffn1_w13'], 'v_ffn1_w2': out['v_ffn1_w2'], 'v_ln_mix': out['v_ln_mix'], 'v_w_in': out['v_w_in'], 'v_conv_w': out['v_conv_w'], 'v_conv_b': out['v_conv_b'], 'v_dt_bias': out['v_dt_bias'], 'v_a_log': out['v_a_log'], 'v_d_skip': out['v_d_skip'], 'v_ssd_norm': out['v_ssd_norm'], 'v_w_ssd_out': out['v_w_ssd_out'], 'v_q_lora_norm': out['v_q_lora_norm'], 'v_w_uq': out['v_w_uq'], 'v_kv_lora_norm': out['v_kv_lora_norm'], 'v_w_ukv': out['v_w_ukv'], 'v_q_norm': out['v_q_norm'], 'v_k_norm': out['v_k_norm'], 'v_w_mla_out': out['v_w_mla_out'], 'v_w_o': out['v_w_o'], 'v_ln_ffn2': out['v_ln_ffn2'], 'v_ffn2_w13': out['v_ffn2_w13'], 'v_ffn2_w2': out['v_ffn2_w2']}


def _loss(weights, diff, rest, loss_target):
    with _jax.named_scope("forward"):
        args = {**rest, TWIN_DIFF_INPUT: diff, **{k: w.astype(_WEIGHT_DTYPES[k]) for k, w in weights.items()}}
        y = _forward(args)
    with _jax.named_scope("loss_head"):
        err = _jnp.square(y.astype(_jnp.float32) - loss_target)
        return 0.5 * _jnp.sum(_jnp.mean(err, axis=-1)) if err.ndim else 0.5 * err


def _adamw(w, g, m, v):
    m = ADAM_B1 * m + (1.0 - ADAM_B1) * g
    v = ADAM_B2 * v + (1.0 - ADAM_B2) * _jnp.square(g)
    m_hat = m / (1.0 - ADAM_B1 ** ADAM_STEP)
    v_hat = v / (1.0 - ADAM_B2 ** ADAM_STEP)
    delta = -ADAM_LR * (m_hat / (_jnp.sqrt(v_hat) + ADAM_EPS) + ADAM_WD * w)
    return delta, m, v


def reference(x, positions, ln_ffn1, ffn1_w13, ffn1_w2, ln_mix, w_in, conv_w, conv_b, dt_bias, a_log, d_skip, ssd_norm, w_ssd_out, q_lora_norm, w_uq, kv_lora_norm, w_ukv, q_norm, k_norm, w_mla_out, w_o, ln_ffn2, ffn2_w13, ffn2_w2, loss_target, m_ln_ffn1, m_ffn1_w13, m_ffn1_w2, m_ln_mix, m_w_in, m_conv_w, m_conv_b, m_dt_bias, m_a_log, m_d_skip, m_ssd_norm, m_w_ssd_out, m_q_lora_norm, m_w_uq, m_kv_lora_norm, m_w_ukv, m_q_norm, m_k_norm, m_w_mla_out, m_w_o, m_ln_ffn2, m_ffn2_w13, m_ffn2_w2, v_ln_ffn1, v_ffn1_w13, v_ffn1_w2, v_ln_mix, v_w_in, v_conv_w, v_conv_b, v_dt_bias, v_a_log, v_d_skip, v_ssd_norm, v_w_ssd_out, v_q_lora_norm, v_w_uq, v_kv_lora_norm, v_w_ukv, v_q_norm, v_k_norm, v_w_mla_out, v_w_o, v_ln_ffn2, v_ffn2_w13, v_ffn2_w2):
    given = dict(x=x, positions=positions, ln_ffn1=ln_ffn1, ffn1_w13=ffn1_w13, ffn1_w2=ffn1_w2, ln_mix=ln_mix, w_in=w_in, conv_w=conv_w, conv_b=conv_b, dt_bias=dt_bias, a_log=a_log, d_skip=d_skip, ssd_norm=ssd_norm, w_ssd_out=w_ssd_out, q_lora_norm=q_lora_norm, w_uq=w_uq, kv_lora_norm=kv_lora_norm, w_ukv=w_ukv, q_norm=q_norm, k_norm=k_norm, w_mla_out=w_mla_out, w_o=w_o, ln_ffn2=ln_ffn2, ffn2_w13=ffn2_w13, ffn2_w2=ffn2_w2, loss_target=loss_target, m_ln_ffn1=m_ln_ffn1, m_ffn1_w13=m_ffn1_w13, m_ffn1_w2=m_ffn1_w2, m_ln_mix=m_ln_mix, m_w_in=m_w_in, m_conv_w=m_conv_w, m_conv_b=m_conv_b, m_dt_bias=m_dt_bias, m_a_log=m_a_log, m_d_skip=m_d_skip, m_ssd_norm=m_ssd_norm, m_w_ssd_out=m_w_ssd_out, m_q_lora_norm=m_q_lora_norm, m_w_uq=m_w_uq, m_kv_lora_norm=m_kv_lora_norm, m_w_ukv=m_w_ukv, m_q_norm=m_q_norm, m_k_norm=m_k_norm, m_w_mla_out=m_w_mla_out, m_w_o=m_w_o, m_ln_ffn2=m_ln_ffn2, m_ffn2_w13=m_ffn2_w13, m_ffn2_w2=m_ffn2_w2, v_ln_ffn1=v_ln_ffn1, v_ffn1_w13=v_ffn1_w13, v_ffn1_w2=v_ffn1_w2, v_ln_mix=v_ln_mix, v_w_in=v_w_in, v_conv_w=v_conv_w, v_conv_b=v_conv_b, v_dt_bias=v_dt_bias, v_a_log=v_a_log, v_d_skip=v_d_skip, v_ssd_norm=v_ssd_norm, v_w_ssd_out=v_w_ssd_out, v_q_lora_norm=v_q_lora_norm, v_w_uq=v_w_uq, v_kv_lora_norm=v_kv_lora_norm, v_w_ukv=v_w_ukv, v_q_norm=v_q_norm, v_k_norm=v_k_norm, v_w_mla_out=v_w_mla_out, v_w_o=v_w_o, v_ln_ffn2=v_ln_ffn2, v_ffn2_w13=v_ffn2_w13, v_ffn2_w2=v_ffn2_w2)
    weights = {n: given[n] for n in TWIN_WEIGHTS}
    shared = {n: given[n] for n in SHARED_INPUTS}
    per_example = {n: given[n] for n in ['x', 'positions']}
    grad_fn = _jax.value_and_grad(_loss, argnums=(0, 1))

    def one_microbatch(ex, loss_target):
        ex = dict(ex)
        diff = ex.pop(TWIN_DIFF_INPUT)
        return grad_fn(weights, diff, {**shared, **ex}, loss_target)

    if N_MICROBATCH == 1:
        loss, (grad_w, grad_x) = one_microbatch(per_example, given["loss_target"])
    else:
        def body(carry, xs):
            loss_sum, grad_sum = carry
            l_k, (gw_k, gx_k) = one_microbatch(xs[0], xs[1])
            with _jax.named_scope("update"):
                return (loss_sum + l_k, _jax.tree.map(_jnp.add, grad_sum, gw_k)), gx_k

        init = (_jnp.zeros((), _jnp.float32), _jax.tree.map(_jnp.zeros_like, weights))
        (loss, grad_w), grad_x = _jax.lax.scan(body, init, (per_example, given["loss_target"]))
    with _jax.named_scope("update"):
        delta_w, new_m, new_v = {}, {}, {}
        for n in TWIN_WEIGHTS:
            delta_w[n], new_m[n], new_v[n] = _adamw(weights[n], grad_w[n], given["m_" + n], given["v_" + n])
    return (loss, grad_x, *[grad_w[n] for n in TWIN_WEIGHTS], *[delta_w[n] for n in TWIN_WEIGHTS],
            *[new_m[n] for n in TWIN_WEIGHTS], *[new_v[n] for n in TWIN_WEIGHTS])
```

```python
import functools
import math

import jax
import jax.numpy as jnp
from jax import lax
from jax.experimental import pallas as pl
from jax.experimental.pallas import tpu as pltpu

F32 = jnp.float32
BF16 = jnp.bfloat16

D_MODEL = 1024
DEPTH = 2
D_FF = 2816
SSD_D_INNER = 2048
SSD_HEADS = 32
SSD_HEAD_DIM = 64
SSD_GROUPS = 4
SSD_STATE = 128
SSD_CHUNK = 128
SSD_CONV = 4
SSD_CONV_DIM = 3072
MLA_HEADS = 8
MLA_Q_LORA = 512
MLA_KV_LORA = 256
MLA_NOPE = 128
MLA_ROPE = 64
MLA_V = 128
MLA_QK = 192
ROPE_THETA = 10000.0
EPS = 1e-6
ADAM_LR = 0.001
ADAM_B1 = 0.9
ADAM_B2 = 0.999
ADAM_EPS = 1e-08
ADAM_WD = 0.01
ADAM_STEP = 10

PROJ_W = 8064
OFF_Z, OFF_XBC, OFF_GATES, OFF_CQ, OFF_CKV, OFF_KRDT = 0, 2048, 5120, 7168, 7680, 7936

LANE = 128
VMEM_LIMIT = 48 * 1024 * 1024
HI = lax.Precision.HIGHEST


def _cp(*sem):
    return pltpu.CompilerParams(dimension_semantics=sem, vmem_limit_bytes=VMEM_LIMIT)


def _pick(dim, target, align):
    if dim <= target:
        return dim
    b = (target // align) * align
    while b >= align:
        if dim % b == 0:
            return b
        b -= align
    raise ValueError(f"no block for {dim} (target {target}, align {align})")


def _silu(x):
    return x * jax.nn.sigmoid(x)


def _dsilu(x):
    s = jax.nn.sigmoid(x)
    return s * (1.0 + x * (1.0 - s))


def _matmul(a, b, mode, *, name, out_dtype=F32, scale=1.0, res=None):
    if mode == "nn":
        (m, k), (k2, n) = a.shape, b.shape
    elif mode == "nt":
        (m, k), (n, k2) = a.shape, b.shape
    else:
        (k, m), (k2, n) = a.shape, b.shape
    assert k == k2, (a.shape, b.shape, mode)
    bm = _pick(m, 512, LANE if mode == "tn" else 8)
    bn = _pick(n, 512, LANE)
    bk = _pick(k, 1536, LANE)
    nk = k // bk

    def body(a_ref, b_ref, *rest):
        res_ref = rest[0] if res is not None else None
        o_ref, acc_ref = rest[-2:]
        kk = pl.program_id(2)
        av = a_ref[...].astype(BF16)
        bv = b_ref[...].astype(BF16)
        if mode == "nn":
            dims = (((1,), (0,)), ((), ()))
        elif mode == "nt":
            dims = (((1,), (1,)), ((), ()))
        else:
            dims = (((0,), (0,)), ((), ()))
        part = lax.dot_general(av, bv, dims, preferred_element_type=F32)

        @pl.when(kk == 0)
        def _():
            acc_ref[...] = part

        @pl.when(kk > 0)
        def _():
            acc_ref[...] += part

        @pl.when(kk == nk - 1)
        def _():
            out = acc_ref[...] * scale
            if res_ref is not None:
                out = res_ref[...] + out
            o_ref[...] = out.astype(o_ref.dtype)

    o_spec = pl.BlockSpec((bm, bn), lambda i, j, kk: (i, j))
    if mode == "nn":
        a_spec = pl.BlockSpec((bm, bk), lambda i, j, kk: (i, kk))
        b_spec = pl.BlockSpec((bk, bn), lambda i, j, kk: (kk, j))
    elif mode == "nt":
        a_spec = pl.BlockSpec((bm, bk), lambda i, j, kk: (i, kk))
        b_spec = pl.BlockSpec((bn, bk), lambda i, j, kk: (j, kk))
    else:
        a_spec = pl.BlockSpec((bk, bm), lambda i, j, kk: (kk, i))
        b_spec = pl.BlockSpec((bk, bn), lambda i, j, kk: (kk, j))
    return pl.pallas_call(
        body, name=name,
        grid=(m // bm, n // bn, nk),
        in_specs=[a_spec, b_spec] + ([o_spec] if res is not None else []),
        out_specs=o_spec,
        out_shape=jax.ShapeDtypeStruct((m, n), out_dtype),
        scratch_shapes=[pltpu.VMEM((bm, bn), F32)],
        compiler_params=_cp("parallel", "parallel", "arbitrary"),
    )(*((a, b) + ((res,) if res is not None else ())))


def _rms_fwd(x, g, *, name, col=0, width=None):
    r = x.shape[0]
    w = width or x.shape[1]
    tr = _pick(r, 512, 8)

    def body(x_ref, g_ref, o_ref):
        xv = x_ref[...]
        rs = lax.rsqrt(jnp.mean(xv * xv, axis=-1, keepdims=True) + EPS)
        o_ref[...] = xv * rs * g_ref[...]

    return pl.pallas_call(
        body, name=name, grid=(r // tr,),
        in_specs=[pl.BlockSpec((tr, w), lambda i: (i, col)), pl.BlockSpec((1, w), lambda i: (0, 0))],
        out_specs=pl.BlockSpec((tr, w), lambda i: (i, 0)),
        out_shape=jax.ShapeDtypeStruct((r, w), F32),
        compiler_params=_cp("parallel"),
    )(x, g)


def _rms_bwd(x, g, dy, *, name, col=0, width=None, res=None):
    r = x.shape[0]
    w = width or x.shape[1]
    tr = _pick(r, 512, 8)

    def body(x_ref, g_ref, dy_ref, *rest):
        res_ref = rest[0] if res is not None else None
        dx_ref, dg_ref = rest[-2:]
        i = pl.program_id(0)
        xv = x_ref[...]
        dyv = dy_ref[...]
        rs = lax.rsqrt(jnp.mean(xv * xv, axis=-1, keepdims=True) + EPS)
        xh = xv * rs
        dxh = dyv * g_ref[...]
        mm = jnp.mean(dxh * xh, axis=-1, keepdims=True)
        dx = rs * (dxh - xh * mm)
        if res_ref is not None:
            dx = res_ref[...] + dx
        dx_ref[...] = dx
        part = jnp.sum(dyv * xh, axis=0, keepdims=True)

        @pl.when(i == 0)
        def _():
            dg_ref[...] = part

        @pl.when(i > 0)
        def _():
            dg_ref[...] += part

    blk = pl.BlockSpec((tr, w), lambda i: (i, 0))
    return pl.pallas_call(
        body, name=name, grid=(r // tr,),
        in_specs=[pl.BlockSpec((tr, w), lambda i: (i, col)), pl.BlockSpec((1, w), lambda i: (0, 0)), blk]
        + ([blk] if res is not None else []),
        out_specs=[blk, pl.BlockSpec((1, w), lambda i: (0, 0))],
        out_shape=[jax.ShapeDtypeStruct((r, w), F32), jax.ShapeDtypeStruct((1, w), F32)],
        compiler_params=_cp("arbitrary"),
    )(*((x, g, dy) + ((res,) if res is not None else ())))


def _gated_rms_fwd(y, proj, g, *, name):
    r, w = y.shape
    tr = _pick(r, 256, 8)

    def body(y_ref, z_ref, g_ref, o_ref):
        t = y_ref[...] * _silu(z_ref[...])
        rs = lax.rsqrt(jnp.mean(t * t, axis=-1, keepdims=True) + EPS)
        o_ref[...] = t * rs * g_ref[...]

    return pl.pallas_call(
        body, name=name, grid=(r // tr,),
        in_specs=[pl.BlockSpec((tr, w), lambda i: (i, 0)), pl.BlockSpec((tr, w), lambda i: (i, OFF_Z // w)),
                  pl.BlockSpec((1, w), lambda i: (0, 0))],
        out_specs=pl.BlockSpec((tr, w), lambda i: (i, 0)),
        out_shape=jax.ShapeDtypeStruct((r, w), F32),
        compiler_params=_cp("parallel"),
    )(y, proj, g)


def _gated_rms_bwd(y, proj, g, do, *, name):
    r, w = y.shape
    tr = _pick(r, 256, 8)

    def body(y_ref, z_ref, g_ref, do_ref, dy_ref, dz_ref, dg_ref):
        i = pl.program_id(0)
        yv, zv, dov = y_ref[...], z_ref[...], do_ref[...]
        sz = _silu(zv)
        t = yv * sz
        rs = lax.rsqrt(jnp.mean(t * t, axis=-1, keepdims=True) + EPS)
        th = t * rs
        dth = dov * g_ref[...]
        mm = jnp.mean(dth * th, axis=-1, keepdims=True)
        dt = rs * (dth - th * mm)
        dy_ref[...] = dt * sz
        dz_ref[...] = dt * yv * _dsilu(zv)
        part = jnp.sum(dov * th, axis=0, keepdims=True)

        @pl.when(i == 0)
        def _():
            dg_ref[...] = part

        @pl.when(i > 0)
        def _():
            dg_ref[...] += part

    blk = pl.BlockSpec((tr, w), lambda i: (i, 0))
    vec = pl.BlockSpec((1, w), lambda i: (0, 0))
    return pl.pallas_call(
        body, name=name, grid=(r // tr,),
        in_specs=[blk, pl.BlockSpec((tr, w), lambda i: (i, OFF_Z // w)), vec, blk],
        out_specs=[blk, blk, vec],
        out_shape=[jax.ShapeDtypeStruct((r, w), F32), jax.ShapeDtypeStruct((r, w), F32),
                   jax.ShapeDtypeStruct((1, w), F32)],
        compiler_params=_cp("arbitrary"),
    )(y, proj, g, do)


def _swiglu_fwd(gu, *, name):
    r = gu.shape[0]
    f = gu.shape[1] // 2
    tr = _pick(r, 256, 8)

    def body(g_ref, u_ref, o_ref):
        o_ref[...] = _silu(g_ref[...]) * u_ref[...]

    return pl.pallas_call(
        body, name=name, grid=(r // tr,),
        in_specs=[pl.BlockSpec((tr, f), lambda i: (i, 0)), pl.BlockSpec((tr, f), lambda i: (i, 1))],
        out_specs=pl.BlockSpec((tr, f), lambda i: (i, 0)),
        out_shape=jax.ShapeDtypeStruct((r, f), F32),
        compiler_params=_cp("parallel"),
    )(gu, gu)


def _swiglu_bwd(gu, da, *, name):
    r = gu.shape[0]
    f = gu.shape[1] // 2
    tr = _pick(r, 256, 8)

    def body(g_ref, u_ref, da_ref, o_ref):
        gv, uv, dav = g_ref[...], u_ref[...], da_ref[...]
        o_ref[:, :f] = dav * uv * _dsilu(gv)
        o_ref[:, f:] = dav * _silu(gv)

    return pl.pallas_call(
        body, name=name, grid=(r // tr,),
        in_specs=[pl.BlockSpec((tr, f), lambda i: (i, 0)), pl.BlockSpec((tr, f), lambda i: (i, 1)),
                  pl.BlockSpec((tr, f), lambda i: (i, 0))],
        out_specs=pl.BlockSpec((tr, 2 * f), lambda i: (i, 0)),
        out_shape=jax.ShapeDtypeStruct((r, 2 * f), F32),
        compiler_params=_cp("parallel"),
    )(gu, gu, da)


CONV_TS = 1024
CONV_TC = 512


def _conv_pre(x, carry, w_ref, b_ref):
    ts = x.shape[0]
    row8 = lax.broadcasted_iota(jnp.int32, (8, x.shape[1]), 0)
    head_x = x[0:8]
    shifted, shifted_head = [], []
    for j in range(SSD_CONV):
        if j == 0:
            shifted.append(x)
            shifted_head.append(head_x)
        else:
            shifted.append(pltpu.roll(x, j, 0))
            shifted_head.append(jnp.where(row8 < j, pltpu.roll(carry, j, 0), pltpu.roll(head_x, j, 0)))
    pre = b_ref[...] + sum(w_ref[SSD_CONV - 1 - j:SSD_CONV - j, :] * shifted[j] for j in range(SSD_CONV))
    pre_head = b_ref[...] + sum(w_ref[SSD_CONV - 1 - j:SSD_CONV - j, :] * shifted_head[j] for j in range(SSD_CONV))
    del ts
    return pre, pre_head, shifted, shifted_head


def _conv_fwd(proj, w, b, *, name):
    s = proj.shape[0]
    c = w.shape[1]
    ts, tc = _pick(s, CONV_TS, 8), CONV_TC
    off = OFF_XBC // tc

    def body(x_ref, w_ref, b_ref, o_ref, carry_ref):
        t = pl.program_id(1)

        @pl.when(t == 0)
        def _():
            carry_ref[...] = jnp.zeros_like(carry_ref)

        x = x_ref[...]
        pre, pre_head, _, _ = _conv_pre(x, carry_ref[...], w_ref, b_ref)
        o_ref[...] = _silu(pre)
        o_ref[0:8, :] = _silu(pre_head)
        carry_ref[...] = x[ts - 8:ts]

    return pl.pallas_call(
        body, name=name, grid=(c // tc, s // ts),
        in_specs=[pl.BlockSpec((ts, tc), lambda j, t: (t, j + off)), pl.BlockSpec((SSD_CONV, tc), lambda j, t: (0, j)),
                  pl.BlockSpec((1, tc), lambda j, t: (0, j))],
        out_specs=pl.BlockSpec((ts, tc), lambda j, t: (t, j)),
        out_shape=jax.ShapeDtypeStruct((s, c), F32),
        scratch_shapes=[pltpu.VMEM((8, tc), F32)],
        compiler_params=_cp("parallel", "arbitrary"),
    )(proj, w, b)


def _conv_bwd_pre(proj, w, b, dy, *, name):
    s = proj.shape[0]
    c = w.shape[1]
    ts, tc = _pick(s, CONV_TS, 8), CONV_TC
    off = OFF_XBC // tc

    def body(x_ref, w_ref, b_ref, dy_ref, dp_ref, dw_ref, db_ref, carry_ref):
        t = pl.program_id(1)

        @pl.when(t == 0)
        def _():
            carry_ref[...] = jnp.zeros_like(carry_ref)
            dw_ref[...] = jnp.zeros_like(dw_ref)
            db_ref[...] = jnp.zeros_like(db_ref)

        x = x_ref[...]
        pre, pre_head, shifted, shifted_head = _conv_pre(x, carry_ref[...], w_ref, b_ref)
        dyv = dy_ref[...]
        dp = dyv * _dsilu(pre)
        dp_head = dyv[0:8] * _dsilu(pre_head)
        row = lax.broadcasted_iota(jnp.int32, dp.shape, 0)
        dp_tail = jnp.where(row >= 8, dp, 0.0)
        dp_ref[...] = dp
        dp_ref[0:8, :] = dp_head
        db_ref[...] += jnp.sum(dp_tail, axis=0, keepdims=True) + jnp.sum(dp_head, axis=0, keepdims=True)
        for j in range(SSD_CONV):
            kk = SSD_CONV - 1 - j
            dw_ref[kk:kk + 1, :] += (jnp.sum(dp_tail * shifted[j], axis=0, keepdims=True)
                                     + jnp.sum(dp_head * shifted_head[j], axis=0, keepdims=True))
        carry_ref[...] = x[ts - 8:ts]

    return pl.pallas_call(
        body, name=name, grid=(c // tc, s // ts),
        in_specs=[pl.BlockSpec((ts, tc), lambda j, t: (t, j + off)), pl.BlockSpec((SSD_CONV, tc), lambda j, t: (0, j)),
                  pl.BlockSpec((1, tc), lambda j, t: (0, j)), pl.BlockSpec((ts, tc), lambda j, t: (t, j))],
        out_specs=[pl.BlockSpec((ts, tc), lambda j, t: (t, j)), pl.BlockSpec((SSD_CONV, tc), lambda j, t: (0, j)),
                   pl.BlockSpec((1, tc), lambda j, t: (0, j))],
        out_shape=[jax.ShapeDtypeStruct((s, c), F32), jax.ShapeDtypeStruct((SSD_CONV, c), F32),
                   jax.ShapeDtypeStruct((1, c), F32)],
        scratch_shapes=[pltpu.VMEM((8, tc), F32)],
        compiler_params=_cp("parallel", "arbitrary"),
    )(proj, w, b, dy)


def _conv_bwd_x(dp, w, *, name):
    s, c = dp.shape
    ts, tc = _pick(s, CONV_TS, 8), CONV_TC
    nt = s // ts

    def body(d_ref, w_ref, o_ref, carry_ref):
        t = pl.program_id(1)

        @pl.when(t == 0)
        def _():
            carry_ref[...] = jnp.zeros_like(carry_ref)

        d = d_ref[...]
        carry = carry_ref[...]
        row8 = lax.broadcasted_iota(jnp.int32, (8, tc), 0)
        tail = d[ts - 8:ts]
        acc = w_ref[SSD_CONV - 1:SSD_CONV, :] * d
        acc_tail = w_ref[SSD_CONV - 1:SSD_CONV, :] * tail
        for j in range(1, SSD_CONV):
            wj = w_ref[SSD_CONV - 1 - j:SSD_CONV - j, :]
            acc = acc + wj * pltpu.roll(d, ts - j, 0)
            up_tail = jnp.where(row8 >= 8 - j, pltpu.roll(carry, 8 - j, 0), pltpu.roll(tail, 8 - j, 0))
            acc_tail = acc_tail + wj * up_tail
        o_ref[...] = acc
        o_ref[ts - 8:ts, :] = acc_tail
        carry_ref[...] = d[0:8]

    return pl.pallas_call(
        body, name=name, grid=(c // tc, nt),
        in_specs=[pl.BlockSpec((ts, tc), lambda j, t: (nt - 1 - t, j)), pl.BlockSpec((SSD_CONV, tc), lambda j, t: (0, j))],
        out_specs=pl.BlockSpec((ts, tc), lambda j, t: (nt - 1 - t, j)),
        out_shape=jax.ShapeDtypeStruct((s, c), F32),
        scratch_shapes=[pltpu.VMEM((8, tc), F32)],
        compiler_params=_cp("parallel", "arbitrary"),
    )(dp, w)


def _merge_fwd(proj, ys, ym, *, name):
    r, w = ys.shape
    tr = _pick(r, 512, 8)
    off = OFF_GATES // w

    def body(g1_ref, g2_ref, ys_ref, ym_ref, o_ref):
        o_ref[...] = jax.nn.sigmoid(g1_ref[...]) * ys_ref[...] + jax.nn.sigmoid(g2_ref[...]) * ym_ref[...]

    blk = pl.BlockSpec((tr, w), lambda i: (i, 0))
    return pl.pallas_call(
        body, name=name, grid=(r // tr,),
        in_specs=[pl.BlockSpec((tr, w), lambda i: (i, off)), pl.BlockSpec((tr, w), lambda i: (i, off + 1)), blk, blk],
        out_specs=blk, out_shape=jax.ShapeDtypeStruct((r, w), F32),
        compiler_params=_cp("parallel"),
    )(proj, proj, ys, ym)


def _merge_bwd(proj, ys, ym, dm, *, name):
    r, w = ys.shape
    tr = _pick(r, 512, 8)
    off = OFF_GATES // w

    def body(g1_ref, g2_ref, ys_ref, ym_ref, dm_ref, dg_ref, dys_ref, dym_ref):
        s1, s2 = jax.nn.sigmoid(g1_ref[...]), jax.nn.sigmoid(g2_ref[...])
        dmv = dm_ref[...]
        dys_ref[...] = dmv * s1
        dym_ref[...] = dmv * s2
        dg_ref[:, :w] = dmv * ys_ref[...] * s1 * (1.0 - s1)
        dg_ref[:, w:] = dmv * ym_ref[...] * s2 * (1.0 - s2)

    blk = pl.BlockSpec((tr, w), lambda i: (i, 0))
    return pl.pallas_call(
        body, name=name, grid=(r // tr,),
        in_specs=[pl.BlockSpec((tr, w), lambda i: (i, off)), pl.BlockSpec((tr, w), lambda i: (i, off + 1)), blk, blk, blk],
        out_specs=[pl.BlockSpec((tr, 2 * w), lambda i: (i, 0)), blk, blk],
        out_shape=[jax.ShapeDtypeStruct((r, 2 * w), F32), jax.ShapeDtypeStruct((r, w), F32),
                   jax.ShapeDtypeStruct((r, w), F32)],
        compiler_params=_cp("parallel"),
    )(proj, proj, ys, ym, dm)


def _loss_fwd_bwd(y, target, *, name):
    r, w = y.shape
    tr = _pick(r, 512, 8)

    def body(y_ref, t_ref, l_ref, dy_ref):
        i = pl.program_id(0)
        e = y_ref[...] - t_ref[...]
        dy_ref[...] = e * (1.0 / w)
        part = jnp.sum(e * e, axis=0, keepdims=True) * (0.5 / w)

        @pl.when(i == 0)
        def _():
            l_ref[...] = part

        @pl.when(i > 0)
        def _():
            l_ref[...] += part

    blk = pl.BlockSpec((tr, w), lambda i: (i, 0))
    return pl.pallas_call(
        body, name=name, grid=(r // tr,),
        in_specs=[blk, blk],
        out_specs=[pl.BlockSpec((1, w), lambda i: (0, 0)), blk],
        out_shape=[jax.ShapeDtypeStruct((1, w), F32), jax.ShapeDtypeStruct((r, w), F32)],
        compiler_params=_cp("arbitrary"),
    )(y, target)


def _adamw(w, g, m, v, *, name):
    r, c = w.shape
    tr = _pick(r, 512, 8)
    c1 = 1.0 - ADAM_B1 ** ADAM_STEP
    c2 = 1.0 - ADAM_B2 ** ADAM_STEP

    def body(w_ref, g_ref, m_ref, v_ref, d_ref, nm_ref, nv_ref):
        gv = g_ref[...]
        nm = ADAM_B1 * m_ref[...] + (1.0 - ADAM_B1) * gv
        nv = ADAM_B2 * v_ref[...] + (1.0 - ADAM_B2) * (gv * gv)
        nm_ref[...] = nm
        nv_ref[...] = nv
        d_ref[...] = -ADAM_LR * ((nm / c1) / (jnp.sqrt(nv / c2) + ADAM_EPS) + ADAM_WD * w_ref[...])

    blk = pl.BlockSpec((tr, c), lambda i: (i, 0))
    sh = jax.ShapeDtypeStruct((r, c), F32)
    return pl.pallas_call(
        body, name=name, grid=(r // tr,),
        in_specs=[blk] * 4, out_specs=[blk] * 3, out_shape=[sh] * 3,
        compiler_params=_cp("parallel"),
    )(w, g, m, v)


def _softplus(x):
    return jnp.maximum(x, 0.0) + jnp.log(1.0 + jnp.exp(-jnp.abs(x)))


def _ssd_common(dtr_ref, dtrT_ref, dtb_ref, dtbT_ref, al_ref, alT_ref, e_ref):
    L = SSD_CHUNK
    ri = lax.broadcasted_iota(jnp.int32, (L, L), 0)
    cj = lax.broadcasted_iota(jnp.int32, (L, L), 1)
    tril = (ri >= cj).astype(F32)
    triu = (ri <= cj).astype(F32)
    a = -jnp.exp(al_ref[...])
    aT = -jnp.exp(alT_ref[...])
    pre = dtr_ref[...] + dtb_ref[...]
    preT = dtrT_ref[...] + dtbT_ref[...]
    dt = _softplus(pre)
    dtT = _softplus(preT)
    acum = jnp.dot(tril, dt * a, precision=HI, preferred_element_type=F32)
    acumT = jnp.dot(dtT * aT, triu, precision=HI, preferred_element_type=F32)
    e = e_ref[...]
    dt_x = jnp.dot(dt, e, precision=HI, preferred_element_type=F32)
    acum_x = jnp.dot(acum, e, precision=HI, preferred_element_type=F32)
    last_x = acum_x[L - 1:L, :]
    return dict(ri=ri, cj=cj, tril=tril, triu=triu, a=a, aT=aT, pre=pre, preT=preT, dt=dt, dtT=dtT,
                acum=acum, acumT=acumT, dt_x=dt_x, eacum_x=jnp.exp(acum_x), w_x=jnp.exp(last_x - acum_x),
                elast_x=jnp.exp(last_x))


def _dot_nt(a, b):
    return lax.dot_general(a, b, (((1,), (1,)), ((), ())), preferred_element_type=F32)


def _dot_tn(a, b):
    return lax.dot_general(a, b, (((0,), (0,)), ((), ())), preferred_element_type=F32)


def _dot(a, b):
    return jnp.dot(a, b, preferred_element_type=F32)


def _ssd_specs(nc, rev):
    L = SSD_CHUNK
    ix = (lambda c: nc - 1 - c) if rev else (lambda c: c)
    return [
        pl.BlockSpec((L, SSD_D_INNER), lambda c: (ix(c), 0)),
        pl.BlockSpec((L, 512), lambda c: (ix(c), 4)),
        pl.BlockSpec((L, 512), lambda c: (ix(c), 5)),
        pl.BlockSpec((L, SSD_HEADS), lambda c: (ix(c), 0)),
        pl.BlockSpec((SSD_HEADS, L), lambda c: (0, ix(c))),
        pl.BlockSpec((1, SSD_HEADS), lambda c: (0, 0)),
        pl.BlockSpec((SSD_HEADS, 1), lambda c: (0, 0)),
        pl.BlockSpec((1, SSD_HEADS), lambda c: (0, 0)),
        pl.BlockSpec((SSD_HEADS, 1), lambda c: (0, 0)),
        pl.BlockSpec((1, SSD_D_INNER), lambda c: (0, 0)),
        pl.BlockSpec((SSD_HEADS, SSD_D_INNER), lambda c: (0, 0)),
    ]


def _ssd_fwd(xc, dtr, dtrT, dtb, dtbT, alog, alogT, dskx, expand, *, name):
    s = xc.shape[0]
    L = SSD_CHUNK
    nc = s // L

    def body(x_ref, b_ref, c_ref, dtr_ref, dtrT_ref, dtb_ref, dtbT_ref, al_ref, alT_ref, dsk_ref, e_ref,
             y_ref, st_ref, state):
        ci = pl.program_id(0)

        @pl.when(ci == 0)
        def _():
            state[...] = jnp.zeros_like(state)

        st_ref[0] = state[...]
        q = _ssd_common(dtr_ref, dtrT_ref, dtb_ref, dtbT_ref, al_ref, alT_ref, e_ref)
        causal = q["ri"] >= q["cj"]
        lane_lo = q["cj"] < 64
        x = x_ref[...]
        xdt = x * q["dt_x"]
        xdt_b = xdt.astype(BF16)
        xdtw_b = (xdt * q["w_x"]).astype(BF16)
        for g in range(SSD_GROUPS):
            bg = b_ref[:, 128 * g:128 * g + 128]
            cg_b = c_ref[:, 128 * g:128 * g + 128].astype(BF16)
            cb = _dot_nt(cg_b, bg.astype(BF16))
            bgT_b = bg.T.astype(BF16)
            s0 = state[g]
            for jj in range(4):
                j = 4 * g + jj
                sl = slice(128 * j, 128 * j + 128)
                sls = slice(128 * jj, 128 * jj + 128)
                ms = []
                for h in (2 * j, 2 * j + 1):
                    seg = q["acum"][:, h:h + 1] - q["acumT"][h:h + 1, :]
                    decay = jnp.exp(jnp.where(causal, seg, -jnp.inf))
                    ms.append((cb * decay).astype(BF16))
                mcat = jnp.concatenate(ms, axis=1)
                xp = xdt_b[:, sl]
                zero = jnp.zeros_like(xp)
                xstack = jnp.concatenate([jnp.where(lane_lo, xp, zero), jnp.where(lane_lo, zero, xp)], axis=0)
                y = _dot(mcat, xstack)
                y = y + q["eacum_x"][:, sl] * _dot(cg_b, s0[:, sls].astype(BF16))
                y = y + x[:, sl] * dsk_ref[:, sl]
                y_ref[:, sl] = y
                state[g, :, sls] = s0[:, sls] * q["elast_x"][:, sl] + _dot(bgT_b, xdtw_b[:, sl])

    return pl.pallas_call(
        body, name=name, grid=(nc,),
        in_specs=_ssd_specs(nc, False),
        out_specs=[pl.BlockSpec((L, SSD_D_INNER), lambda c: (c, 0)),
                   pl.BlockSpec((1, SSD_GROUPS, SSD_STATE, 512), lambda c: (c, 0, 0, 0))],
        out_shape=[jax.ShapeDtypeStruct((s, SSD_D_INNER), F32),
                   jax.ShapeDtypeStruct((nc, SSD_GROUPS, SSD_STATE, 512), F32)],
        scratch_shapes=[pltpu.VMEM((SSD_GROUPS, SSD_STATE, 512), F32)],
        compiler_params=_cp("arbitrary"),
    )(xc, xc, xc, dtr, dtrT, dtb, dtbT, alog, alogT, dskx, expand)


def _ssd_bwd(xc, dtr, dtrT, dtb, dtbT, alog, alogT, dskx, expand, expandT, states, dy, *, name):
    s = xc.shape[0]
    L = SSD_CHUNK
    H = SSD_HEADS
    nc = s // L

    def body(x_ref, b_ref, c_ref, dtr_ref, dtrT_ref, dtb_ref, dtbT_ref, al_ref, alT_ref, dsk_ref, e_ref,
             et_ref, st_ref, dy_ref,
             dxc_ref, ddtc_ref, ddtr_ref, dbc_ref, dbr_ref, dac_ref, dar_ref, ddsk_ref, dstate):
        ci = pl.program_id(0)

        @pl.when(ci == 0)
        def _():
            dstate[...] = jnp.zeros_like(dstate)
            dbc_ref[...] = jnp.zeros_like(dbc_ref)
            dbr_ref[...] = jnp.zeros_like(dbr_ref)
            dac_ref[...] = jnp.zeros_like(dac_ref)
            dar_ref[...] = jnp.zeros_like(dar_ref)
            ddsk_ref[...] = jnp.zeros_like(ddsk_ref)

        q = _ssd_common(dtr_ref, dtrT_ref, dtb_ref, dtbT_ref, al_ref, alT_ref, e_ref)
        ri, cj = q["ri"], q["cj"]
        causal = ri >= cj
        causalT = ri <= cj
        lane_lo = cj < 64
        lane_h = lax.broadcasted_iota(jnp.int32, (1, H), 1)
        sub_h = lax.broadcasted_iota(jnp.int32, (H, 1), 0)
        x = x_ref[...]
        dyv = dy_ref[...]
        xdt = x * q["dt_x"]
        xdt_b = xdt.astype(BF16)
        xdtw = xdt * q["w_x"]
        xdtw_b = xdtw.astype(BF16)
        edy = q["eacum_x"] * dyv
        edy_b = edy.astype(BF16)
        dyv_b = dyv.astype(BF16)
        dacum_col = jnp.zeros((L, H), F32)
        dacum_row = jnp.zeros((H, L), F32)
        dxdt_t, yoff_t, u_t, r_t = [], [], [], []
        for g in range(SSD_GROUPS):
            bg = b_ref[:, 128 * g:128 * g + 128]
            cg = c_ref[:, 128 * g:128 * g + 128]
            bg_b, cg_b = bg.astype(BF16), cg.astype(BF16)
            cb = _dot_nt(cg_b, bg_b)
            cbT = _dot_nt(bg_b, cg_b)
            cgT_b = cg.T.astype(BF16)
            s0 = st_ref[0, g]
            ds = dstate[g]
            s0_b, ds_b = s0.astype(BF16), ds.astype(BF16)
            dcb = jnp.zeros((L, L), F32)
            for jj in range(4):
                j = 4 * g + jj
                sl = slice(128 * j, 128 * j + 128)
                sls = slice(128 * jj, 128 * jj + 128)
                decs, mts = [], []
                for h in (2 * j, 2 * j + 1):
                    seg = q["acum"][:, h:h + 1] - q["acumT"][h:h + 1, :]
                    decs.append(jnp.exp(jnp.where(causal, seg, -jnp.inf)))
                    mts.append((cbT * jnp.exp(jnp.where(causalT, -seg, -jnp.inf))).astype(BF16))
                dyt_b = dyv_b[:, sl]
                zero = jnp.zeros_like(dyt_b)
                dystack = jnp.concatenate([jnp.where(lane_lo, dyt_b, zero), jnp.where(lane_lo, zero, dyt_b)], axis=0)
                dxs = _dot(jnp.concatenate(mts, axis=0), dyt_b)
                dxdt = jnp.where(lane_lo, dxs[:L], dxs[L:])
                dmcat = _dot_nt(dystack, xdt_b[:, sl])
                for idx, h in enumerate((2 * j, 2 * j + 1)):
                    dm = dmcat[L * idx:L * idx + L]
                    dcb = dcb + dm * decs[idx]
                    dseg = dm * cb * decs[idx]
                    dacum_col = dacum_col + jnp.sum(dseg, axis=1, keepdims=True) * (lane_h == h).astype(F32)
                    dacum_row = dacum_row - (sub_h == h).astype(F32) * jnp.sum(dseg, axis=0, keepdims=True)
                gmat = _dot(cg_b, s0_b[:, sls])
                yoff_t.append(edy[:, sl] * gmat)
                qm = _dot(bg_b, ds_b[:, sls])
                dxdt_t.append(dxdt + qm * q["w_x"][:, sl])
                u_t.append(qm * xdtw[:, sl])
                r_t.append(ds[:, sls] * s0[:, sls] * q["elast_x"][:, sl])
                dstate[g, :, sls] = ds[:, sls] * q["elast_x"][:, sl] + _dot(cgT_b, edy_b[:, sl])
            gsl = slice(512 * g, 512 * g + 512)
            dcb_b = dcb.astype(BF16)
            dcg = _dot(dcb_b, bg_b) + _dot_nt(edy_b[:, gsl], s0_b)
            dbg = _dot(dcb.T.astype(BF16), cg_b) + _dot_nt(xdtw_b[:, gsl], ds_b)
            dxc_ref[:, SSD_D_INNER + 128 * g:SSD_D_INNER + 128 * g + 128] = dbg
            dxc_ref[:, SSD_D_INNER + 512 + 128 * g:SSD_D_INNER + 512 + 128 * g + 128] = dcg
        et = et_ref[...]
        dxdt_all = jnp.concatenate(dxdt_t, axis=1)
        yoff = jnp.concatenate(yoff_t, axis=1)
        uu = jnp.concatenate(u_t, axis=1)
        rr = jnp.concatenate(r_t, axis=1)
        dacum_col = dacum_col + jnp.dot(yoff - uu, et, precision=HI, preferred_element_type=F32)
        dlast = jnp.sum(jnp.dot(uu + rr, et, precision=HI, preferred_element_type=F32), axis=0, keepdims=True)
        row_lh = lax.broadcasted_iota(jnp.int32, (L, H), 0)
        dacum_col = dacum_col + jnp.where(row_lh == L - 1, dlast, 0.0)
        d_dta_col = jnp.dot(q["triu"], dacum_col, precision=HI, preferred_element_type=F32)
        d_dta_row = jnp.dot(dacum_row, q["tril"], precision=HI, preferred_element_type=F32)
        ddt_col = d_dta_col * q["a"] + jnp.dot(dxdt_all * x, et, precision=HI, preferred_element_type=F32)
        ddt_row = d_dta_row * q["aT"]
        ddtr_col = ddt_col * jax.nn.sigmoid(q["pre"])
        ddtr_row = ddt_row * jax.nn.sigmoid(q["preT"])
        ddtc_ref[...] = ddtr_col
        ddtr_ref[...] = ddtr_row
        dac_ref[...] += jnp.sum(d_dta_col * q["dt"], axis=0, keepdims=True)
        dar_ref[...] += jnp.sum(d_dta_row * q["dtT"], axis=1, keepdims=True)
        dbc_ref[...] += jnp.sum(ddtr_col, axis=0, keepdims=True)
        dbr_ref[...] += jnp.sum(ddtr_row, axis=1, keepdims=True)
        ddsk_ref[...] += jnp.sum(dyv * x, axis=0, keepdims=True)
        dxc_ref[:, 0:SSD_D_INNER] = dxdt_all * q["dt_x"] + dyv * dsk_ref[...]

    rv = lambda c: nc - 1 - c
    in_specs = _ssd_specs(nc, True) + [
        pl.BlockSpec((SSD_D_INNER, H), lambda c: (0, 0)),
        pl.BlockSpec((1, SSD_GROUPS, SSD_STATE, 512), lambda c: (rv(c), 0, 0, 0)),
        pl.BlockSpec((L, SSD_D_INNER), lambda c: (rv(c), 0)),
    ]
    vec_c = pl.BlockSpec((1, H), lambda c: (0, 0))
    vec_r = pl.BlockSpec((H, 1), lambda c: (0, 0))
    return pl.pallas_call(
        body, name=name, grid=(nc,),
        in_specs=in_specs,
        out_specs=[pl.BlockSpec((L, SSD_CONV_DIM), lambda c: (rv(c), 0)),
                   pl.BlockSpec((L, H), lambda c: (rv(c), 0)),
                   pl.BlockSpec((H, L), lambda c: (0, rv(c))),
                   vec_c, vec_r, vec_c, vec_r,
                   pl.BlockSpec((1, SSD_D_INNER), lambda c: (0, 0))],
        out_shape=[jax.ShapeDtypeStruct((s, SSD_CONV_DIM), F32),
                   jax.ShapeDtypeStruct((s, H), F32), jax.ShapeDtypeStruct((H, s), F32),
                   jax.ShapeDtypeStruct((1, H), F32), jax.ShapeDtypeStruct((H, 1), F32),
                   jax.ShapeDtypeStruct((1, H), F32), jax.ShapeDtypeStruct((H, 1), F32),
                   jax.ShapeDtypeStruct((1, SSD_D_INNER), F32)],
        scratch_shapes=[pltpu.VMEM((SSD_GROUPS, SSD_STATE, 512), F32)],
        compiler_params=_cp("arbitrary"),
    )(xc, xc, xc, dtr, dtrT, dtb, dtbT, alog, alogT, dskx, expand, expandT, states, dy)


QK_PAD = 256
MLA_TS = 256


def _rope_tables4(pos):
    inv = 1.0 / (ROPE_THETA ** (jnp.arange(0, MLA_ROPE, 2, dtype=F32) / MLA_ROPE))
    ang = pos.astype(F32)[:, None] * inv
    c, s = jnp.cos(ang), jnp.sin(ang)
    return jnp.tile(c, (1, 4)), jnp.concatenate([-s, s, -s, s], axis=1)


def _mla_gains(qg, kg):
    z = jnp.zeros((LANE - MLA_ROPE,), F32)
    return (qg[:MLA_NOPE][None], jnp.concatenate([qg[MLA_NOPE:], z])[None],
            kg[:MLA_NOPE][None], jnp.concatenate([kg[MLA_NOPE:], z])[None])


def _rope_swap(t, first):
    return jnp.where(first, pltpu.roll(t, 96, 1), pltpu.roll(t, 32, 1))


def _mla_prep_specs(ts):
    row = lambda w, c=0: pl.BlockSpec((ts, w), lambda i: (i, c))
    vec = pl.BlockSpec((1, LANE), lambda i: (0, 0))
    return [row(MLA_HEADS * MLA_QK), row(2 * MLA_HEADS * MLA_NOPE), row(LANE, OFF_KRDT // LANE), row(LANE), row(LANE),
            vec, vec, vec, vec]


def _mla_prep_fwd(qraw, kvraw, proj, cos4, sin4, gqn, gqr, gkn, gkr, *, name):
    s = qraw.shape[0]
    ts = _pick(s, MLA_TS, 8)

    def body(q_ref, kv_ref, kr_ref, cos_ref, sin_ref, gqn_ref, gqr_ref, gkn_ref, gkr_ref, qo_ref, ko_ref):
        lane = lax.broadcasted_iota(jnp.int32, (ts, LANE), 1)
        lo = lane < 64
        first = (lane % 64) < 32
        cos, sin = cos_ref[...], sin_ref[...]
        kr = jnp.where(lo, kr_ref[...], 0.0)
        ssq_kr = jnp.sum(kr * kr, axis=-1, keepdims=True)

        def head(xn, xr, ssq_r, gn, gr):
            rs = lax.rsqrt((jnp.sum(xn * xn, axis=-1, keepdims=True) + ssq_r) * (1.0 / MLA_QK) + EPS)
            yr = xr * rs * gr
            return xn * rs * gn, yr * cos + _rope_swap(yr, first) * sin

        for h in range(MLA_HEADS):
            tile = q_ref[:, MLA_HEADS * MLA_NOPE + LANE * (h // 2):MLA_HEADS * MLA_NOPE + LANE * (h // 2) + LANE]
            qr = jnp.where(lo, tile if h % 2 == 0 else pltpu.roll(tile, 64, 1), 0.0)
            on, orr = head(q_ref[:, LANE * h:LANE * h + LANE], qr, jnp.sum(qr * qr, axis=-1, keepdims=True),
                           gqn_ref[...], gqr_ref[...])
            qo_ref[h, :, 0:LANE] = on.astype(BF16)
            qo_ref[h, :, LANE:QK_PAD] = orr.astype(BF16)
            on, orr = head(kv_ref[:, LANE * h:LANE * h + LANE], kr, ssq_kr, gkn_ref[...], gkr_ref[...])
            ko_ref[h, :, 0:LANE] = on.astype(BF16)
            ko_ref[h, :, LANE:QK_PAD] = orr.astype(BF16)

    out = pl.BlockSpec((MLA_HEADS, ts, QK_PAD), lambda i: (0, i, 0))
    sh = jax.ShapeDtypeStruct((MLA_HEADS, s, QK_PAD), BF16)
    return pl.pallas_call(
        body, name=name, grid=(s // ts,),
        in_specs=_mla_prep_specs(ts), out_specs=[out, out], out_shape=[sh, sh],
        compiler_params=_cp("parallel"),
    )(qraw, kvraw, proj, cos4, sin4, gqn, gqr, gkn, gkr)


def _mla_prep_bwd(qraw, kvraw, proj, cos4, sin4, gqn, gqr, gkn, gkr, dq, dk, *, name):
    s = qraw.shape[0]
    ts = _pick(s, MLA_TS, 8)

    def body(q_ref, kv_ref, kr_ref, cos_ref, sin_ref, gqn_ref, gqr_ref, gkn_ref, gkr_ref, dq_ref, dk_ref,
             dqraw_ref, dkn_ref, dkr_ref, dgqn_ref, dgqr_ref, dgkn_ref, dgkr_ref):
        i = pl.program_id(0)

        @pl.when(i == 0)
        def _():
            for r in (dgqn_ref, dgqr_ref, dgkn_ref, dgkr_ref):
                r[...] = jnp.zeros_like(r)

        lane = lax.broadcasted_iota(jnp.int32, (ts, LANE), 1)
        lo = lane < 64
        first = (lane % 64) < 32
        cos, sin = cos_ref[...], sin_ref[...]
        kr = jnp.where(lo, kr_ref[...], 0.0)
        ssq_kr = jnp.sum(kr * kr, axis=-1, keepdims=True)

        def head(xn, xr, ssq_r, gn, gr, don, dor):
            rs = lax.rsqrt((jnp.sum(xn * xn, axis=-1, keepdims=True) + ssq_r) * (1.0 / MLA_QK) + EPS)
            xhn, xhr = xn * rs, xr * rs
            dor = jnp.where(lo, dor, 0.0)
            dyr = dor * cos + _rope_swap(dor * sin, first)
            dxn, dxr = don * gn, dyr * gr
            mm = (jnp.sum(dxn * xhn, axis=-1, keepdims=True) + jnp.sum(dxr * xhr, axis=-1, keepdims=True)) * (1.0 / MLA_QK)
            return (rs * (dxn - xhn * mm), rs * (dxr - xhr * mm),
                    jnp.sum(don * xhn, axis=0, keepdims=True), jnp.sum(dyr * xhr, axis=0, keepdims=True))

        dkr_acc = jnp.zeros((ts, LANE), F32)
        prev = None
        for h in range(MLA_HEADS):
            c0 = MLA_HEADS * MLA_NOPE + LANE * (h // 2)
            tile = q_ref[:, c0:c0 + LANE]
            qr = jnp.where(lo, tile if h % 2 == 0 else pltpu.roll(tile, 64, 1), 0.0)
            dn, dr, gn_p, gr_p = head(q_ref[:, LANE * h:LANE * h + LANE], qr, jnp.sum(qr * qr, axis=-1, keepdims=True),
                                      gqn_ref[...], gqr_ref[...], dq_ref[h, :, 0:LANE], dq_ref[h, :, LANE:QK_PAD])
            dqraw_ref[:, LANE * h:LANE * h + LANE] = dn
            dgqn_ref[...] += gn_p
            dgqr_ref[...] += gr_p
            if h % 2 == 0:
                prev = dr
            else:
                dqraw_ref[:, c0:c0 + LANE] = prev + pltpu.roll(dr, 64, 1)
            dn, dr, gn_p, gr_p = head(kv_ref[:, LANE * h:LANE * h + LANE], kr, ssq_kr, gkn_ref[...], gkr_ref[...],
                                      dk_ref[h, :, 0:LANE], dk_ref[h, :, LANE:QK_PAD])
            dkn_ref[:, LANE * h:LANE * h + LANE] = dn
            dkr_acc = dkr_acc + dr
            dgkn_ref[...] += gn_p
            dgkr_ref[...] += gr_p
        dkr_ref[...] = dkr_acc

    row = lambda w: pl.BlockSpec((ts, w), lambda i: (i, 0))
    vec = pl.BlockSpec((1, LANE), lambda i: (0, 0))
    dspec = pl.BlockSpec((MLA_HEADS, ts, QK_PAD), lambda i: (0, i, 0))
    vsh = jax.ShapeDtypeStruct((1, LANE), F32)
    return pl.pallas_call(
        body, name=name, grid=(s // ts,),
        in_specs=_mla_prep_specs(ts) + [dspec, dspec],
        out_specs=[row(MLA_HEADS * MLA_QK), row(MLA_HEADS * MLA_NOPE), row(LANE), vec, vec, vec, vec],
        out_shape=[jax.ShapeDtypeStruct((s, MLA_HEADS * MLA_QK), F32), jax.ShapeDtypeStruct((s, MLA_HEADS * MLA_NOPE), F32),
                   jax.ShapeDtypeStruct((s, LANE), F32), vsh, vsh, vsh, vsh],
        compiler_params=_cp("arbitrary"),
    )(qraw, kvraw, proj, cos4, sin4, gqn, gqr, gkn, gkr, dq, dk)


ATT_T = 512
ATT_SCALE = MLA_QK ** -0.5


def _attn_fwd(q, k, kvraw, *, name):
    nh, s, _ = q.shape
    t = _pick(s, ATT_T, LANE)
    nb = s // t

    def body(q_ref, k_ref, v_ref, o_ref, lse_ref, m_ref, l_ref, acc_ref):
        i, j = pl.program_id(1), pl.program_id(2)

        @pl.when(j == 0)
        def _():
            m_ref[...] = jnp.full_like(m_ref, -jnp.inf)
            l_ref[...] = jnp.zeros_like(l_ref)
            acc_ref[...] = jnp.zeros_like(acc_ref)

        @pl.when(j <= i)
        def _():
            sc = _dot_nt(q_ref[0], k_ref[0]) * ATT_SCALE
            ri = lax.broadcasted_iota(jnp.int32, (t, t), 0)
            cj = lax.broadcasted_iota(jnp.int32, (t, t), 1)
            sc = jnp.where((j < i) | (ri >= cj), sc, -jnp.inf)
            m_new = jnp.maximum(m_ref[...], jnp.max(sc, axis=-1, keepdims=True))
            alpha = jnp.exp(m_ref[...] - m_new)
            p = jnp.exp(sc - m_new)
            l_ref[...] = alpha * l_ref[...] + jnp.sum(p, axis=-1, keepdims=True)
            acc_ref[...] = alpha * acc_ref[...] + _dot(p.astype(BF16), v_ref[...].astype(BF16))
            m_ref[...] = m_new

        @pl.when(j == i)
        def _():
            o_ref[...] = acc_ref[...] / l_ref[...]
            lse_ref[0] = m_ref[...] + jnp.log(l_ref[...])

    return pl.pallas_call(
        body, name=name, grid=(nh, nb, nb),
        in_specs=[pl.BlockSpec((1, t, QK_PAD), lambda h, i, j: (h, i, 0)),
                  pl.BlockSpec((1, t, QK_PAD), lambda h, i, j: (h, jnp.minimum(j, i), 0)),
                  pl.BlockSpec((t, MLA_V), lambda h, i, j: (jnp.minimum(j, i), nh + h))],
        out_specs=[pl.BlockSpec((t, MLA_V), lambda h, i, j: (i, h)),
                   pl.BlockSpec((1, t, 1), lambda h, i, j: (h, i, 0))],
        out_shape=[jax.ShapeDtypeStruct((s, nh * MLA_V), F32), jax.ShapeDtypeStruct((nh, s, 1), F32)],
        scratch_shapes=[pltpu.VMEM((t, 1), F32), pltpu.VMEM((t, 1), F32), pltpu.VMEM((t, MLA_V), F32)],
        compiler_params=_cp("parallel", "parallel", "arbitrary"),
    )(q, k, kvraw)


def _attn_bwd(q, k, kvraw, o, lse, do, *, name):
    nh, s, _ = q.shape
    t = _pick(s, ATT_T, LANE)
    nb = s // t

    def body(q_ref, k_ref, v_ref, o_ref, lse_ref, do_ref, dq_ref, dk_ref, dv_ref, dk_acc, dv_acc):
        j, i = pl.program_id(1), pl.program_id(2)

        @pl.when(i == 0)
        def _():
            dk_acc[...] = jnp.zeros_like(dk_acc)
            dv_acc[...] = jnp.zeros_like(dv_acc)

        @pl.when(i >= j)
        def _():
            qv, kv = q_ref[0], k_ref[0]
            sc = _dot_nt(qv, kv) * ATT_SCALE
            ri = lax.broadcasted_iota(jnp.int32, (t, t), 0)
            cj = lax.broadcasted_iota(jnp.int32, (t, t), 1)
            sc = jnp.where((j < i) | (ri >= cj), sc, -jnp.inf)
            p = jnp.exp(sc - lse_ref[0])
            dov = do_ref[...]
            delta = jnp.sum(dov * o_ref[...], axis=-1, keepdims=True)
            do_b = dov.astype(BF16)
            dv_acc[...] += _dot_tn(p.astype(BF16), do_b)
            dp = _dot_nt(do_b, v_ref[...].astype(BF16))
            ds_b = (p * (dp - delta) * ATT_SCALE).astype(BF16)
            dk_acc[...] += _dot_tn(ds_b, qv)
            dq_part = _dot(ds_b, kv)
            rows = pl.ds(pl.multiple_of(i * t, t), t)

            @pl.when(j == 0)
            def _():
                dq_ref[0, rows, :] = dq_part

            @pl.when(j > 0)
            def _():
                dq_ref[0, rows, :] += dq_part

        @pl.when(i == nb - 1)
        def _():
            dk_ref[0] = dk_acc[...]
            dv_ref[...] = dv_acc[...]

    qi = lambda h, j, i: jnp.maximum(i, j)
    return pl.pallas_call(
        body, name=name, grid=(nh, nb, nb),
        in_specs=[pl.BlockSpec((1, t, QK_PAD), lambda h, j, i: (h, qi(h, j, i), 0)),
                  pl.BlockSpec((1, t, QK_PAD), lambda h, j, i: (h, j, 0)),
                  pl.BlockSpec((t, MLA_V), lambda h, j, i: (j, nh + h)),
                  pl.BlockSpec((t, MLA_V), lambda h, j, i: (qi(h, j, i), h)),
                  pl.BlockSpec((1, t, 1), lambda h, j, i: (h, qi(h, j, i), 0)),
                  pl.BlockSpec((t, MLA_V), lambda h, j, i: (qi(h, j, i), h))],
        out_specs=[pl.BlockSpec((1, s, QK_PAD), lambda h, j, i: (h, 0, 0)),
                   pl.BlockSpec((1, t, QK_PAD), lambda h, j, i: (h, j, 0)),
                   pl.BlockSpec((t, MLA_V), lambda h, j, i: (j, h))],
        out_shape=[jax.ShapeDtypeStruct((nh, s, QK_PAD), F32), jax.ShapeDtypeStruct((nh, s, QK_PAD), F32),
                   jax.ShapeDtypeStruct((s, nh * MLA_V), F32)],
        scratch_shapes=[pltpu.VMEM((t, QK_PAD), F32), pltpu.VMEM((t, MLA_V), F32)],
        compiler_params=_cp("parallel", "arbitrary", "arbitrary"),
    )(q, k, kvraw, o, lse, do)


def _ffn_fwd(h, w, tag):
    n = _rms_fwd(h, w["ln"], name=tag + "_norm")
    gu = _matmul(n, w["w13"], "nn", name=tag + "_up")
    act = _swiglu_fwd(gu, name=tag + "_act")
    out = _matmul(act, w["w2"], "nn", name=tag + "_down", scale=0.5, res=h)
    return out, (h, n, gu, act)


def _ffn_bwd(dout, saved, w, tag):
    h, n, gu, act = saved
    dact = _matmul(dout, w["w2"], "nt", name=tag + "_down_dx", scale=0.5)
    dw2 = _matmul(act, dout, "tn", name=tag + "_down_dw", scale=0.5)
    dgu = _swiglu_bwd(gu, dact, name=tag + "_act_bwd")
    dw13 = _matmul(n, dgu, "tn", name=tag + "_up_dw")
    dn = _matmul(dgu, w["w13"], "nt", name=tag + "_up_dx")
    dh, dln = _rms_bwd(h, w["ln"], dn, name=tag + "_norm_bwd", res=dout)
    return dh, dict(ln=dln, w13=dw13, w2=dw2)


def _mixer_fwd(h, w, rope, tag):
    cos4, sin4 = rope
    u = _rms_fwd(h, w["ln_mix"], name=tag + "_norm")
    proj = _matmul(u, w["w_in"], "nn", name=tag + "_in")
    xc = _conv_fwd(proj, w["conv_w"], w["conv_b"], name=tag + "_conv")
    dtr = proj[:, OFF_KRDT + MLA_ROPE:OFF_KRDT + MLA_ROPE + SSD_HEADS]
    dtrT = dtr.T
    y, states = _ssd_fwd(xc, dtr, dtrT, *w["ssd_aux"], name=tag + "_ssd")
    yn = _gated_rms_fwd(y, proj, w["ssd_norm"], name=tag + "_ssd_norm")
    y_ssd = _matmul(yn, w["w_ssd_out"], "nn", name=tag + "_ssd_out")
    cqn = _rms_fwd(proj, w["q_lora_norm"], name=tag + "_q_lora_norm", col=OFF_CQ // MLA_Q_LORA, width=MLA_Q_LORA)
    qraw = _matmul(cqn, w["w_uq"], "nn", name=tag + "_uq")
    ckvn = _rms_fwd(proj, w["kv_lora_norm"], name=tag + "_kv_lora_norm", col=OFF_CKV // MLA_KV_LORA, width=MLA_KV_LORA)
    kvraw = _matmul(ckvn, w["w_ukv"], "nn", name=tag + "_ukv")
    qf, kf = _mla_prep_fwd(qraw, kvraw, proj, cos4, sin4, *w["qk_gains"], name=tag + "_qk_prep")
    o, lse = _attn_fwd(qf, kf, kvraw, name=tag + "_attn")
    y_mla = _matmul(o, w["w_mla_out"], "nn", name=tag + "_mla_out")
    merged = _merge_fwd(proj, y_ssd, y_mla, name=tag + "_merge")
    out = _matmul(merged, w["w_o"], "nn", name=tag + "_o", res=h)
    saved = dict(h=h, u=u, proj=proj, xc=xc, dtr=dtr, dtrT=dtrT, states=states, y=y, yn=yn, y_ssd=y_ssd, cqn=cqn,
                 qraw=qraw, ckvn=ckvn, kvraw=kvraw, qf=qf, kf=kf, o=o, lse=lse, y_mla=y_mla, merged=merged)
    return out, saved


def _mixer_bwd(dout, s, w, rope, tag):
    cos4, sin4 = rope
    g = {}
    proj = s["proj"]
    dmerged = _matmul(dout, w["w_o"], "nt", name=tag + "_o_dx")
    g["w_o"] = _matmul(s["merged"], dout, "tn", name=tag + "_o_dw")
    dgates, dy_ssd, dy_mla = _merge_bwd(proj, s["y_ssd"], s["y_mla"], dmerged, name=tag + "_merge_bwd")
    do = _matmul(dy_mla, w["w_mla_out"], "nt", name=tag + "_mla_out_dx")
    g["w_mla_out"] = _matmul(s["o"], dy_mla, "tn", name=tag + "_mla_out_dw")
    dqf, dkf, dv = _attn_bwd(s["qf"], s["kf"], s["kvraw"], s["o"], s["lse"], do, name=tag + "_attn_bwd")
    dqraw, dkn, dkrt, dgqn, dgqr, dgkn, dgkr = _mla_prep_bwd(
        s["qraw"], s["kvraw"], proj, cos4, sin4, *w["qk_gains"], dqf, dkf, name=tag + "_qk_prep_bwd")
    g["q_norm"] = jnp.concatenate([dgqn[0], dgqr[0, :MLA_ROPE]])
    g["k_norm"] = jnp.concatenate([dgkn[0], dgkr[0, :MLA_ROPE]])
    dkvraw = jnp.concatenate([dkn, dv], axis=1)
    dcqn = _matmul(dqraw, w["w_uq"], "nt", name=tag + "_uq_dx")
    g["w_uq"] = _matmul(s["cqn"], dqraw, "tn", name=tag + "_uq_dw")
    dckvn = _matmul(dkvraw, w["w_ukv"], "nt", name=tag + "_ukv_dx")
    g["w_ukv"] = _matmul(s["ckvn"], dkvraw, "tn", name=tag + "_ukv_dw")
    dcq, g["q_lora_norm"] = _rms_bwd(proj, w["q_lora_norm"], dcqn, name=tag + "_q_lora_norm_bwd",
                                     col=OFF_CQ // MLA_Q_LORA, width=MLA_Q_LORA)
    dckv, g["kv_lora_norm"] = _rms_bwd(proj, w["kv_lora_norm"], dckvn, name=tag + "_kv_lora_norm_bwd",
                                       col=OFF_CKV // MLA_KV_LORA, width=MLA_KV_LORA)
    dyn = _matmul(dy_ssd, w["w_ssd_out"], "nt", name=tag + "_ssd_out_dx")
    g["w_ssd_out"] = _matmul(s["yn"], dy_ssd, "tn", name=tag + "_ssd_out_dw")
    dy, dz, g["ssd_norm"] = _gated_rms_bwd(s["y"], proj, w["ssd_norm"], dyn, name=tag + "_ssd_norm_bwd")
    aux = w["ssd_aux"]
    dxc, ddt_c, ddt_r, dbias_c, dbias_r, da_c, da_r, ddsk = _ssd_bwd(
        s["xc"], s["dtr"], s["dtrT"], *aux, aux[-1].T, s["states"], dy, name=tag + "_ssd_bwd")
    g["dt_bias"] = dbias_c[0] + dbias_r[:, 0]
    g["a_log"] = (da_c[0] + da_r[:, 0]) * (-jnp.exp(aux[2][0]))
    g["d_skip"] = jnp.sum(ddsk.reshape(SSD_HEADS, SSD_HEAD_DIM), axis=1)
    dpre, g["conv_w"], g["conv_b"] = _conv_bwd_pre(proj, w["conv_w"], w["conv_b"], dxc, name=tag + "_conv_bwd_pre")
    dxbc = _conv_bwd_x(dpre, w["conv_w"], name=tag + "_conv_bwd_x")
    ddtr = ddt_c + ddt_r.T
    dkrdt = jnp.concatenate([dkrt[:, :MLA_ROPE], ddtr, jnp.zeros((ddtr.shape[0], LANE - MLA_ROPE - SSD_HEADS), F32)], axis=1)
    dproj = jnp.concatenate([dz, dxbc, dgates, dcq, dckv, dkrdt], axis=1)
    du = _matmul(dproj, w["w_in"], "nt", name=tag + "_in_dx")
    g["w_in"] = _matmul(s["u"], dproj, "tn", name=tag + "_in_dw")
    dh, g["ln_mix"] = _rms_bwd(s["h"], w["ln_mix"], du, name=tag + "_norm_bwd", res=dout)
    return dh, g


W_NAMES = ["ln_ffn1", "ffn1_w13", "ffn1_w2", "ln_mix", "w_in", "conv_w", "conv_b", "dt_bias", "a_log", "d_skip",
           "ssd_norm", "w_ssd_out", "q_lora_norm", "w_uq", "kv_lora_norm", "w_ukv", "q_norm", "k_norm", "w_mla_out",
           "w_o", "ln_ffn2", "ffn2_w13", "ffn2_w2"]
SHARD_AXIS = {"ffn1_w13": 2, "ffn1_w2": 1, "w_in": 2, "conv_w": 2, "w_ssd_out": 1, "w_uq": 2, "w_ukv": 2,
              "w_mla_out": 1, "w_o": 1, "ffn2_w13": 2, "ffn2_w2": 1}
SHARDED = [n for n in W_NAMES if n in SHARD_AXIS]
REPLICATED = [n for n in W_NAMES if n not in SHARD_AXIS]
N_CHIPS = 4
N_DEV = 8
PACK_COLS = 1024
IN_SPLIT = (2048, 3072, 32, 512, 256, 64, 2048)


def _pack(arrs, rows, dtype):
    flat = jnp.concatenate([a.astype(dtype).reshape(-1) for a in arrs])
    return jnp.pad(flat, (0, rows * PACK_COLS - flat.shape[0])).reshape(rows, PACK_COLS)


def _unpack(packed, shapes):
    flat = packed.reshape(-1)
    out, at = [], 0
    for sh in shapes:
        n = math.prod(sh)
        out.append(flat[at:at + n].reshape(sh))
        at += n
    return out


def _pack_rows(shapes):
    n = sum(math.prod(sh) for sh in shapes)
    return -(-n // (PACK_COLS * 1024)) * 1024


def _in_perm(w_in):
    z, xbc, dt, cq, ckv, kr, gates = jnp.split(w_in, list(np_cumsum(IN_SPLIT))[:-1], axis=1)
    return jnp.concatenate([z, xbc, gates, cq, ckv, kr, dt, jnp.zeros((w_in.shape[0], PROJ_W - sum(IN_SPLIT)), w_in.dtype)], axis=1)


def _in_unperm(g):
    z, xbc, gates, cq, ckv = (g[:, OFF_Z:OFF_XBC], g[:, OFF_XBC:OFF_GATES], g[:, OFF_GATES:OFF_CQ], g[:, OFF_CQ:OFF_CKV],
                              g[:, OFF_CKV:OFF_KRDT])
    kr = g[:, OFF_KRDT:OFF_KRDT + MLA_ROPE]
    dt = g[:, OFF_KRDT + MLA_ROPE:OFF_KRDT + MLA_ROPE + SSD_HEADS]
    return jnp.concatenate([z, xbc, dt, cq, ckv, kr, gates], axis=1)


def np_cumsum(sizes):
    out, t = [], 0
    for s in sizes:
        t += s
        out.append(t)
    return out


def _head_perm(w, first):
    r = w.shape[0]
    w3 = w.reshape(r, MLA_HEADS, -1)
    return jnp.concatenate([w3[:, :, :first].reshape(r, -1), w3[:, :, first:].reshape(r, -1)], axis=1)


def _head_unperm(g, first):
    r = g.shape[0]
    rest = g.shape[1] // MLA_HEADS - first
    a = g[:, :MLA_HEADS * first].reshape(r, MLA_HEADS, first)
    b = g[:, MLA_HEADS * first:].reshape(r, MLA_HEADS, rest)
    return jnp.concatenate([a, b], axis=2).reshape(r, -1)


def _layer_weights(full, l):
    row = lambda n: full[n][l][None].astype(F32)
    expand = jnp.repeat(jnp.eye(SSD_HEADS, dtype=F32), SSD_HEAD_DIM, axis=1)
    dtb, al, dsk = full["dt_bias"][l], full["a_log"][l], full["d_skip"][l]
    mixer = dict(
        ln_mix=row("ln_mix"), w_in=_in_perm(full["w_in"][l]), conv_w=full["conv_w"][l], conv_b=row("conv_b"),
        ssd_aux=(dtb[None], dtb[:, None], al[None], al[:, None], jnp.repeat(dsk, SSD_HEAD_DIM)[None], expand),
        ssd_norm=row("ssd_norm"), w_ssd_out=full["w_ssd_out"][l],
        q_lora_norm=row("q_lora_norm"), w_uq=_head_perm(full["w_uq"][l], MLA_NOPE),
        kv_lora_norm=row("kv_lora_norm"), w_ukv=_head_perm(full["w_ukv"][l], MLA_NOPE),
        qk_gains=_mla_gains(full["q_norm"][l], full["k_norm"][l]),
        w_mla_out=full["w_mla_out"][l], w_o=full["w_o"][l])
    ffn1 = dict(ln=row("ln_ffn1"), w13=full["ffn1_w13"][l], w2=full["ffn1_w2"][l])
    ffn2 = dict(ln=row("ln_ffn2"), w13=full["ffn2_w13"][l], w2=full["ffn2_w2"][l])
    return ffn1, mixer, ffn2


def _layer_grads(g1, gm, g2):
    return {
        "ln_ffn1": g1["ln"][0], "ffn1_w13": g1["w13"], "ffn1_w2": g1["w2"],
        "ln_mix": gm["ln_mix"][0], "w_in": _in_unperm(gm["w_in"]), "conv_w": gm["conv_w"], "conv_b": gm["conv_b"][0],
        "dt_bias": gm["dt_bias"], "a_log": gm["a_log"], "d_skip": gm["d_skip"], "ssd_norm": gm["ssd_norm"][0],
        "w_ssd_out": gm["w_ssd_out"], "q_lora_norm": gm["q_lora_norm"][0], "w_uq": _head_unperm(gm["w_uq"], MLA_NOPE),
        "kv_lora_norm": gm["kv_lora_norm"][0], "w_ukv": _head_unperm(gm["w_ukv"], MLA_NOPE),
        "q_norm": gm["q_norm"], "k_norm": gm["k_norm"], "w_mla_out": gm["w_mla_out"], "w_o": gm["w_o"],
        "ln_ffn2": g2["ln"][0], "ffn2_w13": g2["w13"], "ffn2_w2": g2["w2"],
    }


def _local_step(x, positions, loss_target, full):
    rope = _rope_tables4(positions)
    lw = [_layer_weights(full, l) for l in range(DEPTH)]
    h = x
    saved = []
    for l in range(DEPTH):
        f1, mx, f2 = lw[l]
        h, s1 = _ffn_fwd(h, f1, f"l{l}_ffn1")
        h, sm = _mixer_fwd(h, mx, rope, f"l{l}_mix")
        h, s2 = _ffn_fwd(h, f2, f"l{l}_ffn2")
        saved.append((s1, sm, s2))
    loss_part, dh = _loss_fwd_bwd(h, loss_target, name="loss")
    grads = [None] * DEPTH
    for l in reversed(range(DEPTH)):
        f1, mx, f2 = lw[l]
        s1, sm, s2 = saved[l]
        dh, g2 = _ffn_bwd(dh, s2, f2, f"l{l}_ffn2")
        dh, gm = _mixer_bwd(dh, sm, mx, rope, f"l{l}_mix")
        dh, g1 = _ffn_bwd(dh, s1, f1, f"l{l}_ffn1")
        grads[l] = _layer_grads(g1, gm, g2)
    full_grads = {n: jnp.stack([grads[l][n] for l in range(DEPTH)]) for n in W_NAMES}
    return loss_part, dh, full_grads


MESH = pl.DeviceIdType.MESH
ANY = pl.BlockSpec(memory_space=pl.ANY)


def _place():
    return lax.axis_index("x"), lax.axis_index("y"), lax.axis_index("c")


def _other_chips(x, y):
    return [(1 - x, y), (x, 1 - y), (1 - x, 1 - y)]


def _remote(src, dst, send_sems, recv_sems, k, to):
    return pltpu.make_async_remote_copy(src_ref=src, dst_ref=dst, send_sem=send_sems.at[k], recv_sem=recv_sems.at[k],
                                        device_id=to, device_id_type=MESH)


def _gather_shards(packed, *, name):
    r, ncol = packed.shape
    hr = r // 2

    def body(x_ref, out_ref, send_sems, recv_sems, local_sem):
        x, y, c = _place()
        chips = _other_chips(x, y)

        def half(chip, cc):
            return out_ref.at[2 * chip[0] + chip[1], pl.ds(pl.multiple_of(cc * hr, 16), hr), :]

        mine = pltpu.make_async_copy(x_ref, out_ref.at[2 * x + y], local_sem.at[0])
        mine.start()
        first = [_remote(x_ref.at[pl.ds(pl.multiple_of(c * hr, 16), hr), :], half((x, y), c), send_sems, recv_sems, j,
                         (*chip, c)) for j, chip in enumerate(chips)]
        for cp in first:
            cp.start()
        passed = [_remote(half(chip, c), half(chip, c), send_sems, recv_sems, 3 + j, (x, y, 1 - c))
                  for j, chip in enumerate(chips)]
        for j, chip in enumerate(chips):
            _remote(half(chip, c), half(chip, c), send_sems, recv_sems, j, (x, y, c)).wait_recv()
            passed[j].start()
        for j, chip in enumerate(chips):
            _remote(half(chip, 1 - c), half(chip, 1 - c), send_sems, recv_sems, 3 + j, (x, y, c)).wait_recv()
        for cp in first + passed:
            cp.wait_send()
        mine.wait()

    return pl.pallas_call(
        body, name=name,
        out_shape=jax.ShapeDtypeStruct((N_CHIPS, r, ncol), packed.dtype),
        in_specs=[ANY], out_specs=ANY,
        scratch_shapes=[pltpu.SemaphoreType.DMA((6,)), pltpu.SemaphoreType.DMA((6,)), pltpu.SemaphoreType.DMA((1,))],
    )(packed)


def _swap_halves(g, *, name):
    n, r, ncol = g.shape
    hr = r // 2

    def body(g_ref, own_ref, got_ref, send_sems, recv_sems, local_sem):
        x, y, c = _place()
        keep = pltpu.make_async_copy(g_ref.at[:, pl.ds(pl.multiple_of(c * hr, 8), hr), :], own_ref, local_sem.at[0])
        keep.start()
        cp = _remote(g_ref.at[:, pl.ds(pl.multiple_of((1 - c) * hr, 8), hr), :], got_ref, send_sems, recv_sems, 0,
                     (x, y, 1 - c))
        cp.start()
        cp.wait()
        keep.wait()

    sh = jax.ShapeDtypeStruct((n, hr, ncol), g.dtype)
    return pl.pallas_call(
        body, name=name, out_shape=[sh, sh], in_specs=[ANY], out_specs=[ANY, ANY],
        scratch_shapes=[pltpu.SemaphoreType.DMA((1,)), pltpu.SemaphoreType.DMA((1,)), pltpu.SemaphoreType.DMA((1,))],
    )(g)


def _scatter_to_chips(a, *, name):
    n, r, ncol = a.shape

    def body(a_ref, mine_ref, got_ref, send_sems, recv_sems, local_sem):
        x, y, c = _place()
        keep = pltpu.make_async_copy(a_ref.at[2 * x + y], mine_ref, local_sem.at[0])
        keep.start()
        cps = [_remote(a_ref.at[2 * chip[0] + chip[1]], got_ref.at[j], send_sems, recv_sems, j, (*chip, c))
               for j, chip in enumerate(_other_chips(x, y))]
        for cp in cps:
            cp.start()
        for cp in cps:
            cp.wait()
        keep.wait()

    return pl.pallas_call(
        body, name=name,
        out_shape=[jax.ShapeDtypeStruct((r, ncol), a.dtype), jax.ShapeDtypeStruct((n - 1, r, ncol), a.dtype)],
        in_specs=[ANY], out_specs=[ANY, ANY],
        scratch_shapes=[pltpu.SemaphoreType.DMA((3,)), pltpu.SemaphoreType.DMA((3,)), pltpu.SemaphoreType.DMA((1,))],
    )(a)


def _join_halves(b, *, name):
    hr, ncol = b.shape

    def body(b_ref, out_ref, send_sems, recv_sems, local_sem):
        x, y, c = _place()
        mine = out_ref.at[pl.ds(pl.multiple_of(c * hr, 8), hr), :]
        theirs = out_ref.at[pl.ds(pl.multiple_of((1 - c) * hr, 8), hr), :]
        keep = pltpu.make_async_copy(b_ref, mine, local_sem.at[0])
        keep.start()
        cp = _remote(b_ref, mine, send_sems, recv_sems, 0, (x, y, 1 - c))
        cp.start()
        cp.wait_send()
        _remote(b_ref, theirs, send_sems, recv_sems, 0, (x, y, c)).wait_recv()
        keep.wait()

    return pl.pallas_call(
        body, name=name, out_shape=jax.ShapeDtypeStruct((2 * hr, ncol), b.dtype), in_specs=[ANY], out_specs=ANY,
        scratch_shapes=[pltpu.SemaphoreType.DMA((1,)), pltpu.SemaphoreType.DMA((1,)), pltpu.SemaphoreType.DMA((1,))],
    )(b)


def _sum_rows(arrs, *, name):
    r, ncol = arrs[0].shape
    tr = _pick(r, 512, 8)

    def body(*refs):
        acc = refs[0][...]
        for ref in refs[1:-1]:
            acc = acc + ref[...]
        refs[-1][...] = acc

    blk = pl.BlockSpec((tr, ncol), lambda i: (i, 0))
    return pl.pallas_call(
        body, name=name, grid=(r // tr,), in_specs=[blk] * len(arrs), out_specs=blk,
        out_shape=jax.ShapeDtypeStruct((r, ncol), arrs[0].dtype), compiler_params=_cp("parallel"),
    )(*arrs)


def _reduce_scatter(g, *, name):
    n, r, ncol = g.shape
    hr = r // 2
    own, got = _swap_halves(g, name=name + "_swap")
    chip_sum = _sum_rows([own.reshape(n * hr, ncol), got.reshape(n * hr, ncol)], name=name + "_add_cores")
    mine, others = _scatter_to_chips(chip_sum.reshape(n, hr, ncol), name=name + "_scatter")
    total = _sum_rows([mine, others[0], others[1], others[2]], name=name + "_add_chips")
    return _join_halves(total, name=name + "_join")


def _all_gather_small(v, *, name):
    r, ncol = v.shape

    def body(x_ref, out_ref, send_sems, recv_sems, local_sem):
        x, y, c = _place()
        me, sibling = (x, y, c), (x, y, 1 - c)
        chips = _other_chips(x, y)

        def slot(p):
            return out_ref.at[4 * p[0] + 2 * p[1] + p[2]]

        mine = pltpu.make_async_copy(x_ref, slot(me), local_sem.at[0])
        mine.start()
        first = [_remote(x_ref, slot(me), send_sems, recv_sems, 0, sibling)]
        first += [_remote(x_ref, slot(me), send_sems, recv_sems, 1 + j, (*chip, c)) for j, chip in enumerate(chips)]
        for cp in first:
            cp.start()
        passed = [_remote(slot((*chip, c)), slot((*chip, c)), send_sems, recv_sems, 4 + j, sibling)
                  for j, chip in enumerate(chips)]
        for j, chip in enumerate(chips):
            _remote(slot((*chip, c)), slot((*chip, c)), send_sems, recv_sems, 1 + j, me).wait_recv()
            passed[j].start()
        _remote(slot(sibling), slot(sibling), send_sems, recv_sems, 0, me).wait_recv()
        for j, chip in enumerate(chips):
            _remote(slot((*chip, 1 - c)), slot((*chip, 1 - c)), send_sems, recv_sems, 4 + j, me).wait_recv()
        for cp in first + passed:
            cp.wait_send()
        mine.wait()

    vm = pl.BlockSpec(memory_space=pltpu.VMEM)
    return pl.pallas_call(
        body, name=name, out_shape=jax.ShapeDtypeStruct((N_DEV, r, ncol), v.dtype), in_specs=[vm], out_specs=vm,
        scratch_shapes=[pltpu.SemaphoreType.DMA((7,)), pltpu.SemaphoreType.DMA((7,)), pltpu.SemaphoreType.DMA((1,))],
    )(v)


def _sum_slots(g8, *, name):
    n, r, ncol = g8.shape

    def body(g_ref, o_ref):
        acc = g_ref[0]
        for k in range(1, n):
            acc = acc + g_ref[k]
        o_ref[...] = acc

    return pl.pallas_call(body, name=name, out_shape=jax.ShapeDtypeStruct((r, ncol), g8.dtype))(g8)


def _step(a):
    x = a["x"][0]
    s = x.shape[0]
    del s
    shard_shapes = [a[n].shape for n in SHARDED]
    rows = _pack_rows(shard_shapes)

    gathered = _gather_shards(_pack([a[n] for n in SHARDED], rows, BF16), name="gather_weights")
    conv_rows = -(-math.prod(a["conv_w"].shape) // (LANE * 8)) * 8
    conv_all = _all_gather_small(
        jnp.pad(a["conv_w"].reshape(-1), (0, conv_rows * LANE - math.prod(a["conv_w"].shape))).reshape(conv_rows, LANE),
        name="gather_conv_w")
    per_chip = [dict(zip(SHARDED, _unpack(gathered[k], shard_shapes))) for k in range(N_CHIPS)]
    full = {n: jnp.concatenate([per_chip[k][n] for k in range(N_CHIPS)], axis=SHARD_AXIS[n]) for n in SHARDED}
    full["conv_w"] = jnp.concatenate(
        [conv_all[2 * k].reshape(-1)[:math.prod(a["conv_w"].shape)].reshape(a["conv_w"].shape) for k in range(N_CHIPS)],
        axis=SHARD_AXIS["conv_w"])
    for n in REPLICATED:
        full[n] = a[n]

    loss_part, grad_x, grads = _local_step(x, a["positions"][0], a["loss_target"][0], full)
    loss = lax.psum(jnp.sum(loss_part), ("x", "y", "c"))

    slots = []
    for k in range(N_CHIPS):
        parts = [jnp.split(grads[n], N_CHIPS, axis=SHARD_AXIS[n])[k] for n in SHARDED]
        slots.append(_pack(parts, rows, F32))
    g_shard = _reduce_scatter(jnp.stack(slots), name="reduce_grads")

    rep_shapes = [a[n].shape for n in REPLICATED]
    n_rep = sum(math.prod(sh) for sh in rep_shapes)
    rep_rows = -(-n_rep // (LANE * 8)) * 8
    pack_small = lambda arrs: jnp.pad(jnp.concatenate([t.reshape(-1) for t in arrs]), (0, rep_rows * LANE - n_rep)).reshape(rep_rows, LANE)
    g_rep = _sum_slots(_all_gather_small(pack_small([grads[n] for n in REPLICATED]), name="gather_small_grads"),
                       name="add_small_grads")

    out = {"loss": loss, "grad_x": grad_x[None]}
    d_sh, m_sh, v_sh = _adamw(_pack([a[n] for n in SHARDED], rows, F32), g_shard,
                              _pack([a["m_" + n] for n in SHARDED], rows, F32),
                              _pack([a["v_" + n] for n in SHARDED], rows, F32), name="adamw_sharded")
    d_rp, m_rp, v_rp = _adamw(pack_small([a[n] for n in REPLICATED]), g_rep,
                              pack_small([a["m_" + n] for n in REPLICATED]),
                              pack_small([a["v_" + n] for n in REPLICATED]), name="adamw_replicated")
    for prefix, sh_arr, rp_arr in (("grad_", g_shard, g_rep), ("delta_", d_sh, d_rp), ("new_m_", m_sh, m_rp),
                                   ("new_v_", v_sh, v_rp)):
        for n, t in zip(SHARDED, _unpack(sh_arr, shard_shapes)):
            out[prefix + n] = t
        for n, t in zip(REPLICATED, _unpack(rp_arr.reshape(-1)[:n_rep], rep_shapes)):
            out[prefix + n] = t
    return out


IN_NAMES = ["x", "positions"] + W_NAMES + ["loss_target"] + ["m_" + n for n in W_NAMES] + ["v_" + n for n in W_NAMES]
OUT_NAMES = (["loss", "grad_x"] + ["grad_" + n for n in W_NAMES] + ["delta_" + n for n in W_NAMES]
             + ["new_m_" + n for n in W_NAMES] + ["new_v_" + n for n in W_NAMES])


def kernel(x, positions, ln_ffn1, ffn1_w13, ffn1_w2, ln_mix, w_in, conv_w, conv_b, dt_bias, a_log, d_skip, ssd_norm, w_ssd_out, q_lora_norm, w_uq, kv_lora_norm, w_ukv, q_norm, k_norm, w_mla_out, w_o, ln_ffn2, ffn2_w13, ffn2_w2, loss_target, m_ln_ffn1, m_ffn1_w13, m_ffn1_w2, m_ln_mix, m_w_in, m_conv_w, m_conv_b, m_dt_bias, m_a_log, m_d_skip, m_ssd_norm, m_w_ssd_out, m_q_lora_norm, m_w_uq, m_kv_lora_norm, m_w_ukv, m_q_norm, m_k_norm, m_w_mla_out, m_w_o, m_ln_ffn2, m_ffn2_w13, m_ffn2_w2, v_ln_ffn1, v_ffn1_w13, v_ffn1_w2, v_ln_mix, v_w_in, v_conv_w, v_conv_b, v_dt_bias, v_a_log, v_d_skip, v_ssd_norm, v_w_ssd_out, v_q_lora_norm, v_w_uq, v_kv_lora_norm, v_w_ukv, v_q_norm, v_k_norm, v_w_mla_out, v_w_o, v_ln_ffn2, v_ffn2_w13, v_ffn2_w2):
    given = locals()
    out = _step({n: given[n] for n in IN_NAMES})
    return tuple(out[n] for n in OUT_NAMES)
```

```python
import functools
import math

import jax
import jax.numpy as jnp
from jax import lax
from jax.experimental import pallas as pl
from jax.experimental.pallas import tpu as pltpu

F32 = jnp.float32
BF16 = jnp.bfloat16

D_MODEL = 1024
DEPTH = 2
D_FF = 2816
SSD_D_INNER = 2048
SSD_HEADS = 32
SSD_HEAD_DIM = 64
SSD_GROUPS = 4
SSD_STATE = 128
SSD_CHUNK = 128
SSD_CONV = 4
SSD_CONV_DIM = 3072
MLA_HEADS = 8
MLA_Q_LORA = 512
MLA_KV_LORA = 256
MLA_NOPE = 128
MLA_ROPE = 64
MLA_V = 128
MLA_QK = 192
ROPE_THETA = 10000.0
EPS = 1e-6
ADAM_LR = 0.001
ADAM_B1 = 0.9
ADAM_B2 = 0.999
ADAM_EPS = 1e-08
ADAM_WD = 0.01
ADAM_STEP = 10

PROJ_W = 8064
OFF_Z, OFF_XBC, OFF_GATES, OFF_CQ, OFF_CKV, OFF_KRDT = 0, 2048, 5120, 7168, 7680, 7936

LANE = 128
VMEM_LIMIT = 48 * 1024 * 1024
HI = lax.Precision.HIGHEST


def _cp(*sem):
    return pltpu.CompilerParams(dimension_semantics=sem, vmem_limit_bytes=VMEM_LIMIT)


def _pick(dim, target, align):
    if dim <= target:
        return dim
    b = (target // align) * align
    while b >= align:
        if dim % b == 0:
            return b
        b -= align
    raise ValueError(f"no block for {dim} (target {target}, align {align})")


def _silu(x):
    return x * jax.nn.sigmoid(x)


def _dsilu(x):
    s = jax.nn.sigmoid(x)
    return s * (1.0 + x * (1.0 - s))


def _matmul(a, b, mode, *, name, out_dtype=F32, scale=1.0, res=None):
    if mode == "nn":
        (m, k), (k2, n) = a.shape, b.shape
    elif mode == "nt":
        (m, k), (n, k2) = a.shape, b.shape
    else:
        (k, m), (k2, n) = a.shape, b.shape
    assert k == k2, (a.shape, b.shape, mode)
    if mode == "tn":
        bm, bn, bk = _pick(m, 512, LANE), _pick(n, 2816, LANE), _pick(k, 512, 8)
    else:
        bm, bn, bk = _pick(m, 1024, 8), _pick(n, 1152, LANE), _pick(k, 1536, LANE)
    nk = k // bk

    def body(a_ref, b_ref, *rest):
        res_ref = rest[0] if res is not None else None
        o_ref = rest[-2] if nk > 1 else rest[-1]
        kk = pl.program_id(2)
        av = a_ref[...].astype(BF16)
        bv = b_ref[...].astype(BF16)
        if mode == "nn":
            dims = (((1,), (0,)), ((), ()))
        elif mode == "nt":
            dims = (((1,), (1,)), ((), ()))
        else:
            dims = (((0,), (0,)), ((), ()))
        part = lax.dot_general(av, bv, dims, preferred_element_type=F32)

        def finish(total):
            out = total * scale
            if res_ref is not None:
                out = res_ref[...] + out
            o_ref[...] = out.astype(o_ref.dtype)

        if nk == 1:
            finish(part)
            return
        acc_ref = rest[-1]

        @pl.when(kk == 0)
        def _():
            acc_ref[...] = part

        @pl.when((kk > 0) & (kk < nk - 1))
        def _():
            acc_ref[...] += part

        @pl.when(kk == nk - 1)
        def _():
            finish(acc_ref[...] + part)

    o_spec = pl.BlockSpec((bm, bn), lambda i, j, kk: (i, j))
    if mode == "nn":
        a_spec = pl.BlockSpec((bm, bk), lambda i, j, kk: (i, kk))
        b_spec = pl.BlockSpec((bk, bn), lambda i, j, kk: (kk, j))
    elif mode == "nt":
        a_spec = pl.BlockSpec((bm, bk), lambda i, j, kk: (i, kk))
        b_spec = pl.BlockSpec((bn, bk), lambda i, j, kk: (j, kk))
    else:
        a_spec = pl.BlockSpec((bk, bm), lambda i, j, kk: (kk, i))
        b_spec = pl.BlockSpec((bk, bn), lambda i, j, kk: (kk, j))
    return pl.pallas_call(
        body, name=name,
        grid=(m // bm, n // bn, nk),
        in_specs=[a_spec, b_spec] + ([o_spec] if res is not None else []),
        out_specs=o_spec,
        out_shape=jax.ShapeDtypeStruct((m, n), out_dtype),
        scratch_shapes=[pltpu.VMEM((bm, bn), F32)] if nk > 1 else [],
        compiler_params=_cp("parallel", "parallel", "arbitrary"),
    )(*((a, b) + ((res,) if res is not None else ())))


def _rms_fwd(x, g, *, name, col=0, width=None):
    r = x.shape[0]
    w = width or x.shape[1]
    tr = _pick(r, 512, 16)

    def body(x_ref, g_ref, o_ref):
        xv = x_ref[...]
        rs = lax.rsqrt(jnp.mean(xv * xv, axis=-1, keepdims=True) + EPS)
        o_ref[...] = (xv * rs * g_ref[...]).astype(o_ref.dtype)

    return pl.pallas_call(
        body, name=name, grid=(r // tr,),
        in_specs=[pl.BlockSpec((tr, w), lambda i: (i, col)), pl.BlockSpec((1, w), lambda i: (0, 0))],
        out_specs=pl.BlockSpec((tr, w), lambda i: (i, 0)),
        out_shape=jax.ShapeDtypeStruct((r, w), BF16),
        compiler_params=_cp("parallel"),
    )(x, g)


def _rms_bwd(x, g, dy, *, name, col=0, width=None, res=None):
    r = x.shape[0]
    w = width or x.shape[1]
    tr = _pick(r, 512, 8)

    def body(x_ref, g_ref, dy_ref, *rest):
        res_ref = rest[0] if res is not None else None
        dx_ref, dg_ref = rest[-2:]
        i = pl.program_id(0)
        xv = x_ref[...]
        dyv = dy_ref[...]
        rs = lax.rsqrt(jnp.mean(xv * xv, axis=-1, keepdims=True) + EPS)
        xh = xv * rs
        dxh = dyv * g_ref[...]
        mm = jnp.mean(dxh * xh, axis=-1, keepdims=True)
        dx = rs * (dxh - xh * mm)
        if res_ref is not None:
            dx = res_ref[...] + dx
        dx_ref[...] = dx
        part = jnp.sum(dyv * xh, axis=0, keepdims=True)

        @pl.when(i == 0)
        def _():
            dg_ref[...] = part

        @pl.when(i > 0)
        def _():
            dg_ref[...] += part

    blk = pl.BlockSpec((tr, w), lambda i: (i, 0))
    return pl.pallas_call(
        body, name=name, grid=(r // tr,),
        in_specs=[pl.BlockSpec((tr, w), lambda i: (i, col)), pl.BlockSpec((1, w), lambda i: (0, 0)), blk]
        + ([blk] if res is not None else []),
        out_specs=[blk, pl.BlockSpec((1, w), lambda i: (0, 0))],
        out_shape=[jax.ShapeDtypeStruct((r, w), F32), jax.ShapeDtypeStruct((1, w), F32)],
        compiler_params=_cp("arbitrary"),
    )(*((x, g, dy) + ((res,) if res is not None else ())))


def _gated_rms_fwd(y, proj, g, *, name):
    r, w = y.shape
    tr = _pick(r, 256, 8)

    def body(y_ref, z_ref, g_ref, o_ref):
        t = y_ref[...] * _silu(z_ref[...])
        rs = lax.rsqrt(jnp.mean(t * t, axis=-1, keepdims=True) + EPS)
        o_ref[...] = (t * rs * g_ref[...]).astype(o_ref.dtype)

    return pl.pallas_call(
        body, name=name, grid=(r // tr,),
        in_specs=[pl.BlockSpec((tr, w), lambda i: (i, 0)), pl.BlockSpec((tr, w), lambda i: (i, OFF_Z // w)),
                  pl.BlockSpec((1, w), lambda i: (0, 0))],
        out_specs=pl.BlockSpec((tr, w), lambda i: (i, 0)),
        out_shape=jax.ShapeDtypeStruct((r, w), BF16),
        compiler_params=_cp("parallel"),
    )(y, proj, g)


def _gated_rms_bwd(y, proj, g, do, *, name):
    r, w = y.shape
    tr = _pick(r, 256, 8)

    def body(y_ref, z_ref, g_ref, do_ref, dy_ref, dz_ref, dg_ref):
        i = pl.program_id(0)
        yv, zv, dov = y_ref[...], z_ref[...], do_ref[...]
        sz = _silu(zv)
        t = yv * sz
        rs = lax.rsqrt(jnp.mean(t * t, axis=-1, keepdims=True) + EPS)
        th = t * rs
        dth = dov * g_ref[...]
        mm = jnp.mean(dth * th, axis=-1, keepdims=True)
        dt = rs * (dth - th * mm)
        dy_ref[...] = dt * sz
        dz_ref[...] = dt * yv * _dsilu(zv)
        part = jnp.sum(dov * th, axis=0, keepdims=True)

        @pl.when(i == 0)
        def _():
            dg_ref[...] = part

        @pl.when(i > 0)
        def _():
            dg_ref[...] += part

    blk = pl.BlockSpec((tr, w), lambda i: (i, 0))
    vec = pl.BlockSpec((1, w), lambda i: (0, 0))
    return pl.pallas_call(
        body, name=name, grid=(r // tr,),
        in_specs=[blk, pl.BlockSpec((tr, w), lambda i: (i, OFF_Z // w)), vec, blk],
        out_specs=[blk, blk, vec],
        out_shape=[jax.ShapeDtypeStruct((r, w), F32), jax.ShapeDtypeStruct((r, w), F32),
                   jax.ShapeDtypeStruct((1, w), F32)],
        compiler_params=_cp("arbitrary"),
    )(y, proj, g, do)


def _swiglu_fwd(gu, *, name):
    r = gu.shape[0]
    f = gu.shape[1] // 2
    tr = _pick(r, 256, 8)

    def body(g_ref, u_ref, o_ref):
        o_ref[...] = (_silu(g_ref[...]) * u_ref[...]).astype(o_ref.dtype)

    return pl.pallas_call(
        body, name=name, grid=(r // tr,),
        in_specs=[pl.BlockSpec((tr, f), lambda i: (i, 0)), pl.BlockSpec((tr, f), lambda i: (i, 1))],
        out_specs=pl.BlockSpec((tr, f), lambda i: (i, 0)),
        out_shape=jax.ShapeDtypeStruct((r, f), BF16),
        compiler_params=_cp("parallel"),
    )(gu, gu)


def _swiglu_bwd(gu, da, *, name):
    r = gu.shape[0]
    f = gu.shape[1] // 2
    tr = _pick(r, 256, 8)

    def body(g_ref, u_ref, da_ref, o_ref):
        gv, uv, dav = g_ref[...], u_ref[...], da_ref[...]
        o_ref[:, :f] = (dav * uv * _dsilu(gv)).astype(o_ref.dtype)
        o_ref[:, f:] = (dav * _silu(gv)).astype(o_ref.dtype)

    return pl.pallas_call(
        body, name=name, grid=(r // tr,),
        in_specs=[pl.BlockSpec((tr, f), lambda i: (i, 0)), pl.BlockSpec((tr, f), lambda i: (i, 1)),
                  pl.BlockSpec((tr, f), lambda i: (i, 0))],
        out_specs=pl.BlockSpec((tr, 2 * f), lambda i: (i, 0)),
        out_shape=jax.ShapeDtypeStruct((r, 2 * f), BF16),
        compiler_params=_cp("parallel"),
    )(gu, gu, da)


CONV_TS = 1024
CONV_TC = 512


def _conv_pre(x, carry, w_ref, b_ref):
    ts = x.shape[0]
    row8 = lax.broadcasted_iota(jnp.int32, (8, x.shape[1]), 0)
    head_x = x[0:8]
    shifted, shifted_head = [], []
    for j in range(SSD_CONV):
        if j == 0:
            shifted.append(x)
            shifted_head.append(head_x)
        else:
            shifted.append(pltpu.roll(x, j, 0))
            shifted_head.append(jnp.where(row8 < j, pltpu.roll(carry, j, 0), pltpu.roll(head_x, j, 0)))
    pre = b_ref[...] + sum(w_ref[SSD_CONV - 1 - j:SSD_CONV - j, :] * shifted[j] for j in range(SSD_CONV))
    pre_head = b_ref[...] + sum(w_ref[SSD_CONV - 1 - j:SSD_CONV - j, :] * shifted_head[j] for j in range(SSD_CONV))
    del ts
    return pre, pre_head, shifted, shifted_head


def _conv_fwd(proj, w, b, *, name):
    s = proj.shape[0]
    c = w.shape[1]
    ts, tc = _pick(s, CONV_TS, 8), CONV_TC
    off = OFF_XBC // tc

    def body(x_ref, w_ref, b_ref, o_ref, carry_ref):
        t = pl.program_id(1)

        @pl.when(t == 0)
        def _():
            carry_ref[...] = jnp.zeros_like(carry_ref)

        x = x_ref[...]
        pre, pre_head, _, _ = _conv_pre(x, carry_ref[...], w_ref, b_ref)
        o_ref[...] = _silu(pre)
        o_ref[0:8, :] = _silu(pre_head)
        carry_ref[...] = x[ts - 8:ts]

    return pl.pallas_call(
        body, name=name, grid=(c // tc, s // ts),
        in_specs=[pl.BlockSpec((ts, tc), lambda j, t: (t, j + off)), pl.BlockSpec((SSD_CONV, tc), lambda j, t: (0, j)),
                  pl.BlockSpec((1, tc), lambda j, t: (0, j))],
        out_specs=pl.BlockSpec((ts, tc), lambda j, t: (t, j)),
        out_shape=jax.ShapeDtypeStruct((s, c), F32),
        scratch_shapes=[pltpu.VMEM((8, tc), F32)],
        compiler_params=_cp("parallel", "arbitrary"),
    )(proj, w, b)


def _conv_bwd_pre(proj, w, b, dy, *, name):
    s = proj.shape[0]
    c = w.shape[1]
    ts, tc = _pick(s, CONV_TS, 8), CONV_TC
    off = OFF_XBC // tc

    def body(x_ref, w_ref, b_ref, dy_ref, dp_ref, dw_ref, db_ref, carry_ref):
        t = pl.program_id(1)

        @pl.when(t == 0)
        def _():
            carry_ref[...] = jnp.zeros_like(carry_ref)
            dw_ref[...] = jnp.zeros_like(dw_ref)
            db_ref[...] = jnp.zeros_like(db_ref)

        x = x_ref[...]
        pre, pre_head, shifted, shifted_head = _conv_pre(x, carry_ref[...], w_ref, b_ref)
        dyv = dy_ref[...]
        dp = dyv * _dsilu(pre)
        dp_head = dyv[0:8] * _dsilu(pre_head)
        row = lax.broadcasted_iota(jnp.int32, dp.shape, 0)
        dp_tail = jnp.where(row >= 8, dp, 0.0)
        dp_ref[...] = dp
        dp_ref[0:8, :] = dp_head
        db_ref[...] += jnp.sum(dp_tail, axis=0, keepdims=True) + jnp.sum(dp_head, axis=0, keepdims=True)
        for j in range(SSD_CONV):
            kk = SSD_CONV - 1 - j
            dw_ref[kk:kk + 1, :] += (jnp.sum(dp_tail * shifted[j], axis=0, keepdims=True)
                                     + jnp.sum(dp_head * shifted_head[j], axis=0, keepdims=True))
        carry_ref[...] = x[ts - 8:ts]

    return pl.pallas_call(
        body, name=name, grid=(c // tc, s // ts),
        in_specs=[pl.BlockSpec((ts, tc), lambda j, t: (t, j + off)), pl.BlockSpec((SSD_CONV, tc), lambda j, t: (0, j)),
                  pl.BlockSpec((1, tc), lambda j, t: (0, j)), pl.BlockSpec((ts, tc), lambda j, t: (t, j))],
        out_specs=[pl.BlockSpec((ts, tc), lambda j, t: (t, j)), pl.BlockSpec((SSD_CONV, tc), lambda j, t: (0, j)),
                   pl.BlockSpec((1, tc), lambda j, t: (0, j))],
        out_shape=[jax.ShapeDtypeStruct((s, c), F32), jax.ShapeDtypeStruct((SSD_CONV, c), F32),
                   jax.ShapeDtypeStruct((1, c), F32)],
        scratch_shapes=[pltpu.VMEM((8, tc), F32)],
        compiler_params=_cp("parallel", "arbitrary"),
    )(proj, w, b, dy)


def _conv_bwd_x(dp, w, *, name):
    s, c = dp.shape
    ts, tc = _pick(s, CONV_TS, 8), CONV_TC
    nt = s // ts

    def body(d_ref, w_ref, o_ref, carry_ref):
        t = pl.program_id(1)

        @pl.when(t == 0)
        def _():
            carry_ref[...] = jnp.zeros_like(carry_ref)

        d = d_ref[...]
        carry = carry_ref[...]
        row8 = lax.broadcasted_iota(jnp.int32, (8, tc), 0)
        tail = d[ts - 8:ts]
        acc = w_ref[SSD_CONV - 1:SSD_CONV, :] * d
        acc_tail = w_ref[SSD_CONV - 1:SSD_CONV, :] * tail
        for j in range(1, SSD_CONV):
            wj = w_ref[SSD_CONV - 1 - j:SSD_CONV - j, :]
            acc = acc + wj * pltpu.roll(d, ts - j, 0)
            up_tail = jnp.where(row8 >= 8 - j, pltpu.roll(carry, 8 - j, 0), pltpu.roll(tail, 8 - j, 0))
            acc_tail = acc_tail + wj * up_tail
        o_ref[...] = acc
        o_ref[ts - 8:ts, :] = acc_tail
        carry_ref[...] = d[0:8]

    return pl.pallas_call(
        body, name=name, grid=(c // tc, nt),
        in_specs=[pl.BlockSpec((ts, tc), lambda j, t: (nt - 1 - t, j)), pl.BlockSpec((SSD_CONV, tc), lambda j, t: (0, j))],
        out_specs=pl.BlockSpec((ts, tc), lambda j, t: (nt - 1 - t, j)),
        out_shape=jax.ShapeDtypeStruct((s, c), F32),
        scratch_shapes=[pltpu.VMEM((8, tc), F32)],
        compiler_params=_cp("parallel", "arbitrary"),
    )(dp, w)


def _merge_fwd(proj, ys, ym, *, name):
    r, w = ys.shape
    tr = _pick(r, 512, 8)
    off = OFF_GATES // w

    def body(g1_ref, g2_ref, ys_ref, ym_ref, o_ref):
        o_ref[...] = (jax.nn.sigmoid(g1_ref[...]) * ys_ref[...]
                      + jax.nn.sigmoid(g2_ref[...]) * ym_ref[...]).astype(o_ref.dtype)

    blk = pl.BlockSpec((tr, w), lambda i: (i, 0))
    return pl.pallas_call(
        body, name=name, grid=(r // tr,),
        in_specs=[pl.BlockSpec((tr, w), lambda i: (i, off)), pl.BlockSpec((tr, w), lambda i: (i, off + 1)), blk, blk],
        out_specs=blk, out_shape=jax.ShapeDtypeStruct((r, w), BF16),
        compiler_params=_cp("parallel"),
    )(proj, proj, ys, ym)


def _merge_bwd(proj, ys, ym, dm, *, name):
    r, w = ys.shape
    tr = _pick(r, 512, 8)
    off = OFF_GATES // w

    def body(g1_ref, g2_ref, ys_ref, ym_ref, dm_ref, dg_ref, dys_ref, dym_ref):
        s1, s2 = jax.nn.sigmoid(g1_ref[...]), jax.nn.sigmoid(g2_ref[...])
        dmv = dm_ref[...]
        dys_ref[...] = (dmv * s1).astype(dys_ref.dtype)
        dym_ref[...] = (dmv * s2).astype(dym_ref.dtype)
        dg_ref[:, :w] = dmv * ys_ref[...] * s1 * (1.0 - s1)
        dg_ref[:, w:] = dmv * ym_ref[...] * s2 * (1.0 - s2)

    blk = pl.BlockSpec((tr, w), lambda i: (i, 0))
    return pl.pallas_call(
        body, name=name, grid=(r // tr,),
        in_specs=[pl.BlockSpec((tr, w), lambda i: (i, off)), pl.BlockSpec((tr, w), lambda i: (i, off + 1)), blk, blk, blk],
        out_specs=[pl.BlockSpec((tr, 2 * w), lambda i: (i, 0)), blk, blk],
        out_shape=[jax.ShapeDtypeStruct((r, 2 * w), F32), jax.ShapeDtypeStruct((r, w), BF16),
                   jax.ShapeDtypeStruct((r, w), BF16)],
        compiler_params=_cp("parallel"),
    )(proj, proj, ys, ym, dm)


def _loss_fwd_bwd(y, target, *, name):
    r, w = y.shape
    tr = _pick(r, 512, 8)

    def body(y_ref, t_ref, l_ref, dy_ref):
        i = pl.program_id(0)
        e = y_ref[...] - t_ref[...]
        dy_ref[...] = e * (1.0 / w)
        part = jnp.sum(e * e, axis=0, keepdims=True) * (0.5 / w)

        @pl.when(i == 0)
        def _():
            l_ref[...] = part

        @pl.when(i > 0)
        def _():
            l_ref[...] += part

    blk = pl.BlockSpec((tr, w), lambda i: (i, 0))
    return pl.pallas_call(
        body, name=name, grid=(r // tr,),
        in_specs=[blk, blk],
        out_specs=[pl.BlockSpec((1, w), lambda i: (0, 0)), blk],
        out_shape=[jax.ShapeDtypeStruct((1, w), F32), jax.ShapeDtypeStruct((r, w), F32)],
        compiler_params=_cp("arbitrary"),
    )(y, target)


def _adamw(w, g, m, v, *, name):
    r, c = w.shape
    tr = _pick(r, max(8, (1 << 20) // (4 * c) // 8 * 8), 8)
    c1 = 1.0 - ADAM_B1 ** ADAM_STEP
    c2 = 1.0 - ADAM_B2 ** ADAM_STEP

    def body(w_ref, g_ref, m_ref, v_ref, d_ref, nm_ref, nv_ref):
        gv = g_ref[...]
        nm = ADAM_B1 * m_ref[...] + (1.0 - ADAM_B1) * gv
        nv = ADAM_B2 * v_ref[...] + (1.0 - ADAM_B2) * (gv * gv)
        nm_ref[...] = nm
        nv_ref[...] = nv
        d_ref[...] = -ADAM_LR * ((nm / c1) / (jnp.sqrt(nv / c2) + ADAM_EPS) + ADAM_WD * w_ref[...])

    blk = pl.BlockSpec((tr, c), lambda i: (i, 0))
    sh = jax.ShapeDtypeStruct((r, c), F32)
    return pl.pallas_call(
        body, name=name, grid=(r // tr,),
        in_specs=[blk] * 4, out_specs=[blk] * 3, out_shape=[sh] * 3,
        compiler_params=_cp("parallel"),
    )(w, g, m, v)


def _softplus(x):
    return jnp.maximum(x, 0.0) + jnp.log(1.0 + jnp.exp(-jnp.abs(x)))


def _ssd_common(dtr_ref, dtrT_ref, dtb_ref, dtbT_ref, al_ref, alT_ref, e_ref):
    L = SSD_CHUNK
    ri = lax.broadcasted_iota(jnp.int32, (L, L), 0)
    cj = lax.broadcasted_iota(jnp.int32, (L, L), 1)
    tril = (ri >= cj).astype(F32)
    triu = (ri <= cj).astype(F32)
    a = -jnp.exp(al_ref[...])
    aT = -jnp.exp(alT_ref[...])
    pre = dtr_ref[...] + dtb_ref[...]
    preT = dtrT_ref[...] + dtbT_ref[...]
    dt = _softplus(pre)
    dtT = _softplus(preT)
    acum = jnp.dot(tril, dt * a, precision=HI, preferred_element_type=F32)
    acumT = jnp.dot(dtT * aT, triu, precision=HI, preferred_element_type=F32)
    e = e_ref[...]
    dt_x = jnp.dot(dt, e, precision=HI, preferred_element_type=F32)
    acum_x = jnp.dot(acum, e, precision=HI, preferred_element_type=F32)
    last_x = acum_x[L - 1:L, :]
    return dict(ri=ri, cj=cj, tril=tril, triu=triu, a=a, aT=aT, pre=pre, preT=preT, dt=dt, dtT=dtT,
                acum=acum, acumT=acumT, dt_x=dt_x, eacum_x=jnp.exp(acum_x), w_x=jnp.exp(last_x - acum_x),
                elast_x=jnp.exp(last_x))


def _dot_nt(a, b):
    return lax.dot_general(a, b, (((1,), (1,)), ((), ())), preferred_element_type=F32)


def _dot_tn(a, b):
    return lax.dot_general(a, b, (((0,), (0,)), ((), ())), preferred_element_type=F32)


def _dot(a, b):
    return jnp.dot(a, b, preferred_element_type=F32)


def _ssd_specs(nc, rev):
    L = SSD_CHUNK
    ix = (lambda c: nc - 1 - c) if rev else (lambda c: c)
    return [
        pl.BlockSpec((L, SSD_D_INNER), lambda c: (ix(c), 0)),
        pl.BlockSpec((L, 512), lambda c: (ix(c), 4)),
        pl.BlockSpec((L, 512), lambda c: (ix(c), 5)),
        pl.BlockSpec((L, SSD_HEADS), lambda c: (ix(c), 0)),
        pl.BlockSpec((SSD_HEADS, L), lambda c: (0, ix(c))),
        pl.BlockSpec((1, SSD_HEADS), lambda c: (0, 0)),
        pl.BlockSpec((SSD_HEADS, 1), lambda c: (0, 0)),
        pl.BlockSpec((1, SSD_HEADS), lambda c: (0, 0)),
        pl.BlockSpec((SSD_HEADS, 1), lambda c: (0, 0)),
        pl.BlockSpec((1, SSD_D_INNER), lambda c: (0, 0)),
        pl.BlockSpec((SSD_HEADS, SSD_D_INNER), lambda c: (0, 0)),
    ]


def _ssd_fwd(xc, dtr, dtrT, dtb, dtbT, alog, alogT, dskx, expand, *, name):
    s = xc.shape[0]
    L = SSD_CHUNK
    nc = s // L

    def body(x_ref, b_ref, c_ref, dtr_ref, dtrT_ref, dtb_ref, dtbT_ref, al_ref, alT_ref, dsk_ref, e_ref,
             y_ref, st_ref, state):
        ci = pl.program_id(0)

        @pl.when(ci == 0)
        def _():
            state[...] = jnp.zeros_like(state)

        st_ref[0] = state[...]
        q = _ssd_common(dtr_ref, dtrT_ref, dtb_ref, dtbT_ref, al_ref, alT_ref, e_ref)
        causal = q["ri"] >= q["cj"]
        lane_lo = q["cj"] < 64
        x = x_ref[...]
        xdt = x * q["dt_x"]
        xdt_b = xdt.astype(BF16)
        xdtw_b = (xdt * q["w_x"]).astype(BF16)
        for g in range(SSD_GROUPS):
            bg = b_ref[:, 128 * g:128 * g + 128]
            cg_b = c_ref[:, 128 * g:128 * g + 128].astype(BF16)
            cb = _dot_nt(cg_b, bg.astype(BF16))
            bgT_b = bg.T.astype(BF16)
            s0 = state[g]
            for jj in range(4):
                j = 4 * g + jj
                sl = slice(128 * j, 128 * j + 128)
                sls = slice(128 * jj, 128 * jj + 128)
                ms = []
                for h in (2 * j, 2 * j + 1):
                    seg = q["acum"][:, h:h + 1] - q["acumT"][h:h + 1, :]
                    decay = jnp.exp(jnp.where(causal, seg, -jnp.inf))
                    ms.append((cb * decay).astype(BF16))
                mcat = jnp.concatenate(ms, axis=1)
                xp = xdt_b[:, sl]
                zero = jnp.zeros_like(xp)
                xstack = jnp.concatenate([jnp.where(lane_lo, xp, zero), jnp.where(lane_lo, zero, xp)], axis=0)
                y = _dot(mcat, xstack)
                y = y + q["eacum_x"][:, sl] * _dot(cg_b, s0[:, sls].astype(BF16))
                y = y + x[:, sl] * dsk_ref[:, sl]
                y_ref[:, sl] = y
                state[g, :, sls] = s0[:, sls] * q["elast_x"][:, sl] + _dot(bgT_b, xdtw_b[:, sl])

    return pl.pallas_call(
        body, name=name, grid=(nc,),
        in_specs=_ssd_specs(nc, False),
        out_specs=[pl.BlockSpec((L, SSD_D_INNER), lambda c: (c, 0)),
                   pl.BlockSpec((1, SSD_GROUPS, SSD_STATE, 512), lambda c: (c, 0, 0, 0))],
        out_shape=[jax.ShapeDtypeStruct((s, SSD_D_INNER), F32),
                   jax.ShapeDtypeStruct((nc, SSD_GROUPS, SSD_STATE, 512), F32)],
        scratch_shapes=[pltpu.VMEM((SSD_GROUPS, SSD_STATE, 512), F32)],
        compiler_params=_cp("arbitrary"),
    )(xc, xc, xc, dtr, dtrT, dtb, dtbT, alog, alogT, dskx, expand)


def _ssd_bwd(xc, dtr, dtrT, dtb, dtbT, alog, alogT, dskx, expand, expandT, states, dy, *, name):
    s = xc.shape[0]
    L = SSD_CHUNK
    H = SSD_HEADS
    nc = s // L

    def body(x_ref, b_ref, c_ref, dtr_ref, dtrT_ref, dtb_ref, dtbT_ref, al_ref, alT_ref, dsk_ref, e_ref,
             et_ref, st_ref, dy_ref,
             dxc_ref, ddtc_ref, ddtr_ref, dbc_ref, dbr_ref, dac_ref, dar_ref, ddsk_ref, dstate):
        ci = pl.program_id(0)

        @pl.when(ci == 0)
        def _():
            dstate[...] = jnp.zeros_like(dstate)
            dbc_ref[...] = jnp.zeros_like(dbc_ref)
            dbr_ref[...] = jnp.zeros_like(dbr_ref)
            dac_ref[...] = jnp.zeros_like(dac_ref)
            dar_ref[...] = jnp.zeros_like(dar_ref)
            ddsk_ref[...] = jnp.zeros_like(ddsk_ref)

        q = _ssd_common(dtr_ref, dtrT_ref, dtb_ref, dtbT_ref, al_ref, alT_ref, e_ref)
        ri, cj = q["ri"], q["cj"]
        causal = ri >= cj
        causalT = ri <= cj
        lane_lo = cj < 64
        lane_h = lax.broadcasted_iota(jnp.int32, (1, H), 1)
        sub_h = lax.broadcasted_iota(jnp.int32, (H, 1), 0)
        x = x_ref[...]
        dyv = dy_ref[...]
        xdt = x * q["dt_x"]
        xdt_b = xdt.astype(BF16)
        xdtw = xdt * q["w_x"]
        xdtw_b = xdtw.astype(BF16)
        edy = q["eacum_x"] * dyv
        edy_b = edy.astype(BF16)
        dyv_b = dyv.astype(BF16)
        dacum_col = jnp.zeros((L, H), F32)
        dacum_row = jnp.zeros((H, L), F32)
        dxdt_t, yoff_t, u_t, r_t = [], [], [], []
        for g in range(SSD_GROUPS):
            bg = b_ref[:, 128 * g:128 * g + 128]
            cg = c_ref[:, 128 * g:128 * g + 128]
            bg_b, cg_b = bg.astype(BF16), cg.astype(BF16)
            cb = _dot_nt(cg_b, bg_b)
            cbT = _dot_nt(bg_b, cg_b)
            cgT_b = cg.T.astype(BF16)
            s0 = st_ref[0, g]
            ds = dstate[g]
            s0_b, ds_b = s0.astype(BF16), ds.astype(BF16)
            dcb = jnp.zeros((L, L), F32)
            for jj in range(4):
                j = 4 * g + jj
                sl = slice(128 * j, 128 * j + 128)
                sls = slice(128 * jj, 128 * jj + 128)
                decs, mts = [], []
                for h in (2 * j, 2 * j + 1):
                    seg = q["acum"][:, h:h + 1] - q["acumT"][h:h + 1, :]
                    decs.append(jnp.exp(jnp.where(causal, seg, -jnp.inf)))
                    mts.append((cbT * jnp.exp(jnp.where(causalT, -seg, -jnp.inf))).astype(BF16))
                dyt_b = dyv_b[:, sl]
                zero = jnp.zeros_like(dyt_b)
                dystack = jnp.concatenate([jnp.where(lane_lo, dyt_b, zero), jnp.where(lane_lo, zero, dyt_b)], axis=0)
                dxs = _dot(jnp.concatenate(mts, axis=0), dyt_b)
                dxdt = jnp.where(lane_lo, dxs[:L], dxs[L:])
                dmcat = _dot_nt(dystack, xdt_b[:, sl])
                for idx, h in enumerate((2 * j, 2 * j + 1)):
                    dm = dmcat[L * idx:L * idx + L]
                    dcb = dcb + dm * decs[idx]
                    dseg = dm * cb * decs[idx]
                    dacum_col = dacum_col + jnp.sum(dseg, axis=1, keepdims=True) * (lane_h == h).astype(F32)
                    dacum_row = dacum_row - (sub_h == h).astype(F32) * jnp.sum(dseg, axis=0, keepdims=True)
                gmat = _dot(cg_b, s0_b[:, sls])
                yoff_t.append(edy[:, sl] * gmat)
                qm = _dot(bg_b, ds_b[:, sls])
                dxdt_t.append(dxdt + qm * q["w_x"][:, sl])
                u_t.append(qm * xdtw[:, sl])
                r_t.append(ds[:, sls] * s0[:, sls] * q["elast_x"][:, sl])
                dstate[g, :, sls] = ds[:, sls] * q["elast_x"][:, sl] + _dot(cgT_b, edy_b[:, sl])
            gsl = slice(512 * g, 512 * g + 512)
            dcb_b = dcb.astype(BF16)
            dcg = _dot(dcb_b, bg_b) + _dot_nt(edy_b[:, gsl], s0_b)
            dbg = _dot(dcb.T.astype(BF16), cg_b) + _dot_nt(xdtw_b[:, gsl], ds_b)
            dxc_ref[:, SSD_D_INNER + 128 * g:SSD_D_INNER + 128 * g + 128] = dbg
            dxc_ref[:, SSD_D_INNER + 512 + 128 * g:SSD_D_INNER + 512 + 128 * g + 128] = dcg
        et = et_ref[...]
        dxdt_all = jnp.concatenate(dxdt_t, axis=1)
        yoff = jnp.concatenate(yoff_t, axis=1)
        uu = jnp.concatenate(u_t, axis=1)
        rr = jnp.concatenate(r_t, axis=1)
        dacum_col = dacum_col + jnp.dot(yoff - uu, et, precision=HI, preferred_element_type=F32)
        dlast = jnp.sum(jnp.dot(uu + rr, et, precision=HI, preferred_element_type=F32), axis=0, keepdims=True)
        row_lh = lax.broadcasted_iota(jnp.int32, (L, H), 0)
        dacum_col = dacum_col + jnp.where(row_lh == L - 1, dlast, 0.0)
        d_dta_col = jnp.dot(q["triu"], dacum_col, precision=HI, preferred_element_type=F32)
        d_dta_row = jnp.dot(dacum_row, q["tril"], precision=HI, preferred_element_type=F32)
        ddt_col = d_dta_col * q["a"] + jnp.dot(dxdt_all * x, et, precision=HI, preferred_element_type=F32)
        ddt_row = d_dta_row * q["aT"]
        ddtr_col = ddt_col * jax.nn.sigmoid(q["pre"])
        ddtr_row = ddt_row * jax.nn.sigmoid(q["preT"])
        ddtc_ref[...] = ddtr_col
        ddtr_ref[...] = ddtr_row
        dac_ref[...] += jnp.sum(d_dta_col * q["dt"], axis=0, keepdims=True)
        dar_ref[...] += jnp.sum(d_dta_row * q["dtT"], axis=1, keepdims=True)
        dbc_ref[...] += jnp.sum(ddtr_col, axis=0, keepdims=True)
        dbr_ref[...] += jnp.sum(ddtr_row, axis=1, keepdims=True)
        ddsk_ref[...] += jnp.sum(dyv * x, axis=0, keepdims=True)
        dxc_ref[:, 0:SSD_D_INNER] = dxdt_all * q["dt_x"] + dyv * dsk_ref[...]

    rv = lambda c: nc - 1 - c
    in_specs = _ssd_specs(nc, True) + [
        pl.BlockSpec((SSD_D_INNER, H), lambda c: (0, 0)),
        pl.BlockSpec((1, SSD_GROUPS, SSD_STATE, 512), lambda c: (rv(c), 0, 0, 0)),
        pl.BlockSpec((L, SSD_D_INNER), lambda c: (rv(c), 0)),
    ]
    vec_c = pl.BlockSpec((1, H), lambda c: (0, 0))
    vec_r = pl.BlockSpec((H, 1), lambda c: (0, 0))
    return pl.pallas_call(
        body, name=name, grid=(nc,),
        in_specs=in_specs,
        out_specs=[pl.BlockSpec((L, SSD_CONV_DIM), lambda c: (rv(c), 0)),
                   pl.BlockSpec((L, H), lambda c: (rv(c), 0)),
                   pl.BlockSpec((H, L), lambda c: (0, rv(c))),
                   vec_c, vec_r, vec_c, vec_r,
                   pl.BlockSpec((1, SSD_D_INNER), lambda c: (0, 0))],
        out_shape=[jax.ShapeDtypeStruct((s, SSD_CONV_DIM), F32),
                   jax.ShapeDtypeStruct((s, H), F32), jax.ShapeDtypeStruct((H, s), F32),
                   jax.ShapeDtypeStruct((1, H), F32), jax.ShapeDtypeStruct((H, 1), F32),
                   jax.ShapeDtypeStruct((1, H), F32), jax.ShapeDtypeStruct((H, 1), F32),
                   jax.ShapeDtypeStruct((1, SSD_D_INNER), F32)],
        scratch_shapes=[pltpu.VMEM((SSD_GROUPS, SSD_STATE, 512), F32)],
        compiler_params=_cp("arbitrary"),
    )(xc, xc, xc, dtr, dtrT, dtb, dtbT, alog, alogT, dskx, expand, expandT, states, dy)


QK_PAD = 256
MLA_TS = 256


def _rope_tables4(pos):
    inv = 1.0 / (ROPE_THETA ** (jnp.arange(0, MLA_ROPE, 2, dtype=F32) / MLA_ROPE))
    ang = pos.astype(F32)[:, None] * inv
    c, s = jnp.cos(ang), jnp.sin(ang)
    return jnp.tile(c, (1, 4)), jnp.concatenate([-s, s, -s, s], axis=1)


def _mla_gains(qg, kg):
    z = jnp.zeros((LANE - MLA_ROPE,), F32)
    return (qg[:MLA_NOPE][None], jnp.concatenate([qg[MLA_NOPE:], z])[None],
            kg[:MLA_NOPE][None], jnp.concatenate([kg[MLA_NOPE:], z])[None])


def _rope_swap(t, first):
    return jnp.where(first, pltpu.roll(t, 96, 1), pltpu.roll(t, 32, 1))


def _mla_prep_specs(ts):
    row = lambda w, c=0: pl.BlockSpec((ts, w), lambda i: (i, c))
    vec = pl.BlockSpec((1, LANE), lambda i: (0, 0))
    return [row(MLA_HEADS * MLA_QK), row(2 * MLA_HEADS * MLA_NOPE), row(LANE, OFF_KRDT // LANE), row(LANE), row(LANE),
            vec, vec, vec, vec]


def _mla_prep_fwd(qraw, kvraw, proj, cos4, sin4, gqn, gqr, gkn, gkr, *, name):
    s = qraw.shape[0]
    ts = _pick(s, MLA_TS, 8)

    def body(q_ref, kv_ref, kr_ref, cos_ref, sin_ref, gqn_ref, gqr_ref, gkn_ref, gkr_ref, qo_ref, ko_ref):
        lane = lax.broadcasted_iota(jnp.int32, (ts, LANE), 1)
        lo = lane < 64
        first = (lane % 64) < 32
        cos, sin = cos_ref[...], sin_ref[...]
        kr = jnp.where(lo, kr_ref[...], 0.0)
        ssq_kr = jnp.sum(kr * kr, axis=-1, keepdims=True)

        def head(xn, xr, ssq_r, gn, gr):
            rs = lax.rsqrt((jnp.sum(xn * xn, axis=-1, keepdims=True) + ssq_r) * (1.0 / MLA_QK) + EPS)
            yr = xr * rs * gr
            return xn * rs * gn, yr * cos + _rope_swap(yr, first) * sin

        for h in range(MLA_HEADS):
            tile = q_ref[:, MLA_HEADS * MLA_NOPE + LANE * (h // 2):MLA_HEADS * MLA_NOPE + LANE * (h // 2) + LANE]
            qr = jnp.where(lo, tile if h % 2 == 0 else pltpu.roll(tile, 64, 1), 0.0)
            on, orr = head(q_ref[:, LANE * h:LANE * h + LANE], qr, jnp.sum(qr * qr, axis=-1, keepdims=True),
                           gqn_ref[...], gqr_ref[...])
            qo_ref[h, :, 0:LANE] = on.astype(BF16)
            qo_ref[h, :, LANE:QK_PAD] = orr.astype(BF16)
            on, orr = head(kv_ref[:, LANE * h:LANE * h + LANE], kr, ssq_kr, gkn_ref[...], gkr_ref[...])
            ko_ref[h, :, 0:LANE] = on.astype(BF16)
            ko_ref[h, :, LANE:QK_PAD] = orr.astype(BF16)

    out = pl.BlockSpec((MLA_HEADS, ts, QK_PAD), lambda i: (0, i, 0))
    sh = jax.ShapeDtypeStruct((MLA_HEADS, s, QK_PAD), BF16)
    return pl.pallas_call(
        body, name=name, grid=(s // ts,),
        in_specs=_mla_prep_specs(ts), out_specs=[out, out], out_shape=[sh, sh],
        compiler_params=_cp("parallel"),
    )(qraw, kvraw, proj, cos4, sin4, gqn, gqr, gkn, gkr)


def _mla_prep_bwd(qraw, kvraw, proj, cos4, sin4, gqn, gqr, gkn, gkr, dq, dk, *, name):
    s = qraw.shape[0]
    ts = _pick(s, MLA_TS, 8)

    def body(q_ref, kv_ref, kr_ref, cos_ref, sin_ref, gqn_ref, gqr_ref, gkn_ref, gkr_ref, dq_ref, dk_ref,
             dqraw_ref, dkn_ref, dkr_ref, dgqn_ref, dgqr_ref, dgkn_ref, dgkr_ref):
        i = pl.program_id(0)

        @pl.when(i == 0)
        def _():
            for r in (dgqn_ref, dgqr_ref, dgkn_ref, dgkr_ref):
                r[...] = jnp.zeros_like(r)

        lane = lax.broadcasted_iota(jnp.int32, (ts, LANE), 1)
        lo = lane < 64
        first = (lane % 64) < 32
        cos, sin = cos_ref[...], sin_ref[...]
        kr = jnp.where(lo, kr_ref[...], 0.0)
        ssq_kr = jnp.sum(kr * kr, axis=-1, keepdims=True)

        def head(xn, xr, ssq_r, gn, gr, don, dor):
            rs = lax.rsqrt((jnp.sum(xn * xn, axis=-1, keepdims=True) + ssq_r) * (1.0 / MLA_QK) + EPS)
            xhn, xhr = xn * rs, xr * rs
            dor = jnp.where(lo, dor, 0.0)
            dyr = dor * cos + _rope_swap(dor * sin, first)
            dxn, dxr = don * gn, dyr * gr
            mm = (jnp.sum(dxn * xhn, axis=-1, keepdims=True) + jnp.sum(dxr * xhr, axis=-1, keepdims=True)) * (1.0 / MLA_QK)
            return (rs * (dxn - xhn * mm), rs * (dxr - xhr * mm),
                    jnp.sum(don * xhn, axis=0, keepdims=True), jnp.sum(dyr * xhr, axis=0, keepdims=True))

        dkr_acc = jnp.zeros((ts, LANE), F32)
        prev = None
        for h in range(MLA_HEADS):
            c0 = MLA_HEADS * MLA_NOPE + LANE * (h // 2)
            tile = q_ref[:, c0:c0 + LANE]
            qr = jnp.where(lo, tile if h % 2 == 0 else pltpu.roll(tile, 64, 1), 0.0)
            dn, dr, gn_p, gr_p = head(q_ref[:, LANE * h:LANE * h + LANE], qr, jnp.sum(qr * qr, axis=-1, keepdims=True),
                                      gqn_ref[...], gqr_ref[...], dq_ref[h, :, 0:LANE], dq_ref[h, :, LANE:QK_PAD])
            dqraw_ref[:, LANE * h:LANE * h + LANE] = dn.astype(dqraw_ref.dtype)
            dgqn_ref[...] += gn_p
            dgqr_ref[...] += gr_p
            if h % 2 == 0:
                prev = dr
            else:
                dqraw_ref[:, c0:c0 + LANE] = (prev + pltpu.roll(dr, 64, 1)).astype(dqraw_ref.dtype)
            dn, dr, gn_p, gr_p = head(kv_ref[:, LANE * h:LANE * h + LANE], kr, ssq_kr, gkn_ref[...], gkr_ref[...],
                                      dk_ref[h, :, 0:LANE], dk_ref[h, :, LANE:QK_PAD])
            dkn_ref[:, LANE * h:LANE * h + LANE] = dn
            dkr_acc = dkr_acc + dr
            dgkn_ref[...] += gn_p
            dgkr_ref[...] += gr_p
        dkr_ref[...] = dkr_acc

    row = lambda w: pl.BlockSpec((ts, w), lambda i: (i, 0))
    vec = pl.BlockSpec((1, LANE), lambda i: (0, 0))
    dspec = pl.BlockSpec((MLA_HEADS, ts, QK_PAD), lambda i: (0, i, 0))
    vsh = jax.ShapeDtypeStruct((1, LANE), F32)
    return pl.pallas_call(
        body, name=name, grid=(s // ts,),
        in_specs=_mla_prep_specs(ts) + [dspec, dspec],
        out_specs=[row(MLA_HEADS * MLA_QK), row(MLA_HEADS * MLA_NOPE), row(LANE), vec, vec, vec, vec],
        out_shape=[jax.ShapeDtypeStruct((s, MLA_HEADS * MLA_QK), BF16), jax.ShapeDtypeStruct((s, MLA_HEADS * MLA_NOPE), F32),
                   jax.ShapeDtypeStruct((s, LANE), F32), vsh, vsh, vsh, vsh],
        compiler_params=_cp("arbitrary"),
    )(qraw, kvraw, proj, cos4, sin4, gqn, gqr, gkn, gkr, dq, dk)


ATT_T = 512
ATT_SCALE = MLA_QK ** -0.5


def _attn_fwd(q, k, kvraw, *, name):
    nh, s, _ = q.shape
    t = _pick(s, ATT_T, LANE)
    nb = s // t

    def body(q_ref, k_ref, v_ref, o_ref, lse_ref, m_ref, l_ref, acc_ref):
        i, j = pl.program_id(1), pl.program_id(2)

        @pl.when(j == 0)
        def _():
            m_ref[...] = jnp.full_like(m_ref, -jnp.inf)
            l_ref[...] = jnp.zeros_like(l_ref)
            acc_ref[...] = jnp.zeros_like(acc_ref)

        def step(diagonal):
            sc = _dot_nt(q_ref[0], k_ref[0]) * ATT_SCALE
            if diagonal:
                ri = lax.broadcasted_iota(jnp.int32, (t, t), 0)
                cj = lax.broadcasted_iota(jnp.int32, (t, t), 1)
                sc = jnp.where(ri >= cj, sc, -jnp.inf)
            m_new = jnp.maximum(m_ref[...], jnp.max(sc, axis=-1, keepdims=True))
            alpha = jnp.exp(m_ref[...] - m_new)
            p = jnp.exp(sc - m_new)
            l_ref[...] = alpha * l_ref[...] + jnp.sum(p, axis=-1, keepdims=True)
            acc_ref[...] = alpha * acc_ref[...] + _dot(p.astype(BF16), v_ref[...].astype(BF16))
            m_ref[...] = m_new

        @pl.when(j < i)
        def _():
            step(False)

        @pl.when(j == i)
        def _():
            step(True)
            o_ref[...] = acc_ref[...] / l_ref[...]
            lse_ref[0] = m_ref[...] + jnp.log(l_ref[...])

    return pl.pallas_call(
        body, name=name, grid=(nh, nb, nb),
        in_specs=[pl.BlockSpec((1, t, QK_PAD), lambda h, i, j: (h, i, 0)),
                  pl.BlockSpec((1, t, QK_PAD), lambda h, i, j: (h, jnp.minimum(j, i), 0)),
                  pl.BlockSpec((t, MLA_V), lambda h, i, j: (jnp.minimum(j, i), nh + h))],
        out_specs=[pl.BlockSpec((t, MLA_V), lambda h, i, j: (i, h)),
                   pl.BlockSpec((1, t, 1), lambda h, i, j: (h, i, 0))],
        out_shape=[jax.ShapeDtypeStruct((s, nh * MLA_V), F32), jax.ShapeDtypeStruct((nh, s, 1), F32)],
        scratch_shapes=[pltpu.VMEM((t, 1), F32), pltpu.VMEM((t, 1), F32), pltpu.VMEM((t, MLA_V), F32)],
        compiler_params=_cp("parallel", "parallel", "arbitrary"),
    )(q, k, kvraw)


def _attn_bwd(q, k, kvraw, o, lse, do, *, name):
    nh, s, _ = q.shape
    t = _pick(s, ATT_T, LANE)
    nb = s // t

    def body(q_ref, k_ref, v_ref, o_ref, lse_ref, do_ref, dq_ref, dk_ref, dv_ref, dk_acc, dv_acc):
        j, i = pl.program_id(1), pl.program_id(2)

        @pl.when(i == 0)
        def _():
            dk_acc[...] = jnp.zeros_like(dk_acc)
            dv_acc[...] = jnp.zeros_like(dv_acc)

        def step(diagonal):
            qv, kv = q_ref[0], k_ref[0]
            sc = _dot_nt(qv, kv) * ATT_SCALE
            if diagonal:
                ri = lax.broadcasted_iota(jnp.int32, (t, t), 0)
                cj = lax.broadcasted_iota(jnp.int32, (t, t), 1)
                sc = jnp.where(ri >= cj, sc, -jnp.inf)
            p = jnp.exp(sc - lse_ref[0])
            dov = do_ref[...]
            delta = jnp.sum(dov * o_ref[...], axis=-1, keepdims=True)
            do_b = dov.astype(BF16)
            dv_acc[...] += _dot_tn(p.astype(BF16), do_b)
            dp = _dot_nt(do_b, v_ref[...].astype(BF16))
            ds_b = (p * (dp - delta) * ATT_SCALE).astype(BF16)
            dk_acc[...] += _dot_tn(ds_b, qv)
            dq_part = _dot(ds_b, kv)
            rows = pl.ds(pl.multiple_of(i * t, t), t)

            @pl.when(j == 0)
            def _():
                dq_ref[0, rows, :] = dq_part

            @pl.when(j > 0)
            def _():
                dq_ref[0, rows, :] += dq_part

        @pl.when(i > j)
        def _():
            step(False)

        @pl.when(i == j)
        def _():
            step(True)

        @pl.when(i == nb - 1)
        def _():
            dk_ref[0] = dk_acc[...]
            dv_ref[...] = dv_acc[...]

    qi = lambda h, j, i: jnp.maximum(i, j)
    return pl.pallas_call(
        body, name=name, grid=(nh, nb, nb),
        in_specs=[pl.BlockSpec((1, t, QK_PAD), lambda h, j, i: (h, qi(h, j, i), 0)),
                  pl.BlockSpec((1, t, QK_PAD), lambda h, j, i: (h, j, 0)),
                  pl.BlockSpec((t, MLA_V), lambda h, j, i: (j, nh + h)),
                  pl.BlockSpec((t, MLA_V), lambda h, j, i: (qi(h, j, i), h)),
                  pl.BlockSpec((1, t, 1), lambda h, j, i: (h, qi(h, j, i), 0)),
                  pl.BlockSpec((t, MLA_V), lambda h, j, i: (qi(h, j, i), h))],
        out_specs=[pl.BlockSpec((1, s, QK_PAD), lambda h, j, i: (h, 0, 0)),
                   pl.BlockSpec((1, t, QK_PAD), lambda h, j, i: (h, j, 0)),
                   pl.BlockSpec((t, MLA_V), lambda h, j, i: (j, h))],
        out_shape=[jax.ShapeDtypeStruct((nh, s, QK_PAD), F32), jax.ShapeDtypeStruct((nh, s, QK_PAD), F32),
                   jax.ShapeDtypeStruct((s, nh * MLA_V), F32)],
        scratch_shapes=[pltpu.VMEM((t, QK_PAD), F32), pltpu.VMEM((t, MLA_V), F32)],
        compiler_params=_cp("parallel", "arbitrary", "arbitrary"),
    )(q, k, kvraw, o, lse, do)


def _ffn_fwd(h, w, tag):
    n = _rms_fwd(h, w["ln"], name=tag + "_norm")
    gu = _matmul(n, w["w13"], "nn", name=tag + "_up")
    act = _swiglu_fwd(gu, name=tag + "_act")
    out = _matmul(act, w["w2"], "nn", name=tag + "_down", scale=0.5, res=h)
    return out, (h, n, gu, act)


def _ffn_bwd(dout, saved, w, tag):
    h, n, gu, act = saved
    dact = _matmul(dout, w["w2"], "nt", name=tag + "_down_dx", scale=0.5)
    dw2 = _matmul(act, dout, "tn", name=tag + "_down_dw", scale=0.5)
    dgu = _swiglu_bwd(gu, dact, name=tag + "_act_bwd")
    dw13 = _matmul(n, dgu, "tn", name=tag + "_up_dw")
    dn = _matmul(dgu, w["w13"], "nt", name=tag + "_up_dx")
    dh, dln = _rms_bwd(h, w["ln"], dn, name=tag + "_norm_bwd", res=dout)
    return dh, dict(ln=dln, w13=dw13, w2=dw2)


def _mixer_fwd(h, w, rope, tag):
    cos4, sin4 = rope
    u = _rms_fwd(h, w["ln_mix"], name=tag + "_norm")
    proj = _matmul(u, w["w_in"], "nn", name=tag + "_in")
    xc = _conv_fwd(proj, w["conv_w"], w["conv_b"], name=tag + "_conv")
    dtr = proj[:, OFF_KRDT + MLA_ROPE:OFF_KRDT + MLA_ROPE + SSD_HEADS]
    dtrT = dtr.T
    y, states = _ssd_fwd(xc, dtr, dtrT, *w["ssd_aux"], name=tag + "_ssd")
    yn = _gated_rms_fwd(y, proj, w["ssd_norm"], name=tag + "_ssd_norm")
    y_ssd = _matmul(yn, w["w_ssd_out"], "nn", name=tag + "_ssd_out")
    cqn = _rms_fwd(proj, w["q_lora_norm"], name=tag + "_q_lora_norm", col=OFF_CQ // MLA_Q_LORA, width=MLA_Q_LORA)
    qraw = _matmul(cqn, w["w_uq"], "nn", name=tag + "_uq")
    ckvn = _rms_fwd(proj, w["kv_lora_norm"], name=tag + "_kv_lora_norm", col=OFF_CKV // MLA_KV_LORA, width=MLA_KV_LORA)
    kvraw = _matmul(ckvn, w["w_ukv"], "nn", name=tag + "_ukv")
    qf, kf = _mla_prep_fwd(qraw, kvraw, proj, cos4, sin4, *w["qk_gains"], name=tag + "_qk_prep")
    o, lse = _attn_fwd(qf, kf, kvraw, name=tag + "_attn")
    y_mla = _matmul(o, w["w_mla_out"], "nn", name=tag + "_mla_out")
    merged = _merge_fwd(proj, y_ssd, y_mla, name=tag + "_merge")
    out = _matmul(merged, w["w_o"], "nn", name=tag + "_o", res=h)
    saved = dict(h=h, u=u, proj=proj, xc=xc, dtr=dtr, dtrT=dtrT, states=states, y=y, yn=yn, y_ssd=y_ssd, cqn=cqn,
                 qraw=qraw, ckvn=ckvn, kvraw=kvraw, qf=qf, kf=kf, o=o, lse=lse, y_mla=y_mla, merged=merged)
    return out, saved


def _mixer_bwd(dout, s, w, rope, tag):
    cos4, sin4 = rope
    g = {}
    proj = s["proj"]
    dmerged = _matmul(dout, w["w_o"], "nt", name=tag + "_o_dx")
    g["w_o"] = _matmul(s["merged"], dout, "tn", name=tag + "_o_dw")
    dgates, dy_ssd, dy_mla = _merge_bwd(proj, s["y_ssd"], s["y_mla"], dmerged, name=tag + "_merge_bwd")
    do = _matmul(dy_mla, w["w_mla_out"], "nt", name=tag + "_mla_out_dx")
    g["w_mla_out"] = _matmul(s["o"], dy_mla, "tn", name=tag + "_mla_out_dw")
    dqf, dkf, dv = _attn_bwd(s["qf"], s["kf"], s["kvraw"], s["o"], s["lse"], do, name=tag + "_attn_bwd")
    dqraw, dkn, dkrt, dgqn, dgqr, dgkn, dgkr = _mla_prep_bwd(
        s["qraw"], s["kvraw"], proj, cos4, sin4, *w["qk_gains"], dqf, dkf, name=tag + "_qk_prep_bwd")
    g["q_norm"] = jnp.concatenate([dgqn[0], dgqr[0, :MLA_ROPE]])
    g["k_norm"] = jnp.concatenate([dgkn[0], dgkr[0, :MLA_ROPE]])
    dkvraw = jnp.concatenate([dkn, dv], axis=1).astype(BF16)
    dcqn = _matmul(dqraw, w["w_uq"], "nt", name=tag + "_uq_dx")
    g["w_uq"] = _matmul(s["cqn"], dqraw, "tn", name=tag + "_uq_dw")
    dckvn = _matmul(dkvraw, w["w_ukv"], "nt", name=tag + "_ukv_dx")
    g["w_ukv"] = _matmul(s["ckvn"], dkvraw, "tn", name=tag + "_ukv_dw")
    dcq, g["q_lora_norm"] = _rms_bwd(proj, w["q_lora_norm"], dcqn, name=tag + "_q_lora_norm_bwd",
                                     col=OFF_CQ // MLA_Q_LORA, width=MLA_Q_LORA)
    dckv, g["kv_lora_norm"] = _rms_bwd(proj, w["kv_lora_norm"], dckvn, name=tag + "_kv_lora_norm_bwd",
                                       col=OFF_CKV // MLA_KV_LORA, width=MLA_KV_LORA)
    dyn = _matmul(dy_ssd, w["w_ssd_out"], "nt", name=tag + "_ssd_out_dx")
    g["w_ssd_out"] = _matmul(s["yn"], dy_ssd, "tn", name=tag + "_ssd_out_dw")
    dy, dz, g["ssd_norm"] = _gated_rms_bwd(s["y"], proj, w["ssd_norm"], dyn, name=tag + "_ssd_norm_bwd")
    aux = w["ssd_aux"]
    dxc, ddt_c, ddt_r, dbias_c, dbias_r, da_c, da_r, ddsk = _ssd_bwd(
        s["xc"], s["dtr"], s["dtrT"], *aux, aux[-1].T, s["states"], dy, name=tag + "_ssd_bwd")
    g["dt_bias"] = dbias_c[0] + dbias_r[:, 0]
    g["a_log"] = (da_c[0] + da_r[:, 0]) * (-jnp.exp(aux[2][0]))
    g["d_skip"] = jnp.sum(ddsk.reshape(SSD_HEADS, SSD_HEAD_DIM), axis=1)
    dpre, g["conv_w"], g["conv_b"] = _conv_bwd_pre(proj, w["conv_w"], w["conv_b"], dxc, name=tag + "_conv_bwd_pre")
    dxbc = _conv_bwd_x(dpre, w["conv_w"], name=tag + "_conv_bwd_x")
    ddtr = ddt_c + ddt_r.T
    dkrdt = jnp.concatenate([dkrt[:, :MLA_ROPE], ddtr, jnp.zeros((ddtr.shape[0], LANE - MLA_ROPE - SSD_HEADS), F32)], axis=1)
    dproj = jnp.concatenate([dz, dxbc, dgates, dcq, dckv, dkrdt], axis=1).astype(BF16)
    du = _matmul(dproj, w["w_in"], "nt", name=tag + "_in_dx")
    g["w_in"] = _matmul(s["u"], dproj, "tn", name=tag + "_in_dw")
    dh, g["ln_mix"] = _rms_bwd(s["h"], w["ln_mix"], du, name=tag + "_norm_bwd", res=dout)
    return dh, g


W_NAMES = ["ln_ffn1", "ffn1_w13", "ffn1_w2", "ln_mix", "w_in", "conv_w", "conv_b", "dt_bias", "a_log", "d_skip",
           "ssd_norm", "w_ssd_out", "q_lora_norm", "w_uq", "kv_lora_norm", "w_ukv", "q_norm", "k_norm", "w_mla_out",
           "w_o", "ln_ffn2", "ffn2_w13", "ffn2_w2"]
SHARD_AXIS = {"ffn1_w13": 2, "ffn1_w2": 1, "w_in": 2, "conv_w": 2, "w_ssd_out": 1, "w_uq": 2, "w_ukv": 2,
              "w_mla_out": 1, "w_o": 1, "ffn2_w13": 2, "ffn2_w2": 1}
SHARDED = [n for n in W_NAMES if n in SHARD_AXIS]
REPLICATED = [n for n in W_NAMES if n not in SHARD_AXIS]
N_CHIPS = 4
N_DEV = 8
PACK_COLS = 1024
IN_SPLIT = (2048, 3072, 32, 512, 256, 64, 2048)


def _pack(arrs, rows, dtype):
    flat = jnp.concatenate([a.astype(dtype).reshape(-1) for a in arrs])
    return jnp.pad(flat, (0, rows * PACK_COLS - flat.shape[0])).reshape(rows, PACK_COLS)


def _unpack(packed, shapes):
    flat = packed.reshape(-1)
    out, at = [], 0
    for sh in shapes:
        n = math.prod(sh)
        out.append(flat[at:at + n].reshape(sh))
        at += n
    return out


def _pack_rows(shapes):
    n = sum(math.prod(sh) for sh in shapes)
    return -(-n // (PACK_COLS * 1024)) * 1024


def _in_perm(w_in):
    z, xbc, dt, cq, ckv, kr, gates = jnp.split(w_in, list(np_cumsum(IN_SPLIT))[:-1], axis=1)
    return jnp.concatenate([z, xbc, gates, cq, ckv, kr, dt, jnp.zeros((w_in.shape[0], PROJ_W - sum(IN_SPLIT)), w_in.dtype)], axis=1)


def _in_unperm(g):
    z, xbc, gates, cq, ckv = (g[:, OFF_Z:OFF_XBC], g[:, OFF_XBC:OFF_GATES], g[:, OFF_GATES:OFF_CQ], g[:, OFF_CQ:OFF_CKV],
                              g[:, OFF_CKV:OFF_KRDT])
    kr = g[:, OFF_KRDT:OFF_KRDT + MLA_ROPE]
    dt = g[:, OFF_KRDT + MLA_ROPE:OFF_KRDT + MLA_ROPE + SSD_HEADS]
    return jnp.concatenate([z, xbc, dt, cq, ckv, kr, gates], axis=1)


def np_cumsum(sizes):
    out, t = [], 0
    for s in sizes:
        t += s
        out.append(t)
    return out


def _head_perm(w, first):
    r = w.shape[0]
    w3 = w.reshape(r, MLA_HEADS, -1)
    return jnp.concatenate([w3[:, :, :first].reshape(r, -1), w3[:, :, first:].reshape(r, -1)], axis=1)


def _head_unperm(g, first):
    r = g.shape[0]
    rest = g.shape[1] // MLA_HEADS - first
    a = g[:, :MLA_HEADS * first].reshape(r, MLA_HEADS, first)
    b = g[:, MLA_HEADS * first:].reshape(r, MLA_HEADS, rest)
    return jnp.concatenate([a, b], axis=2).reshape(r, -1)


def _layer_weights(full, l):
    row = lambda n: full[n][l][None].astype(F32)
    expand = jnp.repeat(jnp.eye(SSD_HEADS, dtype=F32), SSD_HEAD_DIM, axis=1)
    dtb, al, dsk = full["dt_bias"][l], full["a_log"][l], full["d_skip"][l]
    mixer = dict(
        ln_mix=row("ln_mix"), w_in=_in_perm(full["w_in"][l]), conv_w=full["conv_w"][l], conv_b=row("conv_b"),
        ssd_aux=(dtb[None], dtb[:, None], al[None], al[:, None], jnp.repeat(dsk, SSD_HEAD_DIM)[None], expand),
        ssd_norm=row("ssd_norm"), w_ssd_out=full["w_ssd_out"][l],
        q_lora_norm=row("q_lora_norm"), w_uq=_head_perm(full["w_uq"][l], MLA_NOPE),
        kv_lora_norm=row("kv_lora_norm"), w_ukv=_head_perm(full["w_ukv"][l], MLA_NOPE),
        qk_gains=_mla_gains(full["q_norm"][l], full["k_norm"][l]),
        w_mla_out=full["w_mla_out"][l], w_o=full["w_o"][l])
    ffn1 = dict(ln=row("ln_ffn1"), w13=full["ffn1_w13"][l], w2=full["ffn1_w2"][l])
    ffn2 = dict(ln=row("ln_ffn2"), w13=full["ffn2_w13"][l], w2=full["ffn2_w2"][l])
    return ffn1, mixer, ffn2


def _layer_grads(g1, gm, g2):
    return {
        "ln_ffn1": g1["ln"][0], "ffn1_w13": g1["w13"], "ffn1_w2": g1["w2"],
        "ln_mix": gm["ln_mix"][0], "w_in": _in_unperm(gm["w_in"]), "conv_w": gm["conv_w"], "conv_b": gm["conv_b"][0],
        "dt_bias": gm["dt_bias"], "a_log": gm["a_log"], "d_skip": gm["d_skip"], "ssd_norm": gm["ssd_norm"][0],
        "w_ssd_out": gm["w_ssd_out"], "q_lora_norm": gm["q_lora_norm"][0], "w_uq": _head_unperm(gm["w_uq"], MLA_NOPE),
        "kv_lora_norm": gm["kv_lora_norm"][0], "w_ukv": _head_unperm(gm["w_ukv"], MLA_NOPE),
        "q_norm": gm["q_norm"], "k_norm": gm["k_norm"], "w_mla_out": gm["w_mla_out"], "w_o": gm["w_o"],
        "ln_ffn2": g2["ln"][0], "ffn2_w13": g2["w13"], "ffn2_w2": g2["w2"],
    }


def _local_step(x, positions, loss_target, full):
    rope = _rope_tables4(positions)
    lw = [_layer_weights(full, l) for l in range(DEPTH)]
    h = x
    saved = []
    for l in range(DEPTH):
        f1, mx, f2 = lw[l]
        h, s1 = _ffn_fwd(h, f1, f"l{l}_ffn1")
        h, sm = _mixer_fwd(h, mx, rope, f"l{l}_mix")
        h, s2 = _ffn_fwd(h, f2, f"l{l}_ffn2")
        saved.append((s1, sm, s2))
    loss_part, dh = _loss_fwd_bwd(h, loss_target, name="loss")
    grads = [None] * DEPTH
    for l in reversed(range(DEPTH)):
        f1, mx, f2 = lw[l]
        s1, sm, s2 = saved[l]
        dh, g2 = _ffn_bwd(dh, s2, f2, f"l{l}_ffn2")
        dh, gm = _mixer_bwd(dh, sm, mx, rope, f"l{l}_mix")
        dh, g1 = _ffn_bwd(dh, s1, f1, f"l{l}_ffn1")
        grads[l] = _layer_grads(g1, gm, g2)
    full_grads = {n: jnp.stack([grads[l][n] for l in range(DEPTH)]) for n in W_NAMES}
    return loss_part, dh, full_grads


MESH = pl.DeviceIdType.MESH
ANY = pl.BlockSpec(memory_space=pl.ANY)


def _place():
    return lax.axis_index("x"), lax.axis_index("y"), lax.axis_index("c")


def _other_chips(x, y):
    return [(1 - x, y), (x, 1 - y), (1 - x, 1 - y)]


def _remote(src, dst, send_sems, recv_sems, k, to):
    return pltpu.make_async_remote_copy(src_ref=src, dst_ref=dst, send_sem=send_sems.at[k], recv_sem=recv_sems.at[k],
                                        device_id=to, device_id_type=MESH)


N_PARTS = 8


def _parts(rows):
    size = rows // N_PARTS
    assert size * N_PARTS == rows and size % 16 == 0, rows
    return [(p * size, size) for p in range(N_PARTS)]


def _rows(ref, lead, base, start, size):
    return ref.at[(*lead, pl.ds(pl.multiple_of(base + start, 16), size), slice(None))]


def _gather_shards(packed, *, name):
    r, ncol = packed.shape
    hr = r // 2
    parts = _parts(hr)

    def body(x_ref, out_ref, send_sems, recv_sems, local_sem):
        x, y, c = _place()
        chips = _other_chips(x, y)
        me = 2 * x + y

        def half(chip, cc):
            return _rows(out_ref, (2 * chip[0] + chip[1],), cc * hr, 0, hr)

        for cc in (0, 1):
            for st, sz in parts:
                pltpu.make_async_copy(_rows(x_ref, (), cc * hr, st, sz), _rows(out_ref, (me,), cc * hr, st, sz),
                                      local_sem.at[0]).start()
        for j, chip in enumerate(chips):
            for st, sz in parts:
                _remote(_rows(x_ref, (), c * hr, st, sz), _rows(out_ref, (me,), c * hr, st, sz), send_sems, recv_sems, j,
                        (*chip, c)).start()
        for j, chip in enumerate(chips):
            _remote(half(chip, c), half(chip, c), send_sems, recv_sems, j, (x, y, c)).wait_recv()
            slot = 2 * chip[0] + chip[1]
            for st, sz in parts:
                _remote(_rows(out_ref, (slot,), c * hr, st, sz), _rows(out_ref, (slot,), c * hr, st, sz), send_sems,
                        recv_sems, 3 + j, (x, y, 1 - c)).start()
        for j, chip in enumerate(chips):
            _remote(half(chip, 1 - c), half(chip, 1 - c), send_sems, recv_sems, 3 + j, (x, y, c)).wait_recv()
        for k in range(6):
            _remote(half((x, y), c), half((x, y), c), send_sems, recv_sems, k, (x, y, c)).wait_send()
        pltpu.make_async_copy(x_ref, out_ref.at[me], local_sem.at[0]).wait()

    return pl.pallas_call(
        body, name=name,
        out_shape=jax.ShapeDtypeStruct((N_CHIPS, r, ncol), packed.dtype),
        in_specs=[ANY], out_specs=ANY,
        scratch_shapes=[pltpu.SemaphoreType.DMA((6,)), pltpu.SemaphoreType.DMA((6,)), pltpu.SemaphoreType.DMA((1,))],
    )(packed)


def _swap_halves(g, *, name):
    n, r, ncol = g.shape
    hr = r // 2
    parts = _parts(hr)

    def body(g_ref, own_ref, got_ref, send_sems, recv_sems, local_sem):
        x, y, c = _place()
        for s in range(n):
            for st, sz in parts:
                pltpu.make_async_copy(_rows(g_ref, (s,), c * hr, st, sz), own_ref.at[s, pl.ds(st, sz), :],
                                      local_sem.at[0]).start()
                _remote(_rows(g_ref, (s,), (1 - c) * hr, st, sz), got_ref.at[s, pl.ds(st, sz), :], send_sems, recv_sems, 0,
                        (x, y, 1 - c)).start()
        _remote(got_ref, got_ref, send_sems, recv_sems, 0, (x, y, c)).wait()
        pltpu.make_async_copy(own_ref, own_ref, local_sem.at[0]).wait()

    sh = jax.ShapeDtypeStruct((n, hr, ncol), g.dtype)
    return pl.pallas_call(
        body, name=name, out_shape=[sh, sh], in_specs=[ANY], out_specs=[ANY, ANY],
        scratch_shapes=[pltpu.SemaphoreType.DMA((1,)), pltpu.SemaphoreType.DMA((1,)), pltpu.SemaphoreType.DMA((1,))],
    )(g)


def _scatter_to_chips(a, *, name):
    n, r, ncol = a.shape
    parts = _parts(r)

    def body(a_ref, mine_ref, got_ref, send_sems, recv_sems, local_sem):
        x, y, c = _place()
        for st, sz in parts:
            pltpu.make_async_copy(a_ref.at[2 * x + y, pl.ds(st, sz), :], mine_ref.at[pl.ds(st, sz), :],
                                  local_sem.at[0]).start()
        for st, sz in parts:
            for j, chip in enumerate(_other_chips(x, y)):
                _remote(a_ref.at[2 * chip[0] + chip[1], pl.ds(st, sz), :], got_ref.at[j, pl.ds(st, sz), :], send_sems,
                        recv_sems, j, (*chip, c)).start()
        for j in range(n - 1):
            _remote(got_ref.at[j], got_ref.at[j], send_sems, recv_sems, j, (x, y, c)).wait()
        pltpu.make_async_copy(mine_ref, mine_ref, local_sem.at[0]).wait()

    return pl.pallas_call(
        body, name=name,
        out_shape=[jax.ShapeDtypeStruct((r, ncol), a.dtype), jax.ShapeDtypeStruct((n - 1, r, ncol), a.dtype)],
        in_specs=[ANY], out_specs=[ANY, ANY],
        scratch_shapes=[pltpu.SemaphoreType.DMA((3,)), pltpu.SemaphoreType.DMA((3,)), pltpu.SemaphoreType.DMA((1,))],
    )(a)


def _join_halves(b, *, name):
    hr, ncol = b.shape
    parts = _parts(hr)

    def body(b_ref, out_ref, send_sems, recv_sems, local_sem):
        x, y, c = _place()
        for st, sz in parts:
            pltpu.make_async_copy(b_ref.at[pl.ds(st, sz), :], _rows(out_ref, (), c * hr, st, sz), local_sem.at[0]).start()
            _remote(b_ref.at[pl.ds(st, sz), :], _rows(out_ref, (), c * hr, st, sz), send_sems, recv_sems, 0,
                    (x, y, 1 - c)).start()
        theirs = _rows(out_ref, (), (1 - c) * hr, 0, hr)
        _remote(b_ref, theirs, send_sems, recv_sems, 0, (x, y, c)).wait()
        pltpu.make_async_copy(b_ref, theirs, local_sem.at[0]).wait()

    return pl.pallas_call(
        body, name=name, out_shape=jax.ShapeDtypeStruct((2 * hr, ncol), b.dtype), in_specs=[ANY], out_specs=ANY,
        scratch_shapes=[pltpu.SemaphoreType.DMA((1,)), pltpu.SemaphoreType.DMA((1,)), pltpu.SemaphoreType.DMA((1,))],
    )(b)


def _sum_rows(arrs, *, name):
    r, ncol = arrs[0].shape
    tr = _pick(r, 512, 8)

    def body(*refs):
        acc = refs[0][...]
        for ref in refs[1:-1]:
            acc = acc + ref[...]
        refs[-1][...] = acc

    blk = pl.BlockSpec((tr, ncol), lambda i: (i, 0))
    return pl.pallas_call(
        body, name=name, grid=(r // tr,), in_specs=[blk] * len(arrs), out_specs=blk,
        out_shape=jax.ShapeDtypeStruct((r, ncol), arrs[0].dtype), compiler_params=_cp("parallel"),
    )(*arrs)


def _reduce_scatter(g, *, name):
    n, r, ncol = g.shape
    hr = r // 2
    own, got = _swap_halves(g, name=name + "_swap")
    chip_sum = _sum_rows([own.reshape(n * hr, ncol), got.reshape(n * hr, ncol)], name=name + "_add_cores")
    mine, others = _scatter_to_chips(chip_sum.reshape(n, hr, ncol), name=name + "_scatter")
    total = _sum_rows([mine, others[0], others[1], others[2]], name=name + "_add_chips")
    return _join_halves(total, name=name + "_join")


def _all_gather_small(v, *, name):
    r, ncol = v.shape

    def body(x_ref, out_ref, send_sems, recv_sems, local_sem):
        x, y, c = _place()
        me, sibling = (x, y, c), (x, y, 1 - c)
        chips = _other_chips(x, y)

        def slot(p):
            return out_ref.at[4 * p[0] + 2 * p[1] + p[2]]

        mine = pltpu.make_async_copy(x_ref, slot(me), local_sem.at[0])
        mine.start()
        first = [_remote(x_ref, slot(me), send_sems, recv_sems, 0, sibling)]
        first += [_remote(x_ref, slot(me), send_sems, recv_sems, 1 + j, (*chip, c)) for j, chip in enumerate(chips)]
        for cp in first:
            cp.start()
        passed = [_remote(slot((*chip, c)), slot((*chip, c)), send_sems, recv_sems, 4 + j, sibling)
                  for j, chip in enumerate(chips)]
        for j, chip in enumerate(chips):
            _remote(slot((*chip, c)), slot((*chip, c)), send_sems, recv_sems, 1 + j, me).wait_recv()
            passed[j].start()
        _remote(slot(sibling), slot(sibling), send_sems, recv_sems, 0, me).wait_recv()
        for j, chip in enumerate(chips):
            _remote(slot((*chip, 1 - c)), slot((*chip, 1 - c)), send_sems, recv_sems, 4 + j, me).wait_recv()
        for cp in first + passed:
            cp.wait_send()
        mine.wait()

    vm = pl.BlockSpec(memory_space=pltpu.VMEM)
    return pl.pallas_call(
        body, name=name, out_shape=jax.ShapeDtypeStruct((N_DEV, r, ncol), v.dtype), in_specs=[vm], out_specs=vm,
        scratch_shapes=[pltpu.SemaphoreType.DMA((7,)), pltpu.SemaphoreType.DMA((7,)), pltpu.SemaphoreType.DMA((1,))],
    )(v)


def _sum_slots(g8, *, name):
    n, r, ncol = g8.shape

    def body(g_ref, o_ref):
        acc = g_ref[0]
        for k in range(1, n):
            acc = acc + g_ref[k]
        o_ref[...] = acc

    return pl.pallas_call(body, name=name, out_shape=jax.ShapeDtypeStruct((r, ncol), g8.dtype))(g8)


def _step(a):
    x = a["x"][0]
    s = x.shape[0]
    del s
    shard_shapes = [a[n].shape for n in SHARDED]
    rows = _pack_rows(shard_shapes)

    gathered = _gather_shards(_pack([a[n] for n in SHARDED], rows, BF16), name="gather_weights")
    conv_rows = -(-math.prod(a["conv_w"].shape) // (LANE * 8)) * 8
    conv_all = _all_gather_small(
        jnp.pad(a["conv_w"].reshape(-1), (0, conv_rows * LANE - math.prod(a["conv_w"].shape))).reshape(conv_rows, LANE),
        name="gather_conv_w")
    per_chip = [dict(zip(SHARDED, _unpack(gathered[k], shard_shapes))) for k in range(N_CHIPS)]
    full = {n: jnp.concatenate([per_chip[k][n] for k in range(N_CHIPS)], axis=SHARD_AXIS[n]) for n in SHARDED}
    full["conv_w"] = jnp.concatenate(
        [conv_all[2 * k].reshape(-1)[:math.prod(a["conv_w"].shape)].reshape(a["conv_w"].shape) for k in range(N_CHIPS)],
        axis=SHARD_AXIS["conv_w"])
    for n in REPLICATED:
        full[n] = a[n]

    loss_part, grad_x, grads = _local_step(x, a["positions"][0], a["loss_target"][0], full)
    loss = lax.psum(jnp.sum(loss_part), ("x", "y", "c"))

    slots = []
    for k in range(N_CHIPS):
        parts = [jnp.split(grads[n], N_CHIPS, axis=SHARD_AXIS[n])[k] for n in SHARDED]
        slots.append(_pack(parts, rows, F32))
    g_shard = _reduce_scatter(jnp.stack(slots), name="reduce_grads")

    rep_shapes = [a[n].shape for n in REPLICATED]
    n_rep = sum(math.prod(sh) for sh in rep_shapes)
    rep_rows = -(-n_rep // (LANE * 8)) * 8
    pack_small = lambda arrs: jnp.pad(jnp.concatenate([t.reshape(-1) for t in arrs]), (0, rep_rows * LANE - n_rep)).reshape(rep_rows, LANE)
    g_rep = _sum_slots(_all_gather_small(pack_small([grads[n] for n in REPLICATED]), name="gather_small_grads"),
                       name="add_small_grads")

    out = {"loss": loss, "grad_x": grad_x[None]}
    for n, g in zip(SHARDED, _unpack(g_shard, shard_shapes)):
        flat = lambda t: t.reshape(-1, t.shape[-1])
        d, nm, nv = _adamw(flat(a[n]), flat(g), flat(a["m_" + n]), flat(a["v_" + n]), name="adamw_" + n)
        out["grad_" + n] = g
        out["delta_" + n], out["new_m_" + n], out["new_v_" + n] = (t.reshape(g.shape) for t in (d, nm, nv))
    d_rp, m_rp, v_rp = _adamw(pack_small([a[n] for n in REPLICATED]), g_rep,
                              pack_small([a["m_" + n] for n in REPLICATED]),
                              pack_small([a["v_" + n] for n in REPLICATED]), name="adamw_replicated")
    for prefix, rp_arr in (("grad_", g_rep), ("delta_", d_rp), ("new_m_", m_rp), ("new_v_", v_rp)):
        for n, t in zip(REPLICATED, _unpack(rp_arr.reshape(-1)[:n_rep], rep_shapes)):
            out[prefix + n] = t
    return out


IN_NAMES = ["x", "positions"] + W_NAMES + ["loss_target"] + ["m_" + n for n in W_NAMES] + ["v_" + n for n in W_NAMES]
OUT_NAMES = (["loss", "grad_x"] + ["grad_" + n for n in W_NAMES] + ["delta_" + n for n in W_NAMES]
             + ["new_m_" + n for n in W_NAMES] + ["new_v_" + n for n in W_NAMES])


def kernel(x, positions, ln_ffn1, ffn1_w13, ffn1_w2, ln_mix, w_in, conv_w, conv_b, dt_bias, a_log, d_skip, ssd_norm, w_ssd_out, q_lora_norm, w_uq, kv_lora_norm, w_ukv, q_norm, k_norm, w_mla_out, w_o, ln_ffn2, ffn2_w13, ffn2_w2, loss_target, m_ln_ffn1, m_ffn1_w13, m_ffn1_w2, m_ln_mix, m_w_in, m_conv_w, m_conv_b, m_dt_bias, m_a_log, m_d_skip, m_ssd_norm, m_w_ssd_out, m_q_lora_norm, m_w_uq, m_kv_lora_norm, m_w_ukv, m_q_norm, m_k_norm, m_w_mla_out, m_w_o, m_ln_ffn2, m_ffn2_w13, m_ffn2_w2, v_ln_ffn1, v_ffn1_w13, v_ffn1_w2, v_ln_mix, v_w_in, v_conv_w, v_conv_b, v_dt_bias, v_a_log, v_d_skip, v_ssd_norm, v_w_ssd_out, v_q_lora_norm, v_w_uq, v_kv_lora_norm, v_w_ukv, v_q_norm, v_k_norm, v_w_mla_out, v_w_o, v_ln_ffn2, v_ffn2_w13, v_ffn2_w2):
    given = locals()
    out = _step({n: given[n] for n in IN_NAMES})
    return tuple(out[n] for n in OUT_NAMES)
```

```python
import functools
import math

import jax
import jax.numpy as jnp
from jax import lax
from jax.experimental import pallas as pl
from jax.experimental.pallas import tpu as pltpu

F32 = jnp.float32
BF16 = jnp.bfloat16

D_MODEL = 1024
DEPTH = 2
D_FF = 2816
SSD_D_INNER = 2048
SSD_HEADS = 32
SSD_HEAD_DIM = 64
SSD_GROUPS = 4
SSD_STATE = 128
SSD_CHUNK = 128
SSD_CONV = 4
SSD_CONV_DIM = 3072
MLA_HEADS = 8
MLA_Q_LORA = 512
MLA_KV_LORA = 256
MLA_NOPE = 128
MLA_ROPE = 64
MLA_V = 128
MLA_QK = 192
ROPE_THETA = 10000.0
EPS = 1e-6
ADAM_LR = 0.001
ADAM_B1 = 0.9
ADAM_B2 = 0.999
ADAM_EPS = 1e-08
ADAM_WD = 0.01
ADAM_STEP = 10

PROJ_W = 8064
OFF_Z, OFF_XBC, OFF_GATES, OFF_CQ, OFF_CKV, OFF_KRDT = 0, 2048, 5120, 7168, 7680, 7936

LANE = 128
VMEM_LIMIT = 48 * 1024 * 1024
HI = lax.Precision.HIGHEST


def _cp(*sem):
    return pltpu.CompilerParams(dimension_semantics=sem, vmem_limit_bytes=VMEM_LIMIT)


def _pick(dim, target, align):
    if dim <= target:
        return dim
    b = (target // align) * align
    while b >= align:
        if dim % b == 0:
            return b
        b -= align
    raise ValueError(f"no block for {dim} (target {target}, align {align})")


def _silu(x):
    return x * jax.nn.sigmoid(x)


def _dsilu(x):
    s = jax.nn.sigmoid(x)
    return s * (1.0 + x * (1.0 - s))


def _matmul(a, b, mode, *, name, out_dtype=F32, scale=1.0, res=None):
    if mode == "nn":
        (m, k), (k2, n) = a.shape, b.shape
    elif mode == "nt":
        (m, k), (n, k2) = a.shape, b.shape
    else:
        (k, m), (k2, n) = a.shape, b.shape
    assert k == k2, (a.shape, b.shape, mode)
    if mode == "tn":
        bn, bk = _pick(n, 2816, LANE), _pick(k, 512, 8)
        bm = _pick(m, max(256, (1408 * 1024 // bn) // LANE * LANE), LANE)
    else:
        bm, bn, bk = _pick(m, 1024, 8), _pick(n, 1152, LANE), _pick(k, 1536, LANE)
    nk = k // bk

    def body(a_ref, b_ref, *rest):
        res_ref = rest[0] if res is not None else None
        o_ref = rest[-2] if nk > 1 else rest[-1]
        kk = pl.program_id(2)
        av = a_ref[...].astype(BF16)
        bv = b_ref[...].astype(BF16)
        if mode == "nn":
            dims = (((1,), (0,)), ((), ()))
        elif mode == "nt":
            dims = (((1,), (1,)), ((), ()))
        else:
            dims = (((0,), (0,)), ((), ()))
        part = lax.dot_general(av, bv, dims, preferred_element_type=F32)

        def finish(total):
            out = total * scale
            if res_ref is not None:
                out = res_ref[...] + out
            o_ref[...] = out.astype(o_ref.dtype)

        if nk == 1:
            finish(part)
            return
        acc_ref = rest[-1]

        @pl.when(kk == 0)
        def _():
            acc_ref[...] = part

        @pl.when((kk > 0) & (kk < nk - 1))
        def _():
            acc_ref[...] += part

        @pl.when(kk == nk - 1)
        def _():
            finish(acc_ref[...] + part)

    o_spec = pl.BlockSpec((bm, bn), lambda i, j, kk: (i, j))
    if mode == "nn":
        a_spec = pl.BlockSpec((bm, bk), lambda i, j, kk: (i, kk))
        b_spec = pl.BlockSpec((bk, bn), lambda i, j, kk: (kk, j))
    elif mode == "nt":
        a_spec = pl.BlockSpec((bm, bk), lambda i, j, kk: (i, kk))
        b_spec = pl.BlockSpec((bn, bk), lambda i, j, kk: (j, kk))
    else:
        a_spec = pl.BlockSpec((bk, bm), lambda i, j, kk: (kk, i))
        b_spec = pl.BlockSpec((bk, bn), lambda i, j, kk: (kk, j))
    return pl.pallas_call(
        body, name=name,
        grid=(m // bm, n // bn, nk),
        in_specs=[a_spec, b_spec] + ([o_spec] if res is not None else []),
        out_specs=o_spec,
        out_shape=jax.ShapeDtypeStruct((m, n), out_dtype),
        scratch_shapes=[pltpu.VMEM((bm, bn), F32)] if nk > 1 else [],
        compiler_params=_cp("parallel", "parallel", "arbitrary"),
    )(*((a, b) + ((res,) if res is not None else ())))


def _rms_fwd(x, g, *, name, col=0, width=None):
    r = x.shape[0]
    w = width or x.shape[1]
    tr = _pick(r, 512, 16)

    def body(x_ref, g_ref, o_ref):
        xv = x_ref[...]
        rs = lax.rsqrt(jnp.mean(xv * xv, axis=-1, keepdims=True) + EPS)
        o_ref[...] = (xv * rs * g_ref[...]).astype(o_ref.dtype)

    return pl.pallas_call(
        body, name=name, grid=(r // tr,),
        in_specs=[pl.BlockSpec((tr, w), lambda i: (i, col)), pl.BlockSpec((1, w), lambda i: (0, 0))],
        out_specs=pl.BlockSpec((tr, w), lambda i: (i, 0)),
        out_shape=jax.ShapeDtypeStruct((r, w), BF16),
        compiler_params=_cp("parallel"),
    )(x, g)


def _rms_bwd(x, g, dy, *, name, col=0, width=None, res=None):
    r = x.shape[0]
    w = width or x.shape[1]
    tr = _pick(r, 512, 8)

    def body(x_ref, g_ref, dy_ref, *rest):
        res_ref = rest[0] if res is not None else None
        dx_ref, dg_ref = rest[-2:]
        i = pl.program_id(0)
        xv = x_ref[...]
        dyv = dy_ref[...]
        rs = lax.rsqrt(jnp.mean(xv * xv, axis=-1, keepdims=True) + EPS)
        xh = xv * rs
        dxh = dyv * g_ref[...]
        mm = jnp.mean(dxh * xh, axis=-1, keepdims=True)
        dx = rs * (dxh - xh * mm)
        if res_ref is not None:
            dx = res_ref[...] + dx
        dx_ref[...] = dx
        part = jnp.sum(dyv * xh, axis=0, keepdims=True)

        @pl.when(i == 0)
        def _():
            dg_ref[...] = part

        @pl.when(i > 0)
        def _():
            dg_ref[...] += part

    blk = pl.BlockSpec((tr, w), lambda i: (i, 0))
    return pl.pallas_call(
        body, name=name, grid=(r // tr,),
        in_specs=[pl.BlockSpec((tr, w), lambda i: (i, col)), pl.BlockSpec((1, w), lambda i: (0, 0)), blk]
        + ([blk] if res is not None else []),
        out_specs=[blk, pl.BlockSpec((1, w), lambda i: (0, 0))],
        out_shape=[jax.ShapeDtypeStruct((r, w), F32), jax.ShapeDtypeStruct((1, w), F32)],
        compiler_params=_cp("arbitrary"),
    )(*((x, g, dy) + ((res,) if res is not None else ())))


def _gated_rms_fwd(y, proj, g, *, name):
    r, w = y.shape
    tr = _pick(r, 256, 8)

    def body(y_ref, z_ref, g_ref, o_ref):
        t = y_ref[...] * _silu(z_ref[...])
        rs = lax.rsqrt(jnp.mean(t * t, axis=-1, keepdims=True) + EPS)
        o_ref[...] = (t * rs * g_ref[...]).astype(o_ref.dtype)

    return pl.pallas_call(
        body, name=name, grid=(r // tr,),
        in_specs=[pl.BlockSpec((tr, w), lambda i: (i, 0)), pl.BlockSpec((tr, w), lambda i: (i, OFF_Z // w)),
                  pl.BlockSpec((1, w), lambda i: (0, 0))],
        out_specs=pl.BlockSpec((tr, w), lambda i: (i, 0)),
        out_shape=jax.ShapeDtypeStruct((r, w), BF16),
        compiler_params=_cp("parallel"),
    )(y, proj, g)


def _gated_rms_bwd(y, proj, g, do, *, name):
    r, w = y.shape
    tr = _pick(r, 256, 8)

    def body(y_ref, z_ref, g_ref, do_ref, dy_ref, dz_ref, dg_ref):
        i = pl.program_id(0)
        yv, zv, dov = y_ref[...], z_ref[...], do_ref[...]
        sz = _silu(zv)
        t = yv * sz
        rs = lax.rsqrt(jnp.mean(t * t, axis=-1, keepdims=True) + EPS)
        th = t * rs
        dth = dov * g_ref[...]
        mm = jnp.mean(dth * th, axis=-1, keepdims=True)
        dt = rs * (dth - th * mm)
        dy_ref[...] = dt * sz
        dz_ref[...] = dt * yv * _dsilu(zv)
        part = jnp.sum(dov * th, axis=0, keepdims=True)

        @pl.when(i == 0)
        def _():
            dg_ref[...] = part

        @pl.when(i > 0)
        def _():
            dg_ref[...] += part

    blk = pl.BlockSpec((tr, w), lambda i: (i, 0))
    vec = pl.BlockSpec((1, w), lambda i: (0, 0))
    return pl.pallas_call(
        body, name=name, grid=(r // tr,),
        in_specs=[blk, pl.BlockSpec((tr, w), lambda i: (i, OFF_Z // w)), vec, blk],
        out_specs=[blk, blk, vec],
        out_shape=[jax.ShapeDtypeStruct((r, w), F32), jax.ShapeDtypeStruct((r, w), F32),
                   jax.ShapeDtypeStruct((1, w), F32)],
        compiler_params=_cp("arbitrary"),
    )(y, proj, g, do)


def _swiglu_fwd(gu, *, name):
    r = gu.shape[0]
    f = gu.shape[1] // 2
    tr = _pick(r, 256, 8)

    def body(g_ref, u_ref, o_ref):
        o_ref[...] = (_silu(g_ref[...]) * u_ref[...]).astype(o_ref.dtype)

    return pl.pallas_call(
        body, name=name, grid=(r // tr,),
        in_specs=[pl.BlockSpec((tr, f), lambda i: (i, 0)), pl.BlockSpec((tr, f), lambda i: (i, 1))],
        out_specs=pl.BlockSpec((tr, f), lambda i: (i, 0)),
        out_shape=jax.ShapeDtypeStruct((r, f), BF16),
        compiler_params=_cp("parallel"),
    )(gu, gu)


def _swiglu_bwd(gu, da, *, name):
    r = gu.shape[0]
    f = gu.shape[1] // 2
    tr = _pick(r, 256, 8)

    def body(g_ref, u_ref, da_ref, o_ref):
        gv, uv, dav = g_ref[...], u_ref[...], da_ref[...]
        o_ref[:, :f] = (dav * uv * _dsilu(gv)).astype(o_ref.dtype)
        o_ref[:, f:] = (dav * _silu(gv)).astype(o_ref.dtype)

    return pl.pallas_call(
        body, name=name, grid=(r // tr,),
        in_specs=[pl.BlockSpec((tr, f), lambda i: (i, 0)), pl.BlockSpec((tr, f), lambda i: (i, 1)),
                  pl.BlockSpec((tr, f), lambda i: (i, 0))],
        out_specs=pl.BlockSpec((tr, 2 * f), lambda i: (i, 0)),
        out_shape=jax.ShapeDtypeStruct((r, 2 * f), BF16),
        compiler_params=_cp("parallel"),
    )(gu, gu, da)


CONV_TS = 1024
CONV_TC = 512


def _conv_pre(x, carry, w_ref, b_ref):
    ts = x.shape[0]
    row8 = lax.broadcasted_iota(jnp.int32, (8, x.shape[1]), 0)
    head_x = x[0:8]
    shifted, shifted_head = [], []
    for j in range(SSD_CONV):
        if j == 0:
            shifted.append(x)
            shifted_head.append(head_x)
        else:
            shifted.append(pltpu.roll(x, j, 0))
            shifted_head.append(jnp.where(row8 < j, pltpu.roll(carry, j, 0), pltpu.roll(head_x, j, 0)))
    pre = b_ref[...] + sum(w_ref[SSD_CONV - 1 - j:SSD_CONV - j, :] * shifted[j] for j in range(SSD_CONV))
    pre_head = b_ref[...] + sum(w_ref[SSD_CONV - 1 - j:SSD_CONV - j, :] * shifted_head[j] for j in range(SSD_CONV))
    del ts
    return pre, pre_head, shifted, shifted_head


def _conv_fwd(proj, w, b, *, name):
    s = proj.shape[0]
    c = w.shape[1]
    ts, tc = _pick(s, CONV_TS, 8), CONV_TC
    off = OFF_XBC // tc

    def body(x_ref, w_ref, b_ref, o_ref, carry_ref):
        t = pl.program_id(1)

        @pl.when(t == 0)
        def _():
            carry_ref[...] = jnp.zeros_like(carry_ref)

        x = x_ref[...]
        pre, pre_head, _, _ = _conv_pre(x, carry_ref[...], w_ref, b_ref)
        o_ref[...] = _silu(pre)
        o_ref[0:8, :] = _silu(pre_head)
        carry_ref[...] = x[ts - 8:ts]

    return pl.pallas_call(
        body, name=name, grid=(c // tc, s // ts),
        in_specs=[pl.BlockSpec((ts, tc), lambda j, t: (t, j + off)), pl.BlockSpec((SSD_CONV, tc), lambda j, t: (0, j)),
                  pl.BlockSpec((1, tc), lambda j, t: (0, j))],
        out_specs=pl.BlockSpec((ts, tc), lambda j, t: (t, j)),
        out_shape=jax.ShapeDtypeStruct((s, c), F32),
        scratch_shapes=[pltpu.VMEM((8, tc), F32)],
        compiler_params=_cp("parallel", "arbitrary"),
    )(proj, w, b)


def _conv_bwd_pre(proj, w, b, dy, *, name):
    s = proj.shape[0]
    c = w.shape[1]
    ts, tc = _pick(s, CONV_TS, 8), CONV_TC
    off = OFF_XBC // tc

    def body(x_ref, w_ref, b_ref, dy_ref, dp_ref, dw_ref, db_ref, carry_ref):
        t = pl.program_id(1)

        @pl.when(t == 0)
        def _():
            carry_ref[...] = jnp.zeros_like(carry_ref)
            dw_ref[...] = jnp.zeros_like(dw_ref)
            db_ref[...] = jnp.zeros_like(db_ref)

        x = x_ref[...]
        pre, pre_head, shifted, shifted_head = _conv_pre(x, carry_ref[...], w_ref, b_ref)
        dyv = dy_ref[...]
        dp = dyv * _dsilu(pre)
        dp_head = dyv[0:8] * _dsilu(pre_head)
        row = lax.broadcasted_iota(jnp.int32, dp.shape, 0)
        dp_tail = jnp.where(row >= 8, dp, 0.0)
        dp_ref[...] = dp
        dp_ref[0:8, :] = dp_head
        db_ref[...] += jnp.sum(dp_tail, axis=0, keepdims=True) + jnp.sum(dp_head, axis=0, keepdims=True)
        for j in range(SSD_CONV):
            kk = SSD_CONV - 1 - j
            dw_ref[kk:kk + 1, :] += (jnp.sum(dp_tail * shifted[j], axis=0, keepdims=True)
                                     + jnp.sum(dp_head * shifted_head[j], axis=0, keepdims=True))
        carry_ref[...] = x[ts - 8:ts]

    return pl.pallas_call(
        body, name=name, grid=(c // tc, s // ts),
        in_specs=[pl.BlockSpec((ts, tc), lambda j, t: (t, j + off)), pl.BlockSpec((SSD_CONV, tc), lambda j, t: (0, j)),
                  pl.BlockSpec((1, tc), lambda j, t: (0, j)), pl.BlockSpec((ts, tc), lambda j, t: (t, j))],
        out_specs=[pl.BlockSpec((ts, tc), lambda j, t: (t, j)), pl.BlockSpec((SSD_CONV, tc), lambda j, t: (0, j)),
                   pl.BlockSpec((1, tc), lambda j, t: (0, j))],
        out_shape=[jax.ShapeDtypeStruct((s, c), F32), jax.ShapeDtypeStruct((SSD_CONV, c), F32),
                   jax.ShapeDtypeStruct((1, c), F32)],
        scratch_shapes=[pltpu.VMEM((8, tc), F32)],
        compiler_params=_cp("parallel", "arbitrary"),
    )(proj, w, b, dy)


def _conv_bwd_x(dp, w, *, name):
    s, c = dp.shape
    ts, tc = _pick(s, CONV_TS, 8), CONV_TC
    nt = s // ts

    def body(d_ref, w_ref, o_ref, carry_ref):
        t = pl.program_id(1)

        @pl.when(t == 0)
        def _():
            carry_ref[...] = jnp.zeros_like(carry_ref)

        d = d_ref[...]
        carry = carry_ref[...]
        row8 = lax.broadcasted_iota(jnp.int32, (8, tc), 0)
        tail = d[ts - 8:ts]
        acc = w_ref[SSD_CONV - 1:SSD_CONV, :] * d
        acc_tail = w_ref[SSD_CONV - 1:SSD_CONV, :] * tail
        for j in range(1, SSD_CONV):
            wj = w_ref[SSD_CONV - 1 - j:SSD_CONV - j, :]
            acc = acc + wj * pltpu.roll(d, ts - j, 0)
            up_tail = jnp.where(row8 >= 8 - j, pltpu.roll(carry, 8 - j, 0), pltpu.roll(tail, 8 - j, 0))
            acc_tail = acc_tail + wj * up_tail
        o_ref[...] = acc
        o_ref[ts - 8:ts, :] = acc_tail
        carry_ref[...] = d[0:8]

    return pl.pallas_call(
        body, name=name, grid=(c // tc, nt),
        in_specs=[pl.BlockSpec((ts, tc), lambda j, t: (nt - 1 - t, j)), pl.BlockSpec((SSD_CONV, tc), lambda j, t: (0, j))],
        out_specs=pl.BlockSpec((ts, tc), lambda j, t: (nt - 1 - t, j)),
        out_shape=jax.ShapeDtypeStruct((s, c), F32),
        scratch_shapes=[pltpu.VMEM((8, tc), F32)],
        compiler_params=_cp("parallel", "arbitrary"),
    )(dp, w)


def _merge_fwd(proj, ys, ym, *, name):
    r, w = ys.shape
    tr = _pick(r, 512, 8)
    off = OFF_GATES // w

    def body(g1_ref, g2_ref, ys_ref, ym_ref, o_ref):
        o_ref[...] = (jax.nn.sigmoid(g1_ref[...]) * ys_ref[...]
                      + jax.nn.sigmoid(g2_ref[...]) * ym_ref[...]).astype(o_ref.dtype)

    blk = pl.BlockSpec((tr, w), lambda i: (i, 0))
    return pl.pallas_call(
        body, name=name, grid=(r // tr,),
        in_specs=[pl.BlockSpec((tr, w), lambda i: (i, off)), pl.BlockSpec((tr, w), lambda i: (i, off + 1)), blk, blk],
        out_specs=blk, out_shape=jax.ShapeDtypeStruct((r, w), BF16),
        compiler_params=_cp("parallel"),
    )(proj, proj, ys, ym)


def _merge_bwd(proj, ys, ym, dm, *, name):
    r, w = ys.shape
    tr = _pick(r, 512, 8)
    off = OFF_GATES // w

    def body(g1_ref, g2_ref, ys_ref, ym_ref, dm_ref, dg_ref, dys_ref, dym_ref):
        s1, s2 = jax.nn.sigmoid(g1_ref[...]), jax.nn.sigmoid(g2_ref[...])
        dmv = dm_ref[...]
        dys_ref[...] = (dmv * s1).astype(dys_ref.dtype)
        dym_ref[...] = (dmv * s2).astype(dym_ref.dtype)
        dg_ref[:, :w] = dmv * ys_ref[...] * s1 * (1.0 - s1)
        dg_ref[:, w:] = dmv * ym_ref[...] * s2 * (1.0 - s2)

    blk = pl.BlockSpec((tr, w), lambda i: (i, 0))
    return pl.pallas_call(
        body, name=name, grid=(r // tr,),
        in_specs=[pl.BlockSpec((tr, w), lambda i: (i, off)), pl.BlockSpec((tr, w), lambda i: (i, off + 1)), blk, blk, blk],
        out_specs=[pl.BlockSpec((tr, 2 * w), lambda i: (i, 0)), blk, blk],
        out_shape=[jax.ShapeDtypeStruct((r, 2 * w), F32), jax.ShapeDtypeStruct((r, w), BF16),
                   jax.ShapeDtypeStruct((r, w), BF16)],
        compiler_params=_cp("parallel"),
    )(proj, proj, ys, ym, dm)


def _loss_fwd_bwd(y, target, *, name):
    r, w = y.shape
    tr = _pick(r, 512, 8)

    def body(y_ref, t_ref, l_ref, dy_ref):
        i = pl.program_id(0)
        e = y_ref[...] - t_ref[...]
        dy_ref[...] = e * (1.0 / w)
        part = jnp.sum(e * e, axis=0, keepdims=True) * (0.5 / w)

        @pl.when(i == 0)
        def _():
            l_ref[...] = part

        @pl.when(i > 0)
        def _():
            l_ref[...] += part

    blk = pl.BlockSpec((tr, w), lambda i: (i, 0))
    return pl.pallas_call(
        body, name=name, grid=(r // tr,),
        in_specs=[blk, blk],
        out_specs=[pl.BlockSpec((1, w), lambda i: (0, 0)), blk],
        out_shape=[jax.ShapeDtypeStruct((1, w), F32), jax.ShapeDtypeStruct((r, w), F32)],
        compiler_params=_cp("arbitrary"),
    )(y, target)


def _adamw(w, g, m, v, *, name):
    r, c = w.shape
    tr = _pick(r, max(8, (1 << 20) // (4 * c) // 8 * 8), 8)
    c1 = 1.0 - ADAM_B1 ** ADAM_STEP
    c2 = 1.0 - ADAM_B2 ** ADAM_STEP

    def body(w_ref, g_ref, m_ref, v_ref, d_ref, nm_ref, nv_ref):
        gv = g_ref[...]
        nm = ADAM_B1 * m_ref[...] + (1.0 - ADAM_B1) * gv
        nv = ADAM_B2 * v_ref[...] + (1.0 - ADAM_B2) * (gv * gv)
        nm_ref[...] = nm
        nv_ref[...] = nv
        d_ref[...] = -ADAM_LR * ((nm / c1) / (jnp.sqrt(nv / c2) + ADAM_EPS) + ADAM_WD * w_ref[...])

    blk = pl.BlockSpec((tr, c), lambda i: (i, 0))
    sh = jax.ShapeDtypeStruct((r, c), F32)
    return pl.pallas_call(
        body, name=name, grid=(r // tr,),
        in_specs=[blk] * 4, out_specs=[blk] * 3, out_shape=[sh] * 3,
        compiler_params=_cp("parallel"),
    )(w, g, m, v)


def _softplus(x):
    return jnp.maximum(x, 0.0) + jnp.log(1.0 + jnp.exp(-jnp.abs(x)))


def _ssd_common(dtr_ref, dtrT_ref, dtb_ref, dtbT_ref, al_ref, alT_ref, e_ref):
    L = SSD_CHUNK
    ri = lax.broadcasted_iota(jnp.int32, (L, L), 0)
    cj = lax.broadcasted_iota(jnp.int32, (L, L), 1)
    tril = (ri >= cj).astype(F32)
    triu = (ri <= cj).astype(F32)
    a = -jnp.exp(al_ref[...])
    aT = -jnp.exp(alT_ref[...])
    pre = dtr_ref[...] + dtb_ref[...]
    preT = dtrT_ref[...] + dtbT_ref[...]
    dt = _softplus(pre)
    dtT = _softplus(preT)
    acum = jnp.dot(tril, dt * a, precision=HI, preferred_element_type=F32)
    acumT = jnp.dot(dtT * aT, triu, precision=HI, preferred_element_type=F32)
    e = e_ref[...]
    dt_x = jnp.dot(dt, e, precision=HI, preferred_element_type=F32)
    acum_x = jnp.dot(acum, e, precision=HI, preferred_element_type=F32)
    last_x = acum_x[L - 1:L, :]
    return dict(ri=ri, cj=cj, tril=tril, triu=triu, a=a, aT=aT, pre=pre, preT=preT, dt=dt, dtT=dtT,
                acum=acum, acumT=acumT, dt_x=dt_x, eacum_x=jnp.exp(acum_x), w_x=jnp.exp(last_x - acum_x),
                elast_x=jnp.exp(last_x))


def _dot_nt(a, b):
    return lax.dot_general(a, b, (((1,), (1,)), ((), ())), preferred_element_type=F32)


def _dot_tn(a, b):
    return lax.dot_general(a, b, (((0,), (0,)), ((), ())), preferred_element_type=F32)


def _dot(a, b):
    return jnp.dot(a, b, preferred_element_type=F32)


def _ssd_specs(nc, rev):
    L = SSD_CHUNK
    ix = (lambda c: nc - 1 - c) if rev else (lambda c: c)
    return [
        pl.BlockSpec((L, SSD_D_INNER), lambda c: (ix(c), 0)),
        pl.BlockSpec((L, 512), lambda c: (ix(c), 4)),
        pl.BlockSpec((L, 512), lambda c: (ix(c), 5)),
        pl.BlockSpec((L, SSD_HEADS), lambda c: (ix(c), 0)),
        pl.BlockSpec((SSD_HEADS, L), lambda c: (0, ix(c))),
        pl.BlockSpec((1, SSD_HEADS), lambda c: (0, 0)),
        pl.BlockSpec((SSD_HEADS, 1), lambda c: (0, 0)),
        pl.BlockSpec((1, SSD_HEADS), lambda c: (0, 0)),
        pl.BlockSpec((SSD_HEADS, 1), lambda c: (0, 0)),
        pl.BlockSpec((1, SSD_D_INNER), lambda c: (0, 0)),
        pl.BlockSpec((SSD_HEADS, SSD_D_INNER), lambda c: (0, 0)),
    ]


def _ssd_fwd(xc, dtr, dtrT, dtb, dtbT, alog, alogT, dskx, expand, *, name):
    s = xc.shape[0]
    L = SSD_CHUNK
    nc = s // L

    def body(x_ref, b_ref, c_ref, dtr_ref, dtrT_ref, dtb_ref, dtbT_ref, al_ref, alT_ref, dsk_ref, e_ref,
             y_ref, st_ref, state):
        ci = pl.program_id(0)

        @pl.when(ci == 0)
        def _():
            state[...] = jnp.zeros_like(state)

        st_ref[0] = state[...]
        q = _ssd_common(dtr_ref, dtrT_ref, dtb_ref, dtbT_ref, al_ref, alT_ref, e_ref)
        causal = q["ri"] >= q["cj"]
        lane_lo = q["cj"] < 64
        x = x_ref[...]
        xdt = x * q["dt_x"]
        xdt_b = xdt.astype(BF16)
        xdtw_b = (xdt * q["w_x"]).astype(BF16)
        for g in range(SSD_GROUPS):
            bg = b_ref[:, 128 * g:128 * g + 128]
            cg_b = c_ref[:, 128 * g:128 * g + 128].astype(BF16)
            cb = _dot_nt(cg_b, bg.astype(BF16))
            bgT_b = bg.T.astype(BF16)
            s0 = state[g]
            for jj in range(4):
                j = 4 * g + jj
                sl = slice(128 * j, 128 * j + 128)
                sls = slice(128 * jj, 128 * jj + 128)
                ms = []
                for h in (2 * j, 2 * j + 1):
                    seg = q["acum"][:, h:h + 1] - q["acumT"][h:h + 1, :]
                    decay = jnp.exp(jnp.where(causal, seg, -jnp.inf))
                    ms.append((cb * decay).astype(BF16))
                mcat = jnp.concatenate(ms, axis=1)
                xp = xdt_b[:, sl]
                zero = jnp.zeros_like(xp)
                xstack = jnp.concatenate([jnp.where(lane_lo, xp, zero), jnp.where(lane_lo, zero, xp)], axis=0)
                y = _dot(mcat, xstack)
                y = y + q["eacum_x"][:, sl] * _dot(cg_b, s0[:, sls].astype(BF16))
                y = y + x[:, sl] * dsk_ref[:, sl]
                y_ref[:, sl] = y
                state[g, :, sls] = s0[:, sls] * q["elast_x"][:, sl] + _dot(bgT_b, xdtw_b[:, sl])

    return pl.pallas_call(
        body, name=name, grid=(nc,),
        in_specs=_ssd_specs(nc, False),
        out_specs=[pl.BlockSpec((L, SSD_D_INNER), lambda c: (c, 0)),
                   pl.BlockSpec((1, SSD_GROUPS, SSD_STATE, 512), lambda c: (c, 0, 0, 0))],
        out_shape=[jax.ShapeDtypeStruct((s, SSD_D_INNER), F32),
                   jax.ShapeDtypeStruct((nc, SSD_GROUPS, SSD_STATE, 512), F32)],
        scratch_shapes=[pltpu.VMEM((SSD_GROUPS, SSD_STATE, 512), F32)],
        compiler_params=_cp("arbitrary"),
    )(xc, xc, xc, dtr, dtrT, dtb, dtbT, alog, alogT, dskx, expand)


def _ssd_bwd(xc, dtr, dtrT, dtb, dtbT, alog, alogT, dskx, expand, expandT, states, dy, *, name):
    s = xc.shape[0]
    L = SSD_CHUNK
    H = SSD_HEADS
    nc = s // L

    def body(x_ref, b_ref, c_ref, dtr_ref, dtrT_ref, dtb_ref, dtbT_ref, al_ref, alT_ref, dsk_ref, e_ref,
             et_ref, st_ref, dy_ref,
             dxc_ref, ddtc_ref, ddtr_ref, dbc_ref, dbr_ref, dac_ref, dar_ref, ddsk_ref, dstate):
        ci = pl.program_id(0)

        @pl.when(ci == 0)
        def _():
            dstate[...] = jnp.zeros_like(dstate)
            dbc_ref[...] = jnp.zeros_like(dbc_ref)
            dbr_ref[...] = jnp.zeros_like(dbr_ref)
            dac_ref[...] = jnp.zeros_like(dac_ref)
            dar_ref[...] = jnp.zeros_like(dar_ref)
            ddsk_ref[...] = jnp.zeros_like(ddsk_ref)

        q = _ssd_common(dtr_ref, dtrT_ref, dtb_ref, dtbT_ref, al_ref, alT_ref, e_ref)
        ri, cj = q["ri"], q["cj"]
        causal = ri >= cj
        causalT = ri <= cj
        lane_lo = cj < 64
        lane_h = lax.broadcasted_iota(jnp.int32, (1, H), 1)
        sub_h = lax.broadcasted_iota(jnp.int32, (H, 1), 0)
        x = x_ref[...]
        dyv = dy_ref[...]
        xdt = x * q["dt_x"]
        xdt_b = xdt.astype(BF16)
        xdtw = xdt * q["w_x"]
        xdtw_b = xdtw.astype(BF16)
        edy = q["eacum_x"] * dyv
        edy_b = edy.astype(BF16)
        dyv_b = dyv.astype(BF16)
        dacum_col = jnp.zeros((L, H), F32)
        dacum_row = jnp.zeros((H, L), F32)
        dxdt_t, yoff_t, u_t, r_t = [], [], [], []
        for g in range(SSD_GROUPS):
            bg = b_ref[:, 128 * g:128 * g + 128]
            cg = c_ref[:, 128 * g:128 * g + 128]
            bg_b, cg_b = bg.astype(BF16), cg.astype(BF16)
            cb = _dot_nt(cg_b, bg_b)
            cbT = _dot_nt(bg_b, cg_b)
            cgT_b = cg.T.astype(BF16)
            s0 = st_ref[0, g]
            ds = dstate[g]
            s0_b, ds_b = s0.astype(BF16), ds.astype(BF16)
            dcb = jnp.zeros((L, L), F32)
            for jj in range(4):
                j = 4 * g + jj
                sl = slice(128 * j, 128 * j + 128)
                sls = slice(128 * jj, 128 * jj + 128)
                decs, mts = [], []
                for h in (2 * j, 2 * j + 1):
                    seg = q["acum"][:, h:h + 1] - q["acumT"][h:h + 1, :]
                    decs.append(jnp.exp(jnp.where(causal, seg, -jnp.inf)))
                    mts.append((cbT * jnp.exp(jnp.where(causalT, -seg, -jnp.inf))).astype(BF16))
                dyt_b = dyv_b[:, sl]
                zero = jnp.zeros_like(dyt_b)
                dystack = jnp.concatenate([jnp.where(lane_lo, dyt_b, zero), jnp.where(lane_lo, zero, dyt_b)], axis=0)
                dxs = _dot(jnp.concatenate(mts, axis=0), dyt_b)
                dxdt = jnp.where(lane_lo, dxs[:L], dxs[L:])
                dmcat = _dot_nt(dystack, xdt_b[:, sl])
                for idx, h in enumerate((2 * j, 2 * j + 1)):
                    dm = dmcat[L * idx:L * idx + L]
                    dcb = dcb + dm * decs[idx]
                    dseg = dm * cb * decs[idx]
                    dacum_col = dacum_col + jnp.sum(dseg, axis=1, keepdims=True) * (lane_h == h).astype(F32)
                    dacum_row = dacum_row - (sub_h == h).astype(F32) * jnp.sum(dseg, axis=0, keepdims=True)
                gmat = _dot(cg_b, s0_b[:, sls])
                yoff_t.append(edy[:, sl] * gmat)
                qm = _dot(bg_b, ds_b[:, sls])
                dxdt_t.append(dxdt + qm * q["w_x"][:, sl])
                u_t.append(qm * xdtw[:, sl])
                r_t.append(ds[:, sls] * s0[:, sls] * q["elast_x"][:, sl])
                dstate[g, :, sls] = ds[:, sls] * q["elast_x"][:, sl] + _dot(cgT_b, edy_b[:, sl])
            gsl = slice(512 * g, 512 * g + 512)
            dcb_b = dcb.astype(BF16)
            dcg = _dot(dcb_b, bg_b) + _dot_nt(edy_b[:, gsl], s0_b)
            dbg = _dot(dcb.T.astype(BF16), cg_b) + _dot_nt(xdtw_b[:, gsl], ds_b)
            dxc_ref[:, SSD_D_INNER + 128 * g:SSD_D_INNER + 128 * g + 128] = dbg
            dxc_ref[:, SSD_D_INNER + 512 + 128 * g:SSD_D_INNER + 512 + 128 * g + 128] = dcg
        et = et_ref[...]
        dxdt_all = jnp.concatenate(dxdt_t, axis=1)
        yoff = jnp.concatenate(yoff_t, axis=1)
        uu = jnp.concatenate(u_t, axis=1)
        rr = jnp.concatenate(r_t, axis=1)
        dacum_col = dacum_col + jnp.dot(yoff - uu, et, precision=HI, preferred_element_type=F32)
        dlast = jnp.sum(jnp.dot(uu + rr, et, precision=HI, preferred_element_type=F32), axis=0, keepdims=True)
        row_lh = lax.broadcasted_iota(jnp.int32, (L, H), 0)
        dacum_col = dacum_col + jnp.where(row_lh == L - 1, dlast, 0.0)
        d_dta_col = jnp.dot(q["triu"], dacum_col, precision=HI, preferred_element_type=F32)
        d_dta_row = jnp.dot(dacum_row, q["tril"], precision=HI, preferred_element_type=F32)
        ddt_col = d_dta_col * q["a"] + jnp.dot(dxdt_all * x, et, precision=HI, preferred_element_type=F32)
        ddt_row = d_dta_row * q["aT"]
        ddtr_col = ddt_col * jax.nn.sigmoid(q["pre"])
        ddtr_row = ddt_row * jax.nn.sigmoid(q["preT"])
        ddtc_ref[...] = ddtr_col
        ddtr_ref[...] = ddtr_row
        dac_ref[...] += jnp.sum(d_dta_col * q["dt"], axis=0, keepdims=True)
        dar_ref[...] += jnp.sum(d_dta_row * q["dtT"], axis=1, keepdims=True)
        dbc_ref[...] += jnp.sum(ddtr_col, axis=0, keepdims=True)
        dbr_ref[...] += jnp.sum(ddtr_row, axis=1, keepdims=True)
        ddsk_ref[...] += jnp.sum(dyv * x, axis=0, keepdims=True)
        dxc_ref[:, 0:SSD_D_INNER] = dxdt_all * q["dt_x"] + dyv * dsk_ref[...]

    rv = lambda c: nc - 1 - c
    in_specs = _ssd_specs(nc, True) + [
        pl.BlockSpec((SSD_D_INNER, H), lambda c: (0, 0)),
        pl.BlockSpec((1, SSD_GROUPS, SSD_STATE, 512), lambda c: (rv(c), 0, 0, 0)),
        pl.BlockSpec((L, SSD_D_INNER), lambda c: (rv(c), 0)),
    ]
    vec_c = pl.BlockSpec((1, H), lambda c: (0, 0))
    vec_r = pl.BlockSpec((H, 1), lambda c: (0, 0))
    return pl.pallas_call(
        body, name=name, grid=(nc,),
        in_specs=in_specs,
        out_specs=[pl.BlockSpec((L, SSD_CONV_DIM), lambda c: (rv(c), 0)),
                   pl.BlockSpec((L, H), lambda c: (rv(c), 0)),
                   pl.BlockSpec((H, L), lambda c: (0, rv(c))),
                   vec_c, vec_r, vec_c, vec_r,
                   pl.BlockSpec((1, SSD_D_INNER), lambda c: (0, 0))],
        out_shape=[jax.ShapeDtypeStruct((s, SSD_CONV_DIM), F32),
                   jax.ShapeDtypeStruct((s, H), F32), jax.ShapeDtypeStruct((H, s), F32),
                   jax.ShapeDtypeStruct((1, H), F32), jax.ShapeDtypeStruct((H, 1), F32),
                   jax.ShapeDtypeStruct((1, H), F32), jax.ShapeDtypeStruct((H, 1), F32),
                   jax.ShapeDtypeStruct((1, SSD_D_INNER), F32)],
        scratch_shapes=[pltpu.VMEM((SSD_GROUPS, SSD_STATE, 512), F32)],
        compiler_params=_cp("arbitrary"),
    )(xc, xc, xc, dtr, dtrT, dtb, dtbT, alog, alogT, dskx, expand, expandT, states, dy)


QK_PAD = 256
MLA_TS = 256


def _rope_tables4(pos):
    inv = 1.0 / (ROPE_THETA ** (jnp.arange(0, MLA_ROPE, 2, dtype=F32) / MLA_ROPE))
    ang = pos.astype(F32)[:, None] * inv
    c, s = jnp.cos(ang), jnp.sin(ang)
    return jnp.tile(c, (1, 4)), jnp.concatenate([-s, s, -s, s], axis=1)


def _mla_gains(qg, kg):
    z = jnp.zeros((LANE - MLA_ROPE,), F32)
    return (qg[:MLA_NOPE][None], jnp.concatenate([qg[MLA_NOPE:], z])[None],
            kg[:MLA_NOPE][None], jnp.concatenate([kg[MLA_NOPE:], z])[None])


def _rope_swap(t, first):
    return jnp.where(first, pltpu.roll(t, 96, 1), pltpu.roll(t, 32, 1))


def _mla_prep_specs(ts):
    row = lambda w, c=0: pl.BlockSpec((ts, w), lambda i: (i, c))
    vec = pl.BlockSpec((1, LANE), lambda i: (0, 0))
    return [row(MLA_HEADS * MLA_QK), row(2 * MLA_HEADS * MLA_NOPE), row(LANE, OFF_KRDT // LANE), row(LANE), row(LANE),
            vec, vec, vec, vec]


def _mla_prep_fwd(qraw, kvraw, proj, cos4, sin4, gqn, gqr, gkn, gkr, *, name):
    s = qraw.shape[0]
    ts = _pick(s, MLA_TS, 8)

    def body(q_ref, kv_ref, kr_ref, cos_ref, sin_ref, gqn_ref, gqr_ref, gkn_ref, gkr_ref, qo_ref, ko_ref):
        lane = lax.broadcasted_iota(jnp.int32, (ts, LANE), 1)
        lo = lane < 64
        first = (lane % 64) < 32
        cos, sin = cos_ref[...], sin_ref[...]
        kr = jnp.where(lo, kr_ref[...], 0.0)
        ssq_kr = jnp.sum(kr * kr, axis=-1, keepdims=True)

        def head(xn, xr, ssq_r, gn, gr):
            rs = lax.rsqrt((jnp.sum(xn * xn, axis=-1, keepdims=True) + ssq_r) * (1.0 / MLA_QK) + EPS)
            yr = xr * rs * gr
            return xn * rs * gn, yr * cos + _rope_swap(yr, first) * sin

        for h in range(MLA_HEADS):
            tile = q_ref[:, MLA_HEADS * MLA_NOPE + LANE * (h // 2):MLA_HEADS * MLA_NOPE + LANE * (h // 2) + LANE]
            qr = jnp.where(lo, tile if h % 2 == 0 else pltpu.roll(tile, 64, 1), 0.0)
            on, orr = head(q_ref[:, LANE * h:LANE * h + LANE], qr, jnp.sum(qr * qr, axis=-1, keepdims=True),
                           gqn_ref[...], gqr_ref[...])
            qo_ref[h, :, 0:LANE] = on.astype(BF16)
            qo_ref[h, :, LANE:QK_PAD] = orr.astype(BF16)
            on, orr = head(kv_ref[:, LANE * h:LANE * h + LANE], kr, ssq_kr, gkn_ref[...], gkr_ref[...])
            ko_ref[h, :, 0:LANE] = on.astype(BF16)
            ko_ref[h, :, LANE:QK_PAD] = orr.astype(BF16)

    out = pl.BlockSpec((MLA_HEADS, ts, QK_PAD), lambda i: (0, i, 0))
    sh = jax.ShapeDtypeStruct((MLA_HEADS, s, QK_PAD), BF16)
    return pl.pallas_call(
        body, name=name, grid=(s // ts,),
        in_specs=_mla_prep_specs(ts), out_specs=[out, out], out_shape=[sh, sh],
        compiler_params=_cp("parallel"),
    )(qraw, kvraw, proj, cos4, sin4, gqn, gqr, gkn, gkr)


def _mla_prep_bwd(qraw, kvraw, proj, cos4, sin4, gqn, gqr, gkn, gkr, dq, dk, *, name):
    s = qraw.shape[0]
    ts = _pick(s, MLA_TS, 8)

    def body(q_ref, kv_ref, kr_ref, cos_ref, sin_ref, gqn_ref, gqr_ref, gkn_ref, gkr_ref, dq_ref, dk_ref,
             dqraw_ref, dkn_ref, dkr_ref, dgqn_ref, dgqr_ref, dgkn_ref, dgkr_ref):
        i = pl.program_id(0)

        @pl.when(i == 0)
        def _():
            for r in (dgqn_ref, dgqr_ref, dgkn_ref, dgkr_ref):
                r[...] = jnp.zeros_like(r)

        lane = lax.broadcasted_iota(jnp.int32, (ts, LANE), 1)
        lo = lane < 64
        first = (lane % 64) < 32
        cos, sin = cos_ref[...], sin_ref[...]
        kr = jnp.where(lo, kr_ref[...], 0.0)
        ssq_kr = jnp.sum(kr * kr, axis=-1, keepdims=True)

        def head(xn, xr, ssq_r, gn, gr, don, dor):
            rs = lax.rsqrt((jnp.sum(xn * xn, axis=-1, keepdims=True) + ssq_r) * (1.0 / MLA_QK) + EPS)
            xhn, xhr = xn * rs, xr * rs
            dor = jnp.where(lo, dor, 0.0)
            dyr = dor * cos + _rope_swap(dor * sin, first)
            dxn, dxr = don * gn, dyr * gr
            mm = (jnp.sum(dxn * xhn, axis=-1, keepdims=True) + jnp.sum(dxr * xhr, axis=-1, keepdims=True)) * (1.0 / MLA_QK)
            return (rs * (dxn - xhn * mm), rs * (dxr - xhr * mm),
                    jnp.sum(don * xhn, axis=0, keepdims=True), jnp.sum(dyr * xhr, axis=0, keepdims=True))

        dkr_acc = jnp.zeros((ts, LANE), F32)
        prev = None
        for h in range(MLA_HEADS):
            c0 = MLA_HEADS * MLA_NOPE + LANE * (h // 2)
            tile = q_ref[:, c0:c0 + LANE]
            qr = jnp.where(lo, tile if h % 2 == 0 else pltpu.roll(tile, 64, 1), 0.0)
            dn, dr, gn_p, gr_p = head(q_ref[:, LANE * h:LANE * h + LANE], qr, jnp.sum(qr * qr, axis=-1, keepdims=True),
                                      gqn_ref[...], gqr_ref[...], dq_ref[h, :, 0:LANE], dq_ref[h, :, LANE:QK_PAD])
            dqraw_ref[:, LANE * h:LANE * h + LANE] = dn.astype(dqraw_ref.dtype)
            dgqn_ref[...] += gn_p
            dgqr_ref[...] += gr_p
            if h % 2 == 0:
                prev = dr
            else:
                dqraw_ref[:, c0:c0 + LANE] = (prev + pltpu.roll(dr, 64, 1)).astype(dqraw_ref.dtype)
            dn, dr, gn_p, gr_p = head(kv_ref[:, LANE * h:LANE * h + LANE], kr, ssq_kr, gkn_ref[...], gkr_ref[...],
                                      dk_ref[h, :, 0:LANE], dk_ref[h, :, LANE:QK_PAD])
            dkn_ref[:, LANE * h:LANE * h + LANE] = dn
            dkr_acc = dkr_acc + dr
            dgkn_ref[...] += gn_p
            dgkr_ref[...] += gr_p
        dkr_ref[...] = dkr_acc

    row = lambda w: pl.BlockSpec((ts, w), lambda i: (i, 0))
    vec = pl.BlockSpec((1, LANE), lambda i: (0, 0))
    dspec = pl.BlockSpec((MLA_HEADS, ts, QK_PAD), lambda i: (0, i, 0))
    vsh = jax.ShapeDtypeStruct((1, LANE), F32)
    return pl.pallas_call(
        body, name=name, grid=(s // ts,),
        in_specs=_mla_prep_specs(ts) + [dspec, dspec],
        out_specs=[row(MLA_HEADS * MLA_QK), row(MLA_HEADS * MLA_NOPE), row(LANE), vec, vec, vec, vec],
        out_shape=[jax.ShapeDtypeStruct((s, MLA_HEADS * MLA_QK), BF16), jax.ShapeDtypeStruct((s, MLA_HEADS * MLA_NOPE), F32),
                   jax.ShapeDtypeStruct((s, LANE), F32), vsh, vsh, vsh, vsh],
        compiler_params=_cp("arbitrary"),
    )(qraw, kvraw, proj, cos4, sin4, gqn, gqr, gkn, gkr, dq, dk)


ATT_T = 512
ATT_SCALE = MLA_QK ** -0.5


def _attn_fwd(q, k, kvraw, *, name):
    nh, s, _ = q.shape
    t = _pick(s, ATT_T, LANE)
    nb = s // t

    def body(q_ref, k_ref, v_ref, o_ref, lse_ref, m_ref, l_ref, acc_ref):
        i, j = pl.program_id(1), pl.program_id(2)

        @pl.when(j == 0)
        def _():
            m_ref[...] = jnp.full_like(m_ref, -jnp.inf)
            l_ref[...] = jnp.zeros_like(l_ref)
            acc_ref[...] = jnp.zeros_like(acc_ref)

        def step(diagonal):
            sc = _dot_nt(q_ref[0], k_ref[0]) * ATT_SCALE
            if diagonal:
                ri = lax.broadcasted_iota(jnp.int32, (t, t), 0)
                cj = lax.broadcasted_iota(jnp.int32, (t, t), 1)
                sc = jnp.where(ri >= cj, sc, -jnp.inf)
            m_new = jnp.maximum(m_ref[...], jnp.max(sc, axis=-1, keepdims=True))
            alpha = jnp.exp(m_ref[...] - m_new)
            p = jnp.exp(sc - m_new)
            l_ref[...] = alpha * l_ref[...] + jnp.sum(p, axis=-1, keepdims=True)
            acc_ref[...] = alpha * acc_ref[...] + _dot(p.astype(BF16), v_ref[...].astype(BF16))
            m_ref[...] = m_new

        @pl.when(j < i)
        def _():
            step(False)

        @pl.when(j == i)
        def _():
            step(True)
            o_ref[...] = acc_ref[...] / l_ref[...]
            lse_ref[0] = m_ref[...] + jnp.log(l_ref[...])

    return pl.pallas_call(
        body, name=name, grid=(nh, nb, nb),
        in_specs=[pl.BlockSpec((1, t, QK_PAD), lambda h, i, j: (h, i, 0)),
                  pl.BlockSpec((1, t, QK_PAD), lambda h, i, j: (h, jnp.minimum(j, i), 0)),
                  pl.BlockSpec((t, MLA_V), lambda h, i, j: (jnp.minimum(j, i), nh + h))],
        out_specs=[pl.BlockSpec((t, MLA_V), lambda h, i, j: (i, h)),
                   pl.BlockSpec((1, t, 1), lambda h, i, j: (h, i, 0))],
        out_shape=[jax.ShapeDtypeStruct((s, nh * MLA_V), F32), jax.ShapeDtypeStruct((nh, s, 1), F32)],
        scratch_shapes=[pltpu.VMEM((t, 1), F32), pltpu.VMEM((t, 1), F32), pltpu.VMEM((t, MLA_V), F32)],
        compiler_params=_cp("parallel", "parallel", "arbitrary"),
    )(q, k, kvraw)


def _attn_bwd(q, k, kvraw, o, lse, do, *, name):
    nh, s, _ = q.shape
    t = _pick(s, ATT_T, LANE)
    nb = s // t

    def body(q_ref, k_ref, v_ref, o_ref, lse_ref, do_ref, dq_ref, dk_ref, dv_ref, dk_acc, dv_acc):
        j, i = pl.program_id(1), pl.program_id(2)

        @pl.when(i == 0)
        def _():
            dk_acc[...] = jnp.zeros_like(dk_acc)
            dv_acc[...] = jnp.zeros_like(dv_acc)

        def step(diagonal):
            qv, kv = q_ref[0], k_ref[0]
            sc = _dot_nt(qv, kv) * ATT_SCALE
            if diagonal:
                ri = lax.broadcasted_iota(jnp.int32, (t, t), 0)
                cj = lax.broadcasted_iota(jnp.int32, (t, t), 1)
                sc = jnp.where(ri >= cj, sc, -jnp.inf)
            p = jnp.exp(sc - lse_ref[0])
            dov = do_ref[...]
            delta = jnp.sum(dov * o_ref[...], axis=-1, keepdims=True)
            do_b = dov.astype(BF16)
            dv_acc[...] += _dot_tn(p.astype(BF16), do_b)
            dp = _dot_nt(do_b, v_ref[...].astype(BF16))
            ds_b = (p * (dp - delta) * ATT_SCALE).astype(BF16)
            dk_acc[...] += _dot_tn(ds_b, qv)
            dq_part = _dot(ds_b, kv)
            rows = pl.ds(pl.multiple_of(i * t, t), t)

            @pl.when(j == 0)
            def _():
                dq_ref[0, rows, :] = dq_part

            @pl.when(j > 0)
            def _():
                dq_ref[0, rows, :] += dq_part

        @pl.when(i > j)
        def _():
            step(False)

        @pl.when(i == j)
        def _():
            step(True)

        @pl.when(i == nb - 1)
        def _():
            dk_ref[0] = dk_acc[...]
            dv_ref[...] = dv_acc[...]

    qi = lambda h, j, i: jnp.maximum(i, j)
    return pl.pallas_call(
        body, name=name, grid=(nh, nb, nb),
        in_specs=[pl.BlockSpec((1, t, QK_PAD), lambda h, j, i: (h, qi(h, j, i), 0)),
                  pl.BlockSpec((1, t, QK_PAD), lambda h, j, i: (h, j, 0)),
                  pl.BlockSpec((t, MLA_V), lambda h, j, i: (j, nh + h)),
                  pl.BlockSpec((t, MLA_V), lambda h, j, i: (qi(h, j, i), h)),
                  pl.BlockSpec((1, t, 1), lambda h, j, i: (h, qi(h, j, i), 0)),
                  pl.BlockSpec((t, MLA_V), lambda h, j, i: (qi(h, j, i), h))],
        out_specs=[pl.BlockSpec((1, s, QK_PAD), lambda h, j, i: (h, 0, 0)),
                   pl.BlockSpec((1, t, QK_PAD), lambda h, j, i: (h, j, 0)),
                   pl.BlockSpec((t, MLA_V), lambda h, j, i: (j, h))],
        out_shape=[jax.ShapeDtypeStruct((nh, s, QK_PAD), F32), jax.ShapeDtypeStruct((nh, s, QK_PAD), F32),
                   jax.ShapeDtypeStruct((s, nh * MLA_V), F32)],
        scratch_shapes=[pltpu.VMEM((t, QK_PAD), F32), pltpu.VMEM((t, MLA_V), F32)],
        compiler_params=_cp("parallel", "arbitrary", "arbitrary"),
    )(q, k, kvraw, o, lse, do)


def _ffn_fwd(h, w, tag):
    n = _rms_fwd(h, w["ln"], name=tag + "_norm")
    gu = _matmul(n, w["w13"], "nn", name=tag + "_up")
    act = _swiglu_fwd(gu, name=tag + "_act")
    out = _matmul(act, w["w2"], "nn", name=tag + "_down", scale=0.5, res=h)
    return out, (h, n, gu, act)


def _ffn_bwd(dout, saved, w, tag):
    h, n, gu, act = saved
    dact = _matmul(dout, w["w2"], "nt", name=tag + "_down_dx", scale=0.5)
    dw2 = _matmul(act, dout, "tn", name=tag + "_down_dw", scale=0.5)
    dgu = _swiglu_bwd(gu, dact, name=tag + "_act_bwd")
    dw13 = _matmul(n, dgu, "tn", name=tag + "_up_dw")
    dn = _matmul(dgu, w["w13"], "nt", name=tag + "_up_dx")
    dh, dln = _rms_bwd(h, w["ln"], dn, name=tag + "_norm_bwd", res=dout)
    return dh, dict(ln=dln, w13=dw13, w2=dw2)


def _mixer_fwd(h, w, rope, tag):
    cos4, sin4 = rope
    u = _rms_fwd(h, w["ln_mix"], name=tag + "_norm")
    proj = _matmul(u, w["w_in"], "nn", name=tag + "_in")
    xc = _conv_fwd(proj, w["conv_w"], w["conv_b"], name=tag + "_conv")
    dtr = proj[:, OFF_KRDT + MLA_ROPE:OFF_KRDT + MLA_ROPE + SSD_HEADS]
    dtrT = dtr.T
    y, states = _ssd_fwd(xc, dtr, dtrT, *w["ssd_aux"], name=tag + "_ssd")
    yn = _gated_rms_fwd(y, proj, w["ssd_norm"], name=tag + "_ssd_norm")
    y_ssd = _matmul(yn, w["w_ssd_out"], "nn", name=tag + "_ssd_out")
    cqn = _rms_fwd(proj, w["q_lora_norm"], name=tag + "_q_lora_norm", col=OFF_CQ // MLA_Q_LORA, width=MLA_Q_LORA)
    qraw = _matmul(cqn, w["w_uq"], "nn", name=tag + "_uq")
    ckvn = _rms_fwd(proj, w["kv_lora_norm"], name=tag + "_kv_lora_norm", col=OFF_CKV // MLA_KV_LORA, width=MLA_KV_LORA)
    kvraw = _matmul(ckvn, w["w_ukv"], "nn", name=tag + "_ukv")
    qf, kf = _mla_prep_fwd(qraw, kvraw, proj, cos4, sin4, *w["qk_gains"], name=tag + "_qk_prep")
    o, lse = _attn_fwd(qf, kf, kvraw, name=tag + "_attn")
    y_mla = _matmul(o, w["w_mla_out"], "nn", name=tag + "_mla_out")
    merged = _merge_fwd(proj, y_ssd, y_mla, name=tag + "_merge")
    out = _matmul(merged, w["w_o"], "nn", name=tag + "_o", res=h)
    saved = dict(h=h, u=u, proj=proj, xc=xc, dtr=dtr, dtrT=dtrT, states=states, y=y, yn=yn, y_ssd=y_ssd, cqn=cqn,
                 qraw=qraw, ckvn=ckvn, kvraw=kvraw, qf=qf, kf=kf, o=o, lse=lse, y_mla=y_mla, merged=merged)
    return out, saved


def _mixer_bwd(dout, s, w, rope, tag):
    cos4, sin4 = rope
    g = {}
    proj = s["proj"]
    dmerged = _matmul(dout, w["w_o"], "nt", name=tag + "_o_dx")
    g["w_o"] = _matmul(s["merged"], dout, "tn", name=tag + "_o_dw")
    dgates, dy_ssd, dy_mla = _merge_bwd(proj, s["y_ssd"], s["y_mla"], dmerged, name=tag + "_merge_bwd")
    do = _matmul(dy_mla, w["w_mla_out"], "nt", name=tag + "_mla_out_dx")
    g["w_mla_out"] = _matmul(s["o"], dy_mla, "tn", name=tag + "_mla_out_dw")
    dqf, dkf, dv = _attn_bwd(s["qf"], s["kf"], s["kvraw"], s["o"], s["lse"], do, name=tag + "_attn_bwd")
    dqraw, dkn, dkrt, dgqn, dgqr, dgkn, dgkr = _mla_prep_bwd(
        s["qraw"], s["kvraw"], proj, cos4, sin4, *w["qk_gains"], dqf, dkf, name=tag + "_qk_prep_bwd")
    g["q_norm"] = jnp.concatenate([dgqn[0], dgqr[0, :MLA_ROPE]])
    g["k_norm"] = jnp.concatenate([dgkn[0], dgkr[0, :MLA_ROPE]])
    dkvraw = jnp.concatenate([dkn, dv], axis=1).astype(BF16)
    dcqn = _matmul(dqraw, w["w_uq"], "nt", name=tag + "_uq_dx")
    g["w_uq"] = _matmul(s["cqn"], dqraw, "tn", name=tag + "_uq_dw")
    dckvn = _matmul(dkvraw, w["w_ukv"], "nt", name=tag + "_ukv_dx")
    g["w_ukv"] = _matmul(s["ckvn"], dkvraw, "tn", name=tag + "_ukv_dw")
    dcq, g["q_lora_norm"] = _rms_bwd(proj, w["q_lora_norm"], dcqn, name=tag + "_q_lora_norm_bwd",
                                     col=OFF_CQ // MLA_Q_LORA, width=MLA_Q_LORA)
    dckv, g["kv_lora_norm"] = _rms_bwd(proj, w["kv_lora_norm"], dckvn, name=tag + "_kv_lora_norm_bwd",
                                       col=OFF_CKV // MLA_KV_LORA, width=MLA_KV_LORA)
    dyn = _matmul(dy_ssd, w["w_ssd_out"], "nt", name=tag + "_ssd_out_dx")
    g["w_ssd_out"] = _matmul(s["yn"], dy_ssd, "tn", name=tag + "_ssd_out_dw")
    dy, dz, g["ssd_norm"] = _gated_rms_bwd(s["y"], proj, w["ssd_norm"], dyn, name=tag + "_ssd_norm_bwd")
    aux = w["ssd_aux"]
    dxc, ddt_c, ddt_r, dbias_c, dbias_r, da_c, da_r, ddsk = _ssd_bwd(
        s["xc"], s["dtr"], s["dtrT"], *aux, aux[-1].T, s["states"], dy, name=tag + "_ssd_bwd")
    g["dt_bias"] = dbias_c[0] + dbias_r[:, 0]
    g["a_log"] = (da_c[0] + da_r[:, 0]) * (-jnp.exp(aux[2][0]))
    g["d_skip"] = jnp.sum(ddsk.reshape(SSD_HEADS, SSD_HEAD_DIM), axis=1)
    dpre, g["conv_w"], g["conv_b"] = _conv_bwd_pre(proj, w["conv_w"], w["conv_b"], dxc, name=tag + "_conv_bwd_pre")
    dxbc = _conv_bwd_x(dpre, w["conv_w"], name=tag + "_conv_bwd_x")
    ddtr = ddt_c + ddt_r.T
    dkrdt = jnp.concatenate([dkrt[:, :MLA_ROPE], ddtr, jnp.zeros((ddtr.shape[0], LANE - MLA_ROPE - SSD_HEADS), F32)], axis=1)
    dproj = jnp.concatenate([dz, dxbc, dgates, dcq, dckv, dkrdt], axis=1).astype(BF16)
    du = _matmul(dproj, w["w_in"], "nt", name=tag + "_in_dx")
    g["w_in"] = _matmul(s["u"], dproj, "tn", name=tag + "_in_dw")
    dh, g["ln_mix"] = _rms_bwd(s["h"], w["ln_mix"], du, name=tag + "_norm_bwd", res=dout)
    return dh, g


W_NAMES = ["ln_ffn1", "ffn1_w13", "ffn1_w2", "ln_mix", "w_in", "conv_w", "conv_b", "dt_bias", "a_log", "d_skip",
           "ssd_norm", "w_ssd_out", "q_lora_norm", "w_uq", "kv_lora_norm", "w_ukv", "q_norm", "k_norm", "w_mla_out",
           "w_o", "ln_ffn2", "ffn2_w13", "ffn2_w2"]
SHARD_AXIS = {"ffn1_w13": 2, "ffn1_w2": 1, "w_in": 2, "conv_w": 2, "w_ssd_out": 1, "w_uq": 2, "w_ukv": 2,
              "w_mla_out": 1, "w_o": 1, "ffn2_w13": 2, "ffn2_w2": 1}
SHARDED = [n for n in W_NAMES if n in SHARD_AXIS]
REPLICATED = [n for n in W_NAMES if n not in SHARD_AXIS]
N_CHIPS = 4
N_DEV = 8
PACK_COLS = 1024
IN_SPLIT = (2048, 3072, 32, 512, 256, 64, 2048)


def _pack(arrs, rows, dtype):
    flat = jnp.concatenate([a.astype(dtype).reshape(-1) for a in arrs])
    return jnp.pad(flat, (0, rows * PACK_COLS - flat.shape[0])).reshape(rows, PACK_COLS)


def _unpack(packed, shapes):
    flat = packed.reshape(-1)
    out, at = [], 0
    for sh in shapes:
        n = math.prod(sh)
        out.append(flat[at:at + n].reshape(sh))
        at += n
    return out


def _pack_rows(shapes):
    n = sum(math.prod(sh) for sh in shapes)
    return -(-n // (PACK_COLS * 1024)) * 1024


def _in_perm(w_in):
    z, xbc, dt, cq, ckv, kr, gates = jnp.split(w_in, list(np_cumsum(IN_SPLIT))[:-1], axis=1)
    return jnp.concatenate([z, xbc, gates, cq, ckv, kr, dt, jnp.zeros((w_in.shape[0], PROJ_W - sum(IN_SPLIT)), w_in.dtype)], axis=1)


def _in_unperm(g):
    z, xbc, gates, cq, ckv = (g[:, OFF_Z:OFF_XBC], g[:, OFF_XBC:OFF_GATES], g[:, OFF_GATES:OFF_CQ], g[:, OFF_CQ:OFF_CKV],
                              g[:, OFF_CKV:OFF_KRDT])
    kr = g[:, OFF_KRDT:OFF_KRDT + MLA_ROPE]
    dt = g[:, OFF_KRDT + MLA_ROPE:OFF_KRDT + MLA_ROPE + SSD_HEADS]
    return jnp.concatenate([z, xbc, dt, cq, ckv, kr, gates], axis=1)


def np_cumsum(sizes):
    out, t = [], 0
    for s in sizes:
        t += s
        out.append(t)
    return out


def _head_perm(w, first):
    r = w.shape[0]
    w3 = w.reshape(r, MLA_HEADS, -1)
    return jnp.concatenate([w3[:, :, :first].reshape(r, -1), w3[:, :, first:].reshape(r, -1)], axis=1)


def _head_unperm(g, first):
    r = g.shape[0]
    rest = g.shape[1] // MLA_HEADS - first
    a = g[:, :MLA_HEADS * first].reshape(r, MLA_HEADS, first)
    b = g[:, MLA_HEADS * first:].reshape(r, MLA_HEADS, rest)
    return jnp.concatenate([a, b], axis=2).reshape(r, -1)


def _layer_weights(full, l):
    row = lambda n: full[n][l][None].astype(F32)
    expand = jnp.repeat(jnp.eye(SSD_HEADS, dtype=F32), SSD_HEAD_DIM, axis=1)
    dtb, al, dsk = full["dt_bias"][l], full["a_log"][l], full["d_skip"][l]
    mixer = dict(
        ln_mix=row("ln_mix"), w_in=_in_perm(full["w_in"][l]), conv_w=full["conv_w"][l], conv_b=row("conv_b"),
        ssd_aux=(dtb[None], dtb[:, None], al[None], al[:, None], jnp.repeat(dsk, SSD_HEAD_DIM)[None], expand),
        ssd_norm=row("ssd_norm"), w_ssd_out=full["w_ssd_out"][l],
        q_lora_norm=row("q_lora_norm"), w_uq=_head_perm(full["w_uq"][l], MLA_NOPE),
        kv_lora_norm=row("kv_lora_norm"), w_ukv=_head_perm(full["w_ukv"][l], MLA_NOPE),
        qk_gains=_mla_gains(full["q_norm"][l], full["k_norm"][l]),
        w_mla_out=full["w_mla_out"][l], w_o=full["w_o"][l])
    ffn1 = dict(ln=row("ln_ffn1"), w13=full["ffn1_w13"][l], w2=full["ffn1_w2"][l])
    ffn2 = dict(ln=row("ln_ffn2"), w13=full["ffn2_w13"][l], w2=full["ffn2_w2"][l])
    return ffn1, mixer, ffn2


def _layer_grads(g1, gm, g2):
    return {
        "ln_ffn1": g1["ln"][0], "ffn1_w13": g1["w13"], "ffn1_w2": g1["w2"],
        "ln_mix": gm["ln_mix"][0], "w_in": _in_unperm(gm["w_in"]), "conv_w": gm["conv_w"], "conv_b": gm["conv_b"][0],
        "dt_bias": gm["dt_bias"], "a_log": gm["a_log"], "d_skip": gm["d_skip"], "ssd_norm": gm["ssd_norm"][0],
        "w_ssd_out": gm["w_ssd_out"], "q_lora_norm": gm["q_lora_norm"][0], "w_uq": _head_unperm(gm["w_uq"], MLA_NOPE),
        "kv_lora_norm": gm["kv_lora_norm"][0], "w_ukv": _head_unperm(gm["w_ukv"], MLA_NOPE),
        "q_norm": gm["q_norm"], "k_norm": gm["k_norm"], "w_mla_out": gm["w_mla_out"], "w_o": gm["w_o"],
        "ln_ffn2": g2["ln"][0], "ffn2_w13": g2["w13"], "ffn2_w2": g2["w2"],
    }


def _local_step(x, positions, loss_target, full):
    rope = _rope_tables4(positions)
    lw = [_layer_weights(full, l) for l in range(DEPTH)]
    h = x
    saved = []
    for l in range(DEPTH):
        f1, mx, f2 = lw[l]
        h, s1 = _ffn_fwd(h, f1, f"l{l}_ffn1")
        h, sm = _mixer_fwd(h, mx, rope, f"l{l}_mix")
        h, s2 = _ffn_fwd(h, f2, f"l{l}_ffn2")
        saved.append((s1, sm, s2))
    loss_part, dh = _loss_fwd_bwd(h, loss_target, name="loss")
    grads = [None] * DEPTH
    for l in reversed(range(DEPTH)):
        f1, mx, f2 = lw[l]
        s1, sm, s2 = saved[l]
        dh, g2 = _ffn_bwd(dh, s2, f2, f"l{l}_ffn2")
        dh, gm = _mixer_bwd(dh, sm, mx, rope, f"l{l}_mix")
        dh, g1 = _ffn_bwd(dh, s1, f1, f"l{l}_ffn1")
        grads[l] = _layer_grads(g1, gm, g2)
    full_grads = {n: jnp.stack([grads[l][n] for l in range(DEPTH)]) for n in W_NAMES}
    return loss_part, dh, full_grads


MESH = pl.DeviceIdType.MESH
ANY = pl.BlockSpec(memory_space=pl.ANY)


def _place():
    return lax.axis_index("x"), lax.axis_index("y"), lax.axis_index("c")


def _other_chips(x, y):
    return [(1 - x, y), (x, 1 - y), (1 - x, 1 - y)]


def _remote(src, dst, send_sems, recv_sems, k, to):
    return pltpu.make_async_remote_copy(src_ref=src, dst_ref=dst, send_sem=send_sems.at[k], recv_sem=recv_sems.at[k],
                                        device_id=to, device_id_type=MESH)


N_PARTS = 8


def _parts(rows):
    size = rows // N_PARTS
    assert size * N_PARTS == rows and size % 16 == 0, rows
    return [(p * size, size) for p in range(N_PARTS)]


def _rows(ref, lead, base, start, size):
    return ref.at[(*lead, pl.ds(pl.multiple_of(base + start, 16), size), slice(None))]


def _my_chip():
    return 2 * lax.axis_index("x") + lax.axis_index("y")


def _own_slot(packed, *, name):
    r, ncol = packed.shape
    tr = _pick(r, 512, 16)

    def body(x_ref, o_ref):
        o_ref[...] = x_ref[...]

    return pl.pallas_call(
        body, name=name, grid=(r // tr,),
        in_specs=[pl.BlockSpec((tr, ncol), lambda i: (i, 0))],
        out_specs=pl.BlockSpec((None, tr, ncol), lambda i: (_my_chip(), i, 0)),
        out_shape=jax.ShapeDtypeStruct((N_CHIPS, r, ncol), packed.dtype),
        compiler_params=_cp("arbitrary"),
    )(packed)


def _gather_shards(packed, slots, *, name):
    r, ncol = packed.shape
    hr = r // 2
    parts = _parts(hr)

    def body(x_ref, slots_ref, out_ref, send_sems, recv_sems):
        del slots_ref
        x, y, c = _place()
        chips = _other_chips(x, y)
        me = 2 * x + y

        def half(chip, cc):
            return _rows(out_ref, (2 * chip[0] + chip[1],), cc * hr, 0, hr)

        for j, chip in enumerate(chips):
            for st, sz in parts:
                _remote(_rows(x_ref, (), c * hr, st, sz), _rows(out_ref, (me,), c * hr, st, sz), send_sems, recv_sems, j,
                        (*chip, c)).start()
        for j, chip in enumerate(chips):
            _remote(half(chip, c), half(chip, c), send_sems, recv_sems, j, (x, y, c)).wait_recv()
            slot = 2 * chip[0] + chip[1]
            for st, sz in parts:
                _remote(_rows(out_ref, (slot,), c * hr, st, sz), _rows(out_ref, (slot,), c * hr, st, sz), send_sems,
                        recv_sems, 3 + j, (x, y, 1 - c)).start()
        for j, chip in enumerate(chips):
            _remote(half(chip, 1 - c), half(chip, 1 - c), send_sems, recv_sems, 3 + j, (x, y, c)).wait_recv()
        for k in range(6):
            _remote(half((x, y), c), half((x, y), c), send_sems, recv_sems, k, (x, y, c)).wait_send()

    return pl.pallas_call(
        body, name=name,
        out_shape=jax.ShapeDtypeStruct((N_CHIPS, r, ncol), packed.dtype),
        in_specs=[ANY, ANY], out_specs=ANY, input_output_aliases={1: 0},
        scratch_shapes=[pltpu.SemaphoreType.DMA((6,)), pltpu.SemaphoreType.DMA((6,))],
    )(packed, slots)


def _swap_halves(g, *, name):
    n, r, ncol = g.shape
    hr = r // 2
    parts = _parts(hr)

    def body(g_ref, got_ref, send_sems, recv_sems):
        x, y, c = _place()
        for s in range(n):
            for st, sz in parts:
                _remote(_rows(g_ref, (s,), (1 - c) * hr, st, sz), got_ref.at[s, pl.ds(st, sz), :], send_sems, recv_sems, 0,
                        (x, y, 1 - c)).start()
        _remote(got_ref, got_ref, send_sems, recv_sems, 0, (x, y, c)).wait()

    return pl.pallas_call(
        body, name=name, out_shape=jax.ShapeDtypeStruct((n, hr, ncol), g.dtype), in_specs=[ANY], out_specs=ANY,
        scratch_shapes=[pltpu.SemaphoreType.DMA((1,)), pltpu.SemaphoreType.DMA((1,))],
    )(g)


def _add_cores(g, got, *, name):
    n, r, ncol = g.shape
    hr = r // 2
    tr = _pick(hr, 512, 8)
    nb = hr // tr

    def body(a_ref, b_ref, o_ref):
        o_ref[...] = a_ref[...] + b_ref[...]

    blk = pl.BlockSpec((None, tr, ncol), lambda s, i: (s, i, 0))
    return pl.pallas_call(
        body, name=name, grid=(n, nb),
        in_specs=[pl.BlockSpec((None, tr, ncol), lambda s, i: (s, lax.axis_index("c") * nb + i, 0)), blk],
        out_specs=blk,
        out_shape=jax.ShapeDtypeStruct((n, hr, ncol), g.dtype),
        compiler_params=_cp("parallel", "parallel"),
    )(g, got)


def _scatter_to_chips(a, *, name):
    n, r, ncol = a.shape
    parts = _parts(r)

    def body(a_ref, got_ref, send_sems, recv_sems):
        x, y, c = _place()
        for st, sz in parts:
            for j, chip in enumerate(_other_chips(x, y)):
                _remote(a_ref.at[2 * chip[0] + chip[1], pl.ds(st, sz), :], got_ref.at[j, pl.ds(st, sz), :], send_sems,
                        recv_sems, j, (*chip, c)).start()
        for j in range(n - 1):
            _remote(got_ref.at[j], got_ref.at[j], send_sems, recv_sems, j, (x, y, c)).wait()

    return pl.pallas_call(
        body, name=name, out_shape=jax.ShapeDtypeStruct((n - 1, r, ncol), a.dtype), in_specs=[ANY], out_specs=ANY,
        scratch_shapes=[pltpu.SemaphoreType.DMA((3,)), pltpu.SemaphoreType.DMA((3,))],
    )(a)


def _add_chips(a, got, *, name):
    n, hr, ncol = a.shape
    tr = _pick(hr, 512, 8)
    nb = hr // tr

    def body(a_ref, g0_ref, g1_ref, g2_ref, o_ref):
        o_ref[...] = ((a_ref[...] + g0_ref[...]) + g1_ref[...]) + g2_ref[...]

    other = lambda j: pl.BlockSpec((None, tr, ncol), lambda i: (j, i, 0))
    return pl.pallas_call(
        body, name=name, grid=(nb,),
        in_specs=[pl.BlockSpec((None, tr, ncol), lambda i: (_my_chip(), i, 0)), other(0), other(1), other(2)],
        out_specs=pl.BlockSpec((tr, ncol), lambda i: (lax.axis_index("c") * nb + i, 0)),
        out_shape=jax.ShapeDtypeStruct((2 * hr, ncol), a.dtype),
        compiler_params=_cp("parallel"),
    )(a, got, got, got)


def _join_halves(buf, *, name):
    r, ncol = buf.shape
    hr = r // 2
    parts = _parts(hr)

    def body(b_ref, out_ref, send_sems, recv_sems):
        del b_ref
        x, y, c = _place()
        for st, sz in parts:
            _remote(_rows(out_ref, (), c * hr, st, sz), _rows(out_ref, (), c * hr, st, sz), send_sems, recv_sems, 0,
                    (x, y, 1 - c)).start()
        theirs = _rows(out_ref, (), (1 - c) * hr, 0, hr)
        _remote(theirs, theirs, send_sems, recv_sems, 0, (x, y, c)).wait()

    return pl.pallas_call(
        body, name=name, out_shape=jax.ShapeDtypeStruct((r, ncol), buf.dtype), in_specs=[ANY], out_specs=ANY,
        input_output_aliases={0: 0},
        scratch_shapes=[pltpu.SemaphoreType.DMA((1,)), pltpu.SemaphoreType.DMA((1,))],
    )(buf)


def _reduce_scatter(g, *, name):
    got = _swap_halves(g, name=name + "_swap")
    chip_sum = _add_cores(g, got, name=name + "_add_cores")
    others = _scatter_to_chips(chip_sum, name=name + "_scatter")
    return _join_halves(_add_chips(chip_sum, others, name=name + "_add_chips"), name=name + "_join")


def _all_gather_small(v, *, name):
    r, ncol = v.shape

    def body(x_ref, out_ref, send_sems, recv_sems, local_sem):
        x, y, c = _place()
        me, sibling = (x, y, c), (x, y, 1 - c)
        chips = _other_chips(x, y)

        def slot(p):
            return out_ref.at[4 * p[0] + 2 * p[1] + p[2]]

        mine = pltpu.make_async_copy(x_ref, slot(me), local_sem.at[0])
        mine.start()
        first = [_remote(x_ref, slot(me), send_sems, recv_sems, 0, sibling)]
        first += [_remote(x_ref, slot(me), send_sems, recv_sems, 1 + j, (*chip, c)) for j, chip in enumerate(chips)]
        for cp in first:
            cp.start()
        passed = [_remote(slot((*chip, c)), slot((*chip, c)), send_sems, recv_sems, 4 + j, sibling)
                  for j, chip in enumerate(chips)]
        for j, chip in enumerate(chips):
            _remote(slot((*chip, c)), slot((*chip, c)), send_sems, recv_sems, 1 + j, me).wait_recv()
            passed[j].start()
        _remote(slot(sibling), slot(sibling), send_sems, recv_sems, 0, me).wait_recv()
        for j, chip in enumerate(chips):
            _remote(slot((*chip, 1 - c)), slot((*chip, 1 - c)), send_sems, recv_sems, 4 + j, me).wait_recv()
        for cp in first + passed:
            cp.wait_send()
        mine.wait()

    vm = pl.BlockSpec(memory_space=pltpu.VMEM)
    return pl.pallas_call(
        body, name=name, out_shape=jax.ShapeDtypeStruct((N_DEV, r, ncol), v.dtype), in_specs=[vm], out_specs=vm,
        scratch_shapes=[pltpu.SemaphoreType.DMA((7,)), pltpu.SemaphoreType.DMA((7,)), pltpu.SemaphoreType.DMA((1,))],
    )(v)


def _sum_slots(g8, *, name):
    n, r, ncol = g8.shape

    def body(g_ref, o_ref):
        acc = g_ref[0]
        for k in range(1, n):
            acc = acc + g_ref[k]
        o_ref[...] = acc

    return pl.pallas_call(body, name=name, out_shape=jax.ShapeDtypeStruct((r, ncol), g8.dtype))(g8)


def _step(a):
    x = a["x"][0]
    s = x.shape[0]
    del s
    shard_shapes = [a[n].shape for n in SHARDED]
    rows = _pack_rows(shard_shapes)

    packed = _pack([a[n] for n in SHARDED], rows, BF16)
    gathered = _gather_shards(packed, _own_slot(packed, name="own_weights"), name="gather_weights")
    conv_rows = -(-math.prod(a["conv_w"].shape) // (LANE * 8)) * 8
    conv_all = _all_gather_small(
        jnp.pad(a["conv_w"].reshape(-1), (0, conv_rows * LANE - math.prod(a["conv_w"].shape))).reshape(conv_rows, LANE),
        name="gather_conv_w")
    per_chip = [dict(zip(SHARDED, _unpack(gathered[k], shard_shapes))) for k in range(N_CHIPS)]
    full = {n: jnp.concatenate([per_chip[k][n] for k in range(N_CHIPS)], axis=SHARD_AXIS[n]) for n in SHARDED}
    full["conv_w"] = jnp.concatenate(
        [conv_all[2 * k].reshape(-1)[:math.prod(a["conv_w"].shape)].reshape(a["conv_w"].shape) for k in range(N_CHIPS)],
        axis=SHARD_AXIS["conv_w"])
    for n in REPLICATED:
        full[n] = a[n]

    loss_part, grad_x, grads = _local_step(x, a["positions"][0], a["loss_target"][0], full)
    loss = lax.psum(jnp.sum(loss_part), ("x", "y", "c"))

    slots = []
    for k in range(N_CHIPS):
        parts = [jnp.split(grads[n], N_CHIPS, axis=SHARD_AXIS[n])[k] for n in SHARDED]
        slots.append(_pack(parts, rows, F32))
    g_shard = _reduce_scatter(jnp.stack(slots), name="reduce_grads")

    rep_shapes = [a[n].shape for n in REPLICATED]
    n_rep = sum(math.prod(sh) for sh in rep_shapes)
    rep_rows = -(-n_rep // (LANE * 8)) * 8
    pack_small = lambda arrs: jnp.pad(jnp.concatenate([t.reshape(-1) for t in arrs]), (0, rep_rows * LANE - n_rep)).reshape(rep_rows, LANE)
    g_rep = _sum_slots(_all_gather_small(pack_small([grads[n] for n in REPLICATED]), name="gather_small_grads"),
                       name="add_small_grads")

    out = {"loss": loss, "grad_x": grad_x[None]}
    for n, g in zip(SHARDED, _unpack(g_shard, shard_shapes)):
        flat = lambda t: t.reshape(-1, t.shape[-1])
        d, nm, nv = _adamw(flat(a[n]), flat(g), flat(a["m_" + n]), flat(a["v_" + n]), name="adamw_" + n)
        out["grad_" + n] = g
        out["delta_" + n], out["new_m_" + n], out["new_v_" + n] = (t.reshape(g.shape) for t in (d, nm, nv))
    d_rp, m_rp, v_rp = _adamw(pack_small([a[n] for n in REPLICATED]), g_rep,
                              pack_small([a["m_" + n] for n in REPLICATED]),
                              pack_small([a["v_" + n] for n in REPLICATED]), name="adamw_replicated")
    for prefix, rp_arr in (("grad_", g_rep), ("delta_", d_rp), ("new_m_", m_rp), ("new_v_", v_rp)):
        for n, t in zip(REPLICATED, _unpack(rp_arr.reshape(-1)[:n_rep], rep_shapes)):
            out[prefix + n] = t
    return out


IN_NAMES = ["x", "positions"] + W_NAMES + ["loss_target"] + ["m_" + n for n in W_NAMES] + ["v_" + n for n in W_NAMES]
OUT_NAMES = (["loss", "grad_x"] + ["grad_" + n for n in W_NAMES] + ["delta_" + n for n in W_NAMES]
             + ["new_m_" + n for n in W_NAMES] + ["new_v_" + n for n in W_NAMES])


def kernel(x, positions, ln_ffn1, ffn1_w13, ffn1_w2, ln_mix, w_in, conv_w, conv_b, dt_bias, a_log, d_skip, ssd_norm, w_ssd_out, q_lora_norm, w_uq, kv_lora_norm, w_ukv, q_norm, k_norm, w_mla_out, w_o, ln_ffn2, ffn2_w13, ffn2_w2, loss_target, m_ln_ffn1, m_ffn1_w13, m_ffn1_w2, m_ln_mix, m_w_in, m_conv_w, m_conv_b, m_dt_bias, m_a_log, m_d_skip, m_ssd_norm, m_w_ssd_out, m_q_lora_norm, m_w_uq, m_kv_lora_norm, m_w_ukv, m_q_norm, m_k_norm, m_w_mla_out, m_w_o, m_ln_ffn2, m_ffn2_w13, m_ffn2_w2, v_ln_ffn1, v_ffn1_w13, v_ffn1_w2, v_ln_mix, v_w_in, v_conv_w, v_conv_b, v_dt_bias, v_a_log, v_d_skip, v_ssd_norm, v_w_ssd_out, v_q_lora_norm, v_w_uq, v_kv_lora_norm, v_w_ukv, v_q_norm, v_k_norm, v_w_mla_out, v_w_o, v_ln_ffn2, v_ffn2_w13, v_ffn2_w2):
    given = locals()
    out = _step({n: given[n] for n in IN_NAMES})
    return tuple(out[n] for n in OUT_NAMES)
```

```python
import functools
import math

import jax
import jax.numpy as jnp
from jax import lax
from jax.experimental import pallas as pl
from jax.experimental.pallas import tpu as pltpu

F32 = jnp.float32
BF16 = jnp.bfloat16

D_MODEL = 1024
DEPTH = 2
D_FF = 2816
SSD_D_INNER = 2048
SSD_HEADS = 32
SSD_HEAD_DIM = 64
SSD_GROUPS = 4
SSD_STATE = 128
SSD_CHUNK = 128
SSD_CONV = 4
SSD_CONV_DIM = 3072
MLA_HEADS = 8
MLA_Q_LORA = 512
MLA_KV_LORA = 256
MLA_NOPE = 128
MLA_ROPE = 64
MLA_V = 128
MLA_QK = 192
ROPE_THETA = 10000.0
EPS = 1e-6
ADAM_LR = 0.001
ADAM_B1 = 0.9
ADAM_B2 = 0.999
ADAM_EPS = 1e-08
ADAM_WD = 0.01
ADAM_STEP = 10

PROJ_W = 8064
OFF_Z, OFF_XBC, OFF_GATES, OFF_CQ, OFF_CKV, OFF_KRDT = 0, 2048, 5120, 7168, 7680, 7936

LANE = 128
VMEM_LIMIT = 48 * 1024 * 1024
HI = lax.Precision.HIGHEST


def _cp(*sem):
    return pltpu.CompilerParams(dimension_semantics=sem, vmem_limit_bytes=VMEM_LIMIT)


def _pick(dim, target, align):
    if dim <= target:
        return dim
    b = (target // align) * align
    while b >= align:
        if dim % b == 0:
            return b
        b -= align
    raise ValueError(f"no block for {dim} (target {target}, align {align})")


def _silu(x):
    return x * jax.nn.sigmoid(x)


def _dsilu(x):
    s = jax.nn.sigmoid(x)
    return s * (1.0 + x * (1.0 - s))


def _matmul(a, b, mode, *, name, out_dtype=F32, scale=1.0, res=None):
    if mode == "nn":
        (m, k), (k2, n) = a.shape, b.shape
    elif mode == "nt":
        (m, k), (n, k2) = a.shape, b.shape
    else:
        (k, m), (k2, n) = a.shape, b.shape
    assert k == k2, (a.shape, b.shape, mode)
    if mode == "tn":
        bn, bk = _pick(n, 2816, LANE), _pick(k, 512, 8)
        bm = _pick(m, max(256, (1408 * 1024 // bn) // LANE * LANE), LANE)
    else:
        bm, bn, bk = _pick(m, 1024, 8), _pick(n, 1152, LANE), _pick(k, 1536, LANE)
    nk = k // bk

    def body(a_ref, b_ref, *rest):
        res_ref = rest[0] if res is not None else None
        o_ref = rest[-2] if nk > 1 else rest[-1]
        kk = pl.program_id(2)
        av = a_ref[...].astype(BF16)
        bv = b_ref[...].astype(BF16)
        if mode == "nn":
            dims = (((1,), (0,)), ((), ()))
        elif mode == "nt":
            dims = (((1,), (1,)), ((), ()))
        else:
            dims = (((0,), (0,)), ((), ()))
        part = lax.dot_general(av, bv, dims, preferred_element_type=F32)

        def finish(total):
            out = total * scale
            if res_ref is not None:
                out = res_ref[...] + out
            o_ref[...] = out.astype(o_ref.dtype)

        if nk == 1:
            finish(part)
            return
        acc_ref = rest[-1]

        @pl.when(kk == 0)
        def _():
            acc_ref[...] = part

        @pl.when((kk > 0) & (kk < nk - 1))
        def _():
            acc_ref[...] += part

        @pl.when(kk == nk - 1)
        def _():
            finish(acc_ref[...] + part)

    o_spec = pl.BlockSpec((bm, bn), lambda i, j, kk: (i, j))
    if mode == "nn":
        a_spec = pl.BlockSpec((bm, bk), lambda i, j, kk: (i, kk))
        b_spec = pl.BlockSpec((bk, bn), lambda i, j, kk: (kk, j))
    elif mode == "nt":
        a_spec = pl.BlockSpec((bm, bk), lambda i, j, kk: (i, kk))
        b_spec = pl.BlockSpec((bn, bk), lambda i, j, kk: (j, kk))
    else:
        a_spec = pl.BlockSpec((bk, bm), lambda i, j, kk: (kk, i))
        b_spec = pl.BlockSpec((bk, bn), lambda i, j, kk: (kk, j))
    return pl.pallas_call(
        body, name=name,
        grid=(m // bm, n // bn, nk),
        in_specs=[a_spec, b_spec] + ([o_spec] if res is not None else []),
        out_specs=o_spec,
        out_shape=jax.ShapeDtypeStruct((m, n), out_dtype),
        scratch_shapes=[pltpu.VMEM((bm, bn), F32)] if nk > 1 else [],
        compiler_params=_cp("parallel", "parallel", "arbitrary"),
    )(*((a, b) + ((res,) if res is not None else ())))


def _rms_fwd(x, g, *, name, col=0, width=None):
    r = x.shape[0]
    w = width or x.shape[1]
    tr = _pick(r, 512, 16)

    def body(x_ref, g_ref, o_ref):
        xv = x_ref[...]
        rs = lax.rsqrt(jnp.mean(xv * xv, axis=-1, keepdims=True) + EPS)
        o_ref[...] = (xv * rs * g_ref[...]).astype(o_ref.dtype)

    return pl.pallas_call(
        body, name=name, grid=(r // tr,),
        in_specs=[pl.BlockSpec((tr, w), lambda i: (i, col)), pl.BlockSpec((1, w), lambda i: (0, 0))],
        out_specs=pl.BlockSpec((tr, w), lambda i: (i, 0)),
        out_shape=jax.ShapeDtypeStruct((r, w), BF16),
        compiler_params=_cp("parallel"),
    )(x, g)


def _rms_bwd(x, g, dy, *, name, col=0, width=None, res=None):
    r = x.shape[0]
    w = width or x.shape[1]
    tr = _pick(r, 512, 8)

    def body(x_ref, g_ref, dy_ref, *rest):
        res_ref = rest[0] if res is not None else None
        dx_ref, dg_ref = rest[-2:]
        i = pl.program_id(0)
        xv = x_ref[...]
        dyv = dy_ref[...]
        rs = lax.rsqrt(jnp.mean(xv * xv, axis=-1, keepdims=True) + EPS)
        xh = xv * rs
        dxh = dyv * g_ref[...]
        mm = jnp.mean(dxh * xh, axis=-1, keepdims=True)
        dx = rs * (dxh - xh * mm)
        if res_ref is not None:
            dx = res_ref[...] + dx
        dx_ref[...] = dx
        part = jnp.sum(dyv * xh, axis=0, keepdims=True)

        @pl.when(i == 0)
        def _():
            dg_ref[...] = part

        @pl.when(i > 0)
        def _():
            dg_ref[...] += part

    blk = pl.BlockSpec((tr, w), lambda i: (i, 0))
    return pl.pallas_call(
        body, name=name, grid=(r // tr,),
        in_specs=[pl.BlockSpec((tr, w), lambda i: (i, col)), pl.BlockSpec((1, w), lambda i: (0, 0)), blk]
        + ([blk] if res is not None else []),
        out_specs=[blk, pl.BlockSpec((1, w), lambda i: (0, 0))],
        out_shape=[jax.ShapeDtypeStruct((r, w), F32), jax.ShapeDtypeStruct((1, w), F32)],
        compiler_params=_cp("arbitrary"),
    )(*((x, g, dy) + ((res,) if res is not None else ())))


def _gated_rms_fwd(y, proj, g, *, name):
    r, w = y.shape
    tr = _pick(r, 256, 8)

    def body(y_ref, z_ref, g_ref, o_ref):
        t = y_ref[...] * _silu(z_ref[...])
        rs = lax.rsqrt(jnp.mean(t * t, axis=-1, keepdims=True) + EPS)
        o_ref[...] = (t * rs * g_ref[...]).astype(o_ref.dtype)

    return pl.pallas_call(
        body, name=name, grid=(r // tr,),
        in_specs=[pl.BlockSpec((tr, w), lambda i: (i, 0)), pl.BlockSpec((tr, w), lambda i: (i, OFF_Z // w)),
                  pl.BlockSpec((1, w), lambda i: (0, 0))],
        out_specs=pl.BlockSpec((tr, w), lambda i: (i, 0)),
        out_shape=jax.ShapeDtypeStruct((r, w), BF16),
        compiler_params=_cp("parallel"),
    )(y, proj, g)


def _gated_rms_bwd(y, proj, g, do, *, name):
    r, w = y.shape
    tr = _pick(r, 256, 8)

    def body(y_ref, z_ref, g_ref, do_ref, dy_ref, dz_ref, dg_ref):
        i = pl.program_id(0)
        yv, zv, dov = y_ref[...], z_ref[...], do_ref[...]
        sz = _silu(zv)
        t = yv * sz
        rs = lax.rsqrt(jnp.mean(t * t, axis=-1, keepdims=True) + EPS)
        th = t * rs
        dth = dov * g_ref[...]
        mm = jnp.mean(dth * th, axis=-1, keepdims=True)
        dt = rs * (dth - th * mm)
        dy_ref[...] = dt * sz
        dz_ref[...] = dt * yv * _dsilu(zv)
        part = jnp.sum(dov * th, axis=0, keepdims=True)

        @pl.when(i == 0)
        def _():
            dg_ref[...] = part

        @pl.when(i > 0)
        def _():
            dg_ref[...] += part

    blk = pl.BlockSpec((tr, w), lambda i: (i, 0))
    vec = pl.BlockSpec((1, w), lambda i: (0, 0))
    return pl.pallas_call(
        body, name=name, grid=(r // tr,),
        in_specs=[blk, pl.BlockSpec((tr, w), lambda i: (i, OFF_Z // w)), vec, blk],
        out_specs=[blk, blk, vec],
        out_shape=[jax.ShapeDtypeStruct((r, w), F32), jax.ShapeDtypeStruct((r, w), F32),
                   jax.ShapeDtypeStruct((1, w), F32)],
        compiler_params=_cp("arbitrary"),
    )(y, proj, g, do)


def _ffn_up(n, w13, *, name):
    m, k = n.shape
    f = w13.shape[1] // 2
    bm, bn = _pick(m, 1024, 16), _pick(f, 1408, LANE)
    nj = f // bn

    def body(a_ref, wg_ref, wu_ref, act_ref, g_ref, u_ref):
        a = a_ref[...].astype(BF16)
        g = _dot(a, wg_ref[...].astype(BF16))
        u = _dot(a, wu_ref[...].astype(BF16))
        act_ref[...] = (_silu(g) * u).astype(act_ref.dtype)
        g_ref[...] = g.astype(g_ref.dtype)
        u_ref[...] = u.astype(u_ref.dtype)

    out = pl.BlockSpec((bm, bn), lambda j, i: (i, j))
    sh = jax.ShapeDtypeStruct((m, f), BF16)
    return pl.pallas_call(
        body, name=name, grid=(nj, m // bm),
        in_specs=[pl.BlockSpec((bm, k), lambda j, i: (i, 0)), pl.BlockSpec((k, bn), lambda j, i: (0, j)),
                  pl.BlockSpec((k, bn), lambda j, i: (0, nj + j))],
        out_specs=[out, out, out], out_shape=[sh, sh, sh],
        compiler_params=_cp("parallel", "parallel"),
    )(n, w13, w13)


def _swiglu_bwd(g, u, da, *, name):
    r, f = g.shape
    tr = _pick(r, 256, 16)

    def body(g_ref, u_ref, da_ref, o_ref):
        gv, uv, dav = g_ref[...].astype(F32), u_ref[...].astype(F32), da_ref[...].astype(F32)
        o_ref[:, :f] = (dav * uv * _dsilu(gv)).astype(o_ref.dtype)
        o_ref[:, f:] = (dav * _silu(gv)).astype(o_ref.dtype)

    blk = pl.BlockSpec((tr, f), lambda i: (i, 0))
    return pl.pallas_call(
        body, name=name, grid=(r // tr,),
        in_specs=[blk, blk, blk],
        out_specs=pl.BlockSpec((tr, 2 * f), lambda i: (i, 0)),
        out_shape=jax.ShapeDtypeStruct((r, 2 * f), BF16),
        compiler_params=_cp("parallel"),
    )(g, u, da)


CONV_TS = 1024
CONV_TC = 512


def _conv_pre(x, carry, w_ref, b_ref):
    ts = x.shape[0]
    row8 = lax.broadcasted_iota(jnp.int32, (8, x.shape[1]), 0)
    head_x = x[0:8]
    shifted, shifted_head = [], []
    for j in range(SSD_CONV):
        if j == 0:
            shifted.append(x)
            shifted_head.append(head_x)
        else:
            shifted.append(pltpu.roll(x, j, 0))
            shifted_head.append(jnp.where(row8 < j, pltpu.roll(carry, j, 0), pltpu.roll(head_x, j, 0)))
    pre = b_ref[...] + sum(w_ref[SSD_CONV - 1 - j:SSD_CONV - j, :] * shifted[j] for j in range(SSD_CONV))
    pre_head = b_ref[...] + sum(w_ref[SSD_CONV - 1 - j:SSD_CONV - j, :] * shifted_head[j] for j in range(SSD_CONV))
    del ts
    return pre, pre_head, shifted, shifted_head


def _conv_fwd(proj, w, b, *, name):
    s = proj.shape[0]
    c = w.shape[1]
    ts, tc = _pick(s, CONV_TS, 8), CONV_TC
    off = OFF_XBC // tc

    def body(x_ref, w_ref, b_ref, o_ref, carry_ref):
        t = pl.program_id(1)

        @pl.when(t == 0)
        def _():
            carry_ref[...] = jnp.zeros_like(carry_ref)

        x = x_ref[...]
        pre, pre_head, _, _ = _conv_pre(x, carry_ref[...], w_ref, b_ref)
        o_ref[...] = _silu(pre)
        o_ref[0:8, :] = _silu(pre_head)
        carry_ref[...] = x[ts - 8:ts]

    return pl.pallas_call(
        body, name=name, grid=(c // tc, s // ts),
        in_specs=[pl.BlockSpec((ts, tc), lambda j, t: (t, j + off)), pl.BlockSpec((SSD_CONV, tc), lambda j, t: (0, j)),
                  pl.BlockSpec((1, tc), lambda j, t: (0, j))],
        out_specs=pl.BlockSpec((ts, tc), lambda j, t: (t, j)),
        out_shape=jax.ShapeDtypeStruct((s, c), F32),
        scratch_shapes=[pltpu.VMEM((8, tc), F32)],
        compiler_params=_cp("parallel", "arbitrary"),
    )(proj, w, b)


def _conv_bwd_pre(proj, w, b, dy, *, name):
    s = proj.shape[0]
    c = w.shape[1]
    ts, tc = _pick(s, CONV_TS, 8), CONV_TC
    off = OFF_XBC // tc

    def body(x_ref, w_ref, b_ref, dy_ref, dp_ref, dw_ref, db_ref, carry_ref):
        t = pl.program_id(1)

        @pl.when(t == 0)
        def _():
            carry_ref[...] = jnp.zeros_like(carry_ref)
            dw_ref[...] = jnp.zeros_like(dw_ref)
            db_ref[...] = jnp.zeros_like(db_ref)

        x = x_ref[...]
        pre, pre_head, shifted, shifted_head = _conv_pre(x, carry_ref[...], w_ref, b_ref)
        dyv = dy_ref[...]
        dp = dyv * _dsilu(pre)
        dp_head = dyv[0:8] * _dsilu(pre_head)
        row = lax.broadcasted_iota(jnp.int32, dp.shape, 0)
        dp_tail = jnp.where(row >= 8, dp, 0.0)
        dp_ref[...] = dp
        dp_ref[0:8, :] = dp_head
        db_ref[...] += jnp.sum(dp_tail, axis=0, keepdims=True) + jnp.sum(dp_head, axis=0, keepdims=True)
        for j in range(SSD_CONV):
            kk = SSD_CONV - 1 - j
            dw_ref[kk:kk + 1, :] += (jnp.sum(dp_tail * shifted[j], axis=0, keepdims=True)
                                     + jnp.sum(dp_head * shifted_head[j], axis=0, keepdims=True))
        carry_ref[...] = x[ts - 8:ts]

    return pl.pallas_call(
        body, name=name, grid=(c // tc, s // ts),
        in_specs=[pl.BlockSpec((ts, tc), lambda j, t: (t, j + off)), pl.BlockSpec((SSD_CONV, tc), lambda j, t: (0, j)),
                  pl.BlockSpec((1, tc), lambda j, t: (0, j)), pl.BlockSpec((ts, tc), lambda j, t: (t, j))],
        out_specs=[pl.BlockSpec((ts, tc), lambda j, t: (t, j)), pl.BlockSpec((SSD_CONV, tc), lambda j, t: (0, j)),
                   pl.BlockSpec((1, tc), lambda j, t: (0, j))],
        out_shape=[jax.ShapeDtypeStruct((s, c), F32), jax.ShapeDtypeStruct((SSD_CONV, c), F32),
                   jax.ShapeDtypeStruct((1, c), F32)],
        scratch_shapes=[pltpu.VMEM((8, tc), F32)],
        compiler_params=_cp("parallel", "arbitrary"),
    )(proj, w, b, dy)


def _conv_bwd_x(dp, w, *, name):
    s, c = dp.shape
    ts, tc = _pick(s, CONV_TS, 8), CONV_TC
    nt = s // ts

    def body(d_ref, w_ref, o_ref, carry_ref):
        t = pl.program_id(1)

        @pl.when(t == 0)
        def _():
            carry_ref[...] = jnp.zeros_like(carry_ref)

        d = d_ref[...]
        carry = carry_ref[...]
        row8 = lax.broadcasted_iota(jnp.int32, (8, tc), 0)
        tail = d[ts - 8:ts]
        acc = w_ref[SSD_CONV - 1:SSD_CONV, :] * d
        acc_tail = w_ref[SSD_CONV - 1:SSD_CONV, :] * tail
        for j in range(1, SSD_CONV):
            wj = w_ref[SSD_CONV - 1 - j:SSD_CONV - j, :]
            acc = acc + wj * pltpu.roll(d, ts - j, 0)
            up_tail = jnp.where(row8 >= 8 - j, pltpu.roll(carry, 8 - j, 0), pltpu.roll(tail, 8 - j, 0))
            acc_tail = acc_tail + wj * up_tail
        o_ref[...] = acc
        o_ref[ts - 8:ts, :] = acc_tail
        carry_ref[...] = d[0:8]

    return pl.pallas_call(
        body, name=name, grid=(c // tc, nt),
        in_specs=[pl.BlockSpec((ts, tc), lambda j, t: (nt - 1 - t, j)), pl.BlockSpec((SSD_CONV, tc), lambda j, t: (0, j))],
        out_specs=pl.BlockSpec((ts, tc), lambda j, t: (nt - 1 - t, j)),
        out_shape=jax.ShapeDtypeStruct((s, c), F32),
        scratch_shapes=[pltpu.VMEM((8, tc), F32)],
        compiler_params=_cp("parallel", "arbitrary"),
    )(dp, w)


def _merge_fwd(proj, ys, ym, *, name):
    r, w = ys.shape
    tr = _pick(r, 512, 8)
    off = OFF_GATES // w

    def body(g1_ref, g2_ref, ys_ref, ym_ref, o_ref):
        o_ref[...] = (jax.nn.sigmoid(g1_ref[...]) * ys_ref[...]
                      + jax.nn.sigmoid(g2_ref[...]) * ym_ref[...]).astype(o_ref.dtype)

    blk = pl.BlockSpec((tr, w), lambda i: (i, 0))
    return pl.pallas_call(
        body, name=name, grid=(r // tr,),
        in_specs=[pl.BlockSpec((tr, w), lambda i: (i, off)), pl.BlockSpec((tr, w), lambda i: (i, off + 1)), blk, blk],
        out_specs=blk, out_shape=jax.ShapeDtypeStruct((r, w), BF16),
        compiler_params=_cp("parallel"),
    )(proj, proj, ys, ym)


def _merge_bwd(proj, ys, ym, dm, *, name):
    r, w = ys.shape
    tr = _pick(r, 512, 8)
    off = OFF_GATES // w

    def body(g1_ref, g2_ref, ys_ref, ym_ref, dm_ref, dg_ref, dys_ref, dym_ref):
        s1, s2 = jax.nn.sigmoid(g1_ref[...]), jax.nn.sigmoid(g2_ref[...])
        dmv = dm_ref[...]
        dys_ref[...] = (dmv * s1).astype(dys_ref.dtype)
        dym_ref[...] = (dmv * s2).astype(dym_ref.dtype)
        dg_ref[:, :w] = dmv * ys_ref[...] * s1 * (1.0 - s1)
        dg_ref[:, w:] = dmv * ym_ref[...] * s2 * (1.0 - s2)

    blk = pl.BlockSpec((tr, w), lambda i: (i, 0))
    return pl.pallas_call(
        body, name=name, grid=(r // tr,),
        in_specs=[pl.BlockSpec((tr, w), lambda i: (i, off)), pl.BlockSpec((tr, w), lambda i: (i, off + 1)), blk, blk, blk],
        out_specs=[pl.BlockSpec((tr, 2 * w), lambda i: (i, 0)), blk, blk],
        out_shape=[jax.ShapeDtypeStruct((r, 2 * w), F32), jax.ShapeDtypeStruct((r, w), BF16),
                   jax.ShapeDtypeStruct((r, w), BF16)],
        compiler_params=_cp("parallel"),
    )(proj, proj, ys, ym, dm)


def _loss_fwd_bwd(y, target, *, name):
    r, w = y.shape
    tr = _pick(r, 512, 8)

    def body(y_ref, t_ref, l_ref, dy_ref):
        i = pl.program_id(0)
        e = y_ref[...] - t_ref[...]
        dy_ref[...] = e * (1.0 / w)
        part = jnp.sum(e * e, axis=0, keepdims=True) * (0.5 / w)

        @pl.when(i == 0)
        def _():
            l_ref[...] = part

        @pl.when(i > 0)
        def _():
            l_ref[...] += part

    blk = pl.BlockSpec((tr, w), lambda i: (i, 0))
    return pl.pallas_call(
        body, name=name, grid=(r // tr,),
        in_specs=[blk, blk],
        out_specs=[pl.BlockSpec((1, w), lambda i: (0, 0)), blk],
        out_shape=[jax.ShapeDtypeStruct((1, w), F32), jax.ShapeDtypeStruct((r, w), F32)],
        compiler_params=_cp("arbitrary"),
    )(y, target)


def _adamw(w, g, m, v, *, name):
    r, c = w.shape
    tr = _pick(r, max(8, (1 << 20) // (4 * c) // 8 * 8), 8)
    c1 = 1.0 - ADAM_B1 ** ADAM_STEP
    c2 = 1.0 - ADAM_B2 ** ADAM_STEP

    def body(w_ref, g_ref, m_ref, v_ref, d_ref, nm_ref, nv_ref):
        gv = g_ref[...]
        nm = ADAM_B1 * m_ref[...] + (1.0 - ADAM_B1) * gv
        nv = ADAM_B2 * v_ref[...] + (1.0 - ADAM_B2) * (gv * gv)
        nm_ref[...] = nm
        nv_ref[...] = nv
        d_ref[...] = -ADAM_LR * ((nm / c1) / (jnp.sqrt(nv / c2) + ADAM_EPS) + ADAM_WD * w_ref[...])

    blk = pl.BlockSpec((tr, c), lambda i: (i, 0))
    sh = jax.ShapeDtypeStruct((r, c), F32)
    return pl.pallas_call(
        body, name=name, grid=(r // tr,),
        in_specs=[blk] * 4, out_specs=[blk] * 3, out_shape=[sh] * 3,
        compiler_params=_cp("parallel"),
    )(w, g, m, v)


def _softplus(x):
    return jnp.maximum(x, 0.0) + jnp.log(1.0 + jnp.exp(-jnp.abs(x)))


def _ssd_common(dtr_ref, dtrT_ref, dtb_ref, dtbT_ref, al_ref, alT_ref, e_ref):
    L = SSD_CHUNK
    ri = lax.broadcasted_iota(jnp.int32, (L, L), 0)
    cj = lax.broadcasted_iota(jnp.int32, (L, L), 1)
    tril = (ri >= cj).astype(F32)
    triu = (ri <= cj).astype(F32)
    a = -jnp.exp(al_ref[...])
    aT = -jnp.exp(alT_ref[...])
    pre = dtr_ref[...] + dtb_ref[...]
    preT = dtrT_ref[...] + dtbT_ref[...]
    dt = _softplus(pre)
    dtT = _softplus(preT)
    acum = jnp.dot(tril, dt * a, precision=HI, preferred_element_type=F32)
    acumT = jnp.dot(dtT * aT, triu, precision=HI, preferred_element_type=F32)
    e = e_ref[...]
    dt_x = jnp.dot(dt, e, precision=HI, preferred_element_type=F32)
    acum_x = jnp.dot(acum, e, precision=HI, preferred_element_type=F32)
    last_x = acum_x[L - 1:L, :]
    return dict(ri=ri, cj=cj, tril=tril, triu=triu, a=a, aT=aT, pre=pre, preT=preT, dt=dt, dtT=dtT,
                acum=acum, acumT=acumT, dt_x=dt_x, eacum_x=jnp.exp(acum_x), w_x=jnp.exp(last_x - acum_x),
                elast_x=jnp.exp(last_x))


def _dot_nt(a, b):
    return lax.dot_general(a, b, (((1,), (1,)), ((), ())), preferred_element_type=F32)


def _dot_tn(a, b):
    return lax.dot_general(a, b, (((0,), (0,)), ((), ())), preferred_element_type=F32)


def _dot(a, b):
    return jnp.dot(a, b, preferred_element_type=F32)


def _ssd_specs(nc, rev):
    L = SSD_CHUNK
    ix = (lambda c: nc - 1 - c) if rev else (lambda c: c)
    return [
        pl.BlockSpec((L, SSD_D_INNER), lambda c: (ix(c), 0)),
        pl.BlockSpec((L, 512), lambda c: (ix(c), 4)),
        pl.BlockSpec((L, 512), lambda c: (ix(c), 5)),
        pl.BlockSpec((L, SSD_HEADS), lambda c: (ix(c), 0)),
        pl.BlockSpec((SSD_HEADS, L), lambda c: (0, ix(c))),
        pl.BlockSpec((1, SSD_HEADS), lambda c: (0, 0)),
        pl.BlockSpec((SSD_HEADS, 1), lambda c: (0, 0)),
        pl.BlockSpec((1, SSD_HEADS), lambda c: (0, 0)),
        pl.BlockSpec((SSD_HEADS, 1), lambda c: (0, 0)),
        pl.BlockSpec((1, SSD_D_INNER), lambda c: (0, 0)),
        pl.BlockSpec((SSD_HEADS, SSD_D_INNER), lambda c: (0, 0)),
    ]


def _ssd_fwd(xc, dtr, dtrT, dtb, dtbT, alog, alogT, dskx, expand, *, name):
    s = xc.shape[0]
    L = SSD_CHUNK
    nc = s // L

    def body(x_ref, b_ref, c_ref, dtr_ref, dtrT_ref, dtb_ref, dtbT_ref, al_ref, alT_ref, dsk_ref, e_ref,
             y_ref, st_ref, state):
        ci = pl.program_id(0)

        @pl.when(ci == 0)
        def _():
            state[...] = jnp.zeros_like(state)

        st_ref[0] = state[...]
        q = _ssd_common(dtr_ref, dtrT_ref, dtb_ref, dtbT_ref, al_ref, alT_ref, e_ref)
        causal = q["ri"] >= q["cj"]
        lane_lo = q["cj"] < 64
        x = x_ref[...]
        xdt = x * q["dt_x"]
        xdt_b = xdt.astype(BF16)
        xdtw_b = (xdt * q["w_x"]).astype(BF16)
        for g in range(SSD_GROUPS):
            bg = b_ref[:, 128 * g:128 * g + 128]
            cg_b = c_ref[:, 128 * g:128 * g + 128].astype(BF16)
            cb = _dot_nt(cg_b, bg.astype(BF16))
            bgT_b = bg.T.astype(BF16)
            s0 = state[g]
            for jj in range(4):
                j = 4 * g + jj
                sl = slice(128 * j, 128 * j + 128)
                sls = slice(128 * jj, 128 * jj + 128)
                ms = []
                for h in (2 * j, 2 * j + 1):
                    seg = q["acum"][:, h:h + 1] - q["acumT"][h:h + 1, :]
                    decay = jnp.exp(jnp.where(causal, seg, -jnp.inf))
                    ms.append((cb * decay).astype(BF16))
                mcat = jnp.concatenate(ms, axis=1)
                xp = xdt_b[:, sl]
                zero = jnp.zeros_like(xp)
                xstack = jnp.concatenate([jnp.where(lane_lo, xp, zero), jnp.where(lane_lo, zero, xp)], axis=0)
                y = _dot(mcat, xstack)
                y = y + q["eacum_x"][:, sl] * _dot(cg_b, s0[:, sls].astype(BF16))
                y = y + x[:, sl] * dsk_ref[:, sl]
                y_ref[:, sl] = y
                state[g, :, sls] = s0[:, sls] * q["elast_x"][:, sl] + _dot(bgT_b, xdtw_b[:, sl])

    return pl.pallas_call(
        body, name=name, grid=(nc,),
        in_specs=_ssd_specs(nc, False),
        out_specs=[pl.BlockSpec((L, SSD_D_INNER), lambda c: (c, 0)),
                   pl.BlockSpec((1, SSD_GROUPS, SSD_STATE, 512), lambda c: (c, 0, 0, 0))],
        out_shape=[jax.ShapeDtypeStruct((s, SSD_D_INNER), F32),
                   jax.ShapeDtypeStruct((nc, SSD_GROUPS, SSD_STATE, 512), F32)],
        scratch_shapes=[pltpu.VMEM((SSD_GROUPS, SSD_STATE, 512), F32)],
        compiler_params=_cp("arbitrary"),
    )(xc, xc, xc, dtr, dtrT, dtb, dtbT, alog, alogT, dskx, expand)


def _ssd_bwd(xc, dtr, dtrT, dtb, dtbT, alog, alogT, dskx, expand, expandT, states, dy, *, name):
    s = xc.shape[0]
    L = SSD_CHUNK
    H = SSD_HEADS
    nc = s // L

    def body(x_ref, b_ref, c_ref, dtr_ref, dtrT_ref, dtb_ref, dtbT_ref, al_ref, alT_ref, dsk_ref, e_ref,
             et_ref, st_ref, dy_ref,
             dxc_ref, ddtc_ref, ddtr_ref, dbc_ref, dbr_ref, dac_ref, dar_ref, ddsk_ref, dstate):
        ci = pl.program_id(0)

        @pl.when(ci == 0)
        def _():
            dstate[...] = jnp.zeros_like(dstate)
            dbc_ref[...] = jnp.zeros_like(dbc_ref)
            dbr_ref[...] = jnp.zeros_like(dbr_ref)
            dac_ref[...] = jnp.zeros_like(dac_ref)
            dar_ref[...] = jnp.zeros_like(dar_ref)
            ddsk_ref[...] = jnp.zeros_like(ddsk_ref)

        q = _ssd_common(dtr_ref, dtrT_ref, dtb_ref, dtbT_ref, al_ref, alT_ref, e_ref)
        ri, cj = q["ri"], q["cj"]
        causal = ri >= cj
        causalT = ri <= cj
        lane_lo = cj < 64
        lane_h = lax.broadcasted_iota(jnp.int32, (1, H), 1)
        sub_h = lax.broadcasted_iota(jnp.int32, (H, 1), 0)
        x = x_ref[...]
        dyv = dy_ref[...]
        xdt = x * q["dt_x"]
        xdt_b = xdt.astype(BF16)
        xdtw = xdt * q["w_x"]
        xdtw_b = xdtw.astype(BF16)
        edy = q["eacum_x"] * dyv
        edy_b = edy.astype(BF16)
        dyv_b = dyv.astype(BF16)
        dacum_col = jnp.zeros((L, H), F32)
        dacum_row = jnp.zeros((H, L), F32)
        dxdt_t, yoff_t, u_t, r_t = [], [], [], []
        for g in range(SSD_GROUPS):
            bg = b_ref[:, 128 * g:128 * g + 128]
            cg = c_ref[:, 128 * g:128 * g + 128]
            bg_b, cg_b = bg.astype(BF16), cg.astype(BF16)
            cb = _dot_nt(cg_b, bg_b)
            cbT = _dot_nt(bg_b, cg_b)
            cgT_b = cg.T.astype(BF16)
            s0 = st_ref[0, g]
            ds = dstate[g]
            s0_b, ds_b = s0.astype(BF16), ds.astype(BF16)
            dcb = jnp.zeros((L, L), F32)
            for jj in range(4):
                j = 4 * g + jj
                sl = slice(128 * j, 128 * j + 128)
                sls = slice(128 * jj, 128 * jj + 128)
                decs, mts = [], []
                for h in (2 * j, 2 * j + 1):
                    seg = q["acum"][:, h:h + 1] - q["acumT"][h:h + 1, :]
                    decs.append(jnp.exp(jnp.where(causal, seg, -jnp.inf)))
                    mts.append((cbT * jnp.exp(jnp.where(causalT, -seg, -jnp.inf))).astype(BF16))
                dyt_b = dyv_b[:, sl]
                zero = jnp.zeros_like(dyt_b)
                dystack = jnp.concatenate([jnp.where(lane_lo, dyt_b, zero), jnp.where(lane_lo, zero, dyt_b)], axis=0)
                dxs = _dot(jnp.concatenate(mts, axis=0), dyt_b)
                dxdt = jnp.where(lane_lo, dxs[:L], dxs[L:])
                dmcat = _dot_nt(dystack, xdt_b[:, sl])
                for idx, h in enumerate((2 * j, 2 * j + 1)):
                    dm = dmcat[L * idx:L * idx + L]
                    dcb = dcb + dm * decs[idx]
                    dseg = dm * cb * decs[idx]
                    dacum_col = dacum_col + jnp.sum(dseg, axis=1, keepdims=True) * (lane_h == h).astype(F32)
                    dacum_row = dacum_row - (sub_h == h).astype(F32) * jnp.sum(dseg, axis=0, keepdims=True)
                gmat = _dot(cg_b, s0_b[:, sls])
                yoff_t.append(edy[:, sl] * gmat)
                qm = _dot(bg_b, ds_b[:, sls])
                dxdt_t.append(dxdt + qm * q["w_x"][:, sl])
                u_t.append(qm * xdtw[:, sl])
                r_t.append(ds[:, sls] * s0[:, sls] * q["elast_x"][:, sl])
                dstate[g, :, sls] = ds[:, sls] * q["elast_x"][:, sl] + _dot(cgT_b, edy_b[:, sl])
            gsl = slice(512 * g, 512 * g + 512)
            dcb_b = dcb.astype(BF16)
            dcg = _dot(dcb_b, bg_b) + _dot_nt(edy_b[:, gsl], s0_b)
            dbg = _dot(dcb.T.astype(BF16), cg_b) + _dot_nt(xdtw_b[:, gsl], ds_b)
            dxc_ref[:, SSD_D_INNER + 128 * g:SSD_D_INNER + 128 * g + 128] = dbg
            dxc_ref[:, SSD_D_INNER + 512 + 128 * g:SSD_D_INNER + 512 + 128 * g + 128] = dcg
        et = et_ref[...]
        dxdt_all = jnp.concatenate(dxdt_t, axis=1)
        yoff = jnp.concatenate(yoff_t, axis=1)
        uu = jnp.concatenate(u_t, axis=1)
        rr = jnp.concatenate(r_t, axis=1)
        dacum_col = dacum_col + jnp.dot(yoff - uu, et, precision=HI, preferred_element_type=F32)
        dlast = jnp.sum(jnp.dot(uu + rr, et, precision=HI, preferred_element_type=F32), axis=0, keepdims=True)
        row_lh = lax.broadcasted_iota(jnp.int32, (L, H), 0)
        dacum_col = dacum_col + jnp.where(row_lh == L - 1, dlast, 0.0)
        d_dta_col = jnp.dot(q["triu"], dacum_col, precision=HI, preferred_element_type=F32)
        d_dta_row = jnp.dot(dacum_row, q["tril"], precision=HI, preferred_element_type=F32)
        ddt_col = d_dta_col * q["a"] + jnp.dot(dxdt_all * x, et, precision=HI, preferred_element_type=F32)
        ddt_row = d_dta_row * q["aT"]
        ddtr_col = ddt_col * jax.nn.sigmoid(q["pre"])
        ddtr_row = ddt_row * jax.nn.sigmoid(q["preT"])
        ddtc_ref[...] = ddtr_col
        ddtr_ref[...] = ddtr_row
        dac_ref[...] += jnp.sum(d_dta_col * q["dt"], axis=0, keepdims=True)
        dar_ref[...] += jnp.sum(d_dta_row * q["dtT"], axis=1, keepdims=True)
        dbc_ref[...] += jnp.sum(ddtr_col, axis=0, keepdims=True)
        dbr_ref[...] += jnp.sum(ddtr_row, axis=1, keepdims=True)
        ddsk_ref[...] += jnp.sum(dyv * x, axis=0, keepdims=True)
        dxc_ref[:, 0:SSD_D_INNER] = dxdt_all * q["dt_x"] + dyv * dsk_ref[...]

    rv = lambda c: nc - 1 - c
    in_specs = _ssd_specs(nc, True) + [
        pl.BlockSpec((SSD_D_INNER, H), lambda c: (0, 0)),
        pl.BlockSpec((1, SSD_GROUPS, SSD_STATE, 512), lambda c: (rv(c), 0, 0, 0)),
        pl.BlockSpec((L, SSD_D_INNER), lambda c: (rv(c), 0)),
    ]
    vec_c = pl.BlockSpec((1, H), lambda c: (0, 0))
    vec_r = pl.BlockSpec((H, 1), lambda c: (0, 0))
    return pl.pallas_call(
        body, name=name, grid=(nc,),
        in_specs=in_specs,
        out_specs=[pl.BlockSpec((L, SSD_CONV_DIM), lambda c: (rv(c), 0)),
                   pl.BlockSpec((L, H), lambda c: (rv(c), 0)),
                   pl.BlockSpec((H, L), lambda c: (0, rv(c))),
                   vec_c, vec_r, vec_c, vec_r,
                   pl.BlockSpec((1, SSD_D_INNER), lambda c: (0, 0))],
        out_shape=[jax.ShapeDtypeStruct((s, SSD_CONV_DIM), F32),
                   jax.ShapeDtypeStruct((s, H), F32), jax.ShapeDtypeStruct((H, s), F32),
                   jax.ShapeDtypeStruct((1, H), F32), jax.ShapeDtypeStruct((H, 1), F32),
                   jax.ShapeDtypeStruct((1, H), F32), jax.ShapeDtypeStruct((H, 1), F32),
                   jax.ShapeDtypeStruct((1, SSD_D_INNER), F32)],
        scratch_shapes=[pltpu.VMEM((SSD_GROUPS, SSD_STATE, 512), F32)],
        compiler_params=_cp("arbitrary"),
    )(xc, xc, xc, dtr, dtrT, dtb, dtbT, alog, alogT, dskx, expand, expandT, states, dy)


QK_PAD = 256
MLA_TS = 256


def _rope_tables4(pos):
    inv = 1.0 / (ROPE_THETA ** (jnp.arange(0, MLA_ROPE, 2, dtype=F32) / MLA_ROPE))
    ang = pos.astype(F32)[:, None] * inv
    c, s = jnp.cos(ang), jnp.sin(ang)
    return jnp.tile(c, (1, 4)), jnp.concatenate([-s, s, -s, s], axis=1)


def _mla_gains(qg, kg):
    z = jnp.zeros((LANE - MLA_ROPE,), F32)
    return (qg[:MLA_NOPE][None], jnp.concatenate([qg[MLA_NOPE:], z])[None],
            kg[:MLA_NOPE][None], jnp.concatenate([kg[MLA_NOPE:], z])[None])


def _rope_swap(t, first):
    return jnp.where(first, pltpu.roll(t, 96, 1), pltpu.roll(t, 32, 1))


def _mla_prep_specs(ts):
    row = lambda w, c=0: pl.BlockSpec((ts, w), lambda i: (i, c))
    vec = pl.BlockSpec((1, LANE), lambda i: (0, 0))
    return [row(MLA_HEADS * MLA_QK), row(2 * MLA_HEADS * MLA_NOPE), row(LANE, OFF_KRDT // LANE), row(LANE), row(LANE),
            vec, vec, vec, vec]


def _mla_prep_fwd(qraw, kvraw, proj, cos4, sin4, gqn, gqr, gkn, gkr, *, name):
    s = qraw.shape[0]
    ts = _pick(s, MLA_TS, 8)

    def body(q_ref, kv_ref, kr_ref, cos_ref, sin_ref, gqn_ref, gqr_ref, gkn_ref, gkr_ref, qo_ref, ko_ref):
        lane = lax.broadcasted_iota(jnp.int32, (ts, LANE), 1)
        lo = lane < 64
        first = (lane % 64) < 32
        cos, sin = cos_ref[...], sin_ref[...]
        kr = jnp.where(lo, kr_ref[...], 0.0)
        ssq_kr = jnp.sum(kr * kr, axis=-1, keepdims=True)

        def head(xn, xr, ssq_r, gn, gr):
            rs = lax.rsqrt((jnp.sum(xn * xn, axis=-1, keepdims=True) + ssq_r) * (1.0 / MLA_QK) + EPS)
            yr = xr * rs * gr
            return xn * rs * gn, yr * cos + _rope_swap(yr, first) * sin

        for h in range(MLA_HEADS):
            tile = q_ref[:, MLA_HEADS * MLA_NOPE + LANE * (h // 2):MLA_HEADS * MLA_NOPE + LANE * (h // 2) + LANE]
            qr = jnp.where(lo, tile if h % 2 == 0 else pltpu.roll(tile, 64, 1), 0.0)
            on, orr = head(q_ref[:, LANE * h:LANE * h + LANE], qr, jnp.sum(qr * qr, axis=-1, keepdims=True),
                           gqn_ref[...], gqr_ref[...])
            qo_ref[h, :, 0:LANE] = on.astype(BF16)
            qo_ref[h, :, LANE:QK_PAD] = orr.astype(BF16)
            on, orr = head(kv_ref[:, LANE * h:LANE * h + LANE], kr, ssq_kr, gkn_ref[...], gkr_ref[...])
            ko_ref[h, :, 0:LANE] = on.astype(BF16)
            ko_ref[h, :, LANE:QK_PAD] = orr.astype(BF16)

    out = pl.BlockSpec((MLA_HEADS, ts, QK_PAD), lambda i: (0, i, 0))
    sh = jax.ShapeDtypeStruct((MLA_HEADS, s, QK_PAD), BF16)
    return pl.pallas_call(
        body, name=name, grid=(s // ts,),
        in_specs=_mla_prep_specs(ts), out_specs=[out, out], out_shape=[sh, sh],
        compiler_params=_cp("parallel"),
    )(qraw, kvraw, proj, cos4, sin4, gqn, gqr, gkn, gkr)


def _mla_prep_bwd(qraw, kvraw, proj, cos4, sin4, gqn, gqr, gkn, gkr, dq, dk, *, name):
    s = qraw.shape[0]
    ts = _pick(s, MLA_TS, 8)

    def body(q_ref, kv_ref, kr_ref, cos_ref, sin_ref, gqn_ref, gqr_ref, gkn_ref, gkr_ref, dq_ref, dk_ref,
             dqraw_ref, dkn_ref, dkr_ref, dgqn_ref, dgqr_ref, dgkn_ref, dgkr_ref):
        i = pl.program_id(0)

        @pl.when(i == 0)
        def _():
            for r in (dgqn_ref, dgqr_ref, dgkn_ref, dgkr_ref):
                r[...] = jnp.zeros_like(r)

        lane = lax.broadcasted_iota(jnp.int32, (ts, LANE), 1)
        lo = lane < 64
        first = (lane % 64) < 32
        cos, sin = cos_ref[...], sin_ref[...]
        kr = jnp.where(lo, kr_ref[...], 0.0)
        ssq_kr = jnp.sum(kr * kr, axis=-1, keepdims=True)

        def head(xn, xr, ssq_r, gn, gr, don, dor):
            rs = lax.rsqrt((jnp.sum(xn * xn, axis=-1, keepdims=True) + ssq_r) * (1.0 / MLA_QK) + EPS)
            xhn, xhr = xn * rs, xr * rs
            dor = jnp.where(lo, dor, 0.0)
            dyr = dor * cos + _rope_swap(dor * sin, first)
            dxn, dxr = don * gn, dyr * gr
            mm = (jnp.sum(dxn * xhn, axis=-1, keepdims=True) + jnp.sum(dxr * xhr, axis=-1, keepdims=True)) * (1.0 / MLA_QK)
            return (rs * (dxn - xhn * mm), rs * (dxr - xhr * mm),
                    jnp.sum(don * xhn, axis=0, keepdims=True), jnp.sum(dyr * xhr, axis=0, keepdims=True))

        dkr_acc = jnp.zeros((ts, LANE), F32)
        prev = None
        for h in range(MLA_HEADS):
            c0 = MLA_HEADS * MLA_NOPE + LANE * (h // 2)
            tile = q_ref[:, c0:c0 + LANE]
            qr = jnp.where(lo, tile if h % 2 == 0 else pltpu.roll(tile, 64, 1), 0.0)
            dn, dr, gn_p, gr_p = head(q_ref[:, LANE * h:LANE * h + LANE], qr, jnp.sum(qr * qr, axis=-1, keepdims=True),
                                      gqn_ref[...], gqr_ref[...], dq_ref[h, :, 0:LANE], dq_ref[h, :, LANE:QK_PAD])
            dqraw_ref[:, LANE * h:LANE * h + LANE] = dn.astype(dqraw_ref.dtype)
            dgqn_ref[...] += gn_p
            dgqr_ref[...] += gr_p
            if h % 2 == 0:
                prev = dr
            else:
                dqraw_ref[:, c0:c0 + LANE] = (prev + pltpu.roll(dr, 64, 1)).astype(dqraw_ref.dtype)
            dn, dr, gn_p, gr_p = head(kv_ref[:, LANE * h:LANE * h + LANE], kr, ssq_kr, gkn_ref[...], gkr_ref[...],
                                      dk_ref[h, :, 0:LANE], dk_ref[h, :, LANE:QK_PAD])
            dkn_ref[:, LANE * h:LANE * h + LANE] = dn
            dkr_acc = dkr_acc + dr
            dgkn_ref[...] += gn_p
            dgkr_ref[...] += gr_p
        dkr_ref[...] = dkr_acc

    row = lambda w: pl.BlockSpec((ts, w), lambda i: (i, 0))
    vec = pl.BlockSpec((1, LANE), lambda i: (0, 0))
    dspec = pl.BlockSpec((MLA_HEADS, ts, QK_PAD), lambda i: (0, i, 0))
    vsh = jax.ShapeDtypeStruct((1, LANE), F32)
    return pl.pallas_call(
        body, name=name, grid=(s // ts,),
        in_specs=_mla_prep_specs(ts) + [dspec, dspec],
        out_specs=[row(MLA_HEADS * MLA_QK), row(MLA_HEADS * MLA_NOPE), row(LANE), vec, vec, vec, vec],
        out_shape=[jax.ShapeDtypeStruct((s, MLA_HEADS * MLA_QK), BF16), jax.ShapeDtypeStruct((s, MLA_HEADS * MLA_NOPE), F32),
                   jax.ShapeDtypeStruct((s, LANE), F32), vsh, vsh, vsh, vsh],
        compiler_params=_cp("arbitrary"),
    )(qraw, kvraw, proj, cos4, sin4, gqn, gqr, gkn, gkr, dq, dk)


ATT_T = 512
ATT_SCALE = MLA_QK ** -0.5


def _attn_fwd(q, k, kvraw, *, name):
    nh, s, _ = q.shape
    t = _pick(s, ATT_T, LANE)
    nb = s // t

    def body(q_ref, k_ref, v_ref, o_ref, lse_ref, m_ref, l_ref, acc_ref):
        i, j = pl.program_id(1), pl.program_id(2)

        @pl.when(j == 0)
        def _():
            m_ref[...] = jnp.full_like(m_ref, -jnp.inf)
            l_ref[...] = jnp.zeros_like(l_ref)
            acc_ref[...] = jnp.zeros_like(acc_ref)

        def step(diagonal):
            sc = _dot_nt(q_ref[0], k_ref[0]) * ATT_SCALE
            if diagonal:
                ri = lax.broadcasted_iota(jnp.int32, (t, t), 0)
                cj = lax.broadcasted_iota(jnp.int32, (t, t), 1)
                sc = jnp.where(ri >= cj, sc, -jnp.inf)
            m_new = jnp.maximum(m_ref[...], jnp.max(sc, axis=-1, keepdims=True))
            alpha = jnp.exp(m_ref[...] - m_new)
            p = jnp.exp(sc - m_new)
            l_ref[...] = alpha * l_ref[...] + jnp.sum(p, axis=-1, keepdims=True)
            acc_ref[...] = alpha * acc_ref[...] + _dot(p.astype(BF16), v_ref[...].astype(BF16))
            m_ref[...] = m_new

        @pl.when(j < i)
        def _():
            step(False)

        @pl.when(j == i)
        def _():
            step(True)
            o_ref[...] = acc_ref[...] / l_ref[...]
            lse_ref[0] = m_ref[...] + jnp.log(l_ref[...])

    return pl.pallas_call(
        body, name=name, grid=(nh, nb, nb),
        in_specs=[pl.BlockSpec((1, t, QK_PAD), lambda h, i, j: (h, i, 0)),
                  pl.BlockSpec((1, t, QK_PAD), lambda h, i, j: (h, jnp.minimum(j, i), 0)),
                  pl.BlockSpec((t, MLA_V), lambda h, i, j: (jnp.minimum(j, i), nh + h))],
        out_specs=[pl.BlockSpec((t, MLA_V), lambda h, i, j: (i, h)),
                   pl.BlockSpec((1, t, 1), lambda h, i, j: (h, i, 0))],
        out_shape=[jax.ShapeDtypeStruct((s, nh * MLA_V), F32), jax.ShapeDtypeStruct((nh, s, 1), F32)],
        scratch_shapes=[pltpu.VMEM((t, 1), F32), pltpu.VMEM((t, 1), F32), pltpu.VMEM((t, MLA_V), F32)],
        compiler_params=_cp("parallel", "parallel", "arbitrary"),
    )(q, k, kvraw)


def _attn_bwd(q, k, kvraw, o, lse, do, *, name):
    nh, s, _ = q.shape
    t = _pick(s, ATT_T, LANE)
    nb = s // t

    def body(q_ref, k_ref, v_ref, o_ref, lse_ref, do_ref, dq_ref, dk_ref, dv_ref, dk_acc, dv_acc):
        j, i = pl.program_id(1), pl.program_id(2)

        @pl.when(i == 0)
        def _():
            dk_acc[...] = jnp.zeros_like(dk_acc)
            dv_acc[...] = jnp.zeros_like(dv_acc)

        def step(diagonal):
            qv, kv = q_ref[0], k_ref[0]
            sc = _dot_nt(qv, kv) * ATT_SCALE
            if diagonal:
                ri = lax.broadcasted_iota(jnp.int32, (t, t), 0)
                cj = lax.broadcasted_iota(jnp.int32, (t, t), 1)
                sc = jnp.where(ri >= cj, sc, -jnp.inf)
            p = jnp.exp(sc - lse_ref[0])
            dov = do_ref[...]
            delta = jnp.sum(dov * o_ref[...], axis=-1, keepdims=True)
            do_b = dov.astype(BF16)
            dv_acc[...] += _dot_tn(p.astype(BF16), do_b)
            dp = _dot_nt(do_b, v_ref[...].astype(BF16))
            ds_b = (p * (dp - delta) * ATT_SCALE).astype(BF16)
            dk_acc[...] += _dot_tn(ds_b, qv)
            dq_part = _dot(ds_b, kv)
            rows = pl.ds(pl.multiple_of(i * t, t), t)

            @pl.when(j == 0)
            def _():
                dq_ref[0, rows, :] = dq_part

            @pl.when(j > 0)
            def _():
                dq_ref[0, rows, :] += dq_part

        @pl.when(i > j)
        def _():
            step(False)

        @pl.when(i == j)
        def _():
            step(True)

        @pl.when(i == nb - 1)
        def _():
            dk_ref[0] = dk_acc[...]
            dv_ref[...] = dv_acc[...]

    qi = lambda h, j, i: jnp.maximum(i, j)
    return pl.pallas_call(
        body, name=name, grid=(nh, nb, nb),
        in_specs=[pl.BlockSpec((1, t, QK_PAD), lambda h, j, i: (h, qi(h, j, i), 0)),
                  pl.BlockSpec((1, t, QK_PAD), lambda h, j, i: (h, j, 0)),
                  pl.BlockSpec((t, MLA_V), lambda h, j, i: (j, nh + h)),
                  pl.BlockSpec((t, MLA_V), lambda h, j, i: (qi(h, j, i), h)),
                  pl.BlockSpec((1, t, 1), lambda h, j, i: (h, qi(h, j, i), 0)),
                  pl.BlockSpec((t, MLA_V), lambda h, j, i: (qi(h, j, i), h))],
        out_specs=[pl.BlockSpec((1, s, QK_PAD), lambda h, j, i: (h, 0, 0)),
                   pl.BlockSpec((1, t, QK_PAD), lambda h, j, i: (h, j, 0)),
                   pl.BlockSpec((t, MLA_V), lambda h, j, i: (j, h))],
        out_shape=[jax.ShapeDtypeStruct((nh, s, QK_PAD), F32), jax.ShapeDtypeStruct((nh, s, QK_PAD), F32),
                   jax.ShapeDtypeStruct((s, nh * MLA_V), F32)],
        scratch_shapes=[pltpu.VMEM((t, QK_PAD), F32), pltpu.VMEM((t, MLA_V), F32)],
        compiler_params=_cp("parallel", "arbitrary", "arbitrary"),
    )(q, k, kvraw, o, lse, do)


def _ffn_fwd(h, w, tag):
    n = _rms_fwd(h, w["ln"], name=tag + "_norm")
    act, gate, up = _ffn_up(n, w["w13"], name=tag + "_up")
    out = _matmul(act, w["w2"], "nn", name=tag + "_down", scale=0.5, res=h)
    return out, (h, n, gate, up, act)


def _ffn_bwd(dout, saved, w, tag):
    h, n, gate, up, act = saved
    dact = _matmul(dout, w["w2"], "nt", name=tag + "_down_dx", scale=0.5, out_dtype=BF16)
    dw2 = _matmul(act, dout, "tn", name=tag + "_down_dw", scale=0.5)
    dgu = _swiglu_bwd(gate, up, dact, name=tag + "_act_bwd")
    dw13 = _matmul(n, dgu, "tn", name=tag + "_up_dw")
    dn = _matmul(dgu, w["w13"], "nt", name=tag + "_up_dx")
    dh, dln = _rms_bwd(h, w["ln"], dn, name=tag + "_norm_bwd", res=dout)
    return dh, dict(ln=dln, w13=dw13, w2=dw2)


def _mixer_fwd(h, w, rope, tag):
    cos4, sin4 = rope
    u = _rms_fwd(h, w["ln_mix"], name=tag + "_norm")
    proj = _matmul(u, w["w_in"], "nn", name=tag + "_in")
    xc = _conv_fwd(proj, w["conv_w"], w["conv_b"], name=tag + "_conv")
    dtr = proj[:, OFF_KRDT + MLA_ROPE:OFF_KRDT + MLA_ROPE + SSD_HEADS]
    dtrT = dtr.T
    y, states = _ssd_fwd(xc, dtr, dtrT, *w["ssd_aux"], name=tag + "_ssd")
    yn = _gated_rms_fwd(y, proj, w["ssd_norm"], name=tag + "_ssd_norm")
    y_ssd = _matmul(yn, w["w_ssd_out"], "nn", name=tag + "_ssd_out")
    cqn = _rms_fwd(proj, w["q_lora_norm"], name=tag + "_q_lora_norm", col=OFF_CQ // MLA_Q_LORA, width=MLA_Q_LORA)
    qraw = _matmul(cqn, w["w_uq"], "nn", name=tag + "_uq")
    ckvn = _rms_fwd(proj, w["kv_lora_norm"], name=tag + "_kv_lora_norm", col=OFF_CKV // MLA_KV_LORA, width=MLA_KV_LORA)
    kvraw = _matmul(ckvn, w["w_ukv"], "nn", name=tag + "_ukv")
    qf, kf = _mla_prep_fwd(qraw, kvraw, proj, cos4, sin4, *w["qk_gains"], name=tag + "_qk_prep")
    o, lse = _attn_fwd(qf, kf, kvraw, name=tag + "_attn")
    y_mla = _matmul(o, w["w_mla_out"], "nn", name=tag + "_mla_out")
    merged = _merge_fwd(proj, y_ssd, y_mla, name=tag + "_merge")
    out = _matmul(merged, w["w_o"], "nn", name=tag + "_o", res=h)
    saved = dict(h=h, u=u, proj=proj, xc=xc, dtr=dtr, dtrT=dtrT, states=states, y=y, yn=yn, y_ssd=y_ssd, cqn=cqn,
                 qraw=qraw, ckvn=ckvn, kvraw=kvraw, qf=qf, kf=kf, o=o, lse=lse, y_mla=y_mla, merged=merged)
    return out, saved


def _mixer_bwd(dout, s, w, rope, tag):
    cos4, sin4 = rope
    g = {}
    proj = s["proj"]
    dmerged = _matmul(dout, w["w_o"], "nt", name=tag + "_o_dx")
    g["w_o"] = _matmul(s["merged"], dout, "tn", name=tag + "_o_dw")
    dgates, dy_ssd, dy_mla = _merge_bwd(proj, s["y_ssd"], s["y_mla"], dmerged, name=tag + "_merge_bwd")
    do = _matmul(dy_mla, w["w_mla_out"], "nt", name=tag + "_mla_out_dx")
    g["w_mla_out"] = _matmul(s["o"], dy_mla, "tn", name=tag + "_mla_out_dw")
    dqf, dkf, dv = _attn_bwd(s["qf"], s["kf"], s["kvraw"], s["o"], s["lse"], do, name=tag + "_attn_bwd")
    dqraw, dkn, dkrt, dgqn, dgqr, dgkn, dgkr = _mla_prep_bwd(
        s["qraw"], s["kvraw"], proj, cos4, sin4, *w["qk_gains"], dqf, dkf, name=tag + "_qk_prep_bwd")
    g["q_norm"] = jnp.concatenate([dgqn[0], dgqr[0, :MLA_ROPE]])
    g["k_norm"] = jnp.concatenate([dgkn[0], dgkr[0, :MLA_ROPE]])
    dkvraw = jnp.concatenate([dkn, dv], axis=1).astype(BF16)
    dcqn = _matmul(dqraw, w["w_uq"], "nt", name=tag + "_uq_dx")
    g["w_uq"] = _matmul(s["cqn"], dqraw, "tn", name=tag + "_uq_dw")
    dckvn = _matmul(dkvraw, w["w_ukv"], "nt", name=tag + "_ukv_dx")
    g["w_ukv"] = _matmul(s["ckvn"], dkvraw, "tn", name=tag + "_ukv_dw")
    dcq, g["q_lora_norm"] = _rms_bwd(proj, w["q_lora_norm"], dcqn, name=tag + "_q_lora_norm_bwd",
                                     col=OFF_CQ // MLA_Q_LORA, width=MLA_Q_LORA)
    dckv, g["kv_lora_norm"] = _rms_bwd(proj, w["kv_lora_norm"], dckvn, name=tag + "_kv_lora_norm_bwd",
                                       col=OFF_CKV // MLA_KV_LORA, width=MLA_KV_LORA)
    dyn = _matmul(dy_ssd, w["w_ssd_out"], "nt", name=tag + "_ssd_out_dx")
    g["w_ssd_out"] = _matmul(s["yn"], dy_ssd, "tn", name=tag + "_ssd_out_dw")
    dy, dz, g["ssd_norm"] = _gated_rms_bwd(s["y"], proj, w["ssd_norm"], dyn, name=tag + "_ssd_norm_bwd")
    aux = w["ssd_aux"]
    dxc, ddt_c, ddt_r, dbias_c, dbias_r, da_c, da_r, ddsk = _ssd_bwd(
        s["xc"], s["dtr"], s["dtrT"], *aux, aux[-1].T, s["states"], dy, name=tag + "_ssd_bwd")
    g["dt_bias"] = dbias_c[0] + dbias_r[:, 0]
    g["a_log"] = (da_c[0] + da_r[:, 0]) * (-jnp.exp(aux[2][0]))
    g["d_skip"] = jnp.sum(ddsk.reshape(SSD_HEADS, SSD_HEAD_DIM), axis=1)
    dpre, g["conv_w"], g["conv_b"] = _conv_bwd_pre(proj, w["conv_w"], w["conv_b"], dxc, name=tag + "_conv_bwd_pre")
    dxbc = _conv_bwd_x(dpre, w["conv_w"], name=tag + "_conv_bwd_x")
    ddtr = ddt_c + ddt_r.T
    dkrdt = jnp.concatenate([dkrt[:, :MLA_ROPE], ddtr, jnp.zeros((ddtr.shape[0], LANE - MLA_ROPE - SSD_HEADS), F32)], axis=1)
    dproj = jnp.concatenate([dz, dxbc, dgates, dcq, dckv, dkrdt], axis=1).astype(BF16)
    du = _matmul(dproj, w["w_in"], "nt", name=tag + "_in_dx")
    g["w_in"] = _matmul(s["u"], dproj, "tn", name=tag + "_in_dw")
    dh, g["ln_mix"] = _rms_bwd(s["h"], w["ln_mix"], du, name=tag + "_norm_bwd", res=dout)
    return dh, g


W_NAMES = ["ln_ffn1", "ffn1_w13", "ffn1_w2", "ln_mix", "w_in", "conv_w", "conv_b", "dt_bias", "a_log", "d_skip",
           "ssd_norm", "w_ssd_out", "q_lora_norm", "w_uq", "kv_lora_norm", "w_ukv", "q_norm", "k_norm", "w_mla_out",
           "w_o", "ln_ffn2", "ffn2_w13", "ffn2_w2"]
SHARD_AXIS = {"ffn1_w13": 2, "ffn1_w2": 1, "w_in": 2, "conv_w": 2, "w_ssd_out": 1, "w_uq": 2, "w_ukv": 2,
              "w_mla_out": 1, "w_o": 1, "ffn2_w13": 2, "ffn2_w2": 1}
SHARDED = [n for n in W_NAMES if n in SHARD_AXIS]
REPLICATED = [n for n in W_NAMES if n not in SHARD_AXIS]
N_CHIPS = 4
N_DEV = 8
PACK_COLS = 1024
IN_SPLIT = (2048, 3072, 32, 512, 256, 64, 2048)


def _pack(arrs, rows, dtype):
    flat = jnp.concatenate([a.astype(dtype).reshape(-1) for a in arrs])
    return jnp.pad(flat, (0, rows * PACK_COLS - flat.shape[0])).reshape(rows, PACK_COLS)


def _unpack(packed, shapes):
    flat = packed.reshape(-1)
    out, at = [], 0
    for sh in shapes:
        n = math.prod(sh)
        out.append(flat[at:at + n].reshape(sh))
        at += n
    return out


def _pack_rows(shapes):
    n = sum(math.prod(sh) for sh in shapes)
    return -(-n // (PACK_COLS * 1024)) * 1024


def _in_perm(w_in):
    z, xbc, dt, cq, ckv, kr, gates = jnp.split(w_in, list(np_cumsum(IN_SPLIT))[:-1], axis=1)
    return jnp.concatenate([z, xbc, gates, cq, ckv, kr, dt, jnp.zeros((w_in.shape[0], PROJ_W - sum(IN_SPLIT)), w_in.dtype)], axis=1)


def _in_unperm(g):
    z, xbc, gates, cq, ckv = (g[:, OFF_Z:OFF_XBC], g[:, OFF_XBC:OFF_GATES], g[:, OFF_GATES:OFF_CQ], g[:, OFF_CQ:OFF_CKV],
                              g[:, OFF_CKV:OFF_KRDT])
    kr = g[:, OFF_KRDT:OFF_KRDT + MLA_ROPE]
    dt = g[:, OFF_KRDT + MLA_ROPE:OFF_KRDT + MLA_ROPE + SSD_HEADS]
    return jnp.concatenate([z, xbc, dt, cq, ckv, kr, gates], axis=1)


def np_cumsum(sizes):
    out, t = [], 0
    for s in sizes:
        t += s
        out.append(t)
    return out


def _head_perm(w, first):
    r = w.shape[0]
    w3 = w.reshape(r, MLA_HEADS, -1)
    return jnp.concatenate([w3[:, :, :first].reshape(r, -1), w3[:, :, first:].reshape(r, -1)], axis=1)


def _head_unperm(g, first):
    r = g.shape[0]
    rest = g.shape[1] // MLA_HEADS - first
    a = g[:, :MLA_HEADS * first].reshape(r, MLA_HEADS, first)
    b = g[:, MLA_HEADS * first:].reshape(r, MLA_HEADS, rest)
    return jnp.concatenate([a, b], axis=2).reshape(r, -1)


def _layer_weights(full, l):
    row = lambda n: full[n][l][None].astype(F32)
    expand = jnp.repeat(jnp.eye(SSD_HEADS, dtype=F32), SSD_HEAD_DIM, axis=1)
    dtb, al, dsk = full["dt_bias"][l], full["a_log"][l], full["d_skip"][l]
    mixer = dict(
        ln_mix=row("ln_mix"), w_in=_in_perm(full["w_in"][l]), conv_w=full["conv_w"][l], conv_b=row("conv_b"),
        ssd_aux=(dtb[None], dtb[:, None], al[None], al[:, None], jnp.repeat(dsk, SSD_HEAD_DIM)[None], expand),
        ssd_norm=row("ssd_norm"), w_ssd_out=full["w_ssd_out"][l],
        q_lora_norm=row("q_lora_norm"), w_uq=_head_perm(full["w_uq"][l], MLA_NOPE),
        kv_lora_norm=row("kv_lora_norm"), w_ukv=_head_perm(full["w_ukv"][l], MLA_NOPE),
        qk_gains=_mla_gains(full["q_norm"][l], full["k_norm"][l]),
        w_mla_out=full["w_mla_out"][l], w_o=full["w_o"][l])
    ffn1 = dict(ln=row("ln_ffn1"), w13=full["ffn1_w13"][l], w2=full["ffn1_w2"][l])
    ffn2 = dict(ln=row("ln_ffn2"), w13=full["ffn2_w13"][l], w2=full["ffn2_w2"][l])
    return ffn1, mixer, ffn2


def _layer_grads(g1, gm, g2):
    return {
        "ln_ffn1": g1["ln"][0], "ffn1_w13": g1["w13"], "ffn1_w2": g1["w2"],
        "ln_mix": gm["ln_mix"][0], "w_in": _in_unperm(gm["w_in"]), "conv_w": gm["conv_w"], "conv_b": gm["conv_b"][0],
        "dt_bias": gm["dt_bias"], "a_log": gm["a_log"], "d_skip": gm["d_skip"], "ssd_norm": gm["ssd_norm"][0],
        "w_ssd_out": gm["w_ssd_out"], "q_lora_norm": gm["q_lora_norm"][0], "w_uq": _head_unperm(gm["w_uq"], MLA_NOPE),
        "kv_lora_norm": gm["kv_lora_norm"][0], "w_ukv": _head_unperm(gm["w_ukv"], MLA_NOPE),
        "q_norm": gm["q_norm"], "k_norm": gm["k_norm"], "w_mla_out": gm["w_mla_out"], "w_o": gm["w_o"],
        "ln_ffn2": g2["ln"][0], "ffn2_w13": g2["w13"], "ffn2_w2": g2["w2"],
    }


def _local_step(x, positions, loss_target, full):
    rope = _rope_tables4(positions)
    lw = [_layer_weights(full, l) for l in range(DEPTH)]
    h = x
    saved = []
    for l in range(DEPTH):
        f1, mx, f2 = lw[l]
        h, s1 = _ffn_fwd(h, f1, f"l{l}_ffn1")
        h, sm = _mixer_fwd(h, mx, rope, f"l{l}_mix")
        h, s2 = _ffn_fwd(h, f2, f"l{l}_ffn2")
        saved.append((s1, sm, s2))
    loss_part, dh = _loss_fwd_bwd(h, loss_target, name="loss")
    grads = [None] * DEPTH
    for l in reversed(range(DEPTH)):
        f1, mx, f2 = lw[l]
        s1, sm, s2 = saved[l]
        dh, g2 = _ffn_bwd(dh, s2, f2, f"l{l}_ffn2")
        dh, gm = _mixer_bwd(dh, sm, mx, rope, f"l{l}_mix")
        dh, g1 = _ffn_bwd(dh, s1, f1, f"l{l}_ffn1")
        grads[l] = _layer_grads(g1, gm, g2)
    full_grads = {n: jnp.stack([grads[l][n] for l in range(DEPTH)]) for n in W_NAMES}
    return loss_part, dh, full_grads


MESH = pl.DeviceIdType.MESH
ANY = pl.BlockSpec(memory_space=pl.ANY)


def _place():
    return lax.axis_index("x"), lax.axis_index("y"), lax.axis_index("c")


def _other_chips(x, y):
    return [(1 - x, y), (x, 1 - y), (1 - x, 1 - y)]


def _remote(src, dst, send_sems, recv_sems, k, to):
    return pltpu.make_async_remote_copy(src_ref=src, dst_ref=dst, send_sem=send_sems.at[k], recv_sem=recv_sems.at[k],
                                        device_id=to, device_id_type=MESH)


N_PARTS = 8


def _parts(rows):
    size = rows // N_PARTS
    assert size * N_PARTS == rows and size % 16 == 0, rows
    return [(p * size, size) for p in range(N_PARTS)]


def _rows(ref, lead, base, start, size):
    return ref.at[(*lead, pl.ds(pl.multiple_of(base + start, 16), size), slice(None))]


def _my_chip():
    return 2 * lax.axis_index("x") + lax.axis_index("y")


def _own_slot(packed, *, name):
    r, ncol = packed.shape
    tr = _pick(r, 512, 16)

    def body(x_ref, o_ref):
        o_ref[...] = x_ref[...]

    return pl.pallas_call(
        body, name=name, grid=(r // tr,),
        in_specs=[pl.BlockSpec((tr, ncol), lambda i: (i, 0))],
        out_specs=pl.BlockSpec((None, tr, ncol), lambda i: (_my_chip(), i, 0)),
        out_shape=jax.ShapeDtypeStruct((N_CHIPS, r, ncol), packed.dtype),
        compiler_params=_cp("arbitrary"),
    )(packed)


def _gather_shards(packed, slots, *, name):
    r, ncol = packed.shape
    hr = r // 2
    parts = _parts(hr)

    def body(x_ref, slots_ref, out_ref, send_sems, recv_sems):
        del slots_ref
        x, y, c = _place()
        chips = _other_chips(x, y)
        me = 2 * x + y

        def half(chip, cc):
            return _rows(out_ref, (2 * chip[0] + chip[1],), cc * hr, 0, hr)

        for j, chip in enumerate(chips):
            for st, sz in parts:
                _remote(_rows(x_ref, (), c * hr, st, sz), _rows(out_ref, (me,), c * hr, st, sz), send_sems, recv_sems, j,
                        (*chip, c)).start()
        for j, chip in enumerate(chips):
            _remote(half(chip, c), half(chip, c), send_sems, recv_sems, j, (x, y, c)).wait_recv()
            slot = 2 * chip[0] + chip[1]
            for st, sz in parts:
                _remote(_rows(out_ref, (slot,), c * hr, st, sz), _rows(out_ref, (slot,), c * hr, st, sz), send_sems,
                        recv_sems, 3 + j, (x, y, 1 - c)).start()
        for j, chip in enumerate(chips):
            _remote(half(chip, 1 - c), half(chip, 1 - c), send_sems, recv_sems, 3 + j, (x, y, c)).wait_recv()
        for k in range(6):
            _remote(half((x, y), c), half((x, y), c), send_sems, recv_sems, k, (x, y, c)).wait_send()

    return pl.pallas_call(
        body, name=name,
        out_shape=jax.ShapeDtypeStruct((N_CHIPS, r, ncol), packed.dtype),
        in_specs=[ANY, ANY], out_specs=ANY, input_output_aliases={1: 0},
        scratch_shapes=[pltpu.SemaphoreType.DMA((6,)), pltpu.SemaphoreType.DMA((6,))],
    )(packed, slots)


def _swap_halves(g, *, name):
    n, r, ncol = g.shape
    hr = r // 2
    parts = _parts(hr)

    def body(g_ref, got_ref, send_sems, recv_sems):
        x, y, c = _place()
        for s in range(n):
            for st, sz in parts:
                _remote(_rows(g_ref, (s,), (1 - c) * hr, st, sz), got_ref.at[s, pl.ds(st, sz), :], send_sems, recv_sems, 0,
                        (x, y, 1 - c)).start()
        _remote(got_ref, got_ref, send_sems, recv_sems, 0, (x, y, c)).wait()

    return pl.pallas_call(
        body, name=name, out_shape=jax.ShapeDtypeStruct((n, hr, ncol), g.dtype), in_specs=[ANY], out_specs=ANY,
        scratch_shapes=[pltpu.SemaphoreType.DMA((1,)), pltpu.SemaphoreType.DMA((1,))],
    )(g)


def _add_cores(g, got, *, name):
    n, r, ncol = g.shape
    hr = r // 2
    tr = _pick(hr, 512, 16)
    nb = hr // tr

    def body(a_ref, b_ref, o_ref):
        o_ref[...] = (a_ref[...] + b_ref[...]).astype(o_ref.dtype)

    blk = pl.BlockSpec((None, tr, ncol), lambda s, i: (s, i, 0))
    return pl.pallas_call(
        body, name=name, grid=(n, nb),
        in_specs=[pl.BlockSpec((None, tr, ncol), lambda s, i: (s, lax.axis_index("c") * nb + i, 0)), blk],
        out_specs=blk,
        out_shape=jax.ShapeDtypeStruct((n, hr, ncol), BF16),
        compiler_params=_cp("parallel", "parallel"),
    )(g, got)


def _scatter_to_chips(a, *, name):
    n, r, ncol = a.shape
    parts = _parts(r)

    def body(a_ref, got_ref, send_sems, recv_sems):
        x, y, c = _place()
        for st, sz in parts:
            for j, chip in enumerate(_other_chips(x, y)):
                _remote(a_ref.at[2 * chip[0] + chip[1], pl.ds(st, sz), :], got_ref.at[j, pl.ds(st, sz), :], send_sems,
                        recv_sems, j, (*chip, c)).start()
        for j in range(n - 1):
            _remote(got_ref.at[j], got_ref.at[j], send_sems, recv_sems, j, (x, y, c)).wait()

    return pl.pallas_call(
        body, name=name, out_shape=jax.ShapeDtypeStruct((n - 1, r, ncol), a.dtype), in_specs=[ANY], out_specs=ANY,
        scratch_shapes=[pltpu.SemaphoreType.DMA((3,)), pltpu.SemaphoreType.DMA((3,))],
    )(a)


def _add_chips(a, got, *, name):
    n, hr, ncol = a.shape
    tr = _pick(hr, 512, 16)
    nb = hr // tr

    def body(a_ref, g0_ref, g1_ref, g2_ref, o_ref):
        f = lambda ref: ref[...].astype(F32)
        o_ref[...] = ((f(a_ref) + f(g0_ref)) + f(g1_ref)) + f(g2_ref)

    other = lambda j: pl.BlockSpec((None, tr, ncol), lambda i: (j, i, 0))
    return pl.pallas_call(
        body, name=name, grid=(nb,),
        in_specs=[pl.BlockSpec((None, tr, ncol), lambda i: (_my_chip(), i, 0)), other(0), other(1), other(2)],
        out_specs=pl.BlockSpec((tr, ncol), lambda i: (lax.axis_index("c") * nb + i, 0)),
        out_shape=jax.ShapeDtypeStruct((2 * hr, ncol), F32),
        compiler_params=_cp("parallel"),
    )(a, got, got, got)


def _join_halves(buf, *, name):
    r, ncol = buf.shape
    hr = r // 2
    parts = _parts(hr)

    def body(b_ref, out_ref, send_sems, recv_sems):
        del b_ref
        x, y, c = _place()
        for st, sz in parts:
            _remote(_rows(out_ref, (), c * hr, st, sz), _rows(out_ref, (), c * hr, st, sz), send_sems, recv_sems, 0,
                    (x, y, 1 - c)).start()
        theirs = _rows(out_ref, (), (1 - c) * hr, 0, hr)
        _remote(theirs, theirs, send_sems, recv_sems, 0, (x, y, c)).wait()

    return pl.pallas_call(
        body, name=name, out_shape=jax.ShapeDtypeStruct((r, ncol), buf.dtype), in_specs=[ANY], out_specs=ANY,
        input_output_aliases={0: 0},
        scratch_shapes=[pltpu.SemaphoreType.DMA((1,)), pltpu.SemaphoreType.DMA((1,))],
    )(buf)


def _reduce_scatter(g, *, name):
    got = _swap_halves(g, name=name + "_swap")
    chip_sum = _add_cores(g, got, name=name + "_add_cores")
    others = _scatter_to_chips(chip_sum, name=name + "_scatter")
    return _join_halves(_add_chips(chip_sum, others, name=name + "_add_chips"), name=name + "_join")


def _all_gather_small(v, *, name):
    r, ncol = v.shape

    def body(x_ref, out_ref, send_sems, recv_sems, local_sem):
        x, y, c = _place()
        me, sibling = (x, y, c), (x, y, 1 - c)
        chips = _other_chips(x, y)

        def slot(p):
            return out_ref.at[4 * p[0] + 2 * p[1] + p[2]]

        mine = pltpu.make_async_copy(x_ref, slot(me), local_sem.at[0])
        mine.start()
        first = [_remote(x_ref, slot(me), send_sems, recv_sems, 0, sibling)]
        first += [_remote(x_ref, slot(me), send_sems, recv_sems, 1 + j, (*chip, c)) for j, chip in enumerate(chips)]
        for cp in first:
            cp.start()
        passed = [_remote(slot((*chip, c)), slot((*chip, c)), send_sems, recv_sems, 4 + j, sibling)
                  for j, chip in enumerate(chips)]
        for j, chip in enumerate(chips):
            _remote(slot((*chip, c)), slot((*chip, c)), send_sems, recv_sems, 1 + j, me).wait_recv()
            passed[j].start()
        _remote(slot(sibling), slot(sibling), send_sems, recv_sems, 0, me).wait_recv()
        for j, chip in enumerate(chips):
            _remote(slot((*chip, 1 - c)), slot((*chip, 1 - c)), send_sems, recv_sems, 4 + j, me).wait_recv()
        for cp in first + passed:
            cp.wait_send()
        mine.wait()

    vm = pl.BlockSpec(memory_space=pltpu.VMEM)
    return pl.pallas_call(
        body, name=name, out_shape=jax.ShapeDtypeStruct((N_DEV, r, ncol), v.dtype), in_specs=[vm], out_specs=vm,
        scratch_shapes=[pltpu.SemaphoreType.DMA((7,)), pltpu.SemaphoreType.DMA((7,)), pltpu.SemaphoreType.DMA((1,))],
    )(v)


def _sum_slots(g8, *, name):
    n, r, ncol = g8.shape

    def body(g_ref, o_ref):
        acc = g_ref[0]
        for k in range(1, n):
            acc = acc + g_ref[k]
        o_ref[...] = acc

    return pl.pallas_call(body, name=name, out_shape=jax.ShapeDtypeStruct((r, ncol), g8.dtype))(g8)


def _step(a):
    x = a["x"][0]
    s = x.shape[0]
    del s
    shard_shapes = [a[n].shape for n in SHARDED]
    rows = _pack_rows(shard_shapes)

    packed = _pack([a[n] for n in SHARDED], rows, BF16)
    gathered = _gather_shards(packed, _own_slot(packed, name="own_weights"), name="gather_weights")
    conv_rows = -(-math.prod(a["conv_w"].shape) // (LANE * 8)) * 8
    conv_all = _all_gather_small(
        jnp.pad(a["conv_w"].reshape(-1), (0, conv_rows * LANE - math.prod(a["conv_w"].shape))).reshape(conv_rows, LANE),
        name="gather_conv_w")
    per_chip = [dict(zip(SHARDED, _unpack(gathered[k], shard_shapes))) for k in range(N_CHIPS)]
    full = {n: jnp.concatenate([per_chip[k][n] for k in range(N_CHIPS)], axis=SHARD_AXIS[n]) for n in SHARDED}
    full["conv_w"] = jnp.concatenate(
        [conv_all[2 * k].reshape(-1)[:math.prod(a["conv_w"].shape)].reshape(a["conv_w"].shape) for k in range(N_CHIPS)],
        axis=SHARD_AXIS["conv_w"])
    for n in REPLICATED:
        full[n] = a[n]

    loss_part, grad_x, grads = _local_step(x, a["positions"][0], a["loss_target"][0], full)
    loss = lax.psum(jnp.sum(loss_part), ("x", "y", "c"))

    slots = []
    for k in range(N_CHIPS):
        parts = [jnp.split(grads[n], N_CHIPS, axis=SHARD_AXIS[n])[k] for n in SHARDED]
        slots.append(_pack(parts, rows, F32))
    g_shard = _reduce_scatter(jnp.stack(slots), name="reduce_grads")

    rep_shapes = [a[n].shape for n in REPLICATED]
    n_rep = sum(math.prod(sh) for sh in rep_shapes)
    rep_rows = -(-n_rep // (LANE * 8)) * 8
    pack_small = lambda arrs: jnp.pad(jnp.concatenate([t.reshape(-1) for t in arrs]), (0, rep_rows * LANE - n_rep)).reshape(rep_rows, LANE)
    g_rep = _sum_slots(_all_gather_small(pack_small([grads[n] for n in REPLICATED]), name="gather_small_grads"),
                       name="add_small_grads")

    out = {"loss": loss, "grad_x": grad_x[None]}
    for n, g in zip(SHARDED, _unpack(g_shard, shard_shapes)):
        flat = lambda t: t.reshape(-1, t.shape[-1])
        d, nm, nv = _adamw(flat(a[n]), flat(g), flat(a["m_" + n]), flat(a["v_" + n]), name="adamw_" + n)
        out["grad_" + n] = g
        out["delta_" + n], out["new_m_" + n], out["new_v_" + n] = (t.reshape(g.shape) for t in (d, nm, nv))
    d_rp, m_rp, v_rp = _adamw(pack_small([a[n] for n in REPLICATED]), g_rep,
                              pack_small([a["m_" + n] for n in REPLICATED]),
                              pack_small([a["v_" + n] for n in REPLICATED]), name="adamw_replicated")
    for prefix, rp_arr in (("grad_", g_rep), ("delta_", d_rp), ("new_m_", m_rp), ("new_v_", v_rp)):
        for n, t in zip(REPLICATED, _unpack(rp_arr.reshape(-1)[:n_rep], rep_shapes)):
            out[prefix + n] = t
    return out


IN_NAMES = ["x", "positions"] + W_NAMES + ["loss_target"] + ["m_" + n for n in W_NAMES] + ["v_" + n for n in W_NAMES]
OUT_NAMES = (["loss", "grad_x"] + ["grad_" + n for n in W_NAMES] + ["delta_" + n for n in W_NAMES]
             + ["new_m_" + n for n in W_NAMES] + ["new_v_" + n for n in W_NAMES])


def kernel(x, positions, ln_ffn1, ffn1_w13, ffn1_w2, ln_mix, w_in, conv_w, conv_b, dt_bias, a_log, d_skip, ssd_norm, w_ssd_out, q_lora_norm, w_uq, kv_lora_norm, w_ukv, q_norm, k_norm, w_mla_out, w_o, ln_ffn2, ffn2_w13, ffn2_w2, loss_target, m_ln_ffn1, m_ffn1_w13, m_ffn1_w2, m_ln_mix, m_w_in, m_conv_w, m_conv_b, m_dt_bias, m_a_log, m_d_skip, m_ssd_norm, m_w_ssd_out, m_q_lora_norm, m_w_uq, m_kv_lora_norm, m_w_ukv, m_q_norm, m_k_norm, m_w_mla_out, m_w_o, m_ln_ffn2, m_ffn2_w13, m_ffn2_w2, v_ln_ffn1, v_ffn1_w13, v_ffn1_w2, v_ln_mix, v_w_in, v_conv_w, v_conv_b, v_dt_bias, v_a_log, v_d_skip, v_ssd_norm, v_w_ssd_out, v_q_lora_norm, v_w_uq, v_kv_lora_norm, v_w_ukv, v_q_norm, v_k_norm, v_w_mla_out, v_w_o, v_ln_ffn2, v_ffn2_w13, v_ffn2_w2):
    given = locals()
    out = _step({n: given[n] for n in IN_NAMES})
    return tuple(out[n] for n in OUT_NAMES)
```

```python
import functools
import math

import jax
import jax.numpy as jnp
from jax import lax
from jax.experimental import pallas as pl
from jax.experimental.pallas import tpu as pltpu

F32 = jnp.float32
BF16 = jnp.bfloat16

D_MODEL = 1024
DEPTH = 2
D_FF = 2816
SSD_D_INNER = 2048
SSD_HEADS = 32
SSD_HEAD_DIM = 64
SSD_GROUPS = 4
SSD_STATE = 128
SSD_CHUNK = 128
SSD_CONV = 4
SSD_CONV_DIM = 3072
MLA_HEADS = 8
MLA_Q_LORA = 512
MLA_KV_LORA = 256
MLA_NOPE = 128
MLA_ROPE = 64
MLA_V = 128
MLA_QK = 192
ROPE_THETA = 10000.0
EPS = 1e-6
ADAM_LR = 0.001
ADAM_B1 = 0.9
ADAM_B2 = 0.999
ADAM_EPS = 1e-08
ADAM_WD = 0.01
ADAM_STEP = 10

PROJ_W = 8064
OFF_Z, OFF_XBC, OFF_GATES, OFF_CQ, OFF_CKV, OFF_KRDT = 0, 2048, 5120, 7168, 7680, 7936

LANE = 128
VMEM_LIMIT = 48 * 1024 * 1024
HI = lax.Precision.HIGHEST


def _cp(*sem):
    return pltpu.CompilerParams(dimension_semantics=sem, vmem_limit_bytes=VMEM_LIMIT)


def _pick(dim, target, align):
    if dim <= target:
        return dim
    b = (target // align) * align
    while b >= align:
        if dim % b == 0:
            return b
        b -= align
    raise ValueError(f"no block for {dim} (target {target}, align {align})")


def _silu(x):
    return x * jax.nn.sigmoid(x)


def _dsilu(x):
    s = jax.nn.sigmoid(x)
    return s * (1.0 + x * (1.0 - s))


MM_VMEM_BUDGET = 40 * 1024 * 1024


def _mm_tiles(m, n, k, a_bytes, b_bytes, o_bytes):
    bn = _pick(n, 1408, LANE)
    for nk in (1, 2, 3, 4, 6, 7, 8):
        if k % nk or (k // nk) % LANE:
            continue
        bk = k // nk
        for bm in (1024, 512):
            if m % bm:
                continue
            need = 2 * (bm * bk * a_bytes + bk * bn * b_bytes + bm * bn * o_bytes) + (bm * bn * 4 if nk > 1 else 0)
            if need <= MM_VMEM_BUDGET:
                return bm, bn, bk
    return _pick(m, 512, 8), bn, _pick(k, 1536, LANE)

def _matmul(a, b, mode, *, name, out_dtype=F32, scale=1.0, res=None):
    if mode == "nn":
        (m, k), (k2, n) = a.shape, b.shape
    elif mode == "nt":
        (m, k), (n, k2) = a.shape, b.shape
    else:
        (k, m), (k2, n) = a.shape, b.shape
    assert k == k2, (a.shape, b.shape, mode)
    if mode == "tn":
        bn, bk = _pick(n, 2816, LANE), _pick(k, 1024, 8)
        bm = _pick(m, max(256, (1408 * 1024 // bn) // LANE * LANE), LANE)
    else:
        bm, bn, bk = _mm_tiles(m, n, k, a.dtype.itemsize, b.dtype.itemsize,
                               jnp.dtype(out_dtype).itemsize + (4 if res is not None else 0))
    nk = k // bk

    def body(a_ref, b_ref, *rest):
        res_ref = rest[0] if res is not None else None
        o_ref = rest[-2] if nk > 1 else rest[-1]
        kk = pl.program_id(2)
        av = a_ref[...].astype(BF16)
        bv = b_ref[...].astype(BF16)
        if mode == "nn":
            dims = (((1,), (0,)), ((), ()))
        elif mode == "nt":
            dims = (((1,), (1,)), ((), ()))
        else:
            dims = (((0,), (0,)), ((), ()))
        part = lax.dot_general(av, bv, dims, preferred_element_type=F32)

        def finish(total):
            out = total * scale
            if res_ref is not None:
                out = res_ref[...] + out
            o_ref[...] = out.astype(o_ref.dtype)

        if nk == 1:
            finish(part)
            return
        acc_ref = rest[-1]

        @pl.when(kk == 0)
        def _():
            acc_ref[...] = part

        @pl.when((kk > 0) & (kk < nk - 1))
        def _():
            acc_ref[...] += part

        @pl.when(kk == nk - 1)
        def _():
            finish(acc_ref[...] + part)

    o_spec = pl.BlockSpec((bm, bn), lambda i, j, kk: (i, j))
    if mode == "nn":
        a_spec = pl.BlockSpec((bm, bk), lambda i, j, kk: (i, kk))
        b_spec = pl.BlockSpec((bk, bn), lambda i, j, kk: (kk, j))
    elif mode == "nt":
        a_spec = pl.BlockSpec((bm, bk), lambda i, j, kk: (i, kk))
        b_spec = pl.BlockSpec((bn, bk), lambda i, j, kk: (j, kk))
    else:
        a_spec = pl.BlockSpec((bk, bm), lambda i, j, kk: (kk, i))
        b_spec = pl.BlockSpec((bk, bn), lambda i, j, kk: (kk, j))
    return pl.pallas_call(
        body, name=name,
        grid=(m // bm, n // bn, nk),
        in_specs=[a_spec, b_spec] + ([o_spec] if res is not None else []),
        out_specs=o_spec,
        out_shape=jax.ShapeDtypeStruct((m, n), out_dtype),
        scratch_shapes=[pltpu.VMEM((bm, bn), F32)] if nk > 1 else [],
        compiler_params=_cp("parallel", "parallel", "arbitrary"),
    )(*((a, b) + ((res,) if res is not None else ())))


def _rms_fwd(x, g, *, name, col=0, width=None):
    r = x.shape[0]
    w = width or x.shape[1]
    tr = _pick(r, 512, 16)

    def body(x_ref, g_ref, o_ref):
        xv = x_ref[...]
        rs = lax.rsqrt(jnp.mean(xv * xv, axis=-1, keepdims=True) + EPS)
        o_ref[...] = (xv * rs * g_ref[...]).astype(o_ref.dtype)

    return pl.pallas_call(
        body, name=name, grid=(r // tr,),
        in_specs=[pl.BlockSpec((tr, w), lambda i: (i, col)), pl.BlockSpec((1, w), lambda i: (0, 0))],
        out_specs=pl.BlockSpec((tr, w), lambda i: (i, 0)),
        out_shape=jax.ShapeDtypeStruct((r, w), BF16),
        compiler_params=_cp("parallel"),
    )(x, g)


def _rms_bwd(x, g, dy, *, name, col=0, width=None, res=None):
    r = x.shape[0]
    w = width or x.shape[1]
    tr = _pick(r, 512, 8)

    def body(x_ref, g_ref, dy_ref, *rest):
        res_ref = rest[0] if res is not None else None
        dx_ref, dg_ref = rest[-2:]
        i = pl.program_id(0)
        xv = x_ref[...]
        dyv = dy_ref[...]
        rs = lax.rsqrt(jnp.mean(xv * xv, axis=-1, keepdims=True) + EPS)
        xh = xv * rs
        dxh = dyv * g_ref[...]
        mm = jnp.mean(dxh * xh, axis=-1, keepdims=True)
        dx = rs * (dxh - xh * mm)
        if res_ref is not None:
            dx = res_ref[...] + dx
        dx_ref[...] = dx
        part = jnp.sum(dyv * xh, axis=0, keepdims=True)

        @pl.when(i == 0)
        def _():
            dg_ref[...] = part

        @pl.when(i > 0)
        def _():
            dg_ref[...] += part

    blk = pl.BlockSpec((tr, w), lambda i: (i, 0))
    return pl.pallas_call(
        body, name=name, grid=(r // tr,),
        in_specs=[pl.BlockSpec((tr, w), lambda i: (i, col)), pl.BlockSpec((1, w), lambda i: (0, 0)), blk]
        + ([blk] if res is not None else []),
        out_specs=[blk, pl.BlockSpec((1, w), lambda i: (0, 0))],
        out_shape=[jax.ShapeDtypeStruct((r, w), F32), jax.ShapeDtypeStruct((1, w), F32)],
        compiler_params=_cp("arbitrary"),
    )(*((x, g, dy) + ((res,) if res is not None else ())))


def _gated_rms_fwd(y, proj, g, *, name):
    r, w = y.shape
    tr = _pick(r, 256, 8)

    def body(y_ref, z_ref, g_ref, o_ref):
        t = y_ref[...] * _silu(z_ref[...])
        rs = lax.rsqrt(jnp.mean(t * t, axis=-1, keepdims=True) + EPS)
        o_ref[...] = (t * rs * g_ref[...]).astype(o_ref.dtype)

    return pl.pallas_call(
        body, name=name, grid=(r // tr,),
        in_specs=[pl.BlockSpec((tr, w), lambda i: (i, 0)), pl.BlockSpec((tr, w), lambda i: (i, OFF_Z // w)),
                  pl.BlockSpec((1, w), lambda i: (0, 0))],
        out_specs=pl.BlockSpec((tr, w), lambda i: (i, 0)),
        out_shape=jax.ShapeDtypeStruct((r, w), BF16),
        compiler_params=_cp("parallel"),
    )(y, proj, g)


def _gated_rms_bwd(y, proj, g, do, *, name):
    r, w = y.shape
    tr = _pick(r, 256, 8)

    def body(y_ref, z_ref, g_ref, do_ref, dy_ref, dz_ref, dg_ref):
        i = pl.program_id(0)
        yv, zv, dov = y_ref[...], z_ref[...], do_ref[...]
        sz = _silu(zv)
        t = yv * sz
        rs = lax.rsqrt(jnp.mean(t * t, axis=-1, keepdims=True) + EPS)
        th = t * rs
        dth = dov * g_ref[...]
        mm = jnp.mean(dth * th, axis=-1, keepdims=True)
        dt = rs * (dth - th * mm)
        dy_ref[...] = dt * sz
        dz_ref[...] = dt * yv * _dsilu(zv)
        part = jnp.sum(dov * th, axis=0, keepdims=True)

        @pl.when(i == 0)
        def _():
            dg_ref[...] = part

        @pl.when(i > 0)
        def _():
            dg_ref[...] += part

    blk = pl.BlockSpec((tr, w), lambda i: (i, 0))
    vec = pl.BlockSpec((1, w), lambda i: (0, 0))
    return pl.pallas_call(
        body, name=name, grid=(r // tr,),
        in_specs=[blk, pl.BlockSpec((tr, w), lambda i: (i, OFF_Z // w)), vec, blk],
        out_specs=[blk, blk, vec],
        out_shape=[jax.ShapeDtypeStruct((r, w), F32), jax.ShapeDtypeStruct((r, w), F32),
                   jax.ShapeDtypeStruct((1, w), F32)],
        compiler_params=_cp("arbitrary"),
    )(y, proj, g, do)


def _ffn_up(n, w13, *, name):
    m, k = n.shape
    f = w13.shape[1] // 2
    bm, bn = _pick(m, 1024, 16), _pick(f, 1408, LANE)
    nj = f // bn

    def body(a_ref, wg_ref, wu_ref, act_ref, g_ref, u_ref):
        a = a_ref[...].astype(BF16)
        g = _dot(a, wg_ref[...].astype(BF16))
        u = _dot(a, wu_ref[...].astype(BF16))
        act_ref[...] = (_silu(g) * u).astype(act_ref.dtype)
        g_ref[...] = g.astype(g_ref.dtype)
        u_ref[...] = u.astype(u_ref.dtype)

    out = pl.BlockSpec((bm, bn), lambda j, i: (i, j))
    sh = jax.ShapeDtypeStruct((m, f), BF16)
    return pl.pallas_call(
        body, name=name, grid=(nj, m // bm),
        in_specs=[pl.BlockSpec((bm, k), lambda j, i: (i, 0)), pl.BlockSpec((k, bn), lambda j, i: (0, j)),
                  pl.BlockSpec((k, bn), lambda j, i: (0, nj + j))],
        out_specs=[out, out, out], out_shape=[sh, sh, sh],
        compiler_params=_cp("parallel", "parallel"),
    )(n, w13, w13)


def _swiglu_bwd(g, u, da, *, name):
    r, f = g.shape
    tr = _pick(r, 256, 16)

    def body(g_ref, u_ref, da_ref, o_ref):
        gv, uv, dav = g_ref[...].astype(F32), u_ref[...].astype(F32), da_ref[...].astype(F32)
        o_ref[:, :f] = (dav * uv * _dsilu(gv)).astype(o_ref.dtype)
        o_ref[:, f:] = (dav * _silu(gv)).astype(o_ref.dtype)

    blk = pl.BlockSpec((tr, f), lambda i: (i, 0))
    return pl.pallas_call(
        body, name=name, grid=(r // tr,),
        in_specs=[blk, blk, blk],
        out_specs=pl.BlockSpec((tr, 2 * f), lambda i: (i, 0)),
        out_shape=jax.ShapeDtypeStruct((r, 2 * f), BF16),
        compiler_params=_cp("parallel"),
    )(g, u, da)


CONV_TS = 1024
CONV_TC = 512


def _conv_pre(x, carry, w_ref, b_ref):
    ts = x.shape[0]
    row8 = lax.broadcasted_iota(jnp.int32, (8, x.shape[1]), 0)
    head_x = x[0:8]
    shifted, shifted_head = [], []
    for j in range(SSD_CONV):
        if j == 0:
            shifted.append(x)
            shifted_head.append(head_x)
        else:
            shifted.append(pltpu.roll(x, j, 0))
            shifted_head.append(jnp.where(row8 < j, pltpu.roll(carry, j, 0), pltpu.roll(head_x, j, 0)))
    pre = b_ref[...] + sum(w_ref[SSD_CONV - 1 - j:SSD_CONV - j, :] * shifted[j] for j in range(SSD_CONV))
    pre_head = b_ref[...] + sum(w_ref[SSD_CONV - 1 - j:SSD_CONV - j, :] * shifted_head[j] for j in range(SSD_CONV))
    del ts
    return pre, pre_head, shifted, shifted_head


def _conv_fwd(proj, w, b, *, name):
    s = proj.shape[0]
    c = w.shape[1]
    ts, tc = _pick(s, CONV_TS, 8), CONV_TC
    off = OFF_XBC // tc

    def body(x_ref, w_ref, b_ref, o_ref, carry_ref):
        t = pl.program_id(1)

        @pl.when(t == 0)
        def _():
            carry_ref[...] = jnp.zeros_like(carry_ref)

        x = x_ref[...]
        pre, pre_head, _, _ = _conv_pre(x, carry_ref[...], w_ref, b_ref)
        o_ref[...] = _silu(pre)
        o_ref[0:8, :] = _silu(pre_head)
        carry_ref[...] = x[ts - 8:ts]

    return pl.pallas_call(
        body, name=name, grid=(c // tc, s // ts),
        in_specs=[pl.BlockSpec((ts, tc), lambda j, t: (t, j + off)), pl.BlockSpec((SSD_CONV, tc), lambda j, t: (0, j)),
                  pl.BlockSpec((1, tc), lambda j, t: (0, j))],
        out_specs=pl.BlockSpec((ts, tc), lambda j, t: (t, j)),
        out_shape=jax.ShapeDtypeStruct((s, c), F32),
        scratch_shapes=[pltpu.VMEM((8, tc), F32)],
        compiler_params=_cp("parallel", "arbitrary"),
    )(proj, w, b)


def _conv_bwd_pre(proj, w, b, dy, *, name):
    s = proj.shape[0]
    c = w.shape[1]
    ts, tc = _pick(s, CONV_TS, 8), CONV_TC
    off = OFF_XBC // tc

    def body(x_ref, w_ref, b_ref, dy_ref, dp_ref, dw_ref, db_ref, carry_ref):
        t = pl.program_id(1)

        @pl.when(t == 0)
        def _():
            carry_ref[...] = jnp.zeros_like(carry_ref)
            dw_ref[...] = jnp.zeros_like(dw_ref)
            db_ref[...] = jnp.zeros_like(db_ref)

        x = x_ref[...]
        pre, pre_head, shifted, shifted_head = _conv_pre(x, carry_ref[...], w_ref, b_ref)
        dyv = dy_ref[...]
        dp = dyv * _dsilu(pre)
        dp_head = dyv[0:8] * _dsilu(pre_head)
        row = lax.broadcasted_iota(jnp.int32, dp.shape, 0)
        dp_tail = jnp.where(row >= 8, dp, 0.0)
        dp_ref[...] = dp
        dp_ref[0:8, :] = dp_head
        db_ref[...] += jnp.sum(dp_tail, axis=0, keepdims=True) + jnp.sum(dp_head, axis=0, keepdims=True)
        for j in range(SSD_CONV):
            kk = SSD_CONV - 1 - j
            dw_ref[kk:kk + 1, :] += (jnp.sum(dp_tail * shifted[j], axis=0, keepdims=True)
                                     + jnp.sum(dp_head * shifted_head[j], axis=0, keepdims=True))
        carry_ref[...] = x[ts - 8:ts]

    return pl.pallas_call(
        body, name=name, grid=(c // tc, s // ts),
        in_specs=[pl.BlockSpec((ts, tc), lambda j, t: (t, j + off)), pl.BlockSpec((SSD_CONV, tc), lambda j, t: (0, j)),
                  pl.BlockSpec((1, tc), lambda j, t: (0, j)), pl.BlockSpec((ts, tc), lambda j, t: (t, j))],
        out_specs=[pl.BlockSpec((ts, tc), lambda j, t: (t, j)), pl.BlockSpec((SSD_CONV, tc), lambda j, t: (0, j)),
                   pl.BlockSpec((1, tc), lambda j, t: (0, j))],
        out_shape=[jax.ShapeDtypeStruct((s, c), F32), jax.ShapeDtypeStruct((SSD_CONV, c), F32),
                   jax.ShapeDtypeStruct((1, c), F32)],
        scratch_shapes=[pltpu.VMEM((8, tc), F32)],
        compiler_params=_cp("parallel", "arbitrary"),
    )(proj, w, b, dy)


def _conv_bwd_x(dp, w, *, name):
    s, c = dp.shape
    ts, tc = _pick(s, CONV_TS, 8), CONV_TC
    nt = s // ts

    def body(d_ref, w_ref, o_ref, carry_ref):
        t = pl.program_id(1)

        @pl.when(t == 0)
        def _():
            carry_ref[...] = jnp.zeros_like(carry_ref)

        d = d_ref[...]
        carry = carry_ref[...]
        row8 = lax.broadcasted_iota(jnp.int32, (8, tc), 0)
        tail = d[ts - 8:ts]
        acc = w_ref[SSD_CONV - 1:SSD_CONV, :] * d
        acc_tail = w_ref[SSD_CONV - 1:SSD_CONV, :] * tail
        for j in range(1, SSD_CONV):
            wj = w_ref[SSD_CONV - 1 - j:SSD_CONV - j, :]
            acc = acc + wj * pltpu.roll(d, ts - j, 0)
            up_tail = jnp.where(row8 >= 8 - j, pltpu.roll(carry, 8 - j, 0), pltpu.roll(tail, 8 - j, 0))
            acc_tail = acc_tail + wj * up_tail
        o_ref[...] = acc
        o_ref[ts - 8:ts, :] = acc_tail
        carry_ref[...] = d[0:8]

    return pl.pallas_call(
        body, name=name, grid=(c // tc, nt),
        in_specs=[pl.BlockSpec((ts, tc), lambda j, t: (nt - 1 - t, j)), pl.BlockSpec((SSD_CONV, tc), lambda j, t: (0, j))],
        out_specs=pl.BlockSpec((ts, tc), lambda j, t: (nt - 1 - t, j)),
        out_shape=jax.ShapeDtypeStruct((s, c), F32),
        scratch_shapes=[pltpu.VMEM((8, tc), F32)],
        compiler_params=_cp("parallel", "arbitrary"),
    )(dp, w)


def _merge_fwd(proj, ys, ym, *, name):
    r, w = ys.shape
    tr = _pick(r, 512, 8)
    off = OFF_GATES // w

    def body(g1_ref, g2_ref, ys_ref, ym_ref, o_ref):
        o_ref[...] = (jax.nn.sigmoid(g1_ref[...]) * ys_ref[...]
                      + jax.nn.sigmoid(g2_ref[...]) * ym_ref[...]).astype(o_ref.dtype)

    blk = pl.BlockSpec((tr, w), lambda i: (i, 0))
    return pl.pallas_call(
        body, name=name, grid=(r // tr,),
        in_specs=[pl.BlockSpec((tr, w), lambda i: (i, off)), pl.BlockSpec((tr, w), lambda i: (i, off + 1)), blk, blk],
        out_specs=blk, out_shape=jax.ShapeDtypeStruct((r, w), BF16),
        compiler_params=_cp("parallel"),
    )(proj, proj, ys, ym)


def _merge_bwd(proj, ys, ym, dm, *, name):
    r, w = ys.shape
    tr = _pick(r, 512, 8)
    off = OFF_GATES // w

    def body(g1_ref, g2_ref, ys_ref, ym_ref, dm_ref, dg_ref, dys_ref, dym_ref):
        s1, s2 = jax.nn.sigmoid(g1_ref[...]), jax.nn.sigmoid(g2_ref[...])
        dmv = dm_ref[...]
        dys_ref[...] = (dmv * s1).astype(dys_ref.dtype)
        dym_ref[...] = (dmv * s2).astype(dym_ref.dtype)
        dg_ref[:, :w] = dmv * ys_ref[...] * s1 * (1.0 - s1)
        dg_ref[:, w:] = dmv * ym_ref[...] * s2 * (1.0 - s2)

    blk = pl.BlockSpec((tr, w), lambda i: (i, 0))
    return pl.pallas_call(
        body, name=name, grid=(r // tr,),
        in_specs=[pl.BlockSpec((tr, w), lambda i: (i, off)), pl.BlockSpec((tr, w), lambda i: (i, off + 1)), blk, blk, blk],
        out_specs=[pl.BlockSpec((tr, 2 * w), lambda i: (i, 0)), blk, blk],
        out_shape=[jax.ShapeDtypeStruct((r, 2 * w), F32), jax.ShapeDtypeStruct((r, w), BF16),
                   jax.ShapeDtypeStruct((r, w), BF16)],
        compiler_params=_cp("parallel"),
    )(proj, proj, ys, ym, dm)


def _loss_fwd_bwd(y, target, *, name):
    r, w = y.shape
    tr = _pick(r, 512, 8)

    def body(y_ref, t_ref, l_ref, dy_ref):
        i = pl.program_id(0)
        e = y_ref[...] - t_ref[...]
        dy_ref[...] = e * (1.0 / w)
        part = jnp.sum(e * e, axis=0, keepdims=True) * (0.5 / w)

        @pl.when(i == 0)
        def _():
            l_ref[...] = part

        @pl.when(i > 0)
        def _():
            l_ref[...] += part

    blk = pl.BlockSpec((tr, w), lambda i: (i, 0))
    return pl.pallas_call(
        body, name=name, grid=(r // tr,),
        in_specs=[blk, blk],
        out_specs=[pl.BlockSpec((1, w), lambda i: (0, 0)), blk],
        out_shape=[jax.ShapeDtypeStruct((1, w), F32), jax.ShapeDtypeStruct((r, w), F32)],
        compiler_params=_cp("arbitrary"),
    )(y, target)


def _adamw(w, g, m, v, *, name):
    r, c = w.shape
    tr = _pick(r, max(8, (1 << 20) // (4 * c) // 8 * 8), 8)
    c1 = 1.0 - ADAM_B1 ** ADAM_STEP
    c2 = 1.0 - ADAM_B2 ** ADAM_STEP

    def body(w_ref, g_ref, m_ref, v_ref, d_ref, nm_ref, nv_ref):
        gv = g_ref[...]
        nm = ADAM_B1 * m_ref[...] + (1.0 - ADAM_B1) * gv
        nv = ADAM_B2 * v_ref[...] + (1.0 - ADAM_B2) * (gv * gv)
        nm_ref[...] = nm
        nv_ref[...] = nv
        d_ref[...] = -ADAM_LR * ((nm / c1) / (jnp.sqrt(nv / c2) + ADAM_EPS) + ADAM_WD * w_ref[...])

    blk = pl.BlockSpec((tr, c), lambda i: (i, 0))
    sh = jax.ShapeDtypeStruct((r, c), F32)
    return pl.pallas_call(
        body, name=name, grid=(r // tr,),
        in_specs=[blk] * 4, out_specs=[blk] * 3, out_shape=[sh] * 3,
        compiler_params=_cp("parallel"),
    )(w, g, m, v)


def _softplus(x):
    return jnp.maximum(x, 0.0) + jnp.log(1.0 + jnp.exp(-jnp.abs(x)))


def _ssd_common(dtr_ref, dtrT_ref, dtb_ref, dtbT_ref, al_ref, alT_ref, e_ref):
    L = SSD_CHUNK
    ri = lax.broadcasted_iota(jnp.int32, (L, L), 0)
    cj = lax.broadcasted_iota(jnp.int32, (L, L), 1)
    tril = (ri >= cj).astype(F32)
    triu = (ri <= cj).astype(F32)
    a = -jnp.exp(al_ref[...])
    aT = -jnp.exp(alT_ref[...])
    pre = dtr_ref[...] + dtb_ref[...]
    preT = dtrT_ref[...] + dtbT_ref[...]
    dt = _softplus(pre)
    dtT = _softplus(preT)
    acum = jnp.dot(tril, dt * a, precision=HI, preferred_element_type=F32)
    acumT = jnp.dot(dtT * aT, triu, precision=HI, preferred_element_type=F32)
    e = e_ref[...]
    dt_x = jnp.dot(dt, e, precision=HI, preferred_element_type=F32)
    acum_x = jnp.dot(acum, e, precision=HI, preferred_element_type=F32)
    last_x = acum_x[L - 1:L, :]
    return dict(ri=ri, cj=cj, tril=tril, triu=triu, a=a, aT=aT, pre=pre, preT=preT, dt=dt, dtT=dtT,
                acum=acum, acumT=acumT, dt_x=dt_x, eacum_x=jnp.exp(acum_x), w_x=jnp.exp(last_x - acum_x),
                elast_x=jnp.exp(last_x))


def _dot_nt(a, b):
    return lax.dot_general(a, b, (((1,), (1,)), ((), ())), preferred_element_type=F32)


def _dot_tn(a, b):
    return lax.dot_general(a, b, (((0,), (0,)), ((), ())), preferred_element_type=F32)


def _dot(a, b):
    return jnp.dot(a, b, preferred_element_type=F32)


def _ssd_specs(nc, rev):
    L = SSD_CHUNK
    ix = (lambda c: nc - 1 - c) if rev else (lambda c: c)
    return [
        pl.BlockSpec((L, SSD_D_INNER), lambda c: (ix(c), 0)),
        pl.BlockSpec((L, 512), lambda c: (ix(c), 4)),
        pl.BlockSpec((L, 512), lambda c: (ix(c), 5)),
        pl.BlockSpec((L, SSD_HEADS), lambda c: (ix(c), 0)),
        pl.BlockSpec((SSD_HEADS, L), lambda c: (0, ix(c))),
        pl.BlockSpec((1, SSD_HEADS), lambda c: (0, 0)),
        pl.BlockSpec((SSD_HEADS, 1), lambda c: (0, 0)),
        pl.BlockSpec((1, SSD_HEADS), lambda c: (0, 0)),
        pl.BlockSpec((SSD_HEADS, 1), lambda c: (0, 0)),
        pl.BlockSpec((1, SSD_D_INNER), lambda c: (0, 0)),
        pl.BlockSpec((SSD_HEADS, SSD_D_INNER), lambda c: (0, 0)),
    ]


def _ssd_fwd(xc, dtr, dtrT, dtb, dtbT, alog, alogT, dskx, expand, *, name):
    s = xc.shape[0]
    L = SSD_CHUNK
    nc = s // L

    def body(x_ref, b_ref, c_ref, dtr_ref, dtrT_ref, dtb_ref, dtbT_ref, al_ref, alT_ref, dsk_ref, e_ref,
             y_ref, st_ref, state):
        ci = pl.program_id(0)

        @pl.when(ci == 0)
        def _():
            state[...] = jnp.zeros_like(state)

        st_ref[0] = state[...]
        q = _ssd_common(dtr_ref, dtrT_ref, dtb_ref, dtbT_ref, al_ref, alT_ref, e_ref)
        causal = q["ri"] >= q["cj"]
        lane_lo = q["cj"] < 64
        x = x_ref[...]
        xdt = x * q["dt_x"]
        xdt_b = xdt.astype(BF16)
        xdtw_b = (xdt * q["w_x"]).astype(BF16)
        for g in range(SSD_GROUPS):
            bg = b_ref[:, 128 * g:128 * g + 128]
            cg_b = c_ref[:, 128 * g:128 * g + 128].astype(BF16)
            cb = _dot_nt(cg_b, bg.astype(BF16))
            bgT_b = bg.T.astype(BF16)
            s0 = state[g]
            for jj in range(4):
                j = 4 * g + jj
                sl = slice(128 * j, 128 * j + 128)
                sls = slice(128 * jj, 128 * jj + 128)
                ms = []
                for h in (2 * j, 2 * j + 1):
                    seg = q["acum"][:, h:h + 1] - q["acumT"][h:h + 1, :]
                    decay = jnp.exp(jnp.where(causal, seg, -jnp.inf))
                    ms.append((cb * decay).astype(BF16))
                mcat = jnp.concatenate(ms, axis=1)
                xp = xdt_b[:, sl]
                zero = jnp.zeros_like(xp)
                xstack = jnp.concatenate([jnp.where(lane_lo, xp, zero), jnp.where(lane_lo, zero, xp)], axis=0)
                y = _dot(mcat, xstack)
                y = y + q["eacum_x"][:, sl] * _dot(cg_b, s0[:, sls].astype(BF16))
                y = y + x[:, sl] * dsk_ref[:, sl]
                y_ref[:, sl] = y
                state[g, :, sls] = s0[:, sls] * q["elast_x"][:, sl] + _dot(bgT_b, xdtw_b[:, sl])

    return pl.pallas_call(
        body, name=name, grid=(nc,),
        in_specs=_ssd_specs(nc, False),
        out_specs=[pl.BlockSpec((L, SSD_D_INNER), lambda c: (c, 0)),
                   pl.BlockSpec((1, SSD_GROUPS, SSD_STATE, 512), lambda c: (c, 0, 0, 0))],
        out_shape=[jax.ShapeDtypeStruct((s, SSD_D_INNER), F32),
                   jax.ShapeDtypeStruct((nc, SSD_GROUPS, SSD_STATE, 512), F32)],
        scratch_shapes=[pltpu.VMEM((SSD_GROUPS, SSD_STATE, 512), F32)],
        compiler_params=_cp("arbitrary"),
    )(xc, xc, xc, dtr, dtrT, dtb, dtbT, alog, alogT, dskx, expand)


def _ssd_bwd(xc, dtr, dtrT, dtb, dtbT, alog, alogT, dskx, expand, expandT, states, dy, *, name):
    s = xc.shape[0]
    L = SSD_CHUNK
    H = SSD_HEADS
    nc = s // L

    def body(x_ref, b_ref, c_ref, dtr_ref, dtrT_ref, dtb_ref, dtbT_ref, al_ref, alT_ref, dsk_ref, e_ref,
             et_ref, st_ref, dy_ref,
             dxc_ref, ddtc_ref, ddtr_ref, dbc_ref, dbr_ref, dac_ref, dar_ref, ddsk_ref, dstate):
        ci = pl.program_id(0)

        @pl.when(ci == 0)
        def _():
            dstate[...] = jnp.zeros_like(dstate)
            dbc_ref[...] = jnp.zeros_like(dbc_ref)
            dbr_ref[...] = jnp.zeros_like(dbr_ref)
            dac_ref[...] = jnp.zeros_like(dac_ref)
            dar_ref[...] = jnp.zeros_like(dar_ref)
            ddsk_ref[...] = jnp.zeros_like(ddsk_ref)

        q = _ssd_common(dtr_ref, dtrT_ref, dtb_ref, dtbT_ref, al_ref, alT_ref, e_ref)
        ri, cj = q["ri"], q["cj"]
        causal = ri >= cj
        causalT = ri <= cj
        lane_lo = cj < 64
        lane_h = lax.broadcasted_iota(jnp.int32, (1, H), 1)
        sub_h = lax.broadcasted_iota(jnp.int32, (H, 1), 0)
        x = x_ref[...]
        dyv = dy_ref[...]
        xdt = x * q["dt_x"]
        xdt_b = xdt.astype(BF16)
        xdtw = xdt * q["w_x"]
        xdtw_b = xdtw.astype(BF16)
        edy = q["eacum_x"] * dyv
        edy_b = edy.astype(BF16)
        dyv_b = dyv.astype(BF16)
        dacum_col = jnp.zeros((L, H), F32)
        dacum_row = jnp.zeros((H, L), F32)
        dxdt_t, yoff_t, u_t, r_t = [], [], [], []
        for g in range(SSD_GROUPS):
            bg = b_ref[:, 128 * g:128 * g + 128]
            cg = c_ref[:, 128 * g:128 * g + 128]
            bg_b, cg_b = bg.astype(BF16), cg.astype(BF16)
            cb = _dot_nt(cg_b, bg_b)
            cbT = _dot_nt(bg_b, cg_b)
            cgT_b = cg.T.astype(BF16)
            s0 = st_ref[0, g]
            ds = dstate[g]
            s0_b, ds_b = s0.astype(BF16), ds.astype(BF16)
            dcb = jnp.zeros((L, L), F32)
            for jj in range(4):
                j = 4 * g + jj
                sl = slice(128 * j, 128 * j + 128)
                sls = slice(128 * jj, 128 * jj + 128)
                decs, mts = [], []
                for h in (2 * j, 2 * j + 1):
                    seg = q["acum"][:, h:h + 1] - q["acumT"][h:h + 1, :]
                    decs.append(jnp.exp(jnp.where(causal, seg, -jnp.inf)))
                    mts.append((cbT * jnp.exp(jnp.where(causalT, -seg, -jnp.inf))).astype(BF16))
                dyt_b = dyv_b[:, sl]
                zero = jnp.zeros_like(dyt_b)
                dystack = jnp.concatenate([jnp.where(lane_lo, dyt_b, zero), jnp.where(lane_lo, zero, dyt_b)], axis=0)
                dxs = _dot(jnp.concatenate(mts, axis=0), dyt_b)
                dxdt = jnp.where(lane_lo, dxs[:L], dxs[L:])
                dmcat = _dot_nt(dystack, xdt_b[:, sl])
                for idx, h in enumerate((2 * j, 2 * j + 1)):
                    dm = dmcat[L * idx:L * idx + L]
                    dcb = dcb + dm * decs[idx]
                    dseg = dm * cb * decs[idx]
                    dacum_col = dacum_col + jnp.sum(dseg, axis=1, keepdims=True) * (lane_h == h).astype(F32)
                    dacum_row = dacum_row - (sub_h == h).astype(F32) * jnp.sum(dseg, axis=0, keepdims=True)
                gmat = _dot(cg_b, s0_b[:, sls])
                yoff_t.append(edy[:, sl] * gmat)
                qm = _dot(bg_b, ds_b[:, sls])
                dxdt_t.append(dxdt + qm * q["w_x"][:, sl])
                u_t.append(qm * xdtw[:, sl])
                r_t.append(ds[:, sls] * s0[:, sls] * q["elast_x"][:, sl])
                dstate[g, :, sls] = ds[:, sls] * q["elast_x"][:, sl] + _dot(cgT_b, edy_b[:, sl])
            gsl = slice(512 * g, 512 * g + 512)
            dcb_b = dcb.astype(BF16)
            dcg = _dot(dcb_b, bg_b) + _dot_nt(edy_b[:, gsl], s0_b)
            dbg = _dot(dcb.T.astype(BF16), cg_b) + _dot_nt(xdtw_b[:, gsl], ds_b)
            dxc_ref[:, SSD_D_INNER + 128 * g:SSD_D_INNER + 128 * g + 128] = dbg
            dxc_ref[:, SSD_D_INNER + 512 + 128 * g:SSD_D_INNER + 512 + 128 * g + 128] = dcg
        et = et_ref[...]
        dxdt_all = jnp.concatenate(dxdt_t, axis=1)
        yoff = jnp.concatenate(yoff_t, axis=1)
        uu = jnp.concatenate(u_t, axis=1)
        rr = jnp.concatenate(r_t, axis=1)
        dacum_col = dacum_col + jnp.dot(yoff - uu, et, precision=HI, preferred_element_type=F32)
        dlast = jnp.sum(jnp.dot(uu + rr, et, precision=HI, preferred_element_type=F32), axis=0, keepdims=True)
        row_lh = lax.broadcasted_iota(jnp.int32, (L, H), 0)
        dacum_col = dacum_col + jnp.where(row_lh == L - 1, dlast, 0.0)
        d_dta_col = jnp.dot(q["triu"], dacum_col, precision=HI, preferred_element_type=F32)
        d_dta_row = jnp.dot(dacum_row, q["tril"], precision=HI, preferred_element_type=F32)
        ddt_col = d_dta_col * q["a"] + jnp.dot(dxdt_all * x, et, precision=HI, preferred_element_type=F32)
        ddt_row = d_dta_row * q["aT"]
        ddtr_col = ddt_col * jax.nn.sigmoid(q["pre"])
        ddtr_row = ddt_row * jax.nn.sigmoid(q["preT"])
        ddtc_ref[...] = ddtr_col
        ddtr_ref[...] = ddtr_row
        dac_ref[...] += jnp.sum(d_dta_col * q["dt"], axis=0, keepdims=True)
        dar_ref[...] += jnp.sum(d_dta_row * q["dtT"], axis=1, keepdims=True)
        dbc_ref[...] += jnp.sum(ddtr_col, axis=0, keepdims=True)
        dbr_ref[...] += jnp.sum(ddtr_row, axis=1, keepdims=True)
        ddsk_ref[...] += jnp.sum(dyv * x, axis=0, keepdims=True)
        dxc_ref[:, 0:SSD_D_INNER] = dxdt_all * q["dt_x"] + dyv * dsk_ref[...]

    rv = lambda c: nc - 1 - c
    in_specs = _ssd_specs(nc, True) + [
        pl.BlockSpec((SSD_D_INNER, H), lambda c: (0, 0)),
        pl.BlockSpec((1, SSD_GROUPS, SSD_STATE, 512), lambda c: (rv(c), 0, 0, 0)),
        pl.BlockSpec((L, SSD_D_INNER), lambda c: (rv(c), 0)),
    ]
    vec_c = pl.BlockSpec((1, H), lambda c: (0, 0))
    vec_r = pl.BlockSpec((H, 1), lambda c: (0, 0))
    return pl.pallas_call(
        body, name=name, grid=(nc,),
        in_specs=in_specs,
        out_specs=[pl.BlockSpec((L, SSD_CONV_DIM), lambda c: (rv(c), 0)),
                   pl.BlockSpec((L, H), lambda c: (rv(c), 0)),
                   pl.BlockSpec((H, L), lambda c: (0, rv(c))),
                   vec_c, vec_r, vec_c, vec_r,
                   pl.BlockSpec((1, SSD_D_INNER), lambda c: (0, 0))],
        out_shape=[jax.ShapeDtypeStruct((s, SSD_CONV_DIM), F32),
                   jax.ShapeDtypeStruct((s, H), F32), jax.ShapeDtypeStruct((H, s), F32),
                   jax.ShapeDtypeStruct((1, H), F32), jax.ShapeDtypeStruct((H, 1), F32),
                   jax.ShapeDtypeStruct((1, H), F32), jax.ShapeDtypeStruct((H, 1), F32),
                   jax.ShapeDtypeStruct((1, SSD_D_INNER), F32)],
        scratch_shapes=[pltpu.VMEM((SSD_GROUPS, SSD_STATE, 512), F32)],
        compiler_params=_cp("arbitrary"),
    )(xc, xc, xc, dtr, dtrT, dtb, dtbT, alog, alogT, dskx, expand, expandT, states, dy)


QK_PAD = 256
MLA_TS = 256


def _rope_tables4(pos):
    inv = 1.0 / (ROPE_THETA ** (jnp.arange(0, MLA_ROPE, 2, dtype=F32) / MLA_ROPE))
    ang = pos.astype(F32)[:, None] * inv
    c, s = jnp.cos(ang), jnp.sin(ang)
    return jnp.tile(c, (1, 4)), jnp.concatenate([-s, s, -s, s], axis=1)


def _mla_gains(qg, kg):
    z = jnp.zeros((LANE - MLA_ROPE,), F32)
    return (qg[:MLA_NOPE][None], jnp.concatenate([qg[MLA_NOPE:], z])[None],
            kg[:MLA_NOPE][None], jnp.concatenate([kg[MLA_NOPE:], z])[None])


def _rope_swap(t, first):
    return jnp.where(first, pltpu.roll(t, 96, 1), pltpu.roll(t, 32, 1))


def _mla_prep_specs(ts):
    row = lambda w, c=0: pl.BlockSpec((ts, w), lambda i: (i, c))
    vec = pl.BlockSpec((1, LANE), lambda i: (0, 0))
    return [row(MLA_HEADS * MLA_QK), row(2 * MLA_HEADS * MLA_NOPE), row(LANE, OFF_KRDT // LANE), row(LANE), row(LANE),
            vec, vec, vec, vec]


def _mla_prep_fwd(qraw, kvraw, proj, cos4, sin4, gqn, gqr, gkn, gkr, *, name):
    s = qraw.shape[0]
    ts = _pick(s, MLA_TS, 8)

    def body(q_ref, kv_ref, kr_ref, cos_ref, sin_ref, gqn_ref, gqr_ref, gkn_ref, gkr_ref, qo_ref, ko_ref):
        lane = lax.broadcasted_iota(jnp.int32, (ts, LANE), 1)
        lo = lane < 64
        first = (lane % 64) < 32
        cos, sin = cos_ref[...], sin_ref[...]
        kr = jnp.where(lo, kr_ref[...], 0.0)
        ssq_kr = jnp.sum(kr * kr, axis=-1, keepdims=True)

        def head(xn, xr, ssq_r, gn, gr):
            rs = lax.rsqrt((jnp.sum(xn * xn, axis=-1, keepdims=True) + ssq_r) * (1.0 / MLA_QK) + EPS)
            yr = xr * rs * gr
            return xn * rs * gn, yr * cos + _rope_swap(yr, first) * sin

        for h in range(MLA_HEADS):
            tile = q_ref[:, MLA_HEADS * MLA_NOPE + LANE * (h // 2):MLA_HEADS * MLA_NOPE + LANE * (h // 2) + LANE]
            qr = jnp.where(lo, tile if h % 2 == 0 else pltpu.roll(tile, 64, 1), 0.0)
            on, orr = head(q_ref[:, LANE * h:LANE * h + LANE], qr, jnp.sum(qr * qr, axis=-1, keepdims=True),
                           gqn_ref[...], gqr_ref[...])
            qo_ref[h, :, 0:LANE] = on.astype(BF16)
            qo_ref[h, :, LANE:QK_PAD] = orr.astype(BF16)
            on, orr = head(kv_ref[:, LANE * h:LANE * h + LANE], kr, ssq_kr, gkn_ref[...], gkr_ref[...])
            ko_ref[h, :, 0:LANE] = on.astype(BF16)
            ko_ref[h, :, LANE:QK_PAD] = orr.astype(BF16)

    out = pl.BlockSpec((MLA_HEADS, ts, QK_PAD), lambda i: (0, i, 0))
    sh = jax.ShapeDtypeStruct((MLA_HEADS, s, QK_PAD), BF16)
    return pl.pallas_call(
        body, name=name, grid=(s // ts,),
        in_specs=_mla_prep_specs(ts), out_specs=[out, out], out_shape=[sh, sh],
        compiler_params=_cp("parallel"),
    )(qraw, kvraw, proj, cos4, sin4, gqn, gqr, gkn, gkr)


def _mla_prep_bwd(qraw, kvraw, proj, cos4, sin4, gqn, gqr, gkn, gkr, dq, dk, *, name):
    s = qraw.shape[0]
    ts = _pick(s, MLA_TS, 8)

    def body(q_ref, kv_ref, kr_ref, cos_ref, sin_ref, gqn_ref, gqr_ref, gkn_ref, gkr_ref, dq_ref, dk_ref,
             dqraw_ref, dkn_ref, dkr_ref, dgqn_ref, dgqr_ref, dgkn_ref, dgkr_ref):
        i = pl.program_id(0)

        @pl.when(i == 0)
        def _():
            for r in (dgqn_ref, dgqr_ref, dgkn_ref, dgkr_ref):
                r[...] = jnp.zeros_like(r)

        lane = lax.broadcasted_iota(jnp.int32, (ts, LANE), 1)
        lo = lane < 64
        first = (lane % 64) < 32
        cos, sin = cos_ref[...], sin_ref[...]
        kr = jnp.where(lo, kr_ref[...], 0.0)
        ssq_kr = jnp.sum(kr * kr, axis=-1, keepdims=True)

        def head(xn, xr, ssq_r, gn, gr, don, dor):
            rs = lax.rsqrt((jnp.sum(xn * xn, axis=-1, keepdims=True) + ssq_r) * (1.0 / MLA_QK) + EPS)
            xhn, xhr = xn * rs, xr * rs
            dor = jnp.where(lo, dor, 0.0)
            dyr = dor * cos + _rope_swap(dor * sin, first)
            dxn, dxr = don * gn, dyr * gr
            mm = (jnp.sum(dxn * xhn, axis=-1, keepdims=True) + jnp.sum(dxr * xhr, axis=-1, keepdims=True)) * (1.0 / MLA_QK)
            return (rs * (dxn - xhn * mm), rs * (dxr - xhr * mm),
                    jnp.sum(don * xhn, axis=0, keepdims=True), jnp.sum(dyr * xhr, axis=0, keepdims=True))

        dkr_acc = jnp.zeros((ts, LANE), F32)
        prev = None
        for h in range(MLA_HEADS):
            c0 = MLA_HEADS * MLA_NOPE + LANE * (h // 2)
            tile = q_ref[:, c0:c0 + LANE]
            qr = jnp.where(lo, tile if h % 2 == 0 else pltpu.roll(tile, 64, 1), 0.0)
            dn, dr, gn_p, gr_p = head(q_ref[:, LANE * h:LANE * h + LANE], qr, jnp.sum(qr * qr, axis=-1, keepdims=True),
                                      gqn_ref[...], gqr_ref[...], dq_ref[h, :, 0:LANE], dq_ref[h, :, LANE:QK_PAD])
            dqraw_ref[:, LANE * h:LANE * h + LANE] = dn.astype(dqraw_ref.dtype)
            dgqn_ref[...] += gn_p
            dgqr_ref[...] += gr_p
            if h % 2 == 0:
                prev = dr
            else:
                dqraw_ref[:, c0:c0 + LANE] = (prev + pltpu.roll(dr, 64, 1)).astype(dqraw_ref.dtype)
            dn, dr, gn_p, gr_p = head(kv_ref[:, LANE * h:LANE * h + LANE], kr, ssq_kr, gkn_ref[...], gkr_ref[...],
                                      dk_ref[h, :, 0:LANE], dk_ref[h, :, LANE:QK_PAD])
            dkn_ref[:, LANE * h:LANE * h + LANE] = dn
            dkr_acc = dkr_acc + dr
            dgkn_ref[...] += gn_p
            dgkr_ref[...] += gr_p
        dkr_ref[...] = dkr_acc

    row = lambda w: pl.BlockSpec((ts, w), lambda i: (i, 0))
    vec = pl.BlockSpec((1, LANE), lambda i: (0, 0))
    dspec = pl.BlockSpec((MLA_HEADS, ts, QK_PAD), lambda i: (0, i, 0))
    vsh = jax.ShapeDtypeStruct((1, LANE), F32)
    return pl.pallas_call(
        body, name=name, grid=(s // ts,),
        in_specs=_mla_prep_specs(ts) + [dspec, dspec],
        out_specs=[row(MLA_HEADS * MLA_QK), row(MLA_HEADS * MLA_NOPE), row(LANE), vec, vec, vec, vec],
        out_shape=[jax.ShapeDtypeStruct((s, MLA_HEADS * MLA_QK), BF16), jax.ShapeDtypeStruct((s, MLA_HEADS * MLA_NOPE), F32),
                   jax.ShapeDtypeStruct((s, LANE), F32), vsh, vsh, vsh, vsh],
        compiler_params=_cp("arbitrary"),
    )(qraw, kvraw, proj, cos4, sin4, gqn, gqr, gkn, gkr, dq, dk)


ATT_T = 1024
ATT_SCALE = MLA_QK ** -0.5


def _attn_fwd(q, k, kvraw, *, name):
    nh, s, _ = q.shape
    t = _pick(s, ATT_T, LANE)
    nb = s // t

    def body(q_ref, k_ref, v_ref, o_ref, lse_ref, m_ref, l_ref, acc_ref):
        i, j = pl.program_id(1), pl.program_id(2)

        @pl.when(j == 0)
        def _():
            m_ref[...] = jnp.full_like(m_ref, -jnp.inf)
            l_ref[...] = jnp.zeros_like(l_ref)
            acc_ref[...] = jnp.zeros_like(acc_ref)

        def step(diagonal):
            sc = _dot_nt(q_ref[0], k_ref[0]) * ATT_SCALE
            if diagonal:
                ri = lax.broadcasted_iota(jnp.int32, (t, t), 0)
                cj = lax.broadcasted_iota(jnp.int32, (t, t), 1)
                sc = jnp.where(ri >= cj, sc, -jnp.inf)
            m_new = jnp.maximum(m_ref[...], jnp.max(sc, axis=-1, keepdims=True))
            alpha = jnp.exp(m_ref[...] - m_new)
            p = jnp.exp(sc - m_new)
            l_ref[...] = alpha * l_ref[...] + jnp.sum(p, axis=-1, keepdims=True)
            acc_ref[...] = alpha * acc_ref[...] + _dot(p.astype(BF16), v_ref[...].astype(BF16))
            m_ref[...] = m_new

        @pl.when(j < i)
        def _():
            step(False)

        @pl.when(j == i)
        def _():
            step(True)
            o_ref[...] = acc_ref[...] / l_ref[...]
            lse_ref[0] = m_ref[...] + jnp.log(l_ref[...])

    return pl.pallas_call(
        body, name=name, grid=(nh, nb, nb),
        in_specs=[pl.BlockSpec((1, t, QK_PAD), lambda h, i, j: (h, i, 0)),
                  pl.BlockSpec((1, t, QK_PAD), lambda h, i, j: (h, jnp.minimum(j, i), 0)),
                  pl.BlockSpec((t, MLA_V), lambda h, i, j: (jnp.minimum(j, i), nh + h))],
        out_specs=[pl.BlockSpec((t, MLA_V), lambda h, i, j: (i, h)),
                   pl.BlockSpec((1, t, 1), lambda h, i, j: (h, i, 0))],
        out_shape=[jax.ShapeDtypeStruct((s, nh * MLA_V), F32), jax.ShapeDtypeStruct((nh, s, 1), F32)],
        scratch_shapes=[pltpu.VMEM((t, 1), F32), pltpu.VMEM((t, 1), F32), pltpu.VMEM((t, MLA_V), F32)],
        compiler_params=_cp("parallel", "parallel", "arbitrary"),
    )(q, k, kvraw)


def _attn_bwd(q, k, kvraw, o, lse, do, *, name):
    nh, s, _ = q.shape
    t = _pick(s, ATT_T, LANE)
    nb = s // t

    def body(q_ref, k_ref, v_ref, o_ref, lse_ref, do_ref, dq_ref, dk_ref, dv_ref, dk_acc, dv_acc):
        j, i = pl.program_id(1), pl.program_id(2)

        @pl.when(i == 0)
        def _():
            dk_acc[...] = jnp.zeros_like(dk_acc)
            dv_acc[...] = jnp.zeros_like(dv_acc)

        def step(diagonal):
            qv, kv = q_ref[0], k_ref[0]
            sc = _dot_nt(qv, kv) * ATT_SCALE
            if diagonal:
                ri = lax.broadcasted_iota(jnp.int32, (t, t), 0)
                cj = lax.broadcasted_iota(jnp.int32, (t, t), 1)
                sc = jnp.where(ri >= cj, sc, -jnp.inf)
            p = jnp.exp(sc - lse_ref[0])
            dov = do_ref[...]
            delta = jnp.sum(dov * o_ref[...], axis=-1, keepdims=True)
            do_b = dov.astype(BF16)
            dv_acc[...] += _dot_tn(p.astype(BF16), do_b)
            dp = _dot_nt(do_b, v_ref[...].astype(BF16))
            ds_b = (p * (dp - delta) * ATT_SCALE).astype(BF16)
            dk_acc[...] += _dot_tn(ds_b, qv)
            dq_part = _dot(ds_b, kv)
            rows = pl.ds(pl.multiple_of(i * t, t), t)

            @pl.when(j == 0)
            def _():
                dq_ref[0, rows, :] = dq_part

            @pl.when(j > 0)
            def _():
                dq_ref[0, rows, :] += dq_part

        @pl.when(i > j)
        def _():
            step(False)

        @pl.when(i == j)
        def _():
            step(True)

        @pl.when(i == nb - 1)
        def _():
            dk_ref[0] = dk_acc[...]
            dv_ref[...] = dv_acc[...]

    qi = lambda h, j, i: jnp.maximum(i, j)
    return pl.pallas_call(
        body, name=name, grid=(nh, nb, nb),
        in_specs=[pl.BlockSpec((1, t, QK_PAD), lambda h, j, i: (h, qi(h, j, i), 0)),
                  pl.BlockSpec((1, t, QK_PAD), lambda h, j, i: (h, j, 0)),
                  pl.BlockSpec((t, MLA_V), lambda h, j, i: (j, nh + h)),
                  pl.BlockSpec((t, MLA_V), lambda h, j, i: (qi(h, j, i), h)),
                  pl.BlockSpec((1, t, 1), lambda h, j, i: (h, qi(h, j, i), 0)),
                  pl.BlockSpec((t, MLA_V), lambda h, j, i: (qi(h, j, i), h))],
        out_specs=[pl.BlockSpec((1, s, QK_PAD), lambda h, j, i: (h, 0, 0)),
                   pl.BlockSpec((1, t, QK_PAD), lambda h, j, i: (h, j, 0)),
                   pl.BlockSpec((t, MLA_V), lambda h, j, i: (j, h))],
        out_shape=[jax.ShapeDtypeStruct((nh, s, QK_PAD), F32), jax.ShapeDtypeStruct((nh, s, QK_PAD), F32),
                   jax.ShapeDtypeStruct((s, nh * MLA_V), F32)],
        scratch_shapes=[pltpu.VMEM((t, QK_PAD), F32), pltpu.VMEM((t, MLA_V), F32)],
        compiler_params=_cp("parallel", "arbitrary", "arbitrary"),
    )(q, k, kvraw, o, lse, do)


def _ffn_fwd(h, w, tag):
    n = _rms_fwd(h, w["ln"], name=tag + "_norm")
    act, gate, up = _ffn_up(n, w["w13"], name=tag + "_up")
    out = _matmul(act, w["w2"], "nn", name=tag + "_down", scale=0.5, res=h)
    return out, (h, n, gate, up, act)


def _ffn_bwd(dout, saved, w, tag):
    h, n, gate, up, act = saved
    dact = _matmul(dout, w["w2"], "nt", name=tag + "_down_dx", scale=0.5, out_dtype=BF16)
    dw2 = _matmul(act, dout, "tn", name=tag + "_down_dw", scale=0.5)
    dgu = _swiglu_bwd(gate, up, dact, name=tag + "_act_bwd")
    dw13 = _matmul(n, dgu, "tn", name=tag + "_up_dw")
    dn = _matmul(dgu, w["w13"], "nt", name=tag + "_up_dx")
    dh, dln = _rms_bwd(h, w["ln"], dn, name=tag + "_norm_bwd", res=dout)
    return dh, dict(ln=dln, w13=dw13, w2=dw2)


def _mixer_fwd(h, w, rope, tag):
    cos4, sin4 = rope
    u = _rms_fwd(h, w["ln_mix"], name=tag + "_norm")
    proj = _matmul(u, w["w_in"], "nn", name=tag + "_in")
    xc = _conv_fwd(proj, w["conv_w"], w["conv_b"], name=tag + "_conv")
    dtr = proj[:, OFF_KRDT + MLA_ROPE:OFF_KRDT + MLA_ROPE + SSD_HEADS]
    dtrT = dtr.T
    y, states = _ssd_fwd(xc, dtr, dtrT, *w["ssd_aux"], name=tag + "_ssd")
    yn = _gated_rms_fwd(y, proj, w["ssd_norm"], name=tag + "_ssd_norm")
    y_ssd = _matmul(yn, w["w_ssd_out"], "nn", name=tag + "_ssd_out")
    cqn = _rms_fwd(proj, w["q_lora_norm"], name=tag + "_q_lora_norm", col=OFF_CQ // MLA_Q_LORA, width=MLA_Q_LORA)
    qraw = _matmul(cqn, w["w_uq"], "nn", name=tag + "_uq")
    ckvn = _rms_fwd(proj, w["kv_lora_norm"], name=tag + "_kv_lora_norm", col=OFF_CKV // MLA_KV_LORA, width=MLA_KV_LORA)
    kvraw = _matmul(ckvn, w["w_ukv"], "nn", name=tag + "_ukv")
    qf, kf = _mla_prep_fwd(qraw, kvraw, proj, cos4, sin4, *w["qk_gains"], name=tag + "_qk_prep")
    o, lse = _attn_fwd(qf, kf, kvraw, name=tag + "_attn")
    y_mla = _matmul(o, w["w_mla_out"], "nn", name=tag + "_mla_out")
    merged = _merge_fwd(proj, y_ssd, y_mla, name=tag + "_merge")
    out = _matmul(merged, w["w_o"], "nn", name=tag + "_o", res=h)
    saved = dict(h=h, u=u, proj=proj, xc=xc, dtr=dtr, dtrT=dtrT, states=states, y=y, yn=yn, y_ssd=y_ssd, cqn=cqn,
                 qraw=qraw, ckvn=ckvn, kvraw=kvraw, qf=qf, kf=kf, o=o, lse=lse, y_mla=y_mla, merged=merged)
    return out, saved


def _mixer_bwd(dout, s, w, rope, tag):
    cos4, sin4 = rope
    g = {}
    proj = s["proj"]
    dmerged = _matmul(dout, w["w_o"], "nt", name=tag + "_o_dx")
    g["w_o"] = _matmul(s["merged"], dout, "tn", name=tag + "_o_dw")
    dgates, dy_ssd, dy_mla = _merge_bwd(proj, s["y_ssd"], s["y_mla"], dmerged, name=tag + "_merge_bwd")
    do = _matmul(dy_mla, w["w_mla_out"], "nt", name=tag + "_mla_out_dx")
    g["w_mla_out"] = _matmul(s["o"], dy_mla, "tn", name=tag + "_mla_out_dw")
    dqf, dkf, dv = _attn_bwd(s["qf"], s["kf"], s["kvraw"], s["o"], s["lse"], do, name=tag + "_attn_bwd")
    dqraw, dkn, dkrt, dgqn, dgqr, dgkn, dgkr = _mla_prep_bwd(
        s["qraw"], s["kvraw"], proj, cos4, sin4, *w["qk_gains"], dqf, dkf, name=tag + "_qk_prep_bwd")
    g["q_norm"] = jnp.concatenate([dgqn[0], dgqr[0, :MLA_ROPE]])
    g["k_norm"] = jnp.concatenate([dgkn[0], dgkr[0, :MLA_ROPE]])
    dkvraw = jnp.concatenate([dkn, dv], axis=1).astype(BF16)
    dcqn = _matmul(dqraw, w["w_uq"], "nt", name=tag + "_uq_dx")
    g["w_uq"] = _matmul(s["cqn"], dqraw, "tn", name=tag + "_uq_dw")
    dckvn = _matmul(dkvraw, w["w_ukv"], "nt", name=tag + "_ukv_dx")
    g["w_ukv"] = _matmul(s["ckvn"], dkvraw, "tn", name=tag + "_ukv_dw")
    dcq, g["q_lora_norm"] = _rms_bwd(proj, w["q_lora_norm"], dcqn, name=tag + "_q_lora_norm_bwd",
                                     col=OFF_CQ // MLA_Q_LORA, width=MLA_Q_LORA)
    dckv, g["kv_lora_norm"] = _rms_bwd(proj, w["kv_lora_norm"], dckvn, name=tag + "_kv_lora_norm_bwd",
                                       col=OFF_CKV // MLA_KV_LORA, width=MLA_KV_LORA)
    dyn = _matmul(dy_ssd, w["w_ssd_out"], "nt", name=tag + "_ssd_out_dx")
    g["w_ssd_out"] = _matmul(s["yn"], dy_ssd, "tn", name=tag + "_ssd_out_dw")
    dy, dz, g["ssd_norm"] = _gated_rms_bwd(s["y"], proj, w["ssd_norm"], dyn, name=tag + "_ssd_norm_bwd")
    aux = w["ssd_aux"]
    dxc, ddt_c, ddt_r, dbias_c, dbias_r, da_c, da_r, ddsk = _ssd_bwd(
        s["xc"], s["dtr"], s["dtrT"], *aux, aux[-1].T, s["states"], dy, name=tag + "_ssd_bwd")
    g["dt_bias"] = dbias_c[0] + dbias_r[:, 0]
    g["a_log"] = (da_c[0] + da_r[:, 0]) * (-jnp.exp(aux[2][0]))
    g["d_skip"] = jnp.sum(ddsk.reshape(SSD_HEADS, SSD_HEAD_DIM), axis=1)
    dpre, g["conv_w"], g["conv_b"] = _conv_bwd_pre(proj, w["conv_w"], w["conv_b"], dxc, name=tag + "_conv_bwd_pre")
    dxbc = _conv_bwd_x(dpre, w["conv_w"], name=tag + "_conv_bwd_x")
    ddtr = ddt_c + ddt_r.T
    dkrdt = jnp.concatenate([dkrt[:, :MLA_ROPE], ddtr, jnp.zeros((ddtr.shape[0], LANE - MLA_ROPE - SSD_HEADS), F32)], axis=1)
    dproj = jnp.concatenate([dz, dxbc, dgates, dcq, dckv, dkrdt], axis=1).astype(BF16)
    du = _matmul(dproj, w["w_in"], "nt", name=tag + "_in_dx")
    g["w_in"] = _matmul(s["u"], dproj, "tn", name=tag + "_in_dw")
    dh, g["ln_mix"] = _rms_bwd(s["h"], w["ln_mix"], du, name=tag + "_norm_bwd", res=dout)
    return dh, g


W_NAMES = ["ln_ffn1", "ffn1_w13", "ffn1_w2", "ln_mix", "w_in", "conv_w", "conv_b", "dt_bias", "a_log", "d_skip",
           "ssd_norm", "w_ssd_out", "q_lora_norm", "w_uq", "kv_lora_norm", "w_ukv", "q_norm", "k_norm", "w_mla_out",
           "w_o", "ln_ffn2", "ffn2_w13", "ffn2_w2"]
SHARD_AXIS = {"ffn1_w13": 2, "ffn1_w2": 1, "w_in": 2, "conv_w": 2, "w_ssd_out": 1, "w_uq": 2, "w_ukv": 2,
              "w_mla_out": 1, "w_o": 1, "ffn2_w13": 2, "ffn2_w2": 1}
SHARDED = [n for n in W_NAMES if n in SHARD_AXIS]
REPLICATED = [n for n in W_NAMES if n not in SHARD_AXIS]
N_CHIPS = 4
N_DEV = 8
PACK_COLS = 1024
IN_SPLIT = (2048, 3072, 32, 512, 256, 64, 2048)


def _pack(arrs, rows, dtype):
    flat = jnp.concatenate([a.astype(dtype).reshape(-1) for a in arrs])
    return jnp.pad(flat, (0, rows * PACK_COLS - flat.shape[0])).reshape(rows, PACK_COLS)


def _unpack(packed, shapes):
    flat = packed.reshape(-1)
    out, at = [], 0
    for sh in shapes:
        n = math.prod(sh)
        out.append(flat[at:at + n].reshape(sh))
        at += n
    return out


def _pack_rows(shapes):
    n = sum(math.prod(sh) for sh in shapes)
    return -(-n // (PACK_COLS * 1024)) * 1024


def _in_perm(w_in):
    z, xbc, dt, cq, ckv, kr, gates = jnp.split(w_in, list(np_cumsum(IN_SPLIT))[:-1], axis=1)
    return jnp.concatenate([z, xbc, gates, cq, ckv, kr, dt, jnp.zeros((w_in.shape[0], PROJ_W - sum(IN_SPLIT)), w_in.dtype)], axis=1)


def _in_unperm(g):
    z, xbc, gates, cq, ckv = (g[:, OFF_Z:OFF_XBC], g[:, OFF_XBC:OFF_GATES], g[:, OFF_GATES:OFF_CQ], g[:, OFF_CQ:OFF_CKV],
                              g[:, OFF_CKV:OFF_KRDT])
    kr = g[:, OFF_KRDT:OFF_KRDT + MLA_ROPE]
    dt = g[:, OFF_KRDT + MLA_ROPE:OFF_KRDT + MLA_ROPE + SSD_HEADS]
    return jnp.concatenate([z, xbc, dt, cq, ckv, kr, gates], axis=1)


def np_cumsum(sizes):
    out, t = [], 0
    for s in sizes:
        t += s
        out.append(t)
    return out


def _head_perm(w, first):
    r = w.shape[0]
    w3 = w.reshape(r, MLA_HEADS, -1)
    return jnp.concatenate([w3[:, :, :first].reshape(r, -1), w3[:, :, first:].reshape(r, -1)], axis=1)


def _head_unperm(g, first):
    r = g.shape[0]
    rest = g.shape[1] // MLA_HEADS - first
    a = g[:, :MLA_HEADS * first].reshape(r, MLA_HEADS, first)
    b = g[:, MLA_HEADS * first:].reshape(r, MLA_HEADS, rest)
    return jnp.concatenate([a, b], axis=2).reshape(r, -1)


def _layer_weights(full, l):
    row = lambda n: full[n][l][None].astype(F32)
    expand = jnp.repeat(jnp.eye(SSD_HEADS, dtype=F32), SSD_HEAD_DIM, axis=1)
    dtb, al, dsk = full["dt_bias"][l], full["a_log"][l], full["d_skip"][l]
    mixer = dict(
        ln_mix=row("ln_mix"), w_in=_in_perm(full["w_in"][l]), conv_w=full["conv_w"][l], conv_b=row("conv_b"),
        ssd_aux=(dtb[None], dtb[:, None], al[None], al[:, None], jnp.repeat(dsk, SSD_HEAD_DIM)[None], expand),
        ssd_norm=row("ssd_norm"), w_ssd_out=full["w_ssd_out"][l],
        q_lora_norm=row("q_lora_norm"), w_uq=_head_perm(full["w_uq"][l], MLA_NOPE),
        kv_lora_norm=row("kv_lora_norm"), w_ukv=_head_perm(full["w_ukv"][l], MLA_NOPE),
        qk_gains=_mla_gains(full["q_norm"][l], full["k_norm"][l]),
        w_mla_out=full["w_mla_out"][l], w_o=full["w_o"][l])
    ffn1 = dict(ln=row("ln_ffn1"), w13=full["ffn1_w13"][l], w2=full["ffn1_w2"][l])
    ffn2 = dict(ln=row("ln_ffn2"), w13=full["ffn2_w13"][l], w2=full["ffn2_w2"][l])
    return ffn1, mixer, ffn2


def _layer_grads(g1, gm, g2):
    return {
        "ln_ffn1": g1["ln"][0], "ffn1_w13": g1["w13"], "ffn1_w2": g1["w2"],
        "ln_mix": gm["ln_mix"][0], "w_in": _in_unperm(gm["w_in"]), "conv_w": gm["conv_w"], "conv_b": gm["conv_b"][0],
        "dt_bias": gm["dt_bias"], "a_log": gm["a_log"], "d_skip": gm["d_skip"], "ssd_norm": gm["ssd_norm"][0],
        "w_ssd_out": gm["w_ssd_out"], "q_lora_norm": gm["q_lora_norm"][0], "w_uq": _head_unperm(gm["w_uq"], MLA_NOPE),
        "kv_lora_norm": gm["kv_lora_norm"][0], "w_ukv": _head_unperm(gm["w_ukv"], MLA_NOPE),
        "q_norm": gm["q_norm"], "k_norm": gm["k_norm"], "w_mla_out": gm["w_mla_out"], "w_o": gm["w_o"],
        "ln_ffn2": g2["ln"][0], "ffn2_w13": g2["w13"], "ffn2_w2": g2["w2"],
    }


def _local_step(x, positions, loss_target, full):
    rope = _rope_tables4(positions)
    lw = [_layer_weights(full, l) for l in range(DEPTH)]
    h = x
    saved = []
    for l in range(DEPTH):
        f1, mx, f2 = lw[l]
        h, s1 = _ffn_fwd(h, f1, f"l{l}_ffn1")
        h, sm = _mixer_fwd(h, mx, rope, f"l{l}_mix")
        h, s2 = _ffn_fwd(h, f2, f"l{l}_ffn2")
        saved.append((s1, sm, s2))
    loss_part, dh = _loss_fwd_bwd(h, loss_target, name="loss")
    grads = [None] * DEPTH
    for l in reversed(range(DEPTH)):
        f1, mx, f2 = lw[l]
        s1, sm, s2 = saved[l]
        dh, g2 = _ffn_bwd(dh, s2, f2, f"l{l}_ffn2")
        dh, gm = _mixer_bwd(dh, sm, mx, rope, f"l{l}_mix")
        dh, g1 = _ffn_bwd(dh, s1, f1, f"l{l}_ffn1")
        grads[l] = _layer_grads(g1, gm, g2)
    full_grads = {n: jnp.stack([grads[l][n] for l in range(DEPTH)]) for n in W_NAMES}
    return loss_part, dh, full_grads


MESH = pl.DeviceIdType.MESH
ANY = pl.BlockSpec(memory_space=pl.ANY)


def _place():
    return lax.axis_index("x"), lax.axis_index("y"), lax.axis_index("c")


def _other_chips(x, y):
    return [(1 - x, y), (x, 1 - y), (1 - x, 1 - y)]


def _remote(src, dst, send_sems, recv_sems, k, to):
    return pltpu.make_async_remote_copy(src_ref=src, dst_ref=dst, send_sem=send_sems.at[k], recv_sem=recv_sems.at[k],
                                        device_id=to, device_id_type=MESH)


N_PARTS = 8


def _parts(rows):
    size = rows // N_PARTS
    assert size * N_PARTS == rows and size % 16 == 0, rows
    return [(p * size, size) for p in range(N_PARTS)]


def _rows(ref, lead, base, start, size):
    return ref.at[(*lead, pl.ds(pl.multiple_of(base + start, 16), size), slice(None))]


def _my_chip():
    return 2 * lax.axis_index("x") + lax.axis_index("y")


def _own_slot(packed, *, name):
    r, ncol = packed.shape
    tr = _pick(r, 512, 16)

    def body(x_ref, o_ref):
        o_ref[...] = x_ref[...]

    return pl.pallas_call(
        body, name=name, grid=(r // tr,),
        in_specs=[pl.BlockSpec((tr, ncol), lambda i: (i, 0))],
        out_specs=pl.BlockSpec((None, tr, ncol), lambda i: (_my_chip(), i, 0)),
        out_shape=jax.ShapeDtypeStruct((N_CHIPS, r, ncol), packed.dtype),
        compiler_params=_cp("arbitrary"),
    )(packed)


def _gather_shards(packed, slots, *, name):
    r, ncol = packed.shape
    hr = r // 2
    parts = _parts(hr)

    def body(x_ref, slots_ref, out_ref, send_sems, recv_sems):
        del slots_ref
        x, y, c = _place()
        chips = _other_chips(x, y)
        me = 2 * x + y

        def half(chip, cc):
            return _rows(out_ref, (2 * chip[0] + chip[1],), cc * hr, 0, hr)

        for j, chip in enumerate(chips):
            for st, sz in parts:
                _remote(_rows(x_ref, (), c * hr, st, sz), _rows(out_ref, (me,), c * hr, st, sz), send_sems, recv_sems, j,
                        (*chip, c)).start()
        for j, chip in enumerate(chips):
            _remote(half(chip, c), half(chip, c), send_sems, recv_sems, j, (x, y, c)).wait_recv()
            slot = 2 * chip[0] + chip[1]
            for st, sz in parts:
                _remote(_rows(out_ref, (slot,), c * hr, st, sz), _rows(out_ref, (slot,), c * hr, st, sz), send_sems,
                        recv_sems, 3 + j, (x, y, 1 - c)).start()
        for j, chip in enumerate(chips):
            _remote(half(chip, 1 - c), half(chip, 1 - c), send_sems, recv_sems, 3 + j, (x, y, c)).wait_recv()
        for k in range(6):
            _remote(half((x, y), c), half((x, y), c), send_sems, recv_sems, k, (x, y, c)).wait_send()

    return pl.pallas_call(
        body, name=name,
        out_shape=jax.ShapeDtypeStruct((N_CHIPS, r, ncol), packed.dtype),
        in_specs=[ANY, ANY], out_specs=ANY, input_output_aliases={1: 0},
        scratch_shapes=[pltpu.SemaphoreType.DMA((6,)), pltpu.SemaphoreType.DMA((6,))],
    )(packed, slots)


def _swap_halves(g, *, name):
    n, r, ncol = g.shape
    hr = r // 2
    parts = _parts(hr)

    def body(g_ref, got_ref, send_sems, recv_sems):
        x, y, c = _place()
        for s in range(n):
            for st, sz in parts:
                _remote(_rows(g_ref, (s,), (1 - c) * hr, st, sz), got_ref.at[s, pl.ds(st, sz), :], send_sems, recv_sems, 0,
                        (x, y, 1 - c)).start()
        _remote(got_ref, got_ref, send_sems, recv_sems, 0, (x, y, c)).wait()

    return pl.pallas_call(
        body, name=name, out_shape=jax.ShapeDtypeStruct((n, hr, ncol), g.dtype), in_specs=[ANY], out_specs=ANY,
        scratch_shapes=[pltpu.SemaphoreType.DMA((1,)), pltpu.SemaphoreType.DMA((1,))],
    )(g)


def _add_cores(g, got, *, name):
    n, r, ncol = g.shape
    hr = r // 2
    tr = _pick(hr, 512, 16)
    nb = hr // tr

    def body(a_ref, b_ref, o_ref):
        o_ref[...] = (a_ref[...] + b_ref[...]).astype(o_ref.dtype)

    blk = pl.BlockSpec((None, tr, ncol), lambda s, i: (s, i, 0))
    return pl.pallas_call(
        body, name=name, grid=(n, nb),
        in_specs=[pl.BlockSpec((None, tr, ncol), lambda s, i: (s, lax.axis_index("c") * nb + i, 0)), blk],
        out_specs=blk,
        out_shape=jax.ShapeDtypeStruct((n, hr, ncol), BF16),
        compiler_params=_cp("parallel", "parallel"),
    )(g, got)


def _scatter_to_chips(a, *, name):
    n, r, ncol = a.shape
    parts = _parts(r)

    def body(a_ref, got_ref, send_sems, recv_sems):
        x, y, c = _place()
        for st, sz in parts:
            for j, chip in enumerate(_other_chips(x, y)):
                _remote(a_ref.at[2 * chip[0] + chip[1], pl.ds(st, sz), :], got_ref.at[j, pl.ds(st, sz), :], send_sems,
                        recv_sems, j, (*chip, c)).start()
        for j in range(n - 1):
            _remote(got_ref.at[j], got_ref.at[j], send_sems, recv_sems, j, (x, y, c)).wait()

    return pl.pallas_call(
        body, name=name, out_shape=jax.ShapeDtypeStruct((n - 1, r, ncol), a.dtype), in_specs=[ANY], out_specs=ANY,
        scratch_shapes=[pltpu.SemaphoreType.DMA((3,)), pltpu.SemaphoreType.DMA((3,))],
    )(a)


def _add_chips(a, got, *, name):
    n, hr, ncol = a.shape
    tr = _pick(hr, 512, 16)
    nb = hr // tr

    def body(a_ref, g0_ref, g1_ref, g2_ref, o_ref):
        f = lambda ref: ref[...].astype(F32)
        o_ref[...] = ((f(a_ref) + f(g0_ref)) + f(g1_ref)) + f(g2_ref)

    other = lambda j: pl.BlockSpec((None, tr, ncol), lambda i: (j, i, 0))
    return pl.pallas_call(
        body, name=name, grid=(nb,),
        in_specs=[pl.BlockSpec((None, tr, ncol), lambda i: (_my_chip(), i, 0)), other(0), other(1), other(2)],
        out_specs=pl.BlockSpec((tr, ncol), lambda i: (lax.axis_index("c") * nb + i, 0)),
        out_shape=jax.ShapeDtypeStruct((2 * hr, ncol), F32),
        compiler_params=_cp("parallel"),
    )(a, got, got, got)


def _join_halves(buf, *, name):
    r, ncol = buf.shape
    hr = r // 2
    parts = _parts(hr)

    def body(b_ref, out_ref, send_sems, recv_sems):
        del b_ref
        x, y, c = _place()
        for st, sz in parts:
            _remote(_rows(out_ref, (), c * hr, st, sz), _rows(out_ref, (), c * hr, st, sz), send_sems, recv_sems, 0,
                    (x, y, 1 - c)).start()
        theirs = _rows(out_ref, (), (1 - c) * hr, 0, hr)
        _remote(theirs, theirs, send_sems, recv_sems, 0, (x, y, c)).wait()

    return pl.pallas_call(
        body, name=name, out_shape=jax.ShapeDtypeStruct((r, ncol), buf.dtype), in_specs=[ANY], out_specs=ANY,
        input_output_aliases={0: 0},
        scratch_shapes=[pltpu.SemaphoreType.DMA((1,)), pltpu.SemaphoreType.DMA((1,))],
    )(buf)


def _reduce_scatter(g, *, name):
    got = _swap_halves(g, name=name + "_swap")
    chip_sum = _add_cores(g, got, name=name + "_add_cores")
    others = _scatter_to_chips(chip_sum, name=name + "_scatter")
    return _join_halves(_add_chips(chip_sum, others, name=name + "_add_chips"), name=name + "_join")


def _all_gather_small(v, *, name):
    r, ncol = v.shape

    def body(x_ref, out_ref, send_sems, recv_sems, local_sem):
        x, y, c = _place()
        me, sibling = (x, y, c), (x, y, 1 - c)
        chips = _other_chips(x, y)

        def slot(p):
            return out_ref.at[4 * p[0] + 2 * p[1] + p[2]]

        mine = pltpu.make_async_copy(x_ref, slot(me), local_sem.at[0])
        mine.start()
        first = [_remote(x_ref, slot(me), send_sems, recv_sems, 0, sibling)]
        first += [_remote(x_ref, slot(me), send_sems, recv_sems, 1 + j, (*chip, c)) for j, chip in enumerate(chips)]
        for cp in first:
            cp.start()
        passed = [_remote(slot((*chip, c)), slot((*chip, c)), send_sems, recv_sems, 4 + j, sibling)
                  for j, chip in enumerate(chips)]
        for j, chip in enumerate(chips):
            _remote(slot((*chip, c)), slot((*chip, c)), send_sems, recv_sems, 1 + j, me).wait_recv()
            passed[j].start()
        _remote(slot(sibling), slot(sibling), send_sems, recv_sems, 0, me).wait_recv()
        for j, chip in enumerate(chips):
            _remote(slot((*chip, 1 - c)), slot((*chip, 1 - c)), send_sems, recv_sems, 4 + j, me).wait_recv()
        for cp in first + passed:
            cp.wait_send()
        mine.wait()

    vm = pl.BlockSpec(memory_space=pltpu.VMEM)
    return pl.pallas_call(
        body, name=name, out_shape=jax.ShapeDtypeStruct((N_DEV, r, ncol), v.dtype), in_specs=[vm], out_specs=vm,
        scratch_shapes=[pltpu.SemaphoreType.DMA((7,)), pltpu.SemaphoreType.DMA((7,)), pltpu.SemaphoreType.DMA((1,))],
    )(v)


def _sum_slots(g8, *, name):
    n, r, ncol = g8.shape

    def body(g_ref, o_ref):
        acc = g_ref[0]
        for k in range(1, n):
            acc = acc + g_ref[k]
        o_ref[...] = acc

    return pl.pallas_call(body, name=name, out_shape=jax.ShapeDtypeStruct((r, ncol), g8.dtype))(g8)


def _step(a):
    x = a["x"][0]
    s = x.shape[0]
    del s
    shard_shapes = [a[n].shape for n in SHARDED]
    rows = _pack_rows(shard_shapes)

    packed = _pack([a[n] for n in SHARDED], rows, BF16)
    gathered = _gather_shards(packed, _own_slot(packed, name="own_weights"), name="gather_weights")
    conv_rows = -(-math.prod(a["conv_w"].shape) // (LANE * 8)) * 8
    conv_all = _all_gather_small(
        jnp.pad(a["conv_w"].reshape(-1), (0, conv_rows * LANE - math.prod(a["conv_w"].shape))).reshape(conv_rows, LANE),
        name="gather_conv_w")
    per_chip = [dict(zip(SHARDED, _unpack(gathered[k], shard_shapes))) for k in range(N_CHIPS)]
    full = {n: jnp.concatenate([per_chip[k][n] for k in range(N_CHIPS)], axis=SHARD_AXIS[n]) for n in SHARDED}
    full["conv_w"] = jnp.concatenate(
        [conv_all[2 * k].reshape(-1)[:math.prod(a["conv_w"].shape)].reshape(a["conv_w"].shape) for k in range(N_CHIPS)],
        axis=SHARD_AXIS["conv_w"])
    for n in REPLICATED:
        full[n] = a[n]

    loss_part, grad_x, grads = _local_step(x, a["positions"][0], a["loss_target"][0], full)
    loss = lax.psum(jnp.sum(loss_part), ("x", "y", "c"))

    slots = []
    for k in range(N_CHIPS):
        parts = [jnp.split(grads[n], N_CHIPS, axis=SHARD_AXIS[n])[k] for n in SHARDED]
        slots.append(_pack(parts, rows, F32))
    g_shard = _reduce_scatter(jnp.stack(slots), name="reduce_grads")

    rep_shapes = [a[n].shape for n in REPLICATED]
    n_rep = sum(math.prod(sh) for sh in rep_shapes)
    rep_rows = -(-n_rep // (LANE * 8)) * 8
    pack_small = lambda arrs: jnp.pad(jnp.concatenate([t.reshape(-1) for t in arrs]), (0, rep_rows * LANE - n_rep)).reshape(rep_rows, LANE)
    g_rep = _sum_slots(_all_gather_small(pack_small([grads[n] for n in REPLICATED]), name="gather_small_grads"),
                       name="add_small_grads")

    out = {"loss": loss, "grad_x": grad_x[None]}
    for n, g in zip(SHARDED, _unpack(g_shard, shard_shapes)):
        flat = lambda t: t.reshape(-1, t.shape[-1])
        d, nm, nv = _adamw(flat(a[n]), flat(g), flat(a["m_" + n]), flat(a["v_" + n]), name="adamw_" + n)
        out["grad_" + n] = g
        out["delta_" + n], out["new_m_" + n], out["new_v_" + n] = (t.reshape(g.shape) for t in (d, nm, nv))
    d_rp, m_rp, v_rp = _adamw(pack_small([a[n] for n in REPLICATED]), g_rep,
                              pack_small([a["m_" + n] for n in REPLICATED]),
                              pack_small([a["v_" + n] for n in REPLICATED]), name="adamw_replicated")
    for prefix, rp_arr in (("grad_", g_rep), ("delta_", d_rp), ("new_m_", m_rp), ("new_v_", v_rp)):
        for n, t in zip(REPLICATED, _unpack(rp_arr.reshape(-1)[:n_rep], rep_shapes)):
            out[prefix + n] = t
    return out


IN_NAMES = ["x", "positions"] + W_NAMES + ["loss_target"] + ["m_" + n for n in W_NAMES] + ["v_" + n for n in W_NAMES]
OUT_NAMES = (["loss", "grad_x"] + ["grad_" + n for n in W_NAMES] + ["delta_" + n for n in W_NAMES]
             + ["new_m_" + n for n in W_NAMES] + ["new_v_" + n for n in W_NAMES])


def kernel(x, positions, ln_ffn1, ffn1_w13, ffn1_w2, ln_mix, w_in, conv_w, conv_b, dt_bias, a_log, d_skip, ssd_norm, w_ssd_out, q_lora_norm, w_uq, kv_lora_norm, w_ukv, q_norm, k_norm, w_mla_out, w_o, ln_ffn2, ffn2_w13, ffn2_w2, loss_target, m_ln_ffn1, m_ffn1_w13, m_ffn1_w2, m_ln_mix, m_w_in, m_conv_w, m_conv_b, m_dt_bias, m_a_log, m_d_skip, m_ssd_norm, m_w_ssd_out, m_q_lora_norm, m_w_uq, m_kv_lora_norm, m_w_ukv, m_q_norm, m_k_norm, m_w_mla_out, m_w_o, m_ln_ffn2, m_ffn2_w13, m_ffn2_w2, v_ln_ffn1, v_ffn1_w13, v_ffn1_w2, v_ln_mix, v_w_in, v_conv_w, v_conv_b, v_dt_bias, v_a_log, v_d_skip, v_ssd_norm, v_w_ssd_out, v_q_lora_norm, v_w_uq, v_kv_lora_norm, v_w_ukv, v_q_norm, v_k_norm, v_w_mla_out, v_w_o, v_ln_ffn2, v_ffn2_w13, v_ffn2_w2):
    given = locals()
    out = _step({n: given[n] for n in IN_NAMES})
    return tuple(out[n] for n in OUT_NAMES)
```

```python
import functools
import math

import jax
import jax.numpy as jnp
from jax import lax
from jax.experimental import pallas as pl
from jax.experimental.pallas import tpu as pltpu

F32 = jnp.float32
BF16 = jnp.bfloat16

D_MODEL = 1024
DEPTH = 2
D_FF = 2816
SSD_D_INNER = 2048
SSD_HEADS = 32
SSD_HEAD_DIM = 64
SSD_GROUPS = 4
SSD_STATE = 128
SSD_CHUNK = 128
SSD_CONV = 4
SSD_CONV_DIM = 3072
MLA_HEADS = 8
MLA_Q_LORA = 512
MLA_KV_LORA = 256
MLA_NOPE = 128
MLA_ROPE = 64
MLA_V = 128
MLA_QK = 192
ROPE_THETA = 10000.0
EPS = 1e-6
ADAM_LR = 0.001
ADAM_B1 = 0.9
ADAM_B2 = 0.999
ADAM_EPS = 1e-08
ADAM_WD = 0.01
ADAM_STEP = 10

PROJ_W = 8064
OFF_Z, OFF_XBC, OFF_GATES, OFF_CQ, OFF_CKV, OFF_KRDT = 0, 2048, 5120, 7168, 7680, 7936

LANE = 128
VMEM_LIMIT = 48 * 1024 * 1024
HI = lax.Precision.HIGHEST


def _cp(*sem):
    return pltpu.CompilerParams(dimension_semantics=sem, vmem_limit_bytes=VMEM_LIMIT)


def _pick(dim, target, align):
    if dim <= target:
        return dim
    b = (target // align) * align
    while b >= align:
        if dim % b == 0:
            return b
        b -= align
    raise ValueError(f"no block for {dim} (target {target}, align {align})")


def _silu(x):
    return x * jax.nn.sigmoid(x)


def _dsilu(x):
    s = jax.nn.sigmoid(x)
    return s * (1.0 + x * (1.0 - s))


MM_VMEM_BUDGET = 40 * 1024 * 1024


def _mm_tiles(m, n, k, a_bytes, b_bytes, o_bytes):
    bn = _pick(n, 1408, LANE)
    for nk in (1, 2, 3, 4, 6, 7, 8):
        if k % nk or (k // nk) % LANE:
            continue
        bk = k // nk
        for bm in (1024, 512):
            if m % bm:
                continue
            need = 2 * (bm * bk * a_bytes + bk * bn * b_bytes + bm * bn * o_bytes) + (bm * bn * 4 if nk > 1 else 0)
            if need <= MM_VMEM_BUDGET:
                return bm, bn, bk
    return _pick(m, 512, 8), bn, _pick(k, 1536, LANE)

def _matmul(a, b, mode, *, name, out_dtype=F32, scale=1.0, res=None):
    if mode == "nn":
        (m, k), (k2, n) = a.shape, b.shape
    elif mode == "nt":
        (m, k), (n, k2) = a.shape, b.shape
    else:
        (k, m), (k2, n) = a.shape, b.shape
    assert k == k2, (a.shape, b.shape, mode)
    if mode == "tn":
        bn, bk = _pick(n, 2816, LANE), _pick(k, 1024, 8)
        bm = _pick(m, max(256, (1408 * 1024 // bn) // LANE * LANE), LANE)
    else:
        bm, bn, bk = _mm_tiles(m, n, k, a.dtype.itemsize, b.dtype.itemsize,
                               jnp.dtype(out_dtype).itemsize + (4 if res is not None else 0))
    nk = k // bk

    def body(a_ref, b_ref, *rest):
        res_ref = rest[0] if res is not None else None
        o_ref = rest[-2] if nk > 1 else rest[-1]
        kk = pl.program_id(2)
        av = a_ref[...].astype(BF16)
        bv = b_ref[...].astype(BF16)
        if mode == "nn":
            dims = (((1,), (0,)), ((), ()))
        elif mode == "nt":
            dims = (((1,), (1,)), ((), ()))
        else:
            dims = (((0,), (0,)), ((), ()))
        part = lax.dot_general(av, bv, dims, preferred_element_type=F32)

        def finish(total):
            out = total * scale
            if res_ref is not None:
                out = res_ref[...] + out
            o_ref[...] = out.astype(o_ref.dtype)

        if nk == 1:
            finish(part)
            return
        acc_ref = rest[-1]

        @pl.when(kk == 0)
        def _():
            acc_ref[...] = part

        @pl.when((kk > 0) & (kk < nk - 1))
        def _():
            acc_ref[...] += part

        @pl.when(kk == nk - 1)
        def _():
            finish(acc_ref[...] + part)

    o_spec = pl.BlockSpec((bm, bn), lambda i, j, kk: (i, j))
    if mode == "nn":
        a_spec = pl.BlockSpec((bm, bk), lambda i, j, kk: (i, kk))
        b_spec = pl.BlockSpec((bk, bn), lambda i, j, kk: (kk, j))
    elif mode == "nt":
        a_spec = pl.BlockSpec((bm, bk), lambda i, j, kk: (i, kk))
        b_spec = pl.BlockSpec((bn, bk), lambda i, j, kk: (j, kk))
    else:
        a_spec = pl.BlockSpec((bk, bm), lambda i, j, kk: (kk, i))
        b_spec = pl.BlockSpec((bk, bn), lambda i, j, kk: (kk, j))
    return pl.pallas_call(
        body, name=name,
        grid=(m // bm, n // bn, nk),
        in_specs=[a_spec, b_spec] + ([o_spec] if res is not None else []),
        out_specs=o_spec,
        out_shape=jax.ShapeDtypeStruct((m, n), out_dtype),
        scratch_shapes=[pltpu.VMEM((bm, bn), F32)] if nk > 1 else [],
        compiler_params=_cp("parallel", "parallel", "arbitrary"),
    )(*((a, b) + ((res,) if res is not None else ())))


def _rms_fwd(x, g, *, name, col=0, width=None):
    r = x.shape[0]
    w = width or x.shape[1]
    tr = _pick(r, 512, 16)

    def body(x_ref, g_ref, o_ref):
        xv = x_ref[...]
        rs = lax.rsqrt(jnp.mean(xv * xv, axis=-1, keepdims=True) + EPS)
        o_ref[...] = (xv * rs * g_ref[...]).astype(o_ref.dtype)

    return pl.pallas_call(
        body, name=name, grid=(r // tr,),
        in_specs=[pl.BlockSpec((tr, w), lambda i: (i, col)), pl.BlockSpec((1, w), lambda i: (0, 0))],
        out_specs=pl.BlockSpec((tr, w), lambda i: (i, 0)),
        out_shape=jax.ShapeDtypeStruct((r, w), BF16),
        compiler_params=_cp("parallel"),
    )(x, g)


def _rms_bwd(x, g, dy, *, name, col=0, width=None, res=None):
    r = x.shape[0]
    w = width or x.shape[1]
    tr = _pick(r, 512, 8)

    def body(x_ref, g_ref, dy_ref, *rest):
        res_ref = rest[0] if res is not None else None
        dx_ref, dg_ref = rest[-2:]
        i = pl.program_id(0)
        xv = x_ref[...]
        dyv = dy_ref[...]
        rs = lax.rsqrt(jnp.mean(xv * xv, axis=-1, keepdims=True) + EPS)
        xh = xv * rs
        dxh = dyv * g_ref[...]
        mm = jnp.mean(dxh * xh, axis=-1, keepdims=True)
        dx = rs * (dxh - xh * mm)
        if res_ref is not None:
            dx = res_ref[...] + dx
        dx_ref[...] = dx
        part = jnp.sum(dyv * xh, axis=0, keepdims=True)

        @pl.when(i == 0)
        def _():
            dg_ref[...] = part

        @pl.when(i > 0)
        def _():
            dg_ref[...] += part

    blk = pl.BlockSpec((tr, w), lambda i: (i, 0))
    return pl.pallas_call(
        body, name=name, grid=(r // tr,),
        in_specs=[pl.BlockSpec((tr, w), lambda i: (i, col)), pl.BlockSpec((1, w), lambda i: (0, 0)), blk]
        + ([blk] if res is not None else []),
        out_specs=[blk, pl.BlockSpec((1, w), lambda i: (0, 0))],
        out_shape=[jax.ShapeDtypeStruct((r, w), F32), jax.ShapeDtypeStruct((1, w), F32)],
        compiler_params=_cp("arbitrary"),
    )(*((x, g, dy) + ((res,) if res is not None else ())))


def _gated_rms_fwd(y, proj, g, *, name):
    r, w = y.shape
    tr = _pick(r, 256, 8)

    def body(y_ref, z_ref, g_ref, o_ref):
        t = y_ref[...] * _silu(z_ref[...])
        rs = lax.rsqrt(jnp.mean(t * t, axis=-1, keepdims=True) + EPS)
        o_ref[...] = (t * rs * g_ref[...]).astype(o_ref.dtype)

    return pl.pallas_call(
        body, name=name, grid=(r // tr,),
        in_specs=[pl.BlockSpec((tr, w), lambda i: (i, 0)), pl.BlockSpec((tr, w), lambda i: (i, OFF_Z // w)),
                  pl.BlockSpec((1, w), lambda i: (0, 0))],
        out_specs=pl.BlockSpec((tr, w), lambda i: (i, 0)),
        out_shape=jax.ShapeDtypeStruct((r, w), BF16),
        compiler_params=_cp("parallel"),
    )(y, proj, g)


def _gated_rms_bwd(y, proj, g, do, *, name):
    r, w = y.shape
    tr = _pick(r, 256, 8)

    def body(y_ref, z_ref, g_ref, do_ref, dy_ref, dz_ref, dg_ref):
        i = pl.program_id(0)
        yv, zv, dov = y_ref[...], z_ref[...], do_ref[...]
        sz = _silu(zv)
        t = yv * sz
        rs = lax.rsqrt(jnp.mean(t * t, axis=-1, keepdims=True) + EPS)
        th = t * rs
        dth = dov * g_ref[...]
        mm = jnp.mean(dth * th, axis=-1, keepdims=True)
        dt = rs * (dth - th * mm)
        dy_ref[...] = dt * sz
        dz_ref[...] = dt * yv * _dsilu(zv)
        part = jnp.sum(dov * th, axis=0, keepdims=True)

        @pl.when(i == 0)
        def _():
            dg_ref[...] = part

        @pl.when(i > 0)
        def _():
            dg_ref[...] += part

    blk = pl.BlockSpec((tr, w), lambda i: (i, 0))
    vec = pl.BlockSpec((1, w), lambda i: (0, 0))
    return pl.pallas_call(
        body, name=name, grid=(r // tr,),
        in_specs=[blk, pl.BlockSpec((tr, w), lambda i: (i, OFF_Z // w)), vec, blk],
        out_specs=[blk, blk, vec],
        out_shape=[jax.ShapeDtypeStruct((r, w), F32), jax.ShapeDtypeStruct((r, w), F32),
                   jax.ShapeDtypeStruct((1, w), F32)],
        compiler_params=_cp("arbitrary"),
    )(y, proj, g, do)


def _ffn_up(n, w13, *, name):
    m, k = n.shape
    f = w13.shape[1] // 2
    bm, bn = _pick(m, 1024, 16), _pick(f, 1408, LANE)
    nj = f // bn

    def body(a_ref, wg_ref, wu_ref, act_ref, g_ref, u_ref):
        a = a_ref[...].astype(BF16)
        g = _dot(a, wg_ref[...].astype(BF16))
        u = _dot(a, wu_ref[...].astype(BF16))
        act_ref[...] = (_silu(g) * u).astype(act_ref.dtype)
        g_ref[...] = g.astype(g_ref.dtype)
        u_ref[...] = u.astype(u_ref.dtype)

    out = pl.BlockSpec((bm, bn), lambda j, i: (i, j))
    sh = jax.ShapeDtypeStruct((m, f), BF16)
    return pl.pallas_call(
        body, name=name, grid=(nj, m // bm),
        in_specs=[pl.BlockSpec((bm, k), lambda j, i: (i, 0)), pl.BlockSpec((k, bn), lambda j, i: (0, j)),
                  pl.BlockSpec((k, bn), lambda j, i: (0, nj + j))],
        out_specs=[out, out, out], out_shape=[sh, sh, sh],
        compiler_params=_cp("parallel", "parallel"),
    )(n, w13, w13)


def _swiglu_bwd(g, u, da, *, name):
    r, f = g.shape
    tr = _pick(r, 256, 16)

    def body(g_ref, u_ref, da_ref, o_ref):
        gv, uv, dav = g_ref[...].astype(F32), u_ref[...].astype(F32), da_ref[...].astype(F32)
        o_ref[:, :f] = (dav * uv * _dsilu(gv)).astype(o_ref.dtype)
        o_ref[:, f:] = (dav * _silu(gv)).astype(o_ref.dtype)

    blk = pl.BlockSpec((tr, f), lambda i: (i, 0))
    return pl.pallas_call(
        body, name=name, grid=(r // tr,),
        in_specs=[blk, blk, blk],
        out_specs=pl.BlockSpec((tr, 2 * f), lambda i: (i, 0)),
        out_shape=jax.ShapeDtypeStruct((r, 2 * f), BF16),
        compiler_params=_cp("parallel"),
    )(g, u, da)


CONV_TS = 1024
CONV_TC = 512


def _conv_pre(x, carry, w_ref, b_ref):
    ts = x.shape[0]
    row8 = lax.broadcasted_iota(jnp.int32, (8, x.shape[1]), 0)
    head_x = x[0:8]
    shifted, shifted_head = [], []
    for j in range(SSD_CONV):
        if j == 0:
            shifted.append(x)
            shifted_head.append(head_x)
        else:
            shifted.append(pltpu.roll(x, j, 0))
            shifted_head.append(jnp.where(row8 < j, pltpu.roll(carry, j, 0), pltpu.roll(head_x, j, 0)))
    pre = b_ref[...] + sum(w_ref[SSD_CONV - 1 - j:SSD_CONV - j, :] * shifted[j] for j in range(SSD_CONV))
    pre_head = b_ref[...] + sum(w_ref[SSD_CONV - 1 - j:SSD_CONV - j, :] * shifted_head[j] for j in range(SSD_CONV))
    del ts
    return pre, pre_head, shifted, shifted_head


def _conv_fwd(proj, w, b, *, name):
    s = proj.shape[0]
    c = w.shape[1]
    ts, tc = _pick(s, CONV_TS, 8), CONV_TC
    off = OFF_XBC // tc

    def body(x_ref, w_ref, b_ref, o_ref, carry_ref):
        t = pl.program_id(1)

        @pl.when(t == 0)
        def _():
            carry_ref[...] = jnp.zeros_like(carry_ref)

        x = x_ref[...]
        pre, pre_head, _, _ = _conv_pre(x, carry_ref[...], w_ref, b_ref)
        o_ref[...] = _silu(pre)
        o_ref[0:8, :] = _silu(pre_head)
        carry_ref[...] = x[ts - 8:ts]

    return pl.pallas_call(
        body, name=name, grid=(c // tc, s // ts),
        in_specs=[pl.BlockSpec((ts, tc), lambda j, t: (t, j + off)), pl.BlockSpec((SSD_CONV, tc), lambda j, t: (0, j)),
                  pl.BlockSpec((1, tc), lambda j, t: (0, j))],
        out_specs=pl.BlockSpec((ts, tc), lambda j, t: (t, j)),
        out_shape=jax.ShapeDtypeStruct((s, c), F32),
        scratch_shapes=[pltpu.VMEM((8, tc), F32)],
        compiler_params=_cp("parallel", "arbitrary"),
    )(proj, w, b)


def _conv_bwd_pre(proj, w, b, dy, *, name):
    s = proj.shape[0]
    c = w.shape[1]
    ts, tc = _pick(s, CONV_TS, 8), CONV_TC
    off = OFF_XBC // tc

    def body(x_ref, w_ref, b_ref, dy_ref, dp_ref, dw_ref, db_ref, carry_ref):
        t = pl.program_id(1)

        @pl.when(t == 0)
        def _():
            carry_ref[...] = jnp.zeros_like(carry_ref)
            dw_ref[...] = jnp.zeros_like(dw_ref)
            db_ref[...] = jnp.zeros_like(db_ref)

        x = x_ref[...]
        pre, pre_head, shifted, shifted_head = _conv_pre(x, carry_ref[...], w_ref, b_ref)
        dyv = dy_ref[...]
        dp = dyv * _dsilu(pre)
        dp_head = dyv[0:8] * _dsilu(pre_head)
        row = lax.broadcasted_iota(jnp.int32, dp.shape, 0)
        dp_tail = jnp.where(row >= 8, dp, 0.0)
        dp_ref[...] = dp
        dp_ref[0:8, :] = dp_head
        db_ref[...] += jnp.sum(dp_tail, axis=0, keepdims=True) + jnp.sum(dp_head, axis=0, keepdims=True)
        for j in range(SSD_CONV):
            kk = SSD_CONV - 1 - j
            dw_ref[kk:kk + 1, :] += (jnp.sum(dp_tail * shifted[j], axis=0, keepdims=True)
                                     + jnp.sum(dp_head * shifted_head[j], axis=0, keepdims=True))
        carry_ref[...] = x[ts - 8:ts]

    return pl.pallas_call(
        body, name=name, grid=(c // tc, s // ts),
        in_specs=[pl.BlockSpec((ts, tc), lambda j, t: (t, j + off)), pl.BlockSpec((SSD_CONV, tc), lambda j, t: (0, j)),
                  pl.BlockSpec((1, tc), lambda j, t: (0, j)), pl.BlockSpec((ts, tc), lambda j, t: (t, j))],
        out_specs=[pl.BlockSpec((ts, tc), lambda j, t: (t, j)), pl.BlockSpec((SSD_CONV, tc), lambda j, t: (0, j)),
                   pl.BlockSpec((1, tc), lambda j, t: (0, j))],
        out_shape=[jax.ShapeDtypeStruct((s, c), F32), jax.ShapeDtypeStruct((SSD_CONV, c), F32),
                   jax.ShapeDtypeStruct((1, c), F32)],
        scratch_shapes=[pltpu.VMEM((8, tc), F32)],
        compiler_params=_cp("parallel", "arbitrary"),
    )(proj, w, b, dy)


def _conv_bwd_x(dp, w, *, name):
    s, c = dp.shape
    ts, tc = _pick(s, CONV_TS, 8), CONV_TC
    nt = s // ts

    def body(d_ref, w_ref, o_ref, carry_ref):
        t = pl.program_id(1)

        @pl.when(t == 0)
        def _():
            carry_ref[...] = jnp.zeros_like(carry_ref)

        d = d_ref[...]
        carry = carry_ref[...]
        row8 = lax.broadcasted_iota(jnp.int32, (8, tc), 0)
        tail = d[ts - 8:ts]
        acc = w_ref[SSD_CONV - 1:SSD_CONV, :] * d
        acc_tail = w_ref[SSD_CONV - 1:SSD_CONV, :] * tail
        for j in range(1, SSD_CONV):
            wj = w_ref[SSD_CONV - 1 - j:SSD_CONV - j, :]
            acc = acc + wj * pltpu.roll(d, ts - j, 0)
            up_tail = jnp.where(row8 >= 8 - j, pltpu.roll(carry, 8 - j, 0), pltpu.roll(tail, 8 - j, 0))
            acc_tail = acc_tail + wj * up_tail
        o_ref[...] = acc
        o_ref[ts - 8:ts, :] = acc_tail
        carry_ref[...] = d[0:8]

    return pl.pallas_call(
        body, name=name, grid=(c // tc, nt),
        in_specs=[pl.BlockSpec((ts, tc), lambda j, t: (nt - 1 - t, j)), pl.BlockSpec((SSD_CONV, tc), lambda j, t: (0, j))],
        out_specs=pl.BlockSpec((ts, tc), lambda j, t: (nt - 1 - t, j)),
        out_shape=jax.ShapeDtypeStruct((s, c), F32),
        scratch_shapes=[pltpu.VMEM((8, tc), F32)],
        compiler_params=_cp("parallel", "arbitrary"),
    )(dp, w)


def _merge_fwd(proj, ys, ym, *, name):
    r, w = ys.shape
    tr = _pick(r, 512, 8)
    off = OFF_GATES // w

    def body(g1_ref, g2_ref, ys_ref, ym_ref, o_ref):
        o_ref[...] = (jax.nn.sigmoid(g1_ref[...]) * ys_ref[...]
                      + jax.nn.sigmoid(g2_ref[...]) * ym_ref[...]).astype(o_ref.dtype)

    blk = pl.BlockSpec((tr, w), lambda i: (i, 0))
    return pl.pallas_call(
        body, name=name, grid=(r // tr,),
        in_specs=[pl.BlockSpec((tr, w), lambda i: (i, off)), pl.BlockSpec((tr, w), lambda i: (i, off + 1)), blk, blk],
        out_specs=blk, out_shape=jax.ShapeDtypeStruct((r, w), BF16),
        compiler_params=_cp("parallel"),
    )(proj, proj, ys, ym)


def _merge_bwd(proj, ys, ym, dm, *, name):
    r, w = ys.shape
    tr = _pick(r, 512, 8)
    off = OFF_GATES // w

    def body(g1_ref, g2_ref, ys_ref, ym_ref, dm_ref, dg_ref, dys_ref, dym_ref):
        s1, s2 = jax.nn.sigmoid(g1_ref[...]), jax.nn.sigmoid(g2_ref[...])
        dmv = dm_ref[...]
        dys_ref[...] = (dmv * s1).astype(dys_ref.dtype)
        dym_ref[...] = (dmv * s2).astype(dym_ref.dtype)
        dg_ref[:, :w] = dmv * ys_ref[...] * s1 * (1.0 - s1)
        dg_ref[:, w:] = dmv * ym_ref[...] * s2 * (1.0 - s2)

    blk = pl.BlockSpec((tr, w), lambda i: (i, 0))
    return pl.pallas_call(
        body, name=name, grid=(r // tr,),
        in_specs=[pl.BlockSpec((tr, w), lambda i: (i, off)), pl.BlockSpec((tr, w), lambda i: (i, off + 1)), blk, blk, blk],
        out_specs=[pl.BlockSpec((tr, 2 * w), lambda i: (i, 0)), blk, blk],
        out_shape=[jax.ShapeDtypeStruct((r, 2 * w), F32), jax.ShapeDtypeStruct((r, w), BF16),
                   jax.ShapeDtypeStruct((r, w), BF16)],
        compiler_params=_cp("parallel"),
    )(proj, proj, ys, ym, dm)


def _loss_fwd_bwd(y, target, *, name):
    r, w = y.shape
    tr = _pick(r, 512, 8)

    def body(y_ref, t_ref, l_ref, dy_ref):
        i = pl.program_id(0)
        e = y_ref[...] - t_ref[...]
        dy_ref[...] = e * (1.0 / w)
        part = jnp.sum(e * e, axis=0, keepdims=True) * (0.5 / w)

        @pl.when(i == 0)
        def _():
            l_ref[...] = part

        @pl.when(i > 0)
        def _():
            l_ref[...] += part

    blk = pl.BlockSpec((tr, w), lambda i: (i, 0))
    return pl.pallas_call(
        body, name=name, grid=(r // tr,),
        in_specs=[blk, blk],
        out_specs=[pl.BlockSpec((1, w), lambda i: (0, 0)), blk],
        out_shape=[jax.ShapeDtypeStruct((1, w), F32), jax.ShapeDtypeStruct((r, w), F32)],
        compiler_params=_cp("arbitrary"),
    )(y, target)


def _adamw(w, g, m, v, *, name):
    r, c = w.shape
    tr = _pick(r, max(8, (1 << 20) // (4 * c) // 8 * 8), 8)
    c1 = 1.0 - ADAM_B1 ** ADAM_STEP
    c2 = 1.0 - ADAM_B2 ** ADAM_STEP

    def body(w_ref, g_ref, m_ref, v_ref, d_ref, nm_ref, nv_ref):
        gv = g_ref[...]
        nm = ADAM_B1 * m_ref[...] + (1.0 - ADAM_B1) * gv
        nv = ADAM_B2 * v_ref[...] + (1.0 - ADAM_B2) * (gv * gv)
        nm_ref[...] = nm
        nv_ref[...] = nv
        d_ref[...] = -ADAM_LR * ((nm / c1) / (jnp.sqrt(nv / c2) + ADAM_EPS) + ADAM_WD * w_ref[...])

    blk = pl.BlockSpec((tr, c), lambda i: (i, 0))
    sh = jax.ShapeDtypeStruct((r, c), F32)
    return pl.pallas_call(
        body, name=name, grid=(r // tr,),
        in_specs=[blk] * 4, out_specs=[blk] * 3, out_shape=[sh] * 3,
        compiler_params=_cp("parallel"),
    )(w, g, m, v)


def _softplus(x):
    return jnp.maximum(x, 0.0) + jnp.log(1.0 + jnp.exp(-jnp.abs(x)))


def _dot_01(x, sel):
    sel_b = sel.astype(BF16)
    acc, rem = None, x
    for _ in range(3):
        piece = rem.astype(BF16)
        part = jnp.dot(piece, sel_b, preferred_element_type=F32)
        acc = part if acc is None else acc + part
        rem = rem - piece.astype(F32)
    return acc


def _ssd_common(dtr_ref, dtrT_ref, dtb_ref, dtbT_ref, al_ref, alT_ref, e_ref):
    L = SSD_CHUNK
    ri = lax.broadcasted_iota(jnp.int32, (L, L), 0)
    cj = lax.broadcasted_iota(jnp.int32, (L, L), 1)
    tril = (ri >= cj).astype(F32)
    triu = (ri <= cj).astype(F32)
    a = -jnp.exp(al_ref[...])
    aT = -jnp.exp(alT_ref[...])
    pre = dtr_ref[...] + dtb_ref[...]
    preT = dtrT_ref[...] + dtbT_ref[...]
    dt = _softplus(pre)
    dtT = _softplus(preT)
    acum = jnp.dot(tril, dt * a, precision=HI, preferred_element_type=F32)
    acumT = jnp.dot(dtT * aT, triu, precision=HI, preferred_element_type=F32)
    e = e_ref[...]
    dt_x = _dot_01(dt, e)
    acum_x = _dot_01(acum, e)
    last_x = acum_x[L - 1:L, :]
    return dict(ri=ri, cj=cj, tril=tril, triu=triu, a=a, aT=aT, pre=pre, preT=preT, dt=dt, dtT=dtT,
                acum=acum, acumT=acumT, dt_x=dt_x, eacum_x=jnp.exp(acum_x), w_x=jnp.exp(last_x - acum_x),
                elast_x=jnp.exp(last_x))


def _dot_nt(a, b):
    return lax.dot_general(a, b, (((1,), (1,)), ((), ())), preferred_element_type=F32)


def _dot_tn(a, b):
    return lax.dot_general(a, b, (((0,), (0,)), ((), ())), preferred_element_type=F32)


def _dot(a, b):
    return jnp.dot(a, b, preferred_element_type=F32)


def _ssd_specs(nc, rev):
    L = SSD_CHUNK
    ix = (lambda c: nc - 1 - c) if rev else (lambda c: c)
    return [
        pl.BlockSpec((L, SSD_D_INNER), lambda c: (ix(c), 0)),
        pl.BlockSpec((L, 512), lambda c: (ix(c), 4)),
        pl.BlockSpec((L, 512), lambda c: (ix(c), 5)),
        pl.BlockSpec((L, SSD_HEADS), lambda c: (ix(c), 0)),
        pl.BlockSpec((SSD_HEADS, L), lambda c: (0, ix(c))),
        pl.BlockSpec((1, SSD_HEADS), lambda c: (0, 0)),
        pl.BlockSpec((SSD_HEADS, 1), lambda c: (0, 0)),
        pl.BlockSpec((1, SSD_HEADS), lambda c: (0, 0)),
        pl.BlockSpec((SSD_HEADS, 1), lambda c: (0, 0)),
        pl.BlockSpec((1, SSD_D_INNER), lambda c: (0, 0)),
        pl.BlockSpec((SSD_HEADS, SSD_D_INNER), lambda c: (0, 0)),
    ]


def _ssd_fwd(xc, dtr, dtrT, dtb, dtbT, alog, alogT, dskx, expand, *, name):
    s = xc.shape[0]
    L = SSD_CHUNK
    nc = s // L

    def body(x_ref, b_ref, c_ref, dtr_ref, dtrT_ref, dtb_ref, dtbT_ref, al_ref, alT_ref, dsk_ref, e_ref,
             y_ref, st_ref, state):
        ci = pl.program_id(0)

        @pl.when(ci == 0)
        def _():
            state[...] = jnp.zeros_like(state)

        st_ref[0] = state[...]
        q = _ssd_common(dtr_ref, dtrT_ref, dtb_ref, dtbT_ref, al_ref, alT_ref, e_ref)
        causal = q["ri"] >= q["cj"]
        lane_lo = q["cj"] < 64
        x = x_ref[...]
        xdt = x * q["dt_x"]
        xdt_b = xdt.astype(BF16)
        xdtw_b = (xdt * q["w_x"]).astype(BF16)
        for g in range(SSD_GROUPS):
            bg = b_ref[:, 128 * g:128 * g + 128]
            cg_b = c_ref[:, 128 * g:128 * g + 128].astype(BF16)
            cb = _dot_nt(cg_b, bg.astype(BF16))
            bgT_b = bg.T.astype(BF16)
            s0 = state[g]
            for jj in range(4):
                j = 4 * g + jj
                sl = slice(128 * j, 128 * j + 128)
                sls = slice(128 * jj, 128 * jj + 128)
                ms = []
                for h in (2 * j, 2 * j + 1):
                    seg = q["acum"][:, h:h + 1] - q["acumT"][h:h + 1, :]
                    decay = jnp.exp(jnp.where(causal, seg, -jnp.inf))
                    ms.append((cb * decay).astype(BF16))
                mcat = jnp.concatenate(ms, axis=1)
                xp = xdt_b[:, sl]
                zero = jnp.zeros_like(xp)
                xstack = jnp.concatenate([jnp.where(lane_lo, xp, zero), jnp.where(lane_lo, zero, xp)], axis=0)
                y = _dot(mcat, xstack)
                y = y + q["eacum_x"][:, sl] * _dot(cg_b, s0[:, sls].astype(BF16))
                y = y + x[:, sl] * dsk_ref[:, sl]
                y_ref[:, sl] = y
                state[g, :, sls] = s0[:, sls] * q["elast_x"][:, sl] + _dot(bgT_b, xdtw_b[:, sl])

    return pl.pallas_call(
        body, name=name, grid=(nc,),
        in_specs=_ssd_specs(nc, False),
        out_specs=[pl.BlockSpec((L, SSD_D_INNER), lambda c: (c, 0)),
                   pl.BlockSpec((1, SSD_GROUPS, SSD_STATE, 512), lambda c: (c, 0, 0, 0))],
        out_shape=[jax.ShapeDtypeStruct((s, SSD_D_INNER), F32),
                   jax.ShapeDtypeStruct((nc, SSD_GROUPS, SSD_STATE, 512), F32)],
        scratch_shapes=[pltpu.VMEM((SSD_GROUPS, SSD_STATE, 512), F32)],
        compiler_params=_cp("arbitrary"),
    )(xc, xc, xc, dtr, dtrT, dtb, dtbT, alog, alogT, dskx, expand)


def _ssd_bwd(xc, dtr, dtrT, dtb, dtbT, alog, alogT, dskx, expand, expandT, states, dy, *, name):
    s = xc.shape[0]
    L = SSD_CHUNK
    H = SSD_HEADS
    nc = s // L

    def body(x_ref, b_ref, c_ref, dtr_ref, dtrT_ref, dtb_ref, dtbT_ref, al_ref, alT_ref, dsk_ref, e_ref,
             et_ref, st_ref, dy_ref,
             dxc_ref, ddtc_ref, ddtr_ref, dbc_ref, dbr_ref, dac_ref, dar_ref, ddsk_ref, dstate):
        ci = pl.program_id(0)

        @pl.when(ci == 0)
        def _():
            dstate[...] = jnp.zeros_like(dstate)
            dbc_ref[...] = jnp.zeros_like(dbc_ref)
            dbr_ref[...] = jnp.zeros_like(dbr_ref)
            dac_ref[...] = jnp.zeros_like(dac_ref)
            dar_ref[...] = jnp.zeros_like(dar_ref)
            ddsk_ref[...] = jnp.zeros_like(ddsk_ref)

        q = _ssd_common(dtr_ref, dtrT_ref, dtb_ref, dtbT_ref, al_ref, alT_ref, e_ref)
        ri, cj = q["ri"], q["cj"]
        causal = ri >= cj
        causalT = ri <= cj
        lane_lo = cj < 64
        lane_h = lax.broadcasted_iota(jnp.int32, (1, H), 1)
        sub_h = lax.broadcasted_iota(jnp.int32, (H, 1), 0)
        x = x_ref[...]
        dyv = dy_ref[...]
        xdt = x * q["dt_x"]
        xdt_b = xdt.astype(BF16)
        xdtw = xdt * q["w_x"]
        xdtw_b = xdtw.astype(BF16)
        edy = q["eacum_x"] * dyv
        edy_b = edy.astype(BF16)
        dyv_b = dyv.astype(BF16)
        dacum_col = jnp.zeros((L, H), F32)
        dacum_row = jnp.zeros((H, L), F32)
        dxdt_t, yoff_t, u_t, r_t = [], [], [], []
        for g in range(SSD_GROUPS):
            bg = b_ref[:, 128 * g:128 * g + 128]
            cg = c_ref[:, 128 * g:128 * g + 128]
            bg_b, cg_b = bg.astype(BF16), cg.astype(BF16)
            cb = _dot_nt(cg_b, bg_b)
            cbT = _dot_nt(bg_b, cg_b)
            cgT_b = cg.T.astype(BF16)
            s0 = st_ref[0, g]
            ds = dstate[g]
            s0_b, ds_b = s0.astype(BF16), ds.astype(BF16)
            dcb = jnp.zeros((L, L), F32)
            for jj in range(4):
                j = 4 * g + jj
                sl = slice(128 * j, 128 * j + 128)
                sls = slice(128 * jj, 128 * jj + 128)
                decs, mts = [], []
                for h in (2 * j, 2 * j + 1):
                    seg = q["acum"][:, h:h + 1] - q["acumT"][h:h + 1, :]
                    decs.append(jnp.exp(jnp.where(causal, seg, -jnp.inf)))
                    mts.append((cbT * jnp.exp(jnp.where(causalT, -seg, -jnp.inf))).astype(BF16))
                dyt_b = dyv_b[:, sl]
                zero = jnp.zeros_like(dyt_b)
                dystack = jnp.concatenate([jnp.where(lane_lo, dyt_b, zero), jnp.where(lane_lo, zero, dyt_b)], axis=0)
                dxs = _dot(jnp.concatenate(mts, axis=0), dyt_b)
                dxdt = jnp.where(lane_lo, dxs[:L], dxs[L:])
                dmcat = _dot_nt(dystack, xdt_b[:, sl])
                for idx, h in enumerate((2 * j, 2 * j + 1)):
                    dm = dmcat[L * idx:L * idx + L]
                    dcb = dcb + dm * decs[idx]
                    dseg = dm * cb * decs[idx]
                    dacum_col = dacum_col + jnp.sum(dseg, axis=1, keepdims=True) * (lane_h == h).astype(F32)
                    dacum_row = dacum_row - (sub_h == h).astype(F32) * jnp.sum(dseg, axis=0, keepdims=True)
                gmat = _dot(cg_b, s0_b[:, sls])
                yoff_t.append(edy[:, sl] * gmat)
                qm = _dot(bg_b, ds_b[:, sls])
                dxdt_t.append(dxdt + qm * q["w_x"][:, sl])
                u_t.append(qm * xdtw[:, sl])
                r_t.append(ds[:, sls] * s0[:, sls] * q["elast_x"][:, sl])
                dstate[g, :, sls] = ds[:, sls] * q["elast_x"][:, sl] + _dot(cgT_b, edy_b[:, sl])
            gsl = slice(512 * g, 512 * g + 512)
            dcb_b = dcb.astype(BF16)
            dcg = _dot(dcb_b, bg_b) + _dot_nt(edy_b[:, gsl], s0_b)
            dbg = _dot(dcb.T.astype(BF16), cg_b) + _dot_nt(xdtw_b[:, gsl], ds_b)
            dxc_ref[:, SSD_D_INNER + 128 * g:SSD_D_INNER + 128 * g + 128] = dbg
            dxc_ref[:, SSD_D_INNER + 512 + 128 * g:SSD_D_INNER + 512 + 128 * g + 128] = dcg
        et = et_ref[...]
        dxdt_all = jnp.concatenate(dxdt_t, axis=1)
        yoff = jnp.concatenate(yoff_t, axis=1)
        uu = jnp.concatenate(u_t, axis=1)
        rr = jnp.concatenate(r_t, axis=1)
        dacum_col = dacum_col + _dot_01(yoff - uu, et)
        dlast = jnp.sum(_dot_01(uu + rr, et), axis=0, keepdims=True)
        row_lh = lax.broadcasted_iota(jnp.int32, (L, H), 0)
        dacum_col = dacum_col + jnp.where(row_lh == L - 1, dlast, 0.0)
        d_dta_col = jnp.dot(q["triu"], dacum_col, precision=HI, preferred_element_type=F32)
        d_dta_row = jnp.dot(dacum_row, q["tril"], precision=HI, preferred_element_type=F32)
        ddt_col = d_dta_col * q["a"] + _dot_01(dxdt_all * x, et)
        ddt_row = d_dta_row * q["aT"]
        ddtr_col = ddt_col * jax.nn.sigmoid(q["pre"])
        ddtr_row = ddt_row * jax.nn.sigmoid(q["preT"])
        ddtc_ref[...] = ddtr_col
        ddtr_ref[...] = ddtr_row
        dac_ref[...] += jnp.sum(d_dta_col * q["dt"], axis=0, keepdims=True)
        dar_ref[...] += jnp.sum(d_dta_row * q["dtT"], axis=1, keepdims=True)
        dbc_ref[...] += jnp.sum(ddtr_col, axis=0, keepdims=True)
        dbr_ref[...] += jnp.sum(ddtr_row, axis=1, keepdims=True)
        ddsk_ref[...] += jnp.sum(dyv * x, axis=0, keepdims=True)
        dxc_ref[:, 0:SSD_D_INNER] = dxdt_all * q["dt_x"] + dyv * dsk_ref[...]

    rv = lambda c: nc - 1 - c
    in_specs = _ssd_specs(nc, True) + [
        pl.BlockSpec((SSD_D_INNER, H), lambda c: (0, 0)),
        pl.BlockSpec((1, SSD_GROUPS, SSD_STATE, 512), lambda c: (rv(c), 0, 0, 0)),
        pl.BlockSpec((L, SSD_D_INNER), lambda c: (rv(c), 0)),
    ]
    vec_c = pl.BlockSpec((1, H), lambda c: (0, 0))
    vec_r = pl.BlockSpec((H, 1), lambda c: (0, 0))
    return pl.pallas_call(
        body, name=name, grid=(nc,),
        in_specs=in_specs,
        out_specs=[pl.BlockSpec((L, SSD_CONV_DIM), lambda c: (rv(c), 0)),
                   pl.BlockSpec((L, H), lambda c: (rv(c), 0)),
                   pl.BlockSpec((H, L), lambda c: (0, rv(c))),
                   vec_c, vec_r, vec_c, vec_r,
                   pl.BlockSpec((1, SSD_D_INNER), lambda c: (0, 0))],
        out_shape=[jax.ShapeDtypeStruct((s, SSD_CONV_DIM), F32),
                   jax.ShapeDtypeStruct((s, H), F32), jax.ShapeDtypeStruct((H, s), F32),
                   jax.ShapeDtypeStruct((1, H), F32), jax.ShapeDtypeStruct((H, 1), F32),
                   jax.ShapeDtypeStruct((1, H), F32), jax.ShapeDtypeStruct((H, 1), F32),
                   jax.ShapeDtypeStruct((1, SSD_D_INNER), F32)],
        scratch_shapes=[pltpu.VMEM((SSD_GROUPS, SSD_STATE, 512), F32)],
        compiler_params=_cp("arbitrary"),
    )(xc, xc, xc, dtr, dtrT, dtb, dtbT, alog, alogT, dskx, expand, expandT, states, dy)


QK_PAD = 256
MLA_TS = 256


def _rope_tables4(pos):
    inv = 1.0 / (ROPE_THETA ** (jnp.arange(0, MLA_ROPE, 2, dtype=F32) / MLA_ROPE))
    ang = pos.astype(F32)[:, None] * inv
    c, s = jnp.cos(ang), jnp.sin(ang)
    return jnp.tile(c, (1, 4)), jnp.concatenate([-s, s, -s, s], axis=1)


def _mla_gains(qg, kg):
    z = jnp.zeros((LANE - MLA_ROPE,), F32)
    return (qg[:MLA_NOPE][None], jnp.concatenate([qg[MLA_NOPE:], z])[None],
            kg[:MLA_NOPE][None], jnp.concatenate([kg[MLA_NOPE:], z])[None])


def _rope_swap(t, first):
    return jnp.where(first, pltpu.roll(t, 96, 1), pltpu.roll(t, 32, 1))


def _mla_prep_specs(ts):
    row = lambda w, c=0: pl.BlockSpec((ts, w), lambda i: (i, c))
    vec = pl.BlockSpec((1, LANE), lambda i: (0, 0))
    return [row(MLA_HEADS * MLA_QK), row(2 * MLA_HEADS * MLA_NOPE), row(LANE, OFF_KRDT // LANE), row(LANE), row(LANE),
            vec, vec, vec, vec]


def _mla_prep_fwd(qraw, kvraw, proj, cos4, sin4, gqn, gqr, gkn, gkr, *, name):
    s = qraw.shape[0]
    ts = _pick(s, MLA_TS, 8)

    def body(q_ref, kv_ref, kr_ref, cos_ref, sin_ref, gqn_ref, gqr_ref, gkn_ref, gkr_ref, qo_ref, ko_ref):
        lane = lax.broadcasted_iota(jnp.int32, (ts, LANE), 1)
        lo = lane < 64
        first = (lane % 64) < 32
        cos, sin = cos_ref[...], sin_ref[...]
        kr = jnp.where(lo, kr_ref[...], 0.0)
        ssq_kr = jnp.sum(kr * kr, axis=-1, keepdims=True)

        def head(xn, xr, ssq_r, gn, gr):
            rs = lax.rsqrt((jnp.sum(xn * xn, axis=-1, keepdims=True) + ssq_r) * (1.0 / MLA_QK) + EPS)
            yr = xr * rs * gr
            return xn * rs * gn, yr * cos + _rope_swap(yr, first) * sin

        for h in range(MLA_HEADS):
            tile = q_ref[:, MLA_HEADS * MLA_NOPE + LANE * (h // 2):MLA_HEADS * MLA_NOPE + LANE * (h // 2) + LANE]
            qr = jnp.where(lo, tile if h % 2 == 0 else pltpu.roll(tile, 64, 1), 0.0)
            on, orr = head(q_ref[:, LANE * h:LANE * h + LANE], qr, jnp.sum(qr * qr, axis=-1, keepdims=True),
                           gqn_ref[...], gqr_ref[...])
            qo_ref[h, :, 0:LANE] = on.astype(BF16)
            qo_ref[h, :, LANE:QK_PAD] = orr.astype(BF16)
            on, orr = head(kv_ref[:, LANE * h:LANE * h + LANE], kr, ssq_kr, gkn_ref[...], gkr_ref[...])
            ko_ref[h, :, 0:LANE] = on.astype(BF16)
            ko_ref[h, :, LANE:QK_PAD] = orr.astype(BF16)

    out = pl.BlockSpec((MLA_HEADS, ts, QK_PAD), lambda i: (0, i, 0))
    sh = jax.ShapeDtypeStruct((MLA_HEADS, s, QK_PAD), BF16)
    return pl.pallas_call(
        body, name=name, grid=(s // ts,),
        in_specs=_mla_prep_specs(ts), out_specs=[out, out], out_shape=[sh, sh],
        compiler_params=_cp("parallel"),
    )(qraw, kvraw, proj, cos4, sin4, gqn, gqr, gkn, gkr)


def _mla_prep_bwd(qraw, kvraw, proj, cos4, sin4, gqn, gqr, gkn, gkr, dq, dk, *, name):
    s = qraw.shape[0]
    ts = _pick(s, MLA_TS, 8)

    def body(q_ref, kv_ref, kr_ref, cos_ref, sin_ref, gqn_ref, gqr_ref, gkn_ref, gkr_ref, dq_ref, dk_ref,
             dqraw_ref, dkn_ref, dkr_ref, dgqn_ref, dgqr_ref, dgkn_ref, dgkr_ref):
        i = pl.program_id(0)

        @pl.when(i == 0)
        def _():
            for r in (dgqn_ref, dgqr_ref, dgkn_ref, dgkr_ref):
                r[...] = jnp.zeros_like(r)

        lane = lax.broadcasted_iota(jnp.int32, (ts, LANE), 1)
        lo = lane < 64
        first = (lane % 64) < 32
        cos, sin = cos_ref[...], sin_ref[...]
        kr = jnp.where(lo, kr_ref[...], 0.0)
        ssq_kr = jnp.sum(kr * kr, axis=-1, keepdims=True)

        def head(xn, xr, ssq_r, gn, gr, don, dor):
            rs = lax.rsqrt((jnp.sum(xn * xn, axis=-1, keepdims=True) + ssq_r) * (1.0 / MLA_QK) + EPS)
            xhn, xhr = xn * rs, xr * rs
            dor = jnp.where(lo, dor, 0.0)
            dyr = dor * cos + _rope_swap(dor * sin, first)
            dxn, dxr = don * gn, dyr * gr
            mm = (jnp.sum(dxn * xhn, axis=-1, keepdims=True) + jnp.sum(dxr * xhr, axis=-1, keepdims=True)) * (1.0 / MLA_QK)
            return (rs * (dxn - xhn * mm), rs * (dxr - xhr * mm),
                    jnp.sum(don * xhn, axis=0, keepdims=True), jnp.sum(dyr * xhr, axis=0, keepdims=True))

        dkr_acc = jnp.zeros((ts, LANE), F32)
        prev = None
        for h in range(MLA_HEADS):
            c0 = MLA_HEADS * MLA_NOPE + LANE * (h // 2)
            tile = q_ref[:, c0:c0 + LANE]
            qr = jnp.where(lo, tile if h % 2 == 0 else pltpu.roll(tile, 64, 1), 0.0)
            dn, dr, gn_p, gr_p = head(q_ref[:, LANE * h:LANE * h + LANE], qr, jnp.sum(qr * qr, axis=-1, keepdims=True),
                                      gqn_ref[...], gqr_ref[...], dq_ref[h, :, 0:LANE], dq_ref[h, :, LANE:QK_PAD])
            dqraw_ref[:, LANE * h:LANE * h + LANE] = dn.astype(dqraw_ref.dtype)
            dgqn_ref[...] += gn_p
            dgqr_ref[...] += gr_p
            if h % 2 == 0:
                prev = dr
            else:
                dqraw_ref[:, c0:c0 + LANE] = (prev + pltpu.roll(dr, 64, 1)).astype(dqraw_ref.dtype)
            dn, dr, gn_p, gr_p = head(kv_ref[:, LANE * h:LANE * h + LANE], kr, ssq_kr, gkn_ref[...], gkr_ref[...],
                                      dk_ref[h, :, 0:LANE], dk_ref[h, :, LANE:QK_PAD])
            dkn_ref[:, LANE * h:LANE * h + LANE] = dn
            dkr_acc = dkr_acc + dr
            dgkn_ref[...] += gn_p
            dgkr_ref[...] += gr_p
        dkr_ref[...] = dkr_acc

    row = lambda w: pl.BlockSpec((ts, w), lambda i: (i, 0))
    vec = pl.BlockSpec((1, LANE), lambda i: (0, 0))
    dspec = pl.BlockSpec((MLA_HEADS, ts, QK_PAD), lambda i: (0, i, 0))
    vsh = jax.ShapeDtypeStruct((1, LANE), F32)
    return pl.pallas_call(
        body, name=name, grid=(s // ts,),
        in_specs=_mla_prep_specs(ts) + [dspec, dspec],
        out_specs=[row(MLA_HEADS * MLA_QK), row(MLA_HEADS * MLA_NOPE), row(LANE), vec, vec, vec, vec],
        out_shape=[jax.ShapeDtypeStruct((s, MLA_HEADS * MLA_QK), BF16), jax.ShapeDtypeStruct((s, MLA_HEADS * MLA_NOPE), F32),
                   jax.ShapeDtypeStruct((s, LANE), F32), vsh, vsh, vsh, vsh],
        compiler_params=_cp("arbitrary"),
    )(qraw, kvraw, proj, cos4, sin4, gqn, gqr, gkn, gkr, dq, dk)


ATT_T = 1024
ATT_T_FWD = 2048
ATT_SCALE = MLA_QK ** -0.5


def _attn_fwd(q, k, kvraw, *, name):
    nh, s, _ = q.shape
    t = _pick(s, ATT_T_FWD, LANE)
    nb = s // t

    def body(q_ref, k_ref, v_ref, o_ref, lse_ref, m_ref, l_ref, acc_ref):
        i, j = pl.program_id(1), pl.program_id(2)

        @pl.when(j == 0)
        def _():
            m_ref[...] = jnp.full_like(m_ref, -jnp.inf)
            l_ref[...] = jnp.zeros_like(l_ref)
            acc_ref[...] = jnp.zeros_like(acc_ref)

        def step(diagonal):
            sc = _dot_nt(q_ref[0], k_ref[0]) * ATT_SCALE
            if diagonal:
                ri = lax.broadcasted_iota(jnp.int32, (t, t), 0)
                cj = lax.broadcasted_iota(jnp.int32, (t, t), 1)
                sc = jnp.where(ri >= cj, sc, -jnp.inf)
            m_new = jnp.maximum(m_ref[...], jnp.max(sc, axis=-1, keepdims=True))
            alpha = jnp.exp(m_ref[...] - m_new)
            p = jnp.exp(sc - m_new)
            l_ref[...] = alpha * l_ref[...] + jnp.sum(p, axis=-1, keepdims=True)
            acc_ref[...] = alpha * acc_ref[...] + _dot(p.astype(BF16), v_ref[...].astype(BF16))
            m_ref[...] = m_new

        @pl.when(j < i)
        def _():
            step(False)

        @pl.when(j == i)
        def _():
            step(True)
            o_ref[...] = acc_ref[...] / l_ref[...]
            lse_ref[0] = m_ref[...] + jnp.log(l_ref[...])

    return pl.pallas_call(
        body, name=name, grid=(nh, nb, nb),
        in_specs=[pl.BlockSpec((1, t, QK_PAD), lambda h, i, j: (h, i, 0)),
                  pl.BlockSpec((1, t, QK_PAD), lambda h, i, j: (h, jnp.minimum(j, i), 0)),
                  pl.BlockSpec((t, MLA_V), lambda h, i, j: (jnp.minimum(j, i), nh + h))],
        out_specs=[pl.BlockSpec((t, MLA_V), lambda h, i, j: (i, h)),
                   pl.BlockSpec((1, t, 1), lambda h, i, j: (h, i, 0))],
        out_shape=[jax.ShapeDtypeStruct((s, nh * MLA_V), F32), jax.ShapeDtypeStruct((nh, s, 1), F32)],
        scratch_shapes=[pltpu.VMEM((t, 1), F32), pltpu.VMEM((t, 1), F32), pltpu.VMEM((t, MLA_V), F32)],
        compiler_params=_cp("parallel", "parallel", "arbitrary"),
    )(q, k, kvraw)


def _attn_bwd(q, k, kvraw, o, lse, do, *, name):
    nh, s, _ = q.shape
    t = _pick(s, ATT_T, LANE)
    nb = s // t

    def body(q_ref, k_ref, v_ref, o_ref, lse_ref, do_ref, dq_ref, dk_ref, dv_ref, dk_acc, dv_acc):
        j, i = pl.program_id(1), pl.program_id(2)

        @pl.when(i == 0)
        def _():
            dk_acc[...] = jnp.zeros_like(dk_acc)
            dv_acc[...] = jnp.zeros_like(dv_acc)

        def step(diagonal):
            qv, kv = q_ref[0], k_ref[0]
            sc = _dot_nt(qv, kv) * ATT_SCALE
            if diagonal:
                ri = lax.broadcasted_iota(jnp.int32, (t, t), 0)
                cj = lax.broadcasted_iota(jnp.int32, (t, t), 1)
                sc = jnp.where(ri >= cj, sc, -jnp.inf)
            p = jnp.exp(sc - lse_ref[0])
            dov = do_ref[...]
            delta = jnp.sum(dov * o_ref[...], axis=-1, keepdims=True)
            do_b = dov.astype(BF16)
            dv_acc[...] += _dot_tn(p.astype(BF16), do_b)
            dp = _dot_nt(do_b, v_ref[...].astype(BF16))
            ds_b = (p * (dp - delta) * ATT_SCALE).astype(BF16)
            dk_acc[...] += _dot_tn(ds_b, qv)
            dq_part = _dot(ds_b, kv)
            rows = pl.ds(pl.multiple_of(i * t, t), t)

            @pl.when(j == 0)
            def _():
                dq_ref[0, rows, :] = dq_part

            @pl.when(j > 0)
            def _():
                dq_ref[0, rows, :] += dq_part

        @pl.when(i > j)
        def _():
            step(False)

        @pl.when(i == j)
        def _():
            step(True)

        @pl.when(i == nb - 1)
        def _():
            dk_ref[0] = dk_acc[...]
            dv_ref[...] = dv_acc[...]

    qi = lambda h, j, i: jnp.maximum(i, j)
    return pl.pallas_call(
        body, name=name, grid=(nh, nb, nb),
        in_specs=[pl.BlockSpec((1, t, QK_PAD), lambda h, j, i: (h, qi(h, j, i), 0)),
                  pl.BlockSpec((1, t, QK_PAD), lambda h, j, i: (h, j, 0)),
                  pl.BlockSpec((t, MLA_V), lambda h, j, i: (j, nh + h)),
                  pl.BlockSpec((t, MLA_V), lambda h, j, i: (qi(h, j, i), h)),
                  pl.BlockSpec((1, t, 1), lambda h, j, i: (h, qi(h, j, i), 0)),
                  pl.BlockSpec((t, MLA_V), lambda h, j, i: (qi(h, j, i), h))],
        out_specs=[pl.BlockSpec((1, s, QK_PAD), lambda h, j, i: (h, 0, 0)),
                   pl.BlockSpec((1, t, QK_PAD), lambda h, j, i: (h, j, 0)),
                   pl.BlockSpec((t, MLA_V), lambda h, j, i: (j, h))],
        out_shape=[jax.ShapeDtypeStruct((nh, s, QK_PAD), F32), jax.ShapeDtypeStruct((nh, s, QK_PAD), F32),
                   jax.ShapeDtypeStruct((s, nh * MLA_V), F32)],
        scratch_shapes=[pltpu.VMEM((t, QK_PAD), F32), pltpu.VMEM((t, MLA_V), F32)],
        compiler_params=_cp("parallel", "arbitrary", "arbitrary"),
    )(q, k, kvraw, o, lse, do)


def _ffn_fwd(h, w, tag):
    n = _rms_fwd(h, w["ln"], name=tag + "_norm")
    act, gate, up = _ffn_up(n, w["w13"], name=tag + "_up")
    out = _matmul(act, w["w2"], "nn", name=tag + "_down", scale=0.5, res=h)
    return out, (h, n, gate, up, act)


def _ffn_bwd(dout, saved, w, tag):
    h, n, gate, up, act = saved
    dact = _matmul(dout, w["w2"], "nt", name=tag + "_down_dx", scale=0.5, out_dtype=BF16)
    dw2 = _matmul(act, dout, "tn", name=tag + "_down_dw", scale=0.5)
    dgu = _swiglu_bwd(gate, up, dact, name=tag + "_act_bwd")
    dw13 = _matmul(n, dgu, "tn", name=tag + "_up_dw")
    dn = _matmul(dgu, w["w13"], "nt", name=tag + "_up_dx")
    dh, dln = _rms_bwd(h, w["ln"], dn, name=tag + "_norm_bwd", res=dout)
    return dh, dict(ln=dln, w13=dw13, w2=dw2)


def _mixer_fwd(h, w, rope, tag):
    cos4, sin4 = rope
    u = _rms_fwd(h, w["ln_mix"], name=tag + "_norm")
    proj = _matmul(u, w["w_in"], "nn", name=tag + "_in")
    xc = _conv_fwd(proj, w["conv_w"], w["conv_b"], name=tag + "_conv")
    dtr = proj[:, OFF_KRDT + MLA_ROPE:OFF_KRDT + MLA_ROPE + SSD_HEADS]
    dtrT = dtr.T
    y, states = _ssd_fwd(xc, dtr, dtrT, *w["ssd_aux"], name=tag + "_ssd")
    yn = _gated_rms_fwd(y, proj, w["ssd_norm"], name=tag + "_ssd_norm")
    y_ssd = _matmul(yn, w["w_ssd_out"], "nn", name=tag + "_ssd_out")
    cqn = _rms_fwd(proj, w["q_lora_norm"], name=tag + "_q_lora_norm", col=OFF_CQ // MLA_Q_LORA, width=MLA_Q_LORA)
    qraw = _matmul(cqn, w["w_uq"], "nn", name=tag + "_uq")
    ckvn = _rms_fwd(proj, w["kv_lora_norm"], name=tag + "_kv_lora_norm", col=OFF_CKV // MLA_KV_LORA, width=MLA_KV_LORA)
    kvraw = _matmul(ckvn, w["w_ukv"], "nn", name=tag + "_ukv")
    qf, kf = _mla_prep_fwd(qraw, kvraw, proj, cos4, sin4, *w["qk_gains"], name=tag + "_qk_prep")
    o, lse = _attn_fwd(qf, kf, kvraw, name=tag + "_attn")
    y_mla = _matmul(o, w["w_mla_out"], "nn", name=tag + "_mla_out")
    merged = _merge_fwd(proj, y_ssd, y_mla, name=tag + "_merge")
    out = _matmul(merged, w["w_o"], "nn", name=tag + "_o", res=h)
    saved = dict(h=h, u=u, proj=proj, xc=xc, dtr=dtr, dtrT=dtrT, states=states, y=y, yn=yn, y_ssd=y_ssd, cqn=cqn,
                 qraw=qraw, ckvn=ckvn, kvraw=kvraw, qf=qf, kf=kf, o=o, lse=lse, y_mla=y_mla, merged=merged)
    return out, saved


def _mixer_bwd(dout, s, w, rope, tag):
    cos4, sin4 = rope
    g = {}
    proj = s["proj"]
    dmerged = _matmul(dout, w["w_o"], "nt", name=tag + "_o_dx")
    g["w_o"] = _matmul(s["merged"], dout, "tn", name=tag + "_o_dw")
    dgates, dy_ssd, dy_mla = _merge_bwd(proj, s["y_ssd"], s["y_mla"], dmerged, name=tag + "_merge_bwd")
    do = _matmul(dy_mla, w["w_mla_out"], "nt", name=tag + "_mla_out_dx")
    g["w_mla_out"] = _matmul(s["o"], dy_mla, "tn", name=tag + "_mla_out_dw")
    dqf, dkf, dv = _attn_bwd(s["qf"], s["kf"], s["kvraw"], s["o"], s["lse"], do, name=tag + "_attn_bwd")
    dqraw, dkn, dkrt, dgqn, dgqr, dgkn, dgkr = _mla_prep_bwd(
        s["qraw"], s["kvraw"], proj, cos4, sin4, *w["qk_gains"], dqf, dkf, name=tag + "_qk_prep_bwd")
    g["q_norm"] = jnp.concatenate([dgqn[0], dgqr[0, :MLA_ROPE]])
    g["k_norm"] = jnp.concatenate([dgkn[0], dgkr[0, :MLA_ROPE]])
    dkvraw = jnp.concatenate([dkn, dv], axis=1).astype(BF16)
    dcqn = _matmul(dqraw, w["w_uq"], "nt", name=tag + "_uq_dx")
    g["w_uq"] = _matmul(s["cqn"], dqraw, "tn", name=tag + "_uq_dw")
    dckvn = _matmul(dkvraw, w["w_ukv"], "nt", name=tag + "_ukv_dx")
    g["w_ukv"] = _matmul(s["ckvn"], dkvraw, "tn", name=tag + "_ukv_dw")
    dcq, g["q_lora_norm"] = _rms_bwd(proj, w["q_lora_norm"], dcqn, name=tag + "_q_lora_norm_bwd",
                                     col=OFF_CQ // MLA_Q_LORA, width=MLA_Q_LORA)
    dckv, g["kv_lora_norm"] = _rms_bwd(proj, w["kv_lora_norm"], dckvn, name=tag + "_kv_lora_norm_bwd",
                                       col=OFF_CKV // MLA_KV_LORA, width=MLA_KV_LORA)
    dyn = _matmul(dy_ssd, w["w_ssd_out"], "nt", name=tag + "_ssd_out_dx")
    g["w_ssd_out"] = _matmul(s["yn"], dy_ssd, "tn", name=tag + "_ssd_out_dw")
    dy, dz, g["ssd_norm"] = _gated_rms_bwd(s["y"], proj, w["ssd_norm"], dyn, name=tag + "_ssd_norm_bwd")
    aux = w["ssd_aux"]
    dxc, ddt_c, ddt_r, dbias_c, dbias_r, da_c, da_r, ddsk = _ssd_bwd(
        s["xc"], s["dtr"], s["dtrT"], *aux, aux[-1].T, s["states"], dy, name=tag + "_ssd_bwd")
    g["dt_bias"] = dbias_c[0] + dbias_r[:, 0]
    g["a_log"] = (da_c[0] + da_r[:, 0]) * (-jnp.exp(aux[2][0]))
    g["d_skip"] = jnp.sum(ddsk.reshape(SSD_HEADS, SSD_HEAD_DIM), axis=1)
    dpre, g["conv_w"], g["conv_b"] = _conv_bwd_pre(proj, w["conv_w"], w["conv_b"], dxc, name=tag + "_conv_bwd_pre")
    dxbc = _conv_bwd_x(dpre, w["conv_w"], name=tag + "_conv_bwd_x")
    ddtr = ddt_c + ddt_r.T
    dkrdt = jnp.concatenate([dkrt[:, :MLA_ROPE], ddtr, jnp.zeros((ddtr.shape[0], LANE - MLA_ROPE - SSD_HEADS), F32)], axis=1)
    dproj = jnp.concatenate([dz, dxbc, dgates, dcq, dckv, dkrdt], axis=1).astype(BF16)
    du = _matmul(dproj, w["w_in"], "nt", name=tag + "_in_dx")
    g["w_in"] = _matmul(s["u"], dproj, "tn", name=tag + "_in_dw")
    dh, g["ln_mix"] = _rms_bwd(s["h"], w["ln_mix"], du, name=tag + "_norm_bwd", res=dout)
    return dh, g


W_NAMES = ["ln_ffn1", "ffn1_w13", "ffn1_w2", "ln_mix", "w_in", "conv_w", "conv_b", "dt_bias", "a_log", "d_skip",
           "ssd_norm", "w_ssd_out", "q_lora_norm", "w_uq", "kv_lora_norm", "w_ukv", "q_norm", "k_norm", "w_mla_out",
           "w_o", "ln_ffn2", "ffn2_w13", "ffn2_w2"]
SHARD_AXIS = {"ffn1_w13": 2, "ffn1_w2": 1, "w_in": 2, "conv_w": 2, "w_ssd_out": 1, "w_uq": 2, "w_ukv": 2,
              "w_mla_out": 1, "w_o": 1, "ffn2_w13": 2, "ffn2_w2": 1}
SHARDED = [n for n in W_NAMES if n in SHARD_AXIS]
REPLICATED = [n for n in W_NAMES if n not in SHARD_AXIS]
N_CHIPS = 4
N_DEV = 8
PACK_COLS = 1024
IN_SPLIT = (2048, 3072, 32, 512, 256, 64, 2048)


def _pack(arrs, rows, dtype):
    flat = jnp.concatenate([a.astype(dtype).reshape(-1) for a in arrs])
    return jnp.pad(flat, (0, rows * PACK_COLS - flat.shape[0])).reshape(rows, PACK_COLS)


def _unpack(packed, shapes):
    flat = packed.reshape(-1)
    out, at = [], 0
    for sh in shapes:
        n = math.prod(sh)
        out.append(flat[at:at + n].reshape(sh))
        at += n
    return out


def _pack_rows(shapes):
    n = sum(math.prod(sh) for sh in shapes)
    return -(-n // (PACK_COLS * 1024)) * 1024


def _in_perm(w_in):
    z, xbc, dt, cq, ckv, kr, gates = jnp.split(w_in, list(np_cumsum(IN_SPLIT))[:-1], axis=1)
    return jnp.concatenate([z, xbc, gates, cq, ckv, kr, dt, jnp.zeros((w_in.shape[0], PROJ_W - sum(IN_SPLIT)), w_in.dtype)], axis=1)


def _in_unperm(g):
    z, xbc, gates, cq, ckv = (g[:, OFF_Z:OFF_XBC], g[:, OFF_XBC:OFF_GATES], g[:, OFF_GATES:OFF_CQ], g[:, OFF_CQ:OFF_CKV],
                              g[:, OFF_CKV:OFF_KRDT])
    kr = g[:, OFF_KRDT:OFF_KRDT + MLA_ROPE]
    dt = g[:, OFF_KRDT + MLA_ROPE:OFF_KRDT + MLA_ROPE + SSD_HEADS]
    return jnp.concatenate([z, xbc, dt, cq, ckv, kr, gates], axis=1)


def np_cumsum(sizes):
    out, t = [], 0
    for s in sizes:
        t += s
        out.append(t)
    return out


def _head_perm(w, first):
    r = w.shape[0]
    w3 = w.reshape(r, MLA_HEADS, -1)
    return jnp.concatenate([w3[:, :, :first].reshape(r, -1), w3[:, :, first:].reshape(r, -1)], axis=1)


def _head_unperm(g, first):
    r = g.shape[0]
    rest = g.shape[1] // MLA_HEADS - first
    a = g[:, :MLA_HEADS * first].reshape(r, MLA_HEADS, first)
    b = g[:, MLA_HEADS * first:].reshape(r, MLA_HEADS, rest)
    return jnp.concatenate([a, b], axis=2).reshape(r, -1)


def _layer_weights(full, l):
    row = lambda n: full[n][l][None].astype(F32)
    expand = jnp.repeat(jnp.eye(SSD_HEADS, dtype=F32), SSD_HEAD_DIM, axis=1)
    dtb, al, dsk = full["dt_bias"][l], full["a_log"][l], full["d_skip"][l]
    mixer = dict(
        ln_mix=row("ln_mix"), w_in=_in_perm(full["w_in"][l]), conv_w=full["conv_w"][l], conv_b=row("conv_b"),
        ssd_aux=(dtb[None], dtb[:, None], al[None], al[:, None], jnp.repeat(dsk, SSD_HEAD_DIM)[None], expand),
        ssd_norm=row("ssd_norm"), w_ssd_out=full["w_ssd_out"][l],
        q_lora_norm=row("q_lora_norm"), w_uq=_head_perm(full["w_uq"][l], MLA_NOPE),
        kv_lora_norm=row("kv_lora_norm"), w_ukv=_head_perm(full["w_ukv"][l], MLA_NOPE),
        qk_gains=_mla_gains(full["q_norm"][l], full["k_norm"][l]),
        w_mla_out=full["w_mla_out"][l], w_o=full["w_o"][l])
    ffn1 = dict(ln=row("ln_ffn1"), w13=full["ffn1_w13"][l], w2=full["ffn1_w2"][l])
    ffn2 = dict(ln=row("ln_ffn2"), w13=full["ffn2_w13"][l], w2=full["ffn2_w2"][l])
    return ffn1, mixer, ffn2


def _layer_grads(g1, gm, g2):
    return {
        "ln_ffn1": g1["ln"][0], "ffn1_w13": g1["w13"], "ffn1_w2": g1["w2"],
        "ln_mix": gm["ln_mix"][0], "w_in": _in_unperm(gm["w_in"]), "conv_w": gm["conv_w"], "conv_b": gm["conv_b"][0],
        "dt_bias": gm["dt_bias"], "a_log": gm["a_log"], "d_skip": gm["d_skip"], "ssd_norm": gm["ssd_norm"][0],
        "w_ssd_out": gm["w_ssd_out"], "q_lora_norm": gm["q_lora_norm"][0], "w_uq": _head_unperm(gm["w_uq"], MLA_NOPE),
        "kv_lora_norm": gm["kv_lora_norm"][0], "w_ukv": _head_unperm(gm["w_ukv"], MLA_NOPE),
        "q_norm": gm["q_norm"], "k_norm": gm["k_norm"], "w_mla_out": gm["w_mla_out"], "w_o": gm["w_o"],
        "ln_ffn2": g2["ln"][0], "ffn2_w13": g2["w13"], "ffn2_w2": g2["w2"],
    }


def _local_step(x, positions, loss_target, full):
    rope = _rope_tables4(positions)
    lw = [_layer_weights(full, l) for l in range(DEPTH)]
    h = x
    saved = []
    for l in range(DEPTH):
        f1, mx, f2 = lw[l]
        h, s1 = _ffn_fwd(h, f1, f"l{l}_ffn1")
        h, sm = _mixer_fwd(h, mx, rope, f"l{l}_mix")
        h, s2 = _ffn_fwd(h, f2, f"l{l}_ffn2")
        saved.append((s1, sm, s2))
    loss_part, dh = _loss_fwd_bwd(h, loss_target, name="loss")
    grads = [None] * DEPTH
    for l in reversed(range(DEPTH)):
        f1, mx, f2 = lw[l]
        s1, sm, s2 = saved[l]
        dh, g2 = _ffn_bwd(dh, s2, f2, f"l{l}_ffn2")
        dh, gm = _mixer_bwd(dh, sm, mx, rope, f"l{l}_mix")
        dh, g1 = _ffn_bwd(dh, s1, f1, f"l{l}_ffn1")
        grads[l] = _layer_grads(g1, gm, g2)
    full_grads = {n: jnp.stack([grads[l][n] for l in range(DEPTH)]) for n in W_NAMES}
    return loss_part, dh, full_grads


MESH = pl.DeviceIdType.MESH
ANY = pl.BlockSpec(memory_space=pl.ANY)


def _place():
    return lax.axis_index("x"), lax.axis_index("y"), lax.axis_index("c")


def _other_chips(x, y):
    return [(1 - x, y), (x, 1 - y), (1 - x, 1 - y)]


def _remote(src, dst, send_sems, recv_sems, k, to):
    return pltpu.make_async_remote_copy(src_ref=src, dst_ref=dst, send_sem=send_sems.at[k], recv_sem=recv_sems.at[k],
                                        device_id=to, device_id_type=MESH)


N_PARTS = 8


def _parts(rows):
    size = rows // N_PARTS
    assert size * N_PARTS == rows and size % 16 == 0, rows
    return [(p * size, size) for p in range(N_PARTS)]


def _rows(ref, lead, base, start, size):
    return ref.at[(*lead, pl.ds(pl.multiple_of(base + start, 16), size), slice(None))]


def _my_chip():
    return 2 * lax.axis_index("x") + lax.axis_index("y")


def _own_slot(packed, *, name):
    r, ncol = packed.shape
    tr = _pick(r, 512, 16)

    def body(x_ref, o_ref):
        o_ref[...] = x_ref[...]

    return pl.pallas_call(
        body, name=name, grid=(r // tr,),
        in_specs=[pl.BlockSpec((tr, ncol), lambda i: (i, 0))],
        out_specs=pl.BlockSpec((None, tr, ncol), lambda i: (_my_chip(), i, 0)),
        out_shape=jax.ShapeDtypeStruct((N_CHIPS, r, ncol), packed.dtype),
        compiler_params=_cp("arbitrary"),
    )(packed)


def _gather_shards(packed, slots, *, name):
    r, ncol = packed.shape
    hr = r // 2
    parts = _parts(hr)

    def body(x_ref, slots_ref, out_ref, send_sems, recv_sems):
        del slots_ref
        x, y, c = _place()
        chips = _other_chips(x, y)
        me = 2 * x + y

        def half(chip, cc):
            return _rows(out_ref, (2 * chip[0] + chip[1],), cc * hr, 0, hr)

        for j, chip in enumerate(chips):
            for st, sz in parts:
                _remote(_rows(x_ref, (), c * hr, st, sz), _rows(out_ref, (me,), c * hr, st, sz), send_sems, recv_sems, j,
                        (*chip, c)).start()
        for j, chip in enumerate(chips):
            _remote(half(chip, c), half(chip, c), send_sems, recv_sems, j, (x, y, c)).wait_recv()
            slot = 2 * chip[0] + chip[1]
            for st, sz in parts:
                _remote(_rows(out_ref, (slot,), c * hr, st, sz), _rows(out_ref, (slot,), c * hr, st, sz), send_sems,
                        recv_sems, 3 + j, (x, y, 1 - c)).start()
        for j, chip in enumerate(chips):
            _remote(half(chip, 1 - c), half(chip, 1 - c), send_sems, recv_sems, 3 + j, (x, y, c)).wait_recv()
        for k in range(6):
            _remote(half((x, y), c), half((x, y), c), send_sems, recv_sems, k, (x, y, c)).wait_send()

    return pl.pallas_call(
        body, name=name,
        out_shape=jax.ShapeDtypeStruct((N_CHIPS, r, ncol), packed.dtype),
        in_specs=[ANY, ANY], out_specs=ANY, input_output_aliases={1: 0},
        scratch_shapes=[pltpu.SemaphoreType.DMA((6,)), pltpu.SemaphoreType.DMA((6,))],
    )(packed, slots)


def _swap_halves(g, *, name):
    n, r, ncol = g.shape
    hr = r // 2
    parts = _parts(hr)

    def body(g_ref, got_ref, send_sems, recv_sems):
        x, y, c = _place()
        for s in range(n):
            for st, sz in parts:
                _remote(_rows(g_ref, (s,), (1 - c) * hr, st, sz), got_ref.at[s, pl.ds(st, sz), :], send_sems, recv_sems, 0,
                        (x, y, 1 - c)).start()
        _remote(got_ref, got_ref, send_sems, recv_sems, 0, (x, y, c)).wait()

    return pl.pallas_call(
        body, name=name, out_shape=jax.ShapeDtypeStruct((n, hr, ncol), g.dtype), in_specs=[ANY], out_specs=ANY,
        scratch_shapes=[pltpu.SemaphoreType.DMA((1,)), pltpu.SemaphoreType.DMA((1,))],
    )(g)


def _add_cores(g, got, *, name):
    n, r, ncol = g.shape
    hr = r // 2
    tr = _pick(hr, 512, 16)
    nb = hr // tr

    def body(a_ref, b_ref, o_ref):
        o_ref[...] = (a_ref[...] + b_ref[...]).astype(o_ref.dtype)

    blk = pl.BlockSpec((None, tr, ncol), lambda s, i: (s, i, 0))
    return pl.pallas_call(
        body, name=name, grid=(n, nb),
        in_specs=[pl.BlockSpec((None, tr, ncol), lambda s, i: (s, lax.axis_index("c") * nb + i, 0)), blk],
        out_specs=blk,
        out_shape=jax.ShapeDtypeStruct((n, hr, ncol), BF16),
        compiler_params=_cp("parallel", "parallel"),
    )(g, got)


def _scatter_to_chips(a, *, name):
    n, r, ncol = a.shape
    parts = _parts(r)

    def body(a_ref, got_ref, send_sems, recv_sems):
        x, y, c = _place()
        for st, sz in parts:
            for j, chip in enumerate(_other_chips(x, y)):
                _remote(a_ref.at[2 * chip[0] + chip[1], pl.ds(st, sz), :], got_ref.at[j, pl.ds(st, sz), :], send_sems,
                        recv_sems, j, (*chip, c)).start()
        for j in range(n - 1):
            _remote(got_ref.at[j], got_ref.at[j], send_sems, recv_sems, j, (x, y, c)).wait()

    return pl.pallas_call(
        body, name=name, out_shape=jax.ShapeDtypeStruct((n - 1, r, ncol), a.dtype), in_specs=[ANY], out_specs=ANY,
        scratch_shapes=[pltpu.SemaphoreType.DMA((3,)), pltpu.SemaphoreType.DMA((3,))],
    )(a)


def _add_chips(a, got, *, name):
    n, hr, ncol = a.shape
    tr = _pick(hr, 512, 16)
    nb = hr // tr

    def body(a_ref, g0_ref, g1_ref, g2_ref, o_ref):
        f = lambda ref: ref[...].astype(F32)
        o_ref[...] = ((f(a_ref) + f(g0_ref)) + f(g1_ref)) + f(g2_ref)

    other = lambda j: pl.BlockSpec((None, tr, ncol), lambda i: (j, i, 0))
    return pl.pallas_call(
        body, name=name, grid=(nb,),
        in_specs=[pl.BlockSpec((None, tr, ncol), lambda i: (_my_chip(), i, 0)), other(0), other(1), other(2)],
        out_specs=pl.BlockSpec((tr, ncol), lambda i: (lax.axis_index("c") * nb + i, 0)),
        out_shape=jax.ShapeDtypeStruct((2 * hr, ncol), F32),
        compiler_params=_cp("parallel"),
    )(a, got, got, got)


def _join_halves(buf, *, name):
    r, ncol = buf.shape
    hr = r // 2
    parts = _parts(hr)

    def body(b_ref, out_ref, send_sems, recv_sems):
        del b_ref
        x, y, c = _place()
        for st, sz in parts:
            _remote(_rows(out_ref, (), c * hr, st, sz), _rows(out_ref, (), c * hr, st, sz), send_sems, recv_sems, 0,
                    (x, y, 1 - c)).start()
        theirs = _rows(out_ref, (), (1 - c) * hr, 0, hr)
        _remote(theirs, theirs, send_sems, recv_sems, 0, (x, y, c)).wait()

    return pl.pallas_call(
        body, name=name, out_shape=jax.ShapeDtypeStruct((r, ncol), buf.dtype), in_specs=[ANY], out_specs=ANY,
        input_output_aliases={0: 0},
        scratch_shapes=[pltpu.SemaphoreType.DMA((1,)), pltpu.SemaphoreType.DMA((1,))],
    )(buf)


def _reduce_scatter(g, *, name):
    got = _swap_halves(g, name=name + "_swap")
    chip_sum = _add_cores(g, got, name=name + "_add_cores")
    others = _scatter_to_chips(chip_sum, name=name + "_scatter")
    return _join_halves(_add_chips(chip_sum, others, name=name + "_add_chips"), name=name + "_join")


def _all_gather_small(v, *, name):
    r, ncol = v.shape

    def body(x_ref, out_ref, send_sems, recv_sems, local_sem):
        x, y, c = _place()
        me, sibling = (x, y, c), (x, y, 1 - c)
        chips = _other_chips(x, y)

        def slot(p):
            return out_ref.at[4 * p[0] + 2 * p[1] + p[2]]

        mine = pltpu.make_async_copy(x_ref, slot(me), local_sem.at[0])
        mine.start()
        first = [_remote(x_ref, slot(me), send_sems, recv_sems, 0, sibling)]
        first += [_remote(x_ref, slot(me), send_sems, recv_sems, 1 + j, (*chip, c)) for j, chip in enumerate(chips)]
        for cp in first:
            cp.start()
        passed = [_remote(slot((*chip, c)), slot((*chip, c)), send_sems, recv_sems, 4 + j, sibling)
                  for j, chip in enumerate(chips)]
        for j, chip in enumerate(chips):
            _remote(slot((*chip, c)), slot((*chip, c)), send_sems, recv_sems, 1 + j, me).wait_recv()
            passed[j].start()
        _remote(slot(sibling), slot(sibling), send_sems, recv_sems, 0, me).wait_recv()
        for j, chip in enumerate(chips):
            _remote(slot((*chip, 1 - c)), slot((*chip, 1 - c)), send_sems, recv_sems, 4 + j, me).wait_recv()
        for cp in first + passed:
            cp.wait_send()
        mine.wait()

    vm = pl.BlockSpec(memory_space=pltpu.VMEM)
    return pl.pallas_call(
        body, name=name, out_shape=jax.ShapeDtypeStruct((N_DEV, r, ncol), v.dtype), in_specs=[vm], out_specs=vm,
        scratch_shapes=[pltpu.SemaphoreType.DMA((7,)), pltpu.SemaphoreType.DMA((7,)), pltpu.SemaphoreType.DMA((1,))],
    )(v)


def _sum_slots(g8, *, name):
    n, r, ncol = g8.shape

    def body(g_ref, o_ref):
        acc = g_ref[0]
        for k in range(1, n):
            acc = acc + g_ref[k]
        o_ref[...] = acc

    return pl.pallas_call(body, name=name, out_shape=jax.ShapeDtypeStruct((r, ncol), g8.dtype))(g8)


def _step(a):
    x = a["x"][0]
    s = x.shape[0]
    del s
    shard_shapes = [a[n].shape for n in SHARDED]
    rows = _pack_rows(shard_shapes)

    packed = _pack([a[n] for n in SHARDED], rows, BF16)
    gathered = _gather_shards(packed, _own_slot(packed, name="own_weights"), name="gather_weights")
    conv_rows = -(-math.prod(a["conv_w"].shape) // (LANE * 8)) * 8
    conv_all = _all_gather_small(
        jnp.pad(a["conv_w"].reshape(-1), (0, conv_rows * LANE - math.prod(a["conv_w"].shape))).reshape(conv_rows, LANE),
        name="gather_conv_w")
    per_chip = [dict(zip(SHARDED, _unpack(gathered[k], shard_shapes))) for k in range(N_CHIPS)]
    full = {n: jnp.concatenate([per_chip[k][n] for k in range(N_CHIPS)], axis=SHARD_AXIS[n]) for n in SHARDED}
    full["conv_w"] = jnp.concatenate(
        [conv_all[2 * k].reshape(-1)[:math.prod(a["conv_w"].shape)].reshape(a["conv_w"].shape) for k in range(N_CHIPS)],
        axis=SHARD_AXIS["conv_w"])
    for n in REPLICATED:
        full[n] = a[n]

    loss_part, grad_x, grads = _local_step(x, a["positions"][0], a["loss_target"][0], full)
    loss = lax.psum(jnp.sum(loss_part), ("x", "y", "c"))

    slots = []
    for k in range(N_CHIPS):
        parts = [jnp.split(grads[n], N_CHIPS, axis=SHARD_AXIS[n])[k] for n in SHARDED]
        slots.append(_pack(parts, rows, F32))
    g_shard = _reduce_scatter(jnp.stack(slots), name="reduce_grads")

    rep_shapes = [a[n].shape for n in REPLICATED]
    n_rep = sum(math.prod(sh) for sh in rep_shapes)
    rep_rows = -(-n_rep // (LANE * 8)) * 8
    pack_small = lambda arrs: jnp.pad(jnp.concatenate([t.reshape(-1) for t in arrs]), (0, rep_rows * LANE - n_rep)).reshape(rep_rows, LANE)
    g_rep = _sum_slots(_all_gather_small(pack_small([grads[n] for n in REPLICATED]), name="gather_small_grads"),
                       name="add_small_grads")

    out = {"loss": loss, "grad_x": grad_x[None]}
    for n, g in zip(SHARDED, _unpack(g_shard, shard_shapes)):
        flat = lambda t: t.reshape(-1, t.shape[-1])
        d, nm, nv = _adamw(flat(a[n]), flat(g), flat(a["m_" + n]), flat(a["v_" + n]), name="adamw_" + n)
        out["grad_" + n] = g
        out["delta_" + n], out["new_m_" + n], out["new_v_" + n] = (t.reshape(g.shape) for t in (d, nm, nv))
    d_rp, m_rp, v_rp = _adamw(pack_small([a[n] for n in REPLICATED]), g_rep,
                              pack_small([a["m_" + n] for n in REPLICATED]),
                              pack_small([a["v_" + n] for n in REPLICATED]), name="adamw_replicated")
    for prefix, rp_arr in (("grad_", g_rep), ("delta_", d_rp), ("new_m_", m_rp), ("new_v_", v_rp)):
        for n, t in zip(REPLICATED, _unpack(rp_arr.reshape(-1)[:n_rep], rep_shapes)):
            out[prefix + n] = t
    return out


IN_NAMES = ["x", "positions"] + W_NAMES + ["loss_target"] + ["m_" + n for n in W_NAMES] + ["v_" + n for n in W_NAMES]
OUT_NAMES = (["loss", "grad_x"] + ["grad_" + n for n in W_NAMES] + ["delta_" + n for n in W_NAMES]
             + ["new_m_" + n for n in W_NAMES] + ["new_v_" + n for n in W_NAMES])


def kernel(x, positions, ln_ffn1, ffn1_w13, ffn1_w2, ln_mix, w_in, conv_w, conv_b, dt_bias, a_log, d_skip, ssd_norm, w_ssd_out, q_lora_norm, w_uq, kv_lora_norm, w_ukv, q_norm, k_norm, w_mla_out, w_o, ln_ffn2, ffn2_w13, ffn2_w2, loss_target, m_ln_ffn1, m_ffn1_w13, m_ffn1_w2, m_ln_mix, m_w_in, m_conv_w, m_conv_b, m_dt_bias, m_a_log, m_d_skip, m_ssd_norm, m_w_ssd_out, m_q_lora_norm, m_w_uq, m_kv_lora_norm, m_w_ukv, m_q_norm, m_k_norm, m_w_mla_out, m_w_o, m_ln_ffn2, m_ffn2_w13, m_ffn2_w2, v_ln_ffn1, v_ffn1_w13, v_ffn1_w2, v_ln_mix, v_w_in, v_conv_w, v_conv_b, v_dt_bias, v_a_log, v_d_skip, v_ssd_norm, v_w_ssd_out, v_q_lora_norm, v_w_uq, v_kv_lora_norm, v_w_ukv, v_q_norm, v_k_norm, v_w_mla_out, v_w_o, v_ln_ffn2, v_ffn2_w13, v_ffn2_w2):
    given = locals()
    out = _step({n: given[n] for n in IN_NAMES})
    return tuple(out[n] for n in OUT_NAMES)
```

```python
import functools
import math

import jax
import jax.numpy as jnp
from jax import lax
from jax.experimental import pallas as pl
from jax.experimental.pallas import tpu as pltpu

F32 = jnp.float32
BF16 = jnp.bfloat16

D_MODEL = 1024
DEPTH = 2
D_FF = 2816
SSD_D_INNER = 2048
SSD_HEADS = 32
SSD_HEAD_DIM = 64
SSD_GROUPS = 4
SSD_STATE = 128
SSD_CHUNK = 128
SSD_CONV = 4
SSD_CONV_DIM = 3072
MLA_HEADS = 8
MLA_Q_LORA = 512
MLA_KV_LORA = 256
MLA_NOPE = 128
MLA_ROPE = 64
MLA_V = 128
MLA_QK = 192
ROPE_THETA = 10000.0
EPS = 1e-6
ADAM_LR = 0.001
ADAM_B1 = 0.9
ADAM_B2 = 0.999
ADAM_EPS = 1e-08
ADAM_WD = 0.01
ADAM_STEP = 10

PROJ_W = 8064
OFF_Z, OFF_XBC, OFF_GATES, OFF_CQ, OFF_CKV, OFF_KRDT = 0, 2048, 5120, 7168, 7680, 7936

LANE = 128
VMEM_LIMIT = 48 * 1024 * 1024
HI = lax.Precision.HIGHEST


def _cp(*sem):
    return pltpu.CompilerParams(dimension_semantics=sem, vmem_limit_bytes=VMEM_LIMIT)


def _pick(dim, target, align):
    if dim <= target:
        return dim
    b = (target // align) * align
    while b >= align:
        if dim % b == 0:
            return b
        b -= align
    raise ValueError(f"no block for {dim} (target {target}, align {align})")


def _silu(x):
    return x * jax.nn.sigmoid(x)


def _dsilu(x):
    s = jax.nn.sigmoid(x)
    return s * (1.0 + x * (1.0 - s))


MM_VMEM_BUDGET = 40 * 1024 * 1024


def _mm_tiles(m, n, k, a_bytes, b_bytes, o_bytes):
    bn = _pick(n, 1408, LANE)
    for nk in (1, 2, 3, 4, 6, 7, 8):
        if k % nk or (k // nk) % LANE:
            continue
        bk = k // nk
        for bm in (1024, 512):
            if m % bm:
                continue
            need = 2 * (bm * bk * a_bytes + bk * bn * b_bytes + bm * bn * o_bytes) + (bm * bn * 4 if nk > 1 else 0)
            if need <= MM_VMEM_BUDGET:
                return bm, bn, bk
    return _pick(m, 512, 8), bn, _pick(k, 1536, LANE)

def _matmul(a, b, mode, *, name, out_dtype=F32, scale=1.0, res=None):
    if mode == "nn":
        (m, k), (k2, n) = a.shape, b.shape
    elif mode == "nt":
        (m, k), (n, k2) = a.shape, b.shape
    else:
        (k, m), (k2, n) = a.shape, b.shape
    assert k == k2, (a.shape, b.shape, mode)
    if mode == "tn":
        bn, bk = _pick(n, 2816, LANE), _pick(k, 1024, 8)
        bm = _pick(m, max(256, (1408 * 1024 // bn) // LANE * LANE), LANE)
    else:
        bm, bn, bk = _mm_tiles(m, n, k, a.dtype.itemsize, b.dtype.itemsize,
                               jnp.dtype(out_dtype).itemsize + (4 if res is not None else 0))
    nk = k // bk

    def body(a_ref, b_ref, *rest):
        res_ref = rest[0] if res is not None else None
        o_ref = rest[-2] if nk > 1 else rest[-1]
        kk = pl.program_id(2)
        av = a_ref[...].astype(BF16)
        bv = b_ref[...].astype(BF16)
        if mode == "nn":
            dims = (((1,), (0,)), ((), ()))
        elif mode == "nt":
            dims = (((1,), (1,)), ((), ()))
        else:
            dims = (((0,), (0,)), ((), ()))
        part = lax.dot_general(av, bv, dims, preferred_element_type=F32)

        def finish(total):
            out = total * scale
            if res_ref is not None:
                out = res_ref[...] + out
            o_ref[...] = out.astype(o_ref.dtype)

        if nk == 1:
            finish(part)
            return
        acc_ref = rest[-1]

        @pl.when(kk == 0)
        def _():
            acc_ref[...] = part

        @pl.when((kk > 0) & (kk < nk - 1))
        def _():
            acc_ref[...] += part

        @pl.when(kk == nk - 1)
        def _():
            finish(acc_ref[...] + part)

    o_spec = pl.BlockSpec((bm, bn), lambda i, j, kk: (i, j))
    if mode == "nn":
        a_spec = pl.BlockSpec((bm, bk), lambda i, j, kk: (i, kk))
        b_spec = pl.BlockSpec((bk, bn), lambda i, j, kk: (kk, j))
    elif mode == "nt":
        a_spec = pl.BlockSpec((bm, bk), lambda i, j, kk: (i, kk))
        b_spec = pl.BlockSpec((bn, bk), lambda i, j, kk: (j, kk))
    else:
        a_spec = pl.BlockSpec((bk, bm), lambda i, j, kk: (kk, i))
        b_spec = pl.BlockSpec((bk, bn), lambda i, j, kk: (kk, j))
    return pl.pallas_call(
        body, name=name,
        grid=(m // bm, n // bn, nk),
        in_specs=[a_spec, b_spec] + ([o_spec] if res is not None else []),
        out_specs=o_spec,
        out_shape=jax.ShapeDtypeStruct((m, n), out_dtype),
        scratch_shapes=[pltpu.VMEM((bm, bn), F32)] if nk > 1 else [],
        compiler_params=_cp("parallel", "parallel", "arbitrary"),
    )(*((a, b) + ((res,) if res is not None else ())))


def _rms_fwd(x, g, *, name, col=0, width=None):
    r = x.shape[0]
    w = width or x.shape[1]
    tr = _pick(r, 512, 16)

    def body(x_ref, g_ref, o_ref):
        xv = x_ref[...]
        rs = lax.rsqrt(jnp.mean(xv * xv, axis=-1, keepdims=True) + EPS)
        o_ref[...] = (xv * rs * g_ref[...]).astype(o_ref.dtype)

    return pl.pallas_call(
        body, name=name, grid=(r // tr,),
        in_specs=[pl.BlockSpec((tr, w), lambda i: (i, col)), pl.BlockSpec((1, w), lambda i: (0, 0))],
        out_specs=pl.BlockSpec((tr, w), lambda i: (i, 0)),
        out_shape=jax.ShapeDtypeStruct((r, w), BF16),
        compiler_params=_cp("parallel"),
    )(x, g)


def _rms_bwd(x, g, dy, *, name, col=0, width=None, res=None, out_dtype=F32):
    r = x.shape[0]
    w = width or x.shape[1]
    tr = _pick(r, 512, 8)

    def body(x_ref, g_ref, dy_ref, *rest):
        res_ref = rest[0] if res is not None else None
        dx_ref, dg_ref = rest[-2:]
        i = pl.program_id(0)
        xv = x_ref[...]
        dyv = dy_ref[...]
        rs = lax.rsqrt(jnp.mean(xv * xv, axis=-1, keepdims=True) + EPS)
        xh = xv * rs
        dxh = dyv * g_ref[...]
        mm = jnp.mean(dxh * xh, axis=-1, keepdims=True)
        dx = rs * (dxh - xh * mm)
        if res_ref is not None:
            dx = res_ref[...] + dx
        dx_ref[...] = dx.astype(dx_ref.dtype)
        part = jnp.sum(dyv * xh, axis=0, keepdims=True)

        @pl.when(i == 0)
        def _():
            dg_ref[...] = part

        @pl.when(i > 0)
        def _():
            dg_ref[...] += part

    blk = pl.BlockSpec((tr, w), lambda i: (i, 0))
    return pl.pallas_call(
        body, name=name, grid=(r // tr,),
        in_specs=[pl.BlockSpec((tr, w), lambda i: (i, col)), pl.BlockSpec((1, w), lambda i: (0, 0)), blk]
        + ([blk] if res is not None else []),
        out_specs=[blk, pl.BlockSpec((1, w), lambda i: (0, 0))],
        out_shape=[jax.ShapeDtypeStruct((r, w), out_dtype), jax.ShapeDtypeStruct((1, w), F32)],
        compiler_params=_cp("arbitrary"),
    )(*((x, g, dy) + ((res,) if res is not None else ())))


def _gated_rms_fwd(y, proj, g, *, name):
    r, w = y.shape
    tr = _pick(r, 256, 8)

    def body(y_ref, z_ref, g_ref, o_ref):
        t = y_ref[...] * _silu(z_ref[...])
        rs = lax.rsqrt(jnp.mean(t * t, axis=-1, keepdims=True) + EPS)
        o_ref[...] = (t * rs * g_ref[...]).astype(o_ref.dtype)

    return pl.pallas_call(
        body, name=name, grid=(r // tr,),
        in_specs=[pl.BlockSpec((tr, w), lambda i: (i, 0)), pl.BlockSpec((tr, w), lambda i: (i, OFF_Z // w)),
                  pl.BlockSpec((1, w), lambda i: (0, 0))],
        out_specs=pl.BlockSpec((tr, w), lambda i: (i, 0)),
        out_shape=jax.ShapeDtypeStruct((r, w), BF16),
        compiler_params=_cp("parallel"),
    )(y, proj, g)


def _gated_rms_bwd(y, proj, g, do, *, name):
    r, w = y.shape
    tr = _pick(r, 256, 8)

    def body(y_ref, z_ref, g_ref, do_ref, dy_ref, dz_ref, dg_ref):
        i = pl.program_id(0)
        yv, zv, dov = y_ref[...], z_ref[...], do_ref[...]
        sz = _silu(zv)
        t = yv * sz
        rs = lax.rsqrt(jnp.mean(t * t, axis=-1, keepdims=True) + EPS)
        th = t * rs
        dth = dov * g_ref[...]
        mm = jnp.mean(dth * th, axis=-1, keepdims=True)
        dt = rs * (dth - th * mm)
        dy_ref[...] = dt * sz
        dz_ref[...] = (dt * yv * _dsilu(zv)).astype(dz_ref.dtype)
        part = jnp.sum(dov * th, axis=0, keepdims=True)

        @pl.when(i == 0)
        def _():
            dg_ref[...] = part

        @pl.when(i > 0)
        def _():
            dg_ref[...] += part

    blk = pl.BlockSpec((tr, w), lambda i: (i, 0))
    vec = pl.BlockSpec((1, w), lambda i: (0, 0))
    return pl.pallas_call(
        body, name=name, grid=(r // tr,),
        in_specs=[blk, pl.BlockSpec((tr, w), lambda i: (i, OFF_Z // w)), vec, blk],
        out_specs=[blk, blk, vec],
        out_shape=[jax.ShapeDtypeStruct((r, w), F32), jax.ShapeDtypeStruct((r, w), BF16),
                   jax.ShapeDtypeStruct((1, w), F32)],
        compiler_params=_cp("arbitrary"),
    )(y, proj, g, do)


def _ffn_up(n, w13, *, name):
    m, k = n.shape
    f = w13.shape[1] // 2
    bm, bn = _pick(m, 1024, 16), _pick(f, 1408, LANE)
    nj = f // bn

    def body(a_ref, wg_ref, wu_ref, act_ref, g_ref, u_ref):
        a = a_ref[...].astype(BF16)
        g = _dot(a, wg_ref[...].astype(BF16))
        u = _dot(a, wu_ref[...].astype(BF16))
        act_ref[...] = (_silu(g) * u).astype(act_ref.dtype)
        g_ref[...] = g.astype(g_ref.dtype)
        u_ref[...] = u.astype(u_ref.dtype)

    out = pl.BlockSpec((bm, bn), lambda j, i: (i, j))
    sh = jax.ShapeDtypeStruct((m, f), BF16)
    return pl.pallas_call(
        body, name=name, grid=(nj, m // bm),
        in_specs=[pl.BlockSpec((bm, k), lambda j, i: (i, 0)), pl.BlockSpec((k, bn), lambda j, i: (0, j)),
                  pl.BlockSpec((k, bn), lambda j, i: (0, nj + j))],
        out_specs=[out, out, out], out_shape=[sh, sh, sh],
        compiler_params=_cp("parallel", "parallel"),
    )(n, w13, w13)


def _swiglu_bwd(g, u, da, *, name):
    r, f = g.shape
    tr = _pick(r, 256, 16)

    def body(g_ref, u_ref, da_ref, o_ref):
        gv, uv, dav = g_ref[...].astype(F32), u_ref[...].astype(F32), da_ref[...].astype(F32)
        o_ref[:, :f] = (dav * uv * _dsilu(gv)).astype(o_ref.dtype)
        o_ref[:, f:] = (dav * _silu(gv)).astype(o_ref.dtype)

    blk = pl.BlockSpec((tr, f), lambda i: (i, 0))
    return pl.pallas_call(
        body, name=name, grid=(r // tr,),
        in_specs=[blk, blk, blk],
        out_specs=pl.BlockSpec((tr, 2 * f), lambda i: (i, 0)),
        out_shape=jax.ShapeDtypeStruct((r, 2 * f), BF16),
        compiler_params=_cp("parallel"),
    )(g, u, da)


CONV_TS = 1024
CONV_TC = 512


def _conv_pre(x, carry, w_ref, b_ref):
    ts = x.shape[0]
    row8 = lax.broadcasted_iota(jnp.int32, (8, x.shape[1]), 0)
    head_x = x[0:8]
    shifted, shifted_head = [], []
    for j in range(SSD_CONV):
        if j == 0:
            shifted.append(x)
            shifted_head.append(head_x)
        else:
            shifted.append(pltpu.roll(x, j, 0))
            shifted_head.append(jnp.where(row8 < j, pltpu.roll(carry, j, 0), pltpu.roll(head_x, j, 0)))
    pre = b_ref[...] + sum(w_ref[SSD_CONV - 1 - j:SSD_CONV - j, :] * shifted[j] for j in range(SSD_CONV))
    pre_head = b_ref[...] + sum(w_ref[SSD_CONV - 1 - j:SSD_CONV - j, :] * shifted_head[j] for j in range(SSD_CONV))
    del ts
    return pre, pre_head, shifted, shifted_head


def _conv_fwd(proj, w, b, *, name):
    s = proj.shape[0]
    c = w.shape[1]
    ts, tc = _pick(s, CONV_TS, 8), CONV_TC
    off = OFF_XBC // tc

    def body(x_ref, w_ref, b_ref, o_ref, carry_ref):
        t = pl.program_id(1)

        @pl.when(t == 0)
        def _():
            carry_ref[...] = jnp.zeros_like(carry_ref)

        x = x_ref[...]
        pre, pre_head, _, _ = _conv_pre(x, carry_ref[...], w_ref, b_ref)
        o_ref[...] = _silu(pre)
        o_ref[0:8, :] = _silu(pre_head)
        carry_ref[...] = x[ts - 8:ts]

    return pl.pallas_call(
        body, name=name, grid=(c // tc, s // ts),
        in_specs=[pl.BlockSpec((ts, tc), lambda j, t: (t, j + off)), pl.BlockSpec((SSD_CONV, tc), lambda j, t: (0, j)),
                  pl.BlockSpec((1, tc), lambda j, t: (0, j))],
        out_specs=pl.BlockSpec((ts, tc), lambda j, t: (t, j)),
        out_shape=jax.ShapeDtypeStruct((s, c), F32),
        scratch_shapes=[pltpu.VMEM((8, tc), F32)],
        compiler_params=_cp("parallel", "arbitrary"),
    )(proj, w, b)


def _conv_bwd_pre(proj, w, b, dy, *, name):
    s = proj.shape[0]
    c = w.shape[1]
    ts, tc = _pick(s, CONV_TS, 8), CONV_TC
    off = OFF_XBC // tc

    def body(x_ref, w_ref, b_ref, dy_ref, dp_ref, dw_ref, db_ref, carry_ref):
        t = pl.program_id(1)

        @pl.when(t == 0)
        def _():
            carry_ref[...] = jnp.zeros_like(carry_ref)
            dw_ref[...] = jnp.zeros_like(dw_ref)
            db_ref[...] = jnp.zeros_like(db_ref)

        x = x_ref[...]
        pre, pre_head, shifted, shifted_head = _conv_pre(x, carry_ref[...], w_ref, b_ref)
        dyv = dy_ref[...]
        dp = dyv * _dsilu(pre)
        dp_head = dyv[0:8] * _dsilu(pre_head)
        row = lax.broadcasted_iota(jnp.int32, dp.shape, 0)
        dp_tail = jnp.where(row >= 8, dp, 0.0)
        dp_ref[...] = dp
        dp_ref[0:8, :] = dp_head
        db_ref[...] += jnp.sum(dp_tail, axis=0, keepdims=True) + jnp.sum(dp_head, axis=0, keepdims=True)
        for j in range(SSD_CONV):
            kk = SSD_CONV - 1 - j
            dw_ref[kk:kk + 1, :] += (jnp.sum(dp_tail * shifted[j], axis=0, keepdims=True)
                                     + jnp.sum(dp_head * shifted_head[j], axis=0, keepdims=True))
        carry_ref[...] = x[ts - 8:ts]

    return pl.pallas_call(
        body, name=name, grid=(c // tc, s // ts),
        in_specs=[pl.BlockSpec((ts, tc), lambda j, t: (t, j + off)), pl.BlockSpec((SSD_CONV, tc), lambda j, t: (0, j)),
                  pl.BlockSpec((1, tc), lambda j, t: (0, j)), pl.BlockSpec((ts, tc), lambda j, t: (t, j))],
        out_specs=[pl.BlockSpec((ts, tc), lambda j, t: (t, j)), pl.BlockSpec((SSD_CONV, tc), lambda j, t: (0, j)),
                   pl.BlockSpec((1, tc), lambda j, t: (0, j))],
        out_shape=[jax.ShapeDtypeStruct((s, c), F32), jax.ShapeDtypeStruct((SSD_CONV, c), F32),
                   jax.ShapeDtypeStruct((1, c), F32)],
        scratch_shapes=[pltpu.VMEM((8, tc), F32)],
        compiler_params=_cp("parallel", "arbitrary"),
    )(proj, w, b, dy)


def _conv_bwd_x(dp, w, *, name):
    s, c = dp.shape
    ts, tc = _pick(s, CONV_TS, 8), CONV_TC
    nt = s // ts

    def body(d_ref, w_ref, o_ref, carry_ref, full_ref):
        t = pl.program_id(1)

        @pl.when(t == 0)
        def _():
            carry_ref[...] = jnp.zeros_like(carry_ref)

        d = d_ref[...]
        carry = carry_ref[...]
        row8 = lax.broadcasted_iota(jnp.int32, (8, tc), 0)
        tail = d[ts - 8:ts]
        acc = w_ref[SSD_CONV - 1:SSD_CONV, :] * d
        acc_tail = w_ref[SSD_CONV - 1:SSD_CONV, :] * tail
        for j in range(1, SSD_CONV):
            wj = w_ref[SSD_CONV - 1 - j:SSD_CONV - j, :]
            acc = acc + wj * pltpu.roll(d, ts - j, 0)
            up_tail = jnp.where(row8 >= 8 - j, pltpu.roll(carry, 8 - j, 0), pltpu.roll(tail, 8 - j, 0))
            acc_tail = acc_tail + wj * up_tail
        full_ref[...] = acc
        full_ref[ts - 8:ts, :] = acc_tail
        o_ref[...] = full_ref[...].astype(o_ref.dtype)
        carry_ref[...] = d[0:8]

    return pl.pallas_call(
        body, name=name, grid=(c // tc, nt),
        in_specs=[pl.BlockSpec((ts, tc), lambda j, t: (nt - 1 - t, j)), pl.BlockSpec((SSD_CONV, tc), lambda j, t: (0, j))],
        out_specs=pl.BlockSpec((ts, tc), lambda j, t: (nt - 1 - t, j)),
        out_shape=jax.ShapeDtypeStruct((s, c), BF16),
        scratch_shapes=[pltpu.VMEM((8, tc), F32), pltpu.VMEM((ts, tc), F32)],
        compiler_params=_cp("parallel", "arbitrary"),
    )(dp, w)


def _merge_fwd(proj, ys, ym, *, name):
    r, w = ys.shape
    tr = _pick(r, 512, 8)
    off = OFF_GATES // w

    def body(g1_ref, g2_ref, ys_ref, ym_ref, o_ref):
        o_ref[...] = (jax.nn.sigmoid(g1_ref[...]) * ys_ref[...]
                      + jax.nn.sigmoid(g2_ref[...]) * ym_ref[...]).astype(o_ref.dtype)

    blk = pl.BlockSpec((tr, w), lambda i: (i, 0))
    return pl.pallas_call(
        body, name=name, grid=(r // tr,),
        in_specs=[pl.BlockSpec((tr, w), lambda i: (i, off)), pl.BlockSpec((tr, w), lambda i: (i, off + 1)), blk, blk],
        out_specs=blk, out_shape=jax.ShapeDtypeStruct((r, w), BF16),
        compiler_params=_cp("parallel"),
    )(proj, proj, ys, ym)


def _merge_bwd(proj, ys, ym, dm, *, name):
    r, w = ys.shape
    tr = _pick(r, 512, 8)
    off = OFF_GATES // w

    def body(g1_ref, g2_ref, ys_ref, ym_ref, dm_ref, dg_ref, dys_ref, dym_ref):
        s1, s2 = jax.nn.sigmoid(g1_ref[...]), jax.nn.sigmoid(g2_ref[...])
        dmv = dm_ref[...]
        dys_ref[...] = (dmv * s1).astype(dys_ref.dtype)
        dym_ref[...] = (dmv * s2).astype(dym_ref.dtype)
        dg_ref[:, :w] = (dmv * ys_ref[...] * s1 * (1.0 - s1)).astype(dg_ref.dtype)
        dg_ref[:, w:] = (dmv * ym_ref[...] * s2 * (1.0 - s2)).astype(dg_ref.dtype)

    blk = pl.BlockSpec((tr, w), lambda i: (i, 0))
    return pl.pallas_call(
        body, name=name, grid=(r // tr,),
        in_specs=[pl.BlockSpec((tr, w), lambda i: (i, off)), pl.BlockSpec((tr, w), lambda i: (i, off + 1)), blk, blk, blk],
        out_specs=[pl.BlockSpec((tr, 2 * w), lambda i: (i, 0)), blk, blk],
        out_shape=[jax.ShapeDtypeStruct((r, 2 * w), BF16), jax.ShapeDtypeStruct((r, w), BF16),
                   jax.ShapeDtypeStruct((r, w), BF16)],
        compiler_params=_cp("parallel"),
    )(proj, proj, ys, ym, dm)


def _loss_fwd_bwd(y, target, *, name):
    r, w = y.shape
    tr = _pick(r, 512, 8)

    def body(y_ref, t_ref, l_ref, dy_ref):
        i = pl.program_id(0)
        e = y_ref[...] - t_ref[...]
        dy_ref[...] = e * (1.0 / w)
        part = jnp.sum(e * e, axis=0, keepdims=True) * (0.5 / w)

        @pl.when(i == 0)
        def _():
            l_ref[...] = part

        @pl.when(i > 0)
        def _():
            l_ref[...] += part

    blk = pl.BlockSpec((tr, w), lambda i: (i, 0))
    return pl.pallas_call(
        body, name=name, grid=(r // tr,),
        in_specs=[blk, blk],
        out_specs=[pl.BlockSpec((1, w), lambda i: (0, 0)), blk],
        out_shape=[jax.ShapeDtypeStruct((1, w), F32), jax.ShapeDtypeStruct((r, w), F32)],
        compiler_params=_cp("arbitrary"),
    )(y, target)


def _adamw(w, g, m, v, *, name):
    r, c = w.shape
    tr = _pick(r, max(8, (1 << 20) // (4 * c) // 8 * 8), 8)
    c1 = 1.0 - ADAM_B1 ** ADAM_STEP
    c2 = 1.0 - ADAM_B2 ** ADAM_STEP

    def body(w_ref, g_ref, m_ref, v_ref, d_ref, nm_ref, nv_ref):
        gv = g_ref[...]
        nm = ADAM_B1 * m_ref[...] + (1.0 - ADAM_B1) * gv
        nv = ADAM_B2 * v_ref[...] + (1.0 - ADAM_B2) * (gv * gv)
        nm_ref[...] = nm
        nv_ref[...] = nv
        d_ref[...] = -ADAM_LR * ((nm / c1) / (jnp.sqrt(nv / c2) + ADAM_EPS) + ADAM_WD * w_ref[...])

    blk = pl.BlockSpec((tr, c), lambda i: (i, 0))
    sh = jax.ShapeDtypeStruct((r, c), F32)
    return pl.pallas_call(
        body, name=name, grid=(r // tr,),
        in_specs=[blk] * 4, out_specs=[blk] * 3, out_shape=[sh] * 3,
        compiler_params=_cp("parallel"),
    )(w, g, m, v)


def _softplus(x):
    return jnp.maximum(x, 0.0) + jnp.log(1.0 + jnp.exp(-jnp.abs(x)))


def _dot_01(x, sel):
    sel_b = sel.astype(BF16)
    acc, rem = None, x
    for _ in range(3):
        piece = rem.astype(BF16)
        part = jnp.dot(piece, sel_b, preferred_element_type=F32)
        acc = part if acc is None else acc + part
        rem = rem - piece.astype(F32)
    return acc


def _ssd_common(dtr_ref, dtrT_ref, dtb_ref, dtbT_ref, al_ref, alT_ref, e_ref):
    L = SSD_CHUNK
    ri = lax.broadcasted_iota(jnp.int32, (L, L), 0)
    cj = lax.broadcasted_iota(jnp.int32, (L, L), 1)
    tril = (ri >= cj).astype(F32)
    triu = (ri <= cj).astype(F32)
    a = -jnp.exp(al_ref[...])
    aT = -jnp.exp(alT_ref[...])
    pre = dtr_ref[...] + dtb_ref[...]
    preT = dtrT_ref[...] + dtbT_ref[...]
    dt = _softplus(pre)
    dtT = _softplus(preT)
    acum = jnp.dot(tril, dt * a, precision=HI, preferred_element_type=F32)
    acumT = jnp.dot(dtT * aT, triu, precision=HI, preferred_element_type=F32)
    e = e_ref[...]
    dt_x = _dot_01(dt, e)
    acum_x = _dot_01(acum, e)
    last_x = acum_x[L - 1:L, :]
    return dict(ri=ri, cj=cj, tril=tril, triu=triu, a=a, aT=aT, pre=pre, preT=preT, dt=dt, dtT=dtT,
                acum=acum, acumT=acumT, dt_x=dt_x, eacum_x=jnp.exp(acum_x), w_x=jnp.exp(last_x - acum_x),
                elast_x=jnp.exp(last_x))


def _dot_nt(a, b):
    return lax.dot_general(a, b, (((1,), (1,)), ((), ())), preferred_element_type=F32)


def _dot_tn(a, b):
    return lax.dot_general(a, b, (((0,), (0,)), ((), ())), preferred_element_type=F32)


def _dot(a, b):
    return jnp.dot(a, b, preferred_element_type=F32)


def _ssd_specs(nc, rev):
    L = SSD_CHUNK
    ix = (lambda c: nc - 1 - c) if rev else (lambda c: c)
    return [
        pl.BlockSpec((L, SSD_D_INNER), lambda c: (ix(c), 0)),
        pl.BlockSpec((L, 512), lambda c: (ix(c), 4)),
        pl.BlockSpec((L, 512), lambda c: (ix(c), 5)),
        pl.BlockSpec((L, SSD_HEADS), lambda c: (ix(c), 0)),
        pl.BlockSpec((SSD_HEADS, L), lambda c: (0, ix(c))),
        pl.BlockSpec((1, SSD_HEADS), lambda c: (0, 0)),
        pl.BlockSpec((SSD_HEADS, 1), lambda c: (0, 0)),
        pl.BlockSpec((1, SSD_HEADS), lambda c: (0, 0)),
        pl.BlockSpec((SSD_HEADS, 1), lambda c: (0, 0)),
        pl.BlockSpec((1, SSD_D_INNER), lambda c: (0, 0)),
        pl.BlockSpec((SSD_HEADS, SSD_D_INNER), lambda c: (0, 0)),
    ]


def _ssd_fwd(xc, dtr, dtrT, dtb, dtbT, alog, alogT, dskx, expand, *, name):
    s = xc.shape[0]
    L = SSD_CHUNK
    nc = s // L

    def body(x_ref, b_ref, c_ref, dtr_ref, dtrT_ref, dtb_ref, dtbT_ref, al_ref, alT_ref, dsk_ref, e_ref,
             y_ref, st_ref, state):
        ci = pl.program_id(0)

        @pl.when(ci == 0)
        def _():
            state[...] = jnp.zeros_like(state)

        st_ref[0] = state[...]
        q = _ssd_common(dtr_ref, dtrT_ref, dtb_ref, dtbT_ref, al_ref, alT_ref, e_ref)
        causal = q["ri"] >= q["cj"]
        lane_lo = q["cj"] < 64
        x = x_ref[...]
        xdt = x * q["dt_x"]
        xdt_b = xdt.astype(BF16)
        xdtw_b = (xdt * q["w_x"]).astype(BF16)
        for g in range(SSD_GROUPS):
            bg = b_ref[:, 128 * g:128 * g + 128]
            cg_b = c_ref[:, 128 * g:128 * g + 128].astype(BF16)
            cb = _dot_nt(cg_b, bg.astype(BF16))
            bgT_b = bg.T.astype(BF16)
            s0 = state[g]
            for jj in range(4):
                j = 4 * g + jj
                sl = slice(128 * j, 128 * j + 128)
                sls = slice(128 * jj, 128 * jj + 128)
                ms = []
                for h in (2 * j, 2 * j + 1):
                    seg = q["acum"][:, h:h + 1] - q["acumT"][h:h + 1, :]
                    decay = jnp.exp(jnp.where(causal, seg, -jnp.inf))
                    ms.append((cb * decay).astype(BF16))
                mcat = jnp.concatenate(ms, axis=1)
                xp = xdt_b[:, sl]
                zero = jnp.zeros_like(xp)
                xstack = jnp.concatenate([jnp.where(lane_lo, xp, zero), jnp.where(lane_lo, zero, xp)], axis=0)
                y = _dot(mcat, xstack)
                y = y + q["eacum_x"][:, sl] * _dot(cg_b, s0[:, sls].astype(BF16))
                y = y + x[:, sl] * dsk_ref[:, sl]
                y_ref[:, sl] = y
                state[g, :, sls] = s0[:, sls] * q["elast_x"][:, sl] + _dot(bgT_b, xdtw_b[:, sl])

    return pl.pallas_call(
        body, name=name, grid=(nc,),
        in_specs=_ssd_specs(nc, False),
        out_specs=[pl.BlockSpec((L, SSD_D_INNER), lambda c: (c, 0)),
                   pl.BlockSpec((1, SSD_GROUPS, SSD_STATE, 512), lambda c: (c, 0, 0, 0))],
        out_shape=[jax.ShapeDtypeStruct((s, SSD_D_INNER), F32),
                   jax.ShapeDtypeStruct((nc, SSD_GROUPS, SSD_STATE, 512), F32)],
        scratch_shapes=[pltpu.VMEM((SSD_GROUPS, SSD_STATE, 512), F32)],
        compiler_params=_cp("arbitrary"),
    )(xc, xc, xc, dtr, dtrT, dtb, dtbT, alog, alogT, dskx, expand)


def _ssd_bwd(xc, dtr, dtrT, dtb, dtbT, alog, alogT, dskx, expand, expandT, states, dy, *, name):
    s = xc.shape[0]
    L = SSD_CHUNK
    H = SSD_HEADS
    nc = s // L

    def body(x_ref, b_ref, c_ref, dtr_ref, dtrT_ref, dtb_ref, dtbT_ref, al_ref, alT_ref, dsk_ref, e_ref,
             et_ref, st_ref, dy_ref,
             dxc_ref, ddtc_ref, ddtr_ref, dbc_ref, dbr_ref, dac_ref, dar_ref, ddsk_ref, dstate):
        ci = pl.program_id(0)

        @pl.when(ci == 0)
        def _():
            dstate[...] = jnp.zeros_like(dstate)
            dbc_ref[...] = jnp.zeros_like(dbc_ref)
            dbr_ref[...] = jnp.zeros_like(dbr_ref)
            dac_ref[...] = jnp.zeros_like(dac_ref)
            dar_ref[...] = jnp.zeros_like(dar_ref)
            ddsk_ref[...] = jnp.zeros_like(ddsk_ref)

        q = _ssd_common(dtr_ref, dtrT_ref, dtb_ref, dtbT_ref, al_ref, alT_ref, e_ref)
        ri, cj = q["ri"], q["cj"]
        causal = ri >= cj
        causalT = ri <= cj
        lane_lo = cj < 64
        lane_h = lax.broadcasted_iota(jnp.int32, (1, H), 1)
        sub_h = lax.broadcasted_iota(jnp.int32, (H, 1), 0)
        x = x_ref[...]
        dyv = dy_ref[...]
        xdt = x * q["dt_x"]
        xdt_b = xdt.astype(BF16)
        xdtw = xdt * q["w_x"]
        xdtw_b = xdtw.astype(BF16)
        edy = q["eacum_x"] * dyv
        edy_b = edy.astype(BF16)
        dyv_b = dyv.astype(BF16)
        dacum_col = jnp.zeros((L, H), F32)
        dacum_row = jnp.zeros((H, L), F32)
        dxdt_t, yoff_t, u_t, r_t = [], [], [], []
        for g in range(SSD_GROUPS):
            bg = b_ref[:, 128 * g:128 * g + 128]
            cg = c_ref[:, 128 * g:128 * g + 128]
            bg_b, cg_b = bg.astype(BF16), cg.astype(BF16)
            cb = _dot_nt(cg_b, bg_b)
            cbT = _dot_nt(bg_b, cg_b)
            cgT_b = cg.T.astype(BF16)
            s0 = st_ref[0, g]
            ds = dstate[g]
            s0_b, ds_b = s0.astype(BF16), ds.astype(BF16)
            dcb = jnp.zeros((L, L), F32)
            for jj in range(4):
                j = 4 * g + jj
                sl = slice(128 * j, 128 * j + 128)
                sls = slice(128 * jj, 128 * jj + 128)
                decs, mts = [], []
                for h in (2 * j, 2 * j + 1):
                    seg = q["acum"][:, h:h + 1] - q["acumT"][h:h + 1, :]
                    decs.append(jnp.exp(jnp.where(causal, seg, -jnp.inf)))
                    mts.append((cbT * jnp.exp(jnp.where(causalT, -seg, -jnp.inf))).astype(BF16))
                dyt_b = dyv_b[:, sl]
                zero = jnp.zeros_like(dyt_b)
                dystack = jnp.concatenate([jnp.where(lane_lo, dyt_b, zero), jnp.where(lane_lo, zero, dyt_b)], axis=0)
                dxs = _dot(jnp.concatenate(mts, axis=0), dyt_b)
                dxdt = jnp.where(lane_lo, dxs[:L], dxs[L:])
                dmcat = _dot_nt(dystack, xdt_b[:, sl])
                for idx, h in enumerate((2 * j, 2 * j + 1)):
                    dm = dmcat[L * idx:L * idx + L]
                    dcb = dcb + dm * decs[idx]
                    dseg = dm * cb * decs[idx]
                    dacum_col = dacum_col + jnp.sum(dseg, axis=1, keepdims=True) * (lane_h == h).astype(F32)
                    dacum_row = dacum_row - (sub_h == h).astype(F32) * jnp.sum(dseg, axis=0, keepdims=True)
                gmat = _dot(cg_b, s0_b[:, sls])
                yoff_t.append(edy[:, sl] * gmat)
                qm = _dot(bg_b, ds_b[:, sls])
                dxdt_t.append(dxdt + qm * q["w_x"][:, sl])
                u_t.append(qm * xdtw[:, sl])
                r_t.append(ds[:, sls] * s0[:, sls] * q["elast_x"][:, sl])
                dstate[g, :, sls] = ds[:, sls] * q["elast_x"][:, sl] + _dot(cgT_b, edy_b[:, sl])
            gsl = slice(512 * g, 512 * g + 512)
            dcb_b = dcb.astype(BF16)
            dcg = _dot(dcb_b, bg_b) + _dot_nt(edy_b[:, gsl], s0_b)
            dbg = _dot(dcb.T.astype(BF16), cg_b) + _dot_nt(xdtw_b[:, gsl], ds_b)
            dxc_ref[:, SSD_D_INNER + 128 * g:SSD_D_INNER + 128 * g + 128] = dbg
            dxc_ref[:, SSD_D_INNER + 512 + 128 * g:SSD_D_INNER + 512 + 128 * g + 128] = dcg
        et = et_ref[...]
        dxdt_all = jnp.concatenate(dxdt_t, axis=1)
        yoff = jnp.concatenate(yoff_t, axis=1)
        uu = jnp.concatenate(u_t, axis=1)
        rr = jnp.concatenate(r_t, axis=1)
        dacum_col = dacum_col + _dot_01(yoff - uu, et)
        dlast = jnp.sum(_dot_01(uu + rr, et), axis=0, keepdims=True)
        row_lh = lax.broadcasted_iota(jnp.int32, (L, H), 0)
        dacum_col = dacum_col + jnp.where(row_lh == L - 1, dlast, 0.0)
        d_dta_col = jnp.dot(q["triu"], dacum_col, precision=HI, preferred_element_type=F32)
        d_dta_row = jnp.dot(dacum_row, q["tril"], precision=HI, preferred_element_type=F32)
        ddt_col = d_dta_col * q["a"] + _dot_01(dxdt_all * x, et)
        ddt_row = d_dta_row * q["aT"]
        ddtr_col = ddt_col * jax.nn.sigmoid(q["pre"])
        ddtr_row = ddt_row * jax.nn.sigmoid(q["preT"])
        ddtc_ref[...] = ddtr_col
        ddtr_ref[...] = ddtr_row
        dac_ref[...] += jnp.sum(d_dta_col * q["dt"], axis=0, keepdims=True)
        dar_ref[...] += jnp.sum(d_dta_row * q["dtT"], axis=1, keepdims=True)
        dbc_ref[...] += jnp.sum(ddtr_col, axis=0, keepdims=True)
        dbr_ref[...] += jnp.sum(ddtr_row, axis=1, keepdims=True)
        ddsk_ref[...] += jnp.sum(dyv * x, axis=0, keepdims=True)
        dxc_ref[:, 0:SSD_D_INNER] = dxdt_all * q["dt_x"] + dyv * dsk_ref[...]

    rv = lambda c: nc - 1 - c
    in_specs = _ssd_specs(nc, True) + [
        pl.BlockSpec((SSD_D_INNER, H), lambda c: (0, 0)),
        pl.BlockSpec((1, SSD_GROUPS, SSD_STATE, 512), lambda c: (rv(c), 0, 0, 0)),
        pl.BlockSpec((L, SSD_D_INNER), lambda c: (rv(c), 0)),
    ]
    vec_c = pl.BlockSpec((1, H), lambda c: (0, 0))
    vec_r = pl.BlockSpec((H, 1), lambda c: (0, 0))
    return pl.pallas_call(
        body, name=name, grid=(nc,),
        in_specs=in_specs,
        out_specs=[pl.BlockSpec((L, SSD_CONV_DIM), lambda c: (rv(c), 0)),
                   pl.BlockSpec((L, H), lambda c: (rv(c), 0)),
                   pl.BlockSpec((H, L), lambda c: (0, rv(c))),
                   vec_c, vec_r, vec_c, vec_r,
                   pl.BlockSpec((1, SSD_D_INNER), lambda c: (0, 0))],
        out_shape=[jax.ShapeDtypeStruct((s, SSD_CONV_DIM), F32),
                   jax.ShapeDtypeStruct((s, H), F32), jax.ShapeDtypeStruct((H, s), F32),
                   jax.ShapeDtypeStruct((1, H), F32), jax.ShapeDtypeStruct((H, 1), F32),
                   jax.ShapeDtypeStruct((1, H), F32), jax.ShapeDtypeStruct((H, 1), F32),
                   jax.ShapeDtypeStruct((1, SSD_D_INNER), F32)],
        scratch_shapes=[pltpu.VMEM((SSD_GROUPS, SSD_STATE, 512), F32)],
        compiler_params=_cp("arbitrary"),
    )(xc, xc, xc, dtr, dtrT, dtb, dtbT, alog, alogT, dskx, expand, expandT, states, dy)


QK_PAD = 256
MLA_TS = 256


def _rope_tables4(pos):
    inv = 1.0 / (ROPE_THETA ** (jnp.arange(0, MLA_ROPE, 2, dtype=F32) / MLA_ROPE))
    ang = pos.astype(F32)[:, None] * inv
    c, s = jnp.cos(ang), jnp.sin(ang)
    return jnp.tile(c, (1, 4)), jnp.concatenate([-s, s, -s, s], axis=1)


def _mla_gains(qg, kg):
    z = jnp.zeros((LANE - MLA_ROPE,), F32)
    return (qg[:MLA_NOPE][None], jnp.concatenate([qg[MLA_NOPE:], z])[None],
            kg[:MLA_NOPE][None], jnp.concatenate([kg[MLA_NOPE:], z])[None])


def _rope_swap(t, first):
    return jnp.where(first, pltpu.roll(t, 96, 1), pltpu.roll(t, 32, 1))


def _mla_prep_specs(ts):
    row = lambda w, c=0: pl.BlockSpec((ts, w), lambda i: (i, c))
    vec = pl.BlockSpec((1, LANE), lambda i: (0, 0))
    return [row(MLA_HEADS * MLA_QK), row(2 * MLA_HEADS * MLA_NOPE), row(LANE, OFF_KRDT // LANE), row(LANE), row(LANE),
            vec, vec, vec, vec]


def _mla_prep_fwd(qraw, kvraw, proj, cos4, sin4, gqn, gqr, gkn, gkr, *, name):
    s = qraw.shape[0]
    ts = _pick(s, MLA_TS, 8)

    def body(q_ref, kv_ref, kr_ref, cos_ref, sin_ref, gqn_ref, gqr_ref, gkn_ref, gkr_ref, qo_ref, ko_ref):
        lane = lax.broadcasted_iota(jnp.int32, (ts, LANE), 1)
        lo = lane < 64
        first = (lane % 64) < 32
        cos, sin = cos_ref[...], sin_ref[...]
        kr = jnp.where(lo, kr_ref[...], 0.0)
        ssq_kr = jnp.sum(kr * kr, axis=-1, keepdims=True)

        def head(xn, xr, ssq_r, gn, gr):
            rs = lax.rsqrt((jnp.sum(xn * xn, axis=-1, keepdims=True) + ssq_r) * (1.0 / MLA_QK) + EPS)
            yr = xr * rs * gr
            return xn * rs * gn, yr * cos + _rope_swap(yr, first) * sin

        for h in range(MLA_HEADS):
            tile = q_ref[:, MLA_HEADS * MLA_NOPE + LANE * (h // 2):MLA_HEADS * MLA_NOPE + LANE * (h // 2) + LANE]
            qr = jnp.where(lo, tile if h % 2 == 0 else pltpu.roll(tile, 64, 1), 0.0)
            on, orr = head(q_ref[:, LANE * h:LANE * h + LANE], qr, jnp.sum(qr * qr, axis=-1, keepdims=True),
                           gqn_ref[...], gqr_ref[...])
            qo_ref[h, :, 0:LANE] = on.astype(BF16)
            qo_ref[h, :, LANE:QK_PAD] = orr.astype(BF16)
            on, orr = head(kv_ref[:, LANE * h:LANE * h + LANE], kr, ssq_kr, gkn_ref[...], gkr_ref[...])
            ko_ref[h, :, 0:LANE] = on.astype(BF16)
            ko_ref[h, :, LANE:QK_PAD] = orr.astype(BF16)

    out = pl.BlockSpec((MLA_HEADS, ts, QK_PAD), lambda i: (0, i, 0))
    sh = jax.ShapeDtypeStruct((MLA_HEADS, s, QK_PAD), BF16)
    return pl.pallas_call(
        body, name=name, grid=(s // ts,),
        in_specs=_mla_prep_specs(ts), out_specs=[out, out], out_shape=[sh, sh],
        compiler_params=_cp("parallel"),
    )(qraw, kvraw, proj, cos4, sin4, gqn, gqr, gkn, gkr)


def _mla_prep_bwd(qraw, kvraw, proj, cos4, sin4, gqn, gqr, gkn, gkr, dq, dk, *, name):
    s = qraw.shape[0]
    ts = _pick(s, MLA_TS, 8)

    def body(q_ref, kv_ref, kr_ref, cos_ref, sin_ref, gqn_ref, gqr_ref, gkn_ref, gkr_ref, dq_ref, dk_ref,
             dqraw_ref, dkn_ref, dkr_ref, dgqn_ref, dgqr_ref, dgkn_ref, dgkr_ref):
        i = pl.program_id(0)

        @pl.when(i == 0)
        def _():
            for r in (dgqn_ref, dgqr_ref, dgkn_ref, dgkr_ref):
                r[...] = jnp.zeros_like(r)

        lane = lax.broadcasted_iota(jnp.int32, (ts, LANE), 1)
        lo = lane < 64
        first = (lane % 64) < 32
        cos, sin = cos_ref[...], sin_ref[...]
        kr = jnp.where(lo, kr_ref[...], 0.0)
        ssq_kr = jnp.sum(kr * kr, axis=-1, keepdims=True)

        def head(xn, xr, ssq_r, gn, gr, don, dor):
            rs = lax.rsqrt((jnp.sum(xn * xn, axis=-1, keepdims=True) + ssq_r) * (1.0 / MLA_QK) + EPS)
            xhn, xhr = xn * rs, xr * rs
            dor = jnp.where(lo, dor, 0.0)
            dyr = dor * cos + _rope_swap(dor * sin, first)
            dxn, dxr = don * gn, dyr * gr
            mm = (jnp.sum(dxn * xhn, axis=-1, keepdims=True) + jnp.sum(dxr * xhr, axis=-1, keepdims=True)) * (1.0 / MLA_QK)
            return (rs * (dxn - xhn * mm), rs * (dxr - xhr * mm),
                    jnp.sum(don * xhn, axis=0, keepdims=True), jnp.sum(dyr * xhr, axis=0, keepdims=True))

        dkr_acc = jnp.zeros((ts, LANE), F32)
        prev = None
        for h in range(MLA_HEADS):
            c0 = MLA_HEADS * MLA_NOPE + LANE * (h // 2)
            tile = q_ref[:, c0:c0 + LANE]
            qr = jnp.where(lo, tile if h % 2 == 0 else pltpu.roll(tile, 64, 1), 0.0)
            dn, dr, gn_p, gr_p = head(q_ref[:, LANE * h:LANE * h + LANE], qr, jnp.sum(qr * qr, axis=-1, keepdims=True),
                                      gqn_ref[...], gqr_ref[...], dq_ref[h, :, 0:LANE], dq_ref[h, :, LANE:QK_PAD])
            dqraw_ref[:, LANE * h:LANE * h + LANE] = dn.astype(dqraw_ref.dtype)
            dgqn_ref[...] += gn_p
            dgqr_ref[...] += gr_p
            if h % 2 == 0:
                prev = dr
            else:
                dqraw_ref[:, c0:c0 + LANE] = (prev + pltpu.roll(dr, 64, 1)).astype(dqraw_ref.dtype)
            dn, dr, gn_p, gr_p = head(kv_ref[:, LANE * h:LANE * h + LANE], kr, ssq_kr, gkn_ref[...], gkr_ref[...],
                                      dk_ref[h, :, 0:LANE], dk_ref[h, :, LANE:QK_PAD])
            dkn_ref[:, LANE * h:LANE * h + LANE] = dn.astype(dkn_ref.dtype)
            dkr_acc = dkr_acc + dr
            dgkn_ref[...] += gn_p
            dgkr_ref[...] += gr_p
        dkr_ref[...] = dkr_acc

    row = lambda w: pl.BlockSpec((ts, w), lambda i: (i, 0))
    vec = pl.BlockSpec((1, LANE), lambda i: (0, 0))
    dspec = pl.BlockSpec((MLA_HEADS, ts, QK_PAD), lambda i: (0, i, 0))
    vsh = jax.ShapeDtypeStruct((1, LANE), F32)
    return pl.pallas_call(
        body, name=name, grid=(s // ts,),
        in_specs=_mla_prep_specs(ts) + [dspec, dspec],
        out_specs=[row(MLA_HEADS * MLA_QK), row(MLA_HEADS * MLA_NOPE), row(LANE), vec, vec, vec, vec],
        out_shape=[jax.ShapeDtypeStruct((s, MLA_HEADS * MLA_QK), BF16), jax.ShapeDtypeStruct((s, MLA_HEADS * MLA_NOPE), BF16),
                   jax.ShapeDtypeStruct((s, LANE), F32), vsh, vsh, vsh, vsh],
        compiler_params=_cp("arbitrary"),
    )(qraw, kvraw, proj, cos4, sin4, gqn, gqr, gkn, gkr, dq, dk)


ATT_T = 1024
ATT_T_FWD = 2048
ATT_SCALE = MLA_QK ** -0.5


def _attn_fwd(q, k, kvraw, *, name):
    nh, s, _ = q.shape
    t = _pick(s, ATT_T_FWD, LANE)
    nb = s // t

    def body(q_ref, k_ref, v_ref, o_ref, lse_ref, m_ref, l_ref, acc_ref):
        i, j = pl.program_id(1), pl.program_id(2)

        @pl.when(j == 0)
        def _():
            m_ref[...] = jnp.full_like(m_ref, -jnp.inf)
            l_ref[...] = jnp.zeros_like(l_ref)
            acc_ref[...] = jnp.zeros_like(acc_ref)

        def step(diagonal):
            sc = _dot_nt(q_ref[0], k_ref[0]) * ATT_SCALE
            if diagonal:
                ri = lax.broadcasted_iota(jnp.int32, (t, t), 0)
                cj = lax.broadcasted_iota(jnp.int32, (t, t), 1)
                sc = jnp.where(ri >= cj, sc, -jnp.inf)
            m_new = jnp.maximum(m_ref[...], jnp.max(sc, axis=-1, keepdims=True))
            alpha = jnp.exp(m_ref[...] - m_new)
            p = jnp.exp(sc - m_new)
            l_ref[...] = alpha * l_ref[...] + jnp.sum(p, axis=-1, keepdims=True)
            acc_ref[...] = alpha * acc_ref[...] + _dot(p.astype(BF16), v_ref[...].astype(BF16))
            m_ref[...] = m_new

        @pl.when(j < i)
        def _():
            step(False)

        @pl.when(j == i)
        def _():
            step(True)
            o_ref[...] = acc_ref[...] / l_ref[...]
            lse_ref[0] = m_ref[...] + jnp.log(l_ref[...])

    return pl.pallas_call(
        body, name=name, grid=(nh, nb, nb),
        in_specs=[pl.BlockSpec((1, t, QK_PAD), lambda h, i, j: (h, i, 0)),
                  pl.BlockSpec((1, t, QK_PAD), lambda h, i, j: (h, jnp.minimum(j, i), 0)),
                  pl.BlockSpec((t, MLA_V), lambda h, i, j: (jnp.minimum(j, i), nh + h))],
        out_specs=[pl.BlockSpec((t, MLA_V), lambda h, i, j: (i, h)),
                   pl.BlockSpec((1, t, 1), lambda h, i, j: (h, i, 0))],
        out_shape=[jax.ShapeDtypeStruct((s, nh * MLA_V), F32), jax.ShapeDtypeStruct((nh, s, 1), F32)],
        scratch_shapes=[pltpu.VMEM((t, 1), F32), pltpu.VMEM((t, 1), F32), pltpu.VMEM((t, MLA_V), F32)],
        compiler_params=_cp("parallel", "parallel", "arbitrary"),
    )(q, k, kvraw)


def _attn_bwd(q, k, kvraw, o, lse, do, *, name):
    nh, s, _ = q.shape
    t = _pick(s, ATT_T, LANE)
    nb = s // t

    def body(q_ref, k_ref, v_ref, o_ref, lse_ref, do_ref, dq_ref, dk_ref, dv_ref, dk_acc, dv_acc):
        j, i = pl.program_id(1), pl.program_id(2)

        @pl.when(i == 0)
        def _():
            dk_acc[...] = jnp.zeros_like(dk_acc)
            dv_acc[...] = jnp.zeros_like(dv_acc)

        def step(diagonal):
            qv, kv = q_ref[0], k_ref[0]
            sc = _dot_nt(qv, kv) * ATT_SCALE
            if diagonal:
                ri = lax.broadcasted_iota(jnp.int32, (t, t), 0)
                cj = lax.broadcasted_iota(jnp.int32, (t, t), 1)
                sc = jnp.where(ri >= cj, sc, -jnp.inf)
            p = jnp.exp(sc - lse_ref[0])
            dov = do_ref[...]
            delta = jnp.sum(dov * o_ref[...], axis=-1, keepdims=True)
            do_b = dov.astype(BF16)
            dv_acc[...] += _dot_tn(p.astype(BF16), do_b)
            dp = _dot_nt(do_b, v_ref[...].astype(BF16))
            ds_b = (p * (dp - delta) * ATT_SCALE).astype(BF16)
            dk_acc[...] += _dot_tn(ds_b, qv)
            dq_part = _dot(ds_b, kv)
            rows = pl.ds(pl.multiple_of(i * t, t), t)

            @pl.when(j == 0)
            def _():
                dq_ref[0, rows, :] = dq_part

            @pl.when(j > 0)
            def _():
                dq_ref[0, rows, :] += dq_part

        @pl.when(i > j)
        def _():
            step(False)

        @pl.when(i == j)
        def _():
            step(True)

        @pl.when(i == nb - 1)
        def _():
            dk_ref[0] = dk_acc[...]
            dv_ref[...] = dv_acc[...].astype(dv_ref.dtype)

    qi = lambda h, j, i: jnp.maximum(i, j)
    return pl.pallas_call(
        body, name=name, grid=(nh, nb, nb),
        in_specs=[pl.BlockSpec((1, t, QK_PAD), lambda h, j, i: (h, qi(h, j, i), 0)),
                  pl.BlockSpec((1, t, QK_PAD), lambda h, j, i: (h, j, 0)),
                  pl.BlockSpec((t, MLA_V), lambda h, j, i: (j, nh + h)),
                  pl.BlockSpec((t, MLA_V), lambda h, j, i: (qi(h, j, i), h)),
                  pl.BlockSpec((1, t, 1), lambda h, j, i: (h, qi(h, j, i), 0)),
                  pl.BlockSpec((t, MLA_V), lambda h, j, i: (qi(h, j, i), h))],
        out_specs=[pl.BlockSpec((1, s, QK_PAD), lambda h, j, i: (h, 0, 0)),
                   pl.BlockSpec((1, t, QK_PAD), lambda h, j, i: (h, j, 0)),
                   pl.BlockSpec((t, MLA_V), lambda h, j, i: (j, h))],
        out_shape=[jax.ShapeDtypeStruct((nh, s, QK_PAD), F32), jax.ShapeDtypeStruct((nh, s, QK_PAD), F32),
                   jax.ShapeDtypeStruct((s, nh * MLA_V), BF16)],
        scratch_shapes=[pltpu.VMEM((t, QK_PAD), F32), pltpu.VMEM((t, MLA_V), F32)],
        compiler_params=_cp("parallel", "arbitrary", "arbitrary"),
    )(q, k, kvraw, o, lse, do)


def _ffn_fwd(h, w, tag):
    n = _rms_fwd(h, w["ln"], name=tag + "_norm")
    act, gate, up = _ffn_up(n, w["w13"], name=tag + "_up")
    out = _matmul(act, w["w2"], "nn", name=tag + "_down", scale=0.5, res=h)
    return out, (h, n, gate, up, act)


def _ffn_bwd(dout, saved, w, tag):
    h, n, gate, up, act = saved
    dact = _matmul(dout, w["w2"], "nt", name=tag + "_down_dx", scale=0.5, out_dtype=BF16)
    dw2 = _matmul(act, dout, "tn", name=tag + "_down_dw", scale=0.5)
    dgu = _swiglu_bwd(gate, up, dact, name=tag + "_act_bwd")
    dw13 = _matmul(n, dgu, "tn", name=tag + "_up_dw")
    dn = _matmul(dgu, w["w13"], "nt", name=tag + "_up_dx")
    dh, dln = _rms_bwd(h, w["ln"], dn, name=tag + "_norm_bwd", res=dout)
    return dh, dict(ln=dln, w13=dw13, w2=dw2)


def _mixer_fwd(h, w, rope, tag):
    cos4, sin4 = rope
    u = _rms_fwd(h, w["ln_mix"], name=tag + "_norm")
    proj = _matmul(u, w["w_in"], "nn", name=tag + "_in")
    xc = _conv_fwd(proj, w["conv_w"], w["conv_b"], name=tag + "_conv")
    dtr = proj[:, OFF_KRDT + MLA_ROPE:OFF_KRDT + MLA_ROPE + SSD_HEADS]
    dtrT = dtr.T
    y, states = _ssd_fwd(xc, dtr, dtrT, *w["ssd_aux"], name=tag + "_ssd")
    yn = _gated_rms_fwd(y, proj, w["ssd_norm"], name=tag + "_ssd_norm")
    y_ssd = _matmul(yn, w["w_ssd_out"], "nn", name=tag + "_ssd_out")
    cqn = _rms_fwd(proj, w["q_lora_norm"], name=tag + "_q_lora_norm", col=OFF_CQ // MLA_Q_LORA, width=MLA_Q_LORA)
    qraw = _matmul(cqn, w["w_uq"], "nn", name=tag + "_uq")
    ckvn = _rms_fwd(proj, w["kv_lora_norm"], name=tag + "_kv_lora_norm", col=OFF_CKV // MLA_KV_LORA, width=MLA_KV_LORA)
    kvraw = _matmul(ckvn, w["w_ukv"], "nn", name=tag + "_ukv")
    qf, kf = _mla_prep_fwd(qraw, kvraw, proj, cos4, sin4, *w["qk_gains"], name=tag + "_qk_prep")
    o, lse = _attn_fwd(qf, kf, kvraw, name=tag + "_attn")
    y_mla = _matmul(o, w["w_mla_out"], "nn", name=tag + "_mla_out")
    merged = _merge_fwd(proj, y_ssd, y_mla, name=tag + "_merge")
    out = _matmul(merged, w["w_o"], "nn", name=tag + "_o", res=h)
    saved = dict(h=h, u=u, proj=proj, xc=xc, dtr=dtr, dtrT=dtrT, states=states, y=y, yn=yn, y_ssd=y_ssd, cqn=cqn,
                 qraw=qraw, ckvn=ckvn, kvraw=kvraw, qf=qf, kf=kf, o=o, lse=lse, y_mla=y_mla, merged=merged)
    return out, saved


def _mixer_bwd(dout, s, w, rope, tag):
    cos4, sin4 = rope
    g = {}
    proj = s["proj"]
    dmerged = _matmul(dout, w["w_o"], "nt", name=tag + "_o_dx")
    g["w_o"] = _matmul(s["merged"], dout, "tn", name=tag + "_o_dw")
    dgates, dy_ssd, dy_mla = _merge_bwd(proj, s["y_ssd"], s["y_mla"], dmerged, name=tag + "_merge_bwd")
    do = _matmul(dy_mla, w["w_mla_out"], "nt", name=tag + "_mla_out_dx")
    g["w_mla_out"] = _matmul(s["o"], dy_mla, "tn", name=tag + "_mla_out_dw")
    dqf, dkf, dv = _attn_bwd(s["qf"], s["kf"], s["kvraw"], s["o"], s["lse"], do, name=tag + "_attn_bwd")
    dqraw, dkn, dkrt, dgqn, dgqr, dgkn, dgkr = _mla_prep_bwd(
        s["qraw"], s["kvraw"], proj, cos4, sin4, *w["qk_gains"], dqf, dkf, name=tag + "_qk_prep_bwd")
    g["q_norm"] = jnp.concatenate([dgqn[0], dgqr[0, :MLA_ROPE]])
    g["k_norm"] = jnp.concatenate([dgkn[0], dgkr[0, :MLA_ROPE]])
    dkvraw = jnp.concatenate([dkn, dv], axis=1)
    dcqn = _matmul(dqraw, w["w_uq"], "nt", name=tag + "_uq_dx")
    g["w_uq"] = _matmul(s["cqn"], dqraw, "tn", name=tag + "_uq_dw")
    dckvn = _matmul(dkvraw, w["w_ukv"], "nt", name=tag + "_ukv_dx")
    g["w_ukv"] = _matmul(s["ckvn"], dkvraw, "tn", name=tag + "_ukv_dw")
    dcq, g["q_lora_norm"] = _rms_bwd(proj, w["q_lora_norm"], dcqn, name=tag + "_q_lora_norm_bwd",
                                     col=OFF_CQ // MLA_Q_LORA, width=MLA_Q_LORA, out_dtype=BF16)
    dckv, g["kv_lora_norm"] = _rms_bwd(proj, w["kv_lora_norm"], dckvn, name=tag + "_kv_lora_norm_bwd",
                                       col=OFF_CKV // MLA_KV_LORA, width=MLA_KV_LORA, out_dtype=BF16)
    dyn = _matmul(dy_ssd, w["w_ssd_out"], "nt", name=tag + "_ssd_out_dx")
    g["w_ssd_out"] = _matmul(s["yn"], dy_ssd, "tn", name=tag + "_ssd_out_dw")
    dy, dz, g["ssd_norm"] = _gated_rms_bwd(s["y"], proj, w["ssd_norm"], dyn, name=tag + "_ssd_norm_bwd")
    aux = w["ssd_aux"]
    dxc, ddt_c, ddt_r, dbias_c, dbias_r, da_c, da_r, ddsk = _ssd_bwd(
        s["xc"], s["dtr"], s["dtrT"], *aux, aux[-1].T, s["states"], dy, name=tag + "_ssd_bwd")
    g["dt_bias"] = dbias_c[0] + dbias_r[:, 0]
    g["a_log"] = (da_c[0] + da_r[:, 0]) * (-jnp.exp(aux[2][0]))
    g["d_skip"] = jnp.sum(ddsk.reshape(SSD_HEADS, SSD_HEAD_DIM), axis=1)
    dpre, g["conv_w"], g["conv_b"] = _conv_bwd_pre(proj, w["conv_w"], w["conv_b"], dxc, name=tag + "_conv_bwd_pre")
    dxbc = _conv_bwd_x(dpre, w["conv_w"], name=tag + "_conv_bwd_x")
    ddtr = ddt_c + ddt_r.T
    dkrdt = jnp.concatenate([dkrt[:, :MLA_ROPE], ddtr, jnp.zeros((ddtr.shape[0], LANE - MLA_ROPE - SSD_HEADS), F32)], axis=1)
    dproj = jnp.concatenate([dz, dxbc, dgates, dcq, dckv, dkrdt.astype(BF16)], axis=1)
    du = _matmul(dproj, w["w_in"], "nt", name=tag + "_in_dx")
    g["w_in"] = _matmul(s["u"], dproj, "tn", name=tag + "_in_dw")
    dh, g["ln_mix"] = _rms_bwd(s["h"], w["ln_mix"], du, name=tag + "_norm_bwd", res=dout)
    return dh, g


W_NAMES = ["ln_ffn1", "ffn1_w13", "ffn1_w2", "ln_mix", "w_in", "conv_w", "conv_b", "dt_bias", "a_log", "d_skip",
           "ssd_norm", "w_ssd_out", "q_lora_norm", "w_uq", "kv_lora_norm", "w_ukv", "q_norm", "k_norm", "w_mla_out",
           "w_o", "ln_ffn2", "ffn2_w13", "ffn2_w2"]
SHARD_AXIS = {"ffn1_w13": 2, "ffn1_w2": 1, "w_in": 2, "conv_w": 2, "w_ssd_out": 1, "w_uq": 2, "w_ukv": 2,
              "w_mla_out": 1, "w_o": 1, "ffn2_w13": 2, "ffn2_w2": 1}
SHARDED = [n for n in W_NAMES if n in SHARD_AXIS and n != "conv_w"] + ["conv_w"]
REPLICATED = [n for n in W_NAMES if n not in SHARD_AXIS]
N_CHIPS = 4
N_DEV = 8
PACK_COLS = 1024
IN_SPLIT = (2048, 3072, 32, 512, 256, 64, 2048)


def _pack_mats(arrs, rows, dtype):
    mats = [a.astype(dtype).reshape(-1, PACK_COLS) for a in arrs]
    used = sum(m.shape[0] for m in mats)
    return mats + [jnp.zeros((rows - used, PACK_COLS), dtype)]


def _pack(arrs, rows, dtype):
    return jnp.concatenate(_pack_mats(arrs, rows, dtype), axis=0)


def _unpack(packed, shapes):
    out, at = [], 0
    for sh in shapes:
        r = math.prod(sh) // PACK_COLS
        out.append(packed[at:at + r].reshape(sh))
        at += r
    return out


def _unpack_flat(flat, shapes):
    out, at = [], 0
    for sh in shapes:
        n = math.prod(sh)
        out.append(flat[at:at + n].reshape(sh))
        at += n
    return out


def _pack_rows(shapes):
    n = sum(math.prod(sh) for sh in shapes)
    return -(-n // (PACK_COLS * 1024)) * 1024


def _in_perm(w_in):
    z, xbc, dt, cq, ckv, kr, gates = jnp.split(w_in, list(np_cumsum(IN_SPLIT))[:-1], axis=1)
    return jnp.concatenate([z, xbc, gates, cq, ckv, kr, dt, jnp.zeros((w_in.shape[0], PROJ_W - sum(IN_SPLIT)), w_in.dtype)], axis=1)


def _in_unperm(g):
    z, xbc, gates, cq, ckv = (g[:, OFF_Z:OFF_XBC], g[:, OFF_XBC:OFF_GATES], g[:, OFF_GATES:OFF_CQ], g[:, OFF_CQ:OFF_CKV],
                              g[:, OFF_CKV:OFF_KRDT])
    kr = g[:, OFF_KRDT:OFF_KRDT + MLA_ROPE]
    dt = g[:, OFF_KRDT + MLA_ROPE:OFF_KRDT + MLA_ROPE + SSD_HEADS]
    return jnp.concatenate([z, xbc, dt, cq, ckv, kr, gates], axis=1)


def np_cumsum(sizes):
    out, t = [], 0
    for s in sizes:
        t += s
        out.append(t)
    return out


def _head_perm(w, first):
    r = w.shape[0]
    w3 = w.reshape(r, MLA_HEADS, -1)
    return jnp.concatenate([w3[:, :, :first].reshape(r, -1), w3[:, :, first:].reshape(r, -1)], axis=1)


def _head_unperm(g, first):
    r = g.shape[0]
    rest = g.shape[1] // MLA_HEADS - first
    a = g[:, :MLA_HEADS * first].reshape(r, MLA_HEADS, first)
    b = g[:, MLA_HEADS * first:].reshape(r, MLA_HEADS, rest)
    return jnp.concatenate([a, b], axis=2).reshape(r, -1)


def _layer_weights(full, l):
    row = lambda n: full[n][l][None].astype(F32)
    expand = jnp.repeat(jnp.eye(SSD_HEADS, dtype=F32), SSD_HEAD_DIM, axis=1)
    dtb, al, dsk = full["dt_bias"][l], full["a_log"][l], full["d_skip"][l]
    mixer = dict(
        ln_mix=row("ln_mix"), w_in=_in_perm(full["w_in"][l]), conv_w=full["conv_w"][l], conv_b=row("conv_b"),
        ssd_aux=(dtb[None], dtb[:, None], al[None], al[:, None], jnp.repeat(dsk, SSD_HEAD_DIM)[None], expand),
        ssd_norm=row("ssd_norm"), w_ssd_out=full["w_ssd_out"][l],
        q_lora_norm=row("q_lora_norm"), w_uq=_head_perm(full["w_uq"][l], MLA_NOPE),
        kv_lora_norm=row("kv_lora_norm"), w_ukv=_head_perm(full["w_ukv"][l], MLA_NOPE),
        qk_gains=_mla_gains(full["q_norm"][l], full["k_norm"][l]),
        w_mla_out=full["w_mla_out"][l], w_o=full["w_o"][l])
    ffn1 = dict(ln=row("ln_ffn1"), w13=full["ffn1_w13"][l], w2=full["ffn1_w2"][l])
    ffn2 = dict(ln=row("ln_ffn2"), w13=full["ffn2_w13"][l], w2=full["ffn2_w2"][l])
    return ffn1, mixer, ffn2


def _layer_grads(g1, gm, g2):
    return {
        "ln_ffn1": g1["ln"][0], "ffn1_w13": g1["w13"], "ffn1_w2": g1["w2"],
        "ln_mix": gm["ln_mix"][0], "w_in": _in_unperm(gm["w_in"]), "conv_w": gm["conv_w"], "conv_b": gm["conv_b"][0],
        "dt_bias": gm["dt_bias"], "a_log": gm["a_log"], "d_skip": gm["d_skip"], "ssd_norm": gm["ssd_norm"][0],
        "w_ssd_out": gm["w_ssd_out"], "q_lora_norm": gm["q_lora_norm"][0], "w_uq": _head_unperm(gm["w_uq"], MLA_NOPE),
        "kv_lora_norm": gm["kv_lora_norm"][0], "w_ukv": _head_unperm(gm["w_ukv"], MLA_NOPE),
        "q_norm": gm["q_norm"], "k_norm": gm["k_norm"], "w_mla_out": gm["w_mla_out"], "w_o": gm["w_o"],
        "ln_ffn2": g2["ln"][0], "ffn2_w13": g2["w13"], "ffn2_w2": g2["w2"],
    }


def _local_step(x, positions, loss_target, full):
    rope = _rope_tables4(positions)
    lw = [_layer_weights(full, l) for l in range(DEPTH)]
    h = x
    saved = []
    for l in range(DEPTH):
        f1, mx, f2 = lw[l]
        h, s1 = _ffn_fwd(h, f1, f"l{l}_ffn1")
        h, sm = _mixer_fwd(h, mx, rope, f"l{l}_mix")
        h, s2 = _ffn_fwd(h, f2, f"l{l}_ffn2")
        saved.append((s1, sm, s2))
    loss_part, dh = _loss_fwd_bwd(h, loss_target, name="loss")
    grads = [None] * DEPTH
    for l in reversed(range(DEPTH)):
        f1, mx, f2 = lw[l]
        s1, sm, s2 = saved[l]
        dh, g2 = _ffn_bwd(dh, s2, f2, f"l{l}_ffn2")
        dh, gm = _mixer_bwd(dh, sm, mx, rope, f"l{l}_mix")
        dh, g1 = _ffn_bwd(dh, s1, f1, f"l{l}_ffn1")
        grads[l] = _layer_grads(g1, gm, g2)
    full_grads = {n: jnp.stack([grads[l][n] for l in range(DEPTH)]) for n in W_NAMES}
    return loss_part, dh, full_grads


MESH = pl.DeviceIdType.MESH
ANY = pl.BlockSpec(memory_space=pl.ANY)


def _place():
    return lax.axis_index("x"), lax.axis_index("y"), lax.axis_index("c")


def _other_chips(x, y):
    return [(1 - x, y), (x, 1 - y), (1 - x, 1 - y)]


def _remote(src, dst, send_sems, recv_sems, k, to):
    return pltpu.make_async_remote_copy(src_ref=src, dst_ref=dst, send_sem=send_sems.at[k], recv_sem=recv_sems.at[k],
                                        device_id=to, device_id_type=MESH)


N_PARTS = 8


def _parts(rows):
    size = rows // N_PARTS
    assert size * N_PARTS == rows and size % 16 == 0, rows
    return [(p * size, size) for p in range(N_PARTS)]


def _rows(ref, lead, base, start, size):
    return ref.at[(*lead, pl.ds(pl.multiple_of(base + start, 16), size), slice(None))]


def _my_chip():
    return 2 * lax.axis_index("x") + lax.axis_index("y")


def _own_slot(packed, *, name):
    r, ncol = packed.shape
    tr = _pick(r, 512, 16)

    def body(x_ref, o_ref):
        o_ref[...] = x_ref[...]

    return pl.pallas_call(
        body, name=name, grid=(r // tr,),
        in_specs=[pl.BlockSpec((tr, ncol), lambda i: (i, 0))],
        out_specs=pl.BlockSpec((None, tr, ncol), lambda i: (_my_chip(), i, 0)),
        out_shape=jax.ShapeDtypeStruct((N_CHIPS, r, ncol), packed.dtype),
        compiler_params=_cp("arbitrary"),
    )(packed)


def _gather_shards(packed, slots, *, name):
    r, ncol = packed.shape
    hr = r // 2
    parts = _parts(hr)

    def body(x_ref, slots_ref, out_ref, send_sems, recv_sems):
        del slots_ref
        x, y, c = _place()
        chips = _other_chips(x, y)
        me = 2 * x + y

        def half(chip, cc):
            return _rows(out_ref, (2 * chip[0] + chip[1],), cc * hr, 0, hr)

        for j, chip in enumerate(chips):
            for st, sz in parts:
                _remote(_rows(x_ref, (), c * hr, st, sz), _rows(out_ref, (me,), c * hr, st, sz), send_sems, recv_sems, j,
                        (*chip, c)).start()
        for j, chip in enumerate(chips):
            _remote(half(chip, c), half(chip, c), send_sems, recv_sems, j, (x, y, c)).wait_recv()
            slot = 2 * chip[0] + chip[1]
            for st, sz in parts:
                _remote(_rows(out_ref, (slot,), c * hr, st, sz), _rows(out_ref, (slot,), c * hr, st, sz), send_sems,
                        recv_sems, 3 + j, (x, y, 1 - c)).start()
        for j, chip in enumerate(chips):
            _remote(half(chip, 1 - c), half(chip, 1 - c), send_sems, recv_sems, 3 + j, (x, y, c)).wait_recv()
        for k in range(6):
            _remote(half((x, y), c), half((x, y), c), send_sems, recv_sems, k, (x, y, c)).wait_send()

    return pl.pallas_call(
        body, name=name,
        out_shape=jax.ShapeDtypeStruct((N_CHIPS, r, ncol), packed.dtype),
        in_specs=[ANY, ANY], out_specs=ANY, input_output_aliases={1: 0},
        scratch_shapes=[pltpu.SemaphoreType.DMA((6,)), pltpu.SemaphoreType.DMA((6,))],
    )(packed, slots)


def _swap_halves(g, *, name):
    n, r, ncol = g.shape
    hr = r // 2
    parts = _parts(hr)

    def body(g_ref, got_ref, send_sems, recv_sems):
        x, y, c = _place()
        for s in range(n):
            for st, sz in parts:
                _remote(_rows(g_ref, (s,), (1 - c) * hr, st, sz), got_ref.at[s, pl.ds(st, sz), :], send_sems, recv_sems, 0,
                        (x, y, 1 - c)).start()
        _remote(got_ref, got_ref, send_sems, recv_sems, 0, (x, y, c)).wait()

    return pl.pallas_call(
        body, name=name, out_shape=jax.ShapeDtypeStruct((n, hr, ncol), g.dtype), in_specs=[ANY], out_specs=ANY,
        scratch_shapes=[pltpu.SemaphoreType.DMA((1,)), pltpu.SemaphoreType.DMA((1,))],
    )(g)


def _add_cores(g, got, *, name):
    n, r, ncol = g.shape
    hr = r // 2
    tr = _pick(hr, 512, 16)
    nb = hr // tr

    def body(a_ref, b_ref, o_ref):
        o_ref[...] = (a_ref[...] + b_ref[...]).astype(o_ref.dtype)

    blk = pl.BlockSpec((None, tr, ncol), lambda s, i: (s, i, 0))
    return pl.pallas_call(
        body, name=name, grid=(n, nb),
        in_specs=[pl.BlockSpec((None, tr, ncol), lambda s, i: (s, lax.axis_index("c") * nb + i, 0)), blk],
        out_specs=blk,
        out_shape=jax.ShapeDtypeStruct((n, hr, ncol), BF16),
        compiler_params=_cp("parallel", "parallel"),
    )(g, got)


def _scatter_to_chips(a, *, name):
    n, r, ncol = a.shape
    parts = _parts(r)

    def body(a_ref, got_ref, send_sems, recv_sems):
        x, y, c = _place()
        for st, sz in parts:
            for j, chip in enumerate(_other_chips(x, y)):
                _remote(a_ref.at[2 * chip[0] + chip[1], pl.ds(st, sz), :], got_ref.at[j, pl.ds(st, sz), :], send_sems,
                        recv_sems, j, (*chip, c)).start()
        for j in range(n - 1):
            _remote(got_ref.at[j], got_ref.at[j], send_sems, recv_sems, j, (x, y, c)).wait()

    return pl.pallas_call(
        body, name=name, out_shape=jax.ShapeDtypeStruct((n - 1, r, ncol), a.dtype), in_specs=[ANY], out_specs=ANY,
        scratch_shapes=[pltpu.SemaphoreType.DMA((3,)), pltpu.SemaphoreType.DMA((3,))],
    )(a)


def _add_chips(a, got, *, name):
    n, hr, ncol = a.shape
    tr = _pick(hr, 512, 16)
    nb = hr // tr

    def body(a_ref, g0_ref, g1_ref, g2_ref, o_ref):
        f = lambda ref: ref[...].astype(F32)
        o_ref[...] = ((f(a_ref) + f(g0_ref)) + f(g1_ref)) + f(g2_ref)

    other = lambda j: pl.BlockSpec((None, tr, ncol), lambda i: (j, i, 0))
    return pl.pallas_call(
        body, name=name, grid=(nb,),
        in_specs=[pl.BlockSpec((None, tr, ncol), lambda i: (_my_chip(), i, 0)), other(0), other(1), other(2)],
        out_specs=pl.BlockSpec((tr, ncol), lambda i: (lax.axis_index("c") * nb + i, 0)),
        out_shape=jax.ShapeDtypeStruct((2 * hr, ncol), F32),
        compiler_params=_cp("parallel"),
    )(a, got, got, got)


def _join_halves(buf, *, name):
    r, ncol = buf.shape
    hr = r // 2
    parts = _parts(hr)

    def body(b_ref, out_ref, send_sems, recv_sems):
        del b_ref
        x, y, c = _place()
        for st, sz in parts:
            _remote(_rows(out_ref, (), c * hr, st, sz), _rows(out_ref, (), c * hr, st, sz), send_sems, recv_sems, 0,
                    (x, y, 1 - c)).start()
        theirs = _rows(out_ref, (), (1 - c) * hr, 0, hr)
        _remote(theirs, theirs, send_sems, recv_sems, 0, (x, y, c)).wait()

    return pl.pallas_call(
        body, name=name, out_shape=jax.ShapeDtypeStruct((r, ncol), buf.dtype), in_specs=[ANY], out_specs=ANY,
        input_output_aliases={0: 0},
        scratch_shapes=[pltpu.SemaphoreType.DMA((1,)), pltpu.SemaphoreType.DMA((1,))],
    )(buf)


def _reduce_scatter(g, *, name):
    got = _swap_halves(g, name=name + "_swap")
    chip_sum = _add_cores(g, got, name=name + "_add_cores")
    others = _scatter_to_chips(chip_sum, name=name + "_scatter")
    return _join_halves(_add_chips(chip_sum, others, name=name + "_add_chips"), name=name + "_join")


def _all_gather_small(v, *, name):
    r, ncol = v.shape

    def body(x_ref, out_ref, send_sems, recv_sems, local_sem):
        x, y, c = _place()
        me, sibling = (x, y, c), (x, y, 1 - c)
        chips = _other_chips(x, y)

        def slot(p):
            return out_ref.at[4 * p[0] + 2 * p[1] + p[2]]

        mine = pltpu.make_async_copy(x_ref, slot(me), local_sem.at[0])
        mine.start()
        first = [_remote(x_ref, slot(me), send_sems, recv_sems, 0, sibling)]
        first += [_remote(x_ref, slot(me), send_sems, recv_sems, 1 + j, (*chip, c)) for j, chip in enumerate(chips)]
        for cp in first:
            cp.start()
        passed = [_remote(slot((*chip, c)), slot((*chip, c)), send_sems, recv_sems, 4 + j, sibling)
                  for j, chip in enumerate(chips)]
        for j, chip in enumerate(chips):
            _remote(slot((*chip, c)), slot((*chip, c)), send_sems, recv_sems, 1 + j, me).wait_recv()
            passed[j].start()
        _remote(slot(sibling), slot(sibling), send_sems, recv_sems, 0, me).wait_recv()
        for j, chip in enumerate(chips):
            _remote(slot((*chip, 1 - c)), slot((*chip, 1 - c)), send_sems, recv_sems, 4 + j, me).wait_recv()
        for cp in first + passed:
            cp.wait_send()
        mine.wait()

    vm = pl.BlockSpec(memory_space=pltpu.VMEM)
    return pl.pallas_call(
        body, name=name, out_shape=jax.ShapeDtypeStruct((N_DEV, r, ncol), v.dtype), in_specs=[vm], out_specs=vm,
        scratch_shapes=[pltpu.SemaphoreType.DMA((7,)), pltpu.SemaphoreType.DMA((7,)), pltpu.SemaphoreType.DMA((1,))],
    )(v)


def _sum_slots(g8, *, name):
    n, r, ncol = g8.shape

    def body(g_ref, o_ref):
        acc = g_ref[0]
        for k in range(1, n):
            acc = acc + g_ref[k]
        o_ref[...] = acc

    return pl.pallas_call(body, name=name, out_shape=jax.ShapeDtypeStruct((r, ncol), g8.dtype))(g8)


def _step(a):
    x = a["x"][0]
    s = x.shape[0]
    del s
    shard_shapes = [a[n].shape for n in SHARDED]
    rows = _pack_rows(shard_shapes)

    packed = _pack([a[n] for n in SHARDED], rows, BF16)
    gathered = _gather_shards(packed, _own_slot(packed, name="own_weights"), name="gather_weights")
    conv_rows = -(-math.prod(a["conv_w"].shape) // (LANE * 8)) * 8
    conv_all = _all_gather_small(
        jnp.pad(a["conv_w"].reshape(-1), (0, conv_rows * LANE - math.prod(a["conv_w"].shape))).reshape(conv_rows, LANE),
        name="gather_conv_w")
    per_chip = [dict(zip(SHARDED, _unpack(gathered[k], shard_shapes))) for k in range(N_CHIPS)]
    full = {n: jnp.concatenate([per_chip[k][n] for k in range(N_CHIPS)], axis=SHARD_AXIS[n]) for n in SHARDED}
    full["conv_w"] = jnp.concatenate(
        [conv_all[2 * k].reshape(-1)[:math.prod(a["conv_w"].shape)].reshape(a["conv_w"].shape) for k in range(N_CHIPS)],
        axis=SHARD_AXIS["conv_w"])
    for n in REPLICATED:
        full[n] = a[n]

    loss_part, grad_x, grads = _local_step(x, a["positions"][0], a["loss_target"][0], full)
    loss = lax.psum(jnp.sum(loss_part), ("x", "y", "c"))

    mats = []
    for k in range(N_CHIPS):
        parts = [jnp.split(grads[n], N_CHIPS, axis=SHARD_AXIS[n])[k] for n in SHARDED]
        mats += _pack_mats(parts, rows, F32)
    g_slots = jnp.concatenate(mats, axis=0).reshape(N_CHIPS, rows, PACK_COLS)
    g_shard = _reduce_scatter(g_slots, name="reduce_grads")

    rep_shapes = [a[n].shape for n in REPLICATED]
    n_rep = sum(math.prod(sh) for sh in rep_shapes)
    rep_rows = -(-n_rep // (LANE * 8)) * 8
    pack_small = lambda arrs: jnp.pad(jnp.concatenate([t.reshape(-1) for t in arrs]), (0, rep_rows * LANE - n_rep)).reshape(rep_rows, LANE)
    g_rep = _sum_slots(_all_gather_small(pack_small([grads[n] for n in REPLICATED]), name="gather_small_grads"),
                       name="add_small_grads")

    out = {"loss": loss, "grad_x": grad_x[None]}
    for n, g in zip(SHARDED, _unpack(g_shard, shard_shapes)):
        flat = lambda t: t.reshape(-1, t.shape[-1])
        d, nm, nv = _adamw(flat(a[n]), flat(g), flat(a["m_" + n]), flat(a["v_" + n]), name="adamw_" + n)
        out["grad_" + n] = g
        out["delta_" + n], out["new_m_" + n], out["new_v_" + n] = (t.reshape(g.shape) for t in (d, nm, nv))
    d_rp, m_rp, v_rp = _adamw(pack_small([a[n] for n in REPLICATED]), g_rep,
                              pack_small([a["m_" + n] for n in REPLICATED]),
                              pack_small([a["v_" + n] for n in REPLICATED]), name="adamw_replicated")
    for prefix, rp_arr in (("grad_", g_rep), ("delta_", d_rp), ("new_m_", m_rp), ("new_v_", v_rp)):
        for n, t in zip(REPLICATED, _unpack_flat(rp_arr.reshape(-1)[:n_rep], rep_shapes)):
            out[prefix + n] = t
    return out


IN_NAMES = ["x", "positions"] + W_NAMES + ["loss_target"] + ["m_" + n for n in W_NAMES] + ["v_" + n for n in W_NAMES]
OUT_NAMES = (["loss", "grad_x"] + ["grad_" + n for n in W_NAMES] + ["delta_" + n for n in W_NAMES]
             + ["new_m_" + n for n in W_NAMES] + ["new_v_" + n for n in W_NAMES])


def kernel(x, positions, ln_ffn1, ffn1_w13, ffn1_w2, ln_mix, w_in, conv_w, conv_b, dt_bias, a_log, d_skip, ssd_norm, w_ssd_out, q_lora_norm, w_uq, kv_lora_norm, w_ukv, q_norm, k_norm, w_mla_out, w_o, ln_ffn2, ffn2_w13, ffn2_w2, loss_target, m_ln_ffn1, m_ffn1_w13, m_ffn1_w2, m_ln_mix, m_w_in, m_conv_w, m_conv_b, m_dt_bias, m_a_log, m_d_skip, m_ssd_norm, m_w_ssd_out, m_q_lora_norm, m_w_uq, m_kv_lora_norm, m_w_ukv, m_q_norm, m_k_norm, m_w_mla_out, m_w_o, m_ln_ffn2, m_ffn2_w13, m_ffn2_w2, v_ln_ffn1, v_ffn1_w13, v_ffn1_w2, v_ln_mix, v_w_in, v_conv_w, v_conv_b, v_dt_bias, v_a_log, v_d_skip, v_ssd_norm, v_w_ssd_out, v_q_lora_norm, v_w_uq, v_kv_lora_norm, v_w_ukv, v_q_norm, v_k_norm, v_w_mla_out, v_w_o, v_ln_ffn2, v_ffn2_w13, v_ffn2_w2):
    given = locals()
    out = _step({n: given[n] for n in IN_NAMES})
    return tuple(out[n] for n in OUT_NAMES)
```

```python
import functools
import math

import jax
import jax.numpy as jnp
from jax import lax
from jax.experimental import pallas as pl
from jax.experimental.pallas import tpu as pltpu

F32 = jnp.float32
BF16 = jnp.bfloat16

D_MODEL = 1024
DEPTH = 2
D_FF = 2816
SSD_D_INNER = 2048
SSD_HEADS = 32
SSD_HEAD_DIM = 64
SSD_GROUPS = 4
SSD_STATE = 128
SSD_CHUNK = 128
SSD_CONV = 4
SSD_CONV_DIM = 3072
MLA_HEADS = 8
MLA_Q_LORA = 512
MLA_KV_LORA = 256
MLA_NOPE = 128
MLA_ROPE = 64
MLA_V = 128
MLA_QK = 192
ROPE_THETA = 10000.0
EPS = 1e-6
ADAM_LR = 0.001
ADAM_B1 = 0.9
ADAM_B2 = 0.999
ADAM_EPS = 1e-08
ADAM_WD = 0.01
ADAM_STEP = 10

PROJ_W = 8064
OFF_Z, OFF_XBC, OFF_GATES, OFF_CQ, OFF_CKV, OFF_KRDT = 0, 2048, 5120, 7168, 7680, 7936

LANE = 128
VMEM_LIMIT = 48 * 1024 * 1024
HI = lax.Precision.HIGHEST


def _cp(*sem):
    return pltpu.CompilerParams(dimension_semantics=sem, vmem_limit_bytes=VMEM_LIMIT)


def _pick(dim, target, align):
    if dim <= target:
        return dim
    b = (target // align) * align
    while b >= align:
        if dim % b == 0:
            return b
        b -= align
    raise ValueError(f"no block for {dim} (target {target}, align {align})")


def _silu(x):
    return x * jax.nn.sigmoid(x)


def _dsilu(x):
    s = jax.nn.sigmoid(x)
    return s * (1.0 + x * (1.0 - s))


MM_VMEM_BUDGET = 40 * 1024 * 1024


def _mm_tiles(m, n, k, a_bytes, b_bytes, o_bytes):
    bn = _pick(n, 1408, LANE)
    for nk in (1, 2, 3, 4, 6, 7, 8):
        if k % nk or (k // nk) % LANE:
            continue
        bk = k // nk
        for bm in (1024, 512):
            if m % bm:
                continue
            need = 2 * (bm * bk * a_bytes + bk * bn * b_bytes + bm * bn * o_bytes) + (bm * bn * 4 if nk > 1 else 0)
            if need <= MM_VMEM_BUDGET:
                return bm, bn, bk
    return _pick(m, 512, 8), bn, _pick(k, 1536, LANE)

def _matmul(a, b, mode, *, name, out_dtype=F32, scale=1.0, res=None):
    if mode == "nn":
        (m, k), (k2, n) = a.shape, b.shape
    elif mode == "nt":
        (m, k), (n, k2) = a.shape, b.shape
    else:
        (k, m), (k2, n) = a.shape, b.shape
    assert k == k2, (a.shape, b.shape, mode)
    if mode == "tn":
        bn, bk = _pick(n, 2816, LANE), _pick(k, 1024, 8)
        bm = _pick(m, max(256, (1408 * 1024 // bn) // LANE * LANE), LANE)
    else:
        bm, bn, bk = _mm_tiles(m, n, k, a.dtype.itemsize, b.dtype.itemsize,
                               jnp.dtype(out_dtype).itemsize + (4 if res is not None else 0))
    nk = k // bk

    def body(a_ref, b_ref, *rest):
        res_ref = rest[0] if res is not None else None
        o_ref = rest[-2] if nk > 1 else rest[-1]
        kk = pl.program_id(2)
        av = a_ref[...].astype(BF16)
        bv = b_ref[...].astype(BF16)
        if mode == "nn":
            dims = (((1,), (0,)), ((), ()))
        elif mode == "nt":
            dims = (((1,), (1,)), ((), ()))
        else:
            dims = (((0,), (0,)), ((), ()))
        part = lax.dot_general(av, bv, dims, preferred_element_type=F32)

        def finish(total):
            out = total * scale
            if res_ref is not None:
                out = res_ref[...] + out
            o_ref[...] = out.astype(o_ref.dtype)

        if nk == 1:
            finish(part)
            return
        acc_ref = rest[-1]

        @pl.when(kk == 0)
        def _():
            acc_ref[...] = part

        @pl.when((kk > 0) & (kk < nk - 1))
        def _():
            acc_ref[...] += part

        @pl.when(kk == nk - 1)
        def _():
            finish(acc_ref[...] + part)

    o_spec = pl.BlockSpec((bm, bn), lambda i, j, kk: (i, j))
    if mode == "nn":
        a_spec = pl.BlockSpec((bm, bk), lambda i, j, kk: (i, kk))
        b_spec = pl.BlockSpec((bk, bn), lambda i, j, kk: (kk, j))
    elif mode == "nt":
        a_spec = pl.BlockSpec((bm, bk), lambda i, j, kk: (i, kk))
        b_spec = pl.BlockSpec((bn, bk), lambda i, j, kk: (j, kk))
    else:
        a_spec = pl.BlockSpec((bk, bm), lambda i, j, kk: (kk, i))
        b_spec = pl.BlockSpec((bk, bn), lambda i, j, kk: (kk, j))
    return pl.pallas_call(
        body, name=name,
        grid=(m // bm, n // bn, nk),
        in_specs=[a_spec, b_spec] + ([o_spec] if res is not None else []),
        out_specs=o_spec,
        out_shape=jax.ShapeDtypeStruct((m, n), out_dtype),
        scratch_shapes=[pltpu.VMEM((bm, bn), F32)] if nk > 1 else [],
        compiler_params=_cp("parallel", "parallel", "arbitrary"),
    )(*((a, b) + ((res,) if res is not None else ())))


def _rms_fwd(x, g, *, name, col=0, width=None):
    r = x.shape[0]
    w = width or x.shape[1]
    tr = _pick(r, 512, 16)

    def body(x_ref, g_ref, o_ref):
        xv = x_ref[...]
        rs = lax.rsqrt(jnp.mean(xv * xv, axis=-1, keepdims=True) + EPS)
        o_ref[...] = (xv * rs * g_ref[...]).astype(o_ref.dtype)

    return pl.pallas_call(
        body, name=name, grid=(r // tr,),
        in_specs=[pl.BlockSpec((tr, w), lambda i: (i, col)), pl.BlockSpec((1, w), lambda i: (0, 0))],
        out_specs=pl.BlockSpec((tr, w), lambda i: (i, 0)),
        out_shape=jax.ShapeDtypeStruct((r, w), BF16),
        compiler_params=_cp("parallel"),
    )(x, g)


def _rms_bwd(x, g, dy, *, name, col=0, width=None, res=None, out_dtype=F32):
    r = x.shape[0]
    w = width or x.shape[1]
    tr = _pick(r, 512, 8)

    def body(x_ref, g_ref, dy_ref, *rest):
        res_ref = rest[0] if res is not None else None
        dx_ref, dg_ref = rest[-2:]
        i = pl.program_id(0)
        xv = x_ref[...]
        dyv = dy_ref[...]
        rs = lax.rsqrt(jnp.mean(xv * xv, axis=-1, keepdims=True) + EPS)
        xh = xv * rs
        dxh = dyv * g_ref[...]
        mm = jnp.mean(dxh * xh, axis=-1, keepdims=True)
        dx = rs * (dxh - xh * mm)
        if res_ref is not None:
            dx = res_ref[...] + dx
        dx_ref[...] = dx.astype(dx_ref.dtype)
        part = jnp.sum(dyv * xh, axis=0, keepdims=True)

        @pl.when(i == 0)
        def _():
            dg_ref[...] = part

        @pl.when(i > 0)
        def _():
            dg_ref[...] += part

    blk = pl.BlockSpec((tr, w), lambda i: (i, 0))
    return pl.pallas_call(
        body, name=name, grid=(r // tr,),
        in_specs=[pl.BlockSpec((tr, w), lambda i: (i, col)), pl.BlockSpec((1, w), lambda i: (0, 0)), blk]
        + ([blk] if res is not None else []),
        out_specs=[blk, pl.BlockSpec((1, w), lambda i: (0, 0))],
        out_shape=[jax.ShapeDtypeStruct((r, w), out_dtype), jax.ShapeDtypeStruct((1, w), F32)],
        compiler_params=_cp("arbitrary"),
    )(*((x, g, dy) + ((res,) if res is not None else ())))


def _gated_rms_fwd(y, proj, g, *, name):
    r, w = y.shape
    tr = _pick(r, 256, 8)

    def body(y_ref, z_ref, g_ref, o_ref):
        t = y_ref[...] * _silu(z_ref[...])
        rs = lax.rsqrt(jnp.mean(t * t, axis=-1, keepdims=True) + EPS)
        o_ref[...] = (t * rs * g_ref[...]).astype(o_ref.dtype)

    return pl.pallas_call(
        body, name=name, grid=(r // tr,),
        in_specs=[pl.BlockSpec((tr, w), lambda i: (i, 0)), pl.BlockSpec((tr, w), lambda i: (i, OFF_Z // w)),
                  pl.BlockSpec((1, w), lambda i: (0, 0))],
        out_specs=pl.BlockSpec((tr, w), lambda i: (i, 0)),
        out_shape=jax.ShapeDtypeStruct((r, w), BF16),
        compiler_params=_cp("parallel"),
    )(y, proj, g)


def _gated_rms_bwd(y, proj, g, do, *, name):
    r, w = y.shape
    tr = _pick(r, 256, 8)

    def body(y_ref, z_ref, g_ref, do_ref, dy_ref, dz_ref, dg_ref):
        i = pl.program_id(0)
        yv, zv, dov = y_ref[...], z_ref[...], do_ref[...]
        sz = _silu(zv)
        t = yv * sz
        rs = lax.rsqrt(jnp.mean(t * t, axis=-1, keepdims=True) + EPS)
        th = t * rs
        dth = dov * g_ref[...]
        mm = jnp.mean(dth * th, axis=-1, keepdims=True)
        dt = rs * (dth - th * mm)
        dy_ref[...] = dt * sz
        dz_ref[...] = (dt * yv * _dsilu(zv)).astype(dz_ref.dtype)
        part = jnp.sum(dov * th, axis=0, keepdims=True)

        @pl.when(i == 0)
        def _():
            dg_ref[...] = part

        @pl.when(i > 0)
        def _():
            dg_ref[...] += part

    blk = pl.BlockSpec((tr, w), lambda i: (i, 0))
    vec = pl.BlockSpec((1, w), lambda i: (0, 0))
    return pl.pallas_call(
        body, name=name, grid=(r // tr,),
        in_specs=[blk, pl.BlockSpec((tr, w), lambda i: (i, OFF_Z // w)), vec, blk],
        out_specs=[blk, blk, vec],
        out_shape=[jax.ShapeDtypeStruct((r, w), F32), jax.ShapeDtypeStruct((r, w), BF16),
                   jax.ShapeDtypeStruct((1, w), F32)],
        compiler_params=_cp("arbitrary"),
    )(y, proj, g, do)


def _ffn_up(n, w13, *, name):
    m, k = n.shape
    f = w13.shape[1] // 2
    bm, bn = _pick(m, 1024, 16), _pick(f, 1408, LANE)
    nj = f // bn

    def body(a_ref, wg_ref, wu_ref, act_ref, g_ref, u_ref):
        a = a_ref[...].astype(BF16)
        g = _dot(a, wg_ref[...].astype(BF16))
        u = _dot(a, wu_ref[...].astype(BF16))
        act_ref[...] = (_silu(g) * u).astype(act_ref.dtype)
        g_ref[...] = g.astype(g_ref.dtype)
        u_ref[...] = u.astype(u_ref.dtype)

    out = pl.BlockSpec((bm, bn), lambda j, i: (i, j))
    sh = jax.ShapeDtypeStruct((m, f), BF16)
    return pl.pallas_call(
        body, name=name, grid=(nj, m // bm),
        in_specs=[pl.BlockSpec((bm, k), lambda j, i: (i, 0)), pl.BlockSpec((k, bn), lambda j, i: (0, j)),
                  pl.BlockSpec((k, bn), lambda j, i: (0, nj + j))],
        out_specs=[out, out, out], out_shape=[sh, sh, sh],
        compiler_params=_cp("parallel", "parallel"),
    )(n, w13, w13)


def _swiglu_bwd(g, u, da, *, name):
    r, f = g.shape
    tr = _pick(r, 256, 16)

    def body(g_ref, u_ref, da_ref, o_ref):
        gv, uv, dav = g_ref[...].astype(F32), u_ref[...].astype(F32), da_ref[...].astype(F32)
        o_ref[:, :f] = (dav * uv * _dsilu(gv)).astype(o_ref.dtype)
        o_ref[:, f:] = (dav * _silu(gv)).astype(o_ref.dtype)

    blk = pl.BlockSpec((tr, f), lambda i: (i, 0))
    return pl.pallas_call(
        body, name=name, grid=(r // tr,),
        in_specs=[blk, blk, blk],
        out_specs=pl.BlockSpec((tr, 2 * f), lambda i: (i, 0)),
        out_shape=jax.ShapeDtypeStruct((r, 2 * f), BF16),
        compiler_params=_cp("parallel"),
    )(g, u, da)


CONV_TS = 1024
CONV_TC = 512


def _conv_pre(x, carry, w_ref, b_ref):
    ts = x.shape[0]
    row8 = lax.broadcasted_iota(jnp.int32, (8, x.shape[1]), 0)
    head_x = x[0:8]
    shifted, shifted_head = [], []
    for j in range(SSD_CONV):
        if j == 0:
            shifted.append(x)
            shifted_head.append(head_x)
        else:
            shifted.append(pltpu.roll(x, j, 0))
            shifted_head.append(jnp.where(row8 < j, pltpu.roll(carry, j, 0), pltpu.roll(head_x, j, 0)))
    pre = b_ref[...] + sum(w_ref[SSD_CONV - 1 - j:SSD_CONV - j, :] * shifted[j] for j in range(SSD_CONV))
    pre_head = b_ref[...] + sum(w_ref[SSD_CONV - 1 - j:SSD_CONV - j, :] * shifted_head[j] for j in range(SSD_CONV))
    del ts
    return pre, pre_head, shifted, shifted_head


def _conv_fwd(proj, w, b, *, name):
    s = proj.shape[0]
    c = w.shape[1]
    ts, tc = _pick(s, CONV_TS, 8), CONV_TC
    off = OFF_XBC // tc

    def body(x_ref, w_ref, b_ref, o_ref, carry_ref):
        t = pl.program_id(1)

        @pl.when(t == 0)
        def _():
            carry_ref[...] = jnp.zeros_like(carry_ref)

        x = x_ref[...]
        pre, pre_head, _, _ = _conv_pre(x, carry_ref[...], w_ref, b_ref)
        o_ref[...] = _silu(pre)
        o_ref[0:8, :] = _silu(pre_head)
        carry_ref[...] = x[ts - 8:ts]

    return pl.pallas_call(
        body, name=name, grid=(c // tc, s // ts),
        in_specs=[pl.BlockSpec((ts, tc), lambda j, t: (t, j + off)), pl.BlockSpec((SSD_CONV, tc), lambda j, t: (0, j)),
                  pl.BlockSpec((1, tc), lambda j, t: (0, j))],
        out_specs=pl.BlockSpec((ts, tc), lambda j, t: (t, j)),
        out_shape=jax.ShapeDtypeStruct((s, c), F32),
        scratch_shapes=[pltpu.VMEM((8, tc), F32)],
        compiler_params=_cp("parallel", "arbitrary"),
    )(proj, w, b)


def _conv_bwd_pre(proj, w, b, dy, *, name):
    s = proj.shape[0]
    c = w.shape[1]
    ts, tc = _pick(s, CONV_TS, 8), CONV_TC
    off = OFF_XBC // tc

    def body(x_ref, w_ref, b_ref, dy_ref, dp_ref, dw_ref, db_ref, carry_ref):
        t = pl.program_id(1)

        @pl.when(t == 0)
        def _():
            carry_ref[...] = jnp.zeros_like(carry_ref)
            dw_ref[...] = jnp.zeros_like(dw_ref)
            db_ref[...] = jnp.zeros_like(db_ref)

        x = x_ref[...]
        pre, pre_head, shifted, shifted_head = _conv_pre(x, carry_ref[...], w_ref, b_ref)
        dyv = dy_ref[...]
        dp = dyv * _dsilu(pre)
        dp_head = dyv[0:8] * _dsilu(pre_head)
        row = lax.broadcasted_iota(jnp.int32, dp.shape, 0)
        dp_tail = jnp.where(row >= 8, dp, 0.0)
        dp_ref[...] = dp
        dp_ref[0:8, :] = dp_head
        db_ref[...] += jnp.sum(dp_tail, axis=0, keepdims=True) + jnp.sum(dp_head, axis=0, keepdims=True)
        for j in range(SSD_CONV):
            kk = SSD_CONV - 1 - j
            dw_ref[kk:kk + 1, :] += (jnp.sum(dp_tail * shifted[j], axis=0, keepdims=True)
                                     + jnp.sum(dp_head * shifted_head[j], axis=0, keepdims=True))
        carry_ref[...] = x[ts - 8:ts]

    return pl.pallas_call(
        body, name=name, grid=(c // tc, s // ts),
        in_specs=[pl.BlockSpec((ts, tc), lambda j, t: (t, j + off)), pl.BlockSpec((SSD_CONV, tc), lambda j, t: (0, j)),
                  pl.BlockSpec((1, tc), lambda j, t: (0, j)), pl.BlockSpec((ts, tc), lambda j, t: (t, j))],
        out_specs=[pl.BlockSpec((ts, tc), lambda j, t: (t, j)), pl.BlockSpec((SSD_CONV, tc), lambda j, t: (0, j)),
                   pl.BlockSpec((1, tc), lambda j, t: (0, j))],
        out_shape=[jax.ShapeDtypeStruct((s, c), F32), jax.ShapeDtypeStruct((SSD_CONV, c), F32),
                   jax.ShapeDtypeStruct((1, c), F32)],
        scratch_shapes=[pltpu.VMEM((8, tc), F32)],
        compiler_params=_cp("parallel", "arbitrary"),
    )(proj, w, b, dy)


def _conv_bwd_x(dp, w, *, name):
    s, c = dp.shape
    ts, tc = _pick(s, CONV_TS, 8), CONV_TC
    nt = s // ts

    def body(d_ref, w_ref, o_ref, carry_ref, full_ref):
        t = pl.program_id(1)

        @pl.when(t == 0)
        def _():
            carry_ref[...] = jnp.zeros_like(carry_ref)

        d = d_ref[...]
        carry = carry_ref[...]
        row8 = lax.broadcasted_iota(jnp.int32, (8, tc), 0)
        tail = d[ts - 8:ts]
        acc = w_ref[SSD_CONV - 1:SSD_CONV, :] * d
        acc_tail = w_ref[SSD_CONV - 1:SSD_CONV, :] * tail
        for j in range(1, SSD_CONV):
            wj = w_ref[SSD_CONV - 1 - j:SSD_CONV - j, :]
            acc = acc + wj * pltpu.roll(d, ts - j, 0)
            up_tail = jnp.where(row8 >= 8 - j, pltpu.roll(carry, 8 - j, 0), pltpu.roll(tail, 8 - j, 0))
            acc_tail = acc_tail + wj * up_tail
        full_ref[...] = acc
        full_ref[ts - 8:ts, :] = acc_tail
        o_ref[...] = full_ref[...].astype(o_ref.dtype)
        carry_ref[...] = d[0:8]

    return pl.pallas_call(
        body, name=name, grid=(c // tc, nt),
        in_specs=[pl.BlockSpec((ts, tc), lambda j, t: (nt - 1 - t, j)), pl.BlockSpec((SSD_CONV, tc), lambda j, t: (0, j))],
        out_specs=pl.BlockSpec((ts, tc), lambda j, t: (nt - 1 - t, j)),
        out_shape=jax.ShapeDtypeStruct((s, c), BF16),
        scratch_shapes=[pltpu.VMEM((8, tc), F32), pltpu.VMEM((ts, tc), F32)],
        compiler_params=_cp("parallel", "arbitrary"),
    )(dp, w)


def _merge_fwd(proj, ys, ym, *, name):
    r, w = ys.shape
    tr = _pick(r, 512, 8)
    off = OFF_GATES // w

    def body(g1_ref, g2_ref, ys_ref, ym_ref, o_ref):
        o_ref[...] = (jax.nn.sigmoid(g1_ref[...]) * ys_ref[...]
                      + jax.nn.sigmoid(g2_ref[...]) * ym_ref[...]).astype(o_ref.dtype)

    blk = pl.BlockSpec((tr, w), lambda i: (i, 0))
    return pl.pallas_call(
        body, name=name, grid=(r // tr,),
        in_specs=[pl.BlockSpec((tr, w), lambda i: (i, off)), pl.BlockSpec((tr, w), lambda i: (i, off + 1)), blk, blk],
        out_specs=blk, out_shape=jax.ShapeDtypeStruct((r, w), BF16),
        compiler_params=_cp("parallel"),
    )(proj, proj, ys, ym)


def _merge_bwd(proj, ys, ym, dm, *, name):
    r, w = ys.shape
    tr = _pick(r, 512, 8)
    off = OFF_GATES // w

    def body(g1_ref, g2_ref, ys_ref, ym_ref, dm_ref, dg_ref, dys_ref, dym_ref):
        s1, s2 = jax.nn.sigmoid(g1_ref[...]), jax.nn.sigmoid(g2_ref[...])
        dmv = dm_ref[...]
        dys_ref[...] = (dmv * s1).astype(dys_ref.dtype)
        dym_ref[...] = (dmv * s2).astype(dym_ref.dtype)
        dg_ref[:, :w] = (dmv * ys_ref[...] * s1 * (1.0 - s1)).astype(dg_ref.dtype)
        dg_ref[:, w:] = (dmv * ym_ref[...] * s2 * (1.0 - s2)).astype(dg_ref.dtype)

    blk = pl.BlockSpec((tr, w), lambda i: (i, 0))
    return pl.pallas_call(
        body, name=name, grid=(r // tr,),
        in_specs=[pl.BlockSpec((tr, w), lambda i: (i, off)), pl.BlockSpec((tr, w), lambda i: (i, off + 1)), blk, blk, blk],
        out_specs=[pl.BlockSpec((tr, 2 * w), lambda i: (i, 0)), blk, blk],
        out_shape=[jax.ShapeDtypeStruct((r, 2 * w), BF16), jax.ShapeDtypeStruct((r, w), BF16),
                   jax.ShapeDtypeStruct((r, w), BF16)],
        compiler_params=_cp("parallel"),
    )(proj, proj, ys, ym, dm)


def _loss_fwd_bwd(y, target, *, name):
    r, w = y.shape
    tr = _pick(r, 512, 8)

    def body(y_ref, t_ref, l_ref, dy_ref):
        i = pl.program_id(0)
        e = y_ref[...] - t_ref[...]
        dy_ref[...] = e * (1.0 / w)
        part = jnp.sum(e * e, axis=0, keepdims=True) * (0.5 / w)

        @pl.when(i == 0)
        def _():
            l_ref[...] = part

        @pl.when(i > 0)
        def _():
            l_ref[...] += part

    blk = pl.BlockSpec((tr, w), lambda i: (i, 0))
    return pl.pallas_call(
        body, name=name, grid=(r // tr,),
        in_specs=[blk, blk],
        out_specs=[pl.BlockSpec((1, w), lambda i: (0, 0)), blk],
        out_shape=[jax.ShapeDtypeStruct((1, w), F32), jax.ShapeDtypeStruct((r, w), F32)],
        compiler_params=_cp("arbitrary"),
    )(y, target)


def _adamw(w, g, m, v, *, name):
    r, c = w.shape
    tr = _pick(r, max(8, (1 << 20) // (4 * c) // 8 * 8), 8)
    c1 = 1.0 - ADAM_B1 ** ADAM_STEP
    c2 = 1.0 - ADAM_B2 ** ADAM_STEP

    def body(w_ref, g_ref, m_ref, v_ref, d_ref, nm_ref, nv_ref):
        gv = g_ref[...]
        nm = ADAM_B1 * m_ref[...] + (1.0 - ADAM_B1) * gv
        nv = ADAM_B2 * v_ref[...] + (1.0 - ADAM_B2) * (gv * gv)
        nm_ref[...] = nm
        nv_ref[...] = nv
        d_ref[...] = -ADAM_LR * ((nm / c1) / (jnp.sqrt(nv / c2) + ADAM_EPS) + ADAM_WD * w_ref[...])

    blk = pl.BlockSpec((tr, c), lambda i: (i, 0))
    sh = jax.ShapeDtypeStruct((r, c), F32)
    return pl.pallas_call(
        body, name=name, grid=(r // tr,),
        in_specs=[blk] * 4, out_specs=[blk] * 3, out_shape=[sh] * 3,
        compiler_params=_cp("parallel"),
    )(w, g, m, v)


def _softplus(x):
    return jnp.maximum(x, 0.0) + jnp.log(1.0 + jnp.exp(-jnp.abs(x)))


def _dot_01(x, sel):
    sel_b = sel.astype(BF16)
    acc, rem = None, x
    for _ in range(3):
        piece = rem.astype(BF16)
        part = jnp.dot(piece, sel_b, preferred_element_type=F32)
        acc = part if acc is None else acc + part
        rem = rem - piece.astype(F32)
    return acc


def _ssd_common(dtr_ref, dtrT_ref, dtb_ref, dtbT_ref, al_ref, alT_ref, e_ref):
    L = SSD_CHUNK
    ri = lax.broadcasted_iota(jnp.int32, (L, L), 0)
    cj = lax.broadcasted_iota(jnp.int32, (L, L), 1)
    tril = (ri >= cj).astype(F32)
    triu = (ri <= cj).astype(F32)
    a = -jnp.exp(al_ref[...])
    aT = -jnp.exp(alT_ref[...])
    pre = dtr_ref[...] + dtb_ref[...]
    preT = dtrT_ref[...] + dtbT_ref[...]
    dt = _softplus(pre)
    dtT = _softplus(preT)
    acum = jnp.dot(tril, dt * a, precision=HI, preferred_element_type=F32)
    acumT = jnp.dot(dtT * aT, triu, precision=HI, preferred_element_type=F32)
    e = e_ref[...]
    dt_x = _dot_01(dt, e)
    acum_x = _dot_01(acum, e)
    last_x = acum_x[L - 1:L, :]
    return dict(ri=ri, cj=cj, tril=tril, triu=triu, a=a, aT=aT, pre=pre, preT=preT, dt=dt, dtT=dtT,
                acum=acum, acumT=acumT, dt_x=dt_x, eacum_x=jnp.exp(acum_x), w_x=jnp.exp(last_x - acum_x),
                elast_x=jnp.exp(last_x))


def _dot_nt(a, b):
    return lax.dot_general(a, b, (((1,), (1,)), ((), ())), preferred_element_type=F32)


def _dot_tn(a, b):
    return lax.dot_general(a, b, (((0,), (0,)), ((), ())), preferred_element_type=F32)


def _dot(a, b):
    return jnp.dot(a, b, preferred_element_type=F32)


def _ssd_specs(nc, rev):
    L = SSD_CHUNK
    ix = (lambda c: nc - 1 - c) if rev else (lambda c: c)
    return [
        pl.BlockSpec((L, SSD_D_INNER), lambda c: (ix(c), 0)),
        pl.BlockSpec((L, 512), lambda c: (ix(c), 4)),
        pl.BlockSpec((L, 512), lambda c: (ix(c), 5)),
        pl.BlockSpec((L, SSD_HEADS), lambda c: (ix(c), 0)),
        pl.BlockSpec((SSD_HEADS, L), lambda c: (0, ix(c))),
        pl.BlockSpec((1, SSD_HEADS), lambda c: (0, 0)),
        pl.BlockSpec((SSD_HEADS, 1), lambda c: (0, 0)),
        pl.BlockSpec((1, SSD_HEADS), lambda c: (0, 0)),
        pl.BlockSpec((SSD_HEADS, 1), lambda c: (0, 0)),
        pl.BlockSpec((1, SSD_D_INNER), lambda c: (0, 0)),
        pl.BlockSpec((SSD_HEADS, SSD_D_INNER), lambda c: (0, 0)),
    ]


def _ssd_fwd(xc, dtr, dtrT, dtb, dtbT, alog, alogT, dskx, expand, *, name):
    s = xc.shape[0]
    L = SSD_CHUNK
    nc = s // L

    def body(x_ref, b_ref, c_ref, dtr_ref, dtrT_ref, dtb_ref, dtbT_ref, al_ref, alT_ref, dsk_ref, e_ref,
             y_ref, st_ref, state):
        ci = pl.program_id(0)

        @pl.when(ci == 0)
        def _():
            state[...] = jnp.zeros_like(state)

        st_ref[0] = state[...]
        q = _ssd_common(dtr_ref, dtrT_ref, dtb_ref, dtbT_ref, al_ref, alT_ref, e_ref)
        causal = q["ri"] >= q["cj"]
        lane_lo = q["cj"] < 64
        x = x_ref[...]
        xdt = x * q["dt_x"]
        xdt_b = xdt.astype(BF16)
        xdtw_b = (xdt * q["w_x"]).astype(BF16)
        for g in range(SSD_GROUPS):
            bg = b_ref[:, 128 * g:128 * g + 128]
            cg_b = c_ref[:, 128 * g:128 * g + 128].astype(BF16)
            cb = _dot_nt(cg_b, bg.astype(BF16))
            bgT_b = bg.T.astype(BF16)
            s0 = state[g]
            for jj in range(4):
                j = 4 * g + jj
                sl = slice(128 * j, 128 * j + 128)
                sls = slice(128 * jj, 128 * jj + 128)
                ms = []
                for h in (2 * j, 2 * j + 1):
                    seg = q["acum"][:, h:h + 1] - q["acumT"][h:h + 1, :]
                    decay = jnp.exp(jnp.where(causal, seg, -jnp.inf))
                    ms.append((cb * decay).astype(BF16))
                mcat = jnp.concatenate(ms, axis=1)
                xp = xdt_b[:, sl]
                zero = jnp.zeros_like(xp)
                xstack = jnp.concatenate([jnp.where(lane_lo, xp, zero), jnp.where(lane_lo, zero, xp)], axis=0)
                y = _dot(mcat, xstack)
                y = y + q["eacum_x"][:, sl] * _dot(cg_b, s0[:, sls].astype(BF16))
                y = y + x[:, sl] * dsk_ref[:, sl]
                y_ref[:, sl] = y
                state[g, :, sls] = s0[:, sls] * q["elast_x"][:, sl] + _dot(bgT_b, xdtw_b[:, sl])

    return pl.pallas_call(
        body, name=name, grid=(nc,),
        in_specs=_ssd_specs(nc, False),
        out_specs=[pl.BlockSpec((L, SSD_D_INNER), lambda c: (c, 0)),
                   pl.BlockSpec((1, SSD_GROUPS, SSD_STATE, 512), lambda c: (c, 0, 0, 0))],
        out_shape=[jax.ShapeDtypeStruct((s, SSD_D_INNER), F32),
                   jax.ShapeDtypeStruct((nc, SSD_GROUPS, SSD_STATE, 512), F32)],
        scratch_shapes=[pltpu.VMEM((SSD_GROUPS, SSD_STATE, 512), F32)],
        compiler_params=_cp("arbitrary"),
    )(xc, xc, xc, dtr, dtrT, dtb, dtbT, alog, alogT, dskx, expand)


def _ssd_bwd(xc, dtr, dtrT, dtb, dtbT, alog, alogT, dskx, expand, expandT, states, dy, *, name):
    s = xc.shape[0]
    L = SSD_CHUNK
    H = SSD_HEADS
    nc = s // L

    def body(x_ref, b_ref, c_ref, dtr_ref, dtrT_ref, dtb_ref, dtbT_ref, al_ref, alT_ref, dsk_ref, e_ref,
             et_ref, st_ref, dy_ref,
             dxc_ref, ddtc_ref, ddtr_ref, dbc_ref, dbr_ref, dac_ref, dar_ref, ddsk_ref, dstate):
        ci = pl.program_id(0)

        @pl.when(ci == 0)
        def _():
            dstate[...] = jnp.zeros_like(dstate)
            dbc_ref[...] = jnp.zeros_like(dbc_ref)
            dbr_ref[...] = jnp.zeros_like(dbr_ref)
            dac_ref[...] = jnp.zeros_like(dac_ref)
            dar_ref[...] = jnp.zeros_like(dar_ref)
            ddsk_ref[...] = jnp.zeros_like(ddsk_ref)

        q = _ssd_common(dtr_ref, dtrT_ref, dtb_ref, dtbT_ref, al_ref, alT_ref, e_ref)
        ri, cj = q["ri"], q["cj"]
        causal = ri >= cj
        causalT = ri <= cj
        lane_lo = cj < 64
        lane_h = lax.broadcasted_iota(jnp.int32, (1, H), 1)
        sub_h = lax.broadcasted_iota(jnp.int32, (H, 1), 0)
        x = x_ref[...]
        dyv = dy_ref[...]
        xdt = x * q["dt_x"]
        xdt_b = xdt.astype(BF16)
        xdtw = xdt * q["w_x"]
        xdtw_b = xdtw.astype(BF16)
        edy = q["eacum_x"] * dyv
        edy_b = edy.astype(BF16)
        dyv_b = dyv.astype(BF16)
        dacum_col = jnp.zeros((L, H), F32)
        dacum_row = jnp.zeros((H, L), F32)
        dxdt_t, yoff_t, u_t, r_t = [], [], [], []
        for g in range(SSD_GROUPS):
            bg = b_ref[:, 128 * g:128 * g + 128]
            cg = c_ref[:, 128 * g:128 * g + 128]
            bg_b, cg_b = bg.astype(BF16), cg.astype(BF16)
            cb = _dot_nt(cg_b, bg_b)
            cbT = _dot_nt(bg_b, cg_b)
            cgT_b = cg.T.astype(BF16)
            s0 = st_ref[0, g]
            ds = dstate[g]
            s0_b, ds_b = s0.astype(BF16), ds.astype(BF16)
            dcb = jnp.zeros((L, L), F32)
            for jj in range(4):
                j = 4 * g + jj
                sl = slice(128 * j, 128 * j + 128)
                sls = slice(128 * jj, 128 * jj + 128)
                decs, mts = [], []
                for h in (2 * j, 2 * j + 1):
                    seg = q["acum"][:, h:h + 1] - q["acumT"][h:h + 1, :]
                    decs.append(jnp.exp(jnp.where(causal, seg, -jnp.inf)))
                    mts.append((cbT * jnp.exp(jnp.where(causalT, -seg, -jnp.inf))).astype(BF16))
                dyt_b = dyv_b[:, sl]
                zero = jnp.zeros_like(dyt_b)
                dystack = jnp.concatenate([jnp.where(lane_lo, dyt_b, zero), jnp.where(lane_lo, zero, dyt_b)], axis=0)
                dxs = _dot(jnp.concatenate(mts, axis=0), dyt_b)
                dxdt = jnp.where(lane_lo, dxs[:L], dxs[L:])
                dmcat = _dot_nt(dystack, xdt_b[:, sl])
                for idx, h in enumerate((2 * j, 2 * j + 1)):
                    dm = dmcat[L * idx:L * idx + L]
                    dcb = dcb + dm * decs[idx]
                    dseg = dm * cb * decs[idx]
                    dacum_col = dacum_col + jnp.sum(dseg, axis=1, keepdims=True) * (lane_h == h).astype(F32)
                    dacum_row = dacum_row - (sub_h == h).astype(F32) * jnp.sum(dseg, axis=0, keepdims=True)
                gmat = _dot(cg_b, s0_b[:, sls])
                yoff_t.append(edy[:, sl] * gmat)
                qm = _dot(bg_b, ds_b[:, sls])
                dxdt_t.append(dxdt + qm * q["w_x"][:, sl])
                u_t.append(qm * xdtw[:, sl])
                r_t.append(ds[:, sls] * s0[:, sls] * q["elast_x"][:, sl])
                dstate[g, :, sls] = ds[:, sls] * q["elast_x"][:, sl] + _dot(cgT_b, edy_b[:, sl])
            gsl = slice(512 * g, 512 * g + 512)
            dcb_b = dcb.astype(BF16)
            dcg = _dot(dcb_b, bg_b) + _dot_nt(edy_b[:, gsl], s0_b)
            dbg = _dot(dcb.T.astype(BF16), cg_b) + _dot_nt(xdtw_b[:, gsl], ds_b)
            dxc_ref[:, SSD_D_INNER + 128 * g:SSD_D_INNER + 128 * g + 128] = dbg
            dxc_ref[:, SSD_D_INNER + 512 + 128 * g:SSD_D_INNER + 512 + 128 * g + 128] = dcg
        et = et_ref[...]
        dxdt_all = jnp.concatenate(dxdt_t, axis=1)
        yoff = jnp.concatenate(yoff_t, axis=1)
        uu = jnp.concatenate(u_t, axis=1)
        rr = jnp.concatenate(r_t, axis=1)
        dacum_col = dacum_col + _dot_01(yoff - uu, et)
        dlast = jnp.sum(_dot_01(uu + rr, et), axis=0, keepdims=True)
        row_lh = lax.broadcasted_iota(jnp.int32, (L, H), 0)
        dacum_col = dacum_col + jnp.where(row_lh == L - 1, dlast, 0.0)
        d_dta_col = jnp.dot(q["triu"], dacum_col, precision=HI, preferred_element_type=F32)
        d_dta_row = jnp.dot(dacum_row, q["tril"], precision=HI, preferred_element_type=F32)
        ddt_col = d_dta_col * q["a"] + _dot_01(dxdt_all * x, et)
        ddt_row = d_dta_row * q["aT"]
        ddtr_col = ddt_col * jax.nn.sigmoid(q["pre"])
        ddtr_row = ddt_row * jax.nn.sigmoid(q["preT"])
        ddtc_ref[...] = ddtr_col
        ddtr_ref[...] = ddtr_row
        dac_ref[...] += jnp.sum(d_dta_col * q["dt"], axis=0, keepdims=True)
        dar_ref[...] += jnp.sum(d_dta_row * q["dtT"], axis=1, keepdims=True)
        dbc_ref[...] += jnp.sum(ddtr_col, axis=0, keepdims=True)
        dbr_ref[...] += jnp.sum(ddtr_row, axis=1, keepdims=True)
        ddsk_ref[...] += jnp.sum(dyv * x, axis=0, keepdims=True)
        dxc_ref[:, 0:SSD_D_INNER] = dxdt_all * q["dt_x"] + dyv * dsk_ref[...]

    rv = lambda c: nc - 1 - c
    in_specs = _ssd_specs(nc, True) + [
        pl.BlockSpec((SSD_D_INNER, H), lambda c: (0, 0)),
        pl.BlockSpec((1, SSD_GROUPS, SSD_STATE, 512), lambda c: (rv(c), 0, 0, 0)),
        pl.BlockSpec((L, SSD_D_INNER), lambda c: (rv(c), 0)),
    ]
    vec_c = pl.BlockSpec((1, H), lambda c: (0, 0))
    vec_r = pl.BlockSpec((H, 1), lambda c: (0, 0))
    return pl.pallas_call(
        body, name=name, grid=(nc,),
        in_specs=in_specs,
        out_specs=[pl.BlockSpec((L, SSD_CONV_DIM), lambda c: (rv(c), 0)),
                   pl.BlockSpec((L, H), lambda c: (rv(c), 0)),
                   pl.BlockSpec((H, L), lambda c: (0, rv(c))),
                   vec_c, vec_r, vec_c, vec_r,
                   pl.BlockSpec((1, SSD_D_INNER), lambda c: (0, 0))],
        out_shape=[jax.ShapeDtypeStruct((s, SSD_CONV_DIM), F32),
                   jax.ShapeDtypeStruct((s, H), F32), jax.ShapeDtypeStruct((H, s), F32),
                   jax.ShapeDtypeStruct((1, H), F32), jax.ShapeDtypeStruct((H, 1), F32),
                   jax.ShapeDtypeStruct((1, H), F32), jax.ShapeDtypeStruct((H, 1), F32),
                   jax.ShapeDtypeStruct((1, SSD_D_INNER), F32)],
        scratch_shapes=[pltpu.VMEM((SSD_GROUPS, SSD_STATE, 512), F32)],
        compiler_params=_cp("arbitrary"),
    )(xc, xc, xc, dtr, dtrT, dtb, dtbT, alog, alogT, dskx, expand, expandT, states, dy)


QK_PAD = 256
MLA_TS = 256


def _rope_tables4(pos):
    inv = 1.0 / (ROPE_THETA ** (jnp.arange(0, MLA_ROPE, 2, dtype=F32) / MLA_ROPE))
    ang = pos.astype(F32)[:, None] * inv
    c, s = jnp.cos(ang), jnp.sin(ang)
    return jnp.tile(c, (1, 4)), jnp.concatenate([-s, s, -s, s], axis=1)


def _mla_gains(qg, kg):
    z = jnp.zeros((LANE - MLA_ROPE,), F32)
    return (qg[:MLA_NOPE][None], jnp.concatenate([qg[MLA_NOPE:], z])[None],
            kg[:MLA_NOPE][None], jnp.concatenate([kg[MLA_NOPE:], z])[None])


def _rope_swap(t, first):
    return jnp.where(first, pltpu.roll(t, 96, 1), pltpu.roll(t, 32, 1))


def _mla_prep_specs(ts):
    row = lambda w, c=0: pl.BlockSpec((ts, w), lambda i: (i, c))
    vec = pl.BlockSpec((1, LANE), lambda i: (0, 0))
    return [row(MLA_HEADS * MLA_QK), row(2 * MLA_HEADS * MLA_NOPE), row(LANE, OFF_KRDT // LANE), row(LANE), row(LANE),
            vec, vec, vec, vec]


def _mla_prep_fwd(qraw, kvraw, proj, cos4, sin4, gqn, gqr, gkn, gkr, *, name):
    s = qraw.shape[0]
    ts = _pick(s, MLA_TS, 8)

    def body(q_ref, kv_ref, kr_ref, cos_ref, sin_ref, gqn_ref, gqr_ref, gkn_ref, gkr_ref, qo_ref, ko_ref):
        lane = lax.broadcasted_iota(jnp.int32, (ts, LANE), 1)
        lo = lane < 64
        first = (lane % 64) < 32
        cos, sin = cos_ref[...], sin_ref[...]
        kr = jnp.where(lo, kr_ref[...], 0.0)
        ssq_kr = jnp.sum(kr * kr, axis=-1, keepdims=True)

        def head(xn, xr, ssq_r, gn, gr):
            rs = lax.rsqrt((jnp.sum(xn * xn, axis=-1, keepdims=True) + ssq_r) * (1.0 / MLA_QK) + EPS)
            yr = xr * rs * gr
            return xn * rs * gn, yr * cos + _rope_swap(yr, first) * sin

        for h in range(MLA_HEADS):
            tile = q_ref[:, MLA_HEADS * MLA_NOPE + LANE * (h // 2):MLA_HEADS * MLA_NOPE + LANE * (h // 2) + LANE]
            qr = jnp.where(lo, tile if h % 2 == 0 else pltpu.roll(tile, 64, 1), 0.0)
            on, orr = head(q_ref[:, LANE * h:LANE * h + LANE], qr, jnp.sum(qr * qr, axis=-1, keepdims=True),
                           gqn_ref[...], gqr_ref[...])
            qo_ref[h, :, 0:LANE] = (on * ATT_SCALE).astype(BF16)
            qo_ref[h, :, LANE:QK_PAD] = (orr * ATT_SCALE).astype(BF16)
            on, orr = head(kv_ref[:, LANE * h:LANE * h + LANE], kr, ssq_kr, gkn_ref[...], gkr_ref[...])
            ko_ref[h, :, 0:LANE] = on.astype(BF16)
            ko_ref[h, :, LANE:QK_PAD] = orr.astype(BF16)

    out = pl.BlockSpec((MLA_HEADS, ts, QK_PAD), lambda i: (0, i, 0))
    sh = jax.ShapeDtypeStruct((MLA_HEADS, s, QK_PAD), BF16)
    return pl.pallas_call(
        body, name=name, grid=(s // ts,),
        in_specs=_mla_prep_specs(ts), out_specs=[out, out], out_shape=[sh, sh],
        compiler_params=_cp("parallel"),
    )(qraw, kvraw, proj, cos4, sin4, gqn, gqr, gkn, gkr)


def _mla_prep_bwd(qraw, kvraw, proj, cos4, sin4, gqn, gqr, gkn, gkr, dq, dk, *, name):
    s = qraw.shape[0]
    ts = _pick(s, MLA_TS, 8)

    def body(q_ref, kv_ref, kr_ref, cos_ref, sin_ref, gqn_ref, gqr_ref, gkn_ref, gkr_ref, dq_ref, dk_ref,
             dqraw_ref, dkn_ref, dkr_ref, dgqn_ref, dgqr_ref, dgkn_ref, dgkr_ref):
        i = pl.program_id(0)

        @pl.when(i == 0)
        def _():
            for r in (dgqn_ref, dgqr_ref, dgkn_ref, dgkr_ref):
                r[...] = jnp.zeros_like(r)

        lane = lax.broadcasted_iota(jnp.int32, (ts, LANE), 1)
        lo = lane < 64
        first = (lane % 64) < 32
        cos, sin = cos_ref[...], sin_ref[...]
        kr = jnp.where(lo, kr_ref[...], 0.0)
        ssq_kr = jnp.sum(kr * kr, axis=-1, keepdims=True)

        def head(xn, xr, ssq_r, gn, gr, don, dor):
            rs = lax.rsqrt((jnp.sum(xn * xn, axis=-1, keepdims=True) + ssq_r) * (1.0 / MLA_QK) + EPS)
            xhn, xhr = xn * rs, xr * rs
            dor = jnp.where(lo, dor, 0.0)
            dyr = dor * cos + _rope_swap(dor * sin, first)
            dxn, dxr = don * gn, dyr * gr
            mm = (jnp.sum(dxn * xhn, axis=-1, keepdims=True) + jnp.sum(dxr * xhr, axis=-1, keepdims=True)) * (1.0 / MLA_QK)
            return (rs * (dxn - xhn * mm), rs * (dxr - xhr * mm),
                    jnp.sum(don * xhn, axis=0, keepdims=True), jnp.sum(dyr * xhr, axis=0, keepdims=True))

        dkr_acc = jnp.zeros((ts, LANE), F32)
        prev = None
        for h in range(MLA_HEADS):
            c0 = MLA_HEADS * MLA_NOPE + LANE * (h // 2)
            tile = q_ref[:, c0:c0 + LANE]
            qr = jnp.where(lo, tile if h % 2 == 0 else pltpu.roll(tile, 64, 1), 0.0)
            dn, dr, gn_p, gr_p = head(q_ref[:, LANE * h:LANE * h + LANE], qr, jnp.sum(qr * qr, axis=-1, keepdims=True),
                                      gqn_ref[...], gqr_ref[...], dq_ref[h, :, 0:LANE], dq_ref[h, :, LANE:QK_PAD])
            dqraw_ref[:, LANE * h:LANE * h + LANE] = dn.astype(dqraw_ref.dtype)
            dgqn_ref[...] += gn_p
            dgqr_ref[...] += gr_p
            if h % 2 == 0:
                prev = dr
            else:
                dqraw_ref[:, c0:c0 + LANE] = (prev + pltpu.roll(dr, 64, 1)).astype(dqraw_ref.dtype)
            dn, dr, gn_p, gr_p = head(kv_ref[:, LANE * h:LANE * h + LANE], kr, ssq_kr, gkn_ref[...], gkr_ref[...],
                                      dk_ref[h, :, 0:LANE], dk_ref[h, :, LANE:QK_PAD])
            dkn_ref[:, LANE * h:LANE * h + LANE] = dn.astype(dkn_ref.dtype)
            dkr_acc = dkr_acc + dr
            dgkn_ref[...] += gn_p
            dgkr_ref[...] += gr_p
        dkr_ref[...] = dkr_acc

    row = lambda w: pl.BlockSpec((ts, w), lambda i: (i, 0))
    vec = pl.BlockSpec((1, LANE), lambda i: (0, 0))
    dspec = pl.BlockSpec((MLA_HEADS, ts, QK_PAD), lambda i: (0, i, 0))
    vsh = jax.ShapeDtypeStruct((1, LANE), F32)
    return pl.pallas_call(
        body, name=name, grid=(s // ts,),
        in_specs=_mla_prep_specs(ts) + [dspec, dspec],
        out_specs=[row(MLA_HEADS * MLA_QK), row(MLA_HEADS * MLA_NOPE), row(LANE), vec, vec, vec, vec],
        out_shape=[jax.ShapeDtypeStruct((s, MLA_HEADS * MLA_QK), BF16), jax.ShapeDtypeStruct((s, MLA_HEADS * MLA_NOPE), BF16),
                   jax.ShapeDtypeStruct((s, LANE), F32), vsh, vsh, vsh, vsh],
        compiler_params=_cp("arbitrary"),
    )(qraw, kvraw, proj, cos4, sin4, gqn, gqr, gkn, gkr, dq, dk)


ATT_T = 1024
ATT_T_FWD = 2048
ATT_SCALE = MLA_QK ** -0.5


def _attn_fwd(q, k, kvraw, *, name):
    nh, s, _ = q.shape
    t = _pick(s, ATT_T_FWD, LANE)
    nb = s // t

    def body(q_ref, k_ref, v_ref, o_ref, lse_ref, m_ref, l_ref, acc_ref):
        i, j = pl.program_id(1), pl.program_id(2)

        @pl.when(j == 0)
        def _():
            m_ref[...] = jnp.full_like(m_ref, -jnp.inf)
            l_ref[...] = jnp.zeros_like(l_ref)
            acc_ref[...] = jnp.zeros_like(acc_ref)

        def step(diagonal):
            sc = _dot_nt(q_ref[0], k_ref[0])
            if diagonal:
                ri = lax.broadcasted_iota(jnp.int32, (t, t), 0)
                cj = lax.broadcasted_iota(jnp.int32, (t, t), 1)
                sc = jnp.where(ri >= cj, sc, -jnp.inf)
            m_new = jnp.maximum(m_ref[...], jnp.max(sc, axis=-1, keepdims=True))
            alpha = jnp.exp(m_ref[...] - m_new)
            p = jnp.exp(sc - m_new)
            l_ref[...] = alpha * l_ref[...] + jnp.sum(p, axis=-1, keepdims=True)
            acc_ref[...] = alpha * acc_ref[...] + _dot(p.astype(BF16), v_ref[...].astype(BF16))
            m_ref[...] = m_new

        @pl.when(j < i)
        def _():
            step(False)

        @pl.when(j == i)
        def _():
            step(True)
            o_ref[...] = acc_ref[...] / l_ref[...]
            lse_ref[0] = m_ref[...] + jnp.log(l_ref[...])

    return pl.pallas_call(
        body, name=name, grid=(nh, nb, nb),
        in_specs=[pl.BlockSpec((1, t, QK_PAD), lambda h, i, j: (h, i, 0)),
                  pl.BlockSpec((1, t, QK_PAD), lambda h, i, j: (h, jnp.minimum(j, i), 0)),
                  pl.BlockSpec((t, MLA_V), lambda h, i, j: (jnp.minimum(j, i), nh + h))],
        out_specs=[pl.BlockSpec((t, MLA_V), lambda h, i, j: (i, h)),
                   pl.BlockSpec((1, t, 1), lambda h, i, j: (h, i, 0))],
        out_shape=[jax.ShapeDtypeStruct((s, nh * MLA_V), F32), jax.ShapeDtypeStruct((nh, s, 1), F32)],
        scratch_shapes=[pltpu.VMEM((t, 1), F32), pltpu.VMEM((t, 1), F32), pltpu.VMEM((t, MLA_V), F32)],
        compiler_params=_cp("parallel", "parallel", "arbitrary"),
    )(q, k, kvraw)


def _attn_bwd(q, k, kvraw, o, lse, do, *, name):
    nh, s, _ = q.shape
    t = _pick(s, ATT_T, LANE)
    nb = s // t

    def body(q_ref, k_ref, v_ref, o_ref, lse_ref, do_ref, dq_ref, dk_ref, dv_ref, dk_acc, dv_acc):
        j, i = pl.program_id(1), pl.program_id(2)

        @pl.when(i == 0)
        def _():
            dk_acc[...] = jnp.zeros_like(dk_acc)
            dv_acc[...] = jnp.zeros_like(dv_acc)

        def step(diagonal):
            qv, kv = q_ref[0], k_ref[0]
            sc = _dot_nt(qv, kv)
            if diagonal:
                ri = lax.broadcasted_iota(jnp.int32, (t, t), 0)
                cj = lax.broadcasted_iota(jnp.int32, (t, t), 1)
                sc = jnp.where(ri >= cj, sc, -jnp.inf)
            p = jnp.exp(sc - lse_ref[0])
            dov = do_ref[...]
            delta = jnp.sum(dov * o_ref[...], axis=-1, keepdims=True)
            do_b = dov.astype(BF16)
            dv_acc[...] += _dot_tn(p.astype(BF16), do_b)
            dp = _dot_nt(do_b, v_ref[...].astype(BF16))
            ds_b = (p * (dp - delta)).astype(BF16)
            dk_acc[...] += _dot_tn(ds_b, qv)
            dq_part = _dot(ds_b, kv) * ATT_SCALE
            rows = pl.ds(pl.multiple_of(i * t, t), t)

            @pl.when(j == 0)
            def _():
                dq_ref[0, rows, :] = dq_part

            @pl.when(j > 0)
            def _():
                dq_ref[0, rows, :] += dq_part

        @pl.when(i > j)
        def _():
            step(False)

        @pl.when(i == j)
        def _():
            step(True)

        @pl.when(i == nb - 1)
        def _():
            dk_ref[0] = dk_acc[...]
            dv_ref[...] = dv_acc[...].astype(dv_ref.dtype)

    qi = lambda h, j, i: jnp.maximum(i, j)
    return pl.pallas_call(
        body, name=name, grid=(nh, nb, nb),
        in_specs=[pl.BlockSpec((1, t, QK_PAD), lambda h, j, i: (h, qi(h, j, i), 0)),
                  pl.BlockSpec((1, t, QK_PAD), lambda h, j, i: (h, j, 0)),
                  pl.BlockSpec((t, MLA_V), lambda h, j, i: (j, nh + h)),
                  pl.BlockSpec((t, MLA_V), lambda h, j, i: (qi(h, j, i), h)),
                  pl.BlockSpec((1, t, 1), lambda h, j, i: (h, qi(h, j, i), 0)),
                  pl.BlockSpec((t, MLA_V), lambda h, j, i: (qi(h, j, i), h))],
        out_specs=[pl.BlockSpec((1, s, QK_PAD), lambda h, j, i: (h, 0, 0)),
                   pl.BlockSpec((1, t, QK_PAD), lambda h, j, i: (h, j, 0)),
                   pl.BlockSpec((t, MLA_V), lambda h, j, i: (j, h))],
        out_shape=[jax.ShapeDtypeStruct((nh, s, QK_PAD), F32), jax.ShapeDtypeStruct((nh, s, QK_PAD), F32),
                   jax.ShapeDtypeStruct((s, nh * MLA_V), BF16)],
        scratch_shapes=[pltpu.VMEM((t, QK_PAD), F32), pltpu.VMEM((t, MLA_V), F32)],
        compiler_params=_cp("parallel", "arbitrary", "arbitrary"),
    )(q, k, kvraw, o, lse, do)


def _ffn_fwd(h, w, tag):
    n = _rms_fwd(h, w["ln"], name=tag + "_norm")
    act, gate, up = _ffn_up(n, w["w13"], name=tag + "_up")
    out = _matmul(act, w["w2"], "nn", name=tag + "_down", scale=0.5, res=h)
    return out, (h, n, gate, up, act)


def _ffn_bwd(dout, saved, w, tag):
    h, n, gate, up, act = saved
    dact = _matmul(dout, w["w2"], "nt", name=tag + "_down_dx", scale=0.5, out_dtype=BF16)
    dw2 = _matmul(act, dout, "tn", name=tag + "_down_dw", scale=0.5)
    dgu = _swiglu_bwd(gate, up, dact, name=tag + "_act_bwd")
    dw13 = _matmul(n, dgu, "tn", name=tag + "_up_dw")
    dn = _matmul(dgu, w["w13"], "nt", name=tag + "_up_dx")
    dh, dln = _rms_bwd(h, w["ln"], dn, name=tag + "_norm_bwd", res=dout)
    return dh, dict(ln=dln, w13=dw13, w2=dw2)


def _mixer_fwd(h, w, rope, tag):
    cos4, sin4 = rope
    u = _rms_fwd(h, w["ln_mix"], name=tag + "_norm")
    proj = _matmul(u, w["w_in"], "nn", name=tag + "_in")
    xc = _conv_fwd(proj, w["conv_w"], w["conv_b"], name=tag + "_conv")
    dtr = proj[:, OFF_KRDT + MLA_ROPE:OFF_KRDT + MLA_ROPE + SSD_HEADS]
    dtrT = dtr.T
    y, states = _ssd_fwd(xc, dtr, dtrT, *w["ssd_aux"], name=tag + "_ssd")
    yn = _gated_rms_fwd(y, proj, w["ssd_norm"], name=tag + "_ssd_norm")
    y_ssd = _matmul(yn, w["w_ssd_out"], "nn", name=tag + "_ssd_out")
    cqn = _rms_fwd(proj, w["q_lora_norm"], name=tag + "_q_lora_norm", col=OFF_CQ // MLA_Q_LORA, width=MLA_Q_LORA)
    qraw = _matmul(cqn, w["w_uq"], "nn", name=tag + "_uq")
    ckvn = _rms_fwd(proj, w["kv_lora_norm"], name=tag + "_kv_lora_norm", col=OFF_CKV // MLA_KV_LORA, width=MLA_KV_LORA)
    kvraw = _matmul(ckvn, w["w_ukv"], "nn", name=tag + "_ukv")
    qf, kf = _mla_prep_fwd(qraw, kvraw, proj, cos4, sin4, *w["qk_gains"], name=tag + "_qk_prep")
    o, lse = _attn_fwd(qf, kf, kvraw, name=tag + "_attn")
    y_mla = _matmul(o, w["w_mla_out"], "nn", name=tag + "_mla_out")
    merged = _merge_fwd(proj, y_ssd, y_mla, name=tag + "_merge")
    out = _matmul(merged, w["w_o"], "nn", name=tag + "_o", res=h)
    saved = dict(h=h, u=u, proj=proj, xc=xc, dtr=dtr, dtrT=dtrT, states=states, y=y, yn=yn, y_ssd=y_ssd, cqn=cqn,
                 qraw=qraw, ckvn=ckvn, kvraw=kvraw, qf=qf, kf=kf, o=o, lse=lse, y_mla=y_mla, merged=merged)
    return out, saved


def _mixer_bwd(dout, s, w, rope, tag):
    cos4, sin4 = rope
    g = {}
    proj = s["proj"]
    dmerged = _matmul(dout, w["w_o"], "nt", name=tag + "_o_dx")
    g["w_o"] = _matmul(s["merged"], dout, "tn", name=tag + "_o_dw")
    dgates, dy_ssd, dy_mla = _merge_bwd(proj, s["y_ssd"], s["y_mla"], dmerged, name=tag + "_merge_bwd")
    do = _matmul(dy_mla, w["w_mla_out"], "nt", name=tag + "_mla_out_dx")
    g["w_mla_out"] = _matmul(s["o"], dy_mla, "tn", name=tag + "_mla_out_dw")
    dqf, dkf, dv = _attn_bwd(s["qf"], s["kf"], s["kvraw"], s["o"], s["lse"], do, name=tag + "_attn_bwd")
    dqraw, dkn, dkrt, dgqn, dgqr, dgkn, dgkr = _mla_prep_bwd(
        s["qraw"], s["kvraw"], proj, cos4, sin4, *w["qk_gains"], dqf, dkf, name=tag + "_qk_prep_bwd")
    g["q_norm"] = jnp.concatenate([dgqn[0], dgqr[0, :MLA_ROPE]])
    g["k_norm"] = jnp.concatenate([dgkn[0], dgkr[0, :MLA_ROPE]])
    dkvraw = jnp.concatenate([dkn, dv], axis=1)
    dcqn = _matmul(dqraw, w["w_uq"], "nt", name=tag + "_uq_dx")
    g["w_uq"] = _matmul(s["cqn"], dqraw, "tn", name=tag + "_uq_dw")
    dckvn = _matmul(dkvraw, w["w_ukv"], "nt", name=tag + "_ukv_dx")
    g["w_ukv"] = _matmul(s["ckvn"], dkvraw, "tn", name=tag + "_ukv_dw")
    dcq, g["q_lora_norm"] = _rms_bwd(proj, w["q_lora_norm"], dcqn, name=tag + "_q_lora_norm_bwd",
                                     col=OFF_CQ // MLA_Q_LORA, width=MLA_Q_LORA, out_dtype=BF16)
    dckv, g["kv_lora_norm"] = _rms_bwd(proj, w["kv_lora_norm"], dckvn, name=tag + "_kv_lora_norm_bwd",
                                       col=OFF_CKV // MLA_KV_LORA, width=MLA_KV_LORA, out_dtype=BF16)
    dyn = _matmul(dy_ssd, w["w_ssd_out"], "nt", name=tag + "_ssd_out_dx")
    g["w_ssd_out"] = _matmul(s["yn"], dy_ssd, "tn", name=tag + "_ssd_out_dw")
    dy, dz, g["ssd_norm"] = _gated_rms_bwd(s["y"], proj, w["ssd_norm"], dyn, name=tag + "_ssd_norm_bwd")
    aux = w["ssd_aux"]
    dxc, ddt_c, ddt_r, dbias_c, dbias_r, da_c, da_r, ddsk = _ssd_bwd(
        s["xc"], s["dtr"], s["dtrT"], *aux, aux[-1].T, s["states"], dy, name=tag + "_ssd_bwd")
    g["dt_bias"] = dbias_c[0] + dbias_r[:, 0]
    g["a_log"] = (da_c[0] + da_r[:, 0]) * (-jnp.exp(aux[2][0]))
    g["d_skip"] = jnp.sum(ddsk.reshape(SSD_HEADS, SSD_HEAD_DIM), axis=1)
    dpre, g["conv_w"], g["conv_b"] = _conv_bwd_pre(proj, w["conv_w"], w["conv_b"], dxc, name=tag + "_conv_bwd_pre")
    dxbc = _conv_bwd_x(dpre, w["conv_w"], name=tag + "_conv_bwd_x")
    ddtr = ddt_c + ddt_r.T
    dkrdt = jnp.concatenate([dkrt[:, :MLA_ROPE], ddtr, jnp.zeros((ddtr.shape[0], LANE - MLA_ROPE - SSD_HEADS), F32)], axis=1)
    dproj = jnp.concatenate([dz, dxbc, dgates, dcq, dckv, dkrdt.astype(BF16)], axis=1)
    du = _matmul(dproj, w["w_in"], "nt", name=tag + "_in_dx")
    g["w_in"] = _matmul(s["u"], dproj, "tn", name=tag + "_in_dw")
    dh, g["ln_mix"] = _rms_bwd(s["h"], w["ln_mix"], du, name=tag + "_norm_bwd", res=dout)
    return dh, g


W_NAMES = ["ln_ffn1", "ffn1_w13", "ffn1_w2", "ln_mix", "w_in", "conv_w", "conv_b", "dt_bias", "a_log", "d_skip",
           "ssd_norm", "w_ssd_out", "q_lora_norm", "w_uq", "kv_lora_norm", "w_ukv", "q_norm", "k_norm", "w_mla_out",
           "w_o", "ln_ffn2", "ffn2_w13", "ffn2_w2"]
SHARD_AXIS = {"ffn1_w13": 2, "ffn1_w2": 1, "w_in": 2, "conv_w": 2, "w_ssd_out": 1, "w_uq": 2, "w_ukv": 2,
              "w_mla_out": 1, "w_o": 1, "ffn2_w13": 2, "ffn2_w2": 1}
SHARDED = [n for n in W_NAMES if n in SHARD_AXIS and n != "conv_w"] + ["conv_w"]
REPLICATED = [n for n in W_NAMES if n not in SHARD_AXIS]
N_CHIPS = 4
N_DEV = 8
PACK_COLS = 1024
IN_SPLIT = (2048, 3072, 32, 512, 256, 64, 2048)


def _pack_mats(arrs, rows, dtype):
    mats = [a.astype(dtype).reshape(-1, PACK_COLS) for a in arrs]
    used = sum(m.shape[0] for m in mats)
    return mats + [jnp.zeros((rows - used, PACK_COLS), dtype)]


def _pack(arrs, rows, dtype):
    return jnp.concatenate(_pack_mats(arrs, rows, dtype), axis=0)


def _unpack(packed, shapes):
    out, at = [], 0
    for sh in shapes:
        r = math.prod(sh) // PACK_COLS
        out.append(packed[at:at + r].reshape(sh))
        at += r
    return out


def _unpack_flat(flat, shapes):
    out, at = [], 0
    for sh in shapes:
        n = math.prod(sh)
        out.append(flat[at:at + n].reshape(sh))
        at += n
    return out


def _pack_rows(shapes):
    n = sum(math.prod(sh) for sh in shapes)
    return -(-n // (PACK_COLS * 1024)) * 1024


def _in_perm(w_in):
    z, xbc, dt, cq, ckv, kr, gates = jnp.split(w_in, list(np_cumsum(IN_SPLIT))[:-1], axis=1)
    return jnp.concatenate([z, xbc, gates, cq, ckv, kr, dt, jnp.zeros((w_in.shape[0], PROJ_W - sum(IN_SPLIT)), w_in.dtype)], axis=1)


def _in_unperm(g):
    z, xbc, gates, cq, ckv = (g[:, OFF_Z:OFF_XBC], g[:, OFF_XBC:OFF_GATES], g[:, OFF_GATES:OFF_CQ], g[:, OFF_CQ:OFF_CKV],
                              g[:, OFF_CKV:OFF_KRDT])
    kr = g[:, OFF_KRDT:OFF_KRDT + MLA_ROPE]
    dt = g[:, OFF_KRDT + MLA_ROPE:OFF_KRDT + MLA_ROPE + SSD_HEADS]
    return jnp.concatenate([z, xbc, dt, cq, ckv, kr, gates], axis=1)


def np_cumsum(sizes):
    out, t = [], 0
    for s in sizes:
        t += s
        out.append(t)
    return out


def _head_perm(w, first):
    r = w.shape[0]
    w3 = w.reshape(r, MLA_HEADS, -1)
    return jnp.concatenate([w3[:, :, :first].reshape(r, -1), w3[:, :, first:].reshape(r, -1)], axis=1)


def _head_unperm(g, first):
    r = g.shape[0]
    rest = g.shape[1] // MLA_HEADS - first
    a = g[:, :MLA_HEADS * first].reshape(r, MLA_HEADS, first)
    b = g[:, MLA_HEADS * first:].reshape(r, MLA_HEADS, rest)
    return jnp.concatenate([a, b], axis=2).reshape(r, -1)


def _layer_weights(full, l):
    row = lambda n: full[n][l][None].astype(F32)
    expand = jnp.repeat(jnp.eye(SSD_HEADS, dtype=F32), SSD_HEAD_DIM, axis=1)
    dtb, al, dsk = full["dt_bias"][l], full["a_log"][l], full["d_skip"][l]
    mixer = dict(
        ln_mix=row("ln_mix"), w_in=_in_perm(full["w_in"][l]), conv_w=full["conv_w"][l], conv_b=row("conv_b"),
        ssd_aux=(dtb[None], dtb[:, None], al[None], al[:, None], jnp.repeat(dsk, SSD_HEAD_DIM)[None], expand),
        ssd_norm=row("ssd_norm"), w_ssd_out=full["w_ssd_out"][l],
        q_lora_norm=row("q_lora_norm"), w_uq=_head_perm(full["w_uq"][l], MLA_NOPE),
        kv_lora_norm=row("kv_lora_norm"), w_ukv=_head_perm(full["w_ukv"][l], MLA_NOPE),
        qk_gains=_mla_gains(full["q_norm"][l], full["k_norm"][l]),
        w_mla_out=full["w_mla_out"][l], w_o=full["w_o"][l])
    ffn1 = dict(ln=row("ln_ffn1"), w13=full["ffn1_w13"][l], w2=full["ffn1_w2"][l])
    ffn2 = dict(ln=row("ln_ffn2"), w13=full["ffn2_w13"][l], w2=full["ffn2_w2"][l])
    return ffn1, mixer, ffn2


def _layer_grads(g1, gm, g2):
    return {
        "ln_ffn1": g1["ln"][0], "ffn1_w13": g1["w13"], "ffn1_w2": g1["w2"],
        "ln_mix": gm["ln_mix"][0], "w_in": _in_unperm(gm["w_in"]), "conv_w": gm["conv_w"], "conv_b": gm["conv_b"][0],
        "dt_bias": gm["dt_bias"], "a_log": gm["a_log"], "d_skip": gm["d_skip"], "ssd_norm": gm["ssd_norm"][0],
        "w_ssd_out": gm["w_ssd_out"], "q_lora_norm": gm["q_lora_norm"][0], "w_uq": _head_unperm(gm["w_uq"], MLA_NOPE),
        "kv_lora_norm": gm["kv_lora_norm"][0], "w_ukv": _head_unperm(gm["w_ukv"], MLA_NOPE),
        "q_norm": gm["q_norm"], "k_norm": gm["k_norm"], "w_mla_out": gm["w_mla_out"], "w_o": gm["w_o"],
        "ln_ffn2": g2["ln"][0], "ffn2_w13": g2["w13"], "ffn2_w2": g2["w2"],
    }


def _local_step(x, positions, loss_target, full):
    rope = _rope_tables4(positions)
    lw = [_layer_weights(full, l) for l in range(DEPTH)]
    h = x
    saved = []
    for l in range(DEPTH):
        f1, mx, f2 = lw[l]
        h, s1 = _ffn_fwd(h, f1, f"l{l}_ffn1")
        h, sm = _mixer_fwd(h, mx, rope, f"l{l}_mix")
        h, s2 = _ffn_fwd(h, f2, f"l{l}_ffn2")
        saved.append((s1, sm, s2))
    loss_part, dh = _loss_fwd_bwd(h, loss_target, name="loss")
    grads = [None] * DEPTH
    for l in reversed(range(DEPTH)):
        f1, mx, f2 = lw[l]
        s1, sm, s2 = saved[l]
        dh, g2 = _ffn_bwd(dh, s2, f2, f"l{l}_ffn2")
        dh, gm = _mixer_bwd(dh, sm, mx, rope, f"l{l}_mix")
        dh, g1 = _ffn_bwd(dh, s1, f1, f"l{l}_ffn1")
        grads[l] = _layer_grads(g1, gm, g2)
    full_grads = {n: jnp.stack([grads[l][n] for l in range(DEPTH)]) for n in W_NAMES}
    return loss_part, dh, full_grads


MESH = pl.DeviceIdType.MESH
ANY = pl.BlockSpec(memory_space=pl.ANY)


def _place():
    return lax.axis_index("x"), lax.axis_index("y"), lax.axis_index("c")


def _other_chips(x, y):
    return [(1 - x, y), (x, 1 - y), (1 - x, 1 - y)]


def _remote(src, dst, send_sems, recv_sems, k, to):
    return pltpu.make_async_remote_copy(src_ref=src, dst_ref=dst, send_sem=send_sems.at[k], recv_sem=recv_sems.at[k],
                                        device_id=to, device_id_type=MESH)


N_PARTS = 8


def _parts(rows):
    size = rows // N_PARTS
    assert size * N_PARTS == rows and size % 16 == 0, rows
    return [(p * size, size) for p in range(N_PARTS)]


def _rows(ref, lead, base, start, size):
    return ref.at[(*lead, pl.ds(pl.multiple_of(base + start, 16), size), slice(None))]


def _my_chip():
    return 2 * lax.axis_index("x") + lax.axis_index("y")


def _own_slot(packed, *, name):
    r, ncol = packed.shape
    tr = _pick(r, 512, 16)

    def body(x_ref, o_ref):
        o_ref[...] = x_ref[...]

    return pl.pallas_call(
        body, name=name, grid=(r // tr,),
        in_specs=[pl.BlockSpec((tr, ncol), lambda i: (i, 0))],
        out_specs=pl.BlockSpec((None, tr, ncol), lambda i: (_my_chip(), i, 0)),
        out_shape=jax.ShapeDtypeStruct((N_CHIPS, r, ncol), packed.dtype),
        compiler_params=_cp("arbitrary"),
    )(packed)


def _gather_shards(packed, slots, *, name):
    r, ncol = packed.shape
    hr = r // 2
    parts = _parts(hr)

    def body(x_ref, slots_ref, out_ref, send_sems, recv_sems):
        del slots_ref
        x, y, c = _place()
        chips = _other_chips(x, y)
        me = 2 * x + y

        def half(chip, cc):
            return _rows(out_ref, (2 * chip[0] + chip[1],), cc * hr, 0, hr)

        for j, chip in enumerate(chips):
            for st, sz in parts:
                _remote(_rows(x_ref, (), c * hr, st, sz), _rows(out_ref, (me,), c * hr, st, sz), send_sems, recv_sems, j,
                        (*chip, c)).start()
        for j, chip in enumerate(chips):
            _remote(half(chip, c), half(chip, c), send_sems, recv_sems, j, (x, y, c)).wait_recv()
            slot = 2 * chip[0] + chip[1]
            for st, sz in parts:
                _remote(_rows(out_ref, (slot,), c * hr, st, sz), _rows(out_ref, (slot,), c * hr, st, sz), send_sems,
                        recv_sems, 3 + j, (x, y, 1 - c)).start()
        for j, chip in enumerate(chips):
            _remote(half(chip, 1 - c), half(chip, 1 - c), send_sems, recv_sems, 3 + j, (x, y, c)).wait_recv()
        for k in range(6):
            _remote(half((x, y), c), half((x, y), c), send_sems, recv_sems, k, (x, y, c)).wait_send()

    return pl.pallas_call(
        body, name=name,
        out_shape=jax.ShapeDtypeStruct((N_CHIPS, r, ncol), packed.dtype),
        in_specs=[ANY, ANY], out_specs=ANY, input_output_aliases={1: 0},
        scratch_shapes=[pltpu.SemaphoreType.DMA((6,)), pltpu.SemaphoreType.DMA((6,))],
    )(packed, slots)


def _swap_halves(g, *, name):
    n, r, ncol = g.shape
    hr = r // 2
    parts = _parts(hr)

    def body(g_ref, got_ref, send_sems, recv_sems):
        x, y, c = _place()
        for s in range(n):
            for st, sz in parts:
                _remote(_rows(g_ref, (s,), (1 - c) * hr, st, sz), got_ref.at[s, pl.ds(st, sz), :], send_sems, recv_sems, 0,
                        (x, y, 1 - c)).start()
        _remote(got_ref, got_ref, send_sems, recv_sems, 0, (x, y, c)).wait()

    return pl.pallas_call(
        body, name=name, out_shape=jax.ShapeDtypeStruct((n, hr, ncol), g.dtype), in_specs=[ANY], out_specs=ANY,
        scratch_shapes=[pltpu.SemaphoreType.DMA((1,)), pltpu.SemaphoreType.DMA((1,))],
    )(g)


def _add_cores(g, got, *, name):
    n, r, ncol = g.shape
    hr = r // 2
    tr = _pick(hr, 512, 16)
    nb = hr // tr

    def body(a_ref, b_ref, o_ref):
        o_ref[...] = (a_ref[...].astype(F32) + b_ref[...].astype(F32)).astype(o_ref.dtype)

    blk = pl.BlockSpec((None, tr, ncol), lambda s, i: (s, i, 0))
    return pl.pallas_call(
        body, name=name, grid=(n, nb),
        in_specs=[pl.BlockSpec((None, tr, ncol), lambda s, i: (s, lax.axis_index("c") * nb + i, 0)), blk],
        out_specs=blk,
        out_shape=jax.ShapeDtypeStruct((n, hr, ncol), BF16),
        compiler_params=_cp("parallel", "parallel"),
    )(g, got)


def _scatter_to_chips(a, *, name):
    n, r, ncol = a.shape
    parts = _parts(r)

    def body(a_ref, got_ref, send_sems, recv_sems):
        x, y, c = _place()
        for st, sz in parts:
            for j, chip in enumerate(_other_chips(x, y)):
                _remote(a_ref.at[2 * chip[0] + chip[1], pl.ds(st, sz), :], got_ref.at[j, pl.ds(st, sz), :], send_sems,
                        recv_sems, j, (*chip, c)).start()
        for j in range(n - 1):
            _remote(got_ref.at[j], got_ref.at[j], send_sems, recv_sems, j, (x, y, c)).wait()

    return pl.pallas_call(
        body, name=name, out_shape=jax.ShapeDtypeStruct((n - 1, r, ncol), a.dtype), in_specs=[ANY], out_specs=ANY,
        scratch_shapes=[pltpu.SemaphoreType.DMA((3,)), pltpu.SemaphoreType.DMA((3,))],
    )(a)


def _add_chips(a, got, *, name):
    n, hr, ncol = a.shape
    tr = _pick(hr, 512, 16)
    nb = hr // tr

    def body(a_ref, g0_ref, g1_ref, g2_ref, o_ref):
        f = lambda ref: ref[...].astype(F32)
        o_ref[...] = ((f(a_ref) + f(g0_ref)) + f(g1_ref)) + f(g2_ref)

    other = lambda j: pl.BlockSpec((None, tr, ncol), lambda i: (j, i, 0))
    return pl.pallas_call(
        body, name=name, grid=(nb,),
        in_specs=[pl.BlockSpec((None, tr, ncol), lambda i: (_my_chip(), i, 0)), other(0), other(1), other(2)],
        out_specs=pl.BlockSpec((tr, ncol), lambda i: (lax.axis_index("c") * nb + i, 0)),
        out_shape=jax.ShapeDtypeStruct((2 * hr, ncol), F32),
        compiler_params=_cp("parallel"),
    )(a, got, got, got)


def _join_halves(buf, *, name):
    r, ncol = buf.shape
    hr = r // 2
    parts = _parts(hr)

    def body(b_ref, out_ref, send_sems, recv_sems):
        del b_ref
        x, y, c = _place()
        for st, sz in parts:
            _remote(_rows(out_ref, (), c * hr, st, sz), _rows(out_ref, (), c * hr, st, sz), send_sems, recv_sems, 0,
                    (x, y, 1 - c)).start()
        theirs = _rows(out_ref, (), (1 - c) * hr, 0, hr)
        _remote(theirs, theirs, send_sems, recv_sems, 0, (x, y, c)).wait()

    return pl.pallas_call(
        body, name=name, out_shape=jax.ShapeDtypeStruct((r, ncol), buf.dtype), in_specs=[ANY], out_specs=ANY,
        input_output_aliases={0: 0},
        scratch_shapes=[pltpu.SemaphoreType.DMA((1,)), pltpu.SemaphoreType.DMA((1,))],
    )(buf)


def _reduce_scatter(g, *, name):
    got = _swap_halves(g, name=name + "_swap")
    chip_sum = _add_cores(g, got, name=name + "_add_cores")
    others = _scatter_to_chips(chip_sum, name=name + "_scatter")
    return _join_halves(_add_chips(chip_sum, others, name=name + "_add_chips"), name=name + "_join")


def _all_gather_small(v, *, name):
    r, ncol = v.shape

    def body(x_ref, out_ref, send_sems, recv_sems, local_sem):
        x, y, c = _place()
        me, sibling = (x, y, c), (x, y, 1 - c)
        chips = _other_chips(x, y)

        def slot(p):
            return out_ref.at[4 * p[0] + 2 * p[1] + p[2]]

        mine = pltpu.make_async_copy(x_ref, slot(me), local_sem.at[0])
        mine.start()
        first = [_remote(x_ref, slot(me), send_sems, recv_sems, 0, sibling)]
        first += [_remote(x_ref, slot(me), send_sems, recv_sems, 1 + j, (*chip, c)) for j, chip in enumerate(chips)]
        for cp in first:
            cp.start()
        passed = [_remote(slot((*chip, c)), slot((*chip, c)), send_sems, recv_sems, 4 + j, sibling)
                  for j, chip in enumerate(chips)]
        for j, chip in enumerate(chips):
            _remote(slot((*chip, c)), slot((*chip, c)), send_sems, recv_sems, 1 + j, me).wait_recv()
            passed[j].start()
        _remote(slot(sibling), slot(sibling), send_sems, recv_sems, 0, me).wait_recv()
        for j, chip in enumerate(chips):
            _remote(slot((*chip, 1 - c)), slot((*chip, 1 - c)), send_sems, recv_sems, 4 + j, me).wait_recv()
        for cp in first + passed:
            cp.wait_send()
        mine.wait()

    vm = pl.BlockSpec(memory_space=pltpu.VMEM)
    return pl.pallas_call(
        body, name=name, out_shape=jax.ShapeDtypeStruct((N_DEV, r, ncol), v.dtype), in_specs=[vm], out_specs=vm,
        scratch_shapes=[pltpu.SemaphoreType.DMA((7,)), pltpu.SemaphoreType.DMA((7,)), pltpu.SemaphoreType.DMA((1,))],
    )(v)


def _sum_slots(g8, *, name):
    n, r, ncol = g8.shape

    def body(g_ref, o_ref):
        acc = g_ref[0]
        for k in range(1, n):
            acc = acc + g_ref[k]
        o_ref[...] = acc

    return pl.pallas_call(body, name=name, out_shape=jax.ShapeDtypeStruct((r, ncol), g8.dtype))(g8)


def _step(a):
    x = a["x"][0]
    s = x.shape[0]
    del s
    shard_shapes = [a[n].shape for n in SHARDED]
    rows = _pack_rows(shard_shapes)

    packed = _pack([a[n] for n in SHARDED], rows, BF16)
    gathered = _gather_shards(packed, _own_slot(packed, name="own_weights"), name="gather_weights")
    conv_rows = -(-math.prod(a["conv_w"].shape) // (LANE * 8)) * 8
    conv_all = _all_gather_small(
        jnp.pad(a["conv_w"].reshape(-1), (0, conv_rows * LANE - math.prod(a["conv_w"].shape))).reshape(conv_rows, LANE),
        name="gather_conv_w")
    per_chip = [dict(zip(SHARDED, _unpack(gathered[k], shard_shapes))) for k in range(N_CHIPS)]
    full = {n: jnp.concatenate([per_chip[k][n] for k in range(N_CHIPS)], axis=SHARD_AXIS[n]) for n in SHARDED}
    full["conv_w"] = jnp.concatenate(
        [conv_all[2 * k].reshape(-1)[:math.prod(a["conv_w"].shape)].reshape(a["conv_w"].shape) for k in range(N_CHIPS)],
        axis=SHARD_AXIS["conv_w"])
    for n in REPLICATED:
        full[n] = a[n]

    loss_part, grad_x, grads = _local_step(x, a["positions"][0], a["loss_target"][0], full)
    loss = lax.psum(jnp.sum(loss_part), ("x", "y", "c"))

    mats = []
    for k in range(N_CHIPS):
        parts = [jnp.split(grads[n], N_CHIPS, axis=SHARD_AXIS[n])[k] for n in SHARDED]
        mats += _pack_mats(parts, rows, BF16)
    g_slots = jnp.concatenate(mats, axis=0).reshape(N_CHIPS, rows, PACK_COLS)
    g_shard = _reduce_scatter(g_slots, name="reduce_grads")

    rep_shapes = [a[n].shape for n in REPLICATED]
    n_rep = sum(math.prod(sh) for sh in rep_shapes)
    rep_rows = -(-n_rep // (LANE * 8)) * 8
    pack_small = lambda arrs: jnp.pad(jnp.concatenate([t.reshape(-1) for t in arrs]), (0, rep_rows * LANE - n_rep)).reshape(rep_rows, LANE)
    g_rep = _sum_slots(_all_gather_small(pack_small([grads[n] for n in REPLICATED]), name="gather_small_grads"),
                       name="add_small_grads")

    out = {"loss": loss, "grad_x": grad_x[None]}
    for n, g in zip(SHARDED, _unpack(g_shard, shard_shapes)):
        flat = lambda t: t.reshape(-1, t.shape[-1])
        d, nm, nv = _adamw(flat(a[n]), flat(g), flat(a["m_" + n]), flat(a["v_" + n]), name="adamw_" + n)
        out["grad_" + n] = g
        out["delta_" + n], out["new_m_" + n], out["new_v_" + n] = (t.reshape(g.shape) for t in (d, nm, nv))
    d_rp, m_rp, v_rp = _adamw(pack_small([a[n] for n in REPLICATED]), g_rep,
                              pack_small([a["m_" + n] for n in REPLICATED]),
                              pack_small([a["v_" + n] for n in REPLICATED]), name="adamw_replicated")
    for prefix, rp_arr in (("grad_", g_rep), ("delta_", d_rp), ("new_m_", m_rp), ("new_v_", v_rp)):
        for n, t in zip(REPLICATED, _unpack_flat(rp_arr.reshape(-1)[:n_rep], rep_shapes)):
            out[prefix + n] = t
    return out


IN_NAMES = ["x", "positions"] + W_NAMES + ["loss_target"] + ["m_" + n for n in W_NAMES] + ["v_" + n for n in W_NAMES]
OUT_NAMES = (["loss", "grad_x"] + ["grad_" + n for n in W_NAMES] + ["delta_" + n for n in W_NAMES]
             + ["new_m_" + n for n in W_NAMES] + ["new_v_" + n for n in W_NAMES])


def kernel(x, positions, ln_ffn1, ffn1_w13, ffn1_w2, ln_mix, w_in, conv_w, conv_b, dt_bias, a_log, d_skip, ssd_norm, w_ssd_out, q_lora_norm, w_uq, kv_lora_norm, w_ukv, q_norm, k_norm, w_mla_out, w_o, ln_ffn2, ffn2_w13, ffn2_w2, loss_target, m_ln_ffn1, m_ffn1_w13, m_ffn1_w2, m_ln_mix, m_w_in, m_conv_w, m_conv_b, m_dt_bias, m_a_log, m_d_skip, m_ssd_norm, m_w_ssd_out, m_q_lora_norm, m_w_uq, m_kv_lora_norm, m_w_ukv, m_q_norm, m_k_norm, m_w_mla_out, m_w_o, m_ln_ffn2, m_ffn2_w13, m_ffn2_w2, v_ln_ffn1, v_ffn1_w13, v_ffn1_w2, v_ln_mix, v_w_in, v_conv_w, v_conv_b, v_dt_bias, v_a_log, v_d_skip, v_ssd_norm, v_w_ssd_out, v_q_lora_norm, v_w_uq, v_kv_lora_norm, v_w_ukv, v_q_norm, v_k_norm, v_w_mla_out, v_w_o, v_ln_ffn2, v_ffn2_w13, v_ffn2_w2):
    given = locals()
    out = _step({n: given[n] for n in IN_NAMES})
    return tuple(out[n] for n in OUT_NAMES)
```

```python
import functools
import math

import jax
import jax.numpy as jnp
from jax import lax
from jax.experimental import pallas as pl
from jax.experimental.pallas import tpu as pltpu

F32 = jnp.float32
BF16 = jnp.bfloat16

D_MODEL = 1024
DEPTH = 2
D_FF = 2816
SSD_D_INNER = 2048
SSD_HEADS = 32
SSD_HEAD_DIM = 64
SSD_GROUPS = 4
SSD_STATE = 128
SSD_CHUNK = 128
SSD_CONV = 4
SSD_CONV_DIM = 3072
MLA_HEADS = 8
MLA_Q_LORA = 512
MLA_KV_LORA = 256
MLA_NOPE = 128
MLA_ROPE = 64
MLA_V = 128
MLA_QK = 192
ROPE_THETA = 10000.0
EPS = 1e-6
ADAM_LR = 0.001
ADAM_B1 = 0.9
ADAM_B2 = 0.999
ADAM_EPS = 1e-08
ADAM_WD = 0.01
ADAM_STEP = 10

PROJ_W = 8064
OFF_Z, OFF_XBC, OFF_GATES, OFF_CQ, OFF_CKV, OFF_KRDT = 0, 2048, 5120, 7168, 7680, 7936

LANE = 128
VMEM_LIMIT = 48 * 1024 * 1024
HI = lax.Precision.HIGHEST


def _cp(*sem):
    return pltpu.CompilerParams(dimension_semantics=sem, vmem_limit_bytes=VMEM_LIMIT)


def _pick(dim, target, align):
    if dim <= target:
        return dim
    b = (target // align) * align
    while b >= align:
        if dim % b == 0:
            return b
        b -= align
    raise ValueError(f"no block for {dim} (target {target}, align {align})")


def _silu(x):
    return x * jax.nn.sigmoid(x)


def _dsilu(x):
    s = jax.nn.sigmoid(x)
    return s * (1.0 + x * (1.0 - s))


MM_VMEM_BUDGET = 40 * 1024 * 1024


def _mm_tiles(m, n, k, a_bytes, b_bytes, o_bytes):
    bn = _pick(n, 1408, LANE)
    for nk in (1, 2, 3, 4, 6, 7, 8):
        if k % nk or (k // nk) % LANE:
            continue
        bk = k // nk
        for bm in (1024, 512):
            if m % bm:
                continue
            need = 2 * (bm * bk * a_bytes + bk * bn * b_bytes + bm * bn * o_bytes) + (bm * bn * 4 if nk > 1 else 0)
            if need <= MM_VMEM_BUDGET:
                return bm, bn, bk
    return _pick(m, 512, 8), bn, _pick(k, 1536, LANE)

def _matmul(a, b, mode, *, name, out_dtype=F32, scale=1.0, res=None):
    if mode == "nn":
        (m, k), (k2, n) = a.shape, b.shape
    elif mode == "nt":
        (m, k), (n, k2) = a.shape, b.shape
    else:
        (k, m), (k2, n) = a.shape, b.shape
    assert k == k2, (a.shape, b.shape, mode)
    if mode == "tn":
        bn, bk = _pick(n, 2816, LANE), _pick(k, 1024, 8)
        bm = _pick(m, max(256, (1408 * 1024 // bn) // LANE * LANE), LANE)
    else:
        bm, bn, bk = _mm_tiles(m, n, k, a.dtype.itemsize, b.dtype.itemsize,
                               jnp.dtype(out_dtype).itemsize + (4 if res is not None else 0))
    nk = k // bk

    def body(a_ref, b_ref, *rest):
        res_ref = rest[0] if res is not None else None
        o_ref = rest[-2] if nk > 1 else rest[-1]
        kk = pl.program_id(2)
        av = a_ref[...].astype(BF16)
        bv = b_ref[...].astype(BF16)
        if mode == "nn":
            dims = (((1,), (0,)), ((), ()))
        elif mode == "nt":
            dims = (((1,), (1,)), ((), ()))
        else:
            dims = (((0,), (0,)), ((), ()))
        part = lax.dot_general(av, bv, dims, preferred_element_type=F32)

        def finish(total):
            out = total * scale
            if res_ref is not None:
                out = res_ref[...] + out
            o_ref[...] = out.astype(o_ref.dtype)

        if nk == 1:
            finish(part)
            return
        acc_ref = rest[-1]

        @pl.when(kk == 0)
        def _():
            acc_ref[...] = part

        @pl.when((kk > 0) & (kk < nk - 1))
        def _():
            acc_ref[...] += part

        @pl.when(kk == nk - 1)
        def _():
            finish(acc_ref[...] + part)

    o_spec = pl.BlockSpec((bm, bn), lambda i, j, kk: (i, j))
    if mode == "nn":
        a_spec = pl.BlockSpec((bm, bk), lambda i, j, kk: (i, kk))
        b_spec = pl.BlockSpec((bk, bn), lambda i, j, kk: (kk, j))
    elif mode == "nt":
        a_spec = pl.BlockSpec((bm, bk), lambda i, j, kk: (i, kk))
        b_spec = pl.BlockSpec((bn, bk), lambda i, j, kk: (j, kk))
    else:
        a_spec = pl.BlockSpec((bk, bm), lambda i, j, kk: (kk, i))
        b_spec = pl.BlockSpec((bk, bn), lambda i, j, kk: (kk, j))
    return pl.pallas_call(
        body, name=name,
        grid=(m // bm, n // bn, nk),
        in_specs=[a_spec, b_spec] + ([o_spec] if res is not None else []),
        out_specs=o_spec,
        out_shape=jax.ShapeDtypeStruct((m, n), out_dtype),
        scratch_shapes=[pltpu.VMEM((bm, bn), F32)] if nk > 1 else [],
        compiler_params=_cp("parallel", "parallel", "arbitrary"),
    )(*((a, b) + ((res,) if res is not None else ())))


def _rms_fwd(x, g, *, name, col=0, width=None):
    r = x.shape[0]
    w = width or x.shape[1]
    tr = _pick(r, 512, 16)

    def body(x_ref, g_ref, o_ref):
        xv = x_ref[...]
        rs = lax.rsqrt(jnp.mean(xv * xv, axis=-1, keepdims=True) + EPS)
        o_ref[...] = (xv * rs * g_ref[...]).astype(o_ref.dtype)

    return pl.pallas_call(
        body, name=name, grid=(r // tr,),
        in_specs=[pl.BlockSpec((tr, w), lambda i: (i, col)), pl.BlockSpec((1, w), lambda i: (0, 0))],
        out_specs=pl.BlockSpec((tr, w), lambda i: (i, 0)),
        out_shape=jax.ShapeDtypeStruct((r, w), BF16),
        compiler_params=_cp("parallel"),
    )(x, g)


def _rms_bwd(x, g, dy, *, name, col=0, width=None, res=None, out_dtype=F32):
    r = x.shape[0]
    w = width or x.shape[1]
    tr = _pick(r, 512, 8)

    def body(x_ref, g_ref, dy_ref, *rest):
        res_ref = rest[0] if res is not None else None
        dx_ref, dg_ref = rest[-2:]
        i = pl.program_id(0)
        xv = x_ref[...]
        dyv = dy_ref[...]
        rs = lax.rsqrt(jnp.mean(xv * xv, axis=-1, keepdims=True) + EPS)
        xh = xv * rs
        dxh = dyv * g_ref[...]
        mm = jnp.mean(dxh * xh, axis=-1, keepdims=True)
        dx = rs * (dxh - xh * mm)
        if res_ref is not None:
            dx = res_ref[...] + dx
        dx_ref[...] = dx.astype(dx_ref.dtype)
        part = jnp.sum(dyv * xh, axis=0, keepdims=True)

        @pl.when(i == 0)
        def _():
            dg_ref[...] = part

        @pl.when(i > 0)
        def _():
            dg_ref[...] += part

    blk = pl.BlockSpec((tr, w), lambda i: (i, 0))
    return pl.pallas_call(
        body, name=name, grid=(r // tr,),
        in_specs=[pl.BlockSpec((tr, w), lambda i: (i, col)), pl.BlockSpec((1, w), lambda i: (0, 0)), blk]
        + ([blk] if res is not None else []),
        out_specs=[blk, pl.BlockSpec((1, w), lambda i: (0, 0))],
        out_shape=[jax.ShapeDtypeStruct((r, w), out_dtype), jax.ShapeDtypeStruct((1, w), F32)],
        compiler_params=_cp("arbitrary"),
    )(*((x, g, dy) + ((res,) if res is not None else ())))


def _gated_rms_fwd(y, proj, g, *, name):
    r, w = y.shape
    tr = _pick(r, 256, 8)

    def body(y_ref, z_ref, g_ref, o_ref):
        t = y_ref[...] * _silu(z_ref[...])
        rs = lax.rsqrt(jnp.mean(t * t, axis=-1, keepdims=True) + EPS)
        o_ref[...] = (t * rs * g_ref[...]).astype(o_ref.dtype)

    return pl.pallas_call(
        body, name=name, grid=(r // tr,),
        in_specs=[pl.BlockSpec((tr, w), lambda i: (i, 0)), pl.BlockSpec((tr, w), lambda i: (i, OFF_Z // w)),
                  pl.BlockSpec((1, w), lambda i: (0, 0))],
        out_specs=pl.BlockSpec((tr, w), lambda i: (i, 0)),
        out_shape=jax.ShapeDtypeStruct((r, w), BF16),
        compiler_params=_cp("parallel"),
    )(y, proj, g)


def _gated_rms_bwd(y, proj, g, do, *, name):
    r, w = y.shape
    tr = _pick(r, 256, 8)

    def body(y_ref, z_ref, g_ref, do_ref, dy_ref, dz_ref, dg_ref):
        i = pl.program_id(0)
        yv, zv, dov = y_ref[...], z_ref[...], do_ref[...]
        sz = _silu(zv)
        t = yv * sz
        rs = lax.rsqrt(jnp.mean(t * t, axis=-1, keepdims=True) + EPS)
        th = t * rs
        dth = dov * g_ref[...]
        mm = jnp.mean(dth * th, axis=-1, keepdims=True)
        dt = rs * (dth - th * mm)
        dy_ref[...] = dt * sz
        dz_ref[...] = (dt * yv * _dsilu(zv)).astype(dz_ref.dtype)
        part = jnp.sum(dov * th, axis=0, keepdims=True)

        @pl.when(i == 0)
        def _():
            dg_ref[...] = part

        @pl.when(i > 0)
        def _():
            dg_ref[...] += part

    blk = pl.BlockSpec((tr, w), lambda i: (i, 0))
    vec = pl.BlockSpec((1, w), lambda i: (0, 0))
    return pl.pallas_call(
        body, name=name, grid=(r // tr,),
        in_specs=[blk, pl.BlockSpec((tr, w), lambda i: (i, OFF_Z // w)), vec, blk],
        out_specs=[blk, blk, vec],
        out_shape=[jax.ShapeDtypeStruct((r, w), F32), jax.ShapeDtypeStruct((r, w), BF16),
                   jax.ShapeDtypeStruct((1, w), F32)],
        compiler_params=_cp("arbitrary"),
    )(y, proj, g, do)


def _ffn_up(n, w13, *, name):
    m, k = n.shape
    f = w13.shape[1] // 2
    bm, bn = _pick(m, 1024, 16), _pick(f, 1408, LANE)
    nj = f // bn

    def body(a_ref, wg_ref, wu_ref, act_ref, g_ref, u_ref):
        a = a_ref[...].astype(BF16)
        g = _dot(a, wg_ref[...].astype(BF16))
        u = _dot(a, wu_ref[...].astype(BF16))
        act_ref[...] = (_silu(g) * u).astype(act_ref.dtype)
        g_ref[...] = g.astype(g_ref.dtype)
        u_ref[...] = u.astype(u_ref.dtype)

    out = pl.BlockSpec((bm, bn), lambda j, i: (i, j))
    sh = jax.ShapeDtypeStruct((m, f), BF16)
    return pl.pallas_call(
        body, name=name, grid=(nj, m // bm),
        in_specs=[pl.BlockSpec((bm, k), lambda j, i: (i, 0)), pl.BlockSpec((k, bn), lambda j, i: (0, j)),
                  pl.BlockSpec((k, bn), lambda j, i: (0, nj + j))],
        out_specs=[out, out, out], out_shape=[sh, sh, sh],
        compiler_params=_cp("parallel", "parallel"),
    )(n, w13, w13)


def _swiglu_bwd(g, u, da, *, name):
    r, f = g.shape
    tr = _pick(r, 256, 16)

    def body(g_ref, u_ref, da_ref, o_ref):
        gv, uv, dav = g_ref[...].astype(F32), u_ref[...].astype(F32), da_ref[...].astype(F32)
        o_ref[:, :f] = (dav * uv * _dsilu(gv)).astype(o_ref.dtype)
        o_ref[:, f:] = (dav * _silu(gv)).astype(o_ref.dtype)

    blk = pl.BlockSpec((tr, f), lambda i: (i, 0))
    return pl.pallas_call(
        body, name=name, grid=(r // tr,),
        in_specs=[blk, blk, blk],
        out_specs=pl.BlockSpec((tr, 2 * f), lambda i: (i, 0)),
        out_shape=jax.ShapeDtypeStruct((r, 2 * f), BF16),
        compiler_params=_cp("parallel"),
    )(g, u, da)


CONV_TS = 1024
CONV_TC = 512


def _conv_pre(x, carry, w_ref, b_ref):
    ts = x.shape[0]
    row8 = lax.broadcasted_iota(jnp.int32, (8, x.shape[1]), 0)
    head_x = x[0:8]
    shifted, shifted_head = [], []
    for j in range(SSD_CONV):
        if j == 0:
            shifted.append(x)
            shifted_head.append(head_x)
        else:
            shifted.append(pltpu.roll(x, j, 0))
            shifted_head.append(jnp.where(row8 < j, pltpu.roll(carry, j, 0), pltpu.roll(head_x, j, 0)))
    pre = b_ref[...] + sum(w_ref[SSD_CONV - 1 - j:SSD_CONV - j, :] * shifted[j] for j in range(SSD_CONV))
    pre_head = b_ref[...] + sum(w_ref[SSD_CONV - 1 - j:SSD_CONV - j, :] * shifted_head[j] for j in range(SSD_CONV))
    del ts
    return pre, pre_head, shifted, shifted_head


def _conv_fwd(proj, w, b, *, name):
    s = proj.shape[0]
    c = w.shape[1]
    ts, tc = _pick(s, CONV_TS, 8), CONV_TC
    off = OFF_XBC // tc

    def body(x_ref, w_ref, b_ref, o_ref, carry_ref):
        t = pl.program_id(1)

        @pl.when(t == 0)
        def _():
            carry_ref[...] = jnp.zeros_like(carry_ref)

        x = x_ref[...]
        pre, pre_head, _, _ = _conv_pre(x, carry_ref[...], w_ref, b_ref)
        o_ref[...] = _silu(pre)
        o_ref[0:8, :] = _silu(pre_head)
        carry_ref[...] = x[ts - 8:ts]

    return pl.pallas_call(
        body, name=name, grid=(c // tc, s // ts),
        in_specs=[pl.BlockSpec((ts, tc), lambda j, t: (t, j + off)), pl.BlockSpec((SSD_CONV, tc), lambda j, t: (0, j)),
                  pl.BlockSpec((1, tc), lambda j, t: (0, j))],
        out_specs=pl.BlockSpec((ts, tc), lambda j, t: (t, j)),
        out_shape=jax.ShapeDtypeStruct((s, c), F32),
        scratch_shapes=[pltpu.VMEM((8, tc), F32)],
        compiler_params=_cp("parallel", "arbitrary"),
    )(proj, w, b)


def _conv_bwd_pre(proj, w, b, dy, *, name):
    s = proj.shape[0]
    c = w.shape[1]
    ts, tc = _pick(s, CONV_TS, 8), CONV_TC
    off = OFF_XBC // tc

    def body(x_ref, w_ref, b_ref, dy_ref, dp_ref, dw_ref, db_ref, carry_ref):
        t = pl.program_id(1)

        @pl.when(t == 0)
        def _():
            carry_ref[...] = jnp.zeros_like(carry_ref)
            dw_ref[...] = jnp.zeros_like(dw_ref)
            db_ref[...] = jnp.zeros_like(db_ref)

        x = x_ref[...]
        pre, pre_head, shifted, shifted_head = _conv_pre(x, carry_ref[...], w_ref, b_ref)
        dyv = dy_ref[...]
        dp = dyv * _dsilu(pre)
        dp_head = dyv[0:8] * _dsilu(pre_head)
        row = lax.broadcasted_iota(jnp.int32, dp.shape, 0)
        dp_tail = jnp.where(row >= 8, dp, 0.0)
        dp_ref[...] = dp
        dp_ref[0:8, :] = dp_head
        db_ref[...] += jnp.sum(dp_tail, axis=0, keepdims=True) + jnp.sum(dp_head, axis=0, keepdims=True)
        for j in range(SSD_CONV):
            kk = SSD_CONV - 1 - j
            dw_ref[kk:kk + 1, :] += (jnp.sum(dp_tail * shifted[j], axis=0, keepdims=True)
                                     + jnp.sum(dp_head * shifted_head[j], axis=0, keepdims=True))
        carry_ref[...] = x[ts - 8:ts]

    return pl.pallas_call(
        body, name=name, grid=(c // tc, s // ts),
        in_specs=[pl.BlockSpec((ts, tc), lambda j, t: (t, j + off)), pl.BlockSpec((SSD_CONV, tc), lambda j, t: (0, j)),
                  pl.BlockSpec((1, tc), lambda j, t: (0, j)), pl.BlockSpec((ts, tc), lambda j, t: (t, j))],
        out_specs=[pl.BlockSpec((ts, tc), lambda j, t: (t, j)), pl.BlockSpec((SSD_CONV, tc), lambda j, t: (0, j)),
                   pl.BlockSpec((1, tc), lambda j, t: (0, j))],
        out_shape=[jax.ShapeDtypeStruct((s, c), F32), jax.ShapeDtypeStruct((SSD_CONV, c), F32),
                   jax.ShapeDtypeStruct((1, c), F32)],
        scratch_shapes=[pltpu.VMEM((8, tc), F32)],
        compiler_params=_cp("parallel", "arbitrary"),
    )(proj, w, b, dy)


def _conv_bwd_x(dp, w, *, name):
    s, c = dp.shape
    ts, tc = _pick(s, CONV_TS, 8), CONV_TC
    nt = s // ts

    def body(d_ref, w_ref, o_ref, carry_ref, full_ref):
        t = pl.program_id(1)

        @pl.when(t == 0)
        def _():
            carry_ref[...] = jnp.zeros_like(carry_ref)

        d = d_ref[...]
        carry = carry_ref[...]
        row8 = lax.broadcasted_iota(jnp.int32, (8, tc), 0)
        tail = d[ts - 8:ts]
        acc = w_ref[SSD_CONV - 1:SSD_CONV, :] * d
        acc_tail = w_ref[SSD_CONV - 1:SSD_CONV, :] * tail
        for j in range(1, SSD_CONV):
            wj = w_ref[SSD_CONV - 1 - j:SSD_CONV - j, :]
            acc = acc + wj * pltpu.roll(d, ts - j, 0)
            up_tail = jnp.where(row8 >= 8 - j, pltpu.roll(carry, 8 - j, 0), pltpu.roll(tail, 8 - j, 0))
            acc_tail = acc_tail + wj * up_tail
        full_ref[...] = acc
        full_ref[ts - 8:ts, :] = acc_tail
        o_ref[...] = full_ref[...].astype(o_ref.dtype)
        carry_ref[...] = d[0:8]

    return pl.pallas_call(
        body, name=name, grid=(c // tc, nt),
        in_specs=[pl.BlockSpec((ts, tc), lambda j, t: (nt - 1 - t, j)), pl.BlockSpec((SSD_CONV, tc), lambda j, t: (0, j))],
        out_specs=pl.BlockSpec((ts, tc), lambda j, t: (nt - 1 - t, j)),
        out_shape=jax.ShapeDtypeStruct((s, c), BF16),
        scratch_shapes=[pltpu.VMEM((8, tc), F32), pltpu.VMEM((ts, tc), F32)],
        compiler_params=_cp("parallel", "arbitrary"),
    )(dp, w)


def _merge_fwd(proj, ys, ym, *, name):
    r, w = ys.shape
    tr = _pick(r, 512, 8)
    off = OFF_GATES // w

    def body(g1_ref, g2_ref, ys_ref, ym_ref, o_ref):
        o_ref[...] = (jax.nn.sigmoid(g1_ref[...]) * ys_ref[...]
                      + jax.nn.sigmoid(g2_ref[...]) * ym_ref[...]).astype(o_ref.dtype)

    blk = pl.BlockSpec((tr, w), lambda i: (i, 0))
    return pl.pallas_call(
        body, name=name, grid=(r // tr,),
        in_specs=[pl.BlockSpec((tr, w), lambda i: (i, off)), pl.BlockSpec((tr, w), lambda i: (i, off + 1)), blk, blk],
        out_specs=blk, out_shape=jax.ShapeDtypeStruct((r, w), BF16),
        compiler_params=_cp("parallel"),
    )(proj, proj, ys, ym)


def _merge_bwd(proj, ys, ym, dm, *, name):
    r, w = ys.shape
    tr = _pick(r, 512, 8)
    off = OFF_GATES // w

    def body(g1_ref, g2_ref, ys_ref, ym_ref, dm_ref, dg_ref, dys_ref, dym_ref):
        s1, s2 = jax.nn.sigmoid(g1_ref[...]), jax.nn.sigmoid(g2_ref[...])
        dmv = dm_ref[...]
        dys_ref[...] = (dmv * s1).astype(dys_ref.dtype)
        dym_ref[...] = (dmv * s2).astype(dym_ref.dtype)
        dg_ref[:, :w] = (dmv * ys_ref[...] * s1 * (1.0 - s1)).astype(dg_ref.dtype)
        dg_ref[:, w:] = (dmv * ym_ref[...] * s2 * (1.0 - s2)).astype(dg_ref.dtype)

    blk = pl.BlockSpec((tr, w), lambda i: (i, 0))
    return pl.pallas_call(
        body, name=name, grid=(r // tr,),
        in_specs=[pl.BlockSpec((tr, w), lambda i: (i, off)), pl.BlockSpec((tr, w), lambda i: (i, off + 1)), blk, blk, blk],
        out_specs=[pl.BlockSpec((tr, 2 * w), lambda i: (i, 0)), blk, blk],
        out_shape=[jax.ShapeDtypeStruct((r, 2 * w), BF16), jax.ShapeDtypeStruct((r, w), BF16),
                   jax.ShapeDtypeStruct((r, w), BF16)],
        compiler_params=_cp("parallel"),
    )(proj, proj, ys, ym, dm)


def _loss_fwd_bwd(y, target, *, name):
    r, w = y.shape
    tr = _pick(r, 512, 8)

    def body(y_ref, t_ref, l_ref, dy_ref):
        i = pl.program_id(0)
        e = y_ref[...] - t_ref[...]
        dy_ref[...] = e * (1.0 / w)
        part = jnp.sum(e * e, axis=0, keepdims=True) * (0.5 / w)

        @pl.when(i == 0)
        def _():
            l_ref[...] = part

        @pl.when(i > 0)
        def _():
            l_ref[...] += part

    blk = pl.BlockSpec((tr, w), lambda i: (i, 0))
    return pl.pallas_call(
        body, name=name, grid=(r // tr,),
        in_specs=[blk, blk],
        out_specs=[pl.BlockSpec((1, w), lambda i: (0, 0)), blk],
        out_shape=[jax.ShapeDtypeStruct((1, w), F32), jax.ShapeDtypeStruct((r, w), F32)],
        compiler_params=_cp("arbitrary"),
    )(y, target)


def _adamw(w, g, m, v, *, name):
    r, c = w.shape
    tr = _pick(r, max(8, (1 << 20) // (4 * c) // 8 * 8), 8)
    c1 = 1.0 - ADAM_B1 ** ADAM_STEP
    c2 = 1.0 - ADAM_B2 ** ADAM_STEP

    def body(w_ref, g_ref, m_ref, v_ref, d_ref, nm_ref, nv_ref):
        gv = g_ref[...]
        nm = ADAM_B1 * m_ref[...] + (1.0 - ADAM_B1) * gv
        nv = ADAM_B2 * v_ref[...] + (1.0 - ADAM_B2) * (gv * gv)
        nm_ref[...] = nm
        nv_ref[...] = nv
        d_ref[...] = -ADAM_LR * ((nm / c1) / (jnp.sqrt(nv / c2) + ADAM_EPS) + ADAM_WD * w_ref[...])

    blk = pl.BlockSpec((tr, c), lambda i: (i, 0))
    sh = jax.ShapeDtypeStruct((r, c), F32)
    return pl.pallas_call(
        body, name=name, grid=(r // tr,),
        in_specs=[blk] * 4, out_specs=[blk] * 3, out_shape=[sh] * 3,
        compiler_params=_cp("parallel"),
    )(w, g, m, v)


def _softplus(x):
    return jnp.maximum(x, 0.0) + jnp.log(1.0 + jnp.exp(-jnp.abs(x)))


def _dot_01(x, sel):
    sel_b = sel.astype(BF16)
    acc, rem = None, x
    for _ in range(3):
        piece = rem.astype(BF16)
        part = jnp.dot(piece, sel_b, preferred_element_type=F32)
        acc = part if acc is None else acc + part
        rem = rem - piece.astype(F32)
    return acc


def _ssd_common(dtr_ref, dtrT_ref, dtb_ref, dtbT_ref, al_ref, alT_ref, e_ref):
    L = SSD_CHUNK
    ri = lax.broadcasted_iota(jnp.int32, (L, L), 0)
    cj = lax.broadcasted_iota(jnp.int32, (L, L), 1)
    tril = (ri >= cj).astype(F32)
    triu = (ri <= cj).astype(F32)
    a = -jnp.exp(al_ref[...])
    aT = -jnp.exp(alT_ref[...])
    pre = dtr_ref[...] + dtb_ref[...]
    preT = dtrT_ref[...] + dtbT_ref[...]
    dt = _softplus(pre)
    dtT = _softplus(preT)
    acum = jnp.dot(tril, dt * a, precision=HI, preferred_element_type=F32)
    acumT = jnp.dot(dtT * aT, triu, precision=HI, preferred_element_type=F32)
    e = e_ref[...]
    dt_x = _dot_01(dt, e)
    acum_x = _dot_01(acum, e)
    last_x = acum_x[L - 1:L, :]
    return dict(ri=ri, cj=cj, tril=tril, triu=triu, a=a, aT=aT, pre=pre, preT=preT, dt=dt, dtT=dtT,
                acum=acum, acumT=acumT, dt_x=dt_x, eacum_x=jnp.exp(acum_x), w_x=jnp.exp(last_x - acum_x),
                elast_x=jnp.exp(last_x))


def _dot_nt(a, b):
    return lax.dot_general(a, b, (((1,), (1,)), ((), ())), preferred_element_type=F32)


def _dot_tn(a, b):
    return lax.dot_general(a, b, (((0,), (0,)), ((), ())), preferred_element_type=F32)


def _dot(a, b):
    return jnp.dot(a, b, preferred_element_type=F32)


def _ssd_specs(nc, rev):
    L = SSD_CHUNK
    ix = (lambda c: nc - 1 - c) if rev else (lambda c: c)
    return [
        pl.BlockSpec((L, SSD_D_INNER), lambda c: (ix(c), 0)),
        pl.BlockSpec((L, 512), lambda c: (ix(c), 4)),
        pl.BlockSpec((L, 512), lambda c: (ix(c), 5)),
        pl.BlockSpec((L, SSD_HEADS), lambda c: (ix(c), 0)),
        pl.BlockSpec((SSD_HEADS, L), lambda c: (0, ix(c))),
        pl.BlockSpec((1, SSD_HEADS), lambda c: (0, 0)),
        pl.BlockSpec((SSD_HEADS, 1), lambda c: (0, 0)),
        pl.BlockSpec((1, SSD_HEADS), lambda c: (0, 0)),
        pl.BlockSpec((SSD_HEADS, 1), lambda c: (0, 0)),
        pl.BlockSpec((1, SSD_D_INNER), lambda c: (0, 0)),
        pl.BlockSpec((SSD_HEADS, SSD_D_INNER), lambda c: (0, 0)),
    ]


def _ssd_fwd(xc, dtr, dtrT, dtb, dtbT, alog, alogT, dskx, expand, *, name):
    s = xc.shape[0]
    L = SSD_CHUNK
    nc = s // L

    def body(x_ref, b_ref, c_ref, dtr_ref, dtrT_ref, dtb_ref, dtbT_ref, al_ref, alT_ref, dsk_ref, e_ref,
             y_ref, st_ref, state):
        ci = pl.program_id(0)

        @pl.when(ci == 0)
        def _():
            state[...] = jnp.zeros_like(state)

        st_ref[0] = state[...]
        q = _ssd_common(dtr_ref, dtrT_ref, dtb_ref, dtbT_ref, al_ref, alT_ref, e_ref)
        causal = q["ri"] >= q["cj"]
        lane_lo = q["cj"] < 64
        x = x_ref[...]
        xdt = x * q["dt_x"]
        xdt_b = xdt.astype(BF16)
        xdtw_b = (xdt * q["w_x"]).astype(BF16)
        for g in range(SSD_GROUPS):
            bg = b_ref[:, 128 * g:128 * g + 128]
            cg_b = c_ref[:, 128 * g:128 * g + 128].astype(BF16)
            cb = _dot_nt(cg_b, bg.astype(BF16))
            bgT_b = bg.T.astype(BF16)
            s0 = state[g]
            for jj in range(4):
                j = 4 * g + jj
                sl = slice(128 * j, 128 * j + 128)
                sls = slice(128 * jj, 128 * jj + 128)
                ms = []
                for h in (2 * j, 2 * j + 1):
                    seg = q["acum"][:, h:h + 1] - q["acumT"][h:h + 1, :]
                    decay = jnp.exp(jnp.where(causal, seg, -jnp.inf))
                    ms.append((cb * decay).astype(BF16))
                mcat = jnp.concatenate(ms, axis=1)
                xp = xdt_b[:, sl]
                zero = jnp.zeros_like(xp)
                xstack = jnp.concatenate([jnp.where(lane_lo, xp, zero), jnp.where(lane_lo, zero, xp)], axis=0)
                y = _dot(mcat, xstack)
                y = y + q["eacum_x"][:, sl] * _dot(cg_b, s0[:, sls].astype(BF16))
                y = y + x[:, sl] * dsk_ref[:, sl]
                y_ref[:, sl] = y
                state[g, :, sls] = s0[:, sls] * q["elast_x"][:, sl] + _dot(bgT_b, xdtw_b[:, sl])

    return pl.pallas_call(
        body, name=name, grid=(nc,),
        in_specs=_ssd_specs(nc, False),
        out_specs=[pl.BlockSpec((L, SSD_D_INNER), lambda c: (c, 0)),
                   pl.BlockSpec((1, SSD_GROUPS, SSD_STATE, 512), lambda c: (c, 0, 0, 0))],
        out_shape=[jax.ShapeDtypeStruct((s, SSD_D_INNER), F32),
                   jax.ShapeDtypeStruct((nc, SSD_GROUPS, SSD_STATE, 512), F32)],
        scratch_shapes=[pltpu.VMEM((SSD_GROUPS, SSD_STATE, 512), F32)],
        compiler_params=_cp("arbitrary"),
    )(xc, xc, xc, dtr, dtrT, dtb, dtbT, alog, alogT, dskx, expand)


def _ssd_bwd(xc, dtr, dtrT, dtb, dtbT, alog, alogT, dskx, expand, expandT, states, dy, *, name):
    s = xc.shape[0]
    L = SSD_CHUNK
    H = SSD_HEADS
    nc = s // L

    def body(x_ref, b_ref, c_ref, dtr_ref, dtrT_ref, dtb_ref, dtbT_ref, al_ref, alT_ref, dsk_ref, e_ref,
             et_ref, st_ref, dy_ref,
             dxc_ref, ddtc_ref, ddtr_ref, dbc_ref, dbr_ref, dac_ref, dar_ref, ddsk_ref, dstate):
        ci = pl.program_id(0)

        @pl.when(ci == 0)
        def _():
            dstate[...] = jnp.zeros_like(dstate)
            dbc_ref[...] = jnp.zeros_like(dbc_ref)
            dbr_ref[...] = jnp.zeros_like(dbr_ref)
            dac_ref[...] = jnp.zeros_like(dac_ref)
            dar_ref[...] = jnp.zeros_like(dar_ref)
            ddsk_ref[...] = jnp.zeros_like(ddsk_ref)

        q = _ssd_common(dtr_ref, dtrT_ref, dtb_ref, dtbT_ref, al_ref, alT_ref, e_ref)
        ri, cj = q["ri"], q["cj"]
        causal = ri >= cj
        causalT = ri <= cj
        lane_lo = cj < 64
        lane_h = lax.broadcasted_iota(jnp.int32, (1, H), 1)
        sub_h = lax.broadcasted_iota(jnp.int32, (H, 1), 0)
        x = x_ref[...]
        dyv = dy_ref[...]
        xdt = x * q["dt_x"]
        xdt_b = xdt.astype(BF16)
        xdtw = xdt * q["w_x"]
        xdtw_b = xdtw.astype(BF16)
        edy = q["eacum_x"] * dyv
        edy_b = edy.astype(BF16)
        dyv_b = dyv.astype(BF16)
        dacum_col = jnp.zeros((L, H), F32)
        dacum_row = jnp.zeros((H, L), F32)
        dxdt_t, yoff_t, u_t, r_t = [], [], [], []
        for g in range(SSD_GROUPS):
            bg = b_ref[:, 128 * g:128 * g + 128]
            cg = c_ref[:, 128 * g:128 * g + 128]
            bg_b, cg_b = bg.astype(BF16), cg.astype(BF16)
            cb = _dot_nt(cg_b, bg_b)
            cbT = _dot_nt(bg_b, cg_b)
            cgT_b = cg.T.astype(BF16)
            s0 = st_ref[0, g]
            ds = dstate[g]
            s0_b, ds_b = s0.astype(BF16), ds.astype(BF16)
            dcb = jnp.zeros((L, L), F32)
            for jj in range(4):
                j = 4 * g + jj
                sl = slice(128 * j, 128 * j + 128)
                sls = slice(128 * jj, 128 * jj + 128)
                decs, mts = [], []
                for h in (2 * j, 2 * j + 1):
                    seg = q["acum"][:, h:h + 1] - q["acumT"][h:h + 1, :]
                    decs.append(jnp.exp(jnp.where(causal, seg, -jnp.inf)))
                    mts.append((cbT * jnp.exp(jnp.where(causalT, -seg, -jnp.inf))).astype(BF16))
                dyt_b = dyv_b[:, sl]
                zero = jnp.zeros_like(dyt_b)
                dystack = jnp.concatenate([jnp.where(lane_lo, dyt_b, zero), jnp.where(lane_lo, zero, dyt_b)], axis=0)
                dxs = _dot(jnp.concatenate(mts, axis=0), dyt_b)
                dxdt = jnp.where(lane_lo, dxs[:L], dxs[L:])
                dmcat = _dot_nt(dystack, xdt_b[:, sl])
                for idx, h in enumerate((2 * j, 2 * j + 1)):
                    dm = dmcat[L * idx:L * idx + L]
                    dcb = dcb + dm * decs[idx]
                    dseg = dm * cb * decs[idx]
                    dacum_col = dacum_col + jnp.sum(dseg, axis=1, keepdims=True) * (lane_h == h).astype(F32)
                    dacum_row = dacum_row - (sub_h == h).astype(F32) * jnp.sum(dseg, axis=0, keepdims=True)
                gmat = _dot(cg_b, s0_b[:, sls])
                yoff_t.append(edy[:, sl] * gmat)
                qm = _dot(bg_b, ds_b[:, sls])
                dxdt_t.append(dxdt + qm * q["w_x"][:, sl])
                u_t.append(qm * xdtw[:, sl])
                r_t.append(ds[:, sls] * s0[:, sls] * q["elast_x"][:, sl])
                dstate[g, :, sls] = ds[:, sls] * q["elast_x"][:, sl] + _dot(cgT_b, edy_b[:, sl])
            gsl = slice(512 * g, 512 * g + 512)
            dcb_b = dcb.astype(BF16)
            dcg = _dot(dcb_b, bg_b) + _dot_nt(edy_b[:, gsl], s0_b)
            dbg = _dot(dcb.T.astype(BF16), cg_b) + _dot_nt(xdtw_b[:, gsl], ds_b)
            dxc_ref[:, SSD_D_INNER + 128 * g:SSD_D_INNER + 128 * g + 128] = dbg
            dxc_ref[:, SSD_D_INNER + 512 + 128 * g:SSD_D_INNER + 512 + 128 * g + 128] = dcg
        et = et_ref[...]
        dxdt_all = jnp.concatenate(dxdt_t, axis=1)
        yoff = jnp.concatenate(yoff_t, axis=1)
        uu = jnp.concatenate(u_t, axis=1)
        rr = jnp.concatenate(r_t, axis=1)
        dacum_col = dacum_col + _dot_01(yoff - uu, et)
        dlast = jnp.sum(_dot_01(uu + rr, et), axis=0, keepdims=True)
        row_lh = lax.broadcasted_iota(jnp.int32, (L, H), 0)
        dacum_col = dacum_col + jnp.where(row_lh == L - 1, dlast, 0.0)
        d_dta_col = jnp.dot(q["triu"], dacum_col, precision=HI, preferred_element_type=F32)
        d_dta_row = jnp.dot(dacum_row, q["tril"], precision=HI, preferred_element_type=F32)
        ddt_col = d_dta_col * q["a"] + _dot_01(dxdt_all * x, et)
        ddt_row = d_dta_row * q["aT"]
        ddtr_col = ddt_col * jax.nn.sigmoid(q["pre"])
        ddtr_row = ddt_row * jax.nn.sigmoid(q["preT"])
        ddtc_ref[...] = ddtr_col
        ddtr_ref[...] = ddtr_row
        dac_ref[...] += jnp.sum(d_dta_col * q["dt"], axis=0, keepdims=True)
        dar_ref[...] += jnp.sum(d_dta_row * q["dtT"], axis=1, keepdims=True)
        dbc_ref[...] += jnp.sum(ddtr_col, axis=0, keepdims=True)
        dbr_ref[...] += jnp.sum(ddtr_row, axis=1, keepdims=True)
        ddsk_ref[...] += jnp.sum(dyv * x, axis=0, keepdims=True)
        dxc_ref[:, 0:SSD_D_INNER] = dxdt_all * q["dt_x"] + dyv * dsk_ref[...]

    rv = lambda c: nc - 1 - c
    in_specs = _ssd_specs(nc, True) + [
        pl.BlockSpec((SSD_D_INNER, H), lambda c: (0, 0)),
        pl.BlockSpec((1, SSD_GROUPS, SSD_STATE, 512), lambda c: (rv(c), 0, 0, 0)),
        pl.BlockSpec((L, SSD_D_INNER), lambda c: (rv(c), 0)),
    ]
    vec_c = pl.BlockSpec((1, H), lambda c: (0, 0))
    vec_r = pl.BlockSpec((H, 1), lambda c: (0, 0))
    return pl.pallas_call(
        body, name=name, grid=(nc,),
        in_specs=in_specs,
        out_specs=[pl.BlockSpec((L, SSD_CONV_DIM), lambda c: (rv(c), 0)),
                   pl.BlockSpec((L, H), lambda c: (rv(c), 0)),
                   pl.BlockSpec((H, L), lambda c: (0, rv(c))),
                   vec_c, vec_r, vec_c, vec_r,
                   pl.BlockSpec((1, SSD_D_INNER), lambda c: (0, 0))],
        out_shape=[jax.ShapeDtypeStruct((s, SSD_CONV_DIM), F32),
                   jax.ShapeDtypeStruct((s, H), F32), jax.ShapeDtypeStruct((H, s), F32),
                   jax.ShapeDtypeStruct((1, H), F32), jax.ShapeDtypeStruct((H, 1), F32),
                   jax.ShapeDtypeStruct((1, H), F32), jax.ShapeDtypeStruct((H, 1), F32),
                   jax.ShapeDtypeStruct((1, SSD_D_INNER), F32)],
        scratch_shapes=[pltpu.VMEM((SSD_GROUPS, SSD_STATE, 512), F32)],
        compiler_params=_cp("arbitrary"),
    )(xc, xc, xc, dtr, dtrT, dtb, dtbT, alog, alogT, dskx, expand, expandT, states, dy)


QK_PAD = 256
MLA_TS = 256


def _rope_tables4(pos):
    inv = 1.0 / (ROPE_THETA ** (jnp.arange(0, MLA_ROPE, 2, dtype=F32) / MLA_ROPE))
    ang = pos.astype(F32)[:, None] * inv
    c, s = jnp.cos(ang), jnp.sin(ang)
    return jnp.tile(c, (1, 4)), jnp.concatenate([-s, s, -s, s], axis=1)


def _mla_gains(qg, kg):
    z = jnp.zeros((LANE - MLA_ROPE,), F32)
    return (qg[:MLA_NOPE][None], jnp.concatenate([qg[MLA_NOPE:], z])[None],
            kg[:MLA_NOPE][None], jnp.concatenate([kg[MLA_NOPE:], z])[None])


def _rope_swap(t, first):
    return jnp.where(first, pltpu.roll(t, 96, 1), pltpu.roll(t, 32, 1))


def _mla_prep_specs(ts):
    row = lambda w, c=0: pl.BlockSpec((ts, w), lambda i: (i, c))
    vec = pl.BlockSpec((1, LANE), lambda i: (0, 0))
    return [row(MLA_HEADS * MLA_QK), row(2 * MLA_HEADS * MLA_NOPE), row(LANE, OFF_KRDT // LANE), row(LANE), row(LANE),
            vec, vec, vec, vec]


def _mla_prep_fwd(qraw, kvraw, proj, cos4, sin4, gqn, gqr, gkn, gkr, *, name):
    s = qraw.shape[0]
    ts = _pick(s, MLA_TS, 8)

    def body(q_ref, kv_ref, kr_ref, cos_ref, sin_ref, gqn_ref, gqr_ref, gkn_ref, gkr_ref, qo_ref, ko_ref):
        lane = lax.broadcasted_iota(jnp.int32, (ts, LANE), 1)
        lo = lane < 64
        first = (lane % 64) < 32
        cos, sin = cos_ref[...], sin_ref[...]
        kr = jnp.where(lo, kr_ref[...], 0.0)
        ssq_kr = jnp.sum(kr * kr, axis=-1, keepdims=True)

        def head(xn, xr, ssq_r, gn, gr):
            rs = lax.rsqrt((jnp.sum(xn * xn, axis=-1, keepdims=True) + ssq_r) * (1.0 / MLA_QK) + EPS)
            yr = xr * rs * gr
            return xn * rs * gn, yr * cos + _rope_swap(yr, first) * sin

        for h in range(MLA_HEADS):
            tile = q_ref[:, MLA_HEADS * MLA_NOPE + LANE * (h // 2):MLA_HEADS * MLA_NOPE + LANE * (h // 2) + LANE]
            qr = jnp.where(lo, tile if h % 2 == 0 else pltpu.roll(tile, 64, 1), 0.0)
            on, orr = head(q_ref[:, LANE * h:LANE * h + LANE], qr, jnp.sum(qr * qr, axis=-1, keepdims=True),
                           gqn_ref[...], gqr_ref[...])
            qo_ref[h, :, 0:LANE] = (on * ATT_SCALE).astype(BF16)
            qo_ref[h, :, LANE:QK_PAD] = (orr * ATT_SCALE).astype(BF16)
            on, orr = head(kv_ref[:, LANE * h:LANE * h + LANE], kr, ssq_kr, gkn_ref[...], gkr_ref[...])
            ko_ref[h, :, 0:LANE] = on.astype(BF16)
            ko_ref[h, :, LANE:QK_PAD] = orr.astype(BF16)

    out = pl.BlockSpec((MLA_HEADS, ts, QK_PAD), lambda i: (0, i, 0))
    sh = jax.ShapeDtypeStruct((MLA_HEADS, s, QK_PAD), BF16)
    return pl.pallas_call(
        body, name=name, grid=(s // ts,),
        in_specs=_mla_prep_specs(ts), out_specs=[out, out], out_shape=[sh, sh],
        compiler_params=_cp("parallel"),
    )(qraw, kvraw, proj, cos4, sin4, gqn, gqr, gkn, gkr)


def _mla_prep_bwd(qraw, kvraw, proj, cos4, sin4, gqn, gqr, gkn, gkr, dq, dk, *, name):
    s = qraw.shape[0]
    ts = _pick(s, MLA_TS, 8)

    def body(q_ref, kv_ref, kr_ref, cos_ref, sin_ref, gqn_ref, gqr_ref, gkn_ref, gkr_ref, dq_ref, dk_ref,
             dqraw_ref, dkn_ref, dkr_ref, dgqn_ref, dgqr_ref, dgkn_ref, dgkr_ref):
        i = pl.program_id(0)

        @pl.when(i == 0)
        def _():
            for r in (dgqn_ref, dgqr_ref, dgkn_ref, dgkr_ref):
                r[...] = jnp.zeros_like(r)

        lane = lax.broadcasted_iota(jnp.int32, (ts, LANE), 1)
        lo = lane < 64
        first = (lane % 64) < 32
        cos, sin = cos_ref[...], sin_ref[...]
        kr = jnp.where(lo, kr_ref[...], 0.0)
        ssq_kr = jnp.sum(kr * kr, axis=-1, keepdims=True)

        def head(xn, xr, ssq_r, gn, gr, don, dor):
            rs = lax.rsqrt((jnp.sum(xn * xn, axis=-1, keepdims=True) + ssq_r) * (1.0 / MLA_QK) + EPS)
            xhn, xhr = xn * rs, xr * rs
            dor = jnp.where(lo, dor, 0.0)
            dyr = dor * cos + _rope_swap(dor * sin, first)
            dxn, dxr = don * gn, dyr * gr
            mm = (jnp.sum(dxn * xhn, axis=-1, keepdims=True) + jnp.sum(dxr * xhr, axis=-1, keepdims=True)) * (1.0 / MLA_QK)
            return (rs * (dxn - xhn * mm), rs * (dxr - xhr * mm),
                    jnp.sum(don * xhn, axis=0, keepdims=True), jnp.sum(dyr * xhr, axis=0, keepdims=True))

        dkr_acc = jnp.zeros((ts, LANE), F32)
        prev = None
        for h in range(MLA_HEADS):
            c0 = MLA_HEADS * MLA_NOPE + LANE * (h // 2)
            tile = q_ref[:, c0:c0 + LANE]
            qr = jnp.where(lo, tile if h % 2 == 0 else pltpu.roll(tile, 64, 1), 0.0)
            dn, dr, gn_p, gr_p = head(q_ref[:, LANE * h:LANE * h + LANE], qr, jnp.sum(qr * qr, axis=-1, keepdims=True),
                                      gqn_ref[...], gqr_ref[...], dq_ref[h, :, 0:LANE], dq_ref[h, :, LANE:QK_PAD])
            dqraw_ref[:, LANE * h:LANE * h + LANE] = dn.astype(dqraw_ref.dtype)
            dgqn_ref[...] += gn_p
            dgqr_ref[...] += gr_p
            if h % 2 == 0:
                prev = dr
            else:
                dqraw_ref[:, c0:c0 + LANE] = (prev + pltpu.roll(dr, 64, 1)).astype(dqraw_ref.dtype)
            dn, dr, gn_p, gr_p = head(kv_ref[:, LANE * h:LANE * h + LANE], kr, ssq_kr, gkn_ref[...], gkr_ref[...],
                                      dk_ref[h, :, 0:LANE], dk_ref[h, :, LANE:QK_PAD])
            dkn_ref[:, LANE * h:LANE * h + LANE] = dn.astype(dkn_ref.dtype)
            dkr_acc = dkr_acc + dr
            dgkn_ref[...] += gn_p
            dgkr_ref[...] += gr_p
        dkr_ref[...] = dkr_acc

    row = lambda w: pl.BlockSpec((ts, w), lambda i: (i, 0))
    vec = pl.BlockSpec((1, LANE), lambda i: (0, 0))
    dspec = pl.BlockSpec((MLA_HEADS, ts, QK_PAD), lambda i: (0, i, 0))
    vsh = jax.ShapeDtypeStruct((1, LANE), F32)
    return pl.pallas_call(
        body, name=name, grid=(s // ts,),
        in_specs=_mla_prep_specs(ts) + [dspec, dspec],
        out_specs=[row(MLA_HEADS * MLA_QK), row(MLA_HEADS * MLA_NOPE), row(LANE), vec, vec, vec, vec],
        out_shape=[jax.ShapeDtypeStruct((s, MLA_HEADS * MLA_QK), BF16), jax.ShapeDtypeStruct((s, MLA_HEADS * MLA_NOPE), BF16),
                   jax.ShapeDtypeStruct((s, LANE), F32), vsh, vsh, vsh, vsh],
        compiler_params=_cp("arbitrary"),
    )(qraw, kvraw, proj, cos4, sin4, gqn, gqr, gkn, gkr, dq, dk)


ATT_T = 1024
ATT_T_FWD = 2048
ATT_SCALE = MLA_QK ** -0.5


def _attn_fwd(q, k, kvraw, *, name):
    nh, s, _ = q.shape
    t = _pick(s, ATT_T_FWD, LANE)
    nb = s // t

    def body(q_ref, k_ref, v_ref, o_ref, lse_ref, m_ref, l_ref, acc_ref):
        i, j = pl.program_id(1), pl.program_id(2)

        @pl.when(j == 0)
        def _():
            m_ref[...] = jnp.full_like(m_ref, -jnp.inf)
            l_ref[...] = jnp.zeros_like(l_ref)
            acc_ref[...] = jnp.zeros_like(acc_ref)

        def step(diagonal):
            sc = _dot_nt(q_ref[0], k_ref[0])
            if diagonal:
                ri = lax.broadcasted_iota(jnp.int32, (t, t), 0)
                cj = lax.broadcasted_iota(jnp.int32, (t, t), 1)
                sc = jnp.where(ri >= cj, sc, -jnp.inf)
            m_new = jnp.maximum(m_ref[...], jnp.max(sc, axis=-1, keepdims=True))
            alpha = jnp.exp(m_ref[...] - m_new)
            p = jnp.exp(sc - m_new)
            l_ref[...] = alpha * l_ref[...] + jnp.sum(p, axis=-1, keepdims=True)
            acc_ref[...] = alpha * acc_ref[...] + _dot(p.astype(BF16), v_ref[...].astype(BF16))
            m_ref[...] = m_new

        @pl.when(j < i)
        def _():
            step(False)

        @pl.when(j == i)
        def _():
            step(True)
            o_ref[...] = acc_ref[...] / l_ref[...]
            lse_ref[0] = m_ref[...] + jnp.log(l_ref[...])

    return pl.pallas_call(
        body, name=name, grid=(nh, nb, nb),
        in_specs=[pl.BlockSpec((1, t, QK_PAD), lambda h, i, j: (h, i, 0)),
                  pl.BlockSpec((1, t, QK_PAD), lambda h, i, j: (h, jnp.minimum(j, i), 0)),
                  pl.BlockSpec((t, MLA_V), lambda h, i, j: (jnp.minimum(j, i), nh + h))],
        out_specs=[pl.BlockSpec((t, MLA_V), lambda h, i, j: (i, h)),
                   pl.BlockSpec((1, t, 1), lambda h, i, j: (h, i, 0))],
        out_shape=[jax.ShapeDtypeStruct((s, nh * MLA_V), F32), jax.ShapeDtypeStruct((nh, s, 1), F32)],
        scratch_shapes=[pltpu.VMEM((t, 1), F32), pltpu.VMEM((t, 1), F32), pltpu.VMEM((t, MLA_V), F32)],
        compiler_params=_cp("parallel", "parallel", "arbitrary"),
    )(q, k, kvraw)


def _attn_bwd(q, k, kvraw, o, lse, do, *, name):
    nh, s, _ = q.shape
    t = _pick(s, ATT_T, LANE)
    nb = s // t

    def body(q_ref, k_ref, v_ref, o_ref, lse_ref, do_ref, dq_ref, dk_ref, dv_ref, dk_acc, dv_acc):
        j, i = pl.program_id(1), pl.program_id(2)

        @pl.when(i == 0)
        def _():
            dk_acc[...] = jnp.zeros_like(dk_acc)
            dv_acc[...] = jnp.zeros_like(dv_acc)

        def step(diagonal):
            qv, kv = q_ref[0], k_ref[0]
            sc = _dot_nt(qv, kv)
            if diagonal:
                ri = lax.broadcasted_iota(jnp.int32, (t, t), 0)
                cj = lax.broadcasted_iota(jnp.int32, (t, t), 1)
                sc = jnp.where(ri >= cj, sc, -jnp.inf)
            p = jnp.exp(sc - lse_ref[0])
            dov = do_ref[...]
            delta = jnp.sum(dov * o_ref[...], axis=-1, keepdims=True)
            do_b = dov.astype(BF16)
            dv_acc[...] += _dot_tn(p.astype(BF16), do_b)
            dp = _dot_nt(do_b, v_ref[...].astype(BF16))
            ds_b = (p * (dp - delta)).astype(BF16)
            dk_acc[...] += _dot_tn(ds_b, qv)
            dq_part = _dot(ds_b, kv) * ATT_SCALE
            rows = pl.ds(pl.multiple_of(i * t, t), t)

            @pl.when(j == 0)
            def _():
                dq_ref[0, rows, :] = dq_part

            @pl.when(j > 0)
            def _():
                dq_ref[0, rows, :] += dq_part

        @pl.when(i > j)
        def _():
            step(False)

        @pl.when(i == j)
        def _():
            step(True)

        @pl.when(i == nb - 1)
        def _():
            dk_ref[0] = dk_acc[...]
            dv_ref[...] = dv_acc[...].astype(dv_ref.dtype)

    qi = lambda h, j, i: jnp.maximum(i, j)
    return pl.pallas_call(
        body, name=name, grid=(nh, nb, nb),
        in_specs=[pl.BlockSpec((1, t, QK_PAD), lambda h, j, i: (h, qi(h, j, i), 0)),
                  pl.BlockSpec((1, t, QK_PAD), lambda h, j, i: (h, j, 0)),
                  pl.BlockSpec((t, MLA_V), lambda h, j, i: (j, nh + h)),
                  pl.BlockSpec((t, MLA_V), lambda h, j, i: (qi(h, j, i), h)),
                  pl.BlockSpec((1, t, 1), lambda h, j, i: (h, qi(h, j, i), 0)),
                  pl.BlockSpec((t, MLA_V), lambda h, j, i: (qi(h, j, i), h))],
        out_specs=[pl.BlockSpec((1, s, QK_PAD), lambda h, j, i: (h, 0, 0)),
                   pl.BlockSpec((1, t, QK_PAD), lambda h, j, i: (h, j, 0)),
                   pl.BlockSpec((t, MLA_V), lambda h, j, i: (j, h))],
        out_shape=[jax.ShapeDtypeStruct((nh, s, QK_PAD), F32), jax.ShapeDtypeStruct((nh, s, QK_PAD), F32),
                   jax.ShapeDtypeStruct((s, nh * MLA_V), BF16)],
        scratch_shapes=[pltpu.VMEM((t, QK_PAD), F32), pltpu.VMEM((t, MLA_V), F32)],
        compiler_params=_cp("parallel", "arbitrary", "arbitrary"),
    )(q, k, kvraw, o, lse, do)


def _ffn_fwd(h, w, tag):
    n = _rms_fwd(h, w["ln"], name=tag + "_norm")
    act, gate, up = _ffn_up(n, w["w13"], name=tag + "_up")
    out = _matmul(act, w["w2"], "nn", name=tag + "_down", scale=0.5, res=h)
    return out, (h, n, gate, up, act)


def _ffn_bwd(dout, saved, w, tag):
    h, n, gate, up, act = saved
    dact = _matmul(dout, w["w2"], "nt", name=tag + "_down_dx", scale=0.5, out_dtype=BF16)
    dw2 = _matmul(act, dout, "tn", name=tag + "_down_dw", scale=0.5)
    dgu = _swiglu_bwd(gate, up, dact, name=tag + "_act_bwd")
    dw13 = _matmul(n, dgu, "tn", name=tag + "_up_dw")
    dn = _matmul(dgu, w["w13"], "nt", name=tag + "_up_dx")
    dh, dln = _rms_bwd(h, w["ln"], dn, name=tag + "_norm_bwd", res=dout)
    return dh, dict(ln=dln, w13=dw13, w2=dw2)


def _mixer_fwd(h, w, rope, tag):
    cos4, sin4 = rope
    u = _rms_fwd(h, w["ln_mix"], name=tag + "_norm")
    proj = _matmul(u, w["w_in"], "nn", name=tag + "_in")
    xc = _conv_fwd(proj, w["conv_w"], w["conv_b"], name=tag + "_conv")
    dtr = proj[:, OFF_KRDT + MLA_ROPE:OFF_KRDT + MLA_ROPE + SSD_HEADS]
    dtrT = dtr.T
    y, states = _ssd_fwd(xc, dtr, dtrT, *w["ssd_aux"], name=tag + "_ssd")
    yn = _gated_rms_fwd(y, proj, w["ssd_norm"], name=tag + "_ssd_norm")
    y_ssd = _matmul(yn, w["w_ssd_out"], "nn", name=tag + "_ssd_out")
    cqn = _rms_fwd(proj, w["q_lora_norm"], name=tag + "_q_lora_norm", col=OFF_CQ // MLA_Q_LORA, width=MLA_Q_LORA)
    qraw = _matmul(cqn, w["w_uq"], "nn", name=tag + "_uq")
    ckvn = _rms_fwd(proj, w["kv_lora_norm"], name=tag + "_kv_lora_norm", col=OFF_CKV // MLA_KV_LORA, width=MLA_KV_LORA)
    kvraw = _matmul(ckvn, w["w_ukv"], "nn", name=tag + "_ukv")
    qf, kf = _mla_prep_fwd(qraw, kvraw, proj, cos4, sin4, *w["qk_gains"], name=tag + "_qk_prep")
    o, lse = _attn_fwd(qf, kf, kvraw, name=tag + "_attn")
    y_mla = _matmul(o, w["w_mla_out"], "nn", name=tag + "_mla_out")
    merged = _merge_fwd(proj, y_ssd, y_mla, name=tag + "_merge")
    out = _matmul(merged, w["w_o"], "nn", name=tag + "_o", res=h)
    saved = dict(h=h, u=u, proj=proj, xc=xc, dtr=dtr, dtrT=dtrT, states=states, y=y, yn=yn, y_ssd=y_ssd, cqn=cqn,
                 qraw=qraw, ckvn=ckvn, kvraw=kvraw, qf=qf, kf=kf, o=o, lse=lse, y_mla=y_mla, merged=merged)
    return out, saved


def _mixer_bwd(dout, s, w, rope, tag):
    cos4, sin4 = rope
    g = {}
    proj = s["proj"]
    dmerged = _matmul(dout, w["w_o"], "nt", name=tag + "_o_dx")
    g["w_o"] = _matmul(s["merged"], dout, "tn", name=tag + "_o_dw")
    dgates, dy_ssd, dy_mla = _merge_bwd(proj, s["y_ssd"], s["y_mla"], dmerged, name=tag + "_merge_bwd")
    do = _matmul(dy_mla, w["w_mla_out"], "nt", name=tag + "_mla_out_dx")
    g["w_mla_out"] = _matmul(s["o"], dy_mla, "tn", name=tag + "_mla_out_dw")
    dqf, dkf, dv = _attn_bwd(s["qf"], s["kf"], s["kvraw"], s["o"], s["lse"], do, name=tag + "_attn_bwd")
    dqraw, dkn, dkrt, dgqn, dgqr, dgkn, dgkr = _mla_prep_bwd(
        s["qraw"], s["kvraw"], proj, cos4, sin4, *w["qk_gains"], dqf, dkf, name=tag + "_qk_prep_bwd")
    g["q_norm"] = jnp.concatenate([dgqn[0], dgqr[0, :MLA_ROPE]])
    g["k_norm"] = jnp.concatenate([dgkn[0], dgkr[0, :MLA_ROPE]])
    dkvraw = jnp.concatenate([dkn, dv], axis=1)
    dcqn = _matmul(dqraw, w["w_uq"], "nt", name=tag + "_uq_dx")
    g["w_uq"] = _matmul(s["cqn"], dqraw, "tn", name=tag + "_uq_dw")
    dckvn = _matmul(dkvraw, w["w_ukv"], "nt", name=tag + "_ukv_dx")
    g["w_ukv"] = _matmul(s["ckvn"], dkvraw, "tn", name=tag + "_ukv_dw")
    dcq, g["q_lora_norm"] = _rms_bwd(proj, w["q_lora_norm"], dcqn, name=tag + "_q_lora_norm_bwd",
                                     col=OFF_CQ // MLA_Q_LORA, width=MLA_Q_LORA, out_dtype=BF16)
    dckv, g["kv_lora_norm"] = _rms_bwd(proj, w["kv_lora_norm"], dckvn, name=tag + "_kv_lora_norm_bwd",
                                       col=OFF_CKV // MLA_KV_LORA, width=MLA_KV_LORA, out_dtype=BF16)
    dyn = _matmul(dy_ssd, w["w_ssd_out"], "nt", name=tag + "_ssd_out_dx")
    g["w_ssd_out"] = _matmul(s["yn"], dy_ssd, "tn", name=tag + "_ssd_out_dw")
    dy, dz, g["ssd_norm"] = _gated_rms_bwd(s["y"], proj, w["ssd_norm"], dyn, name=tag + "_ssd_norm_bwd")
    aux = w["ssd_aux"]
    dxc, ddt_c, ddt_r, dbias_c, dbias_r, da_c, da_r, ddsk = _ssd_bwd(
        s["xc"], s["dtr"], s["dtrT"], *aux, aux[-1].T, s["states"], dy, name=tag + "_ssd_bwd")
    g["dt_bias"] = dbias_c[0] + dbias_r[:, 0]
    g["a_log"] = (da_c[0] + da_r[:, 0]) * (-jnp.exp(aux[2][0]))
    g["d_skip"] = jnp.sum(ddsk.reshape(SSD_HEADS, SSD_HEAD_DIM), axis=1)
    dpre, g["conv_w"], g["conv_b"] = _conv_bwd_pre(proj, w["conv_w"], w["conv_b"], dxc, name=tag + "_conv_bwd_pre")
    dxbc = _conv_bwd_x(dpre, w["conv_w"], name=tag + "_conv_bwd_x")
    ddtr = ddt_c + ddt_r.T
    dkrdt = jnp.concatenate([dkrt[:, :MLA_ROPE], ddtr, jnp.zeros((ddtr.shape[0], LANE - MLA_ROPE - SSD_HEADS), F32)], axis=1)
    dproj = _join_cols([dz, dxbc, dgates, dcq, dckv, dkrdt.astype(BF16)], name=tag + "_dproj")
    du = _matmul(dproj, w["w_in"], "nt", name=tag + "_in_dx")
    g["w_in"] = _matmul(s["u"], dproj, "tn", name=tag + "_in_dw")
    dh, g["ln_mix"] = _rms_bwd(s["h"], w["ln_mix"], du, name=tag + "_norm_bwd", res=dout)
    return dh, g


W_NAMES = ["ln_ffn1", "ffn1_w13", "ffn1_w2", "ln_mix", "w_in", "conv_w", "conv_b", "dt_bias", "a_log", "d_skip",
           "ssd_norm", "w_ssd_out", "q_lora_norm", "w_uq", "kv_lora_norm", "w_ukv", "q_norm", "k_norm", "w_mla_out",
           "w_o", "ln_ffn2", "ffn2_w13", "ffn2_w2"]
SHARD_AXIS = {"ffn1_w13": 2, "ffn1_w2": 1, "w_in": 2, "conv_w": 2, "w_ssd_out": 1, "w_uq": 2, "w_ukv": 2,
              "w_mla_out": 1, "w_o": 1, "ffn2_w13": 2, "ffn2_w2": 1}
SHARDED = [n for n in W_NAMES if n in SHARD_AXIS and n != "conv_w"] + ["conv_w"]
REPLICATED = [n for n in W_NAMES if n not in SHARD_AXIS]
N_CHIPS = 4
N_DEV = 8
PACK_COLS = 1024
IN_SPLIT = (2048, 3072, 32, 512, 256, 64, 2048)


def _pack_mats(arrs, rows, dtype):
    mats = [a.astype(dtype).reshape(-1, PACK_COLS) for a in arrs]
    used = sum(m.shape[0] for m in mats)
    return mats[:-1] + [jnp.concatenate([mats[-1], jnp.zeros((rows - used, PACK_COLS), dtype)], axis=0)]


STAGE_ROWS = 512


def _stack_rows(mats, *, name):
    ncol, dtype = mats[0].shape[1], mats[0].dtype
    total = sum(m.shape[0] for m in mats)
    chunks, at = [], 0
    for i, m in enumerate(mats):
        for st in range(0, m.shape[0], STAGE_ROWS):
            sz = min(STAGE_ROWS, m.shape[0] - st)
            chunks.append((i, st, at + st, sz))
        at += m.shape[0]
    n = len(mats)

    def body(*refs):
        ins, out_ref, buf, sem_in, sem_out = refs[:n], refs[n], refs[n + 1], refs[n + 2], refs[n + 3]

        def put(idx):
            _, _, dst, sz = chunks[idx]
            return pltpu.make_async_copy(buf.at[idx % 2, pl.ds(0, sz), :], out_ref.at[pl.ds(dst, sz), :], sem_out.at[idx % 2])

        for idx, (i, st, _, sz) in enumerate(chunks):
            if idx >= 2:
                put(idx - 2).wait()
            get = pltpu.make_async_copy(ins[i].at[pl.ds(st, sz), :], buf.at[idx % 2, pl.ds(0, sz), :], sem_in.at[idx % 2])
            get.start()
            get.wait()
            put(idx).start()
        for idx in range(max(0, len(chunks) - 2), len(chunks)):
            put(idx).wait()

    return pl.pallas_call(
        body, name=name, out_shape=jax.ShapeDtypeStruct((total, ncol), dtype),
        in_specs=[ANY] * n, out_specs=ANY,
        scratch_shapes=[pltpu.VMEM((2, STAGE_ROWS, ncol), dtype), pltpu.SemaphoreType.DMA((2,)), pltpu.SemaphoreType.DMA((2,))],
    )(*mats)


def _join_cols(pieces, *, name):
    s, dtype = pieces[0].shape[0], pieces[0].dtype
    widths = [p.shape[1] for p in pieces]
    tr = _pick(s, 256, 16)

    def body(*refs):
        o_ref, at = refs[-1], 0
        for ref, w in zip(refs[:-1], widths):
            o_ref[:, at:at + w] = ref[...]
            at += w

    return pl.pallas_call(
        body, name=name, grid=(s // tr,),
        in_specs=[pl.BlockSpec((tr, w), lambda i: (i, 0)) for w in widths],
        out_specs=pl.BlockSpec((tr, sum(widths)), lambda i: (i, 0)),
        out_shape=jax.ShapeDtypeStruct((s, sum(widths)), dtype),
        compiler_params=_cp("parallel"),
    )(*pieces)


def _pack(arrs, rows, dtype, *, name):
    return _stack_rows(_pack_mats(arrs, rows, dtype), name=name)


def _unpack(packed, shapes):
    out, at = [], 0
    for sh in shapes:
        r = math.prod(sh) // PACK_COLS
        out.append(packed[at:at + r].reshape(sh))
        at += r
    return out


def _unpack_flat(flat, shapes):
    out, at = [], 0
    for sh in shapes:
        n = math.prod(sh)
        out.append(flat[at:at + n].reshape(sh))
        at += n
    return out


def _pack_rows(shapes):
    n = sum(math.prod(sh) for sh in shapes)
    return -(-n // (PACK_COLS * 1024)) * 1024


def _in_perm(w_in):
    z, xbc, dt, cq, ckv, kr, gates = jnp.split(w_in, list(np_cumsum(IN_SPLIT))[:-1], axis=1)
    return jnp.concatenate([z, xbc, gates, cq, ckv, kr, dt, jnp.zeros((w_in.shape[0], PROJ_W - sum(IN_SPLIT)), w_in.dtype)], axis=1)


def _in_unperm(g):
    z, xbc, gates, cq, ckv = (g[:, OFF_Z:OFF_XBC], g[:, OFF_XBC:OFF_GATES], g[:, OFF_GATES:OFF_CQ], g[:, OFF_CQ:OFF_CKV],
                              g[:, OFF_CKV:OFF_KRDT])
    kr = g[:, OFF_KRDT:OFF_KRDT + MLA_ROPE]
    dt = g[:, OFF_KRDT + MLA_ROPE:OFF_KRDT + MLA_ROPE + SSD_HEADS]
    return jnp.concatenate([z, xbc, dt, cq, ckv, kr, gates], axis=1)


def np_cumsum(sizes):
    out, t = [], 0
    for s in sizes:
        t += s
        out.append(t)
    return out


def _head_perm(w, first):
    r = w.shape[0]
    w3 = w.reshape(r, MLA_HEADS, -1)
    return jnp.concatenate([w3[:, :, :first].reshape(r, -1), w3[:, :, first:].reshape(r, -1)], axis=1)


def _head_unperm(g, first):
    r = g.shape[0]
    rest = g.shape[1] // MLA_HEADS - first
    a = g[:, :MLA_HEADS * first].reshape(r, MLA_HEADS, first)
    b = g[:, MLA_HEADS * first:].reshape(r, MLA_HEADS, rest)
    return jnp.concatenate([a, b], axis=2).reshape(r, -1)


def _layer_weights(full, l):
    row = lambda n: full[n][l][None].astype(F32)
    expand = jnp.repeat(jnp.eye(SSD_HEADS, dtype=F32), SSD_HEAD_DIM, axis=1)
    dtb, al, dsk = full["dt_bias"][l], full["a_log"][l], full["d_skip"][l]
    mixer = dict(
        ln_mix=row("ln_mix"), w_in=_in_perm(full["w_in"][l]), conv_w=full["conv_w"][l], conv_b=row("conv_b"),
        ssd_aux=(dtb[None], dtb[:, None], al[None], al[:, None], jnp.repeat(dsk, SSD_HEAD_DIM)[None], expand),
        ssd_norm=row("ssd_norm"), w_ssd_out=full["w_ssd_out"][l],
        q_lora_norm=row("q_lora_norm"), w_uq=_head_perm(full["w_uq"][l], MLA_NOPE),
        kv_lora_norm=row("kv_lora_norm"), w_ukv=_head_perm(full["w_ukv"][l], MLA_NOPE),
        qk_gains=_mla_gains(full["q_norm"][l], full["k_norm"][l]),
        w_mla_out=full["w_mla_out"][l], w_o=full["w_o"][l])
    ffn1 = dict(ln=row("ln_ffn1"), w13=full["ffn1_w13"][l], w2=full["ffn1_w2"][l])
    ffn2 = dict(ln=row("ln_ffn2"), w13=full["ffn2_w13"][l], w2=full["ffn2_w2"][l])
    return ffn1, mixer, ffn2


def _layer_grads(g1, gm, g2):
    return {
        "ln_ffn1": g1["ln"][0], "ffn1_w13": g1["w13"], "ffn1_w2": g1["w2"],
        "ln_mix": gm["ln_mix"][0], "w_in": _in_unperm(gm["w_in"]), "conv_w": gm["conv_w"], "conv_b": gm["conv_b"][0],
        "dt_bias": gm["dt_bias"], "a_log": gm["a_log"], "d_skip": gm["d_skip"], "ssd_norm": gm["ssd_norm"][0],
        "w_ssd_out": gm["w_ssd_out"], "q_lora_norm": gm["q_lora_norm"][0], "w_uq": _head_unperm(gm["w_uq"], MLA_NOPE),
        "kv_lora_norm": gm["kv_lora_norm"][0], "w_ukv": _head_unperm(gm["w_ukv"], MLA_NOPE),
        "q_norm": gm["q_norm"], "k_norm": gm["k_norm"], "w_mla_out": gm["w_mla_out"], "w_o": gm["w_o"],
        "ln_ffn2": g2["ln"][0], "ffn2_w13": g2["w13"], "ffn2_w2": g2["w2"],
    }


def _local_step(x, positions, loss_target, full):
    rope = _rope_tables4(positions)
    lw = [_layer_weights(full, l) for l in range(DEPTH)]
    h = x
    saved = []
    for l in range(DEPTH):
        f1, mx, f2 = lw[l]
        h, s1 = _ffn_fwd(h, f1, f"l{l}_ffn1")
        h, sm = _mixer_fwd(h, mx, rope, f"l{l}_mix")
        h, s2 = _ffn_fwd(h, f2, f"l{l}_ffn2")
        saved.append((s1, sm, s2))
    loss_part, dh = _loss_fwd_bwd(h, loss_target, name="loss")
    grads = [None] * DEPTH
    for l in reversed(range(DEPTH)):
        f1, mx, f2 = lw[l]
        s1, sm, s2 = saved[l]
        dh, g2 = _ffn_bwd(dh, s2, f2, f"l{l}_ffn2")
        dh, gm = _mixer_bwd(dh, sm, mx, rope, f"l{l}_mix")
        dh, g1 = _ffn_bwd(dh, s1, f1, f"l{l}_ffn1")
        grads[l] = _layer_grads(g1, gm, g2)
    full_grads = {n: jnp.stack([grads[l][n] for l in range(DEPTH)]) for n in W_NAMES}
    return loss_part, dh, full_grads


MESH = pl.DeviceIdType.MESH
ANY = pl.BlockSpec(memory_space=pl.ANY)


def _place():
    return lax.axis_index("x"), lax.axis_index("y"), lax.axis_index("c")


def _other_chips(x, y):
    return [(1 - x, y), (x, 1 - y), (1 - x, 1 - y)]


def _remote(src, dst, send_sems, recv_sems, k, to):
    return pltpu.make_async_remote_copy(src_ref=src, dst_ref=dst, send_sem=send_sems.at[k], recv_sem=recv_sems.at[k],
                                        device_id=to, device_id_type=MESH)


N_PARTS = 8


def _parts(rows):
    size = rows // N_PARTS
    assert size * N_PARTS == rows and size % 16 == 0, rows
    return [(p * size, size) for p in range(N_PARTS)]


def _rows(ref, lead, base, start, size):
    return ref.at[(*lead, pl.ds(pl.multiple_of(base + start, 16), size), slice(None))]


def _my_chip():
    return 2 * lax.axis_index("x") + lax.axis_index("y")


def _own_slot(packed, *, name):
    r, ncol = packed.shape
    tr = _pick(r, 512, 16)

    def body(x_ref, o_ref):
        o_ref[...] = x_ref[...]

    return pl.pallas_call(
        body, name=name, grid=(r // tr,),
        in_specs=[pl.BlockSpec((tr, ncol), lambda i: (i, 0))],
        out_specs=pl.BlockSpec((None, tr, ncol), lambda i: (_my_chip(), i, 0)),
        out_shape=jax.ShapeDtypeStruct((N_CHIPS, r, ncol), packed.dtype),
        compiler_params=_cp("arbitrary"),
    )(packed)


def _gather_shards(packed, slots, *, name):
    r, ncol = packed.shape
    hr = r // 2
    parts = _parts(hr)

    def body(x_ref, slots_ref, out_ref, send_sems, recv_sems):
        del slots_ref
        x, y, c = _place()
        chips = _other_chips(x, y)
        me = 2 * x + y

        def half(chip, cc):
            return _rows(out_ref, (2 * chip[0] + chip[1],), cc * hr, 0, hr)

        for j, chip in enumerate(chips):
            for st, sz in parts:
                _remote(_rows(x_ref, (), c * hr, st, sz), _rows(out_ref, (me,), c * hr, st, sz), send_sems, recv_sems, j,
                        (*chip, c)).start()
        for j, chip in enumerate(chips):
            _remote(half(chip, c), half(chip, c), send_sems, recv_sems, j, (x, y, c)).wait_recv()
            slot = 2 * chip[0] + chip[1]
            for st, sz in parts:
                _remote(_rows(out_ref, (slot,), c * hr, st, sz), _rows(out_ref, (slot,), c * hr, st, sz), send_sems,
                        recv_sems, 3 + j, (x, y, 1 - c)).start()
        for j, chip in enumerate(chips):
            _remote(half(chip, 1 - c), half(chip, 1 - c), send_sems, recv_sems, 3 + j, (x, y, c)).wait_recv()
        for k in range(6):
            _remote(half((x, y), c), half((x, y), c), send_sems, recv_sems, k, (x, y, c)).wait_send()

    return pl.pallas_call(
        body, name=name,
        out_shape=jax.ShapeDtypeStruct((N_CHIPS, r, ncol), packed.dtype),
        in_specs=[ANY, ANY], out_specs=ANY, input_output_aliases={1: 0},
        scratch_shapes=[pltpu.SemaphoreType.DMA((6,)), pltpu.SemaphoreType.DMA((6,))],
    )(packed, slots)


def _swap_halves(g, *, name):
    n, r, ncol = g.shape
    hr = r // 2
    parts = _parts(hr)

    def body(g_ref, got_ref, send_sems, recv_sems):
        x, y, c = _place()
        for s in range(n):
            for st, sz in parts:
                _remote(_rows(g_ref, (s,), (1 - c) * hr, st, sz), got_ref.at[s, pl.ds(st, sz), :], send_sems, recv_sems, 0,
                        (x, y, 1 - c)).start()
        _remote(got_ref, got_ref, send_sems, recv_sems, 0, (x, y, c)).wait()

    return pl.pallas_call(
        body, name=name, out_shape=jax.ShapeDtypeStruct((n, hr, ncol), g.dtype), in_specs=[ANY], out_specs=ANY,
        scratch_shapes=[pltpu.SemaphoreType.DMA((1,)), pltpu.SemaphoreType.DMA((1,))],
    )(g)


def _add_cores(g, got, *, name):
    n, r, ncol = g.shape
    hr = r // 2
    tr = _pick(hr, 512, 16)
    nb = hr // tr

    def body(a_ref, b_ref, o_ref):
        o_ref[...] = (a_ref[...].astype(F32) + b_ref[...].astype(F32)).astype(o_ref.dtype)

    blk = pl.BlockSpec((None, tr, ncol), lambda s, i: (s, i, 0))
    return pl.pallas_call(
        body, name=name, grid=(n, nb),
        in_specs=[pl.BlockSpec((None, tr, ncol), lambda s, i: (s, lax.axis_index("c") * nb + i, 0)), blk],
        out_specs=blk,
        out_shape=jax.ShapeDtypeStruct((n, hr, ncol), BF16),
        compiler_params=_cp("parallel", "parallel"),
    )(g, got)


def _scatter_to_chips(a, *, name):
    n, r, ncol = a.shape
    parts = _parts(r)

    def body(a_ref, got_ref, send_sems, recv_sems):
        x, y, c = _place()
        for st, sz in parts:
            for j, chip in enumerate(_other_chips(x, y)):
                _remote(a_ref.at[2 * chip[0] + chip[1], pl.ds(st, sz), :], got_ref.at[j, pl.ds(st, sz), :], send_sems,
                        recv_sems, j, (*chip, c)).start()
        for j in range(n - 1):
            _remote(got_ref.at[j], got_ref.at[j], send_sems, recv_sems, j, (x, y, c)).wait()

    return pl.pallas_call(
        body, name=name, out_shape=jax.ShapeDtypeStruct((n - 1, r, ncol), a.dtype), in_specs=[ANY], out_specs=ANY,
        scratch_shapes=[pltpu.SemaphoreType.DMA((3,)), pltpu.SemaphoreType.DMA((3,))],
    )(a)


def _add_chips(a, got, *, name):
    n, hr, ncol = a.shape
    tr = _pick(hr, 512, 16)
    nb = hr // tr

    def body(a_ref, g0_ref, g1_ref, g2_ref, o_ref):
        f = lambda ref: ref[...].astype(F32)
        o_ref[...] = ((f(a_ref) + f(g0_ref)) + f(g1_ref)) + f(g2_ref)

    other = lambda j: pl.BlockSpec((None, tr, ncol), lambda i: (j, i, 0))
    return pl.pallas_call(
        body, name=name, grid=(nb,),
        in_specs=[pl.BlockSpec((None, tr, ncol), lambda i: (_my_chip(), i, 0)), other(0), other(1), other(2)],
        out_specs=pl.BlockSpec((tr, ncol), lambda i: (lax.axis_index("c") * nb + i, 0)),
        out_shape=jax.ShapeDtypeStruct((2 * hr, ncol), F32),
        compiler_params=_cp("parallel"),
    )(a, got, got, got)


def _join_halves(buf, *, name):
    r, ncol = buf.shape
    hr = r // 2
    parts = _parts(hr)

    def body(b_ref, out_ref, send_sems, recv_sems):
        del b_ref
        x, y, c = _place()
        for st, sz in parts:
            _remote(_rows(out_ref, (), c * hr, st, sz), _rows(out_ref, (), c * hr, st, sz), send_sems, recv_sems, 0,
                    (x, y, 1 - c)).start()
        theirs = _rows(out_ref, (), (1 - c) * hr, 0, hr)
        _remote(theirs, theirs, send_sems, recv_sems, 0, (x, y, c)).wait()

    return pl.pallas_call(
        body, name=name, out_shape=jax.ShapeDtypeStruct((r, ncol), buf.dtype), in_specs=[ANY], out_specs=ANY,
        input_output_aliases={0: 0},
        scratch_shapes=[pltpu.SemaphoreType.DMA((1,)), pltpu.SemaphoreType.DMA((1,))],
    )(buf)


def _reduce_scatter(g, *, name):
    got = _swap_halves(g, name=name + "_swap")
    chip_sum = _add_cores(g, got, name=name + "_add_cores")
    others = _scatter_to_chips(chip_sum, name=name + "_scatter")
    return _join_halves(_add_chips(chip_sum, others, name=name + "_add_chips"), name=name + "_join")


def _all_gather_small(v, *, name):
    r, ncol = v.shape

    def body(x_ref, out_ref, send_sems, recv_sems, local_sem):
        x, y, c = _place()
        me, sibling = (x, y, c), (x, y, 1 - c)
        chips = _other_chips(x, y)

        def slot(p):
            return out_ref.at[4 * p[0] + 2 * p[1] + p[2]]

        mine = pltpu.make_async_copy(x_ref, slot(me), local_sem.at[0])
        mine.start()
        first = [_remote(x_ref, slot(me), send_sems, recv_sems, 0, sibling)]
        first += [_remote(x_ref, slot(me), send_sems, recv_sems, 1 + j, (*chip, c)) for j, chip in enumerate(chips)]
        for cp in first:
            cp.start()
        passed = [_remote(slot((*chip, c)), slot((*chip, c)), send_sems, recv_sems, 4 + j, sibling)
                  for j, chip in enumerate(chips)]
        for j, chip in enumerate(chips):
            _remote(slot((*chip, c)), slot((*chip, c)), send_sems, recv_sems, 1 + j, me).wait_recv()
            passed[j].start()
        _remote(slot(sibling), slot(sibling), send_sems, recv_sems, 0, me).wait_recv()
        for j, chip in enumerate(chips):
            _remote(slot((*chip, 1 - c)), slot((*chip, 1 - c)), send_sems, recv_sems, 4 + j, me).wait_recv()
        for cp in first + passed:
            cp.wait_send()
        mine.wait()

    vm = pl.BlockSpec(memory_space=pltpu.VMEM)
    return pl.pallas_call(
        body, name=name, out_shape=jax.ShapeDtypeStruct((N_DEV, r, ncol), v.dtype), in_specs=[vm], out_specs=vm,
        scratch_shapes=[pltpu.SemaphoreType.DMA((7,)), pltpu.SemaphoreType.DMA((7,)), pltpu.SemaphoreType.DMA((1,))],
    )(v)


def _sum_slots(g8, *, name):
    n, r, ncol = g8.shape

    def body(g_ref, o_ref):
        acc = g_ref[0]
        for k in range(1, n):
            acc = acc + g_ref[k]
        o_ref[...] = acc

    return pl.pallas_call(body, name=name, out_shape=jax.ShapeDtypeStruct((r, ncol), g8.dtype))(g8)


def _step(a):
    x = a["x"][0]
    s = x.shape[0]
    del s
    shard_shapes = [a[n].shape for n in SHARDED]
    rows = _pack_rows(shard_shapes)

    packed = _pack([a[n] for n in SHARDED], rows, BF16, name="pack_weights")
    gathered = _gather_shards(packed, _own_slot(packed, name="own_weights"), name="gather_weights")
    conv_rows = -(-math.prod(a["conv_w"].shape) // (LANE * 8)) * 8
    conv_all = _all_gather_small(
        jnp.pad(a["conv_w"].reshape(-1), (0, conv_rows * LANE - math.prod(a["conv_w"].shape))).reshape(conv_rows, LANE),
        name="gather_conv_w")
    per_chip = [dict(zip(SHARDED, _unpack(gathered[k], shard_shapes))) for k in range(N_CHIPS)]
    full = {n: jnp.concatenate([per_chip[k][n] for k in range(N_CHIPS)], axis=SHARD_AXIS[n]) for n in SHARDED}
    full["conv_w"] = jnp.concatenate(
        [conv_all[2 * k].reshape(-1)[:math.prod(a["conv_w"].shape)].reshape(a["conv_w"].shape) for k in range(N_CHIPS)],
        axis=SHARD_AXIS["conv_w"])
    for n in REPLICATED:
        full[n] = a[n]

    loss_part, grad_x, grads = _local_step(x, a["positions"][0], a["loss_target"][0], full)
    loss = lax.psum(jnp.sum(loss_part), ("x", "y", "c"))

    mats = []
    for k in range(N_CHIPS):
        parts = [jnp.split(grads[n], N_CHIPS, axis=SHARD_AXIS[n])[k] for n in SHARDED]
        mats += _pack_mats(parts, rows, BF16)
    g_slots = _stack_rows(mats, name="pack_grads").reshape(N_CHIPS, rows, PACK_COLS)
    g_shard = _reduce_scatter(g_slots, name="reduce_grads")

    rep_shapes = [a[n].shape for n in REPLICATED]
    n_rep = sum(math.prod(sh) for sh in rep_shapes)
    rep_rows = -(-n_rep // (LANE * 8)) * 8
    pack_small = lambda arrs: jnp.pad(jnp.concatenate([t.reshape(-1) for t in arrs]), (0, rep_rows * LANE - n_rep)).reshape(rep_rows, LANE)
    g_rep = _sum_slots(_all_gather_small(pack_small([grads[n] for n in REPLICATED]), name="gather_small_grads"),
                       name="add_small_grads")

    out = {"loss": loss, "grad_x": grad_x[None]}
    for n, g in zip(SHARDED, _unpack(g_shard, shard_shapes)):
        flat = lambda t: t.reshape(-1, t.shape[-1])
        d, nm, nv = _adamw(flat(a[n]), flat(g), flat(a["m_" + n]), flat(a["v_" + n]), name="adamw_" + n)
        out["grad_" + n] = g
        out["delta_" + n], out["new_m_" + n], out["new_v_" + n] = (t.reshape(g.shape) for t in (d, nm, nv))
    d_rp, m_rp, v_rp = _adamw(pack_small([a[n] for n in REPLICATED]), g_rep,
                              pack_small([a["m_" + n] for n in REPLICATED]),
                              pack_small([a["v_" + n] for n in REPLICATED]), name="adamw_replicated")
    for prefix, rp_arr in (("grad_", g_rep), ("delta_", d_rp), ("new_m_", m_rp), ("new_v_", v_rp)):
        for n, t in zip(REPLICATED, _unpack_flat(rp_arr.reshape(-1)[:n_rep], rep_shapes)):
            out[prefix + n] = t
    return out


IN_NAMES = ["x", "positions"] + W_NAMES + ["loss_target"] + ["m_" + n for n in W_NAMES] + ["v_" + n for n in W_NAMES]
OUT_NAMES = (["loss", "grad_x"] + ["grad_" + n for n in W_NAMES] + ["delta_" + n for n in W_NAMES]
             + ["new_m_" + n for n in W_NAMES] + ["new_v_" + n for n in W_NAMES])


def kernel(x, positions, ln_ffn1, ffn1_w13, ffn1_w2, ln_mix, w_in, conv_w, conv_b, dt_bias, a_log, d_skip, ssd_norm, w_ssd_out, q_lora_norm, w_uq, kv_lora_norm, w_ukv, q_norm, k_norm, w_mla_out, w_o, ln_ffn2, ffn2_w13, ffn2_w2, loss_target, m_ln_ffn1, m_ffn1_w13, m_ffn1_w2, m_ln_mix, m_w_in, m_conv_w, m_conv_b, m_dt_bias, m_a_log, m_d_skip, m_ssd_norm, m_w_ssd_out, m_q_lora_norm, m_w_uq, m_kv_lora_norm, m_w_ukv, m_q_norm, m_k_norm, m_w_mla_out, m_w_o, m_ln_ffn2, m_ffn2_w13, m_ffn2_w2, v_ln_ffn1, v_ffn1_w13, v_ffn1_w2, v_ln_mix, v_w_in, v_conv_w, v_conv_b, v_dt_bias, v_a_log, v_d_skip, v_ssd_norm, v_w_ssd_out, v_q_lora_norm, v_w_uq, v_kv_lora_norm, v_w_ukv, v_q_norm, v_k_norm, v_w_mla_out, v_w_o, v_ln_ffn2, v_ffn2_w13, v_ffn2_w2):
    given = locals()
    out = _step({n: given[n] for n in IN_NAMES})
    return tuple(out[n] for n in OUT_NAMES)
```

```python
import functools
import math

import jax
import jax.numpy as jnp
from jax import lax
from jax.experimental import pallas as pl
from jax.experimental.pallas import tpu as pltpu

F32 = jnp.float32
BF16 = jnp.bfloat16

D_MODEL = 1024
DEPTH = 2
D_FF = 2816
SSD_D_INNER = 2048
SSD_HEADS = 32
SSD_HEAD_DIM = 64
SSD_GROUPS = 4
SSD_STATE = 128
SSD_CHUNK = 128
SSD_CONV = 4
SSD_CONV_DIM = 3072
MLA_HEADS = 8
MLA_Q_LORA = 512
MLA_KV_LORA = 256
MLA_NOPE = 128
MLA_ROPE = 64
MLA_V = 128
MLA_QK = 192
ROPE_THETA = 10000.0
EPS = 1e-6
ADAM_LR = 0.001
ADAM_B1 = 0.9
ADAM_B2 = 0.999
ADAM_EPS = 1e-08
ADAM_WD = 0.01
ADAM_STEP = 10

PROJ_W = 8064
OFF_Z, OFF_XBC, OFF_GATES, OFF_CQ, OFF_CKV, OFF_KRDT = 0, 2048, 5120, 7168, 7680, 7936

LANE = 128
VMEM_LIMIT = 48 * 1024 * 1024
HI = lax.Precision.HIGHEST


def _cp(*sem):
    return pltpu.CompilerParams(dimension_semantics=sem, vmem_limit_bytes=VMEM_LIMIT)


def _pick(dim, target, align):
    if dim <= target:
        return dim
    b = (target // align) * align
    while b >= align:
        if dim % b == 0:
            return b
        b -= align
    raise ValueError(f"no block for {dim} (target {target}, align {align})")


def _silu(x):
    return x * jax.nn.sigmoid(x)


def _dsilu(x):
    s = jax.nn.sigmoid(x)
    return s * (1.0 + x * (1.0 - s))


MM_VMEM_BUDGET = 40 * 1024 * 1024


def _mm_tiles(m, n, k, a_bytes, b_bytes, o_bytes):
    bn = _pick(n, 1408, LANE)
    for nk in (1, 2, 3, 4, 6, 7, 8):
        if k % nk or (k // nk) % LANE:
            continue
        bk = k // nk
        for bm in (1024, 512):
            if m % bm:
                continue
            need = 2 * (bm * bk * a_bytes + bk * bn * b_bytes + bm * bn * o_bytes) + (bm * bn * 4 if nk > 1 else 0)
            if need <= MM_VMEM_BUDGET:
                return bm, bn, bk
    return _pick(m, 512, 8), bn, _pick(k, 1536, LANE)

def _matmul(a, b, mode, *, name, out_dtype=F32, scale=1.0, res=None):
    if mode == "nn":
        (m, k), (k2, n) = a.shape, b.shape
    elif mode == "nt":
        (m, k), (n, k2) = a.shape, b.shape
    else:
        (k, m), (k2, n) = a.shape, b.shape
    assert k == k2, (a.shape, b.shape, mode)
    if mode == "tn":
        bn, bk = _pick(n, 2816, LANE), _pick(k, 1024, 8)
        bm = _pick(m, max(256, (1408 * 1024 // bn) // LANE * LANE), LANE)
    else:
        bm, bn, bk = _mm_tiles(m, n, k, a.dtype.itemsize, b.dtype.itemsize,
                               jnp.dtype(out_dtype).itemsize + (4 if res is not None else 0))
    nk = k // bk

    def body(a_ref, b_ref, *rest):
        res_ref = rest[0] if res is not None else None
        o_ref = rest[-2] if nk > 1 else rest[-1]
        kk = pl.program_id(2)
        av = a_ref[...].astype(BF16)
        bv = b_ref[...].astype(BF16)
        if mode == "nn":
            dims = (((1,), (0,)), ((), ()))
        elif mode == "nt":
            dims = (((1,), (1,)), ((), ()))
        else:
            dims = (((0,), (0,)), ((), ()))
        part = lax.dot_general(av, bv, dims, preferred_element_type=F32)

        def finish(total):
            out = total * scale
            if res_ref is not None:
                out = res_ref[...] + out
            o_ref[...] = out.astype(o_ref.dtype)

        if nk == 1:
            finish(part)
            return
        acc_ref = rest[-1]

        @pl.when(kk == 0)
        def _():
            acc_ref[...] = part

        @pl.when((kk > 0) & (kk < nk - 1))
        def _():
            acc_ref[...] += part

        @pl.when(kk == nk - 1)
        def _():
            finish(acc_ref[...] + part)

    o_spec = pl.BlockSpec((bm, bn), lambda i, j, kk: (i, j))
    if mode == "nn":
        a_spec = pl.BlockSpec((bm, bk), lambda i, j, kk: (i, kk))
        b_spec = pl.BlockSpec((bk, bn), lambda i, j, kk: (kk, j))
    elif mode == "nt":
        a_spec = pl.BlockSpec((bm, bk), lambda i, j, kk: (i, kk))
        b_spec = pl.BlockSpec((bn, bk), lambda i, j, kk: (j, kk))
    else:
        a_spec = pl.BlockSpec((bk, bm), lambda i, j, kk: (kk, i))
        b_spec = pl.BlockSpec((bk, bn), lambda i, j, kk: (kk, j))
    return pl.pallas_call(
        body, name=name,
        grid=(m // bm, n // bn, nk),
        in_specs=[a_spec, b_spec] + ([o_spec] if res is not None else []),
        out_specs=o_spec,
        out_shape=jax.ShapeDtypeStruct((m, n), out_dtype),
        scratch_shapes=[pltpu.VMEM((bm, bn), F32)] if nk > 1 else [],
        compiler_params=_cp("parallel", "parallel", "arbitrary"),
    )(*((a, b) + ((res,) if res is not None else ())))


def _rms_fwd(x, g, *, name, col=0, width=None):
    r = x.shape[0]
    w = width or x.shape[1]
    tr = _pick(r, 512, 16)

    def body(x_ref, g_ref, o_ref):
        xv = x_ref[...]
        rs = lax.rsqrt(jnp.mean(xv * xv, axis=-1, keepdims=True) + EPS)
        o_ref[...] = (xv * rs * g_ref[...]).astype(o_ref.dtype)

    return pl.pallas_call(
        body, name=name, grid=(r // tr,),
        in_specs=[pl.BlockSpec((tr, w), lambda i: (i, col)), pl.BlockSpec((1, w), lambda i: (0, 0))],
        out_specs=pl.BlockSpec((tr, w), lambda i: (i, 0)),
        out_shape=jax.ShapeDtypeStruct((r, w), BF16),
        compiler_params=_cp("parallel"),
    )(x, g)


def _rms_bwd(x, g, dy, *, name, col=0, width=None, res=None, out_dtype=F32):
    r = x.shape[0]
    w = width or x.shape[1]
    tr = _pick(r, 512, 8)

    def body(x_ref, g_ref, dy_ref, *rest):
        res_ref = rest[0] if res is not None else None
        dx_ref, dg_ref = rest[-2:]
        i = pl.program_id(0)
        xv = x_ref[...]
        dyv = dy_ref[...]
        rs = lax.rsqrt(jnp.mean(xv * xv, axis=-1, keepdims=True) + EPS)
        xh = xv * rs
        dxh = dyv * g_ref[...]
        mm = jnp.mean(dxh * xh, axis=-1, keepdims=True)
        dx = rs * (dxh - xh * mm)
        if res_ref is not None:
            dx = res_ref[...] + dx
        dx_ref[...] = dx.astype(dx_ref.dtype)
        part = jnp.sum(dyv * xh, axis=0, keepdims=True)

        @pl.when(i == 0)
        def _():
            dg_ref[...] = part

        @pl.when(i > 0)
        def _():
            dg_ref[...] += part

    blk = pl.BlockSpec((tr, w), lambda i: (i, 0))
    return pl.pallas_call(
        body, name=name, grid=(r // tr,),
        in_specs=[pl.BlockSpec((tr, w), lambda i: (i, col)), pl.BlockSpec((1, w), lambda i: (0, 0)), blk]
        + ([blk] if res is not None else []),
        out_specs=[blk, pl.BlockSpec((1, w), lambda i: (0, 0))],
        out_shape=[jax.ShapeDtypeStruct((r, w), out_dtype), jax.ShapeDtypeStruct((1, w), F32)],
        compiler_params=_cp("arbitrary"),
    )(*((x, g, dy) + ((res,) if res is not None else ())))


def _gated_rms_fwd(y, proj, g, *, name):
    r, w = y.shape
    tr = _pick(r, 256, 8)

    def body(y_ref, z_ref, g_ref, o_ref):
        t = y_ref[...] * _silu(z_ref[...])
        rs = lax.rsqrt(jnp.mean(t * t, axis=-1, keepdims=True) + EPS)
        o_ref[...] = (t * rs * g_ref[...]).astype(o_ref.dtype)

    return pl.pallas_call(
        body, name=name, grid=(r // tr,),
        in_specs=[pl.BlockSpec((tr, w), lambda i: (i, 0)), pl.BlockSpec((tr, w), lambda i: (i, OFF_Z // w)),
                  pl.BlockSpec((1, w), lambda i: (0, 0))],
        out_specs=pl.BlockSpec((tr, w), lambda i: (i, 0)),
        out_shape=jax.ShapeDtypeStruct((r, w), BF16),
        compiler_params=_cp("parallel"),
    )(y, proj, g)


def _gated_rms_bwd(y, proj, g, do, *, name):
    r, w = y.shape
    tr = _pick(r, 256, 8)

    def body(y_ref, z_ref, g_ref, do_ref, dy_ref, dz_ref, dg_ref):
        i = pl.program_id(0)
        yv, zv, dov = y_ref[...], z_ref[...], do_ref[...]
        sg = jax.nn.sigmoid(zv)
        sz = zv * sg
        t = yv * sz
        rs = lax.rsqrt(jnp.mean(t * t, axis=-1, keepdims=True) + EPS)
        th = t * rs
        dth = dov * g_ref[...]
        mm = jnp.mean(dth * th, axis=-1, keepdims=True)
        dt = rs * (dth - th * mm)
        dy_ref[...] = dt * sz
        dz_ref[...] = (dt * yv * (sg * (1.0 + zv * (1.0 - sg)))).astype(dz_ref.dtype)
        part = jnp.sum(dov * th, axis=0, keepdims=True)

        @pl.when(i == 0)
        def _():
            dg_ref[...] = part

        @pl.when(i > 0)
        def _():
            dg_ref[...] += part

    blk = pl.BlockSpec((tr, w), lambda i: (i, 0))
    vec = pl.BlockSpec((1, w), lambda i: (0, 0))
    return pl.pallas_call(
        body, name=name, grid=(r // tr,),
        in_specs=[blk, pl.BlockSpec((tr, w), lambda i: (i, OFF_Z // w)), vec, blk],
        out_specs=[blk, blk, vec],
        out_shape=[jax.ShapeDtypeStruct((r, w), F32), jax.ShapeDtypeStruct((r, w), BF16),
                   jax.ShapeDtypeStruct((1, w), F32)],
        compiler_params=_cp("arbitrary"),
    )(y, proj, g, do)


def _ffn_up(n, w13, *, name):
    m, k = n.shape
    f = w13.shape[1] // 2
    bm, bn = _pick(m, 1024, 16), _pick(f, 1408, LANE)
    nj = f // bn

    def body(a_ref, wg_ref, wu_ref, act_ref, g_ref, u_ref):
        a = a_ref[...].astype(BF16)
        g = _dot(a, wg_ref[...].astype(BF16))
        u = _dot(a, wu_ref[...].astype(BF16))
        act_ref[...] = (_silu(g) * u).astype(act_ref.dtype)
        g_ref[...] = g.astype(g_ref.dtype)
        u_ref[...] = u.astype(u_ref.dtype)

    out = pl.BlockSpec((bm, bn), lambda j, i: (i, j))
    sh = jax.ShapeDtypeStruct((m, f), BF16)
    return pl.pallas_call(
        body, name=name, grid=(nj, m // bm),
        in_specs=[pl.BlockSpec((bm, k), lambda j, i: (i, 0)), pl.BlockSpec((k, bn), lambda j, i: (0, j)),
                  pl.BlockSpec((k, bn), lambda j, i: (0, nj + j))],
        out_specs=[out, out, out], out_shape=[sh, sh, sh],
        compiler_params=_cp("parallel", "parallel"),
    )(n, w13, w13)


def _swiglu_bwd(g, u, da, *, name):
    r, f = g.shape
    tr = _pick(r, 256, 16)

    def body(g_ref, u_ref, da_ref, o_ref):
        gv, uv, dav = g_ref[...].astype(F32), u_ref[...].astype(F32), da_ref[...].astype(F32)
        sg = jax.nn.sigmoid(gv)
        o_ref[:, :f] = (dav * uv * (sg * (1.0 + gv * (1.0 - sg)))).astype(o_ref.dtype)
        o_ref[:, f:] = (dav * (gv * sg)).astype(o_ref.dtype)

    blk = pl.BlockSpec((tr, f), lambda i: (i, 0))
    return pl.pallas_call(
        body, name=name, grid=(r // tr,),
        in_specs=[blk, blk, blk],
        out_specs=pl.BlockSpec((tr, 2 * f), lambda i: (i, 0)),
        out_shape=jax.ShapeDtypeStruct((r, 2 * f), BF16),
        compiler_params=_cp("parallel"),
    )(g, u, da)


CONV_TS = 1024
CONV_TC = 512


def _conv_pre(x, carry, w_ref, b_ref):
    ts = x.shape[0]
    row8 = lax.broadcasted_iota(jnp.int32, (8, x.shape[1]), 0)
    head_x = x[0:8]
    shifted, shifted_head = [], []
    for j in range(SSD_CONV):
        if j == 0:
            shifted.append(x)
            shifted_head.append(head_x)
        else:
            shifted.append(pltpu.roll(x, j, 0))
            shifted_head.append(jnp.where(row8 < j, pltpu.roll(carry, j, 0), pltpu.roll(head_x, j, 0)))
    pre = b_ref[...] + sum(w_ref[SSD_CONV - 1 - j:SSD_CONV - j, :] * shifted[j] for j in range(SSD_CONV))
    pre_head = b_ref[...] + sum(w_ref[SSD_CONV - 1 - j:SSD_CONV - j, :] * shifted_head[j] for j in range(SSD_CONV))
    del ts
    return pre, pre_head, shifted, shifted_head


def _conv_fwd(proj, w, b, *, name):
    s = proj.shape[0]
    c = w.shape[1]
    ts, tc = _pick(s, CONV_TS, 8), CONV_TC
    off = OFF_XBC // tc

    def body(x_ref, w_ref, b_ref, o_ref, carry_ref):
        t = pl.program_id(1)

        @pl.when(t == 0)
        def _():
            carry_ref[...] = jnp.zeros_like(carry_ref)

        x = x_ref[...]
        pre, pre_head, _, _ = _conv_pre(x, carry_ref[...], w_ref, b_ref)
        o_ref[...] = _silu(pre)
        o_ref[0:8, :] = _silu(pre_head)
        carry_ref[...] = x[ts - 8:ts]

    return pl.pallas_call(
        body, name=name, grid=(c // tc, s // ts),
        in_specs=[pl.BlockSpec((ts, tc), lambda j, t: (t, j + off)), pl.BlockSpec((SSD_CONV, tc), lambda j, t: (0, j)),
                  pl.BlockSpec((1, tc), lambda j, t: (0, j))],
        out_specs=pl.BlockSpec((ts, tc), lambda j, t: (t, j)),
        out_shape=jax.ShapeDtypeStruct((s, c), F32),
        scratch_shapes=[pltpu.VMEM((8, tc), F32)],
        compiler_params=_cp("parallel", "arbitrary"),
    )(proj, w, b)


def _conv_bwd_pre(proj, w, b, dy, *, name):
    s = proj.shape[0]
    c = w.shape[1]
    ts, tc = _pick(s, CONV_TS, 8), CONV_TC
    off = OFF_XBC // tc

    def body(x_ref, w_ref, b_ref, dy_ref, dp_ref, dw_ref, db_ref, carry_ref):
        t = pl.program_id(1)

        @pl.when(t == 0)
        def _():
            carry_ref[...] = jnp.zeros_like(carry_ref)
            dw_ref[...] = jnp.zeros_like(dw_ref)
            db_ref[...] = jnp.zeros_like(db_ref)

        x = x_ref[...]
        pre, pre_head, shifted, shifted_head = _conv_pre(x, carry_ref[...], w_ref, b_ref)
        dyv = dy_ref[...]
        dp = dyv * _dsilu(pre)
        dp_head = dyv[0:8] * _dsilu(pre_head)
        row = lax.broadcasted_iota(jnp.int32, dp.shape, 0)
        dp_tail = jnp.where(row >= 8, dp, 0.0)
        dp_ref[...] = dp
        dp_ref[0:8, :] = dp_head
        db_ref[...] += jnp.sum(dp_tail, axis=0, keepdims=True) + jnp.sum(dp_head, axis=0, keepdims=True)
        for j in range(SSD_CONV):
            kk = SSD_CONV - 1 - j
            dw_ref[kk:kk + 1, :] += (jnp.sum(dp_tail * shifted[j], axis=0, keepdims=True)
                                     + jnp.sum(dp_head * shifted_head[j], axis=0, keepdims=True))
        carry_ref[...] = x[ts - 8:ts]

    return pl.pallas_call(
        body, name=name, grid=(c // tc, s // ts),
        in_specs=[pl.BlockSpec((ts, tc), lambda j, t: (t, j + off)), pl.BlockSpec((SSD_CONV, tc), lambda j, t: (0, j)),
                  pl.BlockSpec((1, tc), lambda j, t: (0, j)), pl.BlockSpec((ts, tc), lambda j, t: (t, j))],
        out_specs=[pl.BlockSpec((ts, tc), lambda j, t: (t, j)), pl.BlockSpec((SSD_CONV, tc), lambda j, t: (0, j)),
                   pl.BlockSpec((1, tc), lambda j, t: (0, j))],
        out_shape=[jax.ShapeDtypeStruct((s, c), F32), jax.ShapeDtypeStruct((SSD_CONV, c), F32),
                   jax.ShapeDtypeStruct((1, c), F32)],
        scratch_shapes=[pltpu.VMEM((8, tc), F32)],
        compiler_params=_cp("parallel", "arbitrary"),
    )(proj, w, b, dy)


def _conv_bwd_x(dp, w, *, name):
    s, c = dp.shape
    ts, tc = _pick(s, CONV_TS, 8), CONV_TC
    nt = s // ts

    def body(d_ref, w_ref, o_ref, carry_ref, full_ref):
        t = pl.program_id(1)

        @pl.when(t == 0)
        def _():
            carry_ref[...] = jnp.zeros_like(carry_ref)

        d = d_ref[...]
        carry = carry_ref[...]
        row8 = lax.broadcasted_iota(jnp.int32, (8, tc), 0)
        tail = d[ts - 8:ts]
        acc = w_ref[SSD_CONV - 1:SSD_CONV, :] * d
        acc_tail = w_ref[SSD_CONV - 1:SSD_CONV, :] * tail
        for j in range(1, SSD_CONV):
            wj = w_ref[SSD_CONV - 1 - j:SSD_CONV - j, :]
            acc = acc + wj * pltpu.roll(d, ts - j, 0)
            up_tail = jnp.where(row8 >= 8 - j, pltpu.roll(carry, 8 - j, 0), pltpu.roll(tail, 8 - j, 0))
            acc_tail = acc_tail + wj * up_tail
        full_ref[...] = acc
        full_ref[ts - 8:ts, :] = acc_tail
        o_ref[...] = full_ref[...].astype(o_ref.dtype)
        carry_ref[...] = d[0:8]

    return pl.pallas_call(
        body, name=name, grid=(c // tc, nt),
        in_specs=[pl.BlockSpec((ts, tc), lambda j, t: (nt - 1 - t, j)), pl.BlockSpec((SSD_CONV, tc), lambda j, t: (0, j))],
        out_specs=pl.BlockSpec((ts, tc), lambda j, t: (nt - 1 - t, j)),
        out_shape=jax.ShapeDtypeStruct((s, c), BF16),
        scratch_shapes=[pltpu.VMEM((8, tc), F32), pltpu.VMEM((ts, tc), F32)],
        compiler_params=_cp("parallel", "arbitrary"),
    )(dp, w)


def _merge_fwd(proj, ys, ym, *, name):
    r, w = ys.shape
    tr = _pick(r, 512, 8)
    off = OFF_GATES // w

    def body(g1_ref, g2_ref, ys_ref, ym_ref, o_ref):
        o_ref[...] = (jax.nn.sigmoid(g1_ref[...]) * ys_ref[...]
                      + jax.nn.sigmoid(g2_ref[...]) * ym_ref[...]).astype(o_ref.dtype)

    blk = pl.BlockSpec((tr, w), lambda i: (i, 0))
    return pl.pallas_call(
        body, name=name, grid=(r // tr,),
        in_specs=[pl.BlockSpec((tr, w), lambda i: (i, off)), pl.BlockSpec((tr, w), lambda i: (i, off + 1)), blk, blk],
        out_specs=blk, out_shape=jax.ShapeDtypeStruct((r, w), BF16),
        compiler_params=_cp("parallel"),
    )(proj, proj, ys, ym)


def _merge_bwd(proj, ys, ym, dm, *, name):
    r, w = ys.shape
    tr = _pick(r, 512, 8)
    off = OFF_GATES // w

    def body(g1_ref, g2_ref, ys_ref, ym_ref, dm_ref, dg_ref, dys_ref, dym_ref):
        s1, s2 = jax.nn.sigmoid(g1_ref[...]), jax.nn.sigmoid(g2_ref[...])
        dmv = dm_ref[...]
        dys_ref[...] = (dmv * s1).astype(dys_ref.dtype)
        dym_ref[...] = (dmv * s2).astype(dym_ref.dtype)
        dg_ref[:, :w] = (dmv * ys_ref[...] * s1 * (1.0 - s1)).astype(dg_ref.dtype)
        dg_ref[:, w:] = (dmv * ym_ref[...] * s2 * (1.0 - s2)).astype(dg_ref.dtype)

    blk = pl.BlockSpec((tr, w), lambda i: (i, 0))
    return pl.pallas_call(
        body, name=name, grid=(r // tr,),
        in_specs=[pl.BlockSpec((tr, w), lambda i: (i, off)), pl.BlockSpec((tr, w), lambda i: (i, off + 1)), blk, blk, blk],
        out_specs=[pl.BlockSpec((tr, 2 * w), lambda i: (i, 0)), blk, blk],
        out_shape=[jax.ShapeDtypeStruct((r, 2 * w), BF16), jax.ShapeDtypeStruct((r, w), BF16),
                   jax.ShapeDtypeStruct((r, w), BF16)],
        compiler_params=_cp("parallel"),
    )(proj, proj, ys, ym, dm)


def _loss_fwd_bwd(y, target, *, name):
    r, w = y.shape
    tr = _pick(r, 512, 8)

    def body(y_ref, t_ref, l_ref, dy_ref):
        i = pl.program_id(0)
        e = y_ref[...] - t_ref[...]
        dy_ref[...] = e * (1.0 / w)
        part = jnp.sum(e * e, axis=0, keepdims=True) * (0.5 / w)

        @pl.when(i == 0)
        def _():
            l_ref[...] = part

        @pl.when(i > 0)
        def _():
            l_ref[...] += part

    blk = pl.BlockSpec((tr, w), lambda i: (i, 0))
    return pl.pallas_call(
        body, name=name, grid=(r // tr,),
        in_specs=[blk, blk],
        out_specs=[pl.BlockSpec((1, w), lambda i: (0, 0)), blk],
        out_shape=[jax.ShapeDtypeStruct((1, w), F32), jax.ShapeDtypeStruct((r, w), F32)],
        compiler_params=_cp("arbitrary"),
    )(y, target)


def _adamw(w, g, m, v, *, name):
    r, c = w.shape
    tr = _pick(r, max(8, (1 << 20) // (4 * c) // 8 * 8), 8)
    c1 = 1.0 - ADAM_B1 ** ADAM_STEP
    c2 = 1.0 - ADAM_B2 ** ADAM_STEP

    def body(w_ref, g_ref, m_ref, v_ref, d_ref, nm_ref, nv_ref):
        gv = g_ref[...]
        nm = ADAM_B1 * m_ref[...] + (1.0 - ADAM_B1) * gv
        nv = ADAM_B2 * v_ref[...] + (1.0 - ADAM_B2) * (gv * gv)
        nm_ref[...] = nm
        nv_ref[...] = nv
        d_ref[...] = -ADAM_LR * ((nm / c1) / (jnp.sqrt(nv / c2) + ADAM_EPS) + ADAM_WD * w_ref[...])

    blk = pl.BlockSpec((tr, c), lambda i: (i, 0))
    sh = jax.ShapeDtypeStruct((r, c), F32)
    return pl.pallas_call(
        body, name=name, grid=(r // tr,),
        in_specs=[blk] * 4, out_specs=[blk] * 3, out_shape=[sh] * 3,
        compiler_params=_cp("parallel"),
    )(w, g, m, v)


def _softplus(x):
    return jnp.maximum(x, 0.0) + jnp.log(1.0 + jnp.exp(-jnp.abs(x)))


def _dot_01(x, sel):
    sel_b = sel.astype(BF16)
    acc, rem = None, x
    for _ in range(3):
        piece = rem.astype(BF16)
        part = jnp.dot(piece, sel_b, preferred_element_type=F32)
        acc = part if acc is None else acc + part
        rem = rem - piece.astype(F32)
    return acc


def _ssd_common(dtr_ref, dtrT_ref, dtb_ref, dtbT_ref, al_ref, alT_ref, e_ref):
    L = SSD_CHUNK
    ri = lax.broadcasted_iota(jnp.int32, (L, L), 0)
    cj = lax.broadcasted_iota(jnp.int32, (L, L), 1)
    tril = (ri >= cj).astype(F32)
    triu = (ri <= cj).astype(F32)
    a = -jnp.exp(al_ref[...])
    aT = -jnp.exp(alT_ref[...])
    pre = dtr_ref[...] + dtb_ref[...]
    preT = dtrT_ref[...] + dtbT_ref[...]
    dt = _softplus(pre)
    dtT = _softplus(preT)
    acum = jnp.dot(tril, dt * a, precision=HI, preferred_element_type=F32)
    acumT = jnp.dot(dtT * aT, triu, precision=HI, preferred_element_type=F32)
    e = e_ref[...]
    dt_x = _dot_01(dt, e)
    acum_x = _dot_01(acum, e)
    last_x = acum_x[L - 1:L, :]
    return dict(ri=ri, cj=cj, tril=tril, triu=triu, a=a, aT=aT, pre=pre, preT=preT, dt=dt, dtT=dtT,
                acum=acum, acumT=acumT, dt_x=dt_x, eacum_x=jnp.exp(acum_x), w_x=jnp.exp(last_x - acum_x),
                elast_x=jnp.exp(last_x))


def _dot_nt(a, b):
    return lax.dot_general(a, b, (((1,), (1,)), ((), ())), preferred_element_type=F32)


def _dot_tn(a, b):
    return lax.dot_general(a, b, (((0,), (0,)), ((), ())), preferred_element_type=F32)


def _dot(a, b):
    return jnp.dot(a, b, preferred_element_type=F32)


def _ssd_specs(nc, rev):
    L = SSD_CHUNK
    ix = (lambda c: nc - 1 - c) if rev else (lambda c: c)
    return [
        pl.BlockSpec((L, SSD_D_INNER), lambda c: (ix(c), 0)),
        pl.BlockSpec((L, 512), lambda c: (ix(c), 4)),
        pl.BlockSpec((L, 512), lambda c: (ix(c), 5)),
        pl.BlockSpec((L, SSD_HEADS), lambda c: (ix(c), 0)),
        pl.BlockSpec((SSD_HEADS, L), lambda c: (0, ix(c))),
        pl.BlockSpec((1, SSD_HEADS), lambda c: (0, 0)),
        pl.BlockSpec((SSD_HEADS, 1), lambda c: (0, 0)),
        pl.BlockSpec((1, SSD_HEADS), lambda c: (0, 0)),
        pl.BlockSpec((SSD_HEADS, 1), lambda c: (0, 0)),
        pl.BlockSpec((1, SSD_D_INNER), lambda c: (0, 0)),
        pl.BlockSpec((SSD_HEADS, SSD_D_INNER), lambda c: (0, 0)),
    ]


def _ssd_fwd(xc, dtr, dtrT, dtb, dtbT, alog, alogT, dskx, expand, *, name):
    s = xc.shape[0]
    L = SSD_CHUNK
    nc = s // L

    def body(x_ref, b_ref, c_ref, dtr_ref, dtrT_ref, dtb_ref, dtbT_ref, al_ref, alT_ref, dsk_ref, e_ref,
             y_ref, st_ref, state):
        ci = pl.program_id(0)

        @pl.when(ci == 0)
        def _():
            state[...] = jnp.zeros_like(state)

        st_ref[0] = state[...]
        q = _ssd_common(dtr_ref, dtrT_ref, dtb_ref, dtbT_ref, al_ref, alT_ref, e_ref)
        causal = q["ri"] >= q["cj"]
        lane_lo = q["cj"] < 64
        x = x_ref[...]
        xdt = x * q["dt_x"]
        xdt_b = xdt.astype(BF16)
        xdtw_b = (xdt * q["w_x"]).astype(BF16)
        for g in range(SSD_GROUPS):
            bg = b_ref[:, 128 * g:128 * g + 128]
            cg_b = c_ref[:, 128 * g:128 * g + 128].astype(BF16)
            cb = _dot_nt(cg_b, bg.astype(BF16))
            bgT_b = bg.T.astype(BF16)
            s0 = state[g]
            for jj in range(4):
                j = 4 * g + jj
                sl = slice(128 * j, 128 * j + 128)
                sls = slice(128 * jj, 128 * jj + 128)
                ms = []
                for h in (2 * j, 2 * j + 1):
                    seg = q["acum"][:, h:h + 1] - q["acumT"][h:h + 1, :]
                    decay = jnp.exp(jnp.where(causal, seg, -jnp.inf))
                    ms.append((cb * decay).astype(BF16))
                mcat = jnp.concatenate(ms, axis=1)
                xp = xdt_b[:, sl]
                zero = jnp.zeros_like(xp)
                xstack = jnp.concatenate([jnp.where(lane_lo, xp, zero), jnp.where(lane_lo, zero, xp)], axis=0)
                y = _dot(mcat, xstack)
                y = y + q["eacum_x"][:, sl] * _dot(cg_b, s0[:, sls].astype(BF16))
                y = y + x[:, sl] * dsk_ref[:, sl]
                y_ref[:, sl] = y
                state[g, :, sls] = s0[:, sls] * q["elast_x"][:, sl] + _dot(bgT_b, xdtw_b[:, sl])

    return pl.pallas_call(
        body, name=name, grid=(nc,),
        in_specs=_ssd_specs(nc, False),
        out_specs=[pl.BlockSpec((L, SSD_D_INNER), lambda c: (c, 0)),
                   pl.BlockSpec((1, SSD_GROUPS, SSD_STATE, 512), lambda c: (c, 0, 0, 0))],
        out_shape=[jax.ShapeDtypeStruct((s, SSD_D_INNER), F32),
                   jax.ShapeDtypeStruct((nc, SSD_GROUPS, SSD_STATE, 512), F32)],
        scratch_shapes=[pltpu.VMEM((SSD_GROUPS, SSD_STATE, 512), F32)],
        compiler_params=_cp("arbitrary"),
    )(xc, xc, xc, dtr, dtrT, dtb, dtbT, alog, alogT, dskx, expand)


def _ssd_bwd(xc, dtr, dtrT, dtb, dtbT, alog, alogT, dskx, expand, expandT, states, dy, *, name):
    s = xc.shape[0]
    L = SSD_CHUNK
    H = SSD_HEADS
    nc = s // L

    def body(x_ref, b_ref, c_ref, dtr_ref, dtrT_ref, dtb_ref, dtbT_ref, al_ref, alT_ref, dsk_ref, e_ref,
             et_ref, st_ref, dy_ref,
             dxc_ref, ddtc_ref, ddtr_ref, dbc_ref, dbr_ref, dac_ref, dar_ref, ddsk_ref, dstate):
        ci = pl.program_id(0)

        @pl.when(ci == 0)
        def _():
            dstate[...] = jnp.zeros_like(dstate)
            dbc_ref[...] = jnp.zeros_like(dbc_ref)
            dbr_ref[...] = jnp.zeros_like(dbr_ref)
            dac_ref[...] = jnp.zeros_like(dac_ref)
            dar_ref[...] = jnp.zeros_like(dar_ref)
            ddsk_ref[...] = jnp.zeros_like(ddsk_ref)

        q = _ssd_common(dtr_ref, dtrT_ref, dtb_ref, dtbT_ref, al_ref, alT_ref, e_ref)
        ri, cj = q["ri"], q["cj"]
        causal = ri >= cj
        causalT = ri <= cj
        lane_lo = cj < 64
        lane_h = lax.broadcasted_iota(jnp.int32, (1, H), 1)
        sub_h = lax.broadcasted_iota(jnp.int32, (H, 1), 0)
        x = x_ref[...]
        dyv = dy_ref[...]
        xdt = x * q["dt_x"]
        xdt_b = xdt.astype(BF16)
        xdtw = xdt * q["w_x"]
        xdtw_b = xdtw.astype(BF16)
        edy = q["eacum_x"] * dyv
        edy_b = edy.astype(BF16)
        dyv_b = dyv.astype(BF16)
        dacum_col = jnp.zeros((L, H), F32)
        dacum_row = jnp.zeros((H, L), F32)
        dxdt_t, yoff_t, u_t, r_t = [], [], [], []
        for g in range(SSD_GROUPS):
            bg = b_ref[:, 128 * g:128 * g + 128]
            cg = c_ref[:, 128 * g:128 * g + 128]
            bg_b, cg_b = bg.astype(BF16), cg.astype(BF16)
            cb = _dot_nt(cg_b, bg_b)
            cbT = _dot_nt(bg_b, cg_b)
            cgT_b = cg.T.astype(BF16)
            s0 = st_ref[0, g]
            ds = dstate[g]
            s0_b, ds_b = s0.astype(BF16), ds.astype(BF16)
            dcb = jnp.zeros((L, L), F32)
            for jj in range(4):
                j = 4 * g + jj
                sl = slice(128 * j, 128 * j + 128)
                sls = slice(128 * jj, 128 * jj + 128)
                decs, mts = [], []
                for h in (2 * j, 2 * j + 1):
                    seg = q["acum"][:, h:h + 1] - q["acumT"][h:h + 1, :]
                    decs.append(jnp.exp(jnp.where(causal, seg, -jnp.inf)))
                    mts.append((cbT * jnp.exp(jnp.where(causalT, -seg, -jnp.inf))).astype(BF16))
                dyt_b = dyv_b[:, sl]
                zero = jnp.zeros_like(dyt_b)
                dystack = jnp.concatenate([jnp.where(lane_lo, dyt_b, zero), jnp.where(lane_lo, zero, dyt_b)], axis=0)
                dxs = _dot(jnp.concatenate(mts, axis=0), dyt_b)
                dxdt = jnp.where(lane_lo, dxs[:L], dxs[L:])
                dmcat = _dot_nt(dystack, xdt_b[:, sl])
                for idx, h in enumerate((2 * j, 2 * j + 1)):
                    dm = dmcat[L * idx:L * idx + L]
                    dcb = dcb + dm * decs[idx]
                    dseg = dm * cb * decs[idx]
                    dacum_col = dacum_col + jnp.sum(dseg, axis=1, keepdims=True) * (lane_h == h).astype(F32)
                    dacum_row = dacum_row - (sub_h == h).astype(F32) * jnp.sum(dseg, axis=0, keepdims=True)
                gmat = _dot(cg_b, s0_b[:, sls])
                yoff_t.append(edy[:, sl] * gmat)
                qm = _dot(bg_b, ds_b[:, sls])
                dxdt_t.append(dxdt + qm * q["w_x"][:, sl])
                u_t.append(qm * xdtw[:, sl])
                r_t.append(ds[:, sls] * s0[:, sls] * q["elast_x"][:, sl])
                dstate[g, :, sls] = ds[:, sls] * q["elast_x"][:, sl] + _dot(cgT_b, edy_b[:, sl])
            gsl = slice(512 * g, 512 * g + 512)
            dcb_b = dcb.astype(BF16)
            dcg = _dot(dcb_b, bg_b) + _dot_nt(edy_b[:, gsl], s0_b)
            dbg = _dot(dcb.T.astype(BF16), cg_b) + _dot_nt(xdtw_b[:, gsl], ds_b)
            dxc_ref[:, SSD_D_INNER + 128 * g:SSD_D_INNER + 128 * g + 128] = dbg
            dxc_ref[:, SSD_D_INNER + 512 + 128 * g:SSD_D_INNER + 512 + 128 * g + 128] = dcg
        et = et_ref[...]
        dxdt_all = jnp.concatenate(dxdt_t, axis=1)
        yoff = jnp.concatenate(yoff_t, axis=1)
        uu = jnp.concatenate(u_t, axis=1)
        rr = jnp.concatenate(r_t, axis=1)
        dacum_col = dacum_col + _dot_01(yoff - uu, et)
        dlast = jnp.sum(_dot_01(uu + rr, et), axis=0, keepdims=True)
        row_lh = lax.broadcasted_iota(jnp.int32, (L, H), 0)
        dacum_col = dacum_col + jnp.where(row_lh == L - 1, dlast, 0.0)
        d_dta_col = jnp.dot(q["triu"], dacum_col, precision=HI, preferred_element_type=F32)
        d_dta_row = jnp.dot(dacum_row, q["tril"], precision=HI, preferred_element_type=F32)
        ddt_col = d_dta_col * q["a"] + _dot_01(dxdt_all * x, et)
        ddt_row = d_dta_row * q["aT"]
        ddtr_col = ddt_col * jax.nn.sigmoid(q["pre"])
        ddtr_row = ddt_row * jax.nn.sigmoid(q["preT"])
        ddtc_ref[...] = ddtr_col
        ddtr_ref[...] = ddtr_row
        dac_ref[...] += jnp.sum(d_dta_col * q["dt"], axis=0, keepdims=True)
        dar_ref[...] += jnp.sum(d_dta_row * q["dtT"], axis=1, keepdims=True)
        dbc_ref[...] += jnp.sum(ddtr_col, axis=0, keepdims=True)
        dbr_ref[...] += jnp.sum(ddtr_row, axis=1, keepdims=True)
        ddsk_ref[...] += jnp.sum(dyv * x, axis=0, keepdims=True)
        dxc_ref[:, 0:SSD_D_INNER] = dxdt_all * q["dt_x"] + dyv * dsk_ref[...]

    rv = lambda c: nc - 1 - c
    in_specs = _ssd_specs(nc, True) + [
        pl.BlockSpec((SSD_D_INNER, H), lambda c: (0, 0)),
        pl.BlockSpec((1, SSD_GROUPS, SSD_STATE, 512), lambda c: (rv(c), 0, 0, 0)),
        pl.BlockSpec((L, SSD_D_INNER), lambda c: (rv(c), 0)),
    ]
    vec_c = pl.BlockSpec((1, H), lambda c: (0, 0))
    vec_r = pl.BlockSpec((H, 1), lambda c: (0, 0))
    return pl.pallas_call(
        body, name=name, grid=(nc,),
        in_specs=in_specs,
        out_specs=[pl.BlockSpec((L, SSD_CONV_DIM), lambda c: (rv(c), 0)),
                   pl.BlockSpec((L, H), lambda c: (rv(c), 0)),
                   pl.BlockSpec((H, L), lambda c: (0, rv(c))),
                   vec_c, vec_r, vec_c, vec_r,
                   pl.BlockSpec((1, SSD_D_INNER), lambda c: (0, 0))],
        out_shape=[jax.ShapeDtypeStruct((s, SSD_CONV_DIM), F32),
                   jax.ShapeDtypeStruct((s, H), F32), jax.ShapeDtypeStruct((H, s), F32),
                   jax.ShapeDtypeStruct((1, H), F32), jax.ShapeDtypeStruct((H, 1), F32),
                   jax.ShapeDtypeStruct((1, H), F32), jax.ShapeDtypeStruct((H, 1), F32),
                   jax.ShapeDtypeStruct((1, SSD_D_INNER), F32)],
        scratch_shapes=[pltpu.VMEM((SSD_GROUPS, SSD_STATE, 512), F32)],
        compiler_params=_cp("arbitrary"),
    )(xc, xc, xc, dtr, dtrT, dtb, dtbT, alog, alogT, dskx, expand, expandT, states, dy)


QK_PAD = 256
MLA_TS = 512


def _rope_tables4(pos):
    inv = 1.0 / (ROPE_THETA ** (jnp.arange(0, MLA_ROPE, 2, dtype=F32) / MLA_ROPE))
    ang = pos.astype(F32)[:, None] * inv
    c, s = jnp.cos(ang), jnp.sin(ang)
    return jnp.tile(c, (1, 4)), jnp.concatenate([-s, s, -s, s], axis=1)


def _mla_gains(qg, kg):
    z = jnp.zeros((LANE - MLA_ROPE,), F32)
    return (qg[:MLA_NOPE][None], jnp.concatenate([qg[MLA_NOPE:], z])[None],
            kg[:MLA_NOPE][None], jnp.concatenate([kg[MLA_NOPE:], z])[None])


def _rope_swap(t, first):
    return jnp.where(first, pltpu.roll(t, 96, 1), pltpu.roll(t, 32, 1))


def _mla_prep_specs(ts):
    row = lambda w, c=0: pl.BlockSpec((ts, w), lambda i: (i, c))
    vec = pl.BlockSpec((1, LANE), lambda i: (0, 0))
    return [row(MLA_HEADS * MLA_QK), row(2 * MLA_HEADS * MLA_NOPE), row(LANE, OFF_KRDT // LANE), row(LANE), row(LANE),
            vec, vec, vec, vec]


def _mla_prep_fwd(qraw, kvraw, proj, cos4, sin4, gqn, gqr, gkn, gkr, *, name):
    s = qraw.shape[0]
    ts = _pick(s, MLA_TS, 8)

    def body(q_ref, kv_ref, kr_ref, cos_ref, sin_ref, gqn_ref, gqr_ref, gkn_ref, gkr_ref, qo_ref, ko_ref):
        lane = lax.broadcasted_iota(jnp.int32, (ts, LANE), 1)
        lo = lane < 64
        first = (lane % 64) < 32
        cos, sin = cos_ref[...], sin_ref[...]
        kr = jnp.where(lo, kr_ref[...], 0.0)
        ssq_kr = jnp.sum(kr * kr, axis=-1, keepdims=True)

        def head(xn, xr, ssq_r, gn, gr):
            rs = lax.rsqrt((jnp.sum(xn * xn, axis=-1, keepdims=True) + ssq_r) * (1.0 / MLA_QK) + EPS)
            yr = xr * rs * gr
            return xn * rs * gn, yr * cos + _rope_swap(yr, first) * sin

        for h in range(MLA_HEADS):
            tile = q_ref[:, MLA_HEADS * MLA_NOPE + LANE * (h // 2):MLA_HEADS * MLA_NOPE + LANE * (h // 2) + LANE]
            qr = jnp.where(lo, tile if h % 2 == 0 else pltpu.roll(tile, 64, 1), 0.0)
            on, orr = head(q_ref[:, LANE * h:LANE * h + LANE], qr, jnp.sum(qr * qr, axis=-1, keepdims=True),
                           gqn_ref[...], gqr_ref[...])
            qo_ref[h, :, 0:LANE] = (on * ATT_SCALE).astype(BF16)
            qo_ref[h, :, LANE:QK_PAD] = (orr * ATT_SCALE).astype(BF16)
            on, orr = head(kv_ref[:, LANE * h:LANE * h + LANE], kr, ssq_kr, gkn_ref[...], gkr_ref[...])
            ko_ref[h, :, 0:LANE] = on.astype(BF16)
            ko_ref[h, :, LANE:QK_PAD] = orr.astype(BF16)

    out = pl.BlockSpec((MLA_HEADS, ts, QK_PAD), lambda i: (0, i, 0))
    sh = jax.ShapeDtypeStruct((MLA_HEADS, s, QK_PAD), BF16)
    return pl.pallas_call(
        body, name=name, grid=(s // ts,),
        in_specs=_mla_prep_specs(ts), out_specs=[out, out], out_shape=[sh, sh],
        compiler_params=_cp("parallel"),
    )(qraw, kvraw, proj, cos4, sin4, gqn, gqr, gkn, gkr)


def _mla_prep_bwd(qraw, kvraw, proj, cos4, sin4, gqn, gqr, gkn, gkr, dq, dk, *, name):
    s = qraw.shape[0]
    ts = _pick(s, MLA_TS, 8)

    def body(q_ref, kv_ref, kr_ref, cos_ref, sin_ref, gqn_ref, gqr_ref, gkn_ref, gkr_ref, dq_ref, dk_ref,
             dqraw_ref, dkn_ref, dkr_ref, dgqn_ref, dgqr_ref, dgkn_ref, dgkr_ref):
        i = pl.program_id(0)

        @pl.when(i == 0)
        def _():
            for r in (dgqn_ref, dgqr_ref, dgkn_ref, dgkr_ref):
                r[...] = jnp.zeros_like(r)

        lane = lax.broadcasted_iota(jnp.int32, (ts, LANE), 1)
        lo = lane < 64
        first = (lane % 64) < 32
        cos, sin = cos_ref[...], sin_ref[...]
        kr = jnp.where(lo, kr_ref[...], 0.0)
        ssq_kr = jnp.sum(kr * kr, axis=-1, keepdims=True)

        def head(xn, xr, ssq_r, gn, gr, don, dor):
            rs = lax.rsqrt((jnp.sum(xn * xn, axis=-1, keepdims=True) + ssq_r) * (1.0 / MLA_QK) + EPS)
            xhn, xhr = xn * rs, xr * rs
            dor = jnp.where(lo, dor, 0.0)
            dyr = dor * cos + _rope_swap(dor * sin, first)
            dxn, dxr = don * gn, dyr * gr
            mm = (jnp.sum(dxn * xhn, axis=-1, keepdims=True) + jnp.sum(dxr * xhr, axis=-1, keepdims=True)) * (1.0 / MLA_QK)
            return (rs * (dxn - xhn * mm), rs * (dxr - xhr * mm),
                    jnp.sum(don * xhn, axis=0, keepdims=True), jnp.sum(dyr * xhr, axis=0, keepdims=True))

        dkr_acc = jnp.zeros((ts, LANE), F32)
        prev = None
        for h in range(MLA_HEADS):
            c0 = MLA_HEADS * MLA_NOPE + LANE * (h // 2)
            tile = q_ref[:, c0:c0 + LANE]
            qr = jnp.where(lo, tile if h % 2 == 0 else pltpu.roll(tile, 64, 1), 0.0)
            dn, dr, gn_p, gr_p = head(q_ref[:, LANE * h:LANE * h + LANE], qr, jnp.sum(qr * qr, axis=-1, keepdims=True),
                                      gqn_ref[...], gqr_ref[...], dq_ref[h, :, 0:LANE], dq_ref[h, :, LANE:QK_PAD])
            dqraw_ref[:, LANE * h:LANE * h + LANE] = dn.astype(dqraw_ref.dtype)
            dgqn_ref[...] += gn_p
            dgqr_ref[...] += gr_p
            if h % 2 == 0:
                prev = dr
            else:
                dqraw_ref[:, c0:c0 + LANE] = (prev + pltpu.roll(dr, 64, 1)).astype(dqraw_ref.dtype)
            dn, dr, gn_p, gr_p = head(kv_ref[:, LANE * h:LANE * h + LANE], kr, ssq_kr, gkn_ref[...], gkr_ref[...],
                                      dk_ref[h, :, 0:LANE], dk_ref[h, :, LANE:QK_PAD])
            dkn_ref[:, LANE * h:LANE * h + LANE] = dn.astype(dkn_ref.dtype)
            dkr_acc = dkr_acc + dr
            dgkn_ref[...] += gn_p
            dgkr_ref[...] += gr_p
        dkr_ref[...] = dkr_acc

    row = lambda w: pl.BlockSpec((ts, w), lambda i: (i, 0))
    vec = pl.BlockSpec((1, LANE), lambda i: (0, 0))
    dspec = pl.BlockSpec((MLA_HEADS, ts, QK_PAD), lambda i: (0, i, 0))
    vsh = jax.ShapeDtypeStruct((1, LANE), F32)
    return pl.pallas_call(
        body, name=name, grid=(s // ts,),
        in_specs=_mla_prep_specs(ts) + [dspec, dspec],
        out_specs=[row(MLA_HEADS * MLA_QK), row(MLA_HEADS * MLA_NOPE), row(LANE), vec, vec, vec, vec],
        out_shape=[jax.ShapeDtypeStruct((s, MLA_HEADS * MLA_QK), BF16), jax.ShapeDtypeStruct((s, MLA_HEADS * MLA_NOPE), BF16),
                   jax.ShapeDtypeStruct((s, LANE), F32), vsh, vsh, vsh, vsh],
        compiler_params=_cp("arbitrary"),
    )(qraw, kvraw, proj, cos4, sin4, gqn, gqr, gkn, gkr, dq, dk)


ATT_T = 1024
ATT_T_FWD = 2048
ATT_SCALE = MLA_QK ** -0.5


def _attn_fwd(q, k, kvraw, *, name):
    nh, s, _ = q.shape
    t = _pick(s, ATT_T_FWD, LANE)
    nb = s // t

    def body(q_ref, k_ref, v_ref, o_ref, lse_ref, m_ref, l_ref, acc_ref):
        i, j = pl.program_id(1), pl.program_id(2)

        @pl.when(j == 0)
        def _():
            m_ref[...] = jnp.full_like(m_ref, -jnp.inf)
            l_ref[...] = jnp.zeros_like(l_ref)
            acc_ref[...] = jnp.zeros_like(acc_ref)

        def step(diagonal):
            sc = _dot_nt(q_ref[0], k_ref[0])
            if diagonal:
                ri = lax.broadcasted_iota(jnp.int32, (t, t), 0)
                cj = lax.broadcasted_iota(jnp.int32, (t, t), 1)
                sc = jnp.where(ri >= cj, sc, -jnp.inf)
            m_new = jnp.maximum(m_ref[...], jnp.max(sc, axis=-1, keepdims=True))
            alpha = jnp.exp(m_ref[...] - m_new)
            p = jnp.exp(sc - m_new)
            l_ref[...] = alpha * l_ref[...] + jnp.sum(p, axis=-1, keepdims=True)
            acc_ref[...] = alpha * acc_ref[...] + _dot(p.astype(BF16), v_ref[...].astype(BF16))
            m_ref[...] = m_new

        @pl.when(j < i)
        def _():
            step(False)

        @pl.when(j == i)
        def _():
            step(True)
            o_ref[...] = acc_ref[...] / l_ref[...]
            lse_ref[0] = m_ref[...] + jnp.log(l_ref[...])

    return pl.pallas_call(
        body, name=name, grid=(nh, nb, nb),
        in_specs=[pl.BlockSpec((1, t, QK_PAD), lambda h, i, j: (h, i, 0)),
                  pl.BlockSpec((1, t, QK_PAD), lambda h, i, j: (h, jnp.minimum(j, i), 0)),
                  pl.BlockSpec((t, MLA_V), lambda h, i, j: (jnp.minimum(j, i), nh + h))],
        out_specs=[pl.BlockSpec((t, MLA_V), lambda h, i, j: (i, h)),
                   pl.BlockSpec((1, t, 1), lambda h, i, j: (h, i, 0))],
        out_shape=[jax.ShapeDtypeStruct((s, nh * MLA_V), F32), jax.ShapeDtypeStruct((nh, s, 1), F32)],
        scratch_shapes=[pltpu.VMEM((t, 1), F32), pltpu.VMEM((t, 1), F32), pltpu.VMEM((t, MLA_V), F32)],
        compiler_params=_cp("parallel", "parallel", "arbitrary"),
    )(q, k, kvraw)


def _attn_bwd(q, k, kvraw, o, lse, do, *, name):
    nh, s, _ = q.shape
    t = _pick(s, ATT_T, LANE)
    nb = s // t

    def body(q_ref, k_ref, v_ref, o_ref, lse_ref, do_ref, dq_ref, dk_ref, dv_ref, dk_acc, dv_acc):
        j, i = pl.program_id(1), pl.program_id(2)

        @pl.when(i == 0)
        def _():
            dk_acc[...] = jnp.zeros_like(dk_acc)
            dv_acc[...] = jnp.zeros_like(dv_acc)

        def step(diagonal):
            qv, kv = q_ref[0], k_ref[0]
            sc = _dot_nt(qv, kv)
            if diagonal:
                ri = lax.broadcasted_iota(jnp.int32, (t, t), 0)
                cj = lax.broadcasted_iota(jnp.int32, (t, t), 1)
                sc = jnp.where(ri >= cj, sc, -jnp.inf)
            p = jnp.exp(sc - lse_ref[0])
            dov = do_ref[...]
            delta = jnp.sum(dov * o_ref[...], axis=-1, keepdims=True)
            do_b = dov.astype(BF16)
            dv_acc[...] += _dot_tn(p.astype(BF16), do_b)
            dp = _dot_nt(do_b, v_ref[...].astype(BF16))
            ds_b = (p * (dp - delta)).astype(BF16)
            dk_acc[...] += _dot_tn(ds_b, qv)
            dq_part = _dot(ds_b, kv) * ATT_SCALE
            rows = pl.ds(pl.multiple_of(i * t, t), t)

            @pl.when(j == 0)
            def _():
                dq_ref[0, rows, :] = dq_part

            @pl.when(j > 0)
            def _():
                dq_ref[0, rows, :] += dq_part

        @pl.when(i > j)
        def _():
            step(False)

        @pl.when(i == j)
        def _():
            step(True)

        @pl.when(i == nb - 1)
        def _():
            dk_ref[0] = dk_acc[...]
            dv_ref[...] = dv_acc[...].astype(dv_ref.dtype)

    qi = lambda h, j, i: jnp.maximum(i, j)
    return pl.pallas_call(
        body, name=name, grid=(nh, nb, nb),
        in_specs=[pl.BlockSpec((1, t, QK_PAD), lambda h, j, i: (h, qi(h, j, i), 0)),
                  pl.BlockSpec((1, t, QK_PAD), lambda h, j, i: (h, j, 0)),
                  pl.BlockSpec((t, MLA_V), lambda h, j, i: (j, nh + h)),
                  pl.BlockSpec((t, MLA_V), lambda h, j, i: (qi(h, j, i), h)),
                  pl.BlockSpec((1, t, 1), lambda h, j, i: (h, qi(h, j, i), 0)),
                  pl.BlockSpec((t, MLA_V), lambda h, j, i: (qi(h, j, i), h))],
        out_specs=[pl.BlockSpec((1, s, QK_PAD), lambda h, j, i: (h, 0, 0)),
                   pl.BlockSpec((1, t, QK_PAD), lambda h, j, i: (h, j, 0)),
                   pl.BlockSpec((t, MLA_V), lambda h, j, i: (j, h))],
        out_shape=[jax.ShapeDtypeStruct((nh, s, QK_PAD), F32), jax.ShapeDtypeStruct((nh, s, QK_PAD), F32),
                   jax.ShapeDtypeStruct((s, nh * MLA_V), BF16)],
        scratch_shapes=[pltpu.VMEM((t, QK_PAD), F32), pltpu.VMEM((t, MLA_V), F32)],
        compiler_params=_cp("parallel", "arbitrary", "arbitrary"),
    )(q, k, kvraw, o, lse, do)


def _ffn_fwd(h, w, tag):
    n = _rms_fwd(h, w["ln"], name=tag + "_norm")
    act, gate, up = _ffn_up(n, w["w13"], name=tag + "_up")
    out = _matmul(act, w["w2"], "nn", name=tag + "_down", scale=0.5, res=h)
    return out, (h, n, gate, up, act)


def _ffn_bwd(dout, saved, w, tag):
    h, n, gate, up, act = saved
    dact = _matmul(dout, w["w2"], "nt", name=tag + "_down_dx", scale=0.5, out_dtype=BF16)
    dw2 = _matmul(act, dout, "tn", name=tag + "_down_dw", scale=0.5)
    dgu = _swiglu_bwd(gate, up, dact, name=tag + "_act_bwd")
    dw13 = _matmul(n, dgu, "tn", name=tag + "_up_dw")
    dn = _matmul(dgu, w["w13"], "nt", name=tag + "_up_dx")
    dh, dln = _rms_bwd(h, w["ln"], dn, name=tag + "_norm_bwd", res=dout)
    return dh, dict(ln=dln, w13=dw13, w2=dw2)


def _mixer_fwd(h, w, rope, tag):
    cos4, sin4 = rope
    u = _rms_fwd(h, w["ln_mix"], name=tag + "_norm")
    proj = _matmul(u, w["w_in"], "nn", name=tag + "_in")
    xc = _conv_fwd(proj, w["conv_w"], w["conv_b"], name=tag + "_conv")
    dtr = proj[:, OFF_KRDT + MLA_ROPE:OFF_KRDT + MLA_ROPE + SSD_HEADS]
    dtrT = dtr.T
    y, states = _ssd_fwd(xc, dtr, dtrT, *w["ssd_aux"], name=tag + "_ssd")
    yn = _gated_rms_fwd(y, proj, w["ssd_norm"], name=tag + "_ssd_norm")
    y_ssd = _matmul(yn, w["w_ssd_out"], "nn", name=tag + "_ssd_out")
    cqn = _rms_fwd(proj, w["q_lora_norm"], name=tag + "_q_lora_norm", col=OFF_CQ // MLA_Q_LORA, width=MLA_Q_LORA)
    qraw = _matmul(cqn, w["w_uq"], "nn", name=tag + "_uq")
    ckvn = _rms_fwd(proj, w["kv_lora_norm"], name=tag + "_kv_lora_norm", col=OFF_CKV // MLA_KV_LORA, width=MLA_KV_LORA)
    kvraw = _matmul(ckvn, w["w_ukv"], "nn", name=tag + "_ukv")
    qf, kf = _mla_prep_fwd(qraw, kvraw, proj, cos4, sin4, *w["qk_gains"], name=tag + "_qk_prep")
    o, lse = _attn_fwd(qf, kf, kvraw, name=tag + "_attn")
    y_mla = _matmul(o, w["w_mla_out"], "nn", name=tag + "_mla_out")
    merged = _merge_fwd(proj, y_ssd, y_mla, name=tag + "_merge")
    out = _matmul(merged, w["w_o"], "nn", name=tag + "_o", res=h)
    saved = dict(h=h, u=u, proj=proj, xc=xc, dtr=dtr, dtrT=dtrT, states=states, y=y, yn=yn, y_ssd=y_ssd, cqn=cqn,
                 qraw=qraw, ckvn=ckvn, kvraw=kvraw, qf=qf, kf=kf, o=o, lse=lse, y_mla=y_mla, merged=merged)
    return out, saved


def _mixer_bwd(dout, s, w, rope, tag):
    cos4, sin4 = rope
    g = {}
    proj = s["proj"]
    dmerged = _matmul(dout, w["w_o"], "nt", name=tag + "_o_dx")
    g["w_o"] = _matmul(s["merged"], dout, "tn", name=tag + "_o_dw")
    dgates, dy_ssd, dy_mla = _merge_bwd(proj, s["y_ssd"], s["y_mla"], dmerged, name=tag + "_merge_bwd")
    do = _matmul(dy_mla, w["w_mla_out"], "nt", name=tag + "_mla_out_dx")
    g["w_mla_out"] = _matmul(s["o"], dy_mla, "tn", name=tag + "_mla_out_dw")
    dqf, dkf, dv = _attn_bwd(s["qf"], s["kf"], s["kvraw"], s["o"], s["lse"], do, name=tag + "_attn_bwd")
    dqraw, dkn, dkrt, dgqn, dgqr, dgkn, dgkr = _mla_prep_bwd(
        s["qraw"], s["kvraw"], proj, cos4, sin4, *w["qk_gains"], dqf, dkf, name=tag + "_qk_prep_bwd")
    g["q_norm"] = jnp.concatenate([dgqn[0], dgqr[0, :MLA_ROPE]])
    g["k_norm"] = jnp.concatenate([dgkn[0], dgkr[0, :MLA_ROPE]])
    dkvraw = jnp.concatenate([dkn, dv], axis=1)
    dcqn = _matmul(dqraw, w["w_uq"], "nt", name=tag + "_uq_dx")
    g["w_uq"] = _matmul(s["cqn"], dqraw, "tn", name=tag + "_uq_dw")
    dckvn = _matmul(dkvraw, w["w_ukv"], "nt", name=tag + "_ukv_dx")
    g["w_ukv"] = _matmul(s["ckvn"], dkvraw, "tn", name=tag + "_ukv_dw")
    dcq, g["q_lora_norm"] = _rms_bwd(proj, w["q_lora_norm"], dcqn, name=tag + "_q_lora_norm_bwd",
                                     col=OFF_CQ // MLA_Q_LORA, width=MLA_Q_LORA, out_dtype=BF16)
    dckv, g["kv_lora_norm"] = _rms_bwd(proj, w["kv_lora_norm"], dckvn, name=tag + "_kv_lora_norm_bwd",
                                       col=OFF_CKV // MLA_KV_LORA, width=MLA_KV_LORA, out_dtype=BF16)
    dyn = _matmul(dy_ssd, w["w_ssd_out"], "nt", name=tag + "_ssd_out_dx")
    g["w_ssd_out"] = _matmul(s["yn"], dy_ssd, "tn", name=tag + "_ssd_out_dw")
    dy, dz, g["ssd_norm"] = _gated_rms_bwd(s["y"], proj, w["ssd_norm"], dyn, name=tag + "_ssd_norm_bwd")
    aux = w["ssd_aux"]
    dxc, ddt_c, ddt_r, dbias_c, dbias_r, da_c, da_r, ddsk = _ssd_bwd(
        s["xc"], s["dtr"], s["dtrT"], *aux, aux[-1].T, s["states"], dy, name=tag + "_ssd_bwd")
    g["dt_bias"] = dbias_c[0] + dbias_r[:, 0]
    g["a_log"] = (da_c[0] + da_r[:, 0]) * (-jnp.exp(aux[2][0]))
    g["d_skip"] = jnp.sum(ddsk.reshape(SSD_HEADS, SSD_HEAD_DIM), axis=1)
    dpre, g["conv_w"], g["conv_b"] = _conv_bwd_pre(proj, w["conv_w"], w["conv_b"], dxc, name=tag + "_conv_bwd_pre")
    dxbc = _conv_bwd_x(dpre, w["conv_w"], name=tag + "_conv_bwd_x")
    ddtr = ddt_c + ddt_r.T
    dkrdt = jnp.concatenate([dkrt[:, :MLA_ROPE], ddtr, jnp.zeros((ddtr.shape[0], LANE - MLA_ROPE - SSD_HEADS), F32)], axis=1)
    dproj = _join_cols([dz, dxbc, dgates, dcq, dckv, dkrdt.astype(BF16)], name=tag + "_dproj")
    du = _matmul(dproj, w["w_in"], "nt", name=tag + "_in_dx")
    g["w_in"] = _matmul(s["u"], dproj, "tn", name=tag + "_in_dw")
    dh, g["ln_mix"] = _rms_bwd(s["h"], w["ln_mix"], du, name=tag + "_norm_bwd", res=dout)
    return dh, g


W_NAMES = ["ln_ffn1", "ffn1_w13", "ffn1_w2", "ln_mix", "w_in", "conv_w", "conv_b", "dt_bias", "a_log", "d_skip",
           "ssd_norm", "w_ssd_out", "q_lora_norm", "w_uq", "kv_lora_norm", "w_ukv", "q_norm", "k_norm", "w_mla_out",
           "w_o", "ln_ffn2", "ffn2_w13", "ffn2_w2"]
SHARD_AXIS = {"ffn1_w13": 2, "ffn1_w2": 1, "w_in": 2, "conv_w": 2, "w_ssd_out": 1, "w_uq": 2, "w_ukv": 2,
              "w_mla_out": 1, "w_o": 1, "ffn2_w13": 2, "ffn2_w2": 1}
SHARDED = [n for n in W_NAMES if n in SHARD_AXIS and n != "conv_w"] + ["conv_w"]
REPLICATED = [n for n in W_NAMES if n not in SHARD_AXIS]
N_CHIPS = 4
N_DEV = 8
PACK_COLS = 1024
IN_SPLIT = (2048, 3072, 32, 512, 256, 64, 2048)


def _pack_mats(arrs, rows, dtype):
    mats = [a.astype(dtype).reshape(-1, PACK_COLS) for a in arrs]
    used = sum(m.shape[0] for m in mats)
    return mats[:-1] + [jnp.concatenate([mats[-1], jnp.zeros((rows - used, PACK_COLS), dtype)], axis=0)]


STAGE_ROWS = 1024


def _stack_rows(mats, *, name):
    ncol, dtype = mats[0].shape[1], mats[0].dtype
    total = sum(m.shape[0] for m in mats)
    chunks, at = [], 0
    for i, m in enumerate(mats):
        for st in range(0, m.shape[0], STAGE_ROWS):
            sz = min(STAGE_ROWS, m.shape[0] - st)
            chunks.append((i, st, at + st, sz))
        at += m.shape[0]
    n = len(mats)

    def body(*refs):
        ins, out_ref, buf, sem_in, sem_out = refs[:n], refs[n], refs[n + 1], refs[n + 2], refs[n + 3]

        def put(idx):
            _, _, dst, sz = chunks[idx]
            return pltpu.make_async_copy(buf.at[idx % 2, pl.ds(0, sz), :], out_ref.at[pl.ds(dst, sz), :], sem_out.at[idx % 2])

        for idx, (i, st, _, sz) in enumerate(chunks):
            if idx >= 2:
                put(idx - 2).wait()
            get = pltpu.make_async_copy(ins[i].at[pl.ds(st, sz), :], buf.at[idx % 2, pl.ds(0, sz), :], sem_in.at[idx % 2])
            get.start()
            get.wait()
            put(idx).start()
        for idx in range(max(0, len(chunks) - 2), len(chunks)):
            put(idx).wait()

    return pl.pallas_call(
        body, name=name, out_shape=jax.ShapeDtypeStruct((total, ncol), dtype),
        in_specs=[ANY] * n, out_specs=ANY,
        scratch_shapes=[pltpu.VMEM((2, STAGE_ROWS, ncol), dtype), pltpu.SemaphoreType.DMA((2,)), pltpu.SemaphoreType.DMA((2,))],
    )(*mats)


def _join_cols(pieces, *, name):
    s, dtype = pieces[0].shape[0], pieces[0].dtype
    widths = [p.shape[1] for p in pieces]
    tr = _pick(s, 256, 16)

    def body(*refs):
        o_ref, at = refs[-1], 0
        for ref, w in zip(refs[:-1], widths):
            o_ref[:, at:at + w] = ref[...]
            at += w

    return pl.pallas_call(
        body, name=name, grid=(s // tr,),
        in_specs=[pl.BlockSpec((tr, w), lambda i: (i, 0)) for w in widths],
        out_specs=pl.BlockSpec((tr, sum(widths)), lambda i: (i, 0)),
        out_shape=jax.ShapeDtypeStruct((s, sum(widths)), dtype),
        compiler_params=_cp("parallel"),
    )(*pieces)


def _pack(arrs, rows, dtype, *, name):
    return _stack_rows(_pack_mats(arrs, rows, dtype), name=name)


def _unpack(packed, shapes):
    out, at = [], 0
    for sh in shapes:
        r = math.prod(sh) // PACK_COLS
        out.append(packed[at:at + r].reshape(sh))
        at += r
    return out


def _unpack_flat(flat, shapes):
    out, at = [], 0
    for sh in shapes:
        n = math.prod(sh)
        out.append(flat[at:at + n].reshape(sh))
        at += n
    return out


def _pack_rows(shapes):
    n = sum(math.prod(sh) for sh in shapes)
    return -(-n // (PACK_COLS * 1024)) * 1024


def _in_perm(w_in):
    z, xbc, dt, cq, ckv, kr, gates = jnp.split(w_in, list(np_cumsum(IN_SPLIT))[:-1], axis=1)
    return jnp.concatenate([z, xbc, gates, cq, ckv, kr, dt, jnp.zeros((w_in.shape[0], PROJ_W - sum(IN_SPLIT)), w_in.dtype)], axis=1)


def _in_unperm(g):
    z, xbc, gates, cq, ckv = (g[:, OFF_Z:OFF_XBC], g[:, OFF_XBC:OFF_GATES], g[:, OFF_GATES:OFF_CQ], g[:, OFF_CQ:OFF_CKV],
                              g[:, OFF_CKV:OFF_KRDT])
    kr = g[:, OFF_KRDT:OFF_KRDT + MLA_ROPE]
    dt = g[:, OFF_KRDT + MLA_ROPE:OFF_KRDT + MLA_ROPE + SSD_HEADS]
    return jnp.concatenate([z, xbc, dt, cq, ckv, kr, gates], axis=1)


def np_cumsum(sizes):
    out, t = [], 0
    for s in sizes:
        t += s
        out.append(t)
    return out


def _head_perm(w, first):
    r = w.shape[0]
    w3 = w.reshape(r, MLA_HEADS, -1)
    return jnp.concatenate([w3[:, :, :first].reshape(r, -1), w3[:, :, first:].reshape(r, -1)], axis=1)


def _head_unperm(g, first):
    r = g.shape[0]
    rest = g.shape[1] // MLA_HEADS - first
    a = g[:, :MLA_HEADS * first].reshape(r, MLA_HEADS, first)
    b = g[:, MLA_HEADS * first:].reshape(r, MLA_HEADS, rest)
    return jnp.concatenate([a, b], axis=2).reshape(r, -1)


def _layer_weights(full, l):
    row = lambda n: full[n][l][None].astype(F32)
    expand = jnp.repeat(jnp.eye(SSD_HEADS, dtype=F32), SSD_HEAD_DIM, axis=1)
    dtb, al, dsk = full["dt_bias"][l], full["a_log"][l], full["d_skip"][l]
    mixer = dict(
        ln_mix=row("ln_mix"), w_in=_in_perm(full["w_in"][l]), conv_w=full["conv_w"][l], conv_b=row("conv_b"),
        ssd_aux=(dtb[None], dtb[:, None], al[None], al[:, None], jnp.repeat(dsk, SSD_HEAD_DIM)[None], expand),
        ssd_norm=row("ssd_norm"), w_ssd_out=full["w_ssd_out"][l],
        q_lora_norm=row("q_lora_norm"), w_uq=_head_perm(full["w_uq"][l], MLA_NOPE),
        kv_lora_norm=row("kv_lora_norm"), w_ukv=_head_perm(full["w_ukv"][l], MLA_NOPE),
        qk_gains=_mla_gains(full["q_norm"][l], full["k_norm"][l]),
        w_mla_out=full["w_mla_out"][l], w_o=full["w_o"][l])
    ffn1 = dict(ln=row("ln_ffn1"), w13=full["ffn1_w13"][l], w2=full["ffn1_w2"][l])
    ffn2 = dict(ln=row("ln_ffn2"), w13=full["ffn2_w13"][l], w2=full["ffn2_w2"][l])
    return ffn1, mixer, ffn2


def _layer_grads(g1, gm, g2):
    return {
        "ln_ffn1": g1["ln"][0], "ffn1_w13": g1["w13"], "ffn1_w2": g1["w2"],
        "ln_mix": gm["ln_mix"][0], "w_in": _in_unperm(gm["w_in"]), "conv_w": gm["conv_w"], "conv_b": gm["conv_b"][0],
        "dt_bias": gm["dt_bias"], "a_log": gm["a_log"], "d_skip": gm["d_skip"], "ssd_norm": gm["ssd_norm"][0],
        "w_ssd_out": gm["w_ssd_out"], "q_lora_norm": gm["q_lora_norm"][0], "w_uq": _head_unperm(gm["w_uq"], MLA_NOPE),
        "kv_lora_norm": gm["kv_lora_norm"][0], "w_ukv": _head_unperm(gm["w_ukv"], MLA_NOPE),
        "q_norm": gm["q_norm"], "k_norm": gm["k_norm"], "w_mla_out": gm["w_mla_out"], "w_o": gm["w_o"],
        "ln_ffn2": g2["ln"][0], "ffn2_w13": g2["w13"], "ffn2_w2": g2["w2"],
    }


def _local_step(x, positions, loss_target, full):
    rope = _rope_tables4(positions)
    lw = [_layer_weights(full, l) for l in range(DEPTH)]
    h = x
    saved = []
    for l in range(DEPTH):
        f1, mx, f2 = lw[l]
        h, s1 = _ffn_fwd(h, f1, f"l{l}_ffn1")
        h, sm = _mixer_fwd(h, mx, rope, f"l{l}_mix")
        h, s2 = _ffn_fwd(h, f2, f"l{l}_ffn2")
        saved.append((s1, sm, s2))
    loss_part, dh = _loss_fwd_bwd(h, loss_target, name="loss")
    grads = [None] * DEPTH
    for l in reversed(range(DEPTH)):
        f1, mx, f2 = lw[l]
        s1, sm, s2 = saved[l]
        dh, g2 = _ffn_bwd(dh, s2, f2, f"l{l}_ffn2")
        dh, gm = _mixer_bwd(dh, sm, mx, rope, f"l{l}_mix")
        dh, g1 = _ffn_bwd(dh, s1, f1, f"l{l}_ffn1")
        grads[l] = _layer_grads(g1, gm, g2)
    full_grads = {n: jnp.stack([grads[l][n] for l in range(DEPTH)]) for n in W_NAMES}
    return loss_part, dh, full_grads


MESH = pl.DeviceIdType.MESH
ANY = pl.BlockSpec(memory_space=pl.ANY)


def _place():
    return lax.axis_index("x"), lax.axis_index("y"), lax.axis_index("c")


def _other_chips(x, y):
    return [(1 - x, y), (x, 1 - y), (1 - x, 1 - y)]


def _remote(src, dst, send_sems, recv_sems, k, to):
    return pltpu.make_async_remote_copy(src_ref=src, dst_ref=dst, send_sem=send_sems.at[k], recv_sem=recv_sems.at[k],
                                        device_id=to, device_id_type=MESH)


N_PARTS = 8


def _parts(rows):
    size = rows // N_PARTS
    assert size * N_PARTS == rows and size % 16 == 0, rows
    return [(p * size, size) for p in range(N_PARTS)]


def _rows(ref, lead, base, start, size):
    return ref.at[(*lead, pl.ds(pl.multiple_of(base + start, 16), size), slice(None))]


def _my_chip():
    return 2 * lax.axis_index("x") + lax.axis_index("y")


def _own_slot(packed, *, name):
    r, ncol = packed.shape
    tr = _pick(r, 512, 16)

    def body(x_ref, o_ref):
        o_ref[...] = x_ref[...]

    return pl.pallas_call(
        body, name=name, grid=(r // tr,),
        in_specs=[pl.BlockSpec((tr, ncol), lambda i: (i, 0))],
        out_specs=pl.BlockSpec((None, tr, ncol), lambda i: (_my_chip(), i, 0)),
        out_shape=jax.ShapeDtypeStruct((N_CHIPS, r, ncol), packed.dtype),
        compiler_params=_cp("arbitrary"),
    )(packed)


def _gather_shards(packed, slots, *, name):
    r, ncol = packed.shape
    hr = r // 2
    parts = _parts(hr)

    def body(x_ref, slots_ref, out_ref, send_sems, recv_sems):
        del slots_ref
        x, y, c = _place()
        chips = _other_chips(x, y)
        me = 2 * x + y

        def half(chip, cc):
            return _rows(out_ref, (2 * chip[0] + chip[1],), cc * hr, 0, hr)

        for j, chip in enumerate(chips):
            for st, sz in parts:
                _remote(_rows(x_ref, (), c * hr, st, sz), _rows(out_ref, (me,), c * hr, st, sz), send_sems, recv_sems, j,
                        (*chip, c)).start()
        for j, chip in enumerate(chips):
            _remote(half(chip, c), half(chip, c), send_sems, recv_sems, j, (x, y, c)).wait_recv()
            slot = 2 * chip[0] + chip[1]
            for st, sz in parts:
                _remote(_rows(out_ref, (slot,), c * hr, st, sz), _rows(out_ref, (slot,), c * hr, st, sz), send_sems,
                        recv_sems, 3 + j, (x, y, 1 - c)).start()
        for j, chip in enumerate(chips):
            _remote(half(chip, 1 - c), half(chip, 1 - c), send_sems, recv_sems, 3 + j, (x, y, c)).wait_recv()
        for k in range(6):
            _remote(half((x, y), c), half((x, y), c), send_sems, recv_sems, k, (x, y, c)).wait_send()

    return pl.pallas_call(
        body, name=name,
        out_shape=jax.ShapeDtypeStruct((N_CHIPS, r, ncol), packed.dtype),
        in_specs=[ANY, ANY], out_specs=ANY, input_output_aliases={1: 0},
        scratch_shapes=[pltpu.SemaphoreType.DMA((6,)), pltpu.SemaphoreType.DMA((6,))],
    )(packed, slots)


def _swap_halves(g, *, name):
    n, r, ncol = g.shape
    hr = r // 2
    parts = _parts(hr)

    def body(g_ref, got_ref, send_sems, recv_sems):
        x, y, c = _place()
        for s in range(n):
            for st, sz in parts:
                _remote(_rows(g_ref, (s,), (1 - c) * hr, st, sz), got_ref.at[s, pl.ds(st, sz), :], send_sems, recv_sems, 0,
                        (x, y, 1 - c)).start()
        _remote(got_ref, got_ref, send_sems, recv_sems, 0, (x, y, c)).wait()

    return pl.pallas_call(
        body, name=name, out_shape=jax.ShapeDtypeStruct((n, hr, ncol), g.dtype), in_specs=[ANY], out_specs=ANY,
        scratch_shapes=[pltpu.SemaphoreType.DMA((1,)), pltpu.SemaphoreType.DMA((1,))],
    )(g)


def _add_cores(g, got, *, name):
    n, r, ncol = g.shape
    hr = r // 2
    tr = _pick(hr, 512, 16)
    nb = hr // tr

    def body(a_ref, b_ref, o_ref):
        o_ref[...] = (a_ref[...].astype(F32) + b_ref[...].astype(F32)).astype(o_ref.dtype)

    blk = pl.BlockSpec((None, tr, ncol), lambda s, i: (s, i, 0))
    return pl.pallas_call(
        body, name=name, grid=(n, nb),
        in_specs=[pl.BlockSpec((None, tr, ncol), lambda s, i: (s, lax.axis_index("c") * nb + i, 0)), blk],
        out_specs=blk,
        out_shape=jax.ShapeDtypeStruct((n, hr, ncol), BF16),
        compiler_params=_cp("parallel", "parallel"),
    )(g, got)


def _scatter_to_chips(a, *, name):
    n, r, ncol = a.shape
    parts = _parts(r)

    def body(a_ref, got_ref, send_sems, recv_sems):
        x, y, c = _place()
        for st, sz in parts:
            for j, chip in enumerate(_other_chips(x, y)):
                _remote(a_ref.at[2 * chip[0] + chip[1], pl.ds(st, sz), :], got_ref.at[j, pl.ds(st, sz), :], send_sems,
                        recv_sems, j, (*chip, c)).start()
        for j in range(n - 1):
            _remote(got_ref.at[j], got_ref.at[j], send_sems, recv_sems, j, (x, y, c)).wait()

    return pl.pallas_call(
        body, name=name, out_shape=jax.ShapeDtypeStruct((n - 1, r, ncol), a.dtype), in_specs=[ANY], out_specs=ANY,
        scratch_shapes=[pltpu.SemaphoreType.DMA((3,)), pltpu.SemaphoreType.DMA((3,))],
    )(a)


def _add_chips(a, got, *, name):
    n, hr, ncol = a.shape
    tr = _pick(hr, 512, 16)
    nb = hr // tr

    def body(a_ref, g0_ref, g1_ref, g2_ref, o_ref):
        f = lambda ref: ref[...].astype(F32)
        o_ref[...] = ((f(a_ref) + f(g0_ref)) + f(g1_ref)) + f(g2_ref)

    other = lambda j: pl.BlockSpec((None, tr, ncol), lambda i: (j, i, 0))
    return pl.pallas_call(
        body, name=name, grid=(nb,),
        in_specs=[pl.BlockSpec((None, tr, ncol), lambda i: (_my_chip(), i, 0)), other(0), other(1), other(2)],
        out_specs=pl.BlockSpec((tr, ncol), lambda i: (lax.axis_index("c") * nb + i, 0)),
        out_shape=jax.ShapeDtypeStruct((2 * hr, ncol), F32),
        compiler_params=_cp("parallel"),
    )(a, got, got, got)


def _join_halves(buf, *, name):
    r, ncol = buf.shape
    hr = r // 2
    parts = _parts(hr)

    def body(b_ref, out_ref, send_sems, recv_sems):
        del b_ref
        x, y, c = _place()
        for st, sz in parts:
            _remote(_rows(out_ref, (), c * hr, st, sz), _rows(out_ref, (), c * hr, st, sz), send_sems, recv_sems, 0,
                    (x, y, 1 - c)).start()
        theirs = _rows(out_ref, (), (1 - c) * hr, 0, hr)
        _remote(theirs, theirs, send_sems, recv_sems, 0, (x, y, c)).wait()

    return pl.pallas_call(
        body, name=name, out_shape=jax.ShapeDtypeStruct((r, ncol), buf.dtype), in_specs=[ANY], out_specs=ANY,
        input_output_aliases={0: 0},
        scratch_shapes=[pltpu.SemaphoreType.DMA((1,)), pltpu.SemaphoreType.DMA((1,))],
    )(buf)


def _reduce_scatter(g, *, name):
    got = _swap_halves(g, name=name + "_swap")
    chip_sum = _add_cores(g, got, name=name + "_add_cores")
    others = _scatter_to_chips(chip_sum, name=name + "_scatter")
    return _join_halves(_add_chips(chip_sum, others, name=name + "_add_chips"), name=name + "_join")


def _all_gather_small(v, *, name):
    r, ncol = v.shape

    def body(x_ref, out_ref, send_sems, recv_sems, local_sem):
        x, y, c = _place()
        me, sibling = (x, y, c), (x, y, 1 - c)
        chips = _other_chips(x, y)

        def slot(p):
            return out_ref.at[4 * p[0] + 2 * p[1] + p[2]]

        mine = pltpu.make_async_copy(x_ref, slot(me), local_sem.at[0])
        mine.start()
        first = [_remote(x_ref, slot(me), send_sems, recv_sems, 0, sibling)]
        first += [_remote(x_ref, slot(me), send_sems, recv_sems, 1 + j, (*chip, c)) for j, chip in enumerate(chips)]
        for cp in first:
            cp.start()
        passed = [_remote(slot((*chip, c)), slot((*chip, c)), send_sems, recv_sems, 4 + j, sibling)
                  for j, chip in enumerate(chips)]
        for j, chip in enumerate(chips):
            _remote(slot((*chip, c)), slot((*chip, c)), send_sems, recv_sems, 1 + j, me).wait_recv()
            passed[j].start()
        _remote(slot(sibling), slot(sibling), send_sems, recv_sems, 0, me).wait_recv()
        for j, chip in enumerate(chips):
            _remote(slot((*chip, 1 - c)), slot((*chip, 1 - c)), send_sems, recv_sems, 4 + j, me).wait_recv()
        for cp in first + passed:
            cp.wait_send()
        mine.wait()

    vm = pl.BlockSpec(memory_space=pltpu.VMEM)
    return pl.pallas_call(
        body, name=name, out_shape=jax.ShapeDtypeStruct((N_DEV, r, ncol), v.dtype), in_specs=[vm], out_specs=vm,
        scratch_shapes=[pltpu.SemaphoreType.DMA((7,)), pltpu.SemaphoreType.DMA((7,)), pltpu.SemaphoreType.DMA((1,))],
    )(v)


def _sum_slots(g8, *, name):
    n, r, ncol = g8.shape

    def body(g_ref, o_ref):
        acc = g_ref[0]
        for k in range(1, n):
            acc = acc + g_ref[k]
        o_ref[...] = acc

    return pl.pallas_call(body, name=name, out_shape=jax.ShapeDtypeStruct((r, ncol), g8.dtype))(g8)


def _step(a):
    x = a["x"][0]
    s = x.shape[0]
    del s
    shard_shapes = [a[n].shape for n in SHARDED]
    rows = _pack_rows(shard_shapes)

    packed = _pack([a[n] for n in SHARDED], rows, BF16, name="pack_weights")
    gathered = _gather_shards(packed, _own_slot(packed, name="own_weights"), name="gather_weights")
    conv_rows = -(-math.prod(a["conv_w"].shape) // (LANE * 8)) * 8
    conv_all = _all_gather_small(
        jnp.pad(a["conv_w"].reshape(-1), (0, conv_rows * LANE - math.prod(a["conv_w"].shape))).reshape(conv_rows, LANE),
        name="gather_conv_w")
    per_chip = [dict(zip(SHARDED, _unpack(gathered[k], shard_shapes))) for k in range(N_CHIPS)]
    full = {n: jnp.concatenate([per_chip[k][n] for k in range(N_CHIPS)], axis=SHARD_AXIS[n]) for n in SHARDED}
    full["conv_w"] = jnp.concatenate(
        [conv_all[2 * k].reshape(-1)[:math.prod(a["conv_w"].shape)].reshape(a["conv_w"].shape) for k in range(N_CHIPS)],
        axis=SHARD_AXIS["conv_w"])
    for n in REPLICATED:
        full[n] = a[n]

    loss_part, grad_x, grads = _local_step(x, a["positions"][0], a["loss_target"][0], full)
    loss = lax.psum(jnp.sum(loss_part), ("x", "y", "c"))

    mats = []
    for k in range(N_CHIPS):
        parts = [jnp.split(grads[n], N_CHIPS, axis=SHARD_AXIS[n])[k] for n in SHARDED]
        mats += _pack_mats(parts, rows, BF16)
    g_slots = _stack_rows(mats, name="pack_grads").reshape(N_CHIPS, rows, PACK_COLS)
    g_shard = _reduce_scatter(g_slots, name="reduce_grads")

    rep_shapes = [a[n].shape for n in REPLICATED]
    n_rep = sum(math.prod(sh) for sh in rep_shapes)
    rep_rows = -(-n_rep // (LANE * 8)) * 8
    pack_small = lambda arrs: jnp.pad(jnp.concatenate([t.reshape(-1) for t in arrs]), (0, rep_rows * LANE - n_rep)).reshape(rep_rows, LANE)
    g_rep = _sum_slots(_all_gather_small(pack_small([grads[n] for n in REPLICATED]), name="gather_small_grads"),
                       name="add_small_grads")

    out = {"loss": loss, "grad_x": grad_x[None]}
    for n, g in zip(SHARDED, _unpack(g_shard, shard_shapes)):
        flat = lambda t: t.reshape(-1, t.shape[-1])
        d, nm, nv = _adamw(flat(a[n]), flat(g), flat(a["m_" + n]), flat(a["v_" + n]), name="adamw_" + n)
        out["grad_" + n] = g
        out["delta_" + n], out["new_m_" + n], out["new_v_" + n] = (t.reshape(g.shape) for t in (d, nm, nv))
    d_rp, m_rp, v_rp = _adamw(pack_small([a[n] for n in REPLICATED]), g_rep,
                              pack_small([a["m_" + n] for n in REPLICATED]),
                              pack_small([a["v_" + n] for n in REPLICATED]), name="adamw_replicated")
    for prefix, rp_arr in (("grad_", g_rep), ("delta_", d_rp), ("new_m_", m_rp), ("new_v_", v_rp)):
        for n, t in zip(REPLICATED, _unpack_flat(rp_arr.reshape(-1)[:n_rep], rep_shapes)):
            out[prefix + n] = t
    return out


IN_NAMES = ["x", "positions"] + W_NAMES + ["loss_target"] + ["m_" + n for n in W_NAMES] + ["v_" + n for n in W_NAMES]
OUT_NAMES = (["loss", "grad_x"] + ["grad_" + n for n in W_NAMES] + ["delta_" + n for n in W_NAMES]
             + ["new_m_" + n for n in W_NAMES] + ["new_v_" + n for n in W_NAMES])


def kernel(x, positions, ln_ffn1, ffn1_w13, ffn1_w2, ln_mix, w_in, conv_w, conv_b, dt_bias, a_log, d_skip, ssd_norm, w_ssd_out, q_lora_norm, w_uq, kv_lora_norm, w_ukv, q_norm, k_norm, w_mla_out, w_o, ln_ffn2, ffn2_w13, ffn2_w2, loss_target, m_ln_ffn1, m_ffn1_w13, m_ffn1_w2, m_ln_mix, m_w_in, m_conv_w, m_conv_b, m_dt_bias, m_a_log, m_d_skip, m_ssd_norm, m_w_ssd_out, m_q_lora_norm, m_w_uq, m_kv_lora_norm, m_w_ukv, m_q_norm, m_k_norm, m_w_mla_out, m_w_o, m_ln_ffn2, m_ffn2_w13, m_ffn2_w2, v_ln_ffn1, v_ffn1_w13, v_ffn1_w2, v_ln_mix, v_w_in, v_conv_w, v_conv_b, v_dt_bias, v_a_log, v_d_skip, v_ssd_norm, v_w_ssd_out, v_q_lora_norm, v_w_uq, v_kv_lora_norm, v_w_ukv, v_q_norm, v_k_norm, v_w_mla_out, v_w_o, v_ln_ffn2, v_ffn2_w13, v_ffn2_w2):
    given = locals()
    out = _step({n: given[n] for n in IN_NAMES})
    return tuple(out[n] for n in OUT_NAMES)
```

```python
import functools
import math

import jax
import jax.numpy as jnp
from jax import lax
from jax.experimental import pallas as pl
from jax.experimental.pallas import tpu as pltpu

F32 = jnp.float32
BF16 = jnp.bfloat16

D_MODEL = 1024
DEPTH = 2
D_FF = 2816
SSD_D_INNER = 2048
SSD_HEADS = 32
SSD_HEAD_DIM = 64
SSD_GROUPS = 4
SSD_STATE = 128
SSD_CHUNK = 128
SSD_CONV = 4
SSD_CONV_DIM = 3072
MLA_HEADS = 8
MLA_Q_LORA = 512
MLA_KV_LORA = 256
MLA_NOPE = 128
MLA_ROPE = 64
MLA_V = 128
MLA_QK = 192
ROPE_THETA = 10000.0
EPS = 1e-6
ADAM_LR = 0.001
ADAM_B1 = 0.9
ADAM_B2 = 0.999
ADAM_EPS = 1e-08
ADAM_WD = 0.01
ADAM_STEP = 10

PROJ_W = 8064
OFF_Z, OFF_XBC, OFF_GATES, OFF_CQ, OFF_CKV, OFF_KRDT = 0, 2048, 5120, 7168, 7680, 7936

LANE = 128
VMEM_LIMIT = 48 * 1024 * 1024
HI = lax.Precision.HIGHEST


def _cp(*sem):
    return pltpu.CompilerParams(dimension_semantics=sem, vmem_limit_bytes=VMEM_LIMIT)


def _pick(dim, target, align):
    if dim <= target:
        return dim
    b = (target // align) * align
    while b >= align:
        if dim % b == 0:
            return b
        b -= align
    raise ValueError(f"no block for {dim} (target {target}, align {align})")


def _silu(x):
    return x * jax.nn.sigmoid(x)


def _dsilu(x):
    s = jax.nn.sigmoid(x)
    return s * (1.0 + x * (1.0 - s))


MM_VMEM_BUDGET = 40 * 1024 * 1024


def _mm_tiles(m, n, k, a_bytes, b_bytes, o_bytes):
    bn = _pick(n, 1408, LANE)
    for nk in (1, 2, 3, 4, 6, 7, 8):
        if k % nk or (k // nk) % LANE:
            continue
        bk = k // nk
        for bm in (1024, 512):
            if m % bm:
                continue
            need = 2 * (bm * bk * a_bytes + bk * bn * b_bytes + bm * bn * o_bytes) + (bm * bn * 4 if nk > 1 else 0)
            if need <= MM_VMEM_BUDGET:
                return bm, bn, bk
    return _pick(m, 512, 8), bn, _pick(k, 1536, LANE)

def _matmul(a, b, mode, *, name, out_dtype=F32, scale=1.0, res=None):
    if mode == "nn":
        (m, k), (k2, n) = a.shape, b.shape
    elif mode == "nt":
        (m, k), (n, k2) = a.shape, b.shape
    else:
        (k, m), (k2, n) = a.shape, b.shape
    assert k == k2, (a.shape, b.shape, mode)
    if mode == "tn":
        bn, bk = _pick(n, 2816, LANE), _pick(k, 1024, 8)
        bm = _pick(m, max(256, (1408 * 1024 // bn) // LANE * LANE), LANE)
    else:
        bm, bn, bk = _mm_tiles(m, n, k, a.dtype.itemsize, b.dtype.itemsize,
                               jnp.dtype(out_dtype).itemsize + (4 if res is not None else 0))
    nk = k // bk

    def body(a_ref, b_ref, *rest):
        res_ref = rest[0] if res is not None else None
        o_ref = rest[-2] if nk > 1 else rest[-1]
        kk = pl.program_id(2)
        av = a_ref[...].astype(BF16)
        bv = b_ref[...].astype(BF16)
        if mode == "nn":
            dims = (((1,), (0,)), ((), ()))
        elif mode == "nt":
            dims = (((1,), (1,)), ((), ()))
        else:
            dims = (((0,), (0,)), ((), ()))
        part = lax.dot_general(av, bv, dims, preferred_element_type=F32)

        def finish(total):
            out = total * scale
            if res_ref is not None:
                out = res_ref[...] + out
            o_ref[...] = out.astype(o_ref.dtype)

        if nk == 1:
            finish(part)
            return
        acc_ref = rest[-1]

        @pl.when(kk == 0)
        def _():
            acc_ref[...] = part

        @pl.when((kk > 0) & (kk < nk - 1))
        def _():
            acc_ref[...] += part

        @pl.when(kk == nk - 1)
        def _():
            finish(acc_ref[...] + part)

    o_spec = pl.BlockSpec((bm, bn), lambda i, j, kk: (i, j))
    if mode == "nn":
        a_spec = pl.BlockSpec((bm, bk), lambda i, j, kk: (i, kk))
        b_spec = pl.BlockSpec((bk, bn), lambda i, j, kk: (kk, j))
    elif mode == "nt":
        a_spec = pl.BlockSpec((bm, bk), lambda i, j, kk: (i, kk))
        b_spec = pl.BlockSpec((bn, bk), lambda i, j, kk: (j, kk))
    else:
        a_spec = pl.BlockSpec((bk, bm), lambda i, j, kk: (kk, i))
        b_spec = pl.BlockSpec((bk, bn), lambda i, j, kk: (kk, j))
    return pl.pallas_call(
        body, name=name,
        grid=(m // bm, n // bn, nk),
        in_specs=[a_spec, b_spec] + ([o_spec] if res is not None else []),
        out_specs=o_spec,
        out_shape=jax.ShapeDtypeStruct((m, n), out_dtype),
        scratch_shapes=[pltpu.VMEM((bm, bn), F32)] if nk > 1 else [],
        compiler_params=_cp("parallel", "parallel", "arbitrary"),
    )(*((a, b) + ((res,) if res is not None else ())))


def _rms_fwd(x, g, *, name, col=0, width=None):
    r = x.shape[0]
    w = width or x.shape[1]
    tr = _pick(r, 512, 16)

    def body(x_ref, g_ref, o_ref):
        xv = x_ref[...]
        rs = lax.rsqrt(jnp.mean(xv * xv, axis=-1, keepdims=True) + EPS)
        o_ref[...] = (xv * rs * g_ref[...]).astype(o_ref.dtype)

    return pl.pallas_call(
        body, name=name, grid=(r // tr,),
        in_specs=[pl.BlockSpec((tr, w), lambda i: (i, col)), pl.BlockSpec((1, w), lambda i: (0, 0))],
        out_specs=pl.BlockSpec((tr, w), lambda i: (i, 0)),
        out_shape=jax.ShapeDtypeStruct((r, w), BF16),
        compiler_params=_cp("parallel"),
    )(x, g)


def _rms_bwd(x, g, dy, *, name, col=0, width=None, res=None, out_dtype=F32):
    r = x.shape[0]
    w = width or x.shape[1]
    tr = _pick(r, 512, 8)

    def body(x_ref, g_ref, dy_ref, *rest):
        res_ref = rest[0] if res is not None else None
        dx_ref, dg_ref = rest[-2:]
        i = pl.program_id(0)
        xv = x_ref[...]
        dyv = dy_ref[...]
        rs = lax.rsqrt(jnp.mean(xv * xv, axis=-1, keepdims=True) + EPS)
        xh = xv * rs
        dxh = dyv * g_ref[...]
        mm = jnp.mean(dxh * xh, axis=-1, keepdims=True)
        dx = rs * (dxh - xh * mm)
        if res_ref is not None:
            dx = res_ref[...] + dx
        dx_ref[...] = dx.astype(dx_ref.dtype)
        part = jnp.sum(dyv * xh, axis=0, keepdims=True)

        @pl.when(i == 0)
        def _():
            dg_ref[...] = part

        @pl.when(i > 0)
        def _():
            dg_ref[...] += part

    blk = pl.BlockSpec((tr, w), lambda i: (i, 0))
    return pl.pallas_call(
        body, name=name, grid=(r // tr,),
        in_specs=[pl.BlockSpec((tr, w), lambda i: (i, col)), pl.BlockSpec((1, w), lambda i: (0, 0)), blk]
        + ([blk] if res is not None else []),
        out_specs=[blk, pl.BlockSpec((1, w), lambda i: (0, 0))],
        out_shape=[jax.ShapeDtypeStruct((r, w), out_dtype), jax.ShapeDtypeStruct((1, w), F32)],
        compiler_params=_cp("arbitrary"),
    )(*((x, g, dy) + ((res,) if res is not None else ())))


def _gated_rms_fwd(y, proj, g, *, name):
    r, w = y.shape
    tr = _pick(r, 256, 8)

    def body(y_ref, z_ref, g_ref, o_ref):
        t = y_ref[...] * _silu(z_ref[...])
        rs = lax.rsqrt(jnp.mean(t * t, axis=-1, keepdims=True) + EPS)
        o_ref[...] = (t * rs * g_ref[...]).astype(o_ref.dtype)

    return pl.pallas_call(
        body, name=name, grid=(r // tr,),
        in_specs=[pl.BlockSpec((tr, w), lambda i: (i, 0)), pl.BlockSpec((tr, w), lambda i: (i, OFF_Z // w)),
                  pl.BlockSpec((1, w), lambda i: (0, 0))],
        out_specs=pl.BlockSpec((tr, w), lambda i: (i, 0)),
        out_shape=jax.ShapeDtypeStruct((r, w), BF16),
        compiler_params=_cp("parallel"),
    )(y, proj, g)


def _gated_rms_bwd(y, proj, g, do, *, name):
    r, w = y.shape
    tr = _pick(r, 256, 8)

    def body(y_ref, z_ref, g_ref, do_ref, dy_ref, dz_ref, dg_ref):
        i = pl.program_id(0)
        yv, zv, dov = y_ref[...], z_ref[...], do_ref[...]
        sg = jax.nn.sigmoid(zv)
        sz = zv * sg
        t = yv * sz
        rs = lax.rsqrt(jnp.mean(t * t, axis=-1, keepdims=True) + EPS)
        th = t * rs
        dth = dov * g_ref[...]
        mm = jnp.mean(dth * th, axis=-1, keepdims=True)
        dt = rs * (dth - th * mm)
        dy_ref[...] = dt * sz
        dz_ref[...] = (dt * yv * (sg * (1.0 + zv * (1.0 - sg)))).astype(dz_ref.dtype)
        part = jnp.sum(dov * th, axis=0, keepdims=True)

        @pl.when(i == 0)
        def _():
            dg_ref[...] = part

        @pl.when(i > 0)
        def _():
            dg_ref[...] += part

    blk = pl.BlockSpec((tr, w), lambda i: (i, 0))
    vec = pl.BlockSpec((1, w), lambda i: (0, 0))
    return pl.pallas_call(
        body, name=name, grid=(r // tr,),
        in_specs=[blk, pl.BlockSpec((tr, w), lambda i: (i, OFF_Z // w)), vec, blk],
        out_specs=[blk, blk, vec],
        out_shape=[jax.ShapeDtypeStruct((r, w), F32), jax.ShapeDtypeStruct((r, w), BF16),
                   jax.ShapeDtypeStruct((1, w), F32)],
        compiler_params=_cp("arbitrary"),
    )(y, proj, g, do)


def _ffn_up(n, w13, *, name):
    m, k = n.shape
    f = w13.shape[1] // 2
    bm, bn = _pick(m, 1024, 16), _pick(f, 1408, LANE)
    nj = f // bn

    def body(a_ref, wg_ref, wu_ref, act_ref, g_ref, u_ref):
        a = a_ref[...].astype(BF16)
        g = _dot(a, wg_ref[...].astype(BF16))
        u = _dot(a, wu_ref[...].astype(BF16))
        act_ref[...] = (_silu(g) * u).astype(act_ref.dtype)
        g_ref[...] = g.astype(g_ref.dtype)
        u_ref[...] = u.astype(u_ref.dtype)

    out = pl.BlockSpec((bm, bn), lambda j, i: (i, j))
    sh = jax.ShapeDtypeStruct((m, f), BF16)
    return pl.pallas_call(
        body, name=name, grid=(nj, m // bm),
        in_specs=[pl.BlockSpec((bm, k), lambda j, i: (i, 0)), pl.BlockSpec((k, bn), lambda j, i: (0, j)),
                  pl.BlockSpec((k, bn), lambda j, i: (0, nj + j))],
        out_specs=[out, out, out], out_shape=[sh, sh, sh],
        compiler_params=_cp("parallel", "parallel"),
    )(n, w13, w13)


def _swiglu_bwd(g, u, da, *, name):
    r, f = g.shape
    tr = _pick(r, 256, 16)

    def body(g_ref, u_ref, da_ref, o_ref):
        gv, uv, dav = g_ref[...].astype(F32), u_ref[...].astype(F32), da_ref[...].astype(F32)
        sg = jax.nn.sigmoid(gv)
        o_ref[:, :f] = (dav * uv * (sg * (1.0 + gv * (1.0 - sg)))).astype(o_ref.dtype)
        o_ref[:, f:] = (dav * (gv * sg)).astype(o_ref.dtype)

    blk = pl.BlockSpec((tr, f), lambda i: (i, 0))
    return pl.pallas_call(
        body, name=name, grid=(r // tr,),
        in_specs=[blk, blk, blk],
        out_specs=pl.BlockSpec((tr, 2 * f), lambda i: (i, 0)),
        out_shape=jax.ShapeDtypeStruct((r, 2 * f), BF16),
        compiler_params=_cp("parallel"),
    )(g, u, da)


CONV_TS = 1024
CONV_TC = 512


def _conv_pre(x, carry, w_ref, b_ref):
    ts = x.shape[0]
    row8 = lax.broadcasted_iota(jnp.int32, (8, x.shape[1]), 0)
    head_x = x[0:8]
    shifted, shifted_head = [], []
    for j in range(SSD_CONV):
        if j == 0:
            shifted.append(x)
            shifted_head.append(head_x)
        else:
            shifted.append(pltpu.roll(x, j, 0))
            shifted_head.append(jnp.where(row8 < j, pltpu.roll(carry, j, 0), pltpu.roll(head_x, j, 0)))
    pre = b_ref[...] + sum(w_ref[SSD_CONV - 1 - j:SSD_CONV - j, :] * shifted[j] for j in range(SSD_CONV))
    pre_head = b_ref[...] + sum(w_ref[SSD_CONV - 1 - j:SSD_CONV - j, :] * shifted_head[j] for j in range(SSD_CONV))
    del ts
    return pre, pre_head, shifted, shifted_head


def _conv_fwd(proj, w, b, *, name):
    s = proj.shape[0]
    c = w.shape[1]
    ts, tc = _pick(s, CONV_TS, 8), CONV_TC
    off = OFF_XBC // tc

    def body(x_ref, w_ref, b_ref, o_ref, carry_ref):
        t = pl.program_id(1)

        @pl.when(t == 0)
        def _():
            carry_ref[...] = jnp.zeros_like(carry_ref)

        x = x_ref[...]
        pre, pre_head, _, _ = _conv_pre(x, carry_ref[...], w_ref, b_ref)
        o_ref[...] = _silu(pre)
        o_ref[0:8, :] = _silu(pre_head)
        carry_ref[...] = x[ts - 8:ts]

    return pl.pallas_call(
        body, name=name, grid=(c // tc, s // ts),
        in_specs=[pl.BlockSpec((ts, tc), lambda j, t: (t, j + off)), pl.BlockSpec((SSD_CONV, tc), lambda j, t: (0, j)),
                  pl.BlockSpec((1, tc), lambda j, t: (0, j))],
        out_specs=pl.BlockSpec((ts, tc), lambda j, t: (t, j)),
        out_shape=jax.ShapeDtypeStruct((s, c), F32),
        scratch_shapes=[pltpu.VMEM((8, tc), F32)],
        compiler_params=_cp("parallel", "arbitrary"),
    )(proj, w, b)


def _conv_bwd_pre(proj, w, b, dy, *, name):
    s = proj.shape[0]
    c = w.shape[1]
    ts, tc = _pick(s, CONV_TS, 8), CONV_TC
    off = OFF_XBC // tc

    def body(x_ref, w_ref, b_ref, dy_ref, dp_ref, dw_ref, db_ref, carry_ref):
        t = pl.program_id(1)

        @pl.when(t == 0)
        def _():
            carry_ref[...] = jnp.zeros_like(carry_ref)
            dw_ref[...] = jnp.zeros_like(dw_ref)
            db_ref[...] = jnp.zeros_like(db_ref)

        x = x_ref[...]
        pre, pre_head, shifted, shifted_head = _conv_pre(x, carry_ref[...], w_ref, b_ref)
        dyv = dy_ref[...]
        dp = dyv * _dsilu(pre)
        dp_head = dyv[0:8] * _dsilu(pre_head)
        row = lax.broadcasted_iota(jnp.int32, dp.shape, 0)
        dp_tail = jnp.where(row >= 8, dp, 0.0)
        dp_ref[...] = dp
        dp_ref[0:8, :] = dp_head
        db_ref[...] += jnp.sum(dp_tail, axis=0, keepdims=True) + jnp.sum(dp_head, axis=0, keepdims=True)
        for j in range(SSD_CONV):
            kk = SSD_CONV - 1 - j
            dw_ref[kk:kk + 1, :] += (jnp.sum(dp_tail * shifted[j], axis=0, keepdims=True)
                                     + jnp.sum(dp_head * shifted_head[j], axis=0, keepdims=True))
        carry_ref[...] = x[ts - 8:ts]

    return pl.pallas_call(
        body, name=name, grid=(c // tc, s // ts),
        in_specs=[pl.BlockSpec((ts, tc), lambda j, t: (t, j + off)), pl.BlockSpec((SSD_CONV, tc), lambda j, t: (0, j)),
                  pl.BlockSpec((1, tc), lambda j, t: (0, j)), pl.BlockSpec((ts, tc), lambda j, t: (t, j))],
        out_specs=[pl.BlockSpec((ts, tc), lambda j, t: (t, j)), pl.BlockSpec((SSD_CONV, tc), lambda j, t: (0, j)),
                   pl.BlockSpec((1, tc), lambda j, t: (0, j))],
        out_shape=[jax.ShapeDtypeStruct((s, c), F32), jax.ShapeDtypeStruct((SSD_CONV, c), F32),
                   jax.ShapeDtypeStruct((1, c), F32)],
        scratch_shapes=[pltpu.VMEM((8, tc), F32)],
        compiler_params=_cp("parallel", "arbitrary"),
    )(proj, w, b, dy)


def _conv_bwd_x(dp, w, *, name):
    s, c = dp.shape
    ts, tc = _pick(s, CONV_TS, 8), CONV_TC
    nt = s // ts

    def body(d_ref, w_ref, o_ref, carry_ref, full_ref):
        t = pl.program_id(1)

        @pl.when(t == 0)
        def _():
            carry_ref[...] = jnp.zeros_like(carry_ref)

        d = d_ref[...]
        carry = carry_ref[...]
        row8 = lax.broadcasted_iota(jnp.int32, (8, tc), 0)
        tail = d[ts - 8:ts]
        acc = w_ref[SSD_CONV - 1:SSD_CONV, :] * d
        acc_tail = w_ref[SSD_CONV - 1:SSD_CONV, :] * tail
        for j in range(1, SSD_CONV):
            wj = w_ref[SSD_CONV - 1 - j:SSD_CONV - j, :]
            acc = acc + wj * pltpu.roll(d, ts - j, 0)
            up_tail = jnp.where(row8 >= 8 - j, pltpu.roll(carry, 8 - j, 0), pltpu.roll(tail, 8 - j, 0))
            acc_tail = acc_tail + wj * up_tail
        full_ref[...] = acc
        full_ref[ts - 8:ts, :] = acc_tail
        o_ref[...] = full_ref[...].astype(o_ref.dtype)
        carry_ref[...] = d[0:8]

    return pl.pallas_call(
        body, name=name, grid=(c // tc, nt),
        in_specs=[pl.BlockSpec((ts, tc), lambda j, t: (nt - 1 - t, j)), pl.BlockSpec((SSD_CONV, tc), lambda j, t: (0, j))],
        out_specs=pl.BlockSpec((ts, tc), lambda j, t: (nt - 1 - t, j)),
        out_shape=jax.ShapeDtypeStruct((s, c), BF16),
        scratch_shapes=[pltpu.VMEM((8, tc), F32), pltpu.VMEM((ts, tc), F32)],
        compiler_params=_cp("parallel", "arbitrary"),
    )(dp, w)


def _merge_fwd(proj, ys, ym, *, name):
    r, w = ys.shape
    tr = _pick(r, 512, 8)
    off = OFF_GATES // w

    def body(g1_ref, g2_ref, ys_ref, ym_ref, o_ref):
        o_ref[...] = (jax.nn.sigmoid(g1_ref[...]) * ys_ref[...]
                      + jax.nn.sigmoid(g2_ref[...]) * ym_ref[...]).astype(o_ref.dtype)

    blk = pl.BlockSpec((tr, w), lambda i: (i, 0))
    return pl.pallas_call(
        body, name=name, grid=(r // tr,),
        in_specs=[pl.BlockSpec((tr, w), lambda i: (i, off)), pl.BlockSpec((tr, w), lambda i: (i, off + 1)), blk, blk],
        out_specs=blk, out_shape=jax.ShapeDtypeStruct((r, w), BF16),
        compiler_params=_cp("parallel"),
    )(proj, proj, ys, ym)


def _merge_bwd(proj, ys, ym, dm, *, name):
    r, w = ys.shape
    tr = _pick(r, 512, 8)
    off = OFF_GATES // w

    def body(g1_ref, g2_ref, ys_ref, ym_ref, dm_ref, dg_ref, dys_ref, dym_ref):
        s1, s2 = jax.nn.sigmoid(g1_ref[...]), jax.nn.sigmoid(g2_ref[...])
        dmv = dm_ref[...]
        dys_ref[...] = (dmv * s1).astype(dys_ref.dtype)
        dym_ref[...] = (dmv * s2).astype(dym_ref.dtype)
        dg_ref[:, :w] = (dmv * ys_ref[...] * s1 * (1.0 - s1)).astype(dg_ref.dtype)
        dg_ref[:, w:] = (dmv * ym_ref[...] * s2 * (1.0 - s2)).astype(dg_ref.dtype)

    blk = pl.BlockSpec((tr, w), lambda i: (i, 0))
    return pl.pallas_call(
        body, name=name, grid=(r // tr,),
        in_specs=[pl.BlockSpec((tr, w), lambda i: (i, off)), pl.BlockSpec((tr, w), lambda i: (i, off + 1)), blk, blk, blk],
        out_specs=[pl.BlockSpec((tr, 2 * w), lambda i: (i, 0)), blk, blk],
        out_shape=[jax.ShapeDtypeStruct((r, 2 * w), BF16), jax.ShapeDtypeStruct((r, w), BF16),
                   jax.ShapeDtypeStruct((r, w), BF16)],
        compiler_params=_cp("parallel"),
    )(proj, proj, ys, ym, dm)


def _loss_fwd_bwd(y, target, *, name):
    r, w = y.shape
    tr = _pick(r, 512, 8)

    def body(y_ref, t_ref, l_ref, dy_ref):
        i = pl.program_id(0)
        e = y_ref[...] - t_ref[...]
        dy_ref[...] = e * (1.0 / w)
        part = jnp.sum(e * e, axis=0, keepdims=True) * (0.5 / w)

        @pl.when(i == 0)
        def _():
            l_ref[...] = part

        @pl.when(i > 0)
        def _():
            l_ref[...] += part

    blk = pl.BlockSpec((tr, w), lambda i: (i, 0))
    return pl.pallas_call(
        body, name=name, grid=(r // tr,),
        in_specs=[blk, blk],
        out_specs=[pl.BlockSpec((1, w), lambda i: (0, 0)), blk],
        out_shape=[jax.ShapeDtypeStruct((1, w), F32), jax.ShapeDtypeStruct((r, w), F32)],
        compiler_params=_cp("arbitrary"),
    )(y, target)


def _adamw(w, g, m, v, *, name):
    r, c = w.shape
    tr = _pick(r, max(8, (1 << 20) // (4 * c) // 8 * 8), 8)
    c1 = 1.0 - ADAM_B1 ** ADAM_STEP
    c2 = 1.0 - ADAM_B2 ** ADAM_STEP

    def body(w_ref, g_ref, m_ref, v_ref, d_ref, nm_ref, nv_ref):
        gv = g_ref[...]
        nm = ADAM_B1 * m_ref[...] + (1.0 - ADAM_B1) * gv
        nv = ADAM_B2 * v_ref[...] + (1.0 - ADAM_B2) * (gv * gv)
        nm_ref[...] = nm
        nv_ref[...] = nv
        d_ref[...] = -ADAM_LR * ((nm / c1) / (jnp.sqrt(nv / c2) + ADAM_EPS) + ADAM_WD * w_ref[...])

    blk = pl.BlockSpec((tr, c), lambda i: (i, 0))
    sh = jax.ShapeDtypeStruct((r, c), F32)
    return pl.pallas_call(
        body, name=name, grid=(r // tr,),
        in_specs=[blk] * 4, out_specs=[blk] * 3, out_shape=[sh] * 3,
        compiler_params=_cp("parallel"),
    )(w, g, m, v)


def _softplus(x):
    return jnp.maximum(x, 0.0) + jnp.log(1.0 + jnp.exp(-jnp.abs(x)))


def _dot_01(x, sel):
    sel_b = sel.astype(BF16)
    acc, rem = None, x
    for _ in range(3):
        piece = rem.astype(BF16)
        part = jnp.dot(piece, sel_b, preferred_element_type=F32)
        acc = part if acc is None else acc + part
        rem = rem - piece.astype(F32)
    return acc


def _ssd_common(dtr_ref, dtrT_ref, dtb_ref, dtbT_ref, al_ref, alT_ref, e_ref):
    L = SSD_CHUNK
    ri = lax.broadcasted_iota(jnp.int32, (L, L), 0)
    cj = lax.broadcasted_iota(jnp.int32, (L, L), 1)
    tril = (ri >= cj).astype(F32)
    triu = (ri <= cj).astype(F32)
    a = -jnp.exp(al_ref[...])
    aT = -jnp.exp(alT_ref[...])
    pre = dtr_ref[...] + dtb_ref[...]
    preT = dtrT_ref[...] + dtbT_ref[...]
    dt = _softplus(pre)
    dtT = _softplus(preT)
    acum = jnp.dot(tril, dt * a, precision=HI, preferred_element_type=F32)
    acumT = jnp.dot(dtT * aT, triu, precision=HI, preferred_element_type=F32)
    e = e_ref[...]
    dt_x = _dot_01(dt, e)
    acum_x = _dot_01(acum, e)
    last_x = acum_x[L - 1:L, :]
    return dict(ri=ri, cj=cj, tril=tril, triu=triu, a=a, aT=aT, pre=pre, preT=preT, dt=dt, dtT=dtT,
                acum=acum, acumT=acumT, dt_x=dt_x, eacum_x=jnp.exp(acum_x), w_x=jnp.exp(last_x - acum_x),
                elast_x=jnp.exp(last_x))


def _dot_nt(a, b):
    return lax.dot_general(a, b, (((1,), (1,)), ((), ())), preferred_element_type=F32)


def _dot_tn(a, b):
    return lax.dot_general(a, b, (((0,), (0,)), ((), ())), preferred_element_type=F32)


def _dot(a, b):
    return jnp.dot(a, b, preferred_element_type=F32)


def _ssd_specs(nc, rev):
    L = SSD_CHUNK
    ix = (lambda c: nc - 1 - c) if rev else (lambda c: c)
    return [
        pl.BlockSpec((L, SSD_D_INNER), lambda c: (ix(c), 0)),
        pl.BlockSpec((L, 512), lambda c: (ix(c), 4)),
        pl.BlockSpec((L, 512), lambda c: (ix(c), 5)),
        pl.BlockSpec((L, SSD_HEADS), lambda c: (ix(c), 0)),
        pl.BlockSpec((SSD_HEADS, L), lambda c: (0, ix(c))),
        pl.BlockSpec((1, SSD_HEADS), lambda c: (0, 0)),
        pl.BlockSpec((SSD_HEADS, 1), lambda c: (0, 0)),
        pl.BlockSpec((1, SSD_HEADS), lambda c: (0, 0)),
        pl.BlockSpec((SSD_HEADS, 1), lambda c: (0, 0)),
        pl.BlockSpec((1, SSD_D_INNER), lambda c: (0, 0)),
        pl.BlockSpec((SSD_HEADS, SSD_D_INNER), lambda c: (0, 0)),
    ]


def _ssd_fwd(xc, dtr, dtrT, dtb, dtbT, alog, alogT, dskx, expand, *, name):
    s = xc.shape[0]
    L = SSD_CHUNK
    nc = s // L

    def body(x_ref, b_ref, c_ref, dtr_ref, dtrT_ref, dtb_ref, dtbT_ref, al_ref, alT_ref, dsk_ref, e_ref,
             y_ref, st_ref, state):
        ci = pl.program_id(0)

        @pl.when(ci == 0)
        def _():
            state[...] = jnp.zeros_like(state)

        st_ref[0] = state[...]
        q = _ssd_common(dtr_ref, dtrT_ref, dtb_ref, dtbT_ref, al_ref, alT_ref, e_ref)
        causal = q["ri"] >= q["cj"]
        lane_lo = q["cj"] < 64
        x = x_ref[...]
        xdt = x * q["dt_x"]
        xdt_b = xdt.astype(BF16)
        xdtw_b = (xdt * q["w_x"]).astype(BF16)
        for g in range(SSD_GROUPS):
            bg = b_ref[:, 128 * g:128 * g + 128]
            cg_b = c_ref[:, 128 * g:128 * g + 128].astype(BF16)
            cb = _dot_nt(cg_b, bg.astype(BF16))
            bgT_b = bg.T.astype(BF16)
            s0 = state[g]
            for jj in range(4):
                j = 4 * g + jj
                sl = slice(128 * j, 128 * j + 128)
                sls = slice(128 * jj, 128 * jj + 128)
                ms = []
                for h in (2 * j, 2 * j + 1):
                    seg = q["acum"][:, h:h + 1] - q["acumT"][h:h + 1, :]
                    decay = jnp.exp(jnp.where(causal, seg, -jnp.inf))
                    ms.append((cb * decay).astype(BF16))
                mcat = jnp.concatenate(ms, axis=1)
                xp = xdt_b[:, sl]
                zero = jnp.zeros_like(xp)
                xstack = jnp.concatenate([jnp.where(lane_lo, xp, zero), jnp.where(lane_lo, zero, xp)], axis=0)
                y = _dot(mcat, xstack)
                y = y + q["eacum_x"][:, sl] * _dot(cg_b, s0[:, sls].astype(BF16))
                y = y + x[:, sl] * dsk_ref[:, sl]
                y_ref[:, sl] = y
                state[g, :, sls] = s0[:, sls] * q["elast_x"][:, sl] + _dot(bgT_b, xdtw_b[:, sl])

    return pl.pallas_call(
        body, name=name, grid=(nc,),
        in_specs=_ssd_specs(nc, False),
        out_specs=[pl.BlockSpec((L, SSD_D_INNER), lambda c: (c, 0)),
                   pl.BlockSpec((1, SSD_GROUPS, SSD_STATE, 512), lambda c: (c, 0, 0, 0))],
        out_shape=[jax.ShapeDtypeStruct((s, SSD_D_INNER), F32),
                   jax.ShapeDtypeStruct((nc, SSD_GROUPS, SSD_STATE, 512), F32)],
        scratch_shapes=[pltpu.VMEM((SSD_GROUPS, SSD_STATE, 512), F32)],
        compiler_params=_cp("arbitrary"),
    )(xc, xc, xc, dtr, dtrT, dtb, dtbT, alog, alogT, dskx, expand)


def _ssd_bwd(xc, dtr, dtrT, dtb, dtbT, alog, alogT, dskx, expand, expandT, states, dy, *, name):
    s = xc.shape[0]
    L = SSD_CHUNK
    H = SSD_HEADS
    nc = s // L

    def body(x_ref, b_ref, c_ref, dtr_ref, dtrT_ref, dtb_ref, dtbT_ref, al_ref, alT_ref, dsk_ref, e_ref,
             et_ref, st_ref, dy_ref,
             dxc_ref, ddtc_ref, ddtr_ref, dbc_ref, dbr_ref, dac_ref, dar_ref, ddsk_ref, dstate):
        ci = pl.program_id(0)

        @pl.when(ci == 0)
        def _():
            dstate[...] = jnp.zeros_like(dstate)
            dbc_ref[...] = jnp.zeros_like(dbc_ref)
            dbr_ref[...] = jnp.zeros_like(dbr_ref)
            dac_ref[...] = jnp.zeros_like(dac_ref)
            dar_ref[...] = jnp.zeros_like(dar_ref)
            ddsk_ref[...] = jnp.zeros_like(ddsk_ref)

        q = _ssd_common(dtr_ref, dtrT_ref, dtb_ref, dtbT_ref, al_ref, alT_ref, e_ref)
        ri, cj = q["ri"], q["cj"]
        causal = ri >= cj
        causalT = ri <= cj
        lane_lo = cj < 64
        lane_h = lax.broadcasted_iota(jnp.int32, (1, H), 1)
        sub_h = lax.broadcasted_iota(jnp.int32, (H, 1), 0)
        x = x_ref[...]
        dyv = dy_ref[...]
        xdt = x * q["dt_x"]
        xdt_b = xdt.astype(BF16)
        xdtw = xdt * q["w_x"]
        xdtw_b = xdtw.astype(BF16)
        edy = q["eacum_x"] * dyv
        edy_b = edy.astype(BF16)
        dyv_b = dyv.astype(BF16)
        dacum_col = jnp.zeros((L, H), F32)
        dacum_row = jnp.zeros((H, L), F32)
        dxdt_t, yoff_t, u_t, r_t = [], [], [], []
        for g in range(SSD_GROUPS):
            bg = b_ref[:, 128 * g:128 * g + 128]
            cg = c_ref[:, 128 * g:128 * g + 128]
            bg_b, cg_b = bg.astype(BF16), cg.astype(BF16)
            cb = _dot_nt(cg_b, bg_b)
            cbT = _dot_nt(bg_b, cg_b)
            cgT_b = cg.T.astype(BF16)
            s0 = st_ref[0, g]
            ds = dstate[g]
            s0_b, ds_b = s0.astype(BF16), ds.astype(BF16)
            dcb = jnp.zeros((L, L), F32)
            for jj in range(4):
                j = 4 * g + jj
                sl = slice(128 * j, 128 * j + 128)
                sls = slice(128 * jj, 128 * jj + 128)
                decs, mts = [], []
                for h in (2 * j, 2 * j + 1):
                    seg = q["acum"][:, h:h + 1] - q["acumT"][h:h + 1, :]
                    decs.append(jnp.exp(jnp.where(causal, seg, -jnp.inf)))
                    mts.append((cbT * jnp.exp(jnp.where(causalT, -seg, -jnp.inf))).astype(BF16))
                dyt_b = dyv_b[:, sl]
                zero = jnp.zeros_like(dyt_b)
                dystack = jnp.concatenate([jnp.where(lane_lo, dyt_b, zero), jnp.where(lane_lo, zero, dyt_b)], axis=0)
                dxs = _dot(jnp.concatenate(mts, axis=0), dyt_b)
                dxdt = jnp.where(lane_lo, dxs[:L], dxs[L:])
                dmcat = _dot_nt(dystack, xdt_b[:, sl])
                for idx, h in enumerate((2 * j, 2 * j + 1)):
                    dm = dmcat[L * idx:L * idx + L]
                    dcb = dcb + dm * decs[idx]
                    dseg = dm * cb * decs[idx]
                    dacum_col = dacum_col + jnp.sum(dseg, axis=1, keepdims=True) * (lane_h == h).astype(F32)
                    dacum_row = dacum_row - (sub_h == h).astype(F32) * jnp.sum(dseg, axis=0, keepdims=True)
                gmat = _dot(cg_b, s0_b[:, sls])
                yoff_t.append(edy[:, sl] * gmat)
                qm = _dot(bg_b, ds_b[:, sls])
                dxdt_t.append(dxdt + qm * q["w_x"][:, sl])
                u_t.append(qm * xdtw[:, sl])
                r_t.append(ds[:, sls] * s0[:, sls] * q["elast_x"][:, sl])
                dstate[g, :, sls] = ds[:, sls] * q["elast_x"][:, sl] + _dot(cgT_b, edy_b[:, sl])
            gsl = slice(512 * g, 512 * g + 512)
            dcb_b = dcb.astype(BF16)
            dcg = _dot(dcb_b, bg_b) + _dot_nt(edy_b[:, gsl], s0_b)
            dbg = _dot(dcb.T.astype(BF16), cg_b) + _dot_nt(xdtw_b[:, gsl], ds_b)
            dxc_ref[:, SSD_D_INNER + 128 * g:SSD_D_INNER + 128 * g + 128] = dbg
            dxc_ref[:, SSD_D_INNER + 512 + 128 * g:SSD_D_INNER + 512 + 128 * g + 128] = dcg
        et = et_ref[...]
        dxdt_all = jnp.concatenate(dxdt_t, axis=1)
        yoff = jnp.concatenate(yoff_t, axis=1)
        uu = jnp.concatenate(u_t, axis=1)
        rr = jnp.concatenate(r_t, axis=1)
        dacum_col = dacum_col + _dot_01(yoff - uu, et)
        dlast = jnp.sum(_dot_01(uu + rr, et), axis=0, keepdims=True)
        row_lh = lax.broadcasted_iota(jnp.int32, (L, H), 0)
        dacum_col = dacum_col + jnp.where(row_lh == L - 1, dlast, 0.0)
        d_dta_col = jnp.dot(q["triu"], dacum_col, precision=HI, preferred_element_type=F32)
        d_dta_row = jnp.dot(dacum_row, q["tril"], precision=HI, preferred_element_type=F32)
        ddt_col = d_dta_col * q["a"] + _dot_01(dxdt_all * x, et)
        ddt_row = d_dta_row * q["aT"]
        ddtr_col = ddt_col * jax.nn.sigmoid(q["pre"])
        ddtr_row = ddt_row * jax.nn.sigmoid(q["preT"])
        ddtc_ref[...] = ddtr_col
        ddtr_ref[...] = ddtr_row
        dac_ref[...] += jnp.sum(d_dta_col * q["dt"], axis=0, keepdims=True)
        dar_ref[...] += jnp.sum(d_dta_row * q["dtT"], axis=1, keepdims=True)
        dbc_ref[...] += jnp.sum(ddtr_col, axis=0, keepdims=True)
        dbr_ref[...] += jnp.sum(ddtr_row, axis=1, keepdims=True)
        ddsk_ref[...] += jnp.sum(dyv * x, axis=0, keepdims=True)
        dxc_ref[:, 0:SSD_D_INNER] = dxdt_all * q["dt_x"] + dyv * dsk_ref[...]

    rv = lambda c: nc - 1 - c
    in_specs = _ssd_specs(nc, True) + [
        pl.BlockSpec((SSD_D_INNER, H), lambda c: (0, 0)),
        pl.BlockSpec((1, SSD_GROUPS, SSD_STATE, 512), lambda c: (rv(c), 0, 0, 0)),
        pl.BlockSpec((L, SSD_D_INNER), lambda c: (rv(c), 0)),
    ]
    vec_c = pl.BlockSpec((1, H), lambda c: (0, 0))
    vec_r = pl.BlockSpec((H, 1), lambda c: (0, 0))
    return pl.pallas_call(
        body, name=name, grid=(nc,),
        in_specs=in_specs,
        out_specs=[pl.BlockSpec((L, SSD_CONV_DIM), lambda c: (rv(c), 0)),
                   pl.BlockSpec((L, H), lambda c: (rv(c), 0)),
                   pl.BlockSpec((H, L), lambda c: (0, rv(c))),
                   vec_c, vec_r, vec_c, vec_r,
                   pl.BlockSpec((1, SSD_D_INNER), lambda c: (0, 0))],
        out_shape=[jax.ShapeDtypeStruct((s, SSD_CONV_DIM), F32),
                   jax.ShapeDtypeStruct((s, H), F32), jax.ShapeDtypeStruct((H, s), F32),
                   jax.ShapeDtypeStruct((1, H), F32), jax.ShapeDtypeStruct((H, 1), F32),
                   jax.ShapeDtypeStruct((1, H), F32), jax.ShapeDtypeStruct((H, 1), F32),
                   jax.ShapeDtypeStruct((1, SSD_D_INNER), F32)],
        scratch_shapes=[pltpu.VMEM((SSD_GROUPS, SSD_STATE, 512), F32)],
        compiler_params=_cp("arbitrary"),
    )(xc, xc, xc, dtr, dtrT, dtb, dtbT, alog, alogT, dskx, expand, expandT, states, dy)


QK_PAD = 256
MLA_TS = 512


def _rope_tables4(pos):
    inv = 1.0 / (ROPE_THETA ** (jnp.arange(0, MLA_ROPE, 2, dtype=F32) / MLA_ROPE))
    ang = pos.astype(F32)[:, None] * inv
    c, s = jnp.cos(ang), jnp.sin(ang)
    return jnp.tile(c, (1, 4)), jnp.concatenate([-s, s, -s, s], axis=1)


def _mla_gains(qg, kg):
    z = jnp.zeros((LANE - MLA_ROPE,), F32)
    return (qg[:MLA_NOPE][None], jnp.concatenate([qg[MLA_NOPE:], z])[None],
            kg[:MLA_NOPE][None], jnp.concatenate([kg[MLA_NOPE:], z])[None])


def _rope_swap(t, first):
    return jnp.where(first, pltpu.roll(t, 96, 1), pltpu.roll(t, 32, 1))


def _mla_prep_specs(ts):
    row = lambda w, c=0: pl.BlockSpec((ts, w), lambda i: (i, c))
    vec = pl.BlockSpec((1, LANE), lambda i: (0, 0))
    return [row(MLA_HEADS * MLA_QK), row(2 * MLA_HEADS * MLA_NOPE), row(LANE, OFF_KRDT // LANE), row(LANE), row(LANE),
            vec, vec, vec, vec]


def _mla_prep_fwd(qraw, kvraw, proj, cos4, sin4, gqn, gqr, gkn, gkr, *, name):
    s = qraw.shape[0]
    ts = _pick(s, MLA_TS, 8)

    def body(q_ref, kv_ref, kr_ref, cos_ref, sin_ref, gqn_ref, gqr_ref, gkn_ref, gkr_ref, qo_ref, ko_ref):
        lane = lax.broadcasted_iota(jnp.int32, (ts, LANE), 1)
        lo = lane < 64
        first = (lane % 64) < 32
        cos, sin = cos_ref[...], sin_ref[...]
        kr = jnp.where(lo, kr_ref[...], 0.0)
        ssq_kr = jnp.sum(kr * kr, axis=-1, keepdims=True)

        def head(xn, xr, ssq_r, gn, gr):
            rs = lax.rsqrt((jnp.sum(xn * xn, axis=-1, keepdims=True) + ssq_r) * (1.0 / MLA_QK) + EPS)
            yr = xr * rs * gr
            return xn * rs * gn, yr * cos + _rope_swap(yr, first) * sin

        for h in range(MLA_HEADS):
            tile = q_ref[:, MLA_HEADS * MLA_NOPE + LANE * (h // 2):MLA_HEADS * MLA_NOPE + LANE * (h // 2) + LANE]
            qr = jnp.where(lo, tile if h % 2 == 0 else pltpu.roll(tile, 64, 1), 0.0)
            on, orr = head(q_ref[:, LANE * h:LANE * h + LANE], qr, jnp.sum(qr * qr, axis=-1, keepdims=True),
                           gqn_ref[...], gqr_ref[...])
            qo_ref[h, :, 0:LANE] = (on * ATT_SCALE).astype(BF16)
            qo_ref[h, :, LANE:QK_PAD] = (orr * ATT_SCALE).astype(BF16)
            on, orr = head(kv_ref[:, LANE * h:LANE * h + LANE], kr, ssq_kr, gkn_ref[...], gkr_ref[...])
            ko_ref[h, :, 0:LANE] = on.astype(BF16)
            ko_ref[h, :, LANE:QK_PAD] = orr.astype(BF16)

    out = pl.BlockSpec((MLA_HEADS, ts, QK_PAD), lambda i: (0, i, 0))
    sh = jax.ShapeDtypeStruct((MLA_HEADS, s, QK_PAD), BF16)
    return pl.pallas_call(
        body, name=name, grid=(s // ts,),
        in_specs=_mla_prep_specs(ts), out_specs=[out, out], out_shape=[sh, sh],
        compiler_params=_cp("parallel"),
    )(qraw, kvraw, proj, cos4, sin4, gqn, gqr, gkn, gkr)


def _mla_prep_bwd(qraw, kvraw, proj, cos4, sin4, gqn, gqr, gkn, gkr, dq, dk, *, name):
    s = qraw.shape[0]
    ts = _pick(s, MLA_TS, 8)

    def body(q_ref, kv_ref, kr_ref, cos_ref, sin_ref, gqn_ref, gqr_ref, gkn_ref, gkr_ref, dq_ref, dk_ref,
             dqraw_ref, dkn_ref, dkr_ref, dgqn_ref, dgqr_ref, dgkn_ref, dgkr_ref):
        i = pl.program_id(0)

        @pl.when(i == 0)
        def _():
            for r in (dgqn_ref, dgqr_ref, dgkn_ref, dgkr_ref):
                r[...] = jnp.zeros_like(r)

        lane = lax.broadcasted_iota(jnp.int32, (ts, LANE), 1)
        lo = lane < 64
        first = (lane % 64) < 32
        cos, sin = cos_ref[...], sin_ref[...]
        kr = jnp.where(lo, kr_ref[...], 0.0)
        ssq_kr = jnp.sum(kr * kr, axis=-1, keepdims=True)

        def head(xn, xr, ssq_r, gn, gr, don, dor):
            rs = lax.rsqrt((jnp.sum(xn * xn, axis=-1, keepdims=True) + ssq_r) * (1.0 / MLA_QK) + EPS)
            xhn, xhr = xn * rs, xr * rs
            dor = jnp.where(lo, dor, 0.0)
            dyr = dor * cos + _rope_swap(dor * sin, first)
            dxn, dxr = don * gn, dyr * gr
            mm = (jnp.sum(dxn * xhn, axis=-1, keepdims=True) + jnp.sum(dxr * xhr, axis=-1, keepdims=True)) * (1.0 / MLA_QK)
            return (rs * (dxn - xhn * mm), rs * (dxr - xhr * mm),
                    jnp.sum(don * xhn, axis=0, keepdims=True), jnp.sum(dyr * xhr, axis=0, keepdims=True))

        dkr_acc = jnp.zeros((ts, LANE), F32)
        prev = None
        for h in range(MLA_HEADS):
            c0 = MLA_HEADS * MLA_NOPE + LANE * (h // 2)
            tile = q_ref[:, c0:c0 + LANE]
            qr = jnp.where(lo, tile if h % 2 == 0 else pltpu.roll(tile, 64, 1), 0.0)
            dn, dr, gn_p, gr_p = head(q_ref[:, LANE * h:LANE * h + LANE], qr, jnp.sum(qr * qr, axis=-1, keepdims=True),
                                      gqn_ref[...], gqr_ref[...], dq_ref[h, :, 0:LANE], dq_ref[h, :, LANE:QK_PAD])
            dqraw_ref[:, LANE * h:LANE * h + LANE] = dn.astype(dqraw_ref.dtype)
            dgqn_ref[...] += gn_p
            dgqr_ref[...] += gr_p
            if h % 2 == 0:
                prev = dr
            else:
                dqraw_ref[:, c0:c0 + LANE] = (prev + pltpu.roll(dr, 64, 1)).astype(dqraw_ref.dtype)
            dn, dr, gn_p, gr_p = head(kv_ref[:, LANE * h:LANE * h + LANE], kr, ssq_kr, gkn_ref[...], gkr_ref[...],
                                      dk_ref[h, :, 0:LANE], dk_ref[h, :, LANE:QK_PAD])
            dkn_ref[:, LANE * h:LANE * h + LANE] = dn.astype(dkn_ref.dtype)
            dkr_acc = dkr_acc + dr
            dgkn_ref[...] += gn_p
            dgkr_ref[...] += gr_p
        dkr_ref[...] = dkr_acc

    row = lambda w: pl.BlockSpec((ts, w), lambda i: (i, 0))
    vec = pl.BlockSpec((1, LANE), lambda i: (0, 0))
    dspec = pl.BlockSpec((MLA_HEADS, ts, QK_PAD), lambda i: (0, i, 0))
    vsh = jax.ShapeDtypeStruct((1, LANE), F32)
    return pl.pallas_call(
        body, name=name, grid=(s // ts,),
        in_specs=_mla_prep_specs(ts) + [dspec, dspec],
        out_specs=[row(MLA_HEADS * MLA_QK), row(MLA_HEADS * MLA_NOPE), row(LANE), vec, vec, vec, vec],
        out_shape=[jax.ShapeDtypeStruct((s, MLA_HEADS * MLA_QK), BF16), jax.ShapeDtypeStruct((s, MLA_HEADS * MLA_NOPE), BF16),
                   jax.ShapeDtypeStruct((s, LANE), F32), vsh, vsh, vsh, vsh],
        compiler_params=_cp("arbitrary"),
    )(qraw, kvraw, proj, cos4, sin4, gqn, gqr, gkn, gkr, dq, dk)


ATT_T = 1024
ATT_T_FWD = 2048
ATT_SCALE = MLA_QK ** -0.5


def _attn_fwd(q, k, kvraw, *, name):
    nh, s, _ = q.shape
    t = _pick(s, ATT_T_FWD, LANE)
    nb = s // t

    def body(q_ref, k_ref, v_ref, o_ref, lse_ref, m_ref, l_ref, acc_ref):
        i, j = pl.program_id(1), pl.program_id(2)

        @pl.when(j == 0)
        def _():
            m_ref[...] = jnp.full_like(m_ref, -jnp.inf)
            l_ref[...] = jnp.zeros_like(l_ref)
            acc_ref[...] = jnp.zeros_like(acc_ref)

        def update(rows, keys, masked):
            sc = _dot_nt(q_ref[0, rows, :], k_ref[0, keys, :])
            if masked:
                ri = lax.broadcasted_iota(jnp.int32, sc.shape, 0) + rows.start
                cj = lax.broadcasted_iota(jnp.int32, sc.shape, 1) + keys.start
                sc = jnp.where(ri >= cj, sc, -jnp.inf)
            m_old = m_ref[rows, :]
            m_new = jnp.maximum(m_old, jnp.max(sc, axis=-1, keepdims=True))
            alpha = jnp.exp(m_old - m_new)
            p = jnp.exp(sc - m_new)
            l_ref[rows, :] = alpha * l_ref[rows, :] + jnp.sum(p, axis=-1, keepdims=True)
            acc_ref[rows, :] = alpha * acc_ref[rows, :] + _dot(p.astype(BF16), v_ref[keys, :].astype(BF16))
            m_ref[rows, :] = m_new

        @pl.when(j < i)
        def _():
            update(slice(0, t), slice(0, t), False)

        @pl.when(j == i)
        def _():
            if t % 256 == 0:
                update(slice(0, t // 2), slice(0, t // 2), True)
                update(slice(t // 2, t), slice(0, t), True)
            else:
                update(slice(0, t), slice(0, t), True)
            o_ref[...] = acc_ref[...] / l_ref[...]
            lse_ref[0] = m_ref[...] + jnp.log(l_ref[...])

    return pl.pallas_call(
        body, name=name, grid=(nh, nb, nb),
        in_specs=[pl.BlockSpec((1, t, QK_PAD), lambda h, i, j: (h, i, 0)),
                  pl.BlockSpec((1, t, QK_PAD), lambda h, i, j: (h, jnp.minimum(j, i), 0)),
                  pl.BlockSpec((t, MLA_V), lambda h, i, j: (jnp.minimum(j, i), nh + h))],
        out_specs=[pl.BlockSpec((t, MLA_V), lambda h, i, j: (i, h)),
                   pl.BlockSpec((1, t, 1), lambda h, i, j: (h, i, 0))],
        out_shape=[jax.ShapeDtypeStruct((s, nh * MLA_V), F32), jax.ShapeDtypeStruct((nh, s, 1), F32)],
        scratch_shapes=[pltpu.VMEM((t, 1), F32), pltpu.VMEM((t, 1), F32), pltpu.VMEM((t, MLA_V), F32)],
        compiler_params=_cp("parallel", "parallel", "arbitrary"),
    )(q, k, kvraw)


def _attn_bwd(q, k, kvraw, o, lse, do, *, name):
    nh, s, _ = q.shape
    t = _pick(s, ATT_T, LANE)
    nb = s // t

    def body(q_ref, k_ref, v_ref, o_ref, lse_ref, do_ref, dq_ref, dk_ref, dv_ref, dk_acc, dv_acc):
        j, i = pl.program_id(1), pl.program_id(2)

        @pl.when(i == 0)
        def _():
            dk_acc[...] = jnp.zeros_like(dk_acc)
            dv_acc[...] = jnp.zeros_like(dv_acc)

        def step(diagonal):
            qv, kv = q_ref[0], k_ref[0]
            sc = _dot_nt(qv, kv)
            if diagonal:
                ri = lax.broadcasted_iota(jnp.int32, (t, t), 0)
                cj = lax.broadcasted_iota(jnp.int32, (t, t), 1)
                sc = jnp.where(ri >= cj, sc, -jnp.inf)
            p = jnp.exp(sc - lse_ref[0])
            dov = do_ref[...]
            delta = jnp.sum(dov * o_ref[...], axis=-1, keepdims=True)
            do_b = dov.astype(BF16)
            dv_acc[...] += _dot_tn(p.astype(BF16), do_b)
            dp = _dot_nt(do_b, v_ref[...].astype(BF16))
            ds_b = (p * (dp - delta)).astype(BF16)
            dk_acc[...] += _dot_tn(ds_b, qv)
            dq_part = _dot(ds_b, kv) * ATT_SCALE
            rows = pl.ds(pl.multiple_of(i * t, t), t)

            @pl.when(j == 0)
            def _():
                dq_ref[0, rows, :] = dq_part

            @pl.when(j > 0)
            def _():
                dq_ref[0, rows, :] += dq_part

        @pl.when(i > j)
        def _():
            step(False)

        @pl.when(i == j)
        def _():
            step(True)

        @pl.when(i == nb - 1)
        def _():
            dk_ref[0] = dk_acc[...]
            dv_ref[...] = dv_acc[...].astype(dv_ref.dtype)

    qi = lambda h, j, i: jnp.maximum(i, j)
    return pl.pallas_call(
        body, name=name, grid=(nh, nb, nb),
        in_specs=[pl.BlockSpec((1, t, QK_PAD), lambda h, j, i: (h, qi(h, j, i), 0)),
                  pl.BlockSpec((1, t, QK_PAD), lambda h, j, i: (h, j, 0)),
                  pl.BlockSpec((t, MLA_V), lambda h, j, i: (j, nh + h)),
                  pl.BlockSpec((t, MLA_V), lambda h, j, i: (qi(h, j, i), h)),
                  pl.BlockSpec((1, t, 1), lambda h, j, i: (h, qi(h, j, i), 0)),
                  pl.BlockSpec((t, MLA_V), lambda h, j, i: (qi(h, j, i), h))],
        out_specs=[pl.BlockSpec((1, s, QK_PAD), lambda h, j, i: (h, 0, 0)),
                   pl.BlockSpec((1, t, QK_PAD), lambda h, j, i: (h, j, 0)),
                   pl.BlockSpec((t, MLA_V), lambda h, j, i: (j, h))],
        out_shape=[jax.ShapeDtypeStruct((nh, s, QK_PAD), F32), jax.ShapeDtypeStruct((nh, s, QK_PAD), F32),
                   jax.ShapeDtypeStruct((s, nh * MLA_V), BF16)],
        scratch_shapes=[pltpu.VMEM((t, QK_PAD), F32), pltpu.VMEM((t, MLA_V), F32)],
        compiler_params=_cp("parallel", "arbitrary", "arbitrary"),
    )(q, k, kvraw, o, lse, do)


def _ffn_fwd(h, w, tag):
    n = _rms_fwd(h, w["ln"], name=tag + "_norm")
    act, gate, up = _ffn_up(n, w["w13"], name=tag + "_up")
    out = _matmul(act, w["w2"], "nn", name=tag + "_down", scale=0.5, res=h)
    return out, (h, n, gate, up, act)


def _ffn_bwd(dout, saved, w, tag):
    h, n, gate, up, act = saved
    dact = _matmul(dout, w["w2"], "nt", name=tag + "_down_dx", scale=0.5, out_dtype=BF16)
    dw2 = _matmul(act, dout, "tn", name=tag + "_down_dw", scale=0.5)
    dgu = _swiglu_bwd(gate, up, dact, name=tag + "_act_bwd")
    dw13 = _matmul(n, dgu, "tn", name=tag + "_up_dw")
    dn = _matmul(dgu, w["w13"], "nt", name=tag + "_up_dx")
    dh, dln = _rms_bwd(h, w["ln"], dn, name=tag + "_norm_bwd", res=dout)
    return dh, dict(ln=dln, w13=dw13, w2=dw2)


def _mixer_fwd(h, w, rope, tag):
    cos4, sin4 = rope
    u = _rms_fwd(h, w["ln_mix"], name=tag + "_norm")
    proj = _matmul(u, w["w_in"], "nn", name=tag + "_in")
    xc = _conv_fwd(proj, w["conv_w"], w["conv_b"], name=tag + "_conv")
    dtr = proj[:, OFF_KRDT + MLA_ROPE:OFF_KRDT + MLA_ROPE + SSD_HEADS]
    dtrT = dtr.T
    y, states = _ssd_fwd(xc, dtr, dtrT, *w["ssd_aux"], name=tag + "_ssd")
    yn = _gated_rms_fwd(y, proj, w["ssd_norm"], name=tag + "_ssd_norm")
    y_ssd = _matmul(yn, w["w_ssd_out"], "nn", name=tag + "_ssd_out")
    cqn = _rms_fwd(proj, w["q_lora_norm"], name=tag + "_q_lora_norm", col=OFF_CQ // MLA_Q_LORA, width=MLA_Q_LORA)
    qraw = _matmul(cqn, w["w_uq"], "nn", name=tag + "_uq")
    ckvn = _rms_fwd(proj, w["kv_lora_norm"], name=tag + "_kv_lora_norm", col=OFF_CKV // MLA_KV_LORA, width=MLA_KV_LORA)
    kvraw = _matmul(ckvn, w["w_ukv"], "nn", name=tag + "_ukv")
    qf, kf = _mla_prep_fwd(qraw, kvraw, proj, cos4, sin4, *w["qk_gains"], name=tag + "_qk_prep")
    o, lse = _attn_fwd(qf, kf, kvraw, name=tag + "_attn")
    y_mla = _matmul(o, w["w_mla_out"], "nn", name=tag + "_mla_out")
    merged = _merge_fwd(proj, y_ssd, y_mla, name=tag + "_merge")
    out = _matmul(merged, w["w_o"], "nn", name=tag + "_o", res=h)
    saved = dict(h=h, u=u, proj=proj, xc=xc, dtr=dtr, dtrT=dtrT, states=states, y=y, yn=yn, y_ssd=y_ssd, cqn=cqn,
                 qraw=qraw, ckvn=ckvn, kvraw=kvraw, qf=qf, kf=kf, o=o, lse=lse, y_mla=y_mla, merged=merged)
    return out, saved


def _mixer_bwd(dout, s, w, rope, tag):
    cos4, sin4 = rope
    g = {}
    proj = s["proj"]
    dmerged = _matmul(dout, w["w_o"], "nt", name=tag + "_o_dx")
    g["w_o"] = _matmul(s["merged"], dout, "tn", name=tag + "_o_dw")
    dgates, dy_ssd, dy_mla = _merge_bwd(proj, s["y_ssd"], s["y_mla"], dmerged, name=tag + "_merge_bwd")
    do = _matmul(dy_mla, w["w_mla_out"], "nt", name=tag + "_mla_out_dx")
    g["w_mla_out"] = _matmul(s["o"], dy_mla, "tn", name=tag + "_mla_out_dw")
    dqf, dkf, dv = _attn_bwd(s["qf"], s["kf"], s["kvraw"], s["o"], s["lse"], do, name=tag + "_attn_bwd")
    dqraw, dkn, dkrt, dgqn, dgqr, dgkn, dgkr = _mla_prep_bwd(
        s["qraw"], s["kvraw"], proj, cos4, sin4, *w["qk_gains"], dqf, dkf, name=tag + "_qk_prep_bwd")
    g["q_norm"] = jnp.concatenate([dgqn[0], dgqr[0, :MLA_ROPE]])
    g["k_norm"] = jnp.concatenate([dgkn[0], dgkr[0, :MLA_ROPE]])
    dkvraw = jnp.concatenate([dkn, dv], axis=1)
    dcqn = _matmul(dqraw, w["w_uq"], "nt", name=tag + "_uq_dx")
    g["w_uq"] = _matmul(s["cqn"], dqraw, "tn", name=tag + "_uq_dw")
    dckvn = _matmul(dkvraw, w["w_ukv"], "nt", name=tag + "_ukv_dx")
    g["w_ukv"] = _matmul(s["ckvn"], dkvraw, "tn", name=tag + "_ukv_dw")
    dcq, g["q_lora_norm"] = _rms_bwd(proj, w["q_lora_norm"], dcqn, name=tag + "_q_lora_norm_bwd",
                                     col=OFF_CQ // MLA_Q_LORA, width=MLA_Q_LORA, out_dtype=BF16)
    dckv, g["kv_lora_norm"] = _rms_bwd(proj, w["kv_lora_norm"], dckvn, name=tag + "_kv_lora_norm_bwd",
                                       col=OFF_CKV // MLA_KV_LORA, width=MLA_KV_LORA, out_dtype=BF16)
    dyn = _matmul(dy_ssd, w["w_ssd_out"], "nt", name=tag + "_ssd_out_dx")
    g["w_ssd_out"] = _matmul(s["yn"], dy_ssd, "tn", name=tag + "_ssd_out_dw")
    dy, dz, g["ssd_norm"] = _gated_rms_bwd(s["y"], proj, w["ssd_norm"], dyn, name=tag + "_ssd_norm_bwd")
    aux = w["ssd_aux"]
    dxc, ddt_c, ddt_r, dbias_c, dbias_r, da_c, da_r, ddsk = _ssd_bwd(
        s["xc"], s["dtr"], s["dtrT"], *aux, aux[-1].T, s["states"], dy, name=tag + "_ssd_bwd")
    g["dt_bias"] = dbias_c[0] + dbias_r[:, 0]
    g["a_log"] = (da_c[0] + da_r[:, 0]) * (-jnp.exp(aux[2][0]))
    g["d_skip"] = jnp.sum(ddsk.reshape(SSD_HEADS, SSD_HEAD_DIM), axis=1)
    dpre, g["conv_w"], g["conv_b"] = _conv_bwd_pre(proj, w["conv_w"], w["conv_b"], dxc, name=tag + "_conv_bwd_pre")
    dxbc = _conv_bwd_x(dpre, w["conv_w"], name=tag + "_conv_bwd_x")
    ddtr = ddt_c + ddt_r.T
    dkrdt = jnp.concatenate([dkrt[:, :MLA_ROPE], ddtr, jnp.zeros((ddtr.shape[0], LANE - MLA_ROPE - SSD_HEADS), F32)], axis=1)
    dproj = _join_cols([dz, dxbc, dgates, dcq, dckv, dkrdt.astype(BF16)], name=tag + "_dproj")
    du = _matmul(dproj, w["w_in"], "nt", name=tag + "_in_dx")
    g["w_in"] = _matmul(s["u"], dproj, "tn", name=tag + "_in_dw")
    dh, g["ln_mix"] = _rms_bwd(s["h"], w["ln_mix"], du, name=tag + "_norm_bwd", res=dout)
    return dh, g


W_NAMES = ["ln_ffn1", "ffn1_w13", "ffn1_w2", "ln_mix", "w_in", "conv_w", "conv_b", "dt_bias", "a_log", "d_skip",
           "ssd_norm", "w_ssd_out", "q_lora_norm", "w_uq", "kv_lora_norm", "w_ukv", "q_norm", "k_norm", "w_mla_out",
           "w_o", "ln_ffn2", "ffn2_w13", "ffn2_w2"]
SHARD_AXIS = {"ffn1_w13": 2, "ffn1_w2": 1, "w_in": 2, "conv_w": 2, "w_ssd_out": 1, "w_uq": 2, "w_ukv": 2,
              "w_mla_out": 1, "w_o": 1, "ffn2_w13": 2, "ffn2_w2": 1}
SHARDED = [n for n in W_NAMES if n in SHARD_AXIS and n != "conv_w"] + ["conv_w"]
REPLICATED = [n for n in W_NAMES if n not in SHARD_AXIS]
N_CHIPS = 4
N_DEV = 8
PACK_COLS = 1024
IN_SPLIT = (2048, 3072, 32, 512, 256, 64, 2048)


def _pack_mats(arrs, rows, dtype):
    mats = [a.astype(dtype).reshape(-1, PACK_COLS) for a in arrs]
    used = sum(m.shape[0] for m in mats)
    return mats[:-1] + [jnp.concatenate([mats[-1], jnp.zeros((rows - used, PACK_COLS), dtype)], axis=0)]


STAGE_ROWS = 1024


def _stack_rows(mats, *, name):
    ncol, dtype = mats[0].shape[1], mats[0].dtype
    total = sum(m.shape[0] for m in mats)
    chunks, at = [], 0
    for i, m in enumerate(mats):
        for st in range(0, m.shape[0], STAGE_ROWS):
            sz = min(STAGE_ROWS, m.shape[0] - st)
            chunks.append((i, st, at + st, sz))
        at += m.shape[0]
    n = len(mats)

    def body(*refs):
        ins, out_ref, buf, sem_in, sem_out = refs[:n], refs[n], refs[n + 1], refs[n + 2], refs[n + 3]

        def put(idx):
            _, _, dst, sz = chunks[idx]
            return pltpu.make_async_copy(buf.at[idx % 2, pl.ds(0, sz), :], out_ref.at[pl.ds(dst, sz), :], sem_out.at[idx % 2])

        for idx, (i, st, _, sz) in enumerate(chunks):
            if idx >= 2:
                put(idx - 2).wait()
            get = pltpu.make_async_copy(ins[i].at[pl.ds(st, sz), :], buf.at[idx % 2, pl.ds(0, sz), :], sem_in.at[idx % 2])
            get.start()
            get.wait()
            put(idx).start()
        for idx in range(max(0, len(chunks) - 2), len(chunks)):
            put(idx).wait()

    return pl.pallas_call(
        body, name=name, out_shape=jax.ShapeDtypeStruct((total, ncol), dtype),
        in_specs=[ANY] * n, out_specs=ANY,
        scratch_shapes=[pltpu.VMEM((2, STAGE_ROWS, ncol), dtype), pltpu.SemaphoreType.DMA((2,)), pltpu.SemaphoreType.DMA((2,))],
    )(*mats)


def _join_cols(pieces, *, name):
    s, dtype = pieces[0].shape[0], pieces[0].dtype
    widths = [p.shape[1] for p in pieces]
    tr = _pick(s, 256, 16)

    def body(*refs):
        o_ref, at = refs[-1], 0
        for ref, w in zip(refs[:-1], widths):
            o_ref[:, at:at + w] = ref[...]
            at += w

    return pl.pallas_call(
        body, name=name, grid=(s // tr,),
        in_specs=[pl.BlockSpec((tr, w), lambda i: (i, 0)) for w in widths],
        out_specs=pl.BlockSpec((tr, sum(widths)), lambda i: (i, 0)),
        out_shape=jax.ShapeDtypeStruct((s, sum(widths)), dtype),
        compiler_params=_cp("parallel"),
    )(*pieces)


def _pack(arrs, rows, dtype, *, name):
    return _stack_rows(_pack_mats(arrs, rows, dtype), name=name)


def _unpack(packed, shapes):
    out, at = [], 0
    for sh in shapes:
        r = math.prod(sh) // PACK_COLS
        out.append(packed[at:at + r].reshape(sh))
        at += r
    return out


def _unpack_flat(flat, shapes):
    out, at = [], 0
    for sh in shapes:
        n = math.prod(sh)
        out.append(flat[at:at + n].reshape(sh))
        at += n
    return out


def _pack_rows(shapes):
    n = sum(math.prod(sh) for sh in shapes)
    return -(-n // (PACK_COLS * 1024)) * 1024


def _in_perm(w_in):
    z, xbc, dt, cq, ckv, kr, gates = jnp.split(w_in, list(np_cumsum(IN_SPLIT))[:-1], axis=1)
    return jnp.concatenate([z, xbc, gates, cq, ckv, kr, dt, jnp.zeros((w_in.shape[0], PROJ_W - sum(IN_SPLIT)), w_in.dtype)], axis=1)


def _in_unperm(g):
    z, xbc, gates, cq, ckv = (g[:, OFF_Z:OFF_XBC], g[:, OFF_XBC:OFF_GATES], g[:, OFF_GATES:OFF_CQ], g[:, OFF_CQ:OFF_CKV],
                              g[:, OFF_CKV:OFF_KRDT])
    kr = g[:, OFF_KRDT:OFF_KRDT + MLA_ROPE]
    dt = g[:, OFF_KRDT + MLA_ROPE:OFF_KRDT + MLA_ROPE + SSD_HEADS]
    return jnp.concatenate([z, xbc, dt, cq, ckv, kr, gates], axis=1)


def np_cumsum(sizes):
    out, t = [], 0
    for s in sizes:
        t += s
        out.append(t)
    return out


def _head_perm(w, first):
    r = w.shape[0]
    w3 = w.reshape(r, MLA_HEADS, -1)
    return jnp.concatenate([w3[:, :, :first].reshape(r, -1), w3[:, :, first:].reshape(r, -1)], axis=1)


def _head_unperm(g, first):
    r = g.shape[0]
    rest = g.shape[1] // MLA_HEADS - first
    a = g[:, :MLA_HEADS * first].reshape(r, MLA_HEADS, first)
    b = g[:, MLA_HEADS * first:].reshape(r, MLA_HEADS, rest)
    return jnp.concatenate([a, b], axis=2).reshape(r, -1)


def _layer_weights(full, l):
    row = lambda n: full[n][l][None].astype(F32)
    expand = jnp.repeat(jnp.eye(SSD_HEADS, dtype=F32), SSD_HEAD_DIM, axis=1)
    dtb, al, dsk = full["dt_bias"][l], full["a_log"][l], full["d_skip"][l]
    mixer = dict(
        ln_mix=row("ln_mix"), w_in=_in_perm(full["w_in"][l]), conv_w=full["conv_w"][l], conv_b=row("conv_b"),
        ssd_aux=(dtb[None], dtb[:, None], al[None], al[:, None], jnp.repeat(dsk, SSD_HEAD_DIM)[None], expand),
        ssd_norm=row("ssd_norm"), w_ssd_out=full["w_ssd_out"][l],
        q_lora_norm=row("q_lora_norm"), w_uq=_head_perm(full["w_uq"][l], MLA_NOPE),
        kv_lora_norm=row("kv_lora_norm"), w_ukv=_head_perm(full["w_ukv"][l], MLA_NOPE),
        qk_gains=_mla_gains(full["q_norm"][l], full["k_norm"][l]),
        w_mla_out=full["w_mla_out"][l], w_o=full["w_o"][l])
    ffn1 = dict(ln=row("ln_ffn1"), w13=full["ffn1_w13"][l], w2=full["ffn1_w2"][l])
    ffn2 = dict(ln=row("ln_ffn2"), w13=full["ffn2_w13"][l], w2=full["ffn2_w2"][l])
    return ffn1, mixer, ffn2


def _layer_grads(g1, gm, g2):
    return {
        "ln_ffn1": g1["ln"][0], "ffn1_w13": g1["w13"], "ffn1_w2": g1["w2"],
        "ln_mix": gm["ln_mix"][0], "w_in": _in_unperm(gm["w_in"]), "conv_w": gm["conv_w"], "conv_b": gm["conv_b"][0],
        "dt_bias": gm["dt_bias"], "a_log": gm["a_log"], "d_skip": gm["d_skip"], "ssd_norm": gm["ssd_norm"][0],
        "w_ssd_out": gm["w_ssd_out"], "q_lora_norm": gm["q_lora_norm"][0], "w_uq": _head_unperm(gm["w_uq"], MLA_NOPE),
        "kv_lora_norm": gm["kv_lora_norm"][0], "w_ukv": _head_unperm(gm["w_ukv"], MLA_NOPE),
        "q_norm": gm["q_norm"], "k_norm": gm["k_norm"], "w_mla_out": gm["w_mla_out"], "w_o": gm["w_o"],
        "ln_ffn2": g2["ln"][0], "ffn2_w13": g2["w13"], "ffn2_w2": g2["w2"],
    }


def _local_step(x, positions, loss_target, full):
    rope = _rope_tables4(positions)
    lw = [_layer_weights(full, l) for l in range(DEPTH)]
    h = x
    saved = []
    for l in range(DEPTH):
        f1, mx, f2 = lw[l]
        h, s1 = _ffn_fwd(h, f1, f"l{l}_ffn1")
        h, sm = _mixer_fwd(h, mx, rope, f"l{l}_mix")
        h, s2 = _ffn_fwd(h, f2, f"l{l}_ffn2")
        saved.append((s1, sm, s2))
    loss_part, dh = _loss_fwd_bwd(h, loss_target, name="loss")
    grads = [None] * DEPTH
    for l in reversed(range(DEPTH)):
        f1, mx, f2 = lw[l]
        s1, sm, s2 = saved[l]
        dh, g2 = _ffn_bwd(dh, s2, f2, f"l{l}_ffn2")
        dh, gm = _mixer_bwd(dh, sm, mx, rope, f"l{l}_mix")
        dh, g1 = _ffn_bwd(dh, s1, f1, f"l{l}_ffn1")
        grads[l] = _layer_grads(g1, gm, g2)
    full_grads = {n: jnp.stack([grads[l][n] for l in range(DEPTH)]) for n in W_NAMES}
    return loss_part, dh, full_grads


MESH = pl.DeviceIdType.MESH
ANY = pl.BlockSpec(memory_space=pl.ANY)


def _place():
    return lax.axis_index("x"), lax.axis_index("y"), lax.axis_index("c")


def _other_chips(x, y):
    return [(1 - x, y), (x, 1 - y), (1 - x, 1 - y)]


def _remote(src, dst, send_sems, recv_sems, k, to):
    return pltpu.make_async_remote_copy(src_ref=src, dst_ref=dst, send_sem=send_sems.at[k], recv_sem=recv_sems.at[k],
                                        device_id=to, device_id_type=MESH)


N_PARTS = 8


def _parts(rows):
    size = rows // N_PARTS
    assert size * N_PARTS == rows and size % 16 == 0, rows
    return [(p * size, size) for p in range(N_PARTS)]


def _rows(ref, lead, base, start, size):
    return ref.at[(*lead, pl.ds(pl.multiple_of(base + start, 16), size), slice(None))]


def _my_chip():
    return 2 * lax.axis_index("x") + lax.axis_index("y")


def _own_slot(packed, *, name):
    r, ncol = packed.shape
    tr = _pick(r, 512, 16)

    def body(x_ref, o_ref):
        o_ref[...] = x_ref[...]

    return pl.pallas_call(
        body, name=name, grid=(r // tr,),
        in_specs=[pl.BlockSpec((tr, ncol), lambda i: (i, 0))],
        out_specs=pl.BlockSpec((None, tr, ncol), lambda i: (_my_chip(), i, 0)),
        out_shape=jax.ShapeDtypeStruct((N_CHIPS, r, ncol), packed.dtype),
        compiler_params=_cp("arbitrary"),
    )(packed)


def _gather_shards(packed, slots, *, name):
    r, ncol = packed.shape
    hr = r // 2
    parts = _parts(hr)

    def body(x_ref, slots_ref, out_ref, send_sems, recv_sems):
        del slots_ref
        x, y, c = _place()
        chips = _other_chips(x, y)
        me = 2 * x + y

        def half(chip, cc):
            return _rows(out_ref, (2 * chip[0] + chip[1],), cc * hr, 0, hr)

        for j, chip in enumerate(chips):
            for st, sz in parts:
                _remote(_rows(x_ref, (), c * hr, st, sz), _rows(out_ref, (me,), c * hr, st, sz), send_sems, recv_sems, j,
                        (*chip, c)).start()
        for j, chip in enumerate(chips):
            _remote(half(chip, c), half(chip, c), send_sems, recv_sems, j, (x, y, c)).wait_recv()
            slot = 2 * chip[0] + chip[1]
            for st, sz in parts:
                _remote(_rows(out_ref, (slot,), c * hr, st, sz), _rows(out_ref, (slot,), c * hr, st, sz), send_sems,
                        recv_sems, 3 + j, (x, y, 1 - c)).start()
        for j, chip in enumerate(chips):
            _remote(half(chip, 1 - c), half(chip, 1 - c), send_sems, recv_sems, 3 + j, (x, y, c)).wait_recv()
        for k in range(6):
            _remote(half((x, y), c), half((x, y), c), send_sems, recv_sems, k, (x, y, c)).wait_send()

    return pl.pallas_call(
        body, name=name,
        out_shape=jax.ShapeDtypeStruct((N_CHIPS, r, ncol), packed.dtype),
        in_specs=[ANY, ANY], out_specs=ANY, input_output_aliases={1: 0},
        scratch_shapes=[pltpu.SemaphoreType.DMA((6,)), pltpu.SemaphoreType.DMA((6,))],
    )(packed, slots)


def _swap_halves(g, *, name):
    n, r, ncol = g.shape
    hr = r // 2
    parts = _parts(hr)

    def body(g_ref, got_ref, send_sems, recv_sems):
        x, y, c = _place()
        for s in range(n):
            for st, sz in parts:
                _remote(_rows(g_ref, (s,), (1 - c) * hr, st, sz), got_ref.at[s, pl.ds(st, sz), :], send_sems, recv_sems, 0,
                        (x, y, 1 - c)).start()
        _remote(got_ref, got_ref, send_sems, recv_sems, 0, (x, y, c)).wait()

    return pl.pallas_call(
        body, name=name, out_shape=jax.ShapeDtypeStruct((n, hr, ncol), g.dtype), in_specs=[ANY], out_specs=ANY,
        scratch_shapes=[pltpu.SemaphoreType.DMA((1,)), pltpu.SemaphoreType.DMA((1,))],
    )(g)


def _add_cores(g, got, *, name):
    n, r, ncol = g.shape
    hr = r // 2
    tr = _pick(hr, 512, 16)
    nb = hr // tr

    def body(a_ref, b_ref, o_ref):
        o_ref[...] = (a_ref[...].astype(F32) + b_ref[...].astype(F32)).astype(o_ref.dtype)

    blk = pl.BlockSpec((None, tr, ncol), lambda s, i: (s, i, 0))
    return pl.pallas_call(
        body, name=name, grid=(n, nb),
        in_specs=[pl.BlockSpec((None, tr, ncol), lambda s, i: (s, lax.axis_index("c") * nb + i, 0)), blk],
        out_specs=blk,
        out_shape=jax.ShapeDtypeStruct((n, hr, ncol), BF16),
        compiler_params=_cp("parallel", "parallel"),
    )(g, got)


def _scatter_to_chips(a, *, name):
    n, r, ncol = a.shape
    parts = _parts(r)

    def body(a_ref, got_ref, send_sems, recv_sems):
        x, y, c = _place()
        for st, sz in parts:
            for j, chip in enumerate(_other_chips(x, y)):
                _remote(a_ref.at[2 * chip[0] + chip[1], pl.ds(st, sz), :], got_ref.at[j, pl.ds(st, sz), :], send_sems,
                        recv_sems, j, (*chip, c)).start()
        for j in range(n - 1):
            _remote(got_ref.at[j], got_ref.at[j], send_sems, recv_sems, j, (x, y, c)).wait()

    return pl.pallas_call(
        body, name=name, out_shape=jax.ShapeDtypeStruct((n - 1, r, ncol), a.dtype), in_specs=[ANY], out_specs=ANY,
        scratch_shapes=[pltpu.SemaphoreType.DMA((3,)), pltpu.SemaphoreType.DMA((3,))],
    )(a)


def _add_chips(a, got, *, name):
    n, hr, ncol = a.shape
    tr = _pick(hr, 512, 16)
    nb = hr // tr

    def body(a_ref, g0_ref, g1_ref, g2_ref, o_ref):
        f = lambda ref: ref[...].astype(F32)
        o_ref[...] = ((f(a_ref) + f(g0_ref)) + f(g1_ref)) + f(g2_ref)

    other = lambda j: pl.BlockSpec((None, tr, ncol), lambda i: (j, i, 0))
    return pl.pallas_call(
        body, name=name, grid=(nb,),
        in_specs=[pl.BlockSpec((None, tr, ncol), lambda i: (_my_chip(), i, 0)), other(0), other(1), other(2)],
        out_specs=pl.BlockSpec((tr, ncol), lambda i: (lax.axis_index("c") * nb + i, 0)),
        out_shape=jax.ShapeDtypeStruct((2 * hr, ncol), F32),
        compiler_params=_cp("parallel"),
    )(a, got, got, got)


def _join_halves(buf, *, name):
    r, ncol = buf.shape
    hr = r // 2
    parts = _parts(hr)

    def body(b_ref, out_ref, send_sems, recv_sems):
        del b_ref
        x, y, c = _place()
        for st, sz in parts:
            _remote(_rows(out_ref, (), c * hr, st, sz), _rows(out_ref, (), c * hr, st, sz), send_sems, recv_sems, 0,
                    (x, y, 1 - c)).start()
        theirs = _rows(out_ref, (), (1 - c) * hr, 0, hr)
        _remote(theirs, theirs, send_sems, recv_sems, 0, (x, y, c)).wait()

    return pl.pallas_call(
        body, name=name, out_shape=jax.ShapeDtypeStruct((r, ncol), buf.dtype), in_specs=[ANY], out_specs=ANY,
        input_output_aliases={0: 0},
        scratch_shapes=[pltpu.SemaphoreType.DMA((1,)), pltpu.SemaphoreType.DMA((1,))],
    )(buf)


def _reduce_scatter(g, *, name):
    got = _swap_halves(g, name=name + "_swap")
    chip_sum = _add_cores(g, got, name=name + "_add_cores")
    others = _scatter_to_chips(chip_sum, name=name + "_scatter")
    return _join_halves(_add_chips(chip_sum, others, name=name + "_add_chips"), name=name + "_join")


def _all_gather_small(v, *, name):
    r, ncol = v.shape

    def body(x_ref, out_ref, send_sems, recv_sems, local_sem):
        x, y, c = _place()
        me, sibling = (x, y, c), (x, y, 1 - c)
        chips = _other_chips(x, y)

        def slot(p):
            return out_ref.at[4 * p[0] + 2 * p[1] + p[2]]

        mine = pltpu.make_async_copy(x_ref, slot(me), local_sem.at[0])
        mine.start()
        first = [_remote(x_ref, slot(me), send_sems, recv_sems, 0, sibling)]
        first += [_remote(x_ref, slot(me), send_sems, recv_sems, 1 + j, (*chip, c)) for j, chip in enumerate(chips)]
        for cp in first:
            cp.start()
        passed = [_remote(slot((*chip, c)), slot((*chip, c)), send_sems, recv_sems, 4 + j, sibling)
                  for j, chip in enumerate(chips)]
        for j, chip in enumerate(chips):
            _remote(slot((*chip, c)), slot((*chip, c)), send_sems, recv_sems, 1 + j, me).wait_recv()
            passed[j].start()
        _remote(slot(sibling), slot(sibling), send_sems, recv_sems, 0, me).wait_recv()
        for j, chip in enumerate(chips):
            _remote(slot((*chip, 1 - c)), slot((*chip, 1 - c)), send_sems, recv_sems, 4 + j, me).wait_recv()
        for cp in first + passed:
            cp.wait_send()
        mine.wait()

    vm = pl.BlockSpec(memory_space=pltpu.VMEM)
    return pl.pallas_call(
        body, name=name, out_shape=jax.ShapeDtypeStruct((N_DEV, r, ncol), v.dtype), in_specs=[vm], out_specs=vm,
        scratch_shapes=[pltpu.SemaphoreType.DMA((7,)), pltpu.SemaphoreType.DMA((7,)), pltpu.SemaphoreType.DMA((1,))],
    )(v)


def _sum_slots(g8, *, name):
    n, r, ncol = g8.shape

    def body(g_ref, o_ref):
        acc = g_ref[0]
        for k in range(1, n):
            acc = acc + g_ref[k]
        o_ref[...] = acc

    return pl.pallas_call(body, name=name, out_shape=jax.ShapeDtypeStruct((r, ncol), g8.dtype))(g8)


def _step(a):
    x = a["x"][0]
    s = x.shape[0]
    del s
    shard_shapes = [a[n].shape for n in SHARDED]
    rows = _pack_rows(shard_shapes)

    packed = _pack([a[n] for n in SHARDED], rows, BF16, name="pack_weights")
    gathered = _gather_shards(packed, _own_slot(packed, name="own_weights"), name="gather_weights")
    conv_rows = -(-math.prod(a["conv_w"].shape) // (LANE * 8)) * 8
    conv_all = _all_gather_small(
        jnp.pad(a["conv_w"].reshape(-1), (0, conv_rows * LANE - math.prod(a["conv_w"].shape))).reshape(conv_rows, LANE),
        name="gather_conv_w")
    per_chip = [dict(zip(SHARDED, _unpack(gathered[k], shard_shapes))) for k in range(N_CHIPS)]
    full = {n: jnp.concatenate([per_chip[k][n] for k in range(N_CHIPS)], axis=SHARD_AXIS[n]) for n in SHARDED}
    full["conv_w"] = jnp.concatenate(
        [conv_all[2 * k].reshape(-1)[:math.prod(a["conv_w"].shape)].reshape(a["conv_w"].shape) for k in range(N_CHIPS)],
        axis=SHARD_AXIS["conv_w"])
    for n in REPLICATED:
        full[n] = a[n]

    loss_part, grad_x, grads = _local_step(x, a["positions"][0], a["loss_target"][0], full)
    loss = lax.psum(jnp.sum(loss_part), ("x", "y", "c"))

    mats = []
    for k in range(N_CHIPS):
        parts = [jnp.split(grads[n], N_CHIPS, axis=SHARD_AXIS[n])[k] for n in SHARDED]
        mats += _pack_mats(parts, rows, BF16)
    g_slots = _stack_rows(mats, name="pack_grads").reshape(N_CHIPS, rows, PACK_COLS)
    g_shard = _reduce_scatter(g_slots, name="reduce_grads")

    rep_shapes = [a[n].shape for n in REPLICATED]
    n_rep = sum(math.prod(sh) for sh in rep_shapes)
    rep_rows = -(-n_rep // (LANE * 8)) * 8
    pack_small = lambda arrs: jnp.pad(jnp.concatenate([t.reshape(-1) for t in arrs]), (0, rep_rows * LANE - n_rep)).reshape(rep_rows, LANE)
    g_rep = _sum_slots(_all_gather_small(pack_small([grads[n] for n in REPLICATED]), name="gather_small_grads"),
                       name="add_small_grads")

    out = {"loss": loss, "grad_x": grad_x[None]}
    for n, g in zip(SHARDED, _unpack(g_shard, shard_shapes)):
        flat = lambda t: t.reshape(-1, t.shape[-1])
        d, nm, nv = _adamw(flat(a[n]), flat(g), flat(a["m_" + n]), flat(a["v_" + n]), name="adamw_" + n)
        out["grad_" + n] = g
        out["delta_" + n], out["new_m_" + n], out["new_v_" + n] = (t.reshape(g.shape) for t in (d, nm, nv))
    d_rp, m_rp, v_rp = _adamw(pack_small([a[n] for n in REPLICATED]), g_rep,
                              pack_small([a["m_" + n] for n in REPLICATED]),
                              pack_small([a["v_" + n] for n in REPLICATED]), name="adamw_replicated")
    for prefix, rp_arr in (("grad_", g_rep), ("delta_", d_rp), ("new_m_", m_rp), ("new_v_", v_rp)):
        for n, t in zip(REPLICATED, _unpack_flat(rp_arr.reshape(-1)[:n_rep], rep_shapes)):
            out[prefix + n] = t
    return out


IN_NAMES = ["x", "positions"] + W_NAMES + ["loss_target"] + ["m_" + n for n in W_NAMES] + ["v_" + n for n in W_NAMES]
OUT_NAMES = (["loss", "grad_x"] + ["grad_" + n for n in W_NAMES] + ["delta_" + n for n in W_NAMES]
             + ["new_m_" + n for n in W_NAMES] + ["new_v_" + n for n in W_NAMES])


def kernel(x, positions, ln_ffn1, ffn1_w13, ffn1_w2, ln_mix, w_in, conv_w, conv_b, dt_bias, a_log, d_skip, ssd_norm, w_ssd_out, q_lora_norm, w_uq, kv_lora_norm, w_ukv, q_norm, k_norm, w_mla_out, w_o, ln_ffn2, ffn2_w13, ffn2_w2, loss_target, m_ln_ffn1, m_ffn1_w13, m_ffn1_w2, m_ln_mix, m_w_in, m_conv_w, m_conv_b, m_dt_bias, m_a_log, m_d_skip, m_ssd_norm, m_w_ssd_out, m_q_lora_norm, m_w_uq, m_kv_lora_norm, m_w_ukv, m_q_norm, m_k_norm, m_w_mla_out, m_w_o, m_ln_ffn2, m_ffn2_w13, m_ffn2_w2, v_ln_ffn1, v_ffn1_w13, v_ffn1_w2, v_ln_mix, v_w_in, v_conv_w, v_conv_b, v_dt_bias, v_a_log, v_d_skip, v_ssd_norm, v_w_ssd_out, v_q_lora_norm, v_w_uq, v_kv_lora_norm, v_w_ukv, v_q_norm, v_k_norm, v_w_mla_out, v_w_o, v_ln_ffn2, v_ffn2_w13, v_ffn2_w2):
    given = locals()
    out = _step({n: given[n] for n in IN_NAMES})
    return tuple(out[n] for n in OUT_NAMES)
```

```python
import functools
import math

import jax
import jax.numpy as jnp
from jax import lax
from jax.experimental import pallas as pl
from jax.experimental.pallas import tpu as pltpu

F32 = jnp.float32
BF16 = jnp.bfloat16

D_MODEL = 1024
DEPTH = 2
D_FF = 2816
SSD_D_INNER = 2048
SSD_HEADS = 32
SSD_HEAD_DIM = 64
SSD_GROUPS = 4
SSD_STATE = 128
SSD_CHUNK = 128
SSD_CONV = 4
SSD_CONV_DIM = 3072
MLA_HEADS = 8
MLA_Q_LORA = 512
MLA_KV_LORA = 256
MLA_NOPE = 128
MLA_ROPE = 64
MLA_V = 128
MLA_QK = 192
ROPE_THETA = 10000.0
EPS = 1e-6
ADAM_LR = 0.001
ADAM_B1 = 0.9
ADAM_B2 = 0.999
ADAM_EPS = 1e-08
ADAM_WD = 0.01
ADAM_STEP = 10

PROJ_W = 8064
OFF_Z, OFF_XBC, OFF_GATES, OFF_CQ, OFF_CKV, OFF_KRDT = 0, 2048, 5120, 7168, 7680, 7936

LANE = 128
VMEM_LIMIT = 48 * 1024 * 1024
HI = lax.Precision.HIGHEST


def _cp(*sem):
    return pltpu.CompilerParams(dimension_semantics=sem, vmem_limit_bytes=VMEM_LIMIT)


def _pick(dim, target, align):
    if dim <= target:
        return dim
    b = (target // align) * align
    while b >= align:
        if dim % b == 0:
            return b
        b -= align
    raise ValueError(f"no block for {dim} (target {target}, align {align})")


def _silu(x):
    return x * jax.nn.sigmoid(x)


def _dsilu(x):
    s = jax.nn.sigmoid(x)
    return s * (1.0 + x * (1.0 - s))


MM_VMEM_BUDGET = 40 * 1024 * 1024


def _mm_tiles(m, n, k, a_bytes, b_bytes, o_bytes):
    bn = _pick(n, 1408, LANE)
    for nk in (1, 2, 3, 4, 6, 7, 8):
        if k % nk or (k // nk) % LANE:
            continue
        bk = k // nk
        for bm in (1024, 512):
            if m % bm:
                continue
            need = 2 * (bm * bk * a_bytes + bk * bn * b_bytes + bm * bn * o_bytes) + (bm * bn * 4 if nk > 1 else 0)
            if need <= MM_VMEM_BUDGET:
                return bm, bn, bk
    return _pick(m, 512, 8), bn, _pick(k, 1536, LANE)

def _matmul(a, b, mode, *, name, out_dtype=F32, scale=1.0, res=None):
    if mode == "nn":
        (m, k), (k2, n) = a.shape, b.shape
    elif mode == "nt":
        (m, k), (n, k2) = a.shape, b.shape
    else:
        (k, m), (k2, n) = a.shape, b.shape
    assert k == k2, (a.shape, b.shape, mode)
    if mode == "tn":
        bn, bk = _pick(n, 2816, LANE), _pick(k, 1024, 8)
        bm = _pick(m, max(256, (1408 * 1024 // bn) // LANE * LANE), LANE)
    else:
        bm, bn, bk = _mm_tiles(m, n, k, a.dtype.itemsize, b.dtype.itemsize,
                               jnp.dtype(out_dtype).itemsize + (4 if res is not None else 0))
    nk = k // bk

    def body(a_ref, b_ref, *rest):
        res_ref = rest[0] if res is not None else None
        o_ref = rest[-2] if nk > 1 else rest[-1]
        kk = pl.program_id(2)
        av = a_ref[...].astype(BF16)
        bv = b_ref[...].astype(BF16)
        if mode == "nn":
            dims = (((1,), (0,)), ((), ()))
        elif mode == "nt":
            dims = (((1,), (1,)), ((), ()))
        else:
            dims = (((0,), (0,)), ((), ()))
        part = lax.dot_general(av, bv, dims, preferred_element_type=F32)

        def finish(total):
            out = total * scale
            if res_ref is not None:
                out = res_ref[...] + out
            o_ref[...] = out.astype(o_ref.dtype)

        if nk == 1:
            finish(part)
            return
        acc_ref = rest[-1]

        @pl.when(kk == 0)
        def _():
            acc_ref[...] = part

        @pl.when((kk > 0) & (kk < nk - 1))
        def _():
            acc_ref[...] += part

        @pl.when(kk == nk - 1)
        def _():
            finish(acc_ref[...] + part)

    o_spec = pl.BlockSpec((bm, bn), lambda i, j, kk: (i, j))
    if mode == "nn":
        a_spec = pl.BlockSpec((bm, bk), lambda i, j, kk: (i, kk))
        b_spec = pl.BlockSpec((bk, bn), lambda i, j, kk: (kk, j))
    elif mode == "nt":
        a_spec = pl.BlockSpec((bm, bk), lambda i, j, kk: (i, kk))
        b_spec = pl.BlockSpec((bn, bk), lambda i, j, kk: (j, kk))
    else:
        a_spec = pl.BlockSpec((bk, bm), lambda i, j, kk: (kk, i))
        b_spec = pl.BlockSpec((bk, bn), lambda i, j, kk: (kk, j))
    return pl.pallas_call(
        body, name=name,
        grid=(m // bm, n // bn, nk),
        in_specs=[a_spec, b_spec] + ([o_spec] if res is not None else []),
        out_specs=o_spec,
        out_shape=jax.ShapeDtypeStruct((m, n), out_dtype),
        scratch_shapes=[pltpu.VMEM((bm, bn), F32)] if nk > 1 else [],
        compiler_params=_cp("parallel", "parallel", "arbitrary"),
    )(*((a, b) + ((res,) if res is not None else ())))


def _rms_fwd(x, g, *, name, col=0, width=None):
    r = x.shape[0]
    w = width or x.shape[1]
    tr = _pick(r, 512, 16)

    def body(x_ref, g_ref, o_ref):
        xv = x_ref[...]
        rs = lax.rsqrt(jnp.mean(xv * xv, axis=-1, keepdims=True) + EPS)
        o_ref[...] = (xv * rs * g_ref[...]).astype(o_ref.dtype)

    return pl.pallas_call(
        body, name=name, grid=(r // tr,),
        in_specs=[pl.BlockSpec((tr, w), lambda i: (i, col)), pl.BlockSpec((1, w), lambda i: (0, 0))],
        out_specs=pl.BlockSpec((tr, w), lambda i: (i, 0)),
        out_shape=jax.ShapeDtypeStruct((r, w), BF16),
        compiler_params=_cp("parallel"),
    )(x, g)


def _rms_bwd(x, g, dy, *, name, col=0, width=None, res=None, out_dtype=F32):
    r = x.shape[0]
    w = width or x.shape[1]
    tr = _pick(r, 512, 8)

    def body(x_ref, g_ref, dy_ref, *rest):
        res_ref = rest[0] if res is not None else None
        dx_ref, dg_ref = rest[-2:]
        i = pl.program_id(0)
        xv = x_ref[...]
        dyv = dy_ref[...]
        rs = lax.rsqrt(jnp.mean(xv * xv, axis=-1, keepdims=True) + EPS)
        xh = xv * rs
        dxh = dyv * g_ref[...]
        mm = jnp.mean(dxh * xh, axis=-1, keepdims=True)
        dx = rs * (dxh - xh * mm)
        if res_ref is not None:
            dx = res_ref[...] + dx
        dx_ref[...] = dx.astype(dx_ref.dtype)
        part = jnp.sum(dyv * xh, axis=0, keepdims=True)

        @pl.when(i == 0)
        def _():
            dg_ref[...] = part

        @pl.when(i > 0)
        def _():
            dg_ref[...] += part

    blk = pl.BlockSpec((tr, w), lambda i: (i, 0))
    return pl.pallas_call(
        body, name=name, grid=(r // tr,),
        in_specs=[pl.BlockSpec((tr, w), lambda i: (i, col)), pl.BlockSpec((1, w), lambda i: (0, 0)), blk]
        + ([blk] if res is not None else []),
        out_specs=[blk, pl.BlockSpec((1, w), lambda i: (0, 0))],
        out_shape=[jax.ShapeDtypeStruct((r, w), out_dtype), jax.ShapeDtypeStruct((1, w), F32)],
        compiler_params=_cp("arbitrary"),
    )(*((x, g, dy) + ((res,) if res is not None else ())))


def _gated_rms_fwd(y, proj, g, *, name):
    r, w = y.shape
    tr = _pick(r, 256, 8)

    def body(y_ref, z_ref, g_ref, o_ref):
        t = y_ref[...] * _silu(z_ref[...])
        rs = lax.rsqrt(jnp.mean(t * t, axis=-1, keepdims=True) + EPS)
        o_ref[...] = (t * rs * g_ref[...]).astype(o_ref.dtype)

    return pl.pallas_call(
        body, name=name, grid=(r // tr,),
        in_specs=[pl.BlockSpec((tr, w), lambda i: (i, 0)), pl.BlockSpec((tr, w), lambda i: (i, OFF_Z // w)),
                  pl.BlockSpec((1, w), lambda i: (0, 0))],
        out_specs=pl.BlockSpec((tr, w), lambda i: (i, 0)),
        out_shape=jax.ShapeDtypeStruct((r, w), BF16),
        compiler_params=_cp("parallel"),
    )(y, proj, g)


def _gated_rms_bwd(y, proj, g, do, *, name):
    r, w = y.shape
    tr = _pick(r, 256, 8)

    def body(y_ref, z_ref, g_ref, do_ref, dy_ref, dz_ref, dg_ref):
        i = pl.program_id(0)
        yv, zv, dov = y_ref[...], z_ref[...], do_ref[...]
        sg = jax.nn.sigmoid(zv)
        sz = zv * sg
        t = yv * sz
        rs = lax.rsqrt(jnp.mean(t * t, axis=-1, keepdims=True) + EPS)
        th = t * rs
        dth = dov * g_ref[...]
        mm = jnp.mean(dth * th, axis=-1, keepdims=True)
        dt = rs * (dth - th * mm)
        dy_ref[...] = dt * sz
        dz_ref[...] = (dt * yv * (sg * (1.0 + zv * (1.0 - sg)))).astype(dz_ref.dtype)
        part = jnp.sum(dov * th, axis=0, keepdims=True)

        @pl.when(i == 0)
        def _():
            dg_ref[...] = part

        @pl.when(i > 0)
        def _():
            dg_ref[...] += part

    blk = pl.BlockSpec((tr, w), lambda i: (i, 0))
    vec = pl.BlockSpec((1, w), lambda i: (0, 0))
    return pl.pallas_call(
        body, name=name, grid=(r // tr,),
        in_specs=[blk, pl.BlockSpec((tr, w), lambda i: (i, OFF_Z // w)), vec, blk],
        out_specs=[blk, blk, vec],
        out_shape=[jax.ShapeDtypeStruct((r, w), F32), jax.ShapeDtypeStruct((r, w), BF16),
                   jax.ShapeDtypeStruct((1, w), F32)],
        compiler_params=_cp("arbitrary"),
    )(y, proj, g, do)


def _ffn_up(n, w13, *, name):
    m, k = n.shape
    f = w13.shape[1] // 2
    bm, bn = _pick(m, 1024, 16), _pick(f, 1408, LANE)
    nj = f // bn

    def body(a_ref, wg_ref, wu_ref, act_ref, g_ref, u_ref):
        a = a_ref[...].astype(BF16)
        g = _dot(a, wg_ref[...].astype(BF16))
        u = _dot(a, wu_ref[...].astype(BF16))
        act_ref[...] = (_silu(g) * u).astype(act_ref.dtype)
        g_ref[...] = g.astype(g_ref.dtype)
        u_ref[...] = u.astype(u_ref.dtype)

    out = pl.BlockSpec((bm, bn), lambda j, i: (i, j))
    sh = jax.ShapeDtypeStruct((m, f), BF16)
    return pl.pallas_call(
        body, name=name, grid=(nj, m // bm),
        in_specs=[pl.BlockSpec((bm, k), lambda j, i: (i, 0)), pl.BlockSpec((k, bn), lambda j, i: (0, j)),
                  pl.BlockSpec((k, bn), lambda j, i: (0, nj + j))],
        out_specs=[out, out, out], out_shape=[sh, sh, sh],
        compiler_params=_cp("parallel", "parallel"),
    )(n, w13, w13)


def _swiglu_bwd(g, u, da, *, name):
    r, f = g.shape
    tr = _pick(r, 256, 16)

    def body(g_ref, u_ref, da_ref, o_ref):
        gv, uv, dav = g_ref[...].astype(F32), u_ref[...].astype(F32), da_ref[...].astype(F32)
        sg = jax.nn.sigmoid(gv)
        o_ref[:, :f] = (dav * uv * (sg * (1.0 + gv * (1.0 - sg)))).astype(o_ref.dtype)
        o_ref[:, f:] = (dav * (gv * sg)).astype(o_ref.dtype)

    blk = pl.BlockSpec((tr, f), lambda i: (i, 0))
    return pl.pallas_call(
        body, name=name, grid=(r // tr,),
        in_specs=[blk, blk, blk],
        out_specs=pl.BlockSpec((tr, 2 * f), lambda i: (i, 0)),
        out_shape=jax.ShapeDtypeStruct((r, 2 * f), BF16),
        compiler_params=_cp("parallel"),
    )(g, u, da)


CONV_TS = 1024
CONV_TC = 512


def _conv_pre(x, carry, w_ref, b_ref):
    ts = x.shape[0]
    row8 = lax.broadcasted_iota(jnp.int32, (8, x.shape[1]), 0)
    head_x = x[0:8]
    shifted, shifted_head = [], []
    for j in range(SSD_CONV):
        if j == 0:
            shifted.append(x)
            shifted_head.append(head_x)
        else:
            shifted.append(pltpu.roll(x, j, 0))
            shifted_head.append(jnp.where(row8 < j, pltpu.roll(carry, j, 0), pltpu.roll(head_x, j, 0)))
    pre = b_ref[...] + sum(w_ref[SSD_CONV - 1 - j:SSD_CONV - j, :] * shifted[j] for j in range(SSD_CONV))
    pre_head = b_ref[...] + sum(w_ref[SSD_CONV - 1 - j:SSD_CONV - j, :] * shifted_head[j] for j in range(SSD_CONV))
    del ts
    return pre, pre_head, shifted, shifted_head


def _conv_fwd(proj, w, b, *, name):
    s = proj.shape[0]
    c = w.shape[1]
    ts, tc = _pick(s, CONV_TS, 8), CONV_TC
    off = OFF_XBC // tc

    def body(x_ref, w_ref, b_ref, o_ref, carry_ref):
        t = pl.program_id(1)

        @pl.when(t == 0)
        def _():
            carry_ref[...] = jnp.zeros_like(carry_ref)

        x = x_ref[...]
        pre, pre_head, _, _ = _conv_pre(x, carry_ref[...], w_ref, b_ref)
        o_ref[...] = _silu(pre)
        o_ref[0:8, :] = _silu(pre_head)
        carry_ref[...] = x[ts - 8:ts]

    return pl.pallas_call(
        body, name=name, grid=(c // tc, s // ts),
        in_specs=[pl.BlockSpec((ts, tc), lambda j, t: (t, j + off)), pl.BlockSpec((SSD_CONV, tc), lambda j, t: (0, j)),
                  pl.BlockSpec((1, tc), lambda j, t: (0, j))],
        out_specs=pl.BlockSpec((ts, tc), lambda j, t: (t, j)),
        out_shape=jax.ShapeDtypeStruct((s, c), F32),
        scratch_shapes=[pltpu.VMEM((8, tc), F32)],
        compiler_params=_cp("parallel", "arbitrary"),
    )(proj, w, b)


def _conv_bwd_pre(proj, w, b, dy, *, name):
    s = proj.shape[0]
    c = w.shape[1]
    ts, tc = _pick(s, CONV_TS, 8), CONV_TC
    off = OFF_XBC // tc

    def body(x_ref, w_ref, b_ref, dy_ref, dp_ref, dw_ref, db_ref, carry_ref):
        t = pl.program_id(1)

        @pl.when(t == 0)
        def _():
            carry_ref[...] = jnp.zeros_like(carry_ref)
            dw_ref[...] = jnp.zeros_like(dw_ref)
            db_ref[...] = jnp.zeros_like(db_ref)

        x = x_ref[...]
        pre, pre_head, shifted, shifted_head = _conv_pre(x, carry_ref[...], w_ref, b_ref)
        dyv = dy_ref[...]
        dp = dyv * _dsilu(pre)
        dp_head = dyv[0:8] * _dsilu(pre_head)
        row = lax.broadcasted_iota(jnp.int32, dp.shape, 0)
        dp_tail = jnp.where(row >= 8, dp, 0.0)
        dp_ref[...] = dp
        dp_ref[0:8, :] = dp_head
        db_ref[...] += jnp.sum(dp_tail, axis=0, keepdims=True) + jnp.sum(dp_head, axis=0, keepdims=True)
        for j in range(SSD_CONV):
            kk = SSD_CONV - 1 - j
            dw_ref[kk:kk + 1, :] += (jnp.sum(dp_tail * shifted[j], axis=0, keepdims=True)
                                     + jnp.sum(dp_head * shifted_head[j], axis=0, keepdims=True))
        carry_ref[...] = x[ts - 8:ts]

    return pl.pallas_call(
        body, name=name, grid=(c // tc, s // ts),
        in_specs=[pl.BlockSpec((ts, tc), lambda j, t: (t, j + off)), pl.BlockSpec((SSD_CONV, tc), lambda j, t: (0, j)),
                  pl.BlockSpec((1, tc), lambda j, t: (0, j)), pl.BlockSpec((ts, tc), lambda j, t: (t, j))],
        out_specs=[pl.BlockSpec((ts, tc), lambda j, t: (t, j)), pl.BlockSpec((SSD_CONV, tc), lambda j, t: (0, j)),
                   pl.BlockSpec((1, tc), lambda j, t: (0, j))],
        out_shape=[jax.ShapeDtypeStruct((s, c), F32), jax.ShapeDtypeStruct((SSD_CONV, c), F32),
                   jax.ShapeDtypeStruct((1, c), F32)],
        scratch_shapes=[pltpu.VMEM((8, tc), F32)],
        compiler_params=_cp("parallel", "arbitrary"),
    )(proj, w, b, dy)


def _conv_bwd_x(dp, w, *, name):
    s, c = dp.shape
    ts, tc = _pick(s, CONV_TS, 8), CONV_TC
    nt = s // ts

    def body(d_ref, w_ref, o_ref, carry_ref, full_ref):
        t = pl.program_id(1)

        @pl.when(t == 0)
        def _():
            carry_ref[...] = jnp.zeros_like(carry_ref)

        d = d_ref[...]
        carry = carry_ref[...]
        row8 = lax.broadcasted_iota(jnp.int32, (8, tc), 0)
        tail = d[ts - 8:ts]
        acc = w_ref[SSD_CONV - 1:SSD_CONV, :] * d
        acc_tail = w_ref[SSD_CONV - 1:SSD_CONV, :] * tail
        for j in range(1, SSD_CONV):
            wj = w_ref[SSD_CONV - 1 - j:SSD_CONV - j, :]
            acc = acc + wj * pltpu.roll(d, ts - j, 0)
            up_tail = jnp.where(row8 >= 8 - j, pltpu.roll(carry, 8 - j, 0), pltpu.roll(tail, 8 - j, 0))
            acc_tail = acc_tail + wj * up_tail
        full_ref[...] = acc
        full_ref[ts - 8:ts, :] = acc_tail
        o_ref[...] = full_ref[...].astype(o_ref.dtype)
        carry_ref[...] = d[0:8]

    return pl.pallas_call(
        body, name=name, grid=(c // tc, nt),
        in_specs=[pl.BlockSpec((ts, tc), lambda j, t: (nt - 1 - t, j)), pl.BlockSpec((SSD_CONV, tc), lambda j, t: (0, j))],
        out_specs=pl.BlockSpec((ts, tc), lambda j, t: (nt - 1 - t, j)),
        out_shape=jax.ShapeDtypeStruct((s, c), BF16),
        scratch_shapes=[pltpu.VMEM((8, tc), F32), pltpu.VMEM((ts, tc), F32)],
        compiler_params=_cp("parallel", "arbitrary"),
    )(dp, w)


def _merge_fwd(proj, ys, ym, *, name):
    r, w = ys.shape
    tr = _pick(r, 512, 8)
    off = OFF_GATES // w

    def body(g1_ref, g2_ref, ys_ref, ym_ref, o_ref):
        o_ref[...] = (jax.nn.sigmoid(g1_ref[...]) * ys_ref[...]
                      + jax.nn.sigmoid(g2_ref[...]) * ym_ref[...]).astype(o_ref.dtype)

    blk = pl.BlockSpec((tr, w), lambda i: (i, 0))
    return pl.pallas_call(
        body, name=name, grid=(r // tr,),
        in_specs=[pl.BlockSpec((tr, w), lambda i: (i, off)), pl.BlockSpec((tr, w), lambda i: (i, off + 1)), blk, blk],
        out_specs=blk, out_shape=jax.ShapeDtypeStruct((r, w), BF16),
        compiler_params=_cp("parallel"),
    )(proj, proj, ys, ym)


def _merge_bwd(proj, ys, ym, dm, *, name):
    r, w = ys.shape
    tr = _pick(r, 512, 8)
    off = OFF_GATES // w

    def body(g1_ref, g2_ref, ys_ref, ym_ref, dm_ref, dg_ref, dys_ref, dym_ref):
        s1, s2 = jax.nn.sigmoid(g1_ref[...]), jax.nn.sigmoid(g2_ref[...])
        dmv = dm_ref[...]
        dys_ref[...] = (dmv * s1).astype(dys_ref.dtype)
        dym_ref[...] = (dmv * s2).astype(dym_ref.dtype)
        dg_ref[:, :w] = (dmv * ys_ref[...] * s1 * (1.0 - s1)).astype(dg_ref.dtype)
        dg_ref[:, w:] = (dmv * ym_ref[...] * s2 * (1.0 - s2)).astype(dg_ref.dtype)

    blk = pl.BlockSpec((tr, w), lambda i: (i, 0))
    return pl.pallas_call(
        body, name=name, grid=(r // tr,),
        in_specs=[pl.BlockSpec((tr, w), lambda i: (i, off)), pl.BlockSpec((tr, w), lambda i: (i, off + 1)), blk, blk, blk],
        out_specs=[pl.BlockSpec((tr, 2 * w), lambda i: (i, 0)), blk, blk],
        out_shape=[jax.ShapeDtypeStruct((r, 2 * w), BF16), jax.ShapeDtypeStruct((r, w), BF16),
                   jax.ShapeDtypeStruct((r, w), BF16)],
        compiler_params=_cp("parallel"),
    )(proj, proj, ys, ym, dm)


def _loss_fwd_bwd(y, target, *, name):
    r, w = y.shape
    tr = _pick(r, 512, 8)

    def body(y_ref, t_ref, l_ref, dy_ref):
        i = pl.program_id(0)
        e = y_ref[...] - t_ref[...]
        dy_ref[...] = e * (1.0 / w)
        part = jnp.sum(e * e, axis=0, keepdims=True) * (0.5 / w)

        @pl.when(i == 0)
        def _():
            l_ref[...] = part

        @pl.when(i > 0)
        def _():
            l_ref[...] += part

    blk = pl.BlockSpec((tr, w), lambda i: (i, 0))
    return pl.pallas_call(
        body, name=name, grid=(r // tr,),
        in_specs=[blk, blk],
        out_specs=[pl.BlockSpec((1, w), lambda i: (0, 0)), blk],
        out_shape=[jax.ShapeDtypeStruct((1, w), F32), jax.ShapeDtypeStruct((r, w), F32)],
        compiler_params=_cp("arbitrary"),
    )(y, target)


def _adamw(w, g, m, v, *, name):
    r, c = w.shape
    tr = _pick(r, max(8, (1 << 20) // (4 * c) // 8 * 8), 8)
    c1 = 1.0 - ADAM_B1 ** ADAM_STEP
    c2 = 1.0 - ADAM_B2 ** ADAM_STEP

    def body(w_ref, g_ref, m_ref, v_ref, d_ref, nm_ref, nv_ref):
        gv = g_ref[...]
        nm = ADAM_B1 * m_ref[...] + (1.0 - ADAM_B1) * gv
        nv = ADAM_B2 * v_ref[...] + (1.0 - ADAM_B2) * (gv * gv)
        nm_ref[...] = nm
        nv_ref[...] = nv
        d_ref[...] = -ADAM_LR * ((nm / c1) / (jnp.sqrt(nv / c2) + ADAM_EPS) + ADAM_WD * w_ref[...])

    blk = pl.BlockSpec((tr, c), lambda i: (i, 0))
    sh = jax.ShapeDtypeStruct((r, c), F32)
    return pl.pallas_call(
        body, name=name, grid=(r // tr,),
        in_specs=[blk] * 4, out_specs=[blk] * 3, out_shape=[sh] * 3,
        compiler_params=_cp("parallel"),
    )(w, g, m, v)


def _softplus(x):
    return jnp.maximum(x, 0.0) + jnp.log(1.0 + jnp.exp(-jnp.abs(x)))


def _dot_01(x, sel):
    sel_b = sel.astype(BF16)
    acc, rem = None, x
    for _ in range(3):
        piece = rem.astype(BF16)
        part = jnp.dot(piece, sel_b, preferred_element_type=F32)
        acc = part if acc is None else acc + part
        rem = rem - piece.astype(F32)
    return acc


def _ssd_common(dtr_ref, dtrT_ref, dtb_ref, dtbT_ref, al_ref, alT_ref, e_ref):
    L = SSD_CHUNK
    ri = lax.broadcasted_iota(jnp.int32, (L, L), 0)
    cj = lax.broadcasted_iota(jnp.int32, (L, L), 1)
    tril = (ri >= cj).astype(F32)
    triu = (ri <= cj).astype(F32)
    a = -jnp.exp(al_ref[...])
    aT = -jnp.exp(alT_ref[...])
    pre = dtr_ref[...] + dtb_ref[...]
    preT = dtrT_ref[...] + dtbT_ref[...]
    dt = _softplus(pre)
    dtT = _softplus(preT)
    acum = jnp.dot(tril, dt * a, precision=HI, preferred_element_type=F32)
    acumT = jnp.dot(dtT * aT, triu, precision=HI, preferred_element_type=F32)
    e = e_ref[...]
    dt_x = _dot_01(dt, e)
    acum_x = _dot_01(acum, e)
    last_x = acum_x[L - 1:L, :]
    return dict(ri=ri, cj=cj, tril=tril, triu=triu, a=a, aT=aT, pre=pre, preT=preT, dt=dt, dtT=dtT,
                acum=acum, acumT=acumT, dt_x=dt_x, eacum_x=jnp.exp(acum_x), w_x=jnp.exp(last_x - acum_x),
                elast_x=jnp.exp(last_x))


def _dot_nt(a, b):
    return lax.dot_general(a, b, (((1,), (1,)), ((), ())), preferred_element_type=F32)


def _dot_tn(a, b):
    return lax.dot_general(a, b, (((0,), (0,)), ((), ())), preferred_element_type=F32)


def _dot(a, b):
    return jnp.dot(a, b, preferred_element_type=F32)


def _ssd_specs(nc, rev):
    L = SSD_CHUNK
    ix = (lambda c: nc - 1 - c) if rev else (lambda c: c)
    return [
        pl.BlockSpec((L, SSD_D_INNER), lambda c: (ix(c), 0)),
        pl.BlockSpec((L, 512), lambda c: (ix(c), 4)),
        pl.BlockSpec((L, 512), lambda c: (ix(c), 5)),
        pl.BlockSpec((L, SSD_HEADS), lambda c: (ix(c), 0)),
        pl.BlockSpec((SSD_HEADS, L), lambda c: (0, ix(c))),
        pl.BlockSpec((1, SSD_HEADS), lambda c: (0, 0)),
        pl.BlockSpec((SSD_HEADS, 1), lambda c: (0, 0)),
        pl.BlockSpec((1, SSD_HEADS), lambda c: (0, 0)),
        pl.BlockSpec((SSD_HEADS, 1), lambda c: (0, 0)),
        pl.BlockSpec((1, SSD_D_INNER), lambda c: (0, 0)),
        pl.BlockSpec((SSD_HEADS, SSD_D_INNER), lambda c: (0, 0)),
    ]


def _ssd_fwd(xc, dtr, dtrT, dtb, dtbT, alog, alogT, dskx, expand, *, name):
    s = xc.shape[0]
    L = SSD_CHUNK
    nc = s // L

    def body(x_ref, b_ref, c_ref, dtr_ref, dtrT_ref, dtb_ref, dtbT_ref, al_ref, alT_ref, dsk_ref, e_ref,
             y_ref, st_ref, state):
        ci = pl.program_id(0)

        @pl.when(ci == 0)
        def _():
            state[...] = jnp.zeros_like(state)

        st_ref[0] = state[...]
        q = _ssd_common(dtr_ref, dtrT_ref, dtb_ref, dtbT_ref, al_ref, alT_ref, e_ref)
        causal = q["ri"] >= q["cj"]
        lane_lo = q["cj"] < 64
        x = x_ref[...]
        xdt = x * q["dt_x"]
        xdt_b = xdt.astype(BF16)
        xdtw_b = (xdt * q["w_x"]).astype(BF16)
        for g in range(SSD_GROUPS):
            bg = b_ref[:, 128 * g:128 * g + 128]
            cg_b = c_ref[:, 128 * g:128 * g + 128].astype(BF16)
            cb = _dot_nt(cg_b, bg.astype(BF16))
            bgT_b = bg.T.astype(BF16)
            s0 = state[g]
            for jj in range(4):
                j = 4 * g + jj
                sl = slice(128 * j, 128 * j + 128)
                sls = slice(128 * jj, 128 * jj + 128)
                ms = []
                for h in (2 * j, 2 * j + 1):
                    seg = q["acum"][:, h:h + 1] - q["acumT"][h:h + 1, :]
                    decay = jnp.exp(jnp.where(causal, seg, -jnp.inf))
                    ms.append((cb * decay).astype(BF16))
                mcat = jnp.concatenate(ms, axis=1)
                xp = xdt_b[:, sl]
                zero = jnp.zeros_like(xp)
                xstack = jnp.concatenate([jnp.where(lane_lo, xp, zero), jnp.where(lane_lo, zero, xp)], axis=0)
                y = _dot(mcat, xstack)
                y = y + q["eacum_x"][:, sl] * _dot(cg_b, s0[:, sls].astype(BF16))
                y = y + x[:, sl] * dsk_ref[:, sl]
                y_ref[:, sl] = y
                state[g, :, sls] = s0[:, sls] * q["elast_x"][:, sl] + _dot(bgT_b, xdtw_b[:, sl])

    return pl.pallas_call(
        body, name=name, grid=(nc,),
        in_specs=_ssd_specs(nc, False),
        out_specs=[pl.BlockSpec((L, SSD_D_INNER), lambda c: (c, 0)),
                   pl.BlockSpec((1, SSD_GROUPS, SSD_STATE, 512), lambda c: (c, 0, 0, 0))],
        out_shape=[jax.ShapeDtypeStruct((s, SSD_D_INNER), F32),
                   jax.ShapeDtypeStruct((nc, SSD_GROUPS, SSD_STATE, 512), F32)],
        scratch_shapes=[pltpu.VMEM((SSD_GROUPS, SSD_STATE, 512), F32)],
        compiler_params=_cp("arbitrary"),
    )(xc, xc, xc, dtr, dtrT, dtb, dtbT, alog, alogT, dskx, expand)


def _ssd_bwd(xc, dtr, dtrT, dtb, dtbT, alog, alogT, dskx, expand, expandT, states, dy, *, name):
    s = xc.shape[0]
    L = SSD_CHUNK
    H = SSD_HEADS
    nc = s // L

    def body(x_ref, b_ref, c_ref, dtr_ref, dtrT_ref, dtb_ref, dtbT_ref, al_ref, alT_ref, dsk_ref, e_ref,
             et_ref, st_ref, dy_ref,
             dxc_ref, ddtc_ref, ddtr_ref, dbc_ref, dbr_ref, dac_ref, dar_ref, ddsk_ref, dstate):
        ci = pl.program_id(0)

        @pl.when(ci == 0)
        def _():
            dstate[...] = jnp.zeros_like(dstate)
            dbc_ref[...] = jnp.zeros_like(dbc_ref)
            dbr_ref[...] = jnp.zeros_like(dbr_ref)
            dac_ref[...] = jnp.zeros_like(dac_ref)
            dar_ref[...] = jnp.zeros_like(dar_ref)
            ddsk_ref[...] = jnp.zeros_like(ddsk_ref)

        q = _ssd_common(dtr_ref, dtrT_ref, dtb_ref, dtbT_ref, al_ref, alT_ref, e_ref)
        ri, cj = q["ri"], q["cj"]
        causal = ri >= cj
        causalT = ri <= cj
        lane_lo = cj < 64
        lane_h = lax.broadcasted_iota(jnp.int32, (1, H), 1)
        sub_h = lax.broadcasted_iota(jnp.int32, (H, 1), 0)
        x = x_ref[...]
        dyv = dy_ref[...]
        xdt = x * q["dt_x"]
        xdt_b = xdt.astype(BF16)
        xdtw = xdt * q["w_x"]
        xdtw_b = xdtw.astype(BF16)
        edy = q["eacum_x"] * dyv
        edy_b = edy.astype(BF16)
        dyv_b = dyv.astype(BF16)
        dacum_col = jnp.zeros((L, H), F32)
        dacum_row = jnp.zeros((H, L), F32)
        dxdt_t, yoff_t, u_t, r_t = [], [], [], []
        for g in range(SSD_GROUPS):
            bg = b_ref[:, 128 * g:128 * g + 128]
            cg = c_ref[:, 128 * g:128 * g + 128]
            bg_b, cg_b = bg.astype(BF16), cg.astype(BF16)
            cb = _dot_nt(cg_b, bg_b)
            cbT = _dot_nt(bg_b, cg_b)
            cgT_b = cg.T.astype(BF16)
            s0 = st_ref[0, g]
            ds = dstate[g]
            s0_b, ds_b = s0.astype(BF16), ds.astype(BF16)
            dcb = jnp.zeros((L, L), F32)
            for jj in range(4):
                j = 4 * g + jj
                sl = slice(128 * j, 128 * j + 128)
                sls = slice(128 * jj, 128 * jj + 128)
                decs, mts = [], []
                for h in (2 * j, 2 * j + 1):
                    seg = q["acum"][:, h:h + 1] - q["acumT"][h:h + 1, :]
                    decs.append(jnp.exp(jnp.where(causal, seg, -jnp.inf)))
                    mts.append((cbT * jnp.exp(jnp.where(causalT, -seg, -jnp.inf))).astype(BF16))
                dyt_b = dyv_b[:, sl]
                zero = jnp.zeros_like(dyt_b)
                dystack = jnp.concatenate([jnp.where(lane_lo, dyt_b, zero), jnp.where(lane_lo, zero, dyt_b)], axis=0)
                dxs = _dot(jnp.concatenate(mts, axis=0), dyt_b)
                dxdt = jnp.where(lane_lo, dxs[:L], dxs[L:])
                dmcat = _dot_nt(dystack, xdt_b[:, sl])
                for idx, h in enumerate((2 * j, 2 * j + 1)):
                    dm = dmcat[L * idx:L * idx + L]
                    dcb = dcb + dm * decs[idx]
                    dseg = dm * cb * decs[idx]
                    dacum_col = dacum_col + jnp.sum(dseg, axis=1, keepdims=True) * (lane_h == h).astype(F32)
                    dacum_row = dacum_row - (sub_h == h).astype(F32) * jnp.sum(dseg, axis=0, keepdims=True)
                gmat = _dot(cg_b, s0_b[:, sls])
                yoff_t.append(edy[:, sl] * gmat)
                qm = _dot(bg_b, ds_b[:, sls])
                dxdt_t.append(dxdt + qm * q["w_x"][:, sl])
                u_t.append(qm * xdtw[:, sl])
                r_t.append(ds[:, sls] * s0[:, sls] * q["elast_x"][:, sl])
                dstate[g, :, sls] = ds[:, sls] * q["elast_x"][:, sl] + _dot(cgT_b, edy_b[:, sl])
            gsl = slice(512 * g, 512 * g + 512)
            dcb_b = dcb.astype(BF16)
            dcg = _dot(dcb_b, bg_b) + _dot_nt(edy_b[:, gsl], s0_b)
            dbg = _dot(dcb.T.astype(BF16), cg_b) + _dot_nt(xdtw_b[:, gsl], ds_b)
            dxc_ref[:, SSD_D_INNER + 128 * g:SSD_D_INNER + 128 * g + 128] = dbg
            dxc_ref[:, SSD_D_INNER + 512 + 128 * g:SSD_D_INNER + 512 + 128 * g + 128] = dcg
        et = et_ref[...]
        dxdt_all = jnp.concatenate(dxdt_t, axis=1)
        yoff = jnp.concatenate(yoff_t, axis=1)
        uu = jnp.concatenate(u_t, axis=1)
        rr = jnp.concatenate(r_t, axis=1)
        dacum_col = dacum_col + _dot_01(yoff - uu, et)
        dlast = jnp.sum(_dot_01(uu + rr, et), axis=0, keepdims=True)
        row_lh = lax.broadcasted_iota(jnp.int32, (L, H), 0)
        dacum_col = dacum_col + jnp.where(row_lh == L - 1, dlast, 0.0)
        d_dta_col = jnp.dot(q["triu"], dacum_col, precision=HI, preferred_element_type=F32)
        d_dta_row = jnp.dot(dacum_row, q["tril"], precision=HI, preferred_element_type=F32)
        ddt_col = d_dta_col * q["a"] + _dot_01(dxdt_all * x, et)
        ddt_row = d_dta_row * q["aT"]
        ddtr_col = ddt_col * jax.nn.sigmoid(q["pre"])
        ddtr_row = ddt_row * jax.nn.sigmoid(q["preT"])
        ddtc_ref[...] = ddtr_col
        ddtr_ref[...] = ddtr_row
        dac_ref[...] += jnp.sum(d_dta_col * q["dt"], axis=0, keepdims=True)
        dar_ref[...] += jnp.sum(d_dta_row * q["dtT"], axis=1, keepdims=True)
        dbc_ref[...] += jnp.sum(ddtr_col, axis=0, keepdims=True)
        dbr_ref[...] += jnp.sum(ddtr_row, axis=1, keepdims=True)
        ddsk_ref[...] += jnp.sum(dyv * x, axis=0, keepdims=True)
        dxc_ref[:, 0:SSD_D_INNER] = dxdt_all * q["dt_x"] + dyv * dsk_ref[...]

    rv = lambda c: nc - 1 - c
    in_specs = _ssd_specs(nc, True) + [
        pl.BlockSpec((SSD_D_INNER, H), lambda c: (0, 0)),
        pl.BlockSpec((1, SSD_GROUPS, SSD_STATE, 512), lambda c: (rv(c), 0, 0, 0)),
        pl.BlockSpec((L, SSD_D_INNER), lambda c: (rv(c), 0)),
    ]
    vec_c = pl.BlockSpec((1, H), lambda c: (0, 0))
    vec_r = pl.BlockSpec((H, 1), lambda c: (0, 0))
    return pl.pallas_call(
        body, name=name, grid=(nc,),
        in_specs=in_specs,
        out_specs=[pl.BlockSpec((L, SSD_CONV_DIM), lambda c: (rv(c), 0)),
                   pl.BlockSpec((L, H), lambda c: (rv(c), 0)),
                   pl.BlockSpec((H, L), lambda c: (0, rv(c))),
                   vec_c, vec_r, vec_c, vec_r,
                   pl.BlockSpec((1, SSD_D_INNER), lambda c: (0, 0))],
        out_shape=[jax.ShapeDtypeStruct((s, SSD_CONV_DIM), F32),
                   jax.ShapeDtypeStruct((s, H), F32), jax.ShapeDtypeStruct((H, s), F32),
                   jax.ShapeDtypeStruct((1, H), F32), jax.ShapeDtypeStruct((H, 1), F32),
                   jax.ShapeDtypeStruct((1, H), F32), jax.ShapeDtypeStruct((H, 1), F32),
                   jax.ShapeDtypeStruct((1, SSD_D_INNER), F32)],
        scratch_shapes=[pltpu.VMEM((SSD_GROUPS, SSD_STATE, 512), F32)],
        compiler_params=_cp("arbitrary"),
    )(xc, xc, xc, dtr, dtrT, dtb, dtbT, alog, alogT, dskx, expand, expandT, states, dy)


QK_PAD = 256
MLA_TS = 512


def _rope_tables4(pos):
    inv = 1.0 / (ROPE_THETA ** (jnp.arange(0, MLA_ROPE, 2, dtype=F32) / MLA_ROPE))
    ang = pos.astype(F32)[:, None] * inv
    c, s = jnp.cos(ang), jnp.sin(ang)
    return jnp.tile(c, (1, 4)), jnp.concatenate([-s, s, -s, s], axis=1)


def _mla_gains(qg, kg):
    z = jnp.zeros((LANE - MLA_ROPE,), F32)
    return (qg[:MLA_NOPE][None], jnp.concatenate([qg[MLA_NOPE:], z])[None],
            kg[:MLA_NOPE][None], jnp.concatenate([kg[MLA_NOPE:], z])[None])


def _rope_swap(t, first):
    return jnp.where(first, pltpu.roll(t, 96, 1), pltpu.roll(t, 32, 1))


def _mla_prep_specs(ts):
    row = lambda w, c=0: pl.BlockSpec((ts, w), lambda i: (i, c))
    vec = pl.BlockSpec((1, LANE), lambda i: (0, 0))
    return [row(MLA_HEADS * MLA_QK), row(2 * MLA_HEADS * MLA_NOPE), row(LANE, OFF_KRDT // LANE), row(LANE), row(LANE),
            vec, vec, vec, vec]


def _mla_prep_fwd(qraw, kvraw, proj, cos4, sin4, gqn, gqr, gkn, gkr, *, name):
    s = qraw.shape[0]
    ts = _pick(s, MLA_TS, 8)

    def body(q_ref, kv_ref, kr_ref, cos_ref, sin_ref, gqn_ref, gqr_ref, gkn_ref, gkr_ref, qo_ref, ko_ref):
        lane = lax.broadcasted_iota(jnp.int32, (ts, LANE), 1)
        lo = lane < 64
        first = (lane % 64) < 32
        cos, sin = cos_ref[...], sin_ref[...]
        kr = jnp.where(lo, kr_ref[...], 0.0)
        ssq_kr = jnp.sum(kr * kr, axis=-1, keepdims=True)

        def head(xn, xr, ssq_r, gn, gr):
            rs = lax.rsqrt((jnp.sum(xn * xn, axis=-1, keepdims=True) + ssq_r) * (1.0 / MLA_QK) + EPS)
            yr = xr * rs * gr
            return xn * rs * gn, yr * cos + _rope_swap(yr, first) * sin

        for h in range(MLA_HEADS):
            tile = q_ref[:, MLA_HEADS * MLA_NOPE + LANE * (h // 2):MLA_HEADS * MLA_NOPE + LANE * (h // 2) + LANE]
            qr = jnp.where(lo, tile if h % 2 == 0 else pltpu.roll(tile, 64, 1), 0.0)
            on, orr = head(q_ref[:, LANE * h:LANE * h + LANE], qr, jnp.sum(qr * qr, axis=-1, keepdims=True),
                           gqn_ref[...], gqr_ref[...])
            qo_ref[h, :, 0:LANE] = (on * ATT_SCALE).astype(BF16)
            qo_ref[h, :, LANE:QK_PAD] = (orr * ATT_SCALE).astype(BF16)
            on, orr = head(kv_ref[:, LANE * h:LANE * h + LANE], kr, ssq_kr, gkn_ref[...], gkr_ref[...])
            ko_ref[h, :, 0:LANE] = on.astype(BF16)
            ko_ref[h, :, LANE:QK_PAD] = orr.astype(BF16)

    out = pl.BlockSpec((MLA_HEADS, ts, QK_PAD), lambda i: (0, i, 0))
    sh = jax.ShapeDtypeStruct((MLA_HEADS, s, QK_PAD), BF16)
    return pl.pallas_call(
        body, name=name, grid=(s // ts,),
        in_specs=_mla_prep_specs(ts), out_specs=[out, out], out_shape=[sh, sh],
        compiler_params=_cp("parallel"),
    )(qraw, kvraw, proj, cos4, sin4, gqn, gqr, gkn, gkr)


def _mla_prep_bwd(qraw, kvraw, proj, cos4, sin4, gqn, gqr, gkn, gkr, dq, dk, *, name):
    s = qraw.shape[0]
    ts = _pick(s, MLA_TS, 8)

    def body(q_ref, kv_ref, kr_ref, cos_ref, sin_ref, gqn_ref, gqr_ref, gkn_ref, gkr_ref, dq_ref, dk_ref,
             dqraw_ref, dkn_ref, dkr_ref, dgqn_ref, dgqr_ref, dgkn_ref, dgkr_ref):
        i = pl.program_id(0)

        @pl.when(i == 0)
        def _():
            for r in (dgqn_ref, dgqr_ref, dgkn_ref, dgkr_ref):
                r[...] = jnp.zeros_like(r)

        lane = lax.broadcasted_iota(jnp.int32, (ts, LANE), 1)
        lo = lane < 64
        first = (lane % 64) < 32
        cos, sin = cos_ref[...], sin_ref[...]
        kr = jnp.where(lo, kr_ref[...], 0.0)
        ssq_kr = jnp.sum(kr * kr, axis=-1, keepdims=True)

        def head(xn, xr, ssq_r, gn, gr, don, dor):
            rs = lax.rsqrt((jnp.sum(xn * xn, axis=-1, keepdims=True) + ssq_r) * (1.0 / MLA_QK) + EPS)
            xhn, xhr = xn * rs, xr * rs
            dor = jnp.where(lo, dor, 0.0)
            dyr = dor * cos + _rope_swap(dor * sin, first)
            dxn, dxr = don * gn, dyr * gr
            mm = (jnp.sum(dxn * xhn, axis=-1, keepdims=True) + jnp.sum(dxr * xhr, axis=-1, keepdims=True)) * (1.0 / MLA_QK)
            return (rs * (dxn - xhn * mm), rs * (dxr - xhr * mm),
                    jnp.sum(don * xhn, axis=0, keepdims=True), jnp.sum(dyr * xhr, axis=0, keepdims=True))

        dkr_acc = jnp.zeros((ts, LANE), F32)
        prev = None
        for h in range(MLA_HEADS):
            c0 = MLA_HEADS * MLA_NOPE + LANE * (h // 2)
            tile = q_ref[:, c0:c0 + LANE]
            qr = jnp.where(lo, tile if h % 2 == 0 else pltpu.roll(tile, 64, 1), 0.0)
            dn, dr, gn_p, gr_p = head(q_ref[:, LANE * h:LANE * h + LANE], qr, jnp.sum(qr * qr, axis=-1, keepdims=True),
                                      gqn_ref[...], gqr_ref[...], dq_ref[h, :, 0:LANE], dq_ref[h, :, LANE:QK_PAD])
            dqraw_ref[:, LANE * h:LANE * h + LANE] = dn.astype(dqraw_ref.dtype)
            dgqn_ref[...] += gn_p
            dgqr_ref[...] += gr_p
            if h % 2 == 0:
                prev = dr
            else:
                dqraw_ref[:, c0:c0 + LANE] = (prev + pltpu.roll(dr, 64, 1)).astype(dqraw_ref.dtype)
            dn, dr, gn_p, gr_p = head(kv_ref[:, LANE * h:LANE * h + LANE], kr, ssq_kr, gkn_ref[...], gkr_ref[...],
                                      dk_ref[h, :, 0:LANE], dk_ref[h, :, LANE:QK_PAD])
            dkn_ref[:, LANE * h:LANE * h + LANE] = dn.astype(dkn_ref.dtype)
            dkr_acc = dkr_acc + dr
            dgkn_ref[...] += gn_p
            dgkr_ref[...] += gr_p
        dkr_ref[...] = dkr_acc

    row = lambda w: pl.BlockSpec((ts, w), lambda i: (i, 0))
    vec = pl.BlockSpec((1, LANE), lambda i: (0, 0))
    dspec = pl.BlockSpec((MLA_HEADS, ts, QK_PAD), lambda i: (0, i, 0))
    vsh = jax.ShapeDtypeStruct((1, LANE), F32)
    return pl.pallas_call(
        body, name=name, grid=(s // ts,),
        in_specs=_mla_prep_specs(ts) + [dspec, dspec],
        out_specs=[row(MLA_HEADS * MLA_QK), row(MLA_HEADS * MLA_NOPE), row(LANE), vec, vec, vec, vec],
        out_shape=[jax.ShapeDtypeStruct((s, MLA_HEADS * MLA_QK), BF16), jax.ShapeDtypeStruct((s, MLA_HEADS * MLA_NOPE), BF16),
                   jax.ShapeDtypeStruct((s, LANE), F32), vsh, vsh, vsh, vsh],
        compiler_params=_cp("arbitrary"),
    )(qraw, kvraw, proj, cos4, sin4, gqn, gqr, gkn, gkr, dq, dk)


ATT_T = 1024
ATT_T_FWD = 2048
ATT_SCALE = MLA_QK ** -0.5


def _attn_fwd(q, k, kvraw, *, name):
    nh, s, _ = q.shape
    t = _pick(s, ATT_T_FWD, LANE)
    nb = s // t

    def body(q_ref, k_ref, v_ref, o_ref, lse_ref, m_ref, l_ref, acc_ref):
        i, j = pl.program_id(1), pl.program_id(2)

        @pl.when(j == 0)
        def _():
            m_ref[...] = jnp.full_like(m_ref, -jnp.inf)
            l_ref[...] = jnp.zeros_like(l_ref)
            acc_ref[...] = jnp.zeros_like(acc_ref)

        def update(rows, keys, masked):
            sc = _dot_nt(q_ref[0, rows, :], k_ref[0, keys, :])
            if masked:
                ri = lax.broadcasted_iota(jnp.int32, sc.shape, 0) + rows.start
                cj = lax.broadcasted_iota(jnp.int32, sc.shape, 1) + keys.start
                sc = jnp.where(ri >= cj, sc, -jnp.inf)
            m_old = m_ref[rows, :]
            m_new = jnp.maximum(m_old, jnp.max(sc, axis=-1, keepdims=True))
            alpha = jnp.exp(m_old - m_new)
            p = jnp.exp(sc - m_new)
            l_ref[rows, :] = alpha * l_ref[rows, :] + jnp.sum(p, axis=-1, keepdims=True)
            acc_ref[rows, :] = alpha * acc_ref[rows, :] + _dot(p.astype(BF16), v_ref[keys, :].astype(BF16))
            m_ref[rows, :] = m_new

        @pl.when(j < i)
        def _():
            update(slice(0, t), slice(0, t), False)

        @pl.when(j == i)
        def _():
            if t % 256 == 0:
                update(slice(0, t // 2), slice(0, t // 2), True)
                update(slice(t // 2, t), slice(0, t), True)
            else:
                update(slice(0, t), slice(0, t), True)
            o_ref[...] = acc_ref[...] / l_ref[...]
            lse_ref[0] = m_ref[...] + jnp.log(l_ref[...])

    return pl.pallas_call(
        body, name=name, grid=(nh, nb, nb),
        in_specs=[pl.BlockSpec((1, t, QK_PAD), lambda h, i, j: (h, i, 0)),
                  pl.BlockSpec((1, t, QK_PAD), lambda h, i, j: (h, jnp.minimum(j, i), 0)),
                  pl.BlockSpec((t, MLA_V), lambda h, i, j: (jnp.minimum(j, i), nh + h))],
        out_specs=[pl.BlockSpec((t, MLA_V), lambda h, i, j: (i, h)),
                   pl.BlockSpec((1, t, 1), lambda h, i, j: (h, i, 0))],
        out_shape=[jax.ShapeDtypeStruct((s, nh * MLA_V), F32), jax.ShapeDtypeStruct((nh, s, 1), F32)],
        scratch_shapes=[pltpu.VMEM((t, 1), F32), pltpu.VMEM((t, 1), F32), pltpu.VMEM((t, MLA_V), F32)],
        compiler_params=_cp("parallel", "parallel", "arbitrary"),
    )(q, k, kvraw)


def _attn_bwd(q, k, kvraw, o, lse, do, *, name):
    nh, s, _ = q.shape
    t = _pick(s, ATT_T, LANE)
    nb = s // t

    def body(q_ref, k_ref, v_ref, o_ref, lse_ref, do_ref, dq_ref, dk_ref, dv_ref, dk_acc, dv_acc):
        j, i = pl.program_id(1), pl.program_id(2)

        @pl.when(i == 0)
        def _():
            dk_acc[...] = jnp.zeros_like(dk_acc)
            dv_acc[...] = jnp.zeros_like(dv_acc)

        def step(diagonal):
            qv, kv = q_ref[0], k_ref[0]
            sc = _dot_nt(qv, kv)
            if diagonal:
                ri = lax.broadcasted_iota(jnp.int32, (t, t), 0)
                cj = lax.broadcasted_iota(jnp.int32, (t, t), 1)
                sc = jnp.where(ri >= cj, sc, -jnp.inf)
            p = jnp.exp(sc - lse_ref[0])
            dov = do_ref[...]
            delta = jnp.sum(dov * o_ref[...], axis=-1, keepdims=True)
            do_b = dov.astype(BF16)
            dv_acc[...] += _dot_tn(p.astype(BF16), do_b)
            dp = _dot_nt(do_b, v_ref[...].astype(BF16))
            ds_b = (p * (dp - delta)).astype(BF16)
            dk_acc[...] += _dot_tn(ds_b, qv)
            dq_part = _dot(ds_b, kv) * ATT_SCALE
            rows = pl.ds(pl.multiple_of(i * t, t), t)

            @pl.when(j == 0)
            def _():
                dq_ref[0, rows, :] = dq_part

            @pl.when(j > 0)
            def _():
                dq_ref[0, rows, :] += dq_part

        @pl.when(i > j)
        def _():
            step(False)

        @pl.when(i == j)
        def _():
            step(True)

        @pl.when(i == nb - 1)
        def _():
            dk_ref[0] = dk_acc[...]
            dv_ref[...] = dv_acc[...].astype(dv_ref.dtype)

    qi = lambda h, j, i: jnp.maximum(i, j)
    return pl.pallas_call(
        body, name=name, grid=(nh, nb, nb),
        in_specs=[pl.BlockSpec((1, t, QK_PAD), lambda h, j, i: (h, qi(h, j, i), 0)),
                  pl.BlockSpec((1, t, QK_PAD), lambda h, j, i: (h, j, 0)),
                  pl.BlockSpec((t, MLA_V), lambda h, j, i: (j, nh + h)),
                  pl.BlockSpec((t, MLA_V), lambda h, j, i: (qi(h, j, i), h)),
                  pl.BlockSpec((1, t, 1), lambda h, j, i: (h, qi(h, j, i), 0)),
                  pl.BlockSpec((t, MLA_V), lambda h, j, i: (qi(h, j, i), h))],
        out_specs=[pl.BlockSpec((1, s, QK_PAD), lambda h, j, i: (h, 0, 0)),
                   pl.BlockSpec((1, t, QK_PAD), lambda h, j, i: (h, j, 0)),
                   pl.BlockSpec((t, MLA_V), lambda h, j, i: (j, h))],
        out_shape=[jax.ShapeDtypeStruct((nh, s, QK_PAD), F32), jax.ShapeDtypeStruct((nh, s, QK_PAD), F32),
                   jax.ShapeDtypeStruct((s, nh * MLA_V), BF16)],
        scratch_shapes=[pltpu.VMEM((t, QK_PAD), F32), pltpu.VMEM((t, MLA_V), F32)],
        compiler_params=_cp("parallel", "arbitrary", "arbitrary"),
    )(q, k, kvraw, o, lse, do)


def _ffn_fwd(h, w, tag):
    n = _rms_fwd(h, w["ln"], name=tag + "_norm")
    act, gate, up = _ffn_up(n, w["w13"], name=tag + "_up")
    out = _matmul(act, w["w2"], "nn", name=tag + "_down", scale=0.5, res=h)
    return out, (h, n, gate, up, act)


def _ffn_bwd(dout, saved, w, tag):
    h, n, gate, up, act = saved
    dact = _matmul(dout, w["w2"], "nt", name=tag + "_down_dx", scale=0.5, out_dtype=BF16)
    dw2 = _matmul(act, dout, "tn", name=tag + "_down_dw", scale=0.5)
    dgu = _swiglu_bwd(gate, up, dact, name=tag + "_act_bwd")
    dw13 = _matmul(n, dgu, "tn", name=tag + "_up_dw")
    dn = _matmul(dgu, w["w13"], "nt", name=tag + "_up_dx")
    dh, dln = _rms_bwd(h, w["ln"], dn, name=tag + "_norm_bwd", res=dout)
    return dh, dict(ln=dln, w13=dw13, w2=dw2)


def _mixer_fwd(h, w, rope, tag):
    cos4, sin4 = rope
    u = _rms_fwd(h, w["ln_mix"], name=tag + "_norm")
    proj = _matmul(u, w["w_in"], "nn", name=tag + "_in")
    xc = _conv_fwd(proj, w["conv_w"], w["conv_b"], name=tag + "_conv")
    dtr = proj[:, OFF_KRDT + MLA_ROPE:OFF_KRDT + MLA_ROPE + SSD_HEADS]
    dtrT = dtr.T
    y, states = _ssd_fwd(xc, dtr, dtrT, *w["ssd_aux"], name=tag + "_ssd")
    yn = _gated_rms_fwd(y, proj, w["ssd_norm"], name=tag + "_ssd_norm")
    y_ssd = _matmul(yn, w["w_ssd_out"], "nn", name=tag + "_ssd_out")
    cqn = _rms_fwd(proj, w["q_lora_norm"], name=tag + "_q_lora_norm", col=OFF_CQ // MLA_Q_LORA, width=MLA_Q_LORA)
    qraw = _matmul(cqn, w["w_uq"], "nn", name=tag + "_uq")
    ckvn = _rms_fwd(proj, w["kv_lora_norm"], name=tag + "_kv_lora_norm", col=OFF_CKV // MLA_KV_LORA, width=MLA_KV_LORA)
    kvraw = _matmul(ckvn, w["w_ukv"], "nn", name=tag + "_ukv")
    qf, kf = _mla_prep_fwd(qraw, kvraw, proj, cos4, sin4, *w["qk_gains"], name=tag + "_qk_prep")
    o, lse = _attn_fwd(qf, kf, kvraw, name=tag + "_attn")
    y_mla = _matmul(o, w["w_mla_out"], "nn", name=tag + "_mla_out")
    merged = _merge_fwd(proj, y_ssd, y_mla, name=tag + "_merge")
    out = _matmul(merged, w["w_o"], "nn", name=tag + "_o", res=h)
    saved = dict(h=h, u=u, proj=proj, xc=xc, dtr=dtr, dtrT=dtrT, states=states, y=y, yn=yn, y_ssd=y_ssd, cqn=cqn,
                 qraw=qraw, ckvn=ckvn, kvraw=kvraw, qf=qf, kf=kf, o=o, lse=lse, y_mla=y_mla, merged=merged)
    return out, saved


def _mixer_bwd(dout, s, w, rope, tag):
    cos4, sin4 = rope
    g = {}
    proj = s["proj"]
    dmerged = _matmul(dout, w["w_o"], "nt", name=tag + "_o_dx")
    g["w_o"] = _matmul(s["merged"], dout, "tn", name=tag + "_o_dw")
    dgates, dy_ssd, dy_mla = _merge_bwd(proj, s["y_ssd"], s["y_mla"], dmerged, name=tag + "_merge_bwd")
    do = _matmul(dy_mla, w["w_mla_out"], "nt", name=tag + "_mla_out_dx")
    g["w_mla_out"] = _matmul(s["o"], dy_mla, "tn", name=tag + "_mla_out_dw")
    dqf, dkf, dv = _attn_bwd(s["qf"], s["kf"], s["kvraw"], s["o"], s["lse"], do, name=tag + "_attn_bwd")
    dqraw, dkn, dkrt, dgqn, dgqr, dgkn, dgkr = _mla_prep_bwd(
        s["qraw"], s["kvraw"], proj, cos4, sin4, *w["qk_gains"], dqf, dkf, name=tag + "_qk_prep_bwd")
    g["q_norm"] = jnp.concatenate([dgqn[0], dgqr[0, :MLA_ROPE]])
    g["k_norm"] = jnp.concatenate([dgkn[0], dgkr[0, :MLA_ROPE]])
    dkvraw = jnp.concatenate([dkn, dv], axis=1)
    dcqn = _matmul(dqraw, w["w_uq"], "nt", name=tag + "_uq_dx")
    g["w_uq"] = _matmul(s["cqn"], dqraw, "tn", name=tag + "_uq_dw")
    dckvn = _matmul(dkvraw, w["w_ukv"], "nt", name=tag + "_ukv_dx")
    g["w_ukv"] = _matmul(s["ckvn"], dkvraw, "tn", name=tag + "_ukv_dw")
    dcq, g["q_lora_norm"] = _rms_bwd(proj, w["q_lora_norm"], dcqn, name=tag + "_q_lora_norm_bwd",
                                     col=OFF_CQ // MLA_Q_LORA, width=MLA_Q_LORA, out_dtype=BF16)
    dckv, g["kv_lora_norm"] = _rms_bwd(proj, w["kv_lora_norm"], dckvn, name=tag + "_kv_lora_norm_bwd",
                                       col=OFF_CKV // MLA_KV_LORA, width=MLA_KV_LORA, out_dtype=BF16)
    dyn = _matmul(dy_ssd, w["w_ssd_out"], "nt", name=tag + "_ssd_out_dx")
    g["w_ssd_out"] = _matmul(s["yn"], dy_ssd, "tn", name=tag + "_ssd_out_dw")
    dy, dz, g["ssd_norm"] = _gated_rms_bwd(s["y"], proj, w["ssd_norm"], dyn, name=tag + "_ssd_norm_bwd")
    aux = w["ssd_aux"]
    dxc, ddt_c, ddt_r, dbias_c, dbias_r, da_c, da_r, ddsk = _ssd_bwd(
        s["xc"], s["dtr"], s["dtrT"], *aux, aux[-1].T, s["states"], dy, name=tag + "_ssd_bwd")
    g["dt_bias"] = dbias_c[0] + dbias_r[:, 0]
    g["a_log"] = (da_c[0] + da_r[:, 0]) * (-jnp.exp(aux[2][0]))
    g["d_skip"] = jnp.sum(ddsk.reshape(SSD_HEADS, SSD_HEAD_DIM), axis=1)
    dpre, g["conv_w"], g["conv_b"] = _conv_bwd_pre(proj, w["conv_w"], w["conv_b"], dxc, name=tag + "_conv_bwd_pre")
    dxbc = _conv_bwd_x(dpre, w["conv_w"], name=tag + "_conv_bwd_x")
    ddtr = ddt_c + ddt_r.T
    dkrdt = jnp.concatenate([dkrt[:, :MLA_ROPE], ddtr, jnp.zeros((ddtr.shape[0], LANE - MLA_ROPE - SSD_HEADS), F32)], axis=1)
    dproj = _join_cols([dz, dxbc, dgates, dcq, dckv, dkrdt.astype(BF16)], name=tag + "_dproj")
    du = _matmul(dproj, w["w_in"], "nt", name=tag + "_in_dx")
    g["w_in"] = _matmul(s["u"], dproj, "tn", name=tag + "_in_dw")
    dh, g["ln_mix"] = _rms_bwd(s["h"], w["ln_mix"], du, name=tag + "_norm_bwd", res=dout)
    return dh, g


W_NAMES = ["ln_ffn1", "ffn1_w13", "ffn1_w2", "ln_mix", "w_in", "conv_w", "conv_b", "dt_bias", "a_log", "d_skip",
           "ssd_norm", "w_ssd_out", "q_lora_norm", "w_uq", "kv_lora_norm", "w_ukv", "q_norm", "k_norm", "w_mla_out",
           "w_o", "ln_ffn2", "ffn2_w13", "ffn2_w2"]
SHARD_AXIS = {"ffn1_w13": 2, "ffn1_w2": 1, "w_in": 2, "conv_w": 2, "w_ssd_out": 1, "w_uq": 2, "w_ukv": 2,
              "w_mla_out": 1, "w_o": 1, "ffn2_w13": 2, "ffn2_w2": 1}
SHARDED = [n for n in W_NAMES if n in SHARD_AXIS and n != "conv_w"] + ["conv_w"]
REPLICATED = [n for n in W_NAMES if n not in SHARD_AXIS]
N_CHIPS = 4
N_DEV = 8
PACK_COLS = 1024
IN_SPLIT = (2048, 3072, 32, 512, 256, 64, 2048)


def _pack_mats(arrs, rows, dtype):
    mats = [a.astype(dtype).reshape(-1, PACK_COLS) for a in arrs]
    used = sum(m.shape[0] for m in mats)
    return mats[:-1] + [jnp.concatenate([mats[-1], jnp.zeros((rows - used, PACK_COLS), dtype)], axis=0)]


STAGE_ROWS = 1024
STAGE_BUFS = 4
STAGE_AHEAD = 2


def _stack_rows(mats, *, name):
    ncol, dtype = mats[0].shape[1], mats[0].dtype
    total = sum(m.shape[0] for m in mats)
    chunks, at = [], 0
    for i, m in enumerate(mats):
        for st in range(0, m.shape[0], STAGE_ROWS):
            sz = min(STAGE_ROWS, m.shape[0] - st)
            chunks.append((i, st, at + st, sz))
        at += m.shape[0]
    n = len(mats)

    def body(*refs):
        ins, out_ref, buf, sem_in, sem_out = refs[:n], refs[n], refs[n + 1], refs[n + 2], refs[n + 3]

        nc = len(chunks)

        def get(idx):
            i, st, _, sz = chunks[idx]
            k = idx % STAGE_BUFS
            return pltpu.make_async_copy(ins[i].at[pl.ds(st, sz), :], buf.at[k, pl.ds(0, sz), :], sem_in.at[k])

        def put(idx):
            _, _, dst, sz = chunks[idx]
            k = idx % STAGE_BUFS
            return pltpu.make_async_copy(buf.at[k, pl.ds(0, sz), :], out_ref.at[pl.ds(dst, sz), :], sem_out.at[k])

        for idx in range(min(STAGE_AHEAD, nc)):
            get(idx).start()
        for idx in range(nc):
            nxt = idx + STAGE_AHEAD
            if nxt < nc:
                if nxt >= STAGE_BUFS:
                    put(nxt - STAGE_BUFS).wait()
                get(nxt).start()
            get(idx).wait()
            put(idx).start()
        for idx in range(max(0, nc - STAGE_BUFS), nc):
            put(idx).wait()

    return pl.pallas_call(
        body, name=name, out_shape=jax.ShapeDtypeStruct((total, ncol), dtype),
        in_specs=[ANY] * n, out_specs=ANY,
        scratch_shapes=[pltpu.VMEM((STAGE_BUFS, STAGE_ROWS, ncol), dtype), pltpu.SemaphoreType.DMA((STAGE_BUFS,)),
                        pltpu.SemaphoreType.DMA((STAGE_BUFS,))],
    )(*mats)


def _join_cols(pieces, *, name):
    s, dtype = pieces[0].shape[0], pieces[0].dtype
    widths = [p.shape[1] for p in pieces]
    tr = _pick(s, 256, 16)

    def body(*refs):
        o_ref, at = refs[-1], 0
        for ref, w in zip(refs[:-1], widths):
            o_ref[:, at:at + w] = ref[...]
            at += w

    return pl.pallas_call(
        body, name=name, grid=(s // tr,),
        in_specs=[pl.BlockSpec((tr, w), lambda i: (i, 0)) for w in widths],
        out_specs=pl.BlockSpec((tr, sum(widths)), lambda i: (i, 0)),
        out_shape=jax.ShapeDtypeStruct((s, sum(widths)), dtype),
        compiler_params=_cp("parallel"),
    )(*pieces)


def _pack(arrs, rows, dtype, *, name):
    return _stack_rows(_pack_mats(arrs, rows, dtype), name=name)


def _unpack(packed, shapes):
    out, at = [], 0
    for sh in shapes:
        r = math.prod(sh) // PACK_COLS
        out.append(packed[at:at + r].reshape(sh))
        at += r
    return out


def _unpack_flat(flat, shapes):
    out, at = [], 0
    for sh in shapes:
        n = math.prod(sh)
        out.append(flat[at:at + n].reshape(sh))
        at += n
    return out


def _pack_rows(shapes):
    n = sum(math.prod(sh) for sh in shapes)
    return -(-n // (PACK_COLS * 1024)) * 1024


def _in_perm(w_in):
    z, xbc, dt, cq, ckv, kr, gates = jnp.split(w_in, list(np_cumsum(IN_SPLIT))[:-1], axis=1)
    return jnp.concatenate([z, xbc, gates, cq, ckv, kr, dt, jnp.zeros((w_in.shape[0], PROJ_W - sum(IN_SPLIT)), w_in.dtype)], axis=1)


def _in_unperm(g):
    z, xbc, gates, cq, ckv = (g[:, OFF_Z:OFF_XBC], g[:, OFF_XBC:OFF_GATES], g[:, OFF_GATES:OFF_CQ], g[:, OFF_CQ:OFF_CKV],
                              g[:, OFF_CKV:OFF_KRDT])
    kr = g[:, OFF_KRDT:OFF_KRDT + MLA_ROPE]
    dt = g[:, OFF_KRDT + MLA_ROPE:OFF_KRDT + MLA_ROPE + SSD_HEADS]
    return jnp.concatenate([z, xbc, dt, cq, ckv, kr, gates], axis=1)


def np_cumsum(sizes):
    out, t = [], 0
    for s in sizes:
        t += s
        out.append(t)
    return out


def _head_perm(w, first):
    r = w.shape[0]
    w3 = w.reshape(r, MLA_HEADS, -1)
    return jnp.concatenate([w3[:, :, :first].reshape(r, -1), w3[:, :, first:].reshape(r, -1)], axis=1)


def _head_unperm(g, first):
    r = g.shape[0]
    rest = g.shape[1] // MLA_HEADS - first
    a = g[:, :MLA_HEADS * first].reshape(r, MLA_HEADS, first)
    b = g[:, MLA_HEADS * first:].reshape(r, MLA_HEADS, rest)
    return jnp.concatenate([a, b], axis=2).reshape(r, -1)


def _layer_weights(full, l):
    row = lambda n: full[n][l][None].astype(F32)
    expand = jnp.repeat(jnp.eye(SSD_HEADS, dtype=F32), SSD_HEAD_DIM, axis=1)
    dtb, al, dsk = full["dt_bias"][l], full["a_log"][l], full["d_skip"][l]
    mixer = dict(
        ln_mix=row("ln_mix"), w_in=_in_perm(full["w_in"][l]), conv_w=full["conv_w"][l], conv_b=row("conv_b"),
        ssd_aux=(dtb[None], dtb[:, None], al[None], al[:, None], jnp.repeat(dsk, SSD_HEAD_DIM)[None], expand),
        ssd_norm=row("ssd_norm"), w_ssd_out=full["w_ssd_out"][l],
        q_lora_norm=row("q_lora_norm"), w_uq=_head_perm(full["w_uq"][l], MLA_NOPE),
        kv_lora_norm=row("kv_lora_norm"), w_ukv=_head_perm(full["w_ukv"][l], MLA_NOPE),
        qk_gains=_mla_gains(full["q_norm"][l], full["k_norm"][l]),
        w_mla_out=full["w_mla_out"][l], w_o=full["w_o"][l])
    ffn1 = dict(ln=row("ln_ffn1"), w13=full["ffn1_w13"][l], w2=full["ffn1_w2"][l])
    ffn2 = dict(ln=row("ln_ffn2"), w13=full["ffn2_w13"][l], w2=full["ffn2_w2"][l])
    return ffn1, mixer, ffn2


def _layer_grads(g1, gm, g2):
    return {
        "ln_ffn1": g1["ln"][0], "ffn1_w13": g1["w13"], "ffn1_w2": g1["w2"],
        "ln_mix": gm["ln_mix"][0], "w_in": _in_unperm(gm["w_in"]), "conv_w": gm["conv_w"], "conv_b": gm["conv_b"][0],
        "dt_bias": gm["dt_bias"], "a_log": gm["a_log"], "d_skip": gm["d_skip"], "ssd_norm": gm["ssd_norm"][0],
        "w_ssd_out": gm["w_ssd_out"], "q_lora_norm": gm["q_lora_norm"][0], "w_uq": _head_unperm(gm["w_uq"], MLA_NOPE),
        "kv_lora_norm": gm["kv_lora_norm"][0], "w_ukv": _head_unperm(gm["w_ukv"], MLA_NOPE),
        "q_norm": gm["q_norm"], "k_norm": gm["k_norm"], "w_mla_out": gm["w_mla_out"], "w_o": gm["w_o"],
        "ln_ffn2": g2["ln"][0], "ffn2_w13": g2["w13"], "ffn2_w2": g2["w2"],
    }


def _local_step(x, positions, loss_target, full):
    rope = _rope_tables4(positions)
    lw = [_layer_weights(full, l) for l in range(DEPTH)]
    h = x
    saved = []
    for l in range(DEPTH):
        f1, mx, f2 = lw[l]
        h, s1 = _ffn_fwd(h, f1, f"l{l}_ffn1")
        h, sm = _mixer_fwd(h, mx, rope, f"l{l}_mix")
        h, s2 = _ffn_fwd(h, f2, f"l{l}_ffn2")
        saved.append((s1, sm, s2))
    loss_part, dh = _loss_fwd_bwd(h, loss_target, name="loss")
    grads = [None] * DEPTH
    for l in reversed(range(DEPTH)):
        f1, mx, f2 = lw[l]
        s1, sm, s2 = saved[l]
        dh, g2 = _ffn_bwd(dh, s2, f2, f"l{l}_ffn2")
        dh, gm = _mixer_bwd(dh, sm, mx, rope, f"l{l}_mix")
        dh, g1 = _ffn_bwd(dh, s1, f1, f"l{l}_ffn1")
        grads[l] = _layer_grads(g1, gm, g2)
    full_grads = {n: jnp.stack([grads[l][n] for l in range(DEPTH)]) for n in W_NAMES}
    return loss_part, dh, full_grads


MESH = pl.DeviceIdType.MESH
ANY = pl.BlockSpec(memory_space=pl.ANY)


def _place():
    return lax.axis_index("x"), lax.axis_index("y"), lax.axis_index("c")


def _other_chips(x, y):
    return [(1 - x, y), (x, 1 - y), (1 - x, 1 - y)]


def _remote(src, dst, send_sems, recv_sems, k, to):
    return pltpu.make_async_remote_copy(src_ref=src, dst_ref=dst, send_sem=send_sems.at[k], recv_sem=recv_sems.at[k],
                                        device_id=to, device_id_type=MESH)


N_PARTS = 8


def _parts(rows):
    size = rows // N_PARTS
    assert size * N_PARTS == rows and size % 16 == 0, rows
    return [(p * size, size) for p in range(N_PARTS)]


def _rows(ref, lead, base, start, size):
    return ref.at[(*lead, pl.ds(pl.multiple_of(base + start, 16), size), slice(None))]


def _my_chip():
    return 2 * lax.axis_index("x") + lax.axis_index("y")


def _own_slot(packed, *, name):
    r, ncol = packed.shape
    tr = _pick(r, 512, 16)

    def body(x_ref, o_ref):
        o_ref[...] = x_ref[...]

    return pl.pallas_call(
        body, name=name, grid=(r // tr,),
        in_specs=[pl.BlockSpec((tr, ncol), lambda i: (i, 0))],
        out_specs=pl.BlockSpec((None, tr, ncol), lambda i: (_my_chip(), i, 0)),
        out_shape=jax.ShapeDtypeStruct((N_CHIPS, r, ncol), packed.dtype),
        compiler_params=_cp("arbitrary"),
    )(packed)


def _gather_shards(packed, slots, *, name):
    r, ncol = packed.shape
    hr = r // 2
    parts = _parts(hr)

    def body(x_ref, slots_ref, out_ref, send_sems, recv_sems):
        del slots_ref
        x, y, c = _place()
        chips = _other_chips(x, y)
        me = 2 * x + y

        def half(chip, cc):
            return _rows(out_ref, (2 * chip[0] + chip[1],), cc * hr, 0, hr)

        for j, chip in enumerate(chips):
            for st, sz in parts:
                _remote(_rows(x_ref, (), c * hr, st, sz), _rows(out_ref, (me,), c * hr, st, sz), send_sems, recv_sems, j,
                        (*chip, c)).start()
        for j, chip in enumerate(chips):
            _remote(half(chip, c), half(chip, c), send_sems, recv_sems, j, (x, y, c)).wait_recv()
            slot = 2 * chip[0] + chip[1]
            for st, sz in parts:
                _remote(_rows(out_ref, (slot,), c * hr, st, sz), _rows(out_ref, (slot,), c * hr, st, sz), send_sems,
                        recv_sems, 3 + j, (x, y, 1 - c)).start()
        for j, chip in enumerate(chips):
            _remote(half(chip, 1 - c), half(chip, 1 - c), send_sems, recv_sems, 3 + j, (x, y, c)).wait_recv()
        for k in range(6):
            _remote(half((x, y), c), half((x, y), c), send_sems, recv_sems, k, (x, y, c)).wait_send()

    return pl.pallas_call(
        body, name=name,
        out_shape=jax.ShapeDtypeStruct((N_CHIPS, r, ncol), packed.dtype),
        in_specs=[ANY, ANY], out_specs=ANY, input_output_aliases={1: 0},
        scratch_shapes=[pltpu.SemaphoreType.DMA((6,)), pltpu.SemaphoreType.DMA((6,))],
    )(packed, slots)


def _swap_halves(g, *, name):
    n, r, ncol = g.shape
    hr = r // 2
    parts = _parts(hr)

    def body(g_ref, got_ref, send_sems, recv_sems):
        x, y, c = _place()
        for s in range(n):
            for st, sz in parts:
                _remote(_rows(g_ref, (s,), (1 - c) * hr, st, sz), got_ref.at[s, pl.ds(st, sz), :], send_sems, recv_sems, 0,
                        (x, y, 1 - c)).start()
        _remote(got_ref, got_ref, send_sems, recv_sems, 0, (x, y, c)).wait()

    return pl.pallas_call(
        body, name=name, out_shape=jax.ShapeDtypeStruct((n, hr, ncol), g.dtype), in_specs=[ANY], out_specs=ANY,
        scratch_shapes=[pltpu.SemaphoreType.DMA((1,)), pltpu.SemaphoreType.DMA((1,))],
    )(g)


def _add_cores(g, got, *, name):
    n, r, ncol = g.shape
    hr = r // 2
    tr = _pick(hr, 512, 16)
    nb = hr // tr

    def body(a_ref, b_ref, o_ref):
        o_ref[...] = (a_ref[...].astype(F32) + b_ref[...].astype(F32)).astype(o_ref.dtype)

    blk = pl.BlockSpec((None, tr, ncol), lambda s, i: (s, i, 0))
    return pl.pallas_call(
        body, name=name, grid=(n, nb),
        in_specs=[pl.BlockSpec((None, tr, ncol), lambda s, i: (s, lax.axis_index("c") * nb + i, 0)), blk],
        out_specs=blk,
        out_shape=jax.ShapeDtypeStruct((n, hr, ncol), BF16),
        compiler_params=_cp("parallel", "parallel"),
    )(g, got)


def _scatter_to_chips(a, *, name):
    n, r, ncol = a.shape
    parts = _parts(r)

    def body(a_ref, got_ref, send_sems, recv_sems):
        x, y, c = _place()
        for st, sz in parts:
            for j, chip in enumerate(_other_chips(x, y)):
                _remote(a_ref.at[2 * chip[0] + chip[1], pl.ds(st, sz), :], got_ref.at[j, pl.ds(st, sz), :], send_sems,
                        recv_sems, j, (*chip, c)).start()
        for j in range(n - 1):
            _remote(got_ref.at[j], got_ref.at[j], send_sems, recv_sems, j, (x, y, c)).wait()

    return pl.pallas_call(
        body, name=name, out_shape=jax.ShapeDtypeStruct((n - 1, r, ncol), a.dtype), in_specs=[ANY], out_specs=ANY,
        scratch_shapes=[pltpu.SemaphoreType.DMA((3,)), pltpu.SemaphoreType.DMA((3,))],
    )(a)


def _add_chips(a, got, *, name):
    n, hr, ncol = a.shape
    tr = _pick(hr, 512, 16)
    nb = hr // tr

    def body(a_ref, g0_ref, g1_ref, g2_ref, o_ref):
        f = lambda ref: ref[...].astype(F32)
        o_ref[...] = ((f(a_ref) + f(g0_ref)) + f(g1_ref)) + f(g2_ref)

    other = lambda j: pl.BlockSpec((None, tr, ncol), lambda i: (j, i, 0))
    return pl.pallas_call(
        body, name=name, grid=(nb,),
        in_specs=[pl.BlockSpec((None, tr, ncol), lambda i: (_my_chip(), i, 0)), other(0), other(1), other(2)],
        out_specs=pl.BlockSpec((tr, ncol), lambda i: (lax.axis_index("c") * nb + i, 0)),
        out_shape=jax.ShapeDtypeStruct((2 * hr, ncol), F32),
        compiler_params=_cp("parallel"),
    )(a, got, got, got)


def _join_halves(buf, *, name):
    r, ncol = buf.shape
    hr = r // 2
    parts = _parts(hr)

    def body(b_ref, out_ref, send_sems, recv_sems):
        del b_ref
        x, y, c = _place()
        for st, sz in parts:
            _remote(_rows(out_ref, (), c * hr, st, sz), _rows(out_ref, (), c * hr, st, sz), send_sems, recv_sems, 0,
                    (x, y, 1 - c)).start()
        theirs = _rows(out_ref, (), (1 - c) * hr, 0, hr)
        _remote(theirs, theirs, send_sems, recv_sems, 0, (x, y, c)).wait()

    return pl.pallas_call(
        body, name=name, out_shape=jax.ShapeDtypeStruct((r, ncol), buf.dtype), in_specs=[ANY], out_specs=ANY,
        input_output_aliases={0: 0},
        scratch_shapes=[pltpu.SemaphoreType.DMA((1,)), pltpu.SemaphoreType.DMA((1,))],
    )(buf)


def _reduce_scatter(g, *, name):
    got = _swap_halves(g, name=name + "_swap")
    chip_sum = _add_cores(g, got, name=name + "_add_cores")
    others = _scatter_to_chips(chip_sum, name=name + "_scatter")
    return _join_halves(_add_chips(chip_sum, others, name=name + "_add_chips"), name=name + "_join")


def _all_gather_small(v, *, name):
    r, ncol = v.shape

    def body(x_ref, out_ref, send_sems, recv_sems, local_sem):
        x, y, c = _place()
        me, sibling = (x, y, c), (x, y, 1 - c)
        chips = _other_chips(x, y)

        def slot(p):
            return out_ref.at[4 * p[0] + 2 * p[1] + p[2]]

        mine = pltpu.make_async_copy(x_ref, slot(me), local_sem.at[0])
        mine.start()
        first = [_remote(x_ref, slot(me), send_sems, recv_sems, 0, sibling)]
        first += [_remote(x_ref, slot(me), send_sems, recv_sems, 1 + j, (*chip, c)) for j, chip in enumerate(chips)]
        for cp in first:
            cp.start()
        passed = [_remote(slot((*chip, c)), slot((*chip, c)), send_sems, recv_sems, 4 + j, sibling)
                  for j, chip in enumerate(chips)]
        for j, chip in enumerate(chips):
            _remote(slot((*chip, c)), slot((*chip, c)), send_sems, recv_sems, 1 + j, me).wait_recv()
            passed[j].start()
        _remote(slot(sibling), slot(sibling), send_sems, recv_sems, 0, me).wait_recv()
        for j, chip in enumerate(chips):
            _remote(slot((*chip, 1 - c)), slot((*chip, 1 - c)), send_sems, recv_sems, 4 + j, me).wait_recv()
        for cp in first + passed:
            cp.wait_send()
        mine.wait()

    vm = pl.BlockSpec(memory_space=pltpu.VMEM)
    return pl.pallas_call(
        body, name=name, out_shape=jax.ShapeDtypeStruct((N_DEV, r, ncol), v.dtype), in_specs=[vm], out_specs=vm,
        scratch_shapes=[pltpu.SemaphoreType.DMA((7,)), pltpu.SemaphoreType.DMA((7,)), pltpu.SemaphoreType.DMA((1,))],
    )(v)


def _sum_slots(g8, *, name):
    n, r, ncol = g8.shape

    def body(g_ref, o_ref):
        acc = g_ref[0]
        for k in range(1, n):
            acc = acc + g_ref[k]
        o_ref[...] = acc

    return pl.pallas_call(body, name=name, out_shape=jax.ShapeDtypeStruct((r, ncol), g8.dtype))(g8)


def _step(a):
    x = a["x"][0]
    s = x.shape[0]
    del s
    shard_shapes = [a[n].shape for n in SHARDED]
    rows = _pack_rows(shard_shapes)

    packed = _pack([a[n] for n in SHARDED], rows, BF16, name="pack_weights")
    gathered = _gather_shards(packed, _own_slot(packed, name="own_weights"), name="gather_weights")
    conv_rows = -(-math.prod(a["conv_w"].shape) // (LANE * 8)) * 8
    conv_all = _all_gather_small(
        jnp.pad(a["conv_w"].reshape(-1), (0, conv_rows * LANE - math.prod(a["conv_w"].shape))).reshape(conv_rows, LANE),
        name="gather_conv_w")
    per_chip = [dict(zip(SHARDED, _unpack(gathered[k], shard_shapes))) for k in range(N_CHIPS)]
    full = {n: jnp.concatenate([per_chip[k][n] for k in range(N_CHIPS)], axis=SHARD_AXIS[n]) for n in SHARDED}
    full["conv_w"] = jnp.concatenate(
        [conv_all[2 * k].reshape(-1)[:math.prod(a["conv_w"].shape)].reshape(a["conv_w"].shape) for k in range(N_CHIPS)],
        axis=SHARD_AXIS["conv_w"])
    for n in REPLICATED:
        full[n] = a[n]

    loss_part, grad_x, grads = _local_step(x, a["positions"][0], a["loss_target"][0], full)
    loss = lax.psum(jnp.sum(loss_part), ("x", "y", "c"))

    mats = []
    for k in range(N_CHIPS):
        parts = [jnp.split(grads[n], N_CHIPS, axis=SHARD_AXIS[n])[k] for n in SHARDED]
        mats += _pack_mats(parts, rows, BF16)
    g_slots = _stack_rows(mats, name="pack_grads").reshape(N_CHIPS, rows, PACK_COLS)
    g_shard = _reduce_scatter(g_slots, name="reduce_grads")

    rep_shapes = [a[n].shape for n in REPLICATED]
    n_rep = sum(math.prod(sh) for sh in rep_shapes)
    rep_rows = -(-n_rep // (LANE * 8)) * 8
    pack_small = lambda arrs: jnp.pad(jnp.concatenate([t.reshape(-1) for t in arrs]), (0, rep_rows * LANE - n_rep)).reshape(rep_rows, LANE)
    g_rep = _sum_slots(_all_gather_small(pack_small([grads[n] for n in REPLICATED]), name="gather_small_grads"),
                       name="add_small_grads")

    out = {"loss": loss, "grad_x": grad_x[None]}
    for n, g in zip(SHARDED, _unpack(g_shard, shard_shapes)):
        flat = lambda t: t.reshape(-1, t.shape[-1])
        d, nm, nv = _adamw(flat(a[n]), flat(g), flat(a["m_" + n]), flat(a["v_" + n]), name="adamw_" + n)
        out["grad_" + n] = g
        out["delta_" + n], out["new_m_" + n], out["new_v_" + n] = (t.reshape(g.shape) for t in (d, nm, nv))
    d_rp, m_rp, v_rp = _adamw(pack_small([a[n] for n in REPLICATED]), g_rep,
                              pack_small([a["m_" + n] for n in REPLICATED]),
                              pack_small([a["v_" + n] for n in REPLICATED]), name="adamw_replicated")
    for prefix, rp_arr in (("grad_", g_rep), ("delta_", d_rp), ("new_m_", m_rp), ("new_v_", v_rp)):
        for n, t in zip(REPLICATED, _unpack_flat(rp_arr.reshape(-1)[:n_rep], rep_shapes)):
            out[prefix + n] = t
    return out


IN_NAMES = ["x", "positions"] + W_NAMES + ["loss_target"] + ["m_" + n for n in W_NAMES] + ["v_" + n for n in W_NAMES]
OUT_NAMES = (["loss", "grad_x"] + ["grad_" + n for n in W_NAMES] + ["delta_" + n for n in W_NAMES]
             + ["new_m_" + n for n in W_NAMES] + ["new_v_" + n for n in W_NAMES])


def kernel(x, positions, ln_ffn1, ffn1_w13, ffn1_w2, ln_mix, w_in, conv_w, conv_b, dt_bias, a_log, d_skip, ssd_norm, w_ssd_out, q_lora_norm, w_uq, kv_lora_norm, w_ukv, q_norm, k_norm, w_mla_out, w_o, ln_ffn2, ffn2_w13, ffn2_w2, loss_target, m_ln_ffn1, m_ffn1_w13, m_ffn1_w2, m_ln_mix, m_w_in, m_conv_w, m_conv_b, m_dt_bias, m_a_log, m_d_skip, m_ssd_norm, m_w_ssd_out, m_q_lora_norm, m_w_uq, m_kv_lora_norm, m_w_ukv, m_q_norm, m_k_norm, m_w_mla_out, m_w_o, m_ln_ffn2, m_ffn2_w13, m_ffn2_w2, v_ln_ffn1, v_ffn1_w13, v_ffn1_w2, v_ln_mix, v_w_in, v_conv_w, v_conv_b, v_dt_bias, v_a_log, v_d_skip, v_ssd_norm, v_w_ssd_out, v_q_lora_norm, v_w_uq, v_kv_lora_norm, v_w_ukv, v_q_norm, v_k_norm, v_w_mla_out, v_w_o, v_ln_ffn2, v_ffn2_w13, v_ffn2_w2):
    given = locals()
    out = _step({n: given[n] for n in IN_NAMES})
    return tuple(out[n] for n in OUT_NAMES)
```
